```python
import jax, jax.numpy as jnp
from jax import lax
import numpy as np

D_MODEL = 1024
BATCH = 8
SEQ = 16384
DEPTH = 2

D_MIX = D_MODEL
W_A = D_MIX // 4
W_B = D_MIX // 4
W_C = D_MIX // 4
W_D = D_MIX // 4
SHORT_CONV = 3
POOL_WINDOWS = (2, 4, 8, 16)
N_POOL_GROUPS = len(POOL_WINDOWS)
POOL_GROUP_DIM = W_B // N_POOL_GROUPS
CONF_WIDTH = 31
SGU_CHUNK = 128
N_HEADS_D = 4
SGU_HEAD_DIM = W_D // N_HEADS_D
IN_SIZES = (W_A, W_A, W_A, W_A,
            W_B, W_B,
            W_C, W_C, W_C,
            W_D, W_D, W_D)
D_IN = sum(IN_SIZES)
EPS = 1e-6

kernel_name = "hybrid_parallel_conv_pool_conformer_sgu"


def rms_norm(x, g):
    xf = x.astype(jnp.float32)
    y = xf * lax.rsqrt(jnp.mean(xf * xf, axis=-1, keepdims=True) + EPS)
    return (y * g.astype(jnp.float32)).astype(x.dtype)


def layer_norm(x, g, b):
    xf = x.astype(jnp.float32)
    mu = jnp.mean(xf, axis=-1, keepdims=True)
    var = jnp.mean(jnp.square(xf - mu), axis=-1, keepdims=True)
    y = (xf - mu) * lax.rsqrt(var + EPS)
    return (y * g.astype(jnp.float32) + b.astype(jnp.float32)).astype(x.dtype)


def causal_dwconv(x, w):
    k = w.shape[0]
    xp = jnp.pad(x, ((0, 0), (k - 1, 0), (0, 0)))
    return lax.conv_general_dilated(
        xp, w[:, None, :].astype(x.dtype), window_strides=(1,), padding='VALID',
        dimension_numbers=('NWC', 'WIO', 'NWC'), feature_group_count=x.shape[-1])


def short_conv_mixer(b_gate, c_gate, xt, w_conv):
    return b_gate * causal_dwconv(c_gate * xt, w_conv)


def multiscale_pool_mixer(p, w_pool, pool_scale):
    bsz, t_len, _ = p.shape
    pf = p.astype(jnp.float32)
    t_idx = jnp.arange(t_len)
    outs = []
    for grp, w in zip(jnp.split(pf, N_POOL_GROUPS, axis=-1), POOL_WINDOWS):
        cs = jnp.pad(jnp.cumsum(grp, axis=1), ((0, 0), (w, 0), (0, 0)))
        wsum = cs[:, w:] - cs[:, :t_len]
        count = jnp.minimum(t_idx + 1, w).astype(jnp.float32)[None, :, None]
        outs.append(wsum / count - grp)
    pooled = jnp.stack(outs, axis=2).astype(p.dtype)
    y = jnp.einsum('btgc,gcd->btgd', pooled, w_pool).reshape(bsz, t_len, W_B)
    return y * pool_scale


def conformer_conv_mixer(a, gl, w_dw, b_dw, ln_g, ln_b, w_pw2, b_pw2):
    h = a * jax.nn.sigmoid(gl)
    h = causal_dwconv(h, w_dw) + b_dw
    h = jax.nn.silu(layer_norm(h, ln_g, ln_b))
    return h @ w_pw2 + b_pw2


def chunked_sgu_mixer(u, v, ln_g, ln_b, w_s, b_s):
    u = jax.nn.gelu(u)
    v = layer_norm(jax.nn.gelu(v), ln_g, ln_b)
    bsz, t_len, _ = v.shape
    n_chunks = t_len // SGU_CHUNK
    vc = v.reshape(bsz, n_chunks, SGU_CHUNK, N_HEADS_D, SGU_HEAD_DIM)
    mask = jnp.tril(jnp.ones((SGU_CHUNK, SGU_CHUNK), dtype=bool))
    ws = jnp.where(mask[None], w_s, 0)
    mixed = jnp.einsum('hst,bnthc->bnshc', ws, vc) + b_s.T[None, None, :, :, None]
    return u * mixed.reshape(bsz, t_len, W_D)


def _fwd_setup_inputs(seed: int = 0) -> dict:
    key = jax.random.key(seed)
    ks = jax.random.split(key, 24)
    f32 = jnp.float32
    nrm = lambda k, shape, s: jax.random.normal(k, shape, f32) * s
    return {
        "x": nrm(ks[0], (BATCH, SEQ, D_MODEL), 1.0),
        "c": nrm(ks[1], (BATCH, D_MODEL), 1.0),
        "norm_g": 1.0 + nrm(ks[2], (DEPTH, D_MODEL), 0.02),
        "w_ada": nrm(ks[3], (DEPTH, D_MODEL, 3 * D_MODEL), 0.5 * D_MODEL ** -0.5),
        "b_ada": nrm(ks[4], (DEPTH, 3 * D_MODEL), 0.01),
        "w_in": nrm(ks[5], (DEPTH, D_MODEL, D_IN), D_MODEL ** -0.5),
        "w_conv_a": nrm(ks[6], (DEPTH, SHORT_CONV, W_A), SHORT_CONV ** -0.5),
        "w_pool": nrm(ks[7], (DEPTH, N_POOL_GROUPS, POOL_GROUP_DIM, POOL_GROUP_DIM), POOL_GROUP_DIM ** -0.5),
        "pool_scale": 1.0 + nrm(ks[8], (DEPTH, W_B), 0.02),
        "w_dw_c": nrm(ks[9], (DEPTH, CONF_WIDTH, W_C), CONF_WIDTH ** -0.5),
        "b_dw_c": nrm(ks[10], (DEPTH, W_C), 0.01),
        "ln_g_c": 1.0 + nrm(ks[11], (DEPTH, W_C), 0.02),
        "ln_b_c": nrm(ks[12], (DEPTH, W_C), 0.01),
        "w_pw2_c": nrm(ks[13], (DEPTH, W_C, W_C), W_C ** -0.5),
        "b_pw2_c": nrm(ks[14], (DEPTH, W_C), 0.01),
        "ln_g_d": 1.0 + nrm(ks[15], (DEPTH, W_D), 0.02),
        "ln_b_d": nrm(ks[16], (DEPTH, W_D), 0.01),
        "w_s_d": nrm(ks[17], (DEPTH, N_HEADS_D, SGU_CHUNK, SGU_CHUNK), 0.5 * SGU_CHUNK ** -0.5),
        "b_s_d": 1.0 + nrm(ks[18], (DEPTH, N_HEADS_D, SGU_CHUNK), 0.02),
        "w_out": nrm(ks[19], (DEPTH, D_MIX, D_MODEL), D_MIX ** -0.5),
        "final_g": 1.0 + nrm(ks[20], (D_MODEL,), 0.02),
    }


def _fwd_reference(x, c, norm_g, w_ada, b_ada, w_in, w_conv_a, w_pool, pool_scale,
              w_dw_c, b_dw_c, ln_g_c, ln_b_c, w_pw2_c, b_pw2_c,
              ln_g_d, ln_b_d, w_s_d, b_s_d, w_out, final_g):
    split_points = list(np.cumsum(IN_SIZES)[:-1])
    c_act = jax.nn.silu(c)
    for l in range(DEPTH):
        mod = c_act @ w_ada[l] + b_ada[l]
        shift, scale, gate = jnp.split(mod, 3, axis=-1)
        h = rms_norm(x, norm_g[l]) * (1.0 + scale[:, None, :]) + shift[:, None, :]
        z = h @ w_in[l]
        (a_b, a_c, a_x, a_g, b_p, b_g, c_a, c_gl, c_g,
         d_u, d_v, d_g) = jnp.split(z, split_points, axis=-1)
        y_a = short_conv_mixer(a_b, a_c, a_x, w_conv_a[l]) * jax.nn.silu(a_g)
        y_b = multiscale_pool_mixer(b_p, w_pool[l], pool_scale[l]) * jax.nn.silu(b_g)
        y_c = conformer_conv_mixer(c_a, c_gl, w_dw_c[l], b_dw_c[l], ln_g_c[l], ln_b_c[l],
                                   w_pw2_c[l], b_pw2_c[l]) * jax.nn.silu(c_g)
        y_d = chunked_sgu_mixer(d_u, d_v, ln_g_d[l], ln_b_d[l], w_s_d[l], b_s_d[l]) * jax.nn.silu(d_g)
        y = jnp.concatenate([y_a, y_b, y_c, y_d], axis=-1) @ w_out[l]
        x = x + gate[:, None, :] * y
    return rms_norm(x, final_g)


import jax as _jax
import jax.numpy as _jnp

TWIN_FORMAT = 'train_step'
FWD_PARAMS = ['x', 'c', 'norm_g', 'w_ada', 'b_ada', 'w_in', 'w_conv_a', 'w_pool', 'pool_scale', 'w_dw_c', 'b_dw_c', 'ln_g_c', 'ln_b_c', 'w_pw2_c', 'b_pw2_c', 'ln_g_d', 'ln_b_d', 'w_s_d', 'b_s_d', 'w_out', 'final_g']
TWIN_WEIGHTS = ['norm_g', 'w_ada', 'b_ada', 'w_in', 'w_conv_a', 'w_pool', 'pool_scale', 'w_dw_c', 'b_dw_c', 'ln_g_c', 'ln_b_c', 'w_pw2_c', 'b_pw2_c', 'ln_g_d', 'ln_b_d', 'w_s_d', 'b_s_d', 'w_out', 'final_g']
TWIN_DIFF_INPUT = 'x'
TWIN_INPUTS = ['x', 'c', 'norm_g', 'w_ada', 'b_ada', 'w_in', 'w_conv_a', 'w_pool', 'pool_scale', 'w_dw_c', 'b_dw_c', 'ln_g_c', 'ln_b_c', 'w_pw2_c', 'b_pw2_c', 'ln_g_d', 'ln_b_d', 'w_s_d', 'b_s_d', 'w_out', 'final_g', 'loss_target', 'm_norm_g', 'm_w_ada', 'm_b_ada', 'm_w_in', 'm_w_conv_a', 'm_w_pool', 'm_pool_scale', 'm_w_dw_c', 'm_b_dw_c', 'm_ln_g_c', 'm_ln_b_c', 'm_w_pw2_c', 'm_b_pw2_c', 'm_ln_g_d', 'm_ln_b_d', 'm_w_s_d', 'm_b_s_d', 'm_w_out', 'm_final_g', 'v_norm_g', 'v_w_ada', 'v_b_ada', 'v_w_in', 'v_w_conv_a', 'v_w_pool', 'v_pool_scale', 'v_w_dw_c', 'v_b_dw_c', 'v_ln_g_c', 'v_ln_b_c', 'v_w_pw2_c', 'v_b_pw2_c', 'v_ln_g_d', 'v_ln_b_d', 'v_w_s_d', 'v_b_s_d', 'v_w_out', 'v_final_g']
TWIN_OUTPUTS = ['loss', 'grad_x', 'grad_norm_g', 'grad_w_ada', 'grad_b_ada', 'grad_w_in', 'grad_w_conv_a', 'grad_w_pool', 'grad_pool_scale', 'grad_w_dw_c', 'grad_b_dw_c', 'grad_ln_g_c', 'grad_ln_b_c', 'grad_w_pw2_c', 'grad_b_pw2_c', 'grad_ln_g_d', 'grad_ln_b_d', 'grad_w_s_d', 'grad_b_s_d', 'grad_w_out', 'grad_final_g', 'delta_norm_g', 'delta_w_ada', 'delta_b_ada', 'delta_w_in', 'delta_w_conv_a', 'delta_w_pool', 'delta_pool_scale', 'delta_w_dw_c', 'delta_b_dw_c', 'delta_ln_g_c', 'delta_ln_b_c', 'delta_w_pw2_c', 'delta_b_pw2_c', 'delta_ln_g_d', 'delta_ln_b_d', 'delta_w_s_d', 'delta_b_s_d', 'delta_w_out', 'delta_final_g', 'new_m_norm_g', 'new_m_w_ada', 'new_m_b_ada', 'new_m_w_in', 'new_m_w_conv_a', 'new_m_w_pool', 'new_m_pool_scale', 'new_m_w_dw_c', 'new_m_b_dw_c', 'new_m_ln_g_c', 'new_m_ln_b_c', 'new_m_w_pw2_c', 'new_m_b_pw2_c', 'new_m_ln_g_d', 'new_m_ln_b_d', 'new_m_w_s_d', 'new_m_b_s_d', 'new_m_w_out', 'new_m_final_g', 'new_v_norm_g', 'new_v_w_ada', 'new_v_b_ada', 'new_v_w_in', 'new_v_w_conv_a', 'new_v_w_pool', 'new_v_pool_scale', 'new_v_w_dw_c', 'new_v_b_dw_c', 'new_v_ln_g_c', 'new_v_ln_b_c', 'new_v_w_pw2_c', 'new_v_b_pw2_c', 'new_v_ln_g_d', 'new_v_ln_b_d', 'new_v_w_s_d', 'new_v_b_s_d', 'new_v_w_out', 'new_v_final_g']
TWIN_LEAF_KINDS = {'loss': 'loss', 'grad_x': 'grad_x', 'grad_norm_g': 'grad_w', 'grad_w_ada': 'grad_w', 'grad_b_ada': 'grad_w', 'grad_w_in': 'grad_w', 'grad_w_conv_a': 'grad_w', 'grad_w_pool': 'grad_w', 'grad_pool_scale': 'grad_w', 'grad_w_dw_c': 'grad_w', 'grad_b_dw_c': 'grad_w', 'grad_ln_g_c': 'grad_w', 'grad_ln_b_c': 'grad_w', 'grad_w_pw2_c': 'grad_w', 'grad_b_pw2_c': 'grad_w', 'grad_ln_g_d': 'grad_w', 'grad_ln_b_d': 'grad_w', 'grad_w_s_d': 'grad_w', 'grad_b_s_d': 'grad_w', 'grad_w_out': 'grad_w', 'grad_final_g': 'grad_w', 'delta_norm_g': 'delta_w', 'delta_w_ada': 'delta_w', 'delta_b_ada': 'delta_w', 'delta_w_in': 'delta_w', 'delta_w_conv_a': 'delta_w', 'delta_w_pool': 'delta_w', 'delta_pool_scale': 'delta_w', 'delta_w_dw_c': 'delta_w', 'delta_b_dw_c': 'delta_w', 'delta_ln_g_c': 'delta_w', 'delta_ln_b_c': 'delta_w', 'delta_w_pw2_c': 'delta_w', 'delta_b_pw2_c': 'delta_w', 'delta_ln_g_d': 'delta_w', 'delta_ln_b_d': 'delta_w', 'delta_w_s_d': 'delta_w', 'delta_b_s_d': 'delta_w', 'delta_w_out': 'delta_w', 'delta_final_g': 'delta_w', 'new_m_norm_g': 'new_m', 'new_m_w_ada': 'new_m', 'new_m_b_ada': 'new_m', 'new_m_w_in': 'new_m', 'new_m_w_conv_a': 'new_m', 'new_m_w_pool': 'new_m', 'new_m_pool_scale': 'new_m', 'new_m_w_dw_c': 'new_m', 'new_m_b_dw_c': 'new_m', 'new_m_ln_g_c': 'new_m', 'new_m_ln_b_c': 'new_m', 'new_m_w_pw2_c': 'new_m', 'new_m_b_pw2_c': 'new_m', 'new_m_ln_g_d': 'new_m', 'new_m_ln_b_d': 'new_m', 'new_m_w_s_d': 'new_m', 'new_m_b_s_d': 'new_m', 'new_m_w_out': 'new_m', 'new_m_final_g': 'new_m', 'new_v_norm_g': 'new_v', 'new_v_w_ada': 'new_v', 'new_v_b_ada': 'new_v', 'new_v_w_in': 'new_v', 'new_v_w_conv_a': 'new_v', 'new_v_w_pool': 'new_v', 'new_v_pool_scale': 'new_v', 'new_v_w_dw_c': 'new_v', 'new_v_b_dw_c': 'new_v', 'new_v_ln_g_c': 'new_v', 'new_v_ln_b_c': 'new_v', 'new_v_w_pw2_c': 'new_v', 'new_v_b_pw2_c': 'new_v', 'new_v_ln_g_d': 'new_v', 'new_v_ln_b_d': 'new_v', 'new_v_w_s_d': 'new_v', 'new_v_b_s_d': 'new_v', 'new_v_w_out': 'new_v', 'new_v_final_g': 'new_v'}


def _forward(args):
    return _fwd_reference(*[args[k] for k in FWD_PARAMS])


def _output_shape():
    def fwd():
        inp = _fwd_setup_inputs(0)
        return _fwd_reference(*[inp[k] for k in FWD_PARAMS])
    out = _jax.eval_shape(fwd)
    return out.shape, out.dtype

N_MICROBATCH = 1
ADAM_LR = 0.001
ADAM_B1 = 0.9
ADAM_B2 = 0.999
ADAM_EPS = 1e-08
ADAM_WD = 0.01
ADAM_STEP = 10
PER_EXAMPLE_BATCH_AXIS = {'x': 0, 'c': 0, 'loss_target': 0}
SHARED_INPUTS = []
_WEIGHT_DTYPES = {'norm_g': _jnp.float32, 'w_ada': _jnp.float32, 'b_ada': _jnp.float32, 'w_in': _jnp.float32, 'w_conv_a': _jnp.float32, 'w_pool': _jnp.float32, 'pool_scale': _jnp.float32, 'w_dw_c': _jnp.float32, 'b_dw_c': _jnp.float32, 'ln_g_c': _jnp.float32, 'ln_b_c': _jnp.float32, 'w_pw2_c': _jnp.float32, 'b_pw2_c': _jnp.float32, 'ln_g_d': _jnp.float32, 'ln_b_d': _jnp.float32, 'w_s_d': _jnp.float32, 'b_s_d': _jnp.float32, 'w_out': _jnp.float32, 'final_g': _jnp.float32}
MOMENT_SCALE = {'norm_g': 1.154620e-01, 'w_ada': 9.596598e-02, 'b_ada': 1.667386e-01, 'w_in': 6.678487e-02, 'w_conv_a': 9.069626e-02, 'w_pool': 6.575191e-02, 'pool_scale': 7.175192e-02, 'w_dw_c': 4.474803e-02, 'b_dw_c': 8.406754e-02, 'ln_g_c': 5.239712e-02, 'ln_b_c': 4.760107e-02, 'w_pw2_c': 4.316269e-02, 'b_pw2_c': 7.463880e-02, 'ln_g_d': 1.812462e-02, 'ln_b_d': 2.009148e-02, 'w_s_d': 2.660725e-02, 'b_s_d': 3.859811e-02, 'w_out': 6.557005e-02, 'final_g': 1.280552e+02}


def _to_microbatches(a, axis):
    t = _jnp.moveaxis(a, axis, 0)
    t = t.reshape((N_MICROBATCH, t.shape[0] // N_MICROBATCH) + t.shape[1:])
    return _jnp.moveaxis(t, 1, axis + 1)


def setup_inputs(seed: int = 0) -> dict:
    inp = _fwd_setup_inputs(seed)
    key = _jax.random.fold_in(_jax.random.key(seed), 7919)
    shape, _ = _output_shape()
    out = dict(inp)
    out["loss_target"] = _jax.random.normal(_jax.random.fold_in(key, 0), shape, _jnp.float32)
    for i, name in enumerate(TWIN_WEIGHTS):
        w = inp[name].astype(_jnp.float32)
        if MOMENT_SCALE is None:
            s = _jnp.sqrt(_jnp.mean(_jnp.square(w)) + 1e-30)
        else:
            s = MOMENT_SCALE[name]
        km, kv = _jax.random.split(_jax.random.fold_in(key, i + 1))
        out[name] = w
        out["m_" + name] = s * _jax.random.normal(km, w.shape, _jnp.float32)
        out["v_" + name] = (s * s) * _jax.random.uniform(kv, w.shape, _jnp.float32, 0.5, 1.5)
    if N_MICROBATCH > 1:
        for name, axis in PER_EXAMPLE_BATCH_AXIS.items():
            out[name] = _to_microbatches(out[name], axis)
    return {'x': out['x'], 'c': out['c'], 'norm_g': out['norm_g'], 'w_ada': out['w_ada'], 'b_ada': out['b_ada'], 'w_in': out['w_in'], 'w_conv_a': out['w_conv_a'], 'w_pool': out['w_pool'], 'pool_scale': out['pool_scale'], 'w_dw_c': out['w_dw_c'], 'b_dw_c': out['b_dw_c'], 'ln_g_c': out['ln_g_c'], 'ln_b_c': out['ln_b_c'], 'w_pw2_c': out['w_pw2_c'], 'b_pw2_c': out['b_pw2_c'], 'ln_g_d': out['ln_g_d'], 'ln_b_d': out['ln_b_d'], 'w_s_d': out['w_s_d'], 'b_s_d': out['b_s_d'], 'w_out': out['w_out'], 'final_g': out['final_g'], 'loss_target': out['loss_target'], 'm_norm_g': out['m_norm_g'], 'm_w_ada': out['m_w_ada'], 'm_b_ada': out['m_b_ada'], 'm_w_in': out['m_w_in'], 'm_w_conv_a': out['m_w_conv_a'], 'm_w_pool': out['m_w_pool'], 'm_pool_scale': out['m_pool_scale'], 'm_w_dw_c': out['m_w_dw_c'], 'm_b_dw_c': out['m_b_dw_c'], 'm_ln_g_c': out['m_ln_g_c'], 'm_ln_b_c': out['m_ln_b_c'], 'm_w_pw2_c': out['m_w_pw2_c'], 'm_b_pw2_c': out['m_b_pw2_c'], 'm_ln_g_d': out['m_ln_g_d'], 'm_ln_b_d': out['m_ln_b_d'], 'm_w_s_d': out['m_w_s_d'], 'm_b_s_d': out['m_b_s_d'], 'm_w_out': out['m_w_out'], 'm_final_g': out['m_final_g'], 'v_norm_g': out['v_norm_g'], 'v_w_ada': out['v_w_ada'], 'v_b_ada': out['v_b_ada'], 'v_w_in': out['v_w_in'], 'v_w_conv_a': out['v_w_conv_a'], 'v_w_pool': out['v_w_pool'], 'v_pool_scale': out['v_pool_scale'], 'v_w_dw_c': out['v_w_dw_c'], 'v_b_dw_c': out['v_b_dw_c'], 'v_ln_g_c': out['v_ln_g_c'], 'v_ln_b_c': out['v_ln_b_c'], 'v_w_pw2_c': out['v_w_pw2_c'], 'v_b_pw2_c': out['v_b_pw2_c'], 'v_ln_g_d': out['v_ln_g_d'], 'v_ln_b_d': out['v_ln_b_d'], 'v_w_s_d': out['v_w_s_d'], 'v_b_s_d': out['v_b_s_d'], 'v_w_out': out['v_w_out'], 'v_final_g': out['v_final_g']}


def _loss(weights, diff, rest, loss_target):
    with _jax.named_scope("forward"):
        args = {**rest, TWIN_DIFF_INPUT: diff, **{k: w.astype(_WEIGHT_DTYPES[k]) for k, w in weights.items()}}
        y = _forward(args)
    with _jax.named_scope("loss_head"):
        err = _jnp.square(y.astype(_jnp.float32) - loss_target)
        return 0.5 * _jnp.sum(_jnp.mean(err, axis=-1)) if err.ndim else 0.5 * err


def _adamw(w, g, m, v):
    m = ADAM_B1 * m + (1.0 - ADAM_B1) * g
    v = ADAM_B2 * v + (1.0 - ADAM_B2) * _jnp.square(g)
    m_hat = m / (1.0 - ADAM_B1 ** ADAM_STEP)
    v_hat = v / (1.0 - ADAM_B2 ** ADAM_STEP)
    delta = -ADAM_LR * (m_hat / (_jnp.sqrt(v_hat) + ADAM_EPS) + ADAM_WD * w)
    return delta, m, v


def reference(x, c, norm_g, w_ada, b_ada, w_in, w_conv_a, w_pool, pool_scale, w_dw_c, b_dw_c, ln_g_c, ln_b_c, w_pw2_c, b_pw2_c, ln_g_d, ln_b_d, w_s_d, b_s_d, w_out, final_g, loss_target, m_norm_g, m_w_ada, m_b_ada, m_w_in, m_w_conv_a, m_w_pool, m_pool_scale, m_w_dw_c, m_b_dw_c, m_ln_g_c, m_ln_b_c, m_w_pw2_c, m_b_pw2_c, m_ln_g_d, m_ln_b_d, m_w_s_d, m_b_s_d, m_w_out, m_final_g, v_norm_g, v_w_ada, v_b_ada, v_w_in, v_w_conv_a, v_w_pool, v_pool_scale, v_w_dw_c, v_b_dw_c, v_ln_g_c, v_ln_b_c, v_w_pw2_c, v_b_pw2_c, v_ln_g_d, v_ln_b_d, v_w_s_d, v_b_s_d, v_w_out, v_final_g):
    given = dict(x=x, c=c, norm_g=norm_g, w_ada=w_ada, b_ada=b_ada, w_in=w_in, w_conv_a=w_conv_a, w_pool=w_pool, pool_scale=pool_scale, w_dw_c=w_dw_c, b_dw_c=b_dw_c, ln_g_c=ln_g_c, ln_b_c=ln_b_c, w_pw2_c=w_pw2_c, b_pw2_c=b_pw2_c, ln_g_d=ln_g_d, ln_b_d=ln_b_d, w_s_d=w_s_d, b_s_d=b_s_d, w_out=w_out, final_g=final_g, loss_target=loss_target, m_norm_g=m_norm_g, m_w_ada=m_w_ada, m_b_ada=m_b_ada, m_w_in=m_w_in, m_w_conv_a=m_w_conv_a, m_w_pool=m_w_pool, m_pool_scale=m_pool_scale, m_w_dw_c=m_w_dw_c, m_b_dw_c=m_b_dw_c, m_ln_g_c=m_ln_g_c, m_ln_b_c=m_ln_b_c, m_w_pw2_c=m_w_pw2_c, m_b_pw2_c=m_b_pw2_c, m_ln_g_d=m_ln_g_d, m_ln_b_d=m_ln_b_d, m_w_s_d=m_w_s_d, m_b_s_d=m_b_s_d, m_w_out=m_w_out, m_final_g=m_final_g, v_norm_g=v_norm_g, v_w_ada=v_w_ada, v_b_ada=v_b_ada, v_w_in=v_w_in, v_w_conv_a=v_w_conv_a, v_w_pool=v_w_pool, v_pool_scale=v_pool_scale, v_w_dw_c=v_w_dw_c, v_b_dw_c=v_b_dw_c, v_ln_g_c=v_ln_g_c, v_ln_b_c=v_ln_b_c, v_w_pw2_c=v_w_pw2_c, v_b_pw2_c=v_b_pw2_c, v_ln_g_d=v_ln_g_d, v_ln_b_d=v_ln_b_d, v_w_s_d=v_w_s_d, v_b_s_d=v_b_s_d, v_w_out=v_w_out, v_final_g=v_final_g)
    weights = {n: given[n] for n in TWIN_WEIGHTS}
    shared = {n: given[n] for n in SHARED_INPUTS}
    per_example = {n: given[n] for n in ['x', 'c']}
    grad_fn = _jax.value_and_grad(_loss, argnums=(0, 1))

    def one_microbatch(ex, loss_target):
        ex = dict(ex)
        diff = ex.pop(TWIN_DIFF_INPUT)
        return grad_fn(weights, diff, {**shared, **ex}, loss_target)

    if N_MICROBATCH == 1:
        loss, (grad_w, grad_x) = one_microbatch(per_example, given["loss_target"])
    else:
        def body(carry, xs):
            loss_sum, grad_sum = carry
            l_k, (gw_k, gx_k) = one_microbatch(xs[0], xs[1])
            with _jax.named_scope("update"):
                return (loss_sum + l_k, _jax.tree.map(_jnp.add, grad_sum, gw_k)), gx_k

        init = (_jnp.zeros((), _jnp.float32), _jax.tree.map(_jnp.zeros_like, weights))
        (loss, grad_w), grad_x = _jax.lax.scan(body, init, (per_example, given["loss_target"]))
    with _jax.named_scope("update"):
        delta_w, new_m, new_v = {}, {}, {}
        for n in TWIN_WEIGHTS:
            delta_w[n], new_m[n], new_v[n] = _adamw(weights[n], grad_w[n], given["m_" + n], given["v_" + n])
    return (loss, grad_x, *[grad_w[n] for n in TWIN_WEIGHTS], *[delta_w[n] for n in TWIN_WEIGHTS],
            *[new_m[n] for n in TWIN_WEIGHTS], *[new_v[n] for n in TWIN_WEIGHTS])
```

```python
import functools
import math

import jax
import jax.numpy as jnp
from jax import lax
from jax.experimental import pallas as pl
from jax.experimental.pallas import tpu as pltpu

F32 = jnp.float32
BF16 = jnp.bfloat16

N_DEV = 8
D_MODEL = 1024
GROUP = 256
D_IN = 12 * GROUP
N_LAYERS = 2
HALO = 32
SUB = 128
WIN = SUB + HALO
TOKEN_TILE = 512
REDUCE_TILE = 1024
EPS = 1e-6
VMEM_BYTES_V7X = 64 * 1024 * 1024
VMEM_LIMIT = VMEM_BYTES_V7X - 8 * 1024 * 1024

ADAM_LR = 0.001
ADAM_B1 = 0.9
ADAM_B2 = 0.999
ADAM_EPS = 1e-08
ADAM_WD = 0.01
ADAM_STEP = 10

A_B, A_C, A_X, A_G, B_P, B_G, C_A, C_GL, C_G, D_U, D_V, D_G = range(12)
V_PSCALE, V_BDW, V_LNGC, V_LNBC, V_BPW2, V_LNGD, V_LNBD = range(7)
S_WCONV, S_PSCALE, S_BDW, S_LNGC, S_LNBC, S_BPW2, S_LNGD, S_LNBD, S_WDW = 0, 3, 4, 5, 6, 7, 8, 9, 16
N_SUMS = 64
CONV_A = 3
CONV_C = 31

WEIGHTS = ('norm_g', 'w_ada', 'b_ada', 'w_in', 'w_conv_a', 'w_pool', 'pool_scale', 'w_dw_c', 'b_dw_c', 'ln_g_c',
           'ln_b_c', 'w_pw2_c', 'b_pw2_c', 'ln_g_d', 'ln_b_d', 'w_s_d', 'b_s_d', 'w_out', 'final_g')
REPLICATED = ('norm_g', 'b_ada', 'w_pool', 'pool_scale', 'b_dw_c', 'ln_g_c', 'ln_b_c', 'b_pw2_c', 'ln_g_d', 'ln_b_d',
              'w_s_d', 'b_s_d', 'final_g')
CHANNEL_SHARDED = ('w_conv_a', 'w_dw_c')


def _params(semantics, vmem=VMEM_LIMIT):
    return pltpu.CompilerParams(dimension_semantics=semantics, vmem_limit_bytes=vmem)


def _cols(g):
    return slice(g * GROUP, (g + 1) * GROUP)


def _full(shape):
    return pl.BlockSpec(shape, lambda *_: (0,) * len(shape))


def _silu(x):
    s = jax.nn.sigmoid(x)
    return x * s, s


def _dsilu(x, s):
    return s * (1.0 + x * (1.0 - s))


_GELU_C0 = math.sqrt(2.0 / math.pi)
_GELU_C1 = 0.044715


def _gelu(x):
    th = jnp.tanh(_GELU_C0 * (x + _GELU_C1 * (x * x * x)))
    return 0.5 * x * (1.0 + th), th


def _dgelu(x, th):
    return 0.5 * (1.0 + th) + 0.5 * x * (1.0 - th * th) * (_GELU_C0 * (1.0 + 3.0 * _GELU_C1 * (x * x)))


def _layer_norm(x):
    mu = jnp.mean(x, axis=-1, keepdims=True)
    xc = x - mu
    rstd = lax.rsqrt(jnp.mean(xc * xc, axis=-1, keepdims=True) + EPS)
    return xc * rstd, rstd


def _layer_norm_bwd(dn, n, rstd):
    return rstd * (dn - jnp.mean(dn, axis=-1, keepdims=True) - n * jnp.mean(dn * n, axis=-1, keepdims=True))


def _shift_rows(a, k):
    k = k % a.shape[0]
    return a if k == 0 else pltpu.roll(a, k, 0)


def _row_sum8(a):
    s = a[0:8]
    for m in range(1, a.shape[0] // 8):
        s = s + a[8 * m:8 * m + 8]
    return s


def _lane():
    return lax.broadcasted_iota(jnp.int32, (SUB, GROUP), 1)


def _by_quarter(lane, parts):
    return jnp.where(lane < 64, parts[0], jnp.where(lane < 128, parts[1], jnp.where(lane < 192, parts[2], parts[3])))


def _conv_inputs(z_ref, rows):
    def f(g):
        return z_ref[rows, _cols(g)].astype(F32)
    return f(A_C) * f(A_X), f(B_P), f(C_A) * jax.nn.sigmoid(f(C_GL))


def _fill_past(past_ref, zh_ref, zm_ref, is_first, tile):
    parts = _conv_inputs(zh_ref, slice(None))
    for n, a in enumerate(parts):
        past_ref[0:HALO, _cols(n)] = jnp.where(is_first, 0.0, a)

    def body(j, carry):
        r0 = pl.multiple_of(j * SUB, SUB)
        for n, a in enumerate(_conv_inputs(zm_ref, pl.ds(r0, SUB))):
            past_ref[pl.ds(r0 + HALO, SUB), _cols(n)] = a
        return carry

    lax.fori_loop(0, tile // SUB, body, 0)


def _short_conv_taps(qw):
    return [_shift_rows(qw, CONV_A - 1 - k)[HALO:WIN] for k in range(CONV_A)]


def _window_sums(pw, lane):
    s2 = pw + _shift_rows(pw, 1)
    s4 = s2 + _shift_rows(s2, 2)
    s8 = s4 + _shift_rows(s4, 4)
    s16 = s8 + _shift_rows(s8, 8)
    return _by_quarter(lane, [s[HALO:WIN] for s in (s2, s4, s8, s16)])


def _inv_count(lane, t_first):
    width = _by_quarter(lane, [2.0, 4.0, 8.0, 16.0])
    t = lax.broadcasted_iota(jnp.int32, (SUB, GROUP), 0) + t_first
    return 1.0 / jnp.minimum((t + 1).astype(F32), width)


def _forward_window_sums(ew, lane):
    n = ew.shape[0]
    f2 = ew + _shift_rows(ew, n - 1)
    f4 = f2 + _shift_rows(f2, n - 2)
    f8 = f4 + _shift_rows(f4, n - 4)
    f16 = f8 + _shift_rows(f8, n - 8)
    return _by_quarter(lane, [f[0:SUB] for f in (f2, f4, f8, f16)])


def _mixers_forward(zc, win, t_first, wc_ref, wdw_ref, vec_ref, wp_ref, w2_ref, ws_ref, bs_ref):
    lane = _lane()

    def vec(n):
        return vec_ref[n:n + 1, :]

    taps = _short_conv_taps(win(0))
    o_a = wc_ref[0:1, :] * taps[0] + wc_ref[1:2, :] * taps[1] + wc_ref[2:3, :] * taps[2]
    a_b, a_g = zc(A_B), zc(A_G)
    sg_a, s_a = _silu(a_g)
    y_a = a_b * o_a * sg_a

    pw = win(1)
    ic = _inv_count(lane, t_first)
    pooled = _window_sums(pw, lane) * ic - pw[HALO:WIN]
    pooled_b = pooled.astype(BF16)
    y0_b = jnp.dot(pooled_b, wp_ref[...], preferred_element_type=F32)
    b_g = zc(B_G)
    sg_b, s_b = _silu(b_g)
    y_b = y0_b * vec(V_PSCALE) * sg_b

    hw = win(2)
    o_c = wdw_ref[CONV_C - 1:CONV_C, :] * hw[HALO:WIN] + vec(V_BDW)
    for k in range(CONV_C - 1):
        o_c = o_c + wdw_ref[k:k + 1, :] * _shift_rows(hw, CONV_C - 1 - k)[HALO:WIN]
    n_c, rstd_c = _layer_norm(o_c)
    ln_c = n_c * vec(V_LNGC) + vec(V_LNBC)
    sl_c, ssl_c = _silu(ln_c)
    sl_b = sl_c.astype(BF16)
    yc = jnp.dot(sl_b, w2_ref[...], preferred_element_type=F32) + vec(V_BPW2)
    c_g = zc(C_G)
    sg_c, s_c = _silu(c_g)
    y_c = yc * sg_c

    d_u, d_v, d_g = zc(D_U), zc(D_V), zc(D_G)
    u, th_u = _gelu(d_u)
    gv, th_v = _gelu(d_v)
    n_d, rstd_d = _layer_norm(gv)
    v_b = (n_d * vec(V_LNGD) + vec(V_LNBD)).astype(BF16)
    r = jnp.dot(ws_ref[...], v_b, preferred_element_type=F32)
    mixed = _by_quarter(lane, [r[h * SUB:(h + 1) * SUB] for h in range(4)]) + bs_ref[...]
    sg_d, s_d = _silu(d_g)
    y_d = u * mixed * sg_d

    saved = dict(lane=lane, taps=taps, o_a=o_a, a_b=a_b, a_g=a_g, sg_a=sg_a, s_a=s_a,
                 ic=ic, pooled_b=pooled_b, y0_b=y0_b, b_g=b_g, sg_b=sg_b, s_b=s_b,
                 hw=hw, n_c=n_c, rstd_c=rstd_c, ln_c=ln_c, ssl_c=ssl_c, sl_b=sl_b, yc=yc, c_g=c_g, sg_c=sg_c, s_c=s_c,
                 d_u=d_u, d_v=d_v, d_g=d_g, u=u, th_u=th_u, th_v=th_v, n_d=n_d, rstd_d=rstd_d, v_b=v_b, mixed=mixed,
                 sg_d=sg_d, s_d=s_d)
    return (y_a, y_b, y_c, y_d), saved


def _in_proj(x, gs, shift, w_in_b, tile):
    n_tok = x.shape[0]

    def body(x_ref, gs_ref, sh_ref, w_ref, h_ref, z_ref):
        xv = x_ref[...]
        r = lax.rsqrt(jnp.mean(xv * xv, axis=-1, keepdims=True) + EPS)
        h = ((xv * r) * gs_ref[...] + sh_ref[...]).astype(BF16)
        h_ref[...] = h
        for j in range(D_IN // D_MODEL):
            cs = slice(j * D_MODEL, (j + 1) * D_MODEL)
            z_ref[:, cs] = jnp.dot(h, w_ref[:, cs], preferred_element_type=F32).astype(BF16)

    return pl.pallas_call(
        body, name="in_proj", grid=(n_tok // tile,),
        in_specs=[pl.BlockSpec((tile, D_MODEL), lambda i: (i, 0)), _full((1, D_MODEL)), _full((1, D_MODEL)),
                  _full((D_MODEL, D_IN))],
        out_specs=[pl.BlockSpec((tile, D_MODEL), lambda i: (i, 0)), pl.BlockSpec((tile, D_IN), lambda i: (i, 0))],
        out_shape=[jax.ShapeDtypeStruct((n_tok, D_MODEL), BF16), jax.ShapeDtypeStruct((n_tok, D_IN), BF16)],
        compiler_params=_params(("parallel",)),
    )(x, gs, shift, w_in_b)


def _small_specs(with_transposes):
    specs = [_full((8, GROUP)), _full((HALO, GROUP)), _full((16, GROUP)), _full((GROUP, GROUP)), _full((GROUP, GROUP)),
             _full((4 * SUB, SUB)), _full((SUB, GROUP))]
    if with_transposes:
        specs += [_full((GROUP, GROUP)), _full((GROUP, GROUP)), _full((4 * SUB, SUB))]
    return specs


def _mix_out(z, x, gate, small, w_out_b, tile):
    n_tok = x.shape[0]
    per_halo = tile // HALO

    def body(zm_ref, zh_ref, x_ref, gate_ref, wc_ref, wdw_ref, vec_ref, wp_ref, w2_ref, ws_ref, bs_ref, wout_ref,
             xo_ref, past_ref, ycat_ref):
        i = pl.program_id(0)
        _fill_past(past_ref, zh_ref, zm_ref, i == 0, tile)

        def step(j, carry):
            r0 = pl.multiple_of(j * SUB, SUB)
            rows = pl.ds(r0, SUB)
            ys, _ = _mixers_forward(
                lambda g: zm_ref[rows, _cols(g)].astype(F32), lambda n: past_ref[pl.ds(r0, WIN), _cols(n)],
                i * tile + r0, wc_ref, wdw_ref, vec_ref, wp_ref, w2_ref, ws_ref, bs_ref)
            for n, y in enumerate(ys):
                ycat_ref[rows, _cols(n)] = y.astype(BF16)
            return carry

        lax.fori_loop(0, tile // SUB, step, 0)
        y = jnp.dot(ycat_ref[...], wout_ref[...], preferred_element_type=F32)
        xo_ref[...] = x_ref[...] + gate_ref[...] * y

    return pl.pallas_call(
        body, name="mix_out", grid=(n_tok // tile,),
        in_specs=[pl.BlockSpec((tile, D_IN), lambda i: (i, 0)),
                  pl.BlockSpec((HALO, D_IN), lambda i: (jnp.maximum(i * per_halo - 1, 0), 0)),
                  pl.BlockSpec((tile, D_MODEL), lambda i: (i, 0)), _full((1, D_MODEL)),
                  *_small_specs(False), _full((D_MODEL, D_MODEL))],
        out_specs=pl.BlockSpec((tile, D_MODEL), lambda i: (i, 0)),
        out_shape=jax.ShapeDtypeStruct((n_tok, D_MODEL), F32),
        scratch_shapes=[pltpu.VMEM((tile + HALO, 3 * GROUP), F32), pltpu.VMEM((tile, D_MODEL), BF16)],
        compiler_params=_params(("parallel",)),
    )(z, z, x, gate, *small, w_out_b)


def _mix_bwd(z, dx_next, gate, small, small_t, w_out_t, tile):
    n_tok = z.shape[0]
    n_tiles = n_tok // tile
    per_halo = tile // HALO

    def tile_of(i):
        return n_tiles - 1 - i

    def body(zm_ref, zh_ref, dxn_ref, gate_ref, wc_ref, wdw_ref, vec_ref, wp_ref, w2_ref, ws_ref, bs_ref,
             wpt_ref, w2t_ref, wst_ref, woutt_ref,
             dz_ref, ycat_ref, sums_ref, dwp_ref, dw2_ref, dws_ref, dbs_ref,
             past_ref, future_ref, dy_ref, acc_ref):
        i = pl.program_id(0)
        t = tile_of(i)

        @pl.when(i == 0)
        def _():
            acc_ref[...] = jnp.zeros_like(acc_ref)
            dwp_ref[...] = jnp.zeros_like(dwp_ref)
            dw2_ref[...] = jnp.zeros_like(dw2_ref)
            dws_ref[...] = jnp.zeros_like(dws_ref)
            dbs_ref[...] = jnp.zeros_like(dbs_ref)
            future_ref[tile:tile + HALO, :] = jnp.zeros((HALO, 3 * GROUP), F32)

        dy_ref[...] = jnp.dot((dxn_ref[...] * gate_ref[...]).astype(BF16), woutt_ref[...], preferred_element_type=F32)
        _fill_past(past_ref, zh_ref, zm_ref, t == 0, tile)

        def vec(n):
            return vec_ref[n:n + 1, :]

        def step(jj, carry):
            j = tile // SUB - 1 - jj
            r0 = pl.multiple_of(j * SUB, SUB)
            rows = pl.ds(r0, SUB)

            def zc(g):
                return zm_ref[rows, _cols(g)].astype(F32)

            def add(n, a):
                acc_ref[n] = acc_ref[n] + _row_sum8(a)

            def put(g, a):
                dz_ref[rows, _cols(g)] = a.astype(BF16)

            def future_window(n, a):
                future_ref[rows, _cols(n)] = a
                return future_ref[pl.ds(r0, WIN), _cols(n)]

            ys, s = _mixers_forward(zc, lambda n: past_ref[pl.ds(r0, WIN), _cols(n)], t * tile + r0,
                                    wc_ref, wdw_ref, vec_ref, wp_ref, w2_ref, ws_ref, bs_ref)
            for n, y in enumerate(ys):
                ycat_ref[rows, _cols(n)] = y.astype(BF16)
            lane = s["lane"]

            dy = dy_ref[rows, _cols(0)]
            put(A_B, dy * s["o_a"] * s["sg_a"])
            put(A_G, dy * s["a_b"] * s["o_a"] * _dsilu(s["a_g"], s["s_a"]))
            do = dy * s["a_b"] * s["sg_a"]
            for k in range(CONV_A):
                add(S_WCONV + k, do * s["taps"][k])
            dow = future_window(0, do)
            dq = wc_ref[CONV_A - 1:CONV_A, :] * dow[0:SUB]
            for k in range(CONV_A - 1):
                dq = dq + wc_ref[k:k + 1, :] * _shift_rows(dow, WIN - (CONV_A - 1 - k))[0:SUB]
            put(A_C, dq * zc(A_X))
            put(A_X, dq * zc(A_C))

            dy = dy_ref[rows, _cols(1)]
            put(B_G, dy * (s["y0_b"] * vec(V_PSCALE)) * _dsilu(s["b_g"], s["s_b"]))
            dyb = dy * s["sg_b"]
            add(S_PSCALE, dyb * s["y0_b"])
            dpw_b = (dyb * vec(V_PSCALE)).astype(BF16)
            dwp_ref[...] += lax.dot_general(s["pooled_b"], dpw_b, (((0,), (0,)), ((), ())), preferred_element_type=F32)
            dpooled = jnp.dot(dpw_b, wpt_ref[...], preferred_element_type=F32)
            ew = future_window(1, dpooled * s["ic"])
            put(B_P, _forward_window_sums(ew, lane) - dpooled)

            dy = dy_ref[rows, _cols(2)]
            put(C_G, dy * s["yc"] * _dsilu(s["c_g"], s["s_c"]))
            dyc = dy * s["sg_c"]
            add(S_BPW2, dyc)
            dyc_b = dyc.astype(BF16)
            dw2_ref[...] += lax.dot_general(s["sl_b"], dyc_b, (((0,), (0,)), ((), ())), preferred_element_type=F32)
            dln = jnp.dot(dyc_b, w2t_ref[...], preferred_element_type=F32) * _dsilu(s["ln_c"], s["ssl_c"])
            add(S_LNGC, dln * s["n_c"])
            add(S_LNBC, dln)
            do = _layer_norm_bwd(dln * vec(V_LNGC), s["n_c"], s["rstd_c"])
            add(S_BDW, do)
            hw = s["hw"]
            for k in range(CONV_C):
                add(S_WDW + k, do * _shift_rows(hw, CONV_C - 1 - k)[HALO:WIN])
            dow = future_window(2, do)
            dhc = wdw_ref[CONV_C - 1:CONV_C, :] * dow[0:SUB]
            for k in range(CONV_C - 1):
                dhc = dhc + wdw_ref[k:k + 1, :] * _shift_rows(dow, WIN - (CONV_C - 1 - k))[0:SUB]
            c_a = zc(C_A)
            sgl = jax.nn.sigmoid(zc(C_GL))
            put(C_A, dhc * sgl)
            put(C_GL, dhc * c_a * sgl * (1.0 - sgl))

            dy = dy_ref[rows, _cols(3)]
            put(D_G, dy * s["u"] * s["mixed"] * _dsilu(s["d_g"], s["s_d"]))
            put(D_U, dy * s["mixed"] * s["sg_d"] * _dgelu(s["d_u"], s["th_u"]))
            dmixed = dy * s["u"] * s["sg_d"]
            dbs_ref[...] += dmixed
            by_head = jnp.concatenate(
                [jnp.where((lane >= 64 * h) & (lane < 64 * h + 64), dmixed, 0.0) for h in range(4)], axis=0).astype(BF16)
            dws_ref[...] += lax.dot_general(by_head, s["v_b"], (((1,), (1,)), ((), ())), preferred_element_type=F32)
            rv = jnp.dot(wst_ref[...], dmixed.astype(BF16), preferred_element_type=F32)
            dv = _by_quarter(lane, [rv[h * SUB:(h + 1) * SUB] for h in range(4)])
            add(S_LNGD, dv * s["n_d"])
            add(S_LNBD, dv)
            dgv = _layer_norm_bwd(dv * vec(V_LNGD), s["n_d"], s["rstd_d"])
            put(D_V, dgv * _dgelu(s["d_v"], s["th_v"]))
            return carry

        lax.fori_loop(0, tile // SUB, step, 0)
        future_ref[tile:tile + HALO, :] = future_ref[0:HALO, :]

        @pl.when(i == n_tiles - 1)
        def _():
            for n in range(N_SUMS):
                sums_ref[n:n + 1, :] = jnp.sum(acc_ref[n], axis=0, keepdims=True)

    return pl.pallas_call(
        body, name="mix_bwd", grid=(n_tiles,),
        in_specs=[pl.BlockSpec((tile, D_IN), lambda i: (tile_of(i), 0)),
                  pl.BlockSpec((HALO, D_IN), lambda i: (jnp.maximum(tile_of(i) * per_halo - 1, 0), 0)),
                  pl.BlockSpec((tile, D_MODEL), lambda i: (tile_of(i), 0)), _full((1, D_MODEL)),
                  *_small_specs(True), _full((D_MODEL, D_MODEL))],
        out_specs=[pl.BlockSpec((tile, D_IN), lambda i: (tile_of(i), 0)),
                   pl.BlockSpec((tile, D_MODEL), lambda i: (tile_of(i), 0)),
                   _full((N_SUMS, GROUP)), _full((GROUP, GROUP)), _full((GROUP, GROUP)), _full((4 * SUB, SUB)),
                   _full((SUB, GROUP))],
        out_shape=[jax.ShapeDtypeStruct((n_tok, D_IN), BF16), jax.ShapeDtypeStruct((n_tok, D_MODEL), BF16),
                   jax.ShapeDtypeStruct((N_SUMS, GROUP), F32), jax.ShapeDtypeStruct((GROUP, GROUP), F32),
                   jax.ShapeDtypeStruct((GROUP, GROUP), F32), jax.ShapeDtypeStruct((4 * SUB, SUB), F32),
                   jax.ShapeDtypeStruct((SUB, GROUP), F32)],
        scratch_shapes=[pltpu.VMEM((tile + HALO, 3 * GROUP), F32), pltpu.VMEM((tile + HALO, 3 * GROUP), F32),
                        pltpu.VMEM((tile, D_MODEL), F32), pltpu.VMEM((N_SUMS, 8, GROUP), F32)],
        compiler_params=_params(("arbitrary",)),
    )(z, z, dx_next, gate, *small, *small_t, w_out_t)


def _norm_bwd(x, dz, dx_next, gs, w_in_t, tile):
    n_tok = x.shape[0]
    n_tiles = n_tok // tile

    def body(x_ref, dz_ref, dxn_ref, gs_ref, wt_ref, dx_ref, dsh_ref, dgs_ref, acc_ref):
        i = pl.program_id(0)

        @pl.when(i == 0)
        def _():
            acc_ref[...] = jnp.zeros_like(acc_ref)

        dh = jnp.dot(dz_ref[...], wt_ref[...], preferred_element_type=F32)
        xv = x_ref[...]
        r = lax.rsqrt(jnp.mean(xv * xv, axis=-1, keepdims=True) + EPS)
        xn = xv * r
        acc_ref[0] = acc_ref[0] + _row_sum8(dh)
        acc_ref[1] = acc_ref[1] + _row_sum8(dh * xn)
        dxn = dh * gs_ref[...]
        dx_ref[...] = dxn_ref[...] + r * (dxn - xn * jnp.mean(dxn * xn, axis=-1, keepdims=True))

        @pl.when(i == n_tiles - 1)
        def _():
            dsh_ref[...] = jnp.sum(acc_ref[0], axis=0, keepdims=True)
            dgs_ref[...] = jnp.sum(acc_ref[1], axis=0, keepdims=True)

    return pl.pallas_call(
        body, name="norm_bwd", grid=(n_tiles,),
        in_specs=[pl.BlockSpec((tile, D_MODEL), lambda i: (i, 0)), pl.BlockSpec((tile, D_IN), lambda i: (i, 0)),
                  pl.BlockSpec((tile, D_MODEL), lambda i: (i, 0)), _full((1, D_MODEL)), _full((D_IN, D_MODEL))],
        out_specs=[pl.BlockSpec((tile, D_MODEL), lambda i: (i, 0)), _full((1, D_MODEL)), _full((1, D_MODEL))],
        out_shape=[jax.ShapeDtypeStruct((n_tok, D_MODEL), F32), jax.ShapeDtypeStruct((1, D_MODEL), F32),
                   jax.ShapeDtypeStruct((1, D_MODEL), F32)],
        scratch_shapes=[pltpu.VMEM((2, 8, D_MODEL), F32)],
        compiler_params=_params(("arbitrary",)),
    )(x, dz, dx_next, gs, w_in_t)


def _loss_head(x, final_g, target, tile):
    n_tok = x.shape[0]
    n_tiles = n_tok // tile

    def body(x_ref, g_ref, t_ref, dx_ref, loss_ref, dg_ref, acc_ref):
        i = pl.program_id(0)

        @pl.when(i == 0)
        def _():
            acc_ref[...] = jnp.zeros_like(acc_ref)

        xv = x_ref[...]
        r = lax.rsqrt(jnp.mean(xv * xv, axis=-1, keepdims=True) + EPS)
        xn = xv * r
        err = xn * g_ref[...] - t_ref[...]
        acc_ref[0] = acc_ref[0] + _row_sum8(err * err)
        dy = err * (1.0 / D_MODEL)
        acc_ref[1] = acc_ref[1] + _row_sum8(dy * xn)
        a = dy * g_ref[...]
        dx_ref[...] = r * (a - xn * jnp.mean(a * xn, axis=-1, keepdims=True))

        @pl.when(i == n_tiles - 1)
        def _():
            loss_ref[...] = jnp.full((8, 128), 0.5 / D_MODEL, F32) * jnp.sum(acc_ref[0])
            dg_ref[...] = jnp.sum(acc_ref[1], axis=0, keepdims=True)

    return pl.pallas_call(
        body, name="loss_head", grid=(n_tiles,),
        in_specs=[pl.BlockSpec((tile, D_MODEL), lambda i: (i, 0)), _full((1, D_MODEL)),
                  pl.BlockSpec((tile, D_MODEL), lambda i: (i, 0))],
        out_specs=[pl.BlockSpec((tile, D_MODEL), lambda i: (i, 0)), _full((8, 128)), _full((1, D_MODEL))],
        out_shape=[jax.ShapeDtypeStruct((n_tok, D_MODEL), F32), jax.ShapeDtypeStruct((8, 128), F32),
                   jax.ShapeDtypeStruct((1, D_MODEL), F32)],
        scratch_shapes=[pltpu.VMEM((2, 8, D_MODEL), F32)],
        compiler_params=_params(("arbitrary",)),
    )(x, final_g, target)


def _tokens_matmul(a, b, name):
    n_tok, ka = a.shape
    nb = b.shape[1]
    tk = min(REDUCE_TILE, n_tok)
    cb = min(D_MODEL, nb)

    def body(a_ref, b_ref, o_ref):
        @pl.when(pl.program_id(1) == 0)
        def _():
            o_ref[...] = jnp.zeros_like(o_ref)

        o_ref[...] += lax.dot_general(a_ref[...], b_ref[...].astype(BF16), (((0,), (0,)), ((), ())),
                                      preferred_element_type=F32)

    return pl.pallas_call(
        body, name=name, grid=(nb // cb, n_tok // tk),
        in_specs=[pl.BlockSpec((tk, ka), lambda j, i: (i, 0)), pl.BlockSpec((tk, cb), lambda j, i: (i, j))],
        out_specs=pl.BlockSpec((ka, cb), lambda j, i: (0, j)),
        out_shape=jax.ShapeDtypeStruct((ka, nb), F32),
        compiler_params=_params(("parallel", "arbitrary")),
    )(a, b)


def _out_proj_grads(m, w_out_b, gate):
    rb = 256
    n_blocks = D_MODEL // rb

    def body(m_ref, w_ref, gate_ref, dw_ref, dgate_ref, acc_ref):
        i = pl.program_id(0)

        @pl.when(i == 0)
        def _():
            acc_ref[...] = jnp.zeros_like(acc_ref)

        mv = m_ref[...]
        dw_ref[...] = mv * gate_ref[...]
        acc_ref[...] += _row_sum8(mv * w_ref[...].astype(F32))

        @pl.when(i == n_blocks - 1)
        def _():
            dgate_ref[...] = jnp.sum(acc_ref[...], axis=0, keepdims=True)

    return pl.pallas_call(
        body, name="out_proj_grads", grid=(n_blocks,),
        in_specs=[pl.BlockSpec((rb, D_MODEL), lambda i: (i, 0)), pl.BlockSpec((rb, D_MODEL), lambda i: (i, 0)),
                  _full((1, D_MODEL))],
        out_specs=[pl.BlockSpec((rb, D_MODEL), lambda i: (i, 0)), _full((1, D_MODEL))],
        out_shape=[jax.ShapeDtypeStruct((D_MODEL, D_MODEL), F32), jax.ShapeDtypeStruct((1, D_MODEL), F32)],
        scratch_shapes=[pltpu.VMEM((8, D_MODEL), F32)],
        compiler_params=_params(("arbitrary",)),
    )(m, w_out_b, gate)


def _modulation_columns(c_all, w_ada, b_cols):
    cols = w_ada.shape[2]

    def body(c_ref, w_ref, b_ref, ca_ref, mod_ref):
        ca, _ = _silu(c_ref[...])
        ca_ref[...] = ca
        for l in range(N_LAYERS):
            mod_ref[l] = jnp.dot(ca, w_ref[l], precision=lax.Precision.HIGHEST, preferred_element_type=F32) + b_ref[l:l + 1, :]

    return pl.pallas_call(
        body, name="modulation_columns",
        out_shape=[jax.ShapeDtypeStruct((N_DEV, D_MODEL), F32), jax.ShapeDtypeStruct((N_LAYERS, N_DEV, cols), F32)],
        compiler_params=pltpu.CompilerParams(vmem_limit_bytes=VMEM_LIMIT),
    )(c_all, w_ada, b_cols)


def _adam(w, g, m, v):
    m2 = ADAM_B1 * m + (1.0 - ADAM_B1) * g
    v2 = ADAM_B2 * v + (1.0 - ADAM_B2) * (g * g)
    m_hat = m2 / (1.0 - ADAM_B1 ** ADAM_STEP)
    v_hat = v2 / (1.0 - ADAM_B2 ** ADAM_STEP)
    return -ADAM_LR * (m_hat / (jnp.sqrt(v_hat) + ADAM_EPS) + ADAM_WD * w), m2, v2


def _row_block(rows, cols, slots):
    target = max(8, (1 << 19) // (cols * max(slots, 1)))
    rb = rows
    while rb > target and rb % 2 == 0 and (rb // 2) % 8 == 0:
        rb //= 2
    return rb


def _sum_slots(slots, name):
    _, rows, cols = slots.shape
    rb = _row_block(rows, cols, N_DEV)

    def body(s_ref, g_ref):
        g = s_ref[0]
        for q in range(1, N_DEV):
            g = g + s_ref[q]
        g_ref[...] = g

    return pl.pallas_call(
        body, name=name, grid=(rows // rb,),
        in_specs=[pl.BlockSpec((N_DEV, rb, cols), lambda i: (0, i, 0))],
        out_specs=pl.BlockSpec((rb, cols), lambda i: (i, 0)),
        out_shape=jax.ShapeDtypeStruct((rows, cols), F32),
        compiler_params=_params(("parallel",)),
    )(slots)


def _adam_update(w, g, m, v, name):
    rows, cols = w.shape
    slotted = g.ndim == 3
    rb = _row_block(rows, cols, N_DEV if slotted else 1)

    def body(w_ref, g_ref, m_ref, v_ref, go_ref, d_ref, mo_ref, vo_ref):
        if slotted:
            gv = g_ref[0]
            for q in range(1, N_DEV):
                gv = gv + g_ref[q]
        else:
            gv = g_ref[...]
        go_ref[...] = gv
        d_ref[...], mo_ref[...], vo_ref[...] = _adam(w_ref[...], gv, m_ref[...], v_ref[...])

    blk = pl.BlockSpec((rb, cols), lambda i: (i, 0))
    g_blk = pl.BlockSpec((N_DEV, rb, cols), lambda i: (0, i, 0)) if slotted else blk
    return pl.pallas_call(
        body, name=name, grid=(rows // rb,),
        in_specs=[blk, g_blk, blk, blk], out_specs=[blk] * 4,
        out_shape=[jax.ShapeDtypeStruct((rows, cols), F32)] * 4,
        compiler_params=_params(("parallel",)),
    )(w, g, m, v)


def _ada_update(ca_t, dmod_cols, w, m, v):
    _, rows, cols = w.shape

    def body(ca_ref, dm_ref, w_ref, m_ref, v_ref, g_ref, d_ref, mo_ref, vo_ref):
        g = ca_ref[:, 0:1] * dm_ref[0, 0:1, :]
        for b in range(1, N_DEV):
            g = g + ca_ref[:, b:b + 1] * dm_ref[0, b:b + 1, :]
        g_ref[0] = g
        d_ref[0], mo_ref[0], vo_ref[0] = _adam(w_ref[0], g, m_ref[0], v_ref[0])

    blk = pl.BlockSpec((1, rows, cols), lambda l: (l, 0, 0))
    return pl.pallas_call(
        body, name="ada_update", grid=(N_LAYERS,),
        in_specs=[_full((rows, N_DEV)), pl.BlockSpec((1, N_DEV, cols), lambda l: (l, 0, 0)), blk, blk, blk],
        out_specs=[blk] * 4, out_shape=[jax.ShapeDtypeStruct(w.shape, F32)] * 4,
        compiler_params=_params(("parallel",)),
    )(ca_t, dmod_cols, w, m, v)


def _exchange(name, out_shapes, plans):
    n = len(plans)
    hbm = pl.BlockSpec(memory_space=pltpu.HBM)

    def body(*refs):
        srcs, outs = refs[:n], refs[n:n + len(out_shapes)]
        send_sems, recv_sems, local_sems = refs[n + len(out_shapes):]
        x, y, c = lax.axis_index("x"), lax.axis_index("y"), lax.axis_index("c")
        me = 4 * x + 2 * y + c

        def peer(k):
            px = 1 - x if k & 4 else x
            py = 1 - y if k & 2 else y
            pc = 1 - c if k & 1 else c
            return (px, py, pc), 4 * px + 2 * py + pc

        def remote(i, k, incoming):
            _, o, send, land = plans[i]
            coords, p = peer(k)
            return pltpu.make_async_remote_copy(
                src_ref=send(srcs[i], p), dst_ref=land(outs[o], p if incoming else me),
                send_sem=send_sems.at[i, k - 1], recv_sem=recv_sems.at[i, k - 1],
                device_id=coords, device_id_type=pl.DeviceIdType.MESH)

        local = [pltpu.make_async_copy(plans[i][2](srcs[i], me), plans[i][3](outs[plans[i][1]], me), local_sems.at[i])
                 for i in range(n)]
        for cp in local:
            cp.start()
        sent = [remote(i, k, False) for k in range(1, N_DEV) for i in range(n)]
        for cp in sent:
            cp.start()
        for k in range(1, N_DEV):
            for i in range(n):
                remote(i, k, True).wait_recv()
        for cp in sent:
            cp.wait_send()
        for cp in local:
            cp.wait()

    return pl.pallas_call(
        body, name=name,
        in_specs=[hbm] * n, out_specs=[hbm] * len(out_shapes), out_shape=list(out_shapes),
        scratch_shapes=[pltpu.SemaphoreType.DMA((n, N_DEV - 1)), pltpu.SemaphoreType.DMA((n, N_DEV - 1)),
                        pltpu.SemaphoreType.DMA((n,))],
    )(*[p[0] for p in plans])


def _gather(a):
    return jax.ShapeDtypeStruct((N_DEV,) + a.shape, a.dtype), lambda s, p: s, lambda o, q: o.at[q]


def _gather_rows(a):
    nl, r, c = a.shape
    return (jax.ShapeDtypeStruct((nl, N_DEV * r, c), a.dtype), lambda s, p: s,
            lambda o, q: o.at[:, pl.ds(pl.multiple_of(q * r, r), r), :])


def _gather_cols(a):
    nl, r, c = a.shape
    return (jax.ShapeDtypeStruct((nl, r, N_DEV * c), a.dtype), lambda s, p: s,
            lambda o, q: o.at[:, :, pl.ds(pl.multiple_of(q * c, c), c)])


def _scatter_rows(a, layer):
    r = a.shape[0] // N_DEV
    return (jax.ShapeDtypeStruct((N_DEV, N_LAYERS, r, a.shape[1]), a.dtype),
            lambda s, p: s.at[pl.ds(pl.multiple_of(p * r, r), r), :], lambda o, q: o.at[q, layer])


def _scatter_cols(a, layer):
    c = a.shape[1] // N_DEV
    return (jax.ShapeDtypeStruct((N_DEV, N_LAYERS, a.shape[0], c), a.dtype),
            lambda s, p: s.at[:, pl.ds(pl.multiple_of(p * c, c), c)], lambda o, q: o.at[q, layer])


def _exchange_each(name, arrays, rules):
    shapes, plans = [], []
    for o, (a, rule) in enumerate(zip(arrays, rules)):
        shape, send, land = rule(a)
        shapes.append(shape)
        plans.append((a, o, send, land))
    return _exchange(name, shapes, plans)


def _pack(pieces):
    flat = []
    for a in pieces:
        f = a.reshape(-1)
        flat.append(jnp.pad(f, (0, (-f.shape[0]) % 128)))
    total = sum(f.shape[0] for f in flat)
    flat.append(jnp.zeros(((-total) % 1024,), F32))
    return jnp.concatenate(flat).reshape(-1, 128)


def _unpack(buf, shapes, lead=()):
    flat = buf.reshape(lead + (-1,))
    out, off = [], 0
    for s in shapes:
        n = math.prod(s)
        out.append(flat[..., off:off + n].reshape(lead + tuple(s)))
        off += n + (-n) % 128
    return out


def _pad_rows(a, rows):
    return jnp.pad(a, ((0, rows - a.shape[0]), (0, 0)))


def _layer_tables(l, w):
    eye = jnp.eye(4, dtype=F32)
    wp = jnp.einsum('gcd,gh->gchd', w['w_pool'][l], eye).reshape(GROUP, GROUP)
    tril = jnp.tril(jnp.ones((SUB, SUB), F32))
    ws = w['w_s_d'][l] * tril
    vec = jnp.stack([w[n][l] for n in ('pool_scale', 'b_dw_c', 'ln_g_c', 'ln_b_c', 'b_pw2_c', 'ln_g_d', 'ln_b_d')])
    small = (_pad_rows(w['w_conv_a'][l], 8), _pad_rows(w['w_dw_c'][l], HALO), _pad_rows(vec, 16), wp.astype(BF16),
             w['w_pw2_c'][l].astype(BF16), ws.reshape(4 * SUB, SUB).astype(BF16),
             jnp.repeat(w['b_s_d'][l].T, 64, axis=1))
    small_t = (wp.T.astype(BF16), w['w_pw2_c'][l].T.astype(BF16),
               ws.transpose(0, 2, 1).reshape(4 * SUB, SUB).astype(BF16))
    return small, small_t


def kernel(x, c, norm_g, w_ada, b_ada, w_in, w_conv_a, w_pool, pool_scale, w_dw_c, b_dw_c, ln_g_c, ln_b_c, w_pw2_c, b_pw2_c, ln_g_d, ln_b_d, w_s_d, b_s_d, w_out, final_g, loss_target, m_norm_g, m_w_ada, m_b_ada, m_w_in, m_w_conv_a, m_w_pool, m_pool_scale, m_w_dw_c, m_b_dw_c, m_ln_g_c, m_ln_b_c, m_w_pw2_c, m_b_pw2_c, m_ln_g_d, m_ln_b_d, m_w_s_d, m_b_s_d, m_w_out, m_final_g, v_norm_g, v_w_ada, v_b_ada, v_w_in, v_w_conv_a, v_w_pool, v_pool_scale, v_w_dw_c, v_b_dw_c, v_ln_g_c, v_ln_b_c, v_w_pw2_c, v_b_pw2_c, v_ln_g_d, v_ln_b_d, v_w_s_d, v_b_s_d, v_w_out, v_final_g):
    given = dict(locals())
    shard = {n: given[n] for n in WEIGHTS}
    mom_m = {n: given['m_' + n] for n in WEIGHTS}
    mom_v = {n: given['v_' + n] for n in WEIGHTS}
    me = 4 * lax.axis_index("x") + 2 * lax.axis_index("y") + lax.axis_index("c")
    n_tok = x.shape[1]
    tile = min(TOKEN_TILE, n_tok)
    x0 = x.reshape(n_tok, D_MODEL)
    target = loss_target.reshape(n_tok, D_MODEL)
    ada_cols = w_ada.shape[2]

    c_all, w_in_b, w_out_b, w_pw2_full, wconv_parts, wdw_parts = _exchange_each(
        "gather_weights", [c, w_in.astype(BF16), w_out.astype(BF16), w_pw2_c, w_conv_a, w_dw_c],
        [_gather, _gather_cols, _gather_rows, _gather_rows, _gather, _gather])
    full = dict(shard)
    full['w_pw2_c'] = w_pw2_full
    full['w_conv_a'] = wconv_parts.transpose(1, 2, 0, 3).reshape(N_LAYERS, CONV_A, GROUP)
    full['w_dw_c'] = wdw_parts.transpose(1, 2, 0, 3).reshape(N_LAYERS, CONV_C, GROUP)

    b_cols = lax.dynamic_slice_in_dim(b_ada, me * ada_cols, ada_cols, axis=1)
    c_act, mod_cols = _modulation_columns(c_all.reshape(N_DEV, D_MODEL), w_ada, b_cols)
    (mod_all,) = _exchange_each("gather_modulation", [mod_cols], [_gather])
    mod = lax.dynamic_index_in_dim(mod_all, me, axis=2, keepdims=False)
    mod = mod.transpose(1, 0, 2).reshape(N_LAYERS, 3 * D_MODEL)
    shift, scale, gate = (mod[:, k * D_MODEL:(k + 1) * D_MODEL].reshape(N_LAYERS, 1, D_MODEL) for k in range(3))
    gs = norm_g.reshape(N_LAYERS, 1, D_MODEL) * (1.0 + scale)

    tables = [_layer_tables(l, full) for l in range(N_LAYERS)]
    xs, hs, zs = [x0], [], []
    for l in range(N_LAYERS):
        h, z = _in_proj(xs[l], gs[l], shift[l], w_in_b[l], tile)
        hs.append(h)
        zs.append(z)
        xs.append(_mix_out(z, xs[l], gate[l], tables[l][0], w_out_b[l], tile))
    dx, loss_part, dfinal_g = _loss_head(xs[N_LAYERS], final_g.reshape(1, D_MODEL), target, tile)
    loss = lax.psum(loss_part[0, 0], ("x", "y", "c"))

    part = {}
    layer_parts = [None] * N_LAYERS
    for l in reversed(range(N_LAYERS)):
        small, small_t = tables[l]
        dz, ycat, sums, dwp, dw2, dws, dbs = _mix_bwd(zs[l], dx, gate[l], small, small_t, w_out_b[l].T, tile)
        m_out = _tokens_matmul(ycat, dx, "out_proj_tokens_matmul")
        dw_out, dgate = _out_proj_grads(m_out, w_out_b[l], gate[l])
        dx, dshift, dgs = _norm_bwd(xs[l], dz, dx, gs[l], w_in_b[l].T, tile)
        dw_in = _tokens_matmul(hs[l], dz, "in_proj_tokens_matmul")
        layer_parts[l] = dict(
            dw_in=dw_in, dw_out=dw_out, dw2=dw2,
            b_ada=jnp.concatenate([dshift, dgs * norm_g[l][None], dgate], axis=1)[0],
            norm_g=(dgs * (1.0 + scale[l]))[0],
            w_conv_a=sums[S_WCONV:S_WCONV + CONV_A], w_dw_c=sums[S_WDW:S_WDW + CONV_C],
            pool_scale=sums[S_PSCALE], b_dw_c=sums[S_BDW], ln_g_c=sums[S_LNGC], ln_b_c=sums[S_LNBC],
            b_pw2_c=sums[S_BPW2], ln_g_d=sums[S_LNGD], ln_b_d=sums[S_LNBD],
            w_pool=jnp.einsum('gchd,gh->gcd', dwp.reshape(4, 64, 4, 64), jnp.eye(4, dtype=F32)),
            w_s_d=dws.reshape(4, SUB, SUB) * jnp.tril(jnp.ones((SUB, SUB), F32)),
            b_s_d=dbs.reshape(SUB, 4, 64).sum(axis=-1).T)
    grad_x = dx.reshape(x.shape)
    for n in REPLICATED + CHANNEL_SHARDED:
        part[n] = dfinal_g[0] if n == 'final_g' else jnp.stack([layer_parts[l][n] for l in range(N_LAYERS)])

    small_names = REPLICATED + CHANNEL_SHARDED
    small_shapes = [part[n].shape for n in small_names]
    packed = _pack([part[n] for n in small_names])
    shape, send, land = _gather(packed)
    out_shapes, plans = [shape], [(packed, 0, send, land)]
    for key, rule in (('dw_in', _scatter_cols), ('dw_out', _scatter_rows), ('dw2', _scatter_rows)):
        for l in range(N_LAYERS):
            shape, send, land = rule(layer_parts[l][key], l)
            if l == 0:
                out_shapes.append(shape)
            plans.append((layer_parts[l][key], len(out_shapes) - 1, send, land))
    small_slots, slots_w_in, slots_w_out, slots_w_pw2 = _exchange("exchange_gradients", out_shapes, plans)

    grads, deltas, new_m, new_v = {}, {}, {}, {}

    def update(n, g):
        w2d = shard[n].reshape(-1, shard[n].shape[-1])
        outs = _adam_update(w2d, g.reshape((N_DEV,) + w2d.shape), mom_m[n].reshape(w2d.shape),
                            mom_v[n].reshape(w2d.shape), "update_" + n)
        grads[n], deltas[n], new_m[n], new_v[n] = (o.reshape(shard[n].shape) for o in outs)

    update('w_in', slots_w_in)
    update('w_out', slots_w_out)
    update('w_pw2_c', slots_w_pw2)

    gsum = dict(zip(small_names, _unpack(_sum_slots(small_slots, "sum_small_gradients"), small_shapes)))
    for n in CHANNEL_SHARDED:
        width = shard[n].shape[2]
        gsum[n] = lax.dynamic_slice_in_dim(gsum[n], me * width, width, axis=2)
    outs = _adam_update(_pack([shard[n] for n in small_names]), _pack([gsum[n] for n in small_names]),
                        _pack([mom_m[n] for n in small_names]), _pack([mom_v[n] for n in small_names]), "update_small")
    own_shapes = [shard[n].shape for n in small_names]
    _, d_small, m_small, v_small = (_unpack(o, own_shapes) for o in outs)
    for j, n in enumerate(small_names):
        grads[n], deltas[n], new_m[n], new_v[n] = gsum[n], d_small[j], m_small[j], v_small[j]

    dmod_all = _unpack(small_slots, small_shapes, lead=(N_DEV,))[small_names.index('b_ada')]
    dmod_cols = lax.dynamic_slice_in_dim(dmod_all, me * ada_cols, ada_cols, axis=2).transpose(1, 0, 2)
    grads['w_ada'], deltas['w_ada'], new_m['w_ada'], new_v['w_ada'] = _ada_update(
        c_act.T, dmod_cols, w_ada, m_w_ada, v_w_ada)

    return (loss, grad_x, *[grads[n] for n in WEIGHTS], *[deltas[n] for n in WEIGHTS],
            *[new_m[n] for n in WEIGHTS], *[new_v[n] for n in WEIGHTS])
```

```python
import functools
import math

import jax
import jax.numpy as jnp
from jax import lax
from jax.experimental import pallas as pl
from jax.experimental.pallas import tpu as pltpu

F32 = jnp.float32
BF16 = jnp.bfloat16

N_DEV = 8
D_MODEL = 1024
GROUP = 256
D_IN = 12 * GROUP
N_LAYERS = 2
HALO = 32
SUB = 128
WIN = SUB + HALO
TOKEN_TILE = 512
REDUCE_TILE = 1024
EPS = 1e-6
VMEM_BYTES_V7X = 64 * 1024 * 1024
VMEM_LIMIT = VMEM_BYTES_V7X - 8 * 1024 * 1024

ADAM_LR = 0.001
ADAM_B1 = 0.9
ADAM_B2 = 0.999
ADAM_EPS = 1e-08
ADAM_WD = 0.01
ADAM_STEP = 10

A_B, A_C, A_X, A_G, B_P, B_G, C_A, C_GL, C_G, D_U, D_V, D_G = range(12)
V_PSCALE, V_BDW, V_LNGC, V_LNBC, V_BPW2, V_LNGD, V_LNBD = range(7)
S_WCONV, S_PSCALE, S_BDW, S_LNGC, S_LNBC, S_BPW2, S_LNGD, S_LNBD, S_WDW = 0, 3, 4, 5, 6, 7, 8, 9, 16
N_SUMS = 64
CONV_A = 3
CONV_C = 31

WEIGHTS = ('norm_g', 'w_ada', 'b_ada', 'w_in', 'w_conv_a', 'w_pool', 'pool_scale', 'w_dw_c', 'b_dw_c', 'ln_g_c',
           'ln_b_c', 'w_pw2_c', 'b_pw2_c', 'ln_g_d', 'ln_b_d', 'w_s_d', 'b_s_d', 'w_out', 'final_g')
REPLICATED = ('norm_g', 'b_ada', 'w_pool', 'pool_scale', 'b_dw_c', 'ln_g_c', 'ln_b_c', 'b_pw2_c', 'ln_g_d', 'ln_b_d',
              'w_s_d', 'b_s_d', 'final_g')
CHANNEL_SHARDED = ('w_conv_a', 'w_dw_c')


def _params(semantics, vmem=VMEM_LIMIT):
    return pltpu.CompilerParams(dimension_semantics=semantics, vmem_limit_bytes=vmem)


def _cols(g):
    return slice(g * GROUP, (g + 1) * GROUP)


def _full(shape):
    return pl.BlockSpec(shape, lambda *_: (0,) * len(shape))


def _silu(x):
    s = jax.nn.sigmoid(x)
    return x * s, s


def _dsilu(x, s):
    return s * (1.0 + x * (1.0 - s))


_GELU_C0 = math.sqrt(2.0 / math.pi)
_GELU_C1 = 0.044715


def _gelu(x):
    th = jnp.tanh(_GELU_C0 * (x + _GELU_C1 * (x * x * x)))
    return 0.5 * x * (1.0 + th), th


def _dgelu(x, th):
    return 0.5 * (1.0 + th) + 0.5 * x * (1.0 - th * th) * (_GELU_C0 * (1.0 + 3.0 * _GELU_C1 * (x * x)))


def _layer_norm(x):
    mu = jnp.mean(x, axis=-1, keepdims=True)
    xc = x - mu
    rstd = lax.rsqrt(jnp.mean(xc * xc, axis=-1, keepdims=True) + EPS)
    return xc * rstd, rstd


def _layer_norm_bwd(dn, n, rstd):
    return rstd * (dn - jnp.mean(dn, axis=-1, keepdims=True) - n * jnp.mean(dn * n, axis=-1, keepdims=True))


def _shift_rows(a, k):
    k = k % a.shape[0]
    return a if k == 0 else pltpu.roll(a, k, 0)


def _row_sum8(a):
    s = a[0:8]
    for m in range(1, a.shape[0] // 8):
        s = s + a[8 * m:8 * m + 8]
    return s


def _lane():
    return lax.broadcasted_iota(jnp.int32, (SUB, GROUP), 1)


def _by_quarter(lane, parts):
    return jnp.where(lane < 64, parts[0], jnp.where(lane < 128, parts[1], jnp.where(lane < 192, parts[2], parts[3])))


def _conv_inputs(z_ref, rows):
    def f(g):
        return z_ref[rows, _cols(g)].astype(F32)
    return f(A_C) * f(A_X), f(B_P), f(C_A) * jax.nn.sigmoid(f(C_GL))


def _fill_past(past_ref, zh_ref, zm_ref, is_first, tile):
    parts = _conv_inputs(zh_ref, slice(None))
    for n, a in enumerate(parts):
        past_ref[0:HALO, _cols(n)] = jnp.where(is_first, 0.0, a)

    def body(j, carry):
        r0 = pl.multiple_of(j * SUB, SUB)
        for n, a in enumerate(_conv_inputs(zm_ref, pl.ds(r0, SUB))):
            past_ref[pl.ds(r0 + HALO, SUB), _cols(n)] = a
        return carry

    lax.fori_loop(0, tile // SUB, body, 0)


def _short_conv_taps(qw):
    return [_shift_rows(qw, CONV_A - 1 - k)[HALO:WIN] for k in range(CONV_A)]


def _window_sums(pw, lane):
    s2 = pw + _shift_rows(pw, 1)
    s4 = s2 + _shift_rows(s2, 2)
    s8 = s4 + _shift_rows(s4, 4)
    s16 = s8 + _shift_rows(s8, 8)
    return _by_quarter(lane, [s[HALO:WIN] for s in (s2, s4, s8, s16)])


def _inv_count(lane, t_first):
    width = _by_quarter(lane, [2.0, 4.0, 8.0, 16.0])
    t = lax.broadcasted_iota(jnp.int32, (SUB, GROUP), 0) + t_first
    return 1.0 / jnp.minimum((t + 1).astype(F32), width)


def _forward_window_sums(ew, lane):
    n = ew.shape[0]
    f2 = ew + _shift_rows(ew, n - 1)
    f4 = f2 + _shift_rows(f2, n - 2)
    f8 = f4 + _shift_rows(f4, n - 4)
    f16 = f8 + _shift_rows(f8, n - 8)
    return _by_quarter(lane, [f[0:SUB] for f in (f2, f4, f8, f16)])


def _mixers_forward(zc, win, t_first, wc_ref, wdw_ref, vec_ref, wp_ref, w2_ref, ws_ref, bs_ref):
    lane = _lane()

    def vec(n):
        return vec_ref[n:n + 1, :]

    taps = _short_conv_taps(win(0))
    o_a = wc_ref[0:1, :] * taps[0] + wc_ref[1:2, :] * taps[1] + wc_ref[2:3, :] * taps[2]
    a_b, a_g = zc(A_B), zc(A_G)
    sg_a, s_a = _silu(a_g)
    y_a = a_b * o_a * sg_a

    pw = win(1)
    ic = _inv_count(lane, t_first)
    pooled = _window_sums(pw, lane) * ic - pw[HALO:WIN]
    pooled_b = pooled.astype(BF16)
    y0_b = jnp.dot(pooled_b, wp_ref[...], preferred_element_type=F32)
    b_g = zc(B_G)
    sg_b, s_b = _silu(b_g)
    y_b = y0_b * vec(V_PSCALE) * sg_b

    hw = win(2)
    o_c = wdw_ref[CONV_C - 1:CONV_C, :] * hw[HALO:WIN] + vec(V_BDW)
    for k in range(CONV_C - 1):
        o_c = o_c + wdw_ref[k:k + 1, :] * _shift_rows(hw, CONV_C - 1 - k)[HALO:WIN]
    n_c, rstd_c = _layer_norm(o_c)
    ln_c = n_c * vec(V_LNGC) + vec(V_LNBC)
    sl_c, ssl_c = _silu(ln_c)
    sl_b = sl_c.astype(BF16)
    yc = jnp.dot(sl_b, w2_ref[...], preferred_element_type=F32) + vec(V_BPW2)
    c_g = zc(C_G)
    sg_c, s_c = _silu(c_g)
    y_c = yc * sg_c

    d_u, d_v, d_g = zc(D_U), zc(D_V), zc(D_G)
    u, th_u = _gelu(d_u)
    gv, th_v = _gelu(d_v)
    n_d, rstd_d = _layer_norm(gv)
    v_b = (n_d * vec(V_LNGD) + vec(V_LNBD)).astype(BF16)
    r = jnp.dot(ws_ref[...], v_b, preferred_element_type=F32)
    mixed = _by_quarter(lane, [r[h * SUB:(h + 1) * SUB] for h in range(4)]) + bs_ref[...]
    sg_d, s_d = _silu(d_g)
    y_d = u * mixed * sg_d

    saved = dict(lane=lane, taps=taps, o_a=o_a, a_b=a_b, a_g=a_g, sg_a=sg_a, s_a=s_a,
                 ic=ic, pooled_b=pooled_b, y0_b=y0_b, b_g=b_g, sg_b=sg_b, s_b=s_b,
                 hw=hw, n_c=n_c, rstd_c=rstd_c, ln_c=ln_c, ssl_c=ssl_c, sl_b=sl_b, yc=yc, c_g=c_g, sg_c=sg_c, s_c=s_c,
                 d_u=d_u, d_v=d_v, d_g=d_g, u=u, th_u=th_u, th_v=th_v, n_d=n_d, rstd_d=rstd_d, v_b=v_b, mixed=mixed,
                 sg_d=sg_d, s_d=s_d)
    return (y_a, y_b, y_c, y_d), saved


def _in_proj(x, gs, shift, w_in_b, tile):
    n_tok = x.shape[0]

    def body(x_ref, gs_ref, sh_ref, w_ref, h_ref, z_ref):
        xv = x_ref[...]
        r = lax.rsqrt(jnp.mean(xv * xv, axis=-1, keepdims=True) + EPS)
        h = ((xv * r) * gs_ref[...] + sh_ref[...]).astype(BF16)
        h_ref[...] = h
        for j in range(D_IN // D_MODEL):
            cs = slice(j * D_MODEL, (j + 1) * D_MODEL)
            z_ref[:, cs] = jnp.dot(h, w_ref[:, cs], preferred_element_type=F32).astype(BF16)

    return pl.pallas_call(
        body, name="in_proj", grid=(n_tok // tile,),
        in_specs=[pl.BlockSpec((tile, D_MODEL), lambda i: (i, 0)), _full((1, D_MODEL)), _full((1, D_MODEL)),
                  _full((D_MODEL, D_IN))],
        out_specs=[pl.BlockSpec((tile, D_MODEL), lambda i: (i, 0)), pl.BlockSpec((tile, D_IN), lambda i: (i, 0))],
        out_shape=[jax.ShapeDtypeStruct((n_tok, D_MODEL), BF16), jax.ShapeDtypeStruct((n_tok, D_IN), BF16)],
        compiler_params=_params(("parallel",)),
    )(x, gs, shift, w_in_b)


def _small_specs(with_transposes):
    specs = [_full((8, GROUP)), _full((HALO, GROUP)), _full((16, GROUP)), _full((GROUP, GROUP)), _full((GROUP, GROUP)),
             _full((4 * SUB, SUB)), _full((SUB, GROUP))]
    if with_transposes:
        specs += [_full((GROUP, GROUP)), _full((GROUP, GROUP)), _full((4 * SUB, SUB))]
    return specs


def _mix_out(z, x, gate, small, w_out_b, tile, ride=None):
    n_tok = x.shape[0]
    per_halo = tile // HALO

    def body(zm_ref, zh_ref, x_ref, gate_ref, wc_ref, wdw_ref, vec_ref, wp_ref, w2_ref, ws_ref, bs_ref, wout_ref,
             xo_ref, past_ref):
        i = pl.program_id(0)
        _fill_past(past_ref, zh_ref, zm_ref, i == 0, tile)

        def step(j, carry):
            r0 = pl.multiple_of(j * SUB, SUB)
            rows = pl.ds(r0, SUB)
            ys, _ = _mixers_forward(
                lambda g: zm_ref[rows, _cols(g)].astype(F32), lambda n: past_ref[pl.ds(r0, WIN), _cols(n)],
                i * tile + r0, wc_ref, wdw_ref, vec_ref, wp_ref, w2_ref, ws_ref, bs_ref)
            ycat = jnp.concatenate([y.astype(BF16) for y in ys], axis=1)
            y = jnp.dot(ycat, wout_ref[...], preferred_element_type=F32)
            xo_ref[rows, :] = x_ref[rows, :] + gate_ref[...] * y
            return carry

        lax.fori_loop(0, tile // SUB, step, 0)

    (x_next,), rode = _tiled_call(
        body, (z, z, x, gate, *small, w_out_b), name="mix_out", grid=(n_tok // tile,),
        in_specs=[pl.BlockSpec((tile, D_IN), lambda i: (i, 0)),
                  pl.BlockSpec((HALO, D_IN), lambda i: (jnp.maximum(i * per_halo - 1, 0), 0)),
                  pl.BlockSpec((tile, D_MODEL), lambda i: (i, 0)), _full((1, D_MODEL)),
                  *_small_specs(False), _full((D_MODEL, D_MODEL))],
        out_specs=[pl.BlockSpec((tile, D_MODEL), lambda i: (i, 0))],
        out_shape=[jax.ShapeDtypeStruct((n_tok, D_MODEL), F32)],
        scratch_shapes=[pltpu.VMEM((tile + HALO, 3 * GROUP), F32)], ride=ride)
    return x_next, rode


def _mix_bwd(z, dx_next, gate, small, small_t, w_out_b, tile, ride=None):
    n_tok = z.shape[0]
    n_tiles = n_tok // tile
    per_halo = tile // HALO

    def tile_of(i):
        return n_tiles - 1 - i

    def body(zm_ref, zh_ref, dxn_ref, gate_ref, wc_ref, wdw_ref, vec_ref, wp_ref, w2_ref, ws_ref, bs_ref,
             wpt_ref, w2t_ref, wst_ref, wout_ref,
             dz_ref, ycat_ref, sums_ref, dwp_ref, dw2_ref, dws_ref, dbs_ref,
             past_ref, future_ref, acc_ref):
        i = pl.program_id(0)
        t = tile_of(i)

        @pl.when(i == 0)
        def _():
            acc_ref[...] = jnp.zeros_like(acc_ref)
            dwp_ref[...] = jnp.zeros_like(dwp_ref)
            dw2_ref[...] = jnp.zeros_like(dw2_ref)
            dws_ref[...] = jnp.zeros_like(dws_ref)
            dbs_ref[...] = jnp.zeros_like(dbs_ref)
            future_ref[tile:tile + HALO, :] = jnp.zeros((HALO, 3 * GROUP), F32)

        _fill_past(past_ref, zh_ref, zm_ref, t == 0, tile)

        def vec(n):
            return vec_ref[n:n + 1, :]

        def step(jj, carry):
            j = tile // SUB - 1 - jj
            r0 = pl.multiple_of(j * SUB, SUB)
            rows = pl.ds(r0, SUB)

            def zc(g):
                return zm_ref[rows, _cols(g)].astype(F32)

            def add(n, a):
                acc_ref[n] = acc_ref[n] + _row_sum8(a)

            def put(g, a):
                dz_ref[rows, _cols(g)] = a.astype(BF16)

            def future_window(n, a):
                future_ref[rows, _cols(n)] = a
                return future_ref[pl.ds(r0, WIN), _cols(n)]

            ys, s = _mixers_forward(zc, lambda n: past_ref[pl.ds(r0, WIN), _cols(n)], t * tile + r0,
                                    wc_ref, wdw_ref, vec_ref, wp_ref, w2_ref, ws_ref, bs_ref)
            for n, y in enumerate(ys):
                ycat_ref[rows, _cols(n)] = y.astype(BF16)
            lane = s["lane"]
            dycat = lax.dot_general((dxn_ref[rows, :] * gate_ref[...]).astype(BF16), wout_ref[...],
                                    (((1,), (1,)), ((), ())), preferred_element_type=F32)

            dy = dycat[:, _cols(0)]
            put(A_B, dy * s["o_a"] * s["sg_a"])
            put(A_G, dy * s["a_b"] * s["o_a"] * _dsilu(s["a_g"], s["s_a"]))
            do = dy * s["a_b"] * s["sg_a"]
            for k in range(CONV_A):
                add(S_WCONV + k, do * s["taps"][k])
            dow = future_window(0, do)
            dq = wc_ref[CONV_A - 1:CONV_A, :] * dow[0:SUB]
            for k in range(CONV_A - 1):
                dq = dq + wc_ref[k:k + 1, :] * _shift_rows(dow, WIN - (CONV_A - 1 - k))[0:SUB]
            put(A_C, dq * zc(A_X))
            put(A_X, dq * zc(A_C))

            dy = dycat[:, _cols(1)]
            put(B_G, dy * (s["y0_b"] * vec(V_PSCALE)) * _dsilu(s["b_g"], s["s_b"]))
            dyb = dy * s["sg_b"]
            add(S_PSCALE, dyb * s["y0_b"])
            dpw_b = (dyb * vec(V_PSCALE)).astype(BF16)
            dwp_ref[...] += lax.dot_general(s["pooled_b"], dpw_b, (((0,), (0,)), ((), ())), preferred_element_type=F32)
            dpooled = jnp.dot(dpw_b, wpt_ref[...], preferred_element_type=F32)
            ew = future_window(1, dpooled * s["ic"])
            put(B_P, _forward_window_sums(ew, lane) - dpooled)

            dy = dycat[:, _cols(2)]
            put(C_G, dy * s["yc"] * _dsilu(s["c_g"], s["s_c"]))
            dyc = dy * s["sg_c"]
            add(S_BPW2, dyc)
            dyc_b = dyc.astype(BF16)
            dw2_ref[...] += lax.dot_general(s["sl_b"], dyc_b, (((0,), (0,)), ((), ())), preferred_element_type=F32)
            dln = jnp.dot(dyc_b, w2t_ref[...], preferred_element_type=F32) * _dsilu(s["ln_c"], s["ssl_c"])
            add(S_LNGC, dln * s["n_c"])
            add(S_LNBC, dln)
            do = _layer_norm_bwd(dln * vec(V_LNGC), s["n_c"], s["rstd_c"])
            add(S_BDW, do)
            hw = s["hw"]
            for k in range(CONV_C):
                add(S_WDW + k, do * _shift_rows(hw, CONV_C - 1 - k)[HALO:WIN])
            dow = future_window(2, do)
            dhc = wdw_ref[CONV_C - 1:CONV_C, :] * dow[0:SUB]
            for k in range(CONV_C - 1):
                dhc = dhc + wdw_ref[k:k + 1, :] * _shift_rows(dow, WIN - (CONV_C - 1 - k))[0:SUB]
            c_a = zc(C_A)
            sgl = jax.nn.sigmoid(zc(C_GL))
            put(C_A, dhc * sgl)
            put(C_GL, dhc * c_a * sgl * (1.0 - sgl))

            dy = dycat[:, _cols(3)]
            put(D_G, dy * s["u"] * s["mixed"] * _dsilu(s["d_g"], s["s_d"]))
            put(D_U, dy * s["mixed"] * s["sg_d"] * _dgelu(s["d_u"], s["th_u"]))
            dmixed = dy * s["u"] * s["sg_d"]
            dbs_ref[...] += dmixed
            by_head = jnp.concatenate(
                [jnp.where((lane >= 64 * h) & (lane < 64 * h + 64), dmixed, 0.0) for h in range(4)], axis=0).astype(BF16)
            dws_ref[...] += lax.dot_general(by_head, s["v_b"], (((1,), (1,)), ((), ())), preferred_element_type=F32)
            rv = jnp.dot(wst_ref[...], dmixed.astype(BF16), preferred_element_type=F32)
            dv = _by_quarter(lane, [rv[h * SUB:(h + 1) * SUB] for h in range(4)])
            add(S_LNGD, dv * s["n_d"])
            add(S_LNBD, dv)
            dgv = _layer_norm_bwd(dv * vec(V_LNGD), s["n_d"], s["rstd_d"])
            put(D_V, dgv * _dgelu(s["d_v"], s["th_v"]))
            return carry

        lax.fori_loop(0, tile // SUB, step, 0)
        future_ref[tile:tile + HALO, :] = future_ref[0:HALO, :]

        @pl.when(i == n_tiles - 1)
        def _():
            for n in range(N_SUMS):
                sums_ref[n:n + 1, :] = jnp.sum(acc_ref[n], axis=0, keepdims=True)

    return _tiled_call(
        body, (z, z, dx_next, gate, *small, *small_t, w_out_b), name="mix_bwd", grid=(n_tiles,),
        in_specs=[pl.BlockSpec((tile, D_IN), lambda i: (tile_of(i), 0)),
                  pl.BlockSpec((HALO, D_IN), lambda i: (jnp.maximum(tile_of(i) * per_halo - 1, 0), 0)),
                  pl.BlockSpec((tile, D_MODEL), lambda i: (tile_of(i), 0)), _full((1, D_MODEL)),
                  *_small_specs(True), _full((D_MODEL, D_MODEL))],
        out_specs=[pl.BlockSpec((tile, D_IN), lambda i: (tile_of(i), 0)),
                   pl.BlockSpec((tile, D_MODEL), lambda i: (tile_of(i), 0)),
                   _full((N_SUMS, GROUP)), _full((GROUP, GROUP)), _full((GROUP, GROUP)), _full((4 * SUB, SUB)),
                   _full((SUB, GROUP))],
        out_shape=[jax.ShapeDtypeStruct((n_tok, D_IN), BF16), jax.ShapeDtypeStruct((n_tok, D_MODEL), BF16),
                   jax.ShapeDtypeStruct((N_SUMS, GROUP), F32), jax.ShapeDtypeStruct((GROUP, GROUP), F32),
                   jax.ShapeDtypeStruct((GROUP, GROUP), F32), jax.ShapeDtypeStruct((4 * SUB, SUB), F32),
                   jax.ShapeDtypeStruct((SUB, GROUP), F32)],
        scratch_shapes=[pltpu.VMEM((tile + HALO, 3 * GROUP), F32), pltpu.VMEM((tile + HALO, 3 * GROUP), F32),
                        pltpu.VMEM((N_SUMS, 8, GROUP), F32)], ride=ride)


def _norm_bwd(x, dz, dx_next, gs, w_in_b, tile, ride=None):
    n_tok = x.shape[0]
    n_tiles = n_tok // tile

    def body(x_ref, dz_ref, dxn_ref, gs_ref, w_ref, dx_ref, dsh_ref, dgs_ref, acc_ref):
        i = pl.program_id(0)

        @pl.when(i == 0)
        def _():
            acc_ref[...] = jnp.zeros_like(acc_ref)

        dh = lax.dot_general(dz_ref[...], w_ref[...], (((1,), (1,)), ((), ())), preferred_element_type=F32)
        xv = x_ref[...]
        r = lax.rsqrt(jnp.mean(xv * xv, axis=-1, keepdims=True) + EPS)
        xn = xv * r
        acc_ref[0] = acc_ref[0] + _row_sum8(dh)
        acc_ref[1] = acc_ref[1] + _row_sum8(dh * xn)
        dxn = dh * gs_ref[...]
        dx_ref[...] = dxn_ref[...] + r * (dxn - xn * jnp.mean(dxn * xn, axis=-1, keepdims=True))

        @pl.when(i == n_tiles - 1)
        def _():
            dsh_ref[...] = jnp.sum(acc_ref[0], axis=0, keepdims=True)
            dgs_ref[...] = jnp.sum(acc_ref[1], axis=0, keepdims=True)

    return _tiled_call(
        body, (x, dz, dx_next, gs, w_in_b), name="norm_bwd", grid=(n_tiles,),
        in_specs=[pl.BlockSpec((tile, D_MODEL), lambda i: (i, 0)), pl.BlockSpec((tile, D_IN), lambda i: (i, 0)),
                  pl.BlockSpec((tile, D_MODEL), lambda i: (i, 0)), _full((1, D_MODEL)), _full((D_MODEL, D_IN))],
        out_specs=[pl.BlockSpec((tile, D_MODEL), lambda i: (i, 0)), _full((1, D_MODEL)), _full((1, D_MODEL))],
        out_shape=[jax.ShapeDtypeStruct((n_tok, D_MODEL), F32), jax.ShapeDtypeStruct((1, D_MODEL), F32),
                   jax.ShapeDtypeStruct((1, D_MODEL), F32)],
        scratch_shapes=[pltpu.VMEM((2, 8, D_MODEL), F32)], ride=ride)


def _loss_head(x, final_g, target, tile):
    n_tok = x.shape[0]
    n_tiles = n_tok // tile

    def body(x_ref, g_ref, t_ref, dx_ref, loss_ref, dg_ref, acc_ref):
        i = pl.program_id(0)

        @pl.when(i == 0)
        def _():
            acc_ref[...] = jnp.zeros_like(acc_ref)

        xv = x_ref[...]
        r = lax.rsqrt(jnp.mean(xv * xv, axis=-1, keepdims=True) + EPS)
        xn = xv * r
        err = xn * g_ref[...] - t_ref[...]
        acc_ref[0] = acc_ref[0] + _row_sum8(err * err)
        dy = err * (1.0 / D_MODEL)
        acc_ref[1] = acc_ref[1] + _row_sum8(dy * xn)
        a = dy * g_ref[...]
        dx_ref[...] = r * (a - xn * jnp.mean(a * xn, axis=-1, keepdims=True))

        @pl.when(i == n_tiles - 1)
        def _():
            loss_ref[...] = jnp.full((8, 128), 0.5 / D_MODEL, F32) * jnp.sum(acc_ref[0])
            dg_ref[...] = jnp.sum(acc_ref[1], axis=0, keepdims=True)

    return pl.pallas_call(
        body, name="loss_head", grid=(n_tiles,),
        in_specs=[pl.BlockSpec((tile, D_MODEL), lambda i: (i, 0)), _full((1, D_MODEL)),
                  pl.BlockSpec((tile, D_MODEL), lambda i: (i, 0))],
        out_specs=[pl.BlockSpec((tile, D_MODEL), lambda i: (i, 0)), _full((8, 128)), _full((1, D_MODEL))],
        out_shape=[jax.ShapeDtypeStruct((n_tok, D_MODEL), F32), jax.ShapeDtypeStruct((8, 128), F32),
                   jax.ShapeDtypeStruct((1, D_MODEL), F32)],
        scratch_shapes=[pltpu.VMEM((2, 8, D_MODEL), F32)],
        compiler_params=_params(("arbitrary",)),
    )(x, final_g, target)


def _tokens_matmul(a, b, name):
    n_tok, ka = a.shape
    nb = b.shape[1]
    tk = min(REDUCE_TILE, n_tok)
    cb = min(D_MODEL, nb)

    def body(a_ref, b_ref, o_ref):
        @pl.when(pl.program_id(1) == 0)
        def _():
            o_ref[...] = jnp.zeros_like(o_ref)

        o_ref[...] += lax.dot_general(a_ref[...], b_ref[...].astype(BF16), (((0,), (0,)), ((), ())),
                                      preferred_element_type=F32)

    return pl.pallas_call(
        body, name=name, grid=(nb // cb, n_tok // tk),
        in_specs=[pl.BlockSpec((tk, ka), lambda j, i: (i, 0)), pl.BlockSpec((tk, cb), lambda j, i: (i, j))],
        out_specs=pl.BlockSpec((ka, cb), lambda j, i: (0, j)),
        out_shape=jax.ShapeDtypeStruct((ka, nb), F32),
        compiler_params=_params(("parallel", "arbitrary")),
    )(a, b)


def _out_proj_grads(m, w_out_b, gate):
    rb = 256
    n_blocks = D_MODEL // rb

    def body(m_ref, w_ref, gate_ref, dw_ref, dgate_ref, acc_ref):
        i = pl.program_id(0)

        @pl.when(i == 0)
        def _():
            acc_ref[...] = jnp.zeros_like(acc_ref)

        mv = m_ref[...]
        dw_ref[...] = mv * gate_ref[...]
        acc_ref[...] += _row_sum8(mv * w_ref[...].astype(F32))

        @pl.when(i == n_blocks - 1)
        def _():
            dgate_ref[...] = jnp.sum(acc_ref[...], axis=0, keepdims=True)

    return pl.pallas_call(
        body, name="out_proj_grads", grid=(n_blocks,),
        in_specs=[pl.BlockSpec((rb, D_MODEL), lambda i: (i, 0)), pl.BlockSpec((rb, D_MODEL), lambda i: (i, 0)),
                  _full((1, D_MODEL))],
        out_specs=[pl.BlockSpec((rb, D_MODEL), lambda i: (i, 0)), _full((1, D_MODEL))],
        out_shape=[jax.ShapeDtypeStruct((D_MODEL, D_MODEL), F32), jax.ShapeDtypeStruct((1, D_MODEL), F32)],
        scratch_shapes=[pltpu.VMEM((8, D_MODEL), F32)],
        compiler_params=_params(("arbitrary",)),
    )(m, w_out_b, gate)


def _modulation_columns(c_all, w_ada, b_cols):
    cols = w_ada.shape[2]

    def body(c_ref, w_ref, b_ref, ca_ref, mod_ref):
        ca, _ = _silu(c_ref[...])
        ca_ref[...] = ca
        for l in range(N_LAYERS):
            mod_ref[l] = jnp.dot(ca, w_ref[l], precision=lax.Precision.HIGHEST, preferred_element_type=F32) + b_ref[l:l + 1, :]

    return pl.pallas_call(
        body, name="modulation_columns",
        out_shape=[jax.ShapeDtypeStruct((N_DEV, D_MODEL), F32), jax.ShapeDtypeStruct((N_LAYERS, N_DEV, cols), F32)],
        compiler_params=pltpu.CompilerParams(vmem_limit_bytes=VMEM_LIMIT),
    )(c_all, w_ada, b_cols)


def _adam(w, g, m, v):
    m2 = ADAM_B1 * m + (1.0 - ADAM_B1) * g
    v2 = ADAM_B2 * v + (1.0 - ADAM_B2) * (g * g)
    m_hat = m2 / (1.0 - ADAM_B1 ** ADAM_STEP)
    v_hat = v2 / (1.0 - ADAM_B2 ** ADAM_STEP)
    return -ADAM_LR * (m_hat / (jnp.sqrt(v_hat) + ADAM_EPS) + ADAM_WD * w), m2, v2


def _row_block(rows, cols, slots):
    target = max(8, (1 << 19) // (cols * max(slots, 1)))
    rb = rows
    while rb > target and rb % 2 == 0 and (rb // 2) % 8 == 0:
        rb //= 2
    return rb


def _sum_slots(slots, name):
    _, rows, cols = slots.shape
    rb = _row_block(rows, cols, N_DEV)

    def body(s_ref, g_ref):
        g = s_ref[0]
        for q in range(1, N_DEV):
            g = g + s_ref[q]
        g_ref[...] = g

    return pl.pallas_call(
        body, name=name, grid=(rows // rb,),
        in_specs=[pl.BlockSpec((N_DEV, rb, cols), lambda i: (0, i, 0))],
        out_specs=pl.BlockSpec((rb, cols), lambda i: (i, 0)),
        out_shape=jax.ShapeDtypeStruct((rows, cols), F32),
        compiler_params=_params(("parallel",)),
    )(slots)


def _adam_update(w, g, m, v, name):
    rows, cols = w.shape
    slotted = g.ndim == 3
    rb = _row_block(rows, cols, N_DEV if slotted else 1)

    def body(w_ref, g_ref, m_ref, v_ref, go_ref, d_ref, mo_ref, vo_ref):
        if slotted:
            gv = g_ref[0]
            for q in range(1, N_DEV):
                gv = gv + g_ref[q]
        else:
            gv = g_ref[...]
        go_ref[...] = gv
        d_ref[...], mo_ref[...], vo_ref[...] = _adam(w_ref[...], gv, m_ref[...], v_ref[...])

    blk = pl.BlockSpec((rb, cols), lambda i: (i, 0))
    g_blk = pl.BlockSpec((N_DEV, rb, cols), lambda i: (0, i, 0)) if slotted else blk
    return pl.pallas_call(
        body, name=name, grid=(rows // rb,),
        in_specs=[blk, g_blk, blk, blk], out_specs=[blk] * 4,
        out_shape=[jax.ShapeDtypeStruct((rows, cols), F32)] * 4,
        compiler_params=_params(("parallel",)),
    )(w, g, m, v)


def _ada_update(ca_t, dmod_cols, w, m, v):
    _, rows, cols = w.shape

    def body(ca_ref, dm_ref, w_ref, m_ref, v_ref, g_ref, d_ref, mo_ref, vo_ref):
        g = ca_ref[:, 0:1] * dm_ref[0, 0:1, :]
        for b in range(1, N_DEV):
            g = g + ca_ref[:, b:b + 1] * dm_ref[0, b:b + 1, :]
        g_ref[0] = g
        d_ref[0], mo_ref[0], vo_ref[0] = _adam(w_ref[0], g, m_ref[0], v_ref[0])

    blk = pl.BlockSpec((1, rows, cols), lambda l: (l, 0, 0))
    return pl.pallas_call(
        body, name="ada_update", grid=(N_LAYERS,),
        in_specs=[_full((rows, N_DEV)), pl.BlockSpec((1, N_DEV, cols), lambda l: (l, 0, 0)), blk, blk, blk],
        out_specs=[blk] * 4, out_shape=[jax.ShapeDtypeStruct(w.shape, F32)] * 4,
        compiler_params=_params(("parallel",)),
    )(ca_t, dmod_cols, w, m, v)


def _exchange_sems(n):
    return [pltpu.SemaphoreType.DMA((n, N_DEV - 1)), pltpu.SemaphoreType.DMA((n, N_DEV - 1)),
            pltpu.SemaphoreType.DMA((n,))]


def _exchange_copies(plans, srcs, outs, sems, receiving):
    send_sems, recv_sems, local_sems = sems
    x, y, c = lax.axis_index("x"), lax.axis_index("y"), lax.axis_index("c")
    me = 4 * x + 2 * y + c

    def remote(i, k, incoming):
        _, o, send, land = plans[i]
        px = 1 - x if k & 4 else x
        py = 1 - y if k & 2 else y
        pc = 1 - c if k & 1 else c
        p = 4 * px + 2 * py + pc
        return pltpu.make_async_remote_copy(
            src_ref=send(srcs[i], p), dst_ref=land(outs[o], p if incoming else me),
            send_sem=send_sems.at[i, k - 1], recv_sem=recv_sems.at[i, k - 1],
            device_id=(px, py, pc), device_id_type=pl.DeviceIdType.MESH)

    pairs = [(i, k) for k in range(1, N_DEV) for i in range(len(plans))]
    local = [pltpu.make_async_copy(send(srcs[i], me), land(outs[o], me), local_sems.at[i])
             for i, (_, o, send, land) in enumerate(plans)]
    return local, [remote(i, k, False) for i, k in pairs], [remote(i, k, True) for i, k in pairs] if receiving else []


def _exchange_start(plans, srcs, outs, sems):
    local, outgoing, _ = _exchange_copies(plans, srcs, outs, sems, receiving=False)
    for cp in local + outgoing:
        cp.start()


def _exchange_wait(plans, srcs, outs, sems):
    local, outgoing, incoming = _exchange_copies(plans, srcs, outs, sems, receiving=True)
    for cp in incoming:
        cp.wait_recv()
    for cp in outgoing:
        cp.wait_send()
    for cp in local:
        cp.wait()


def _exchange(name, ride):
    out_shapes, plans = ride
    n = len(plans)
    hbm = pl.BlockSpec(memory_space=pltpu.HBM)

    def body(*refs):
        srcs, outs, sems = refs[:n], refs[n:n + len(out_shapes)], refs[n + len(out_shapes):]
        _exchange_start(plans, srcs, outs, sems)
        _exchange_wait(plans, srcs, outs, sems)

    return pl.pallas_call(
        body, name=name, in_specs=[hbm] * n, out_specs=[hbm] * len(out_shapes), out_shape=list(out_shapes),
        scratch_shapes=_exchange_sems(n),
    )(*[p[0] for p in plans])


def _tiled_call(body, args, *, name, grid, in_specs, out_specs, out_shape, scratch_shapes=(), ride=None):
    params = _params(("arbitrary",) * len(grid))
    if ride is None:
        return pl.pallas_call(body, name=name, grid=grid, in_specs=in_specs, out_specs=out_specs, out_shape=out_shape,
                              scratch_shapes=list(scratch_shapes), compiler_params=params)(*args), []
    shapes, plans = ride
    n_in, n_src, n_out, n_dst, n_scr = len(in_specs), len(plans), len(out_specs), len(shapes), len(scratch_shapes)
    hbm = pl.BlockSpec(memory_space=pltpu.HBM)

    def carrying(*refs):
        ins, srcs, refs = refs[:n_in], refs[n_in:n_in + n_src], refs[n_in + n_src:]
        outs, dsts, refs = refs[:n_out], refs[n_out:n_out + n_dst], refs[n_out + n_dst:]
        scratch, sems = refs[:n_scr], refs[n_scr:]
        ids = [pl.program_id(a) for a in range(len(grid))]
        first = functools.reduce(jnp.logical_and, [i == 0 for i in ids])
        last = functools.reduce(jnp.logical_and, [i == g - 1 for i, g in zip(ids, grid)])

        @pl.when(first)
        def _():
            _exchange_start(plans, srcs, dsts, sems)

        body(*ins, *outs, *scratch)

        @pl.when(last)
        def _():
            _exchange_wait(plans, srcs, dsts, sems)

    res = pl.pallas_call(
        carrying, name=name, grid=grid, in_specs=[*in_specs, *[hbm] * n_src], out_specs=[*out_specs, *[hbm] * n_dst],
        out_shape=[*out_shape, *shapes], scratch_shapes=[*scratch_shapes, *_exchange_sems(n_src)],
        compiler_params=params)(*args, *[p[0] for p in plans])
    return res[:n_out], res[n_out:]


def _tail(nd, idx):
    return (slice(None),) * (nd - 2) + idx


def _gather(a):
    return jax.ShapeDtypeStruct((N_DEV,) + a.shape, a.dtype), lambda s, p: s, lambda o, q: o.at[q]


def _gather_rows(a):
    r = a.shape[-2]
    return (jax.ShapeDtypeStruct(a.shape[:-2] + (N_DEV * r, a.shape[-1]), a.dtype), lambda s, p: s,
            lambda o, q: o.at[_tail(a.ndim, (pl.ds(pl.multiple_of(q * r, r), r), slice(None)))])


def _gather_cols(a):
    c = a.shape[-1]
    return (jax.ShapeDtypeStruct(a.shape[:-1] + (N_DEV * c,), a.dtype), lambda s, p: s,
            lambda o, q: o.at[_tail(a.ndim, (slice(None), pl.ds(pl.multiple_of(q * c, c), c)))])


def _scatter_rows(a):
    r = a.shape[0] // N_DEV
    return (jax.ShapeDtypeStruct((N_DEV, r, a.shape[1]), a.dtype),
            lambda s, p: s.at[pl.ds(pl.multiple_of(p * r, r), r), :], lambda o, q: o.at[q])


def _scatter_cols(a):
    c = a.shape[1] // N_DEV
    return (jax.ShapeDtypeStruct((N_DEV, a.shape[0], c), a.dtype),
            lambda s, p: s.at[:, pl.ds(pl.multiple_of(p * c, c), c)], lambda o, q: o.at[q])


def _plans(arrays, rules):
    shapes, plans = [], []
    for o, (a, rule) in enumerate(zip(arrays, rules)):
        shape, send, land = rule(a)
        shapes.append(shape)
        plans.append((a, o, send, land))
    return shapes, plans


def _pack(pieces):
    flat = []
    for a in pieces:
        f = a.reshape(-1)
        flat.append(jnp.pad(f, (0, (-f.shape[0]) % 128)))
    total = sum(f.shape[0] for f in flat)
    flat.append(jnp.zeros(((-total) % 1024,), F32))
    return jnp.concatenate(flat).reshape(-1, 128)


def _unpack(buf, shapes, lead=()):
    flat = buf.reshape(lead + (-1,))
    out, off = [], 0
    for s in shapes:
        n = math.prod(s)
        out.append(flat[..., off:off + n].reshape(lead + tuple(s)))
        off += n + (-n) % 128
    return out


def _pad_rows(a, rows):
    return jnp.pad(a, ((0, rows - a.shape[0]), (0, 0)))


VEC_NAMES = ('pool_scale', 'b_dw_c', 'ln_g_c', 'ln_b_c', 'b_pw2_c', 'ln_g_d', 'ln_b_d')
GATHERED = ('w_in', 'w_out', 'w_pw2_c', 'w_conv_a', 'w_dw_c')
GATHER_RULES = (_gather_cols, _gather_rows, _gather_rows, _gather, _gather)
SCATTER_RULES = (_scatter_cols, _scatter_rows, _scatter_rows)


def _weight_shards(shard, l):
    return [shard[n][l].astype(BF16) if n in ('w_in', 'w_out') else shard[n][l] for n in GATHERED]


def _layer_weights(shard, l, gathered):
    w_in_b, w_out_b, w_pw2, wconv_parts, wdw_parts = gathered
    wconv = wconv_parts.transpose(1, 0, 2).reshape(CONV_A, GROUP)
    wdw = wdw_parts.transpose(1, 0, 2).reshape(CONV_C, GROUP)
    wp = jnp.einsum('gcd,gh->gchd', shard['w_pool'][l], jnp.eye(4, dtype=F32)).reshape(GROUP, GROUP)
    ws = shard['w_s_d'][l] * jnp.tril(jnp.ones((SUB, SUB), F32))
    vec = jnp.stack([shard[n][l] for n in VEC_NAMES])
    small = (_pad_rows(wconv, 8), _pad_rows(wdw, HALO), _pad_rows(vec, 16), wp.astype(BF16), w_pw2.astype(BF16),
             ws.reshape(4 * SUB, SUB).astype(BF16), jnp.repeat(shard['b_s_d'][l].T, 64, axis=1))
    small_t = (wp.T.astype(BF16), w_pw2.T.astype(BF16), ws.transpose(0, 2, 1).reshape(4 * SUB, SUB).astype(BF16))
    return w_in_b, w_out_b, small, small_t


def kernel(x, c, norm_g, w_ada, b_ada, w_in, w_conv_a, w_pool, pool_scale, w_dw_c, b_dw_c, ln_g_c, ln_b_c, w_pw2_c, b_pw2_c, ln_g_d, ln_b_d, w_s_d, b_s_d, w_out, final_g, loss_target, m_norm_g, m_w_ada, m_b_ada, m_w_in, m_w_conv_a, m_w_pool, m_pool_scale, m_w_dw_c, m_b_dw_c, m_ln_g_c, m_ln_b_c, m_w_pw2_c, m_b_pw2_c, m_ln_g_d, m_ln_b_d, m_w_s_d, m_b_s_d, m_w_out, m_final_g, v_norm_g, v_w_ada, v_b_ada, v_w_in, v_w_conv_a, v_w_pool, v_pool_scale, v_w_dw_c, v_b_dw_c, v_ln_g_c, v_ln_b_c, v_w_pw2_c, v_b_pw2_c, v_ln_g_d, v_ln_b_d, v_w_s_d, v_b_s_d, v_w_out, v_final_g):
    given = dict(locals())
    shard = {n: given[n] for n in WEIGHTS}
    mom_m = {n: given['m_' + n] for n in WEIGHTS}
    mom_v = {n: given['v_' + n] for n in WEIGHTS}
    me = 4 * lax.axis_index("x") + 2 * lax.axis_index("y") + lax.axis_index("c")
    n_tok = x.shape[1]
    tile = min(TOKEN_TILE, n_tok)
    x0 = x.reshape(n_tok, D_MODEL)
    target = loss_target.reshape(n_tok, D_MODEL)
    ada_cols = w_ada.shape[2]

    c_all, *gathered = _exchange("gather_weights", _plans([c] + _weight_shards(shard, 0), (_gather,) + GATHER_RULES))
    layers = [_layer_weights(shard, 0, gathered)]

    b_cols = lax.dynamic_slice_in_dim(b_ada, me * ada_cols, ada_cols, axis=1)
    c_act, mod_cols = _modulation_columns(c_all.reshape(N_DEV, D_MODEL), w_ada, b_cols)
    (mod_all,) = _exchange("gather_modulation", _plans([mod_cols], [_gather]))
    mod = lax.dynamic_index_in_dim(mod_all, me, axis=2, keepdims=False)
    mod = mod.transpose(1, 0, 2).reshape(N_LAYERS, 3 * D_MODEL)
    shift, scale, gate = (mod[:, k * D_MODEL:(k + 1) * D_MODEL].reshape(N_LAYERS, 1, D_MODEL) for k in range(3))
    gs = norm_g.reshape(N_LAYERS, 1, D_MODEL) * (1.0 + scale)

    xs, hs, zs = [x0], [], []
    for l in range(N_LAYERS):
        w_in_b, w_out_b, small, _ = layers[l]
        h, z = _in_proj(xs[l], gs[l], shift[l], w_in_b, tile)
        hs.append(h)
        zs.append(z)
        ride = _plans(_weight_shards(shard, l + 1), GATHER_RULES) if l + 1 < N_LAYERS else None
        x_next, gathered = _mix_out(z, xs[l], gate[l], small, w_out_b, tile, ride=ride)
        xs.append(x_next)
        if ride:
            layers.append(_layer_weights(shard, l + 1, gathered))
    dx, loss_part, dfinal_g = _loss_head(xs[N_LAYERS], final_g.reshape(1, D_MODEL), target, tile)
    loss = lax.psum(loss_part[0, 0], ("x", "y", "c"))

    part = {}
    layer_parts = [None] * N_LAYERS
    slots = [None] * N_LAYERS
    for l in reversed(range(N_LAYERS)):
        w_in_b, w_out_b, small, small_t = layers[l]
        ride = _plans(layer_parts[l + 1]['big'], SCATTER_RULES) if l + 1 < N_LAYERS else None
        (dz, ycat, sums, dwp, dw2, dws, dbs), rode = _mix_bwd(zs[l], dx, gate[l], small, small_t, w_out_b, tile, ride=ride)
        if ride:
            slots[l + 1] = rode
        m_out = _tokens_matmul(ycat, dx, "out_proj_tokens_matmul")
        dw_out, dgate = _out_proj_grads(m_out, w_out_b, gate[l])
        dw_in = _tokens_matmul(hs[l], dz, "in_proj_tokens_matmul")
        ride = _plans([dw_in, dw_out, dw2], SCATTER_RULES) if l == 0 else None
        (dx, dshift, dgs), rode = _norm_bwd(xs[l], dz, dx, gs[l], w_in_b, tile, ride=ride)
        if ride:
            slots[l] = rode
        layer_parts[l] = dict(
            big=[dw_in, dw_out, dw2],
            b_ada=jnp.concatenate([dshift, dgs * norm_g[l][None], dgate], axis=1)[0],
            norm_g=(dgs * (1.0 + scale[l]))[0],
            w_conv_a=sums[S_WCONV:S_WCONV + CONV_A], w_dw_c=sums[S_WDW:S_WDW + CONV_C],
            pool_scale=sums[S_PSCALE], b_dw_c=sums[S_BDW], ln_g_c=sums[S_LNGC], ln_b_c=sums[S_LNBC],
            b_pw2_c=sums[S_BPW2], ln_g_d=sums[S_LNGD], ln_b_d=sums[S_LNBD],
            w_pool=jnp.einsum('gchd,gh->gcd', dwp.reshape(4, 64, 4, 64), jnp.eye(4, dtype=F32)),
            w_s_d=dws.reshape(4, SUB, SUB) * jnp.tril(jnp.ones((SUB, SUB), F32)),
            b_s_d=dbs.reshape(SUB, 4, 64).sum(axis=-1).T)
    grad_x = dx.reshape(x.shape)
    for n in REPLICATED + CHANNEL_SHARDED:
        part[n] = dfinal_g[0] if n == 'final_g' else jnp.stack([layer_parts[l][n] for l in range(N_LAYERS)])

    small_names = REPLICATED + CHANNEL_SHARDED
    small_shapes = [part[n].shape for n in small_names]
    (small_slots,) = _exchange("gather_small_gradients", _plans([_pack([part[n] for n in small_names])], [_gather]))

    grads, deltas, new_m, new_v = {}, {}, {}, {}
    for j, n in enumerate(('w_in', 'w_out', 'w_pw2_c')):
        outs = [_adam_update(shard[n][l], slots[l][j], mom_m[n][l], mom_v[n][l], "update_" + n) for l in range(N_LAYERS)]
        grads[n], deltas[n], new_m[n], new_v[n] = (jnp.stack(o) for o in zip(*outs))

    gsum = dict(zip(small_names, _unpack(_sum_slots(small_slots, "sum_small_gradients"), small_shapes)))
    for n in CHANNEL_SHARDED:
        width = shard[n].shape[2]
        gsum[n] = lax.dynamic_slice_in_dim(gsum[n], me * width, width, axis=2)
    outs = _adam_update(_pack([shard[n] for n in small_names]), _pack([gsum[n] for n in small_names]),
                        _pack([mom_m[n] for n in small_names]), _pack([mom_v[n] for n in small_names]), "update_small")
    own_shapes = [shard[n].shape for n in small_names]
    _, d_small, m_small, v_small = (_unpack(o, own_shapes) for o in outs)
    for j, n in enumerate(small_names):
        grads[n], deltas[n], new_m[n], new_v[n] = gsum[n], d_small[j], m_small[j], v_small[j]

    dmod_all = _unpack(small_slots, small_shapes, lead=(N_DEV,))[small_names.index('b_ada')]
    dmod_cols = lax.dynamic_slice_in_dim(dmod_all, me * ada_cols, ada_cols, axis=2).transpose(1, 0, 2)
    grads['w_ada'], deltas['w_ada'], new_m['w_ada'], new_v['w_ada'] = _ada_update(
        c_act.T, dmod_cols, w_ada, m_w_ada, v_w_ada)

    return (loss, grad_x, *[grads[n] for n in WEIGHTS], *[deltas[n] for n in WEIGHTS],
            *[new_m[n] for n in WEIGHTS], *[new_v[n] for n in WEIGHTS])
```

```python
import functools
import math

import jax
import jax.numpy as jnp
from jax import lax
from jax.experimental import pallas as pl
from jax.experimental.pallas import tpu as pltpu

F32 = jnp.float32
BF16 = jnp.bfloat16

N_DEV = 8
D_MODEL = 1024
GROUP = 256
D_IN = 12 * GROUP
N_LAYERS = 2
HALO = 32
SUB = 128
WIN = SUB + HALO
TOKEN_TILE = 512
REDUCE_TILE = 1024
EPS = 1e-6
VMEM_BYTES_V7X = 64 * 1024 * 1024
VMEM_LIMIT = VMEM_BYTES_V7X - 8 * 1024 * 1024

ADAM_LR = 0.001
ADAM_B1 = 0.9
ADAM_B2 = 0.999
ADAM_EPS = 1e-08
ADAM_WD = 0.01
ADAM_STEP = 10

A_B, A_C, A_X, A_G, B_P, B_G, C_A, C_GL, C_G, D_U, D_V, D_G = range(12)
V_PSCALE, V_BDW, V_LNGC, V_LNBC, V_BPW2, V_LNGD, V_LNBD = range(7)
S_WCONV, S_PSCALE, S_BDW, S_LNGC, S_LNBC, S_BPW2, S_LNGD, S_LNBD, S_WDW = 0, 3, 4, 5, 6, 7, 8, 9, 16
N_SUMS = 64
CONV_A = 3
CONV_C = 31

WEIGHTS = ('norm_g', 'w_ada', 'b_ada', 'w_in', 'w_conv_a', 'w_pool', 'pool_scale', 'w_dw_c', 'b_dw_c', 'ln_g_c',
           'ln_b_c', 'w_pw2_c', 'b_pw2_c', 'ln_g_d', 'ln_b_d', 'w_s_d', 'b_s_d', 'w_out', 'final_g')
REPLICATED = ('norm_g', 'b_ada', 'w_pool', 'pool_scale', 'b_dw_c', 'ln_g_c', 'ln_b_c', 'b_pw2_c', 'ln_g_d', 'ln_b_d',
              'w_s_d', 'b_s_d', 'final_g')
CHANNEL_SHARDED = ('w_conv_a', 'w_dw_c')


def _params(semantics, vmem=VMEM_LIMIT):
    return pltpu.CompilerParams(dimension_semantics=semantics, vmem_limit_bytes=vmem)


def _cols(g):
    return slice(g * GROUP, (g + 1) * GROUP)


def _full(shape):
    return pl.BlockSpec(shape, lambda *_: (0,) * len(shape))


def _silu(x):
    s = jax.nn.sigmoid(x)
    return x * s, s


def _dsilu(x, s):
    return s * (1.0 + x * (1.0 - s))


_GELU_C0 = math.sqrt(2.0 / math.pi)
_GELU_C1 = 0.044715


def _gelu(x):
    th = jnp.tanh(_GELU_C0 * (x + _GELU_C1 * (x * x * x)))
    return 0.5 * x * (1.0 + th), th


def _dgelu(x, th):
    return 0.5 * (1.0 + th) + 0.5 * x * (1.0 - th * th) * (_GELU_C0 * (1.0 + 3.0 * _GELU_C1 * (x * x)))


def _layer_norm(x):
    mu = jnp.mean(x, axis=-1, keepdims=True)
    xc = x - mu
    rstd = lax.rsqrt(jnp.mean(xc * xc, axis=-1, keepdims=True) + EPS)
    return xc * rstd, rstd


def _layer_norm_bwd(dn, n, rstd):
    return rstd * (dn - jnp.mean(dn, axis=-1, keepdims=True) - n * jnp.mean(dn * n, axis=-1, keepdims=True))


def _shift_rows(a, k):
    k = k % a.shape[0]
    return a if k == 0 else pltpu.roll(a, k, 0)


def _row_sum8(a):
    s = a[0:8]
    for m in range(1, a.shape[0] // 8):
        s = s + a[8 * m:8 * m + 8]
    return s


def _lane():
    return lax.broadcasted_iota(jnp.int32, (SUB, GROUP), 1)


def _by_quarter(lane, parts):
    return jnp.where(lane < 64, parts[0], jnp.where(lane < 128, parts[1], jnp.where(lane < 192, parts[2], parts[3])))


def _conv_inputs(z_ref, rows):
    def f(g):
        return z_ref[rows, _cols(g)].astype(F32)
    return f(A_C) * f(A_X), f(B_P), f(C_A) * jax.nn.sigmoid(f(C_GL))


def _fill_past(past_ref, zh_ref, zm_ref, is_first, tile):
    parts = _conv_inputs(zh_ref, slice(None))
    for n, a in enumerate(parts):
        past_ref[0:HALO, _cols(n)] = jnp.where(is_first, 0.0, a)

    def body(j, carry):
        r0 = pl.multiple_of(j * SUB, SUB)
        for n, a in enumerate(_conv_inputs(zm_ref, pl.ds(r0, SUB))):
            past_ref[pl.ds(r0 + HALO, SUB), _cols(n)] = a
        return carry

    lax.fori_loop(0, tile // SUB, body, 0)


def _short_conv_taps(qw):
    return [_shift_rows(qw, CONV_A - 1 - k)[HALO:WIN] for k in range(CONV_A)]


def _window_sums(pw, lane):
    s2 = pw + _shift_rows(pw, 1)
    s4 = s2 + _shift_rows(s2, 2)
    s8 = s4 + _shift_rows(s4, 4)
    s16 = s8 + _shift_rows(s8, 8)
    return _by_quarter(lane, [s[HALO:WIN] for s in (s2, s4, s8, s16)])


def _inv_count(lane, t_first):
    width = _by_quarter(lane, [2.0, 4.0, 8.0, 16.0])
    t = lax.broadcasted_iota(jnp.int32, (SUB, GROUP), 0) + t_first
    return 1.0 / jnp.minimum((t + 1).astype(F32), width)


def _forward_window_sums(ew, lane):
    n = ew.shape[0]
    f2 = ew + _shift_rows(ew, n - 1)
    f4 = f2 + _shift_rows(f2, n - 2)
    f8 = f4 + _shift_rows(f4, n - 4)
    f16 = f8 + _shift_rows(f8, n - 8)
    return _by_quarter(lane, [f[0:SUB] for f in (f2, f4, f8, f16)])


def _mixers_forward(zc, win, t_first, wc_ref, wdw_ref, vec_ref, wp_ref, w2_ref, ws_ref, bs_ref, o_c=None):
    lane = _lane()

    def vec(n):
        return vec_ref[n:n + 1, :]

    taps = _short_conv_taps(win(0))
    o_a = wc_ref[0:1, :] * taps[0] + wc_ref[1:2, :] * taps[1] + wc_ref[2:3, :] * taps[2]
    a_b, a_g = zc(A_B), zc(A_G)
    sg_a, s_a = _silu(a_g)
    y_a = a_b * o_a * sg_a

    pw = win(1)
    ic = _inv_count(lane, t_first)
    pooled = _window_sums(pw, lane) * ic - pw[HALO:WIN]
    pooled_b = pooled.astype(BF16)
    y0_b = jnp.dot(pooled_b, wp_ref[...], preferred_element_type=F32)
    b_g = zc(B_G)
    sg_b, s_b = _silu(b_g)
    y_b = y0_b * vec(V_PSCALE) * sg_b

    hw = win(2)
    if o_c is None:
        o_c = wdw_ref[CONV_C - 1:CONV_C, :] * hw[HALO:WIN] + vec(V_BDW)
        for k in range(CONV_C - 1):
            o_c = o_c + wdw_ref[k:k + 1, :] * _shift_rows(hw, CONV_C - 1 - k)[HALO:WIN]
    n_c, rstd_c = _layer_norm(o_c)
    ln_c = n_c * vec(V_LNGC) + vec(V_LNBC)
    sl_c, ssl_c = _silu(ln_c)
    sl_b = sl_c.astype(BF16)
    yc = jnp.dot(sl_b, w2_ref[...], preferred_element_type=F32) + vec(V_BPW2)
    c_g = zc(C_G)
    sg_c, s_c = _silu(c_g)
    y_c = yc * sg_c

    d_u, d_v, d_g = zc(D_U), zc(D_V), zc(D_G)
    u, th_u = _gelu(d_u)
    gv, th_v = _gelu(d_v)
    n_d, rstd_d = _layer_norm(gv)
    v_b = (n_d * vec(V_LNGD) + vec(V_LNBD)).astype(BF16)
    r = jnp.dot(ws_ref[...], v_b, preferred_element_type=F32)
    mixed = _by_quarter(lane, [r[h * SUB:(h + 1) * SUB] for h in range(4)]) + bs_ref[...]
    sg_d, s_d = _silu(d_g)
    y_d = u * mixed * sg_d

    saved = dict(lane=lane, taps=taps, o_a=o_a, a_b=a_b, a_g=a_g, sg_a=sg_a, s_a=s_a,
                 ic=ic, pooled_b=pooled_b, y0_b=y0_b, b_g=b_g, sg_b=sg_b, s_b=s_b,
                 hw=hw, o_c=o_c, n_c=n_c, rstd_c=rstd_c, ln_c=ln_c, ssl_c=ssl_c, sl_b=sl_b, yc=yc, c_g=c_g, sg_c=sg_c, s_c=s_c,
                 d_u=d_u, d_v=d_v, d_g=d_g, u=u, th_u=th_u, th_v=th_v, n_d=n_d, rstd_d=rstd_d, v_b=v_b, mixed=mixed,
                 sg_d=sg_d, s_d=s_d)
    return (y_a, y_b, y_c, y_d), saved


def _in_proj(x, gs, shift, w_in_b, tile, ride=None):
    n_tok = x.shape[0]

    def body(x_ref, gs_ref, sh_ref, w_ref, h_ref, z_ref):
        xv = x_ref[...]
        r = lax.rsqrt(jnp.mean(xv * xv, axis=-1, keepdims=True) + EPS)
        h = ((xv * r) * gs_ref[...] + sh_ref[...]).astype(BF16)
        h_ref[...] = h
        for j in range(D_IN // D_MODEL):
            cs = slice(j * D_MODEL, (j + 1) * D_MODEL)
            z_ref[:, cs] = jnp.dot(h, w_ref[:, cs], preferred_element_type=F32).astype(BF16)

    return _tiled_call(
        body, (x, gs, shift, w_in_b), name="in_proj", grid=(n_tok // tile,),
        in_specs=[pl.BlockSpec((tile, D_MODEL), lambda i: (i, 0)), _full((1, D_MODEL)), _full((1, D_MODEL)),
                  _full((D_MODEL, D_IN))],
        out_specs=[pl.BlockSpec((tile, D_MODEL), lambda i: (i, 0)), pl.BlockSpec((tile, D_IN), lambda i: (i, 0))],
        out_shape=[jax.ShapeDtypeStruct((n_tok, D_MODEL), BF16), jax.ShapeDtypeStruct((n_tok, D_IN), BF16)],
        ride=ride)


def _small_specs(with_transposes):
    specs = [_full((8, GROUP)), _full((HALO, GROUP)), _full((16, GROUP)), _full((GROUP, GROUP)), _full((GROUP, GROUP)),
             _full((4 * SUB, SUB)), _full((SUB, GROUP))]
    if with_transposes:
        specs += [_full((GROUP, GROUP)), _full((GROUP, GROUP)), _full((4 * SUB, SUB))]
    return specs


def _mix_out(z, x, gate, small, w_out_b, tile, ride=None):
    n_tok = x.shape[0]
    n_tiles = n_tok // tile
    n_sub = tile // SUB
    cw = D_MODEL // n_sub
    per_halo = tile // HALO

    def cur(i):
        return jnp.minimum(i, n_tiles - 1)

    def prev(i):
        return jnp.maximum(i - 1, 0)

    def body(zm_ref, zh_ref, x_ref, gate_ref, wc_ref, wdw_ref, vec_ref, wp_ref, w2_ref, ws_ref, bs_ref, wout_ref,
             xo_ref, oc_ref, past_ref, ycat_ref, ycat_prev_ref):
        i = pl.program_id(0)
        t = cur(i)

        @pl.when(i == 0)
        def _():
            ycat_prev_ref[...] = jnp.zeros_like(ycat_prev_ref)

        _fill_past(past_ref, zh_ref, zm_ref, t == 0, tile)
        for j in range(n_sub):
            cs = slice(j * cw, (j + 1) * cw)
            y = jnp.dot(ycat_prev_ref[...], wout_ref[:, cs], preferred_element_type=F32)
            xo_ref[:, cs] = x_ref[:, cs] + gate_ref[:, cs] * y
            rows = slice(j * SUB, (j + 1) * SUB)
            ys, s = _mixers_forward(
                lambda g: zm_ref[rows, _cols(g)].astype(F32), lambda n: past_ref[j * SUB:j * SUB + WIN, _cols(n)],
                t * tile + j * SUB, wc_ref, wdw_ref, vec_ref, wp_ref, w2_ref, ws_ref, bs_ref)
            for n, y in enumerate(ys):
                ycat_ref[rows, _cols(n)] = y.astype(BF16)
            oc_ref[rows, :] = s["o_c"]
        ycat_prev_ref[...] = ycat_ref[...]

    (x_next, o_c), rode = _tiled_call(
        body, (z, z, x, gate, *small, w_out_b), name="mix_out", grid=(n_tiles + 1,),
        in_specs=[pl.BlockSpec((tile, D_IN), lambda i: (cur(i), 0)),
                  pl.BlockSpec((HALO, D_IN), lambda i: (jnp.maximum(cur(i) * per_halo - 1, 0), 0)),
                  pl.BlockSpec((tile, D_MODEL), lambda i: (prev(i), 0)), _full((1, D_MODEL)),
                  *_small_specs(False), _full((D_MODEL, D_MODEL))],
        out_specs=[pl.BlockSpec((tile, D_MODEL), lambda i: (prev(i), 0)), pl.BlockSpec((tile, GROUP), lambda i: (cur(i), 0))],
        out_shape=[jax.ShapeDtypeStruct((n_tok, D_MODEL), F32), jax.ShapeDtypeStruct((n_tok, GROUP), F32)],
        scratch_shapes=[pltpu.VMEM((tile + HALO, 3 * GROUP), F32), pltpu.VMEM((tile, D_MODEL), BF16),
                        pltpu.VMEM((tile, D_MODEL), BF16)], ride=ride)
    return x_next, o_c, rode


def _mix_bwd(z, o_c, dx_next, gate, small, small_t, w_out_b, tile, ride=None):
    n_tok = z.shape[0]
    n_tiles = n_tok // tile
    n_sub = tile // SUB
    cw = D_MODEL // n_sub
    per_halo = tile // HALO
    nt_dims = (((1,), (1,)), ((), ()))

    def tile_of(i):
        return n_tiles - 1 - i

    def next_tile_of(i):
        return jnp.maximum(n_tiles - 2 - i, 0)

    def body(zm_ref, zh_ref, oc_ref, dxn_ref, dxn_next_ref, gate_ref, wc_ref, wdw_ref, vec_ref, wp_ref, w2_ref, ws_ref,
             bs_ref, wpt_ref, w2t_ref, wst_ref, wout_ref,
             dz_ref, ycat_ref, sums_ref, dwp_ref, dw2_ref, dws_ref, dbs_ref,
             past_ref, future_ref, dy_ref, dy_next_ref, acc_ref):
        i = pl.program_id(0)
        t = tile_of(i)

        @pl.when(i == 0)
        def _():
            acc_ref[...] = jnp.zeros_like(acc_ref)
            dwp_ref[...] = jnp.zeros_like(dwp_ref)
            dw2_ref[...] = jnp.zeros_like(dw2_ref)
            dws_ref[...] = jnp.zeros_like(dws_ref)
            dbs_ref[...] = jnp.zeros_like(dbs_ref)
            future_ref[tile:tile + HALO, :] = jnp.zeros((HALO, 3 * GROUP), F32)
            dy_ref[...] = lax.dot_general((dxn_ref[...] * gate_ref[...]).astype(BF16), wout_ref[...], nt_dims,
                                        preferred_element_type=F32)

        _fill_past(past_ref, zh_ref, zm_ref, t == 0, tile)
        dyb_next = (dxn_next_ref[...] * gate_ref[...]).astype(BF16)

        def vec(n):
            return vec_ref[n:n + 1, :]

        for jj in range(n_sub):
            j = n_sub - 1 - jj
            r0 = j * SUB
            rows = slice(r0, r0 + SUB)

            def zc(g):
                return zm_ref[rows, _cols(g)].astype(F32)

            def add(n, a):
                acc_ref[n] = acc_ref[n] + _row_sum8(a)

            def put(g, a):
                dz_ref[rows, _cols(g)] = a.astype(BF16)

            def future_window(n, a):
                future_ref[rows, _cols(n)] = a
                return future_ref[r0:r0 + WIN, _cols(n)]

            ys, s = _mixers_forward(zc, lambda n: past_ref[r0:r0 + WIN, _cols(n)], t * tile + r0,
                                    wc_ref, wdw_ref, vec_ref, wp_ref, w2_ref, ws_ref, bs_ref, o_c=oc_ref[rows, :])
            for n, y in enumerate(ys):
                ycat_ref[rows, _cols(n)] = y.astype(BF16)
            lane = s["lane"]
            ks = slice(jj * cw, (jj + 1) * cw)
            dy_next_ref[:, ks] = lax.dot_general(dyb_next, wout_ref[ks, :], nt_dims, preferred_element_type=F32)

            dy = dy_ref[rows,_cols(0)]
            put(A_B, dy * s["o_a"] * s["sg_a"])
            put(A_G, dy * s["a_b"] * s["o_a"] * _dsilu(s["a_g"], s["s_a"]))
            do = dy * s["a_b"] * s["sg_a"]
            for k in range(CONV_A):
                add(S_WCONV + k, do * s["taps"][k])
            dow = future_window(0, do)
            dq = wc_ref[CONV_A - 1:CONV_A, :] * dow[0:SUB]
            for k in range(CONV_A - 1):
                dq = dq + wc_ref[k:k + 1, :] * _shift_rows(dow, WIN - (CONV_A - 1 - k))[0:SUB]
            put(A_C, dq * zc(A_X))
            put(A_X, dq * zc(A_C))

            dy = dy_ref[rows,_cols(1)]
            put(B_G, dy * (s["y0_b"] * vec(V_PSCALE)) * _dsilu(s["b_g"], s["s_b"]))
            dyb = dy * s["sg_b"]
            add(S_PSCALE, dyb * s["y0_b"])
            dpw_b = (dyb * vec(V_PSCALE)).astype(BF16)
            dwp_ref[...] += lax.dot_general(s["pooled_b"], dpw_b, (((0,), (0,)), ((), ())), preferred_element_type=F32)
            dpooled = jnp.dot(dpw_b, wpt_ref[...], preferred_element_type=F32)
            ew = future_window(1, dpooled * s["ic"])
            put(B_P, _forward_window_sums(ew, lane) - dpooled)

            dy = dy_ref[rows,_cols(2)]
            put(C_G, dy * s["yc"] * _dsilu(s["c_g"], s["s_c"]))
            dyc = dy * s["sg_c"]
            add(S_BPW2, dyc)
            dyc_b = dyc.astype(BF16)
            dw2_ref[...] += lax.dot_general(s["sl_b"], dyc_b, (((0,), (0,)), ((), ())), preferred_element_type=F32)
            dln = jnp.dot(dyc_b, w2t_ref[...], preferred_element_type=F32) * _dsilu(s["ln_c"], s["ssl_c"])
            add(S_LNGC, dln * s["n_c"])
            add(S_LNBC, dln)
            do = _layer_norm_bwd(dln * vec(V_LNGC), s["n_c"], s["rstd_c"])
            add(S_BDW, do)
            hw = s["hw"]
            for k in range(CONV_C):
                add(S_WDW + k, do * _shift_rows(hw, CONV_C - 1 - k)[HALO:WIN])
            dow = future_window(2, do)
            dhc = wdw_ref[CONV_C - 1:CONV_C, :] * dow[0:SUB]
            for k in range(CONV_C - 1):
                dhc = dhc + wdw_ref[k:k + 1, :] * _shift_rows(dow, WIN - (CONV_C - 1 - k))[0:SUB]
            c_a = zc(C_A)
            sgl = jax.nn.sigmoid(zc(C_GL))
            put(C_A, dhc * sgl)
            put(C_GL, dhc * c_a * sgl * (1.0 - sgl))

            dy = dy_ref[rows,_cols(3)]
            put(D_G, dy * s["u"] * s["mixed"] * _dsilu(s["d_g"], s["s_d"]))
            put(D_U, dy * s["mixed"] * s["sg_d"] * _dgelu(s["d_u"], s["th_u"]))
            dmixed = dy * s["u"] * s["sg_d"]
            dbs_ref[...] += dmixed
            by_head = jnp.concatenate(
                [jnp.where((lane >= 64 * h) & (lane < 64 * h + 64), dmixed, 0.0) for h in range(4)], axis=0).astype(BF16)
            dws_ref[...] += lax.dot_general(by_head, s["v_b"], (((1,), (1,)), ((), ())), preferred_element_type=F32)
            rv = jnp.dot(wst_ref[...], dmixed.astype(BF16), preferred_element_type=F32)
            dv = _by_quarter(lane, [rv[h * SUB:(h + 1) * SUB] for h in range(4)])
            add(S_LNGD, dv * s["n_d"])
            add(S_LNBD, dv)
            dgv = _layer_norm_bwd(dv * vec(V_LNGD), s["n_d"], s["rstd_d"])
            put(D_V, dgv * _dgelu(s["d_v"], s["th_v"]))

        future_ref[tile:tile + HALO, :] = future_ref[0:HALO, :]
        dy_ref[...] = dy_next_ref[...]

        @pl.when(i == n_tiles - 1)
        def _():
            for n in range(N_SUMS):
                sums_ref[n:n + 1, :] = jnp.sum(acc_ref[n], axis=0, keepdims=True)

    return _tiled_call(
        body, (z, z, o_c, dx_next, dx_next, gate, *small, *small_t, w_out_b), name="mix_bwd", grid=(n_tiles,),
        in_specs=[pl.BlockSpec((tile, D_IN), lambda i: (tile_of(i), 0)),
                  pl.BlockSpec((HALO, D_IN), lambda i: (jnp.maximum(tile_of(i) * per_halo - 1, 0), 0)),
                  pl.BlockSpec((tile, GROUP), lambda i: (tile_of(i), 0)),
                  pl.BlockSpec((tile, D_MODEL), lambda i: (tile_of(i), 0)),
                  pl.BlockSpec((tile, D_MODEL), lambda i: (next_tile_of(i), 0)), _full((1, D_MODEL)),
                  *_small_specs(True), _full((D_MODEL, D_MODEL))],
        out_specs=[pl.BlockSpec((tile, D_IN), lambda i: (tile_of(i), 0)),
                   pl.BlockSpec((tile, D_MODEL), lambda i: (tile_of(i), 0)),
                   _full((N_SUMS, GROUP)), _full((GROUP, GROUP)), _full((GROUP, GROUP)), _full((4 * SUB, SUB)),
                   _full((SUB, GROUP))],
        out_shape=[jax.ShapeDtypeStruct((n_tok, D_IN), BF16), jax.ShapeDtypeStruct((n_tok, D_MODEL), BF16),
                   jax.ShapeDtypeStruct((N_SUMS, GROUP), F32), jax.ShapeDtypeStruct((GROUP, GROUP), F32),
                   jax.ShapeDtypeStruct((GROUP, GROUP), F32), jax.ShapeDtypeStruct((4 * SUB, SUB), F32),
                   jax.ShapeDtypeStruct((SUB, GROUP), F32)],
        scratch_shapes=[pltpu.VMEM((tile + HALO, 3 * GROUP), F32), pltpu.VMEM((tile + HALO, 3 * GROUP), F32),
                        pltpu.VMEM((tile, D_MODEL), F32), pltpu.VMEM((tile, D_MODEL), F32),
                        pltpu.VMEM((N_SUMS, 8, GROUP), F32)], ride=ride)


def _norm_bwd(x, dz, dx_next, gs, w_in_b, tile, ride=None):
    n_tok = x.shape[0]
    n_tiles = n_tok // tile

    def body(x_ref, dz_ref, dxn_ref, gs_ref, w_ref, dx_ref, dsh_ref, dgs_ref, acc_ref):
        i = pl.program_id(0)

        @pl.when(i == 0)
        def _():
            acc_ref[...] = jnp.zeros_like(acc_ref)

        dh = lax.dot_general(dz_ref[...], w_ref[...], (((1,), (1,)), ((), ())), preferred_element_type=F32)
        xv = x_ref[...]
        r = lax.rsqrt(jnp.mean(xv * xv, axis=-1, keepdims=True) + EPS)
        xn = xv * r
        acc_ref[0] = acc_ref[0] + _row_sum8(dh)
        acc_ref[1] = acc_ref[1] + _row_sum8(dh * xn)
        dxn = dh * gs_ref[...]
        dx_ref[...] = dxn_ref[...] + r * (dxn - xn * jnp.mean(dxn * xn, axis=-1, keepdims=True))

        @pl.when(i == n_tiles - 1)
        def _():
            dsh_ref[...] = jnp.sum(acc_ref[0], axis=0, keepdims=True)
            dgs_ref[...] = jnp.sum(acc_ref[1], axis=0, keepdims=True)

    return _tiled_call(
        body, (x, dz, dx_next, gs, w_in_b), name="norm_bwd", grid=(n_tiles,),
        in_specs=[pl.BlockSpec((tile, D_MODEL), lambda i: (i, 0)), pl.BlockSpec((tile, D_IN), lambda i: (i, 0)),
                  pl.BlockSpec((tile, D_MODEL), lambda i: (i, 0)), _full((1, D_MODEL)), _full((D_MODEL, D_IN))],
        out_specs=[pl.BlockSpec((tile, D_MODEL), lambda i: (i, 0)), _full((1, D_MODEL)), _full((1, D_MODEL))],
        out_shape=[jax.ShapeDtypeStruct((n_tok, D_MODEL), F32), jax.ShapeDtypeStruct((1, D_MODEL), F32),
                   jax.ShapeDtypeStruct((1, D_MODEL), F32)],
        scratch_shapes=[pltpu.VMEM((2, 8, D_MODEL), F32)], ride=ride)


def _loss_head(x, final_g, target, tile):
    n_tok = x.shape[0]
    n_tiles = n_tok // tile

    def body(x_ref, g_ref, t_ref, dx_ref, loss_ref, dg_ref, acc_ref):
        i = pl.program_id(0)

        @pl.when(i == 0)
        def _():
            acc_ref[...] = jnp.zeros_like(acc_ref)

        xv = x_ref[...]
        r = lax.rsqrt(jnp.mean(xv * xv, axis=-1, keepdims=True) + EPS)
        xn = xv * r
        err = xn * g_ref[...] - t_ref[...]
        acc_ref[0] = acc_ref[0] + _row_sum8(err * err)
        dy = err * (1.0 / D_MODEL)
        acc_ref[1] = acc_ref[1] + _row_sum8(dy * xn)
        a = dy * g_ref[...]
        dx_ref[...] = r * (a - xn * jnp.mean(a * xn, axis=-1, keepdims=True))

        @pl.when(i == n_tiles - 1)
        def _():
            loss_ref[...] = jnp.full((8, 128), 0.5 / D_MODEL, F32) * jnp.sum(acc_ref[0])
            dg_ref[...] = jnp.sum(acc_ref[1], axis=0, keepdims=True)

    return pl.pallas_call(
        body, name="loss_head", grid=(n_tiles,),
        in_specs=[pl.BlockSpec((tile, D_MODEL), lambda i: (i, 0)), _full((1, D_MODEL)),
                  pl.BlockSpec((tile, D_MODEL), lambda i: (i, 0))],
        out_specs=[pl.BlockSpec((tile, D_MODEL), lambda i: (i, 0)), _full((8, 128)), _full((1, D_MODEL))],
        out_shape=[jax.ShapeDtypeStruct((n_tok, D_MODEL), F32), jax.ShapeDtypeStruct((8, 128), F32),
                   jax.ShapeDtypeStruct((1, D_MODEL), F32)],
        scratch_shapes=[pltpu.VMEM((2, 8, D_MODEL), F32)],
        compiler_params=_params(("arbitrary",)),
    )(x, final_g, target)


def _tokens_matmul(a, b, name):
    n_tok, ka = a.shape
    nb = b.shape[1]
    tk = min(REDUCE_TILE, n_tok)
    cb = min(D_MODEL, nb)

    def body(a_ref, b_ref, o_ref):
        @pl.when(pl.program_id(1) == 0)
        def _():
            o_ref[...] = jnp.zeros_like(o_ref)

        o_ref[...] += lax.dot_general(a_ref[...], b_ref[...].astype(BF16), (((0,), (0,)), ((), ())),
                                      preferred_element_type=F32)

    return pl.pallas_call(
        body, name=name, grid=(nb // cb, n_tok // tk),
        in_specs=[pl.BlockSpec((tk, ka), lambda j, i: (i, 0)), pl.BlockSpec((tk, cb), lambda j, i: (i, j))],
        out_specs=pl.BlockSpec((ka, cb), lambda j, i: (0, j)),
        out_shape=jax.ShapeDtypeStruct((ka, nb), F32),
        compiler_params=_params(("parallel", "arbitrary")),
    )(a, b)


def _out_proj_grads(m, w_out_b, gate):
    rb = 256
    n_blocks = D_MODEL // rb

    def body(m_ref, w_ref, gate_ref, dw_ref, dgate_ref, acc_ref):
        i = pl.program_id(0)

        @pl.when(i == 0)
        def _():
            acc_ref[...] = jnp.zeros_like(acc_ref)

        mv = m_ref[...]
        dw_ref[...] = mv * gate_ref[...]
        acc_ref[...] += _row_sum8(mv * w_ref[...].astype(F32))

        @pl.when(i == n_blocks - 1)
        def _():
            dgate_ref[...] = jnp.sum(acc_ref[...], axis=0, keepdims=True)

    return pl.pallas_call(
        body, name="out_proj_grads", grid=(n_blocks,),
        in_specs=[pl.BlockSpec((rb, D_MODEL), lambda i: (i, 0)), pl.BlockSpec((rb, D_MODEL), lambda i: (i, 0)),
                  _full((1, D_MODEL))],
        out_specs=[pl.BlockSpec((rb, D_MODEL), lambda i: (i, 0)), _full((1, D_MODEL))],
        out_shape=[jax.ShapeDtypeStruct((D_MODEL, D_MODEL), F32), jax.ShapeDtypeStruct((1, D_MODEL), F32)],
        scratch_shapes=[pltpu.VMEM((8, D_MODEL), F32)],
        compiler_params=_params(("arbitrary",)),
    )(m, w_out_b, gate)


def _modulation_columns(c_all, w_ada, b_cols):
    cols = w_ada.shape[2]

    def body(c_ref, w_ref, b_ref, ca_ref, mod_ref):
        ca, _ = _silu(c_ref[...])
        ca_ref[...] = ca
        for l in range(N_LAYERS):
            mod_ref[l] = jnp.dot(ca, w_ref[l], precision=lax.Precision.HIGHEST, preferred_element_type=F32) + b_ref[l:l + 1, :]

    return pl.pallas_call(
        body, name="modulation_columns",
        out_shape=[jax.ShapeDtypeStruct((N_DEV, D_MODEL), F32), jax.ShapeDtypeStruct((N_LAYERS, N_DEV, cols), F32)],
        compiler_params=pltpu.CompilerParams(vmem_limit_bytes=VMEM_LIMIT),
    )(c_all, w_ada, b_cols)


def _adam(w, g, m, v):
    m2 = ADAM_B1 * m + (1.0 - ADAM_B1) * g
    v2 = ADAM_B2 * v + (1.0 - ADAM_B2) * (g * g)
    m_hat = m2 / (1.0 - ADAM_B1 ** ADAM_STEP)
    v_hat = v2 / (1.0 - ADAM_B2 ** ADAM_STEP)
    return -ADAM_LR * (m_hat / (jnp.sqrt(v_hat) + ADAM_EPS) + ADAM_WD * w), m2, v2


def _row_block(rows, cols, slots):
    target = max(8, (1 << 19) // (cols * max(slots, 1)))
    rb = rows
    while rb > target and rb % 2 == 0 and (rb // 2) % 8 == 0:
        rb //= 2
    return rb


def _sum_slots(slots, name):
    _, rows, cols = slots.shape
    rb = _row_block(rows, cols, N_DEV)

    def body(s_ref, g_ref):
        g = s_ref[0]
        for q in range(1, N_DEV):
            g = g + s_ref[q]
        g_ref[...] = g

    return pl.pallas_call(
        body, name=name, grid=(rows // rb,),
        in_specs=[pl.BlockSpec((N_DEV, rb, cols), lambda i: (0, i, 0))],
        out_specs=pl.BlockSpec((rb, cols), lambda i: (i, 0)),
        out_shape=jax.ShapeDtypeStruct((rows, cols), F32),
        compiler_params=_params(("parallel",)),
    )(slots)


def _adam_update(w, g, m, v, name):
    rows, cols = w.shape
    slotted = g.ndim == 3
    rb = _row_block(rows, cols, N_DEV if slotted else 1)

    def body(w_ref, g_ref, m_ref, v_ref, go_ref, d_ref, mo_ref, vo_ref):
        if slotted:
            gv = g_ref[0]
            for q in range(1, N_DEV):
                gv = gv + g_ref[q]
        else:
            gv = g_ref[...]
        go_ref[...] = gv
        d_ref[...], mo_ref[...], vo_ref[...] = _adam(w_ref[...], gv, m_ref[...], v_ref[...])

    blk = pl.BlockSpec((rb, cols), lambda i: (i, 0))
    g_blk = pl.BlockSpec((N_DEV, rb, cols), lambda i: (0, i, 0)) if slotted else blk
    return pl.pallas_call(
        body, name=name, grid=(rows // rb,),
        in_specs=[blk, g_blk, blk, blk], out_specs=[blk] * 4,
        out_shape=[jax.ShapeDtypeStruct((rows, cols), F32)] * 4,
        compiler_params=_params(("parallel",)),
    )(w, g, m, v)


def _ada_update(ca_t, dmod_cols, w, m, v):
    _, rows, cols = w.shape

    def body(ca_ref, dm_ref, w_ref, m_ref, v_ref, g_ref, d_ref, mo_ref, vo_ref):
        g = ca_ref[:, 0:1] * dm_ref[0, 0:1, :]
        for b in range(1, N_DEV):
            g = g + ca_ref[:, b:b + 1] * dm_ref[0, b:b + 1, :]
        g_ref[0] = g
        d_ref[0], mo_ref[0], vo_ref[0] = _adam(w_ref[0], g, m_ref[0], v_ref[0])

    blk = pl.BlockSpec((1, rows, cols), lambda l: (l, 0, 0))
    return pl.pallas_call(
        body, name="ada_update", grid=(N_LAYERS,),
        in_specs=[_full((rows, N_DEV)), pl.BlockSpec((1, N_DEV, cols), lambda l: (l, 0, 0)), blk, blk, blk],
        out_specs=[blk] * 4, out_shape=[jax.ShapeDtypeStruct(w.shape, F32)] * 4,
        compiler_params=_params(("parallel",)),
    )(ca_t, dmod_cols, w, m, v)


def _exchange_sems(n):
    return [pltpu.SemaphoreType.DMA((n, N_DEV - 1)), pltpu.SemaphoreType.DMA((n, N_DEV - 1)),
            pltpu.SemaphoreType.DMA((n,))]


def _exchange_copies(plans, srcs, outs, sems, receiving):
    send_sems, recv_sems, local_sems = sems
    x, y, c = lax.axis_index("x"), lax.axis_index("y"), lax.axis_index("c")
    me = 4 * x + 2 * y + c

    def remote(i, k, incoming):
        _, o, send, land = plans[i]
        px = 1 - x if k & 4 else x
        py = 1 - y if k & 2 else y
        pc = 1 - c if k & 1 else c
        p = 4 * px + 2 * py + pc
        return pltpu.make_async_remote_copy(
            src_ref=send(srcs[i], p), dst_ref=land(outs[o], p if incoming else me),
            send_sem=send_sems.at[i, k - 1], recv_sem=recv_sems.at[i, k - 1],
            device_id=(px, py, pc), device_id_type=pl.DeviceIdType.MESH)

    pairs = [(i, k) for k in range(1, N_DEV) for i in range(len(plans))]
    local = [pltpu.make_async_copy(send(srcs[i], me), land(outs[o], me), local_sems.at[i])
             for i, (_, o, send, land) in enumerate(plans)]
    return local, [remote(i, k, False) for i, k in pairs], [remote(i, k, True) for i, k in pairs] if receiving else []


def _exchange_start(plans, srcs, outs, sems):
    local, outgoing, _ = _exchange_copies(plans, srcs, outs, sems, receiving=False)
    for cp in local + outgoing:
        cp.start()


def _exchange_wait(plans, srcs, outs, sems):
    local, outgoing, incoming = _exchange_copies(plans, srcs, outs, sems, receiving=True)
    for cp in incoming:
        cp.wait_recv()
    for cp in outgoing:
        cp.wait_send()
    for cp in local:
        cp.wait()


def _exchange(name, ride):
    out_shapes, plans = ride
    n = len(plans)
    hbm = pl.BlockSpec(memory_space=pltpu.HBM)

    def body(*refs):
        srcs, outs, sems = refs[:n], refs[n:n + len(out_shapes)], refs[n + len(out_shapes):]
        _exchange_start(plans, srcs, outs, sems)
        _exchange_wait(plans, srcs, outs, sems)

    return pl.pallas_call(
        body, name=name, in_specs=[hbm] * n, out_specs=[hbm] * len(out_shapes), out_shape=list(out_shapes),
        scratch_shapes=_exchange_sems(n),
    )(*[p[0] for p in plans])


def _tiled_call(body, args, *, name, grid, in_specs, out_specs, out_shape, scratch_shapes=(), ride=None):
    params = _params(("arbitrary",) * len(grid))
    if ride is None:
        return pl.pallas_call(body, name=name, grid=grid, in_specs=in_specs, out_specs=out_specs, out_shape=out_shape,
                              scratch_shapes=list(scratch_shapes), compiler_params=params)(*args), []
    shapes, plans = ride
    n_in, n_src, n_out, n_dst, n_scr = len(in_specs), len(plans), len(out_specs), len(shapes), len(scratch_shapes)
    hbm = pl.BlockSpec(memory_space=pltpu.HBM)

    def carrying(*refs):
        ins, srcs, refs = refs[:n_in], refs[n_in:n_in + n_src], refs[n_in + n_src:]
        outs, dsts, refs = refs[:n_out], refs[n_out:n_out + n_dst], refs[n_out + n_dst:]
        scratch, sems = refs[:n_scr], refs[n_scr:]
        ids = [pl.program_id(a) for a in range(len(grid))]
        first = functools.reduce(jnp.logical_and, [i == 0 for i in ids])
        last = functools.reduce(jnp.logical_and, [i == g - 1 for i, g in zip(ids, grid)])

        @pl.when(first)
        def _():
            _exchange_start(plans, srcs, dsts, sems)

        body(*ins, *outs, *scratch)

        @pl.when(last)
        def _():
            _exchange_wait(plans, srcs, dsts, sems)

    res = pl.pallas_call(
        carrying, name=name, grid=grid, in_specs=[*in_specs, *[hbm] * n_src], out_specs=[*out_specs, *[hbm] * n_dst],
        out_shape=[*out_shape, *shapes], scratch_shapes=[*scratch_shapes, *_exchange_sems(n_src)],
        compiler_params=params)(*args, *[p[0] for p in plans])
    return res[:n_out], res[n_out:]


def _tail(nd, idx):
    return (slice(None),) * (nd - 2) + idx


def _gather(a):
    return jax.ShapeDtypeStruct((N_DEV,) + a.shape, a.dtype), lambda s, p: s, lambda o, q: o.at[q]


def _gather_rows(a):
    r = a.shape[-2]
    return (jax.ShapeDtypeStruct(a.shape[:-2] + (N_DEV * r, a.shape[-1]), a.dtype), lambda s, p: s,
            lambda o, q: o.at[_tail(a.ndim, (pl.ds(pl.multiple_of(q * r, r), r), slice(None)))])


def _gather_cols(a):
    c = a.shape[-1]
    return (jax.ShapeDtypeStruct(a.shape[:-1] + (N_DEV * c,), a.dtype), lambda s, p: s,
            lambda o, q: o.at[_tail(a.ndim, (slice(None), pl.ds(pl.multiple_of(q * c, c), c)))])


def _scatter_rows(a):
    r = a.shape[0] // N_DEV
    return (jax.ShapeDtypeStruct((N_DEV, r, a.shape[1]), a.dtype),
            lambda s, p: s.at[pl.ds(pl.multiple_of(p * r, r), r), :], lambda o, q: o.at[q])


def _scatter_cols(a):
    c = a.shape[1] // N_DEV
    return (jax.ShapeDtypeStruct((N_DEV, a.shape[0], c), a.dtype),
            lambda s, p: s.at[:, pl.ds(pl.multiple_of(p * c, c), c)], lambda o, q: o.at[q])


def _plans(arrays, rules):
    shapes, plans = [], []
    for o, (a, rule) in enumerate(zip(arrays, rules)):
        shape, send, land = rule(a)
        shapes.append(shape)
        plans.append((a, o, send, land))
    return shapes, plans


def _pack(pieces):
    flat = []
    for a in pieces:
        f = a.reshape(-1)
        flat.append(jnp.pad(f, (0, (-f.shape[0]) % 128)))
    total = sum(f.shape[0] for f in flat)
    flat.append(jnp.zeros(((-total) % 1024,), F32))
    return jnp.concatenate(flat).reshape(-1, 128)


def _unpack(buf, shapes, lead=()):
    flat = buf.reshape(lead + (-1,))
    out, off = [], 0
    for s in shapes:
        n = math.prod(s)
        out.append(flat[..., off:off + n].reshape(lead + tuple(s)))
        off += n + (-n) % 128
    return out


def _pad_rows(a, rows):
    return jnp.pad(a, ((0, rows - a.shape[0]), (0, 0)))


VEC_NAMES = ('pool_scale', 'b_dw_c', 'ln_g_c', 'ln_b_c', 'b_pw2_c', 'ln_g_d', 'ln_b_d')
GATHERED = ('w_in', 'w_out', 'w_pw2_c', 'w_conv_a', 'w_dw_c')
GATHER_RULES = (_gather_cols, _gather_rows, _gather_rows, _gather, _gather)
SCATTER_RULES = (_scatter_cols, _scatter_rows, _scatter_rows)


def _weight_shards(shard, l):
    return [shard[n][l].astype(BF16) if n in ('w_in', 'w_out') else shard[n][l] for n in GATHERED]


def _layer_weights(shard, l, gathered):
    w_in_b, w_out_b, w_pw2, wconv_parts, wdw_parts = gathered
    wconv = wconv_parts.transpose(1, 0, 2).reshape(CONV_A, GROUP)
    wdw = wdw_parts.transpose(1, 0, 2).reshape(CONV_C, GROUP)
    wp = jnp.einsum('gcd,gh->gchd', shard['w_pool'][l], jnp.eye(4, dtype=F32)).reshape(GROUP, GROUP)
    ws = shard['w_s_d'][l] * jnp.tril(jnp.ones((SUB, SUB), F32))
    vec = jnp.stack([shard[n][l] for n in VEC_NAMES])
    small = (_pad_rows(wconv, 8), _pad_rows(wdw, HALO), _pad_rows(vec, 16), wp.astype(BF16), w_pw2.astype(BF16),
             ws.reshape(4 * SUB, SUB).astype(BF16), jnp.repeat(shard['b_s_d'][l].T, 64, axis=1))
    small_t = (wp.T.astype(BF16), w_pw2.T.astype(BF16), ws.transpose(0, 2, 1).reshape(4 * SUB, SUB).astype(BF16))
    return w_in_b, w_out_b, small, small_t


def kernel(x, c, norm_g, w_ada, b_ada, w_in, w_conv_a, w_pool, pool_scale, w_dw_c, b_dw_c, ln_g_c, ln_b_c, w_pw2_c, b_pw2_c, ln_g_d, ln_b_d, w_s_d, b_s_d, w_out, final_g, loss_target, m_norm_g, m_w_ada, m_b_ada, m_w_in, m_w_conv_a, m_w_pool, m_pool_scale, m_w_dw_c, m_b_dw_c, m_ln_g_c, m_ln_b_c, m_w_pw2_c, m_b_pw2_c, m_ln_g_d, m_ln_b_d, m_w_s_d, m_b_s_d, m_w_out, m_final_g, v_norm_g, v_w_ada, v_b_ada, v_w_in, v_w_conv_a, v_w_pool, v_pool_scale, v_w_dw_c, v_b_dw_c, v_ln_g_c, v_ln_b_c, v_w_pw2_c, v_b_pw2_c, v_ln_g_d, v_ln_b_d, v_w_s_d, v_b_s_d, v_w_out, v_final_g):
    given = dict(locals())
    shard = {n: given[n] for n in WEIGHTS}
    mom_m = {n: given['m_' + n] for n in WEIGHTS}
    mom_v = {n: given['v_' + n] for n in WEIGHTS}
    me = 4 * lax.axis_index("x") + 2 * lax.axis_index("y") + lax.axis_index("c")
    n_tok = x.shape[1]
    tile = min(TOKEN_TILE, n_tok)
    x0 = x.reshape(n_tok, D_MODEL)
    target = loss_target.reshape(n_tok, D_MODEL)
    ada_cols = w_ada.shape[2]

    first_shards = _weight_shards(shard, 0)
    c_all, w_in_first = _exchange("gather_weights", _plans([c, first_shards[0]], (_gather, GATHER_RULES[0])))

    b_cols = lax.dynamic_slice_in_dim(b_ada, me * ada_cols, ada_cols, axis=1)
    c_act, mod_cols = _modulation_columns(c_all.reshape(N_DEV, D_MODEL), w_ada, b_cols)
    (mod_all,) = _exchange("gather_modulation", _plans([mod_cols], [_gather]))
    mod = lax.dynamic_index_in_dim(mod_all, me, axis=2, keepdims=False)
    mod = mod.transpose(1, 0, 2).reshape(N_LAYERS, 3 * D_MODEL)
    shift, scale, gate = (mod[:, k * D_MODEL:(k + 1) * D_MODEL].reshape(N_LAYERS, 1, D_MODEL) for k in range(3))
    gs = norm_g.reshape(N_LAYERS, 1, D_MODEL) * (1.0 + scale)

    xs, hs, zs, ocs, layers = [x0], [], [], [], []
    for l in range(N_LAYERS):
        if l == 0:
            (h, z), rest = _in_proj(xs[0], gs[0], shift[0], w_in_first, tile,
                                    ride=_plans(first_shards[1:], GATHER_RULES[1:]))
            layers.append(_layer_weights(shard, 0, [w_in_first, *rest]))
        else:
            (h, z), _ = _in_proj(xs[l], gs[l], shift[l], layers[l][0], tile)
        _, w_out_b, small, _ = layers[l]
        hs.append(h)
        zs.append(z)
        ride = _plans(_weight_shards(shard, l + 1), GATHER_RULES) if l + 1 < N_LAYERS else None
        x_next, o_c, gathered = _mix_out(z, xs[l], gate[l], small, w_out_b, tile, ride=ride)
        xs.append(x_next)
        ocs.append(o_c)
        if ride:
            layers.append(_layer_weights(shard, l + 1, gathered))
    dx, loss_part, dfinal_g = _loss_head(xs[N_LAYERS], final_g.reshape(1, D_MODEL), target, tile)
    loss = lax.psum(loss_part[0, 0], ("x", "y", "c"))

    part = {}
    layer_parts = [None] * N_LAYERS
    slots = [None] * N_LAYERS
    for l in reversed(range(N_LAYERS)):
        w_in_b, w_out_b, small, small_t = layers[l]
        ride = _plans(layer_parts[l + 1]['big'], SCATTER_RULES) if l + 1 < N_LAYERS else None
        (dz, ycat, sums, dwp, dw2, dws, dbs), rode = _mix_bwd(zs[l], ocs[l], dx, gate[l], small, small_t, w_out_b, tile,
                                                              ride=ride)
        if ride:
            slots[l + 1] = rode
        m_out = _tokens_matmul(ycat, dx, "out_proj_tokens_matmul")
        dw_out, dgate = _out_proj_grads(m_out, w_out_b, gate[l])
        dw_in = _tokens_matmul(hs[l], dz, "in_proj_tokens_matmul")
        ride = _plans([dw_in, dw_out, dw2], SCATTER_RULES) if l == 0 else None
        (dx, dshift, dgs), rode = _norm_bwd(xs[l], dz, dx, gs[l], w_in_b, tile, ride=ride)
        if ride:
            slots[l] = rode
        layer_parts[l] = dict(
            big=[dw_in, dw_out, dw2],
            b_ada=jnp.concatenate([dshift, dgs * norm_g[l][None], dgate], axis=1)[0],
            norm_g=(dgs * (1.0 + scale[l]))[0],
            w_conv_a=sums[S_WCONV:S_WCONV + CONV_A], w_dw_c=sums[S_WDW:S_WDW + CONV_C],
            pool_scale=sums[S_PSCALE], b_dw_c=sums[S_BDW], ln_g_c=sums[S_LNGC], ln_b_c=sums[S_LNBC],
            b_pw2_c=sums[S_BPW2], ln_g_d=sums[S_LNGD], ln_b_d=sums[S_LNBD],
            w_pool=jnp.einsum('gchd,gh->gcd', dwp.reshape(4, 64, 4, 64), jnp.eye(4, dtype=F32)),
            w_s_d=dws.reshape(4, SUB, SUB) * jnp.tril(jnp.ones((SUB, SUB), F32)),
            b_s_d=dbs.reshape(SUB, 4, 64).sum(axis=-1).T)
    grad_x = dx.reshape(x.shape)
    for n in REPLICATED + CHANNEL_SHARDED:
        part[n] = dfinal_g[0] if n == 'final_g' else jnp.stack([layer_parts[l][n] for l in range(N_LAYERS)])

    small_names = REPLICATED + CHANNEL_SHARDED
    small_shapes = [part[n].shape for n in small_names]
    (small_slots,) = _exchange("gather_small_gradients", _plans([_pack([part[n] for n in small_names])], [_gather]))

    grads, deltas, new_m, new_v = {}, {}, {}, {}
    for j, n in enumerate(('w_in', 'w_out', 'w_pw2_c')):
        outs = [_adam_update(shard[n][l], slots[l][j], mom_m[n][l], mom_v[n][l], "update_" + n) for l in range(N_LAYERS)]
        grads[n], deltas[n], new_m[n], new_v[n] = (jnp.stack(o) for o in zip(*outs))

    gsum = dict(zip(small_names, _unpack(_sum_slots(small_slots, "sum_small_gradients"), small_shapes)))
    for n in CHANNEL_SHARDED:
        width = shard[n].shape[2]
        gsum[n] = lax.dynamic_slice_in_dim(gsum[n], me * width, width, axis=2)
    outs = _adam_update(_pack([shard[n] for n in small_names]), _pack([gsum[n] for n in small_names]),
                        _pack([mom_m[n] for n in small_names]), _pack([mom_v[n] for n in small_names]), "update_small")
    own_shapes = [shard[n].shape for n in small_names]
    _, d_small, m_small, v_small = (_unpack(o, own_shapes) for o in outs)
    for j, n in enumerate(small_names):
        grads[n], deltas[n], new_m[n], new_v[n] = gsum[n], d_small[j], m_small[j], v_small[j]

    dmod_all = _unpack(small_slots, small_shapes, lead=(N_DEV,))[small_names.index('b_ada')]
    dmod_cols = lax.dynamic_slice_in_dim(dmod_all, me * ada_cols, ada_cols, axis=2).transpose(1, 0, 2)
    grads['w_ada'], deltas['w_ada'], new_m['w_ada'], new_v['w_ada'] = _ada_update(
        c_act.T, dmod_cols, w_ada, m_w_ada, v_w_ada)

    return (loss, grad_x, *[grads[n] for n in WEIGHTS], *[deltas[n] for n in WEIGHTS],
            *[new_m[n] for n in WEIGHTS], *[new_v[n] for n in WEIGHTS])
```

```python
import functools
import math

import jax
import jax.numpy as jnp
from jax import lax
from jax.experimental import pallas as pl
from jax.experimental.pallas import tpu as pltpu

F32 = jnp.float32
BF16 = jnp.bfloat16

N_DEV = 8
D_MODEL = 1024
GROUP = 256
D_IN = 12 * GROUP
N_LAYERS = 2
HALO = 32
SUB = 128
WIN = SUB + HALO
TOKEN_TILE = 512
REDUCE_TILE = 1024
EPS = 1e-6
VMEM_BYTES_V7X = 64 * 1024 * 1024
VMEM_LIMIT = VMEM_BYTES_V7X - 8 * 1024 * 1024

ADAM_LR = 0.001
ADAM_B1 = 0.9
ADAM_B2 = 0.999
ADAM_EPS = 1e-08
ADAM_WD = 0.01
ADAM_STEP = 10

A_B, A_C, A_X, A_G, B_P, B_G, C_A, C_GL, C_G, D_U, D_V, D_G = range(12)
V_PSCALE, V_BDW, V_LNGC, V_LNBC, V_BPW2, V_LNGD, V_LNBD = range(7)
S_WCONV, S_PSCALE, S_BDW, S_LNGC, S_LNBC, S_BPW2, S_LNGD, S_LNBD, S_WDW = 0, 3, 4, 5, 6, 7, 8, 9, 16
N_SUMS = 64
CONV_A = 3
CONV_C = 31

WEIGHTS = ('norm_g', 'w_ada', 'b_ada', 'w_in', 'w_conv_a', 'w_pool', 'pool_scale', 'w_dw_c', 'b_dw_c', 'ln_g_c',
           'ln_b_c', 'w_pw2_c', 'b_pw2_c', 'ln_g_d', 'ln_b_d', 'w_s_d', 'b_s_d', 'w_out', 'final_g')
REPLICATED = ('norm_g', 'b_ada', 'w_pool', 'pool_scale', 'b_dw_c', 'ln_g_c', 'ln_b_c', 'b_pw2_c', 'ln_g_d', 'ln_b_d',
              'w_s_d', 'b_s_d', 'final_g')
CHANNEL_SHARDED = ('w_conv_a', 'w_dw_c')


def _params(semantics, vmem=VMEM_LIMIT):
    return pltpu.CompilerParams(dimension_semantics=semantics, vmem_limit_bytes=vmem)


def _cols(g):
    return slice(g * GROUP, (g + 1) * GROUP)


def _full(shape):
    return pl.BlockSpec(shape, lambda *_: (0,) * len(shape))


def _silu(x):
    s = jax.nn.sigmoid(x)
    return x * s, s


def _dsilu(x, s):
    return s * (1.0 + x * (1.0 - s))


_GELU_C0 = math.sqrt(2.0 / math.pi)
_GELU_C1 = 0.044715


def _gelu(x):
    th = jnp.tanh(_GELU_C0 * (x + _GELU_C1 * (x * x * x)))
    return 0.5 * x * (1.0 + th), th


def _dgelu(x, th):
    return 0.5 * (1.0 + th) + 0.5 * x * (1.0 - th * th) * (_GELU_C0 * (1.0 + 3.0 * _GELU_C1 * (x * x)))


def _layer_norm(x):
    mu = jnp.mean(x, axis=-1, keepdims=True)
    xc = x - mu
    rstd = lax.rsqrt(jnp.mean(xc * xc, axis=-1, keepdims=True) + EPS)
    return xc * rstd, rstd


def _layer_norm_bwd(dn, n, rstd):
    return rstd * (dn - jnp.mean(dn, axis=-1, keepdims=True) - n * jnp.mean(dn * n, axis=-1, keepdims=True))


def _shift_rows(a, k):
    k = k % a.shape[0]
    return a if k == 0 else pltpu.roll(a, k, 0)


def _row_sum8(a):
    s = a[0:8]
    for m in range(1, a.shape[0] // 8):
        s = s + a[8 * m:8 * m + 8]
    return s


def _lane():
    return lax.broadcasted_iota(jnp.int32, (SUB, GROUP), 1)


def _by_quarter(lane, parts):
    return jnp.where(lane < 64, parts[0], jnp.where(lane < 128, parts[1], jnp.where(lane < 192, parts[2], parts[3])))


def _conv_inputs(z_ref, rows):
    def f(g):
        return z_ref[rows, _cols(g)].astype(F32)
    return f(A_C) * f(A_X), f(B_P), f(C_A) * jax.nn.sigmoid(f(C_GL))


def _fill_past(past_ref, zh_ref, zm_ref, is_first, tile):
    parts = _conv_inputs(zh_ref, slice(None))
    for n, a in enumerate(parts):
        past_ref[0:HALO, _cols(n)] = jnp.where(is_first, 0.0, a)

    def body(j, carry):
        r0 = pl.multiple_of(j * SUB, SUB)
        for n, a in enumerate(_conv_inputs(zm_ref, pl.ds(r0, SUB))):
            past_ref[pl.ds(r0 + HALO, SUB), _cols(n)] = a
        return carry

    lax.fori_loop(0, tile // SUB, body, 0)


def _short_conv_taps(qw):
    return [_shift_rows(qw, CONV_A - 1 - k)[HALO:WIN] for k in range(CONV_A)]


def _window_sums(pw, lane):
    s2 = pw + _shift_rows(pw, 1)
    s4 = s2 + _shift_rows(s2, 2)
    s8 = s4 + _shift_rows(s4, 4)
    s16 = s8 + _shift_rows(s8, 8)
    return _by_quarter(lane, [s[HALO:WIN] for s in (s2, s4, s8, s16)])


def _inv_count(lane, t_first):
    width = _by_quarter(lane, [2.0, 4.0, 8.0, 16.0])
    t = lax.broadcasted_iota(jnp.int32, (SUB, GROUP), 0) + t_first
    return 1.0 / jnp.minimum((t + 1).astype(F32), width)


def _forward_window_sums(ew, lane):
    n = ew.shape[0]
    f2 = ew + _shift_rows(ew, n - 1)
    f4 = f2 + _shift_rows(f2, n - 2)
    f8 = f4 + _shift_rows(f4, n - 4)
    f16 = f8 + _shift_rows(f8, n - 8)
    return _by_quarter(lane, [f[0:SUB] for f in (f2, f4, f8, f16)])


def _mixers_forward(zc, win, t_first, wc_ref, wdw_ref, vec_ref, wp_ref, w2_ref, ws_ref, bs_ref, o_c=None):
    lane = _lane()

    def vec(n):
        return vec_ref[n:n + 1, :]

    taps = _short_conv_taps(win(0))
    o_a = wc_ref[0:1, :] * taps[0] + wc_ref[1:2, :] * taps[1] + wc_ref[2:3, :] * taps[2]
    a_b, a_g = zc(A_B), zc(A_G)
    sg_a, s_a = _silu(a_g)
    y_a = a_b * o_a * sg_a

    pw = win(1)
    ic = _inv_count(lane, t_first)
    pooled = _window_sums(pw, lane) * ic - pw[HALO:WIN]
    pooled_b = pooled.astype(BF16)
    y0_b = jnp.dot(pooled_b, wp_ref[...], preferred_element_type=F32)
    b_g = zc(B_G)
    sg_b, s_b = _silu(b_g)
    y_b = y0_b * vec(V_PSCALE) * sg_b

    hw = win(2)
    if o_c is None:
        o_c = wdw_ref[CONV_C - 1:CONV_C, :] * hw[HALO:WIN] + vec(V_BDW)
        for k in range(CONV_C - 1):
            o_c = o_c + wdw_ref[k:k + 1, :] * _shift_rows(hw, CONV_C - 1 - k)[HALO:WIN]
    n_c, rstd_c = _layer_norm(o_c)
    ln_c = n_c * vec(V_LNGC) + vec(V_LNBC)
    sl_c, ssl_c = _silu(ln_c)
    sl_b = sl_c.astype(BF16)
    yc = jnp.dot(sl_b, w2_ref[...], preferred_element_type=F32) + vec(V_BPW2)
    c_g = zc(C_G)
    sg_c, s_c = _silu(c_g)
    y_c = yc * sg_c

    d_u, d_v, d_g = zc(D_U), zc(D_V), zc(D_G)
    u, th_u = _gelu(d_u)
    gv, th_v = _gelu(d_v)
    n_d, rstd_d = _layer_norm(gv)
    v_b = (n_d * vec(V_LNGD) + vec(V_LNBD)).astype(BF16)
    r = jnp.dot(ws_ref[...], v_b, preferred_element_type=F32)
    mixed = _by_quarter(lane, [r[h * SUB:(h + 1) * SUB] for h in range(4)]) + bs_ref[...]
    sg_d, s_d = _silu(d_g)
    y_d = u * mixed * sg_d

    saved = dict(lane=lane, taps=taps, o_a=o_a, a_b=a_b, a_g=a_g, sg_a=sg_a, s_a=s_a,
                 ic=ic, pooled_b=pooled_b, y0_b=y0_b, b_g=b_g, sg_b=sg_b, s_b=s_b,
                 hw=hw, o_c=o_c, n_c=n_c, rstd_c=rstd_c, ln_c=ln_c, ssl_c=ssl_c, sl_b=sl_b, yc=yc, c_g=c_g, sg_c=sg_c, s_c=s_c,
                 d_u=d_u, d_v=d_v, d_g=d_g, u=u, th_u=th_u, th_v=th_v, n_d=n_d, rstd_d=rstd_d, v_b=v_b, mixed=mixed,
                 sg_d=sg_d, s_d=s_d)
    return (y_a, y_b, y_c, y_d), saved


def _in_proj(x, gs, shift, w_in_b, tile, ride=None):
    n_tok = x.shape[0]

    def body(x_ref, gs_ref, sh_ref, w_ref, h_ref, z_ref):
        xv = x_ref[...]
        r = lax.rsqrt(jnp.mean(xv * xv, axis=-1, keepdims=True) + EPS)
        h = ((xv * r) * gs_ref[...] + sh_ref[...]).astype(BF16)
        h_ref[...] = h
        for j in range(D_IN // D_MODEL):
            cs = slice(j * D_MODEL, (j + 1) * D_MODEL)
            z_ref[:, cs] = jnp.dot(h, w_ref[:, cs], preferred_element_type=F32).astype(BF16)

    return _tiled_call(
        body, (x, gs, shift, w_in_b), name="in_proj", grid=(n_tok // tile,),
        in_specs=[pl.BlockSpec((tile, D_MODEL), lambda i: (i, 0)), _full((1, D_MODEL)), _full((1, D_MODEL)),
                  _full((D_MODEL, D_IN))],
        out_specs=[pl.BlockSpec((tile, D_MODEL), lambda i: (i, 0)), pl.BlockSpec((tile, D_IN), lambda i: (i, 0))],
        out_shape=[jax.ShapeDtypeStruct((n_tok, D_MODEL), BF16), jax.ShapeDtypeStruct((n_tok, D_IN), BF16)],
        ride=ride)


def _small_specs(with_transposes):
    specs = [_full((8, GROUP)), _full((HALO, GROUP)), _full((16, GROUP)), _full((GROUP, GROUP)), _full((GROUP, GROUP)),
             _full((4 * SUB, SUB)), _full((SUB, GROUP))]
    if with_transposes:
        specs += [_full((GROUP, GROUP)), _full((GROUP, GROUP)), _full((4 * SUB, SUB))]
    return specs


def _mix_out(z, x, gate, small, w_out_b, tile, ride=None):
    n_tok = x.shape[0]
    n_tiles = n_tok // tile
    n_sub = tile // SUB
    cw = D_MODEL // n_sub
    per_halo = tile // HALO

    def cur(i):
        return jnp.minimum(i, n_tiles - 1)

    def prev(i):
        return jnp.maximum(i - 1, 0)

    def body(zm_ref, zh_ref, x_ref, gate_ref, wc_ref, wdw_ref, vec_ref, wp_ref, w2_ref, ws_ref, bs_ref, wout_ref,
             xo_ref, oc_ref, past_ref, ycat_ref, ycat_prev_ref):
        i = pl.program_id(0)
        t = cur(i)

        @pl.when(i == 0)
        def _():
            ycat_prev_ref[...] = jnp.zeros_like(ycat_prev_ref)

        _fill_past(past_ref, zh_ref, zm_ref, t == 0, tile)
        for j in range(n_sub):
            cs = slice(j * cw, (j + 1) * cw)
            y = jnp.dot(ycat_prev_ref[...], wout_ref[:, cs], preferred_element_type=F32)
            xo_ref[:, cs] = x_ref[:, cs] + gate_ref[:, cs] * y
            rows = slice(j * SUB, (j + 1) * SUB)
            ys, s = _mixers_forward(
                lambda g: zm_ref[rows, _cols(g)].astype(F32), lambda n: past_ref[j * SUB:j * SUB + WIN, _cols(n)],
                t * tile + j * SUB, wc_ref, wdw_ref, vec_ref, wp_ref, w2_ref, ws_ref, bs_ref)
            for n, y in enumerate(ys):
                ycat_ref[rows, _cols(n)] = y.astype(BF16)
            oc_ref[rows, :] = s["o_c"]
        ycat_prev_ref[...] = ycat_ref[...]

    (x_next, o_c), rode = _tiled_call(
        body, (z, z, x, gate, *small, w_out_b), name="mix_out", grid=(n_tiles + 1,),
        in_specs=[pl.BlockSpec((tile, D_IN), lambda i: (cur(i), 0)),
                  pl.BlockSpec((HALO, D_IN), lambda i: (jnp.maximum(cur(i) * per_halo - 1, 0), 0)),
                  pl.BlockSpec((tile, D_MODEL), lambda i: (prev(i), 0)), _full((1, D_MODEL)),
                  *_small_specs(False), _full((D_MODEL, D_MODEL))],
        out_specs=[pl.BlockSpec((tile, D_MODEL), lambda i: (prev(i), 0)), pl.BlockSpec((tile, GROUP), lambda i: (cur(i), 0))],
        out_shape=[jax.ShapeDtypeStruct((n_tok, D_MODEL), F32), jax.ShapeDtypeStruct((n_tok, GROUP), F32)],
        scratch_shapes=[pltpu.VMEM((tile + HALO, 3 * GROUP), F32), pltpu.VMEM((tile, D_MODEL), BF16),
                        pltpu.VMEM((tile, D_MODEL), BF16)], ride=ride)
    return x_next, o_c, rode


def _mix_bwd(z, o_c, dx_next, gate, small, small_t, w_out_b, tile, ride=None):
    n_tok = z.shape[0]
    n_tiles = n_tok // tile
    n_sub = tile // SUB
    cw = D_MODEL // n_sub
    per_halo = tile // HALO
    nt_dims = (((1,), (1,)), ((), ()))

    def tile_of(i):
        return n_tiles - 1 - i

    def next_tile_of(i):
        return jnp.maximum(n_tiles - 2 - i, 0)

    def body(zm_ref, zh_ref, oc_ref, dxn_ref, dxn_next_ref, gate_ref, wc_ref, wdw_ref, vec_ref, wp_ref, w2_ref, ws_ref,
             bs_ref, wpt_ref, w2t_ref, wst_ref, wout_ref,
             dz_ref, ycat_ref, sums_ref, dwp_ref, dw2_ref, dws_ref, dbs_ref,
             past_ref, future_ref, dy_ref, dy_next_ref, acc_ref):
        i = pl.program_id(0)
        t = tile_of(i)

        @pl.when(i == 0)
        def _():
            acc_ref[...] = jnp.zeros_like(acc_ref)
            dwp_ref[...] = jnp.zeros_like(dwp_ref)
            dw2_ref[...] = jnp.zeros_like(dw2_ref)
            dws_ref[...] = jnp.zeros_like(dws_ref)
            dbs_ref[...] = jnp.zeros_like(dbs_ref)
            future_ref[tile:tile + HALO, :] = jnp.zeros((HALO, 3 * GROUP), F32)
            dy_ref[...] = lax.dot_general((dxn_ref[...] * gate_ref[...]).astype(BF16), wout_ref[...], nt_dims,
                                        preferred_element_type=F32)

        _fill_past(past_ref, zh_ref, zm_ref, t == 0, tile)
        dyb_next = (dxn_next_ref[...] * gate_ref[...]).astype(BF16)

        def vec(n):
            return vec_ref[n:n + 1, :]

        for jj in range(n_sub):
            j = n_sub - 1 - jj
            r0 = j * SUB
            rows = slice(r0, r0 + SUB)

            def zc(g):
                return zm_ref[rows, _cols(g)].astype(F32)

            def add(n, a):
                acc_ref[n] = acc_ref[n] + _row_sum8(a)

            def put(g, a):
                dz_ref[rows, _cols(g)] = a.astype(BF16)

            def future_window(n, a):
                future_ref[rows, _cols(n)] = a
                return future_ref[r0:r0 + WIN, _cols(n)]

            ys, s = _mixers_forward(zc, lambda n: past_ref[r0:r0 + WIN, _cols(n)], t * tile + r0,
                                    wc_ref, wdw_ref, vec_ref, wp_ref, w2_ref, ws_ref, bs_ref, o_c=oc_ref[rows, :])
            for n, y in enumerate(ys):
                ycat_ref[rows, _cols(n)] = y.astype(BF16)
            lane = s["lane"]
            ks = slice(jj * cw, (jj + 1) * cw)
            dy_next_ref[:, ks] = lax.dot_general(dyb_next, wout_ref[ks, :], nt_dims, preferred_element_type=F32)

            dy = dy_ref[rows,_cols(0)]
            put(A_B, dy * s["o_a"] * s["sg_a"])
            put(A_G, dy * s["a_b"] * s["o_a"] * _dsilu(s["a_g"], s["s_a"]))
            do = dy * s["a_b"] * s["sg_a"]
            for k in range(CONV_A):
                add(S_WCONV + k, do * s["taps"][k])
            dow = future_window(0, do)
            dq = wc_ref[CONV_A - 1:CONV_A, :] * dow[0:SUB]
            for k in range(CONV_A - 1):
                dq = dq + wc_ref[k:k + 1, :] * _shift_rows(dow, WIN - (CONV_A - 1 - k))[0:SUB]
            put(A_C, dq * zc(A_X))
            put(A_X, dq * zc(A_C))

            dy = dy_ref[rows,_cols(1)]
            put(B_G, dy * (s["y0_b"] * vec(V_PSCALE)) * _dsilu(s["b_g"], s["s_b"]))
            dyb = dy * s["sg_b"]
            add(S_PSCALE, dyb * s["y0_b"])
            dpw_b = (dyb * vec(V_PSCALE)).astype(BF16)
            dwp_ref[...] += lax.dot_general(s["pooled_b"], dpw_b, (((0,), (0,)), ((), ())), preferred_element_type=F32)
            dpooled = jnp.dot(dpw_b, wpt_ref[...], preferred_element_type=F32)
            ew = future_window(1, dpooled * s["ic"])
            put(B_P, _forward_window_sums(ew, lane) - dpooled)

            dy = dy_ref[rows,_cols(2)]
            put(C_G, dy * s["yc"] * _dsilu(s["c_g"], s["s_c"]))
            dyc = dy * s["sg_c"]
            add(S_BPW2, dyc)
            dyc_b = dyc.astype(BF16)
            dw2_ref[...] += lax.dot_general(s["sl_b"], dyc_b, (((0,), (0,)), ((), ())), preferred_element_type=F32)
            dln = jnp.dot(dyc_b, w2t_ref[...], preferred_element_type=F32) * _dsilu(s["ln_c"], s["ssl_c"])
            add(S_LNGC, dln * s["n_c"])
            add(S_LNBC, dln)
            do = _layer_norm_bwd(dln * vec(V_LNGC), s["n_c"], s["rstd_c"])
            add(S_BDW, do)
            hw = s["hw"]
            for k in range(CONV_C):
                add(S_WDW + k, do * _shift_rows(hw, CONV_C - 1 - k)[HALO:WIN])
            dow = future_window(2, do)
            dhc = wdw_ref[CONV_C - 1:CONV_C, :] * dow[0:SUB]
            for k in range(CONV_C - 1):
                dhc = dhc + wdw_ref[k:k + 1, :] * _shift_rows(dow, WIN - (CONV_C - 1 - k))[0:SUB]
            c_a = zc(C_A)
            sgl = jax.nn.sigmoid(zc(C_GL))
            put(C_A, dhc * sgl)
            put(C_GL, dhc * c_a * sgl * (1.0 - sgl))

            dy = dy_ref[rows,_cols(3)]
            put(D_G, dy * s["u"] * s["mixed"] * _dsilu(s["d_g"], s["s_d"]))
            put(D_U, dy * s["mixed"] * s["sg_d"] * _dgelu(s["d_u"], s["th_u"]))
            dmixed = dy * s["u"] * s["sg_d"]
            dbs_ref[...] += dmixed
            by_head = jnp.concatenate(
                [jnp.where((lane >= 64 * h) & (lane < 64 * h + 64), dmixed, 0.0) for h in range(4)], axis=0).astype(BF16)
            dws_ref[...] += lax.dot_general(by_head, s["v_b"], (((1,), (1,)), ((), ())), preferred_element_type=F32)
            rv = jnp.dot(wst_ref[...], dmixed.astype(BF16), preferred_element_type=F32)
            dv = _by_quarter(lane, [rv[h * SUB:(h + 1) * SUB] for h in range(4)])
            add(S_LNGD, dv * s["n_d"])
            add(S_LNBD, dv)
            dgv = _layer_norm_bwd(dv * vec(V_LNGD), s["n_d"], s["rstd_d"])
            put(D_V, dgv * _dgelu(s["d_v"], s["th_v"]))

        future_ref[tile:tile + HALO, :] = future_ref[0:HALO, :]
        dy_ref[...] = dy_next_ref[...]

        @pl.when(i == n_tiles - 1)
        def _():
            for n in range(N_SUMS):
                sums_ref[n:n + 1, :] = jnp.sum(acc_ref[n], axis=0, keepdims=True)

    return _tiled_call(
        body, (z, z, o_c, dx_next, dx_next, gate, *small, *small_t, w_out_b), name="mix_bwd", grid=(n_tiles,),
        in_specs=[pl.BlockSpec((tile, D_IN), lambda i: (tile_of(i), 0)),
                  pl.BlockSpec((HALO, D_IN), lambda i: (jnp.maximum(tile_of(i) * per_halo - 1, 0), 0)),
                  pl.BlockSpec((tile, GROUP), lambda i: (tile_of(i), 0)),
                  pl.BlockSpec((tile, D_MODEL), lambda i: (tile_of(i), 0)),
                  pl.BlockSpec((tile, D_MODEL), lambda i: (next_tile_of(i), 0)), _full((1, D_MODEL)),
                  *_small_specs(True), _full((D_MODEL, D_MODEL))],
        out_specs=[pl.BlockSpec((tile, D_IN), lambda i: (tile_of(i), 0)),
                   pl.BlockSpec((tile, D_MODEL), lambda i: (tile_of(i), 0)),
                   _full((N_SUMS, GROUP)), _full((GROUP, GROUP)), _full((GROUP, GROUP)), _full((4 * SUB, SUB)),
                   _full((SUB, GROUP))],
        out_shape=[jax.ShapeDtypeStruct((n_tok, D_IN), BF16), jax.ShapeDtypeStruct((n_tok, D_MODEL), BF16),
                   jax.ShapeDtypeStruct((N_SUMS, GROUP), F32), jax.ShapeDtypeStruct((GROUP, GROUP), F32),
                   jax.ShapeDtypeStruct((GROUP, GROUP), F32), jax.ShapeDtypeStruct((4 * SUB, SUB), F32),
                   jax.ShapeDtypeStruct((SUB, GROUP), F32)],
        scratch_shapes=[pltpu.VMEM((tile + HALO, 3 * GROUP), F32), pltpu.VMEM((tile + HALO, 3 * GROUP), F32),
                        pltpu.VMEM((tile, D_MODEL), F32), pltpu.VMEM((tile, D_MODEL), F32),
                        pltpu.VMEM((N_SUMS, 8, GROUP), F32)], ride=ride)


def _norm_bwd(x, dz, dx_next, gs, w_in_b, tile, ride=None):
    n_tok = x.shape[0]
    n_tiles = n_tok // tile

    def body(x_ref, dz_ref, dxn_ref, gs_ref, w_ref, dx_ref, dsh_ref, dgs_ref, acc_ref):
        i = pl.program_id(0)

        @pl.when(i == 0)
        def _():
            acc_ref[...] = jnp.zeros_like(acc_ref)

        dh = lax.dot_general(dz_ref[...], w_ref[...], (((1,), (1,)), ((), ())), preferred_element_type=F32)
        xv = x_ref[...]
        r = lax.rsqrt(jnp.mean(xv * xv, axis=-1, keepdims=True) + EPS)
        xn = xv * r
        acc_ref[0] = acc_ref[0] + _row_sum8(dh)
        acc_ref[1] = acc_ref[1] + _row_sum8(dh * xn)
        dxn = dh * gs_ref[...]
        dx_ref[...] = dxn_ref[...] + r * (dxn - xn * jnp.mean(dxn * xn, axis=-1, keepdims=True))

        @pl.when(i == n_tiles - 1)
        def _():
            dsh_ref[...] = jnp.sum(acc_ref[0], axis=0, keepdims=True)
            dgs_ref[...] = jnp.sum(acc_ref[1], axis=0, keepdims=True)

    return _tiled_call(
        body, (x, dz, dx_next, gs, w_in_b), name="norm_bwd", grid=(n_tiles,),
        in_specs=[pl.BlockSpec((tile, D_MODEL), lambda i: (i, 0)), pl.BlockSpec((tile, D_IN), lambda i: (i, 0)),
                  pl.BlockSpec((tile, D_MODEL), lambda i: (i, 0)), _full((1, D_MODEL)), _full((D_MODEL, D_IN))],
        out_specs=[pl.BlockSpec((tile, D_MODEL), lambda i: (i, 0)), _full((1, D_MODEL)), _full((1, D_MODEL))],
        out_shape=[jax.ShapeDtypeStruct((n_tok, D_MODEL), F32), jax.ShapeDtypeStruct((1, D_MODEL), F32),
                   jax.ShapeDtypeStruct((1, D_MODEL), F32)],
        scratch_shapes=[pltpu.VMEM((2, 8, D_MODEL), F32)], ride=ride)


def _loss_head(x, final_g, target, tile):
    n_tok = x.shape[0]
    n_tiles = n_tok // tile

    def body(x_ref, g_ref, t_ref, dx_ref, loss_ref, dg_ref, acc_ref):
        i = pl.program_id(0)

        @pl.when(i == 0)
        def _():
            acc_ref[...] = jnp.zeros_like(acc_ref)

        xv = x_ref[...]
        r = lax.rsqrt(jnp.mean(xv * xv, axis=-1, keepdims=True) + EPS)
        xn = xv * r
        err = xn * g_ref[...] - t_ref[...]
        acc_ref[0] = acc_ref[0] + _row_sum8(err * err)
        dy = err * (1.0 / D_MODEL)
        acc_ref[1] = acc_ref[1] + _row_sum8(dy * xn)
        a = dy * g_ref[...]
        dx_ref[...] = r * (a - xn * jnp.mean(a * xn, axis=-1, keepdims=True))

        @pl.when(i == n_tiles - 1)
        def _():
            loss_ref[...] = jnp.full((8, 128), 0.5 / D_MODEL, F32) * jnp.sum(acc_ref[0])
            dg_ref[...] = jnp.sum(acc_ref[1], axis=0, keepdims=True)

    return pl.pallas_call(
        body, name="loss_head", grid=(n_tiles,),
        in_specs=[pl.BlockSpec((tile, D_MODEL), lambda i: (i, 0)), _full((1, D_MODEL)),
                  pl.BlockSpec((tile, D_MODEL), lambda i: (i, 0))],
        out_specs=[pl.BlockSpec((tile, D_MODEL), lambda i: (i, 0)), _full((8, 128)), _full((1, D_MODEL))],
        out_shape=[jax.ShapeDtypeStruct((n_tok, D_MODEL), F32), jax.ShapeDtypeStruct((8, 128), F32),
                   jax.ShapeDtypeStruct((1, D_MODEL), F32)],
        scratch_shapes=[pltpu.VMEM((2, 8, D_MODEL), F32)],
        compiler_params=_params(("arbitrary",)),
    )(x, final_g, target)


def _tokens_matmul(a, b, name, out_dtype=F32, ride=None):
    n_tok, ka = a.shape
    nb = b.shape[1]
    tk = min(REDUCE_TILE, n_tok)
    cb = min(D_MODEL, nb)
    n_steps = n_tok // tk

    def body(a_ref, b_ref, o_ref, acc_ref):
        i = pl.program_id(1)

        @pl.when(i == 0)
        def _():
            acc_ref[...] = jnp.zeros_like(acc_ref)

        acc_ref[...] += lax.dot_general(a_ref[...], b_ref[...].astype(BF16), (((0,), (0,)), ((), ())),
                                        preferred_element_type=F32)

        @pl.when(i == n_steps - 1)
        def _():
            o_ref[...] = acc_ref[...].astype(out_dtype)

    (out,), rode = _tiled_call(
        body, (a, b), name=name, grid=(nb // cb, n_steps),
        in_specs=[pl.BlockSpec((tk, ka), lambda j, i: (i, 0)), pl.BlockSpec((tk, cb), lambda j, i: (i, j))],
        out_specs=[pl.BlockSpec((ka, cb), lambda j, i: (0, j))],
        out_shape=[jax.ShapeDtypeStruct((ka, nb), out_dtype)],
        scratch_shapes=[pltpu.VMEM((ka, cb), F32)], ride=ride)
    return out, rode


def _out_proj_grads(m, w_out_b, gate):
    rb = 256
    n_blocks = D_MODEL // rb

    def body(m_ref, w_ref, gate_ref, dw_ref, dgate_ref, acc_ref):
        i = pl.program_id(0)

        @pl.when(i == 0)
        def _():
            acc_ref[...] = jnp.zeros_like(acc_ref)

        mv = m_ref[...]
        dw_ref[...] = (mv * gate_ref[...]).astype(BF16)
        acc_ref[...] += _row_sum8(mv * w_ref[...].astype(F32))

        @pl.when(i == n_blocks - 1)
        def _():
            dgate_ref[...] = jnp.sum(acc_ref[...], axis=0, keepdims=True)

    return pl.pallas_call(
        body, name="out_proj_grads", grid=(n_blocks,),
        in_specs=[pl.BlockSpec((rb, D_MODEL), lambda i: (i, 0)), pl.BlockSpec((rb, D_MODEL), lambda i: (i, 0)),
                  _full((1, D_MODEL))],
        out_specs=[pl.BlockSpec((rb, D_MODEL), lambda i: (i, 0)), _full((1, D_MODEL))],
        out_shape=[jax.ShapeDtypeStruct((D_MODEL, D_MODEL), BF16), jax.ShapeDtypeStruct((1, D_MODEL), F32)],
        scratch_shapes=[pltpu.VMEM((8, D_MODEL), F32)],
        compiler_params=_params(("arbitrary",)),
    )(m, w_out_b, gate)


def _modulation_columns(c_all, w_ada, b_cols):
    cols = w_ada.shape[2]

    def body(c_ref, w_ref, b_ref, ca_ref, mod_ref):
        ca, _ = _silu(c_ref[...])
        ca_ref[...] = ca
        for l in range(N_LAYERS):
            mod_ref[l] = jnp.dot(ca, w_ref[l], precision=lax.Precision.HIGHEST, preferred_element_type=F32) + b_ref[l:l + 1, :]

    return pl.pallas_call(
        body, name="modulation_columns",
        out_shape=[jax.ShapeDtypeStruct((N_DEV, D_MODEL), F32), jax.ShapeDtypeStruct((N_LAYERS, N_DEV, cols), F32)],
        compiler_params=pltpu.CompilerParams(vmem_limit_bytes=VMEM_LIMIT),
    )(c_all, w_ada, b_cols)


def _adam(w, g, m, v):
    m2 = ADAM_B1 * m + (1.0 - ADAM_B1) * g
    v2 = ADAM_B2 * v + (1.0 - ADAM_B2) * (g * g)
    m_hat = m2 / (1.0 - ADAM_B1 ** ADAM_STEP)
    v_hat = v2 / (1.0 - ADAM_B2 ** ADAM_STEP)
    return -ADAM_LR * (m_hat / (jnp.sqrt(v_hat) + ADAM_EPS) + ADAM_WD * w), m2, v2


def _row_block(rows, cols, slots):
    target = max(8, (1 << 19) // (cols * max(slots, 1)))
    rb = rows
    while rb > target and rb % 2 == 0 and (rb // 2) % 8 == 0:
        rb //= 2
    return rb


def _sum_slots(slots, name):
    _, rows, cols = slots.shape
    rb = _row_block(rows, cols, N_DEV)

    def body(s_ref, g_ref):
        g = s_ref[0]
        for q in range(1, N_DEV):
            g = g + s_ref[q]
        g_ref[...] = g

    return pl.pallas_call(
        body, name=name, grid=(rows // rb,),
        in_specs=[pl.BlockSpec((N_DEV, rb, cols), lambda i: (0, i, 0))],
        out_specs=pl.BlockSpec((rb, cols), lambda i: (i, 0)),
        out_shape=jax.ShapeDtypeStruct((rows, cols), F32),
        compiler_params=_params(("parallel",)),
    )(slots)


def _adam_update(w, g, m, v, name):
    rows, cols = w.shape
    slotted = g.ndim == 3
    rb = _row_block(rows, cols, N_DEV if slotted else 1)

    def body(w_ref, g_ref, m_ref, v_ref, go_ref, d_ref, mo_ref, vo_ref):
        if slotted:
            gv = g_ref[0].astype(F32)
            for q in range(1, N_DEV):
                gv = gv + g_ref[q].astype(F32)
        else:
            gv = g_ref[...]
        go_ref[...] = gv
        d_ref[...], mo_ref[...], vo_ref[...] = _adam(w_ref[...], gv, m_ref[...], v_ref[...])

    blk = pl.BlockSpec((rb, cols), lambda i: (i, 0))
    g_blk = pl.BlockSpec((N_DEV, rb, cols), lambda i: (0, i, 0)) if slotted else blk
    return pl.pallas_call(
        body, name=name, grid=(rows // rb,),
        in_specs=[blk, g_blk, blk, blk], out_specs=[blk] * 4,
        out_shape=[jax.ShapeDtypeStruct((rows, cols), F32)] * 4,
        compiler_params=_params(("parallel",)),
    )(w, g, m, v)


def _ada_update(ca_t, dmod_cols, w, m, v):
    _, rows, cols = w.shape

    def body(ca_ref, dm_ref, w_ref, m_ref, v_ref, g_ref, d_ref, mo_ref, vo_ref):
        g = ca_ref[:, 0:1] * dm_ref[0, 0:1, :]
        for b in range(1, N_DEV):
            g = g + ca_ref[:, b:b + 1] * dm_ref[0, b:b + 1, :]
        g_ref[0] = g
        d_ref[0], mo_ref[0], vo_ref[0] = _adam(w_ref[0], g, m_ref[0], v_ref[0])

    blk = pl.BlockSpec((1, rows, cols), lambda l: (l, 0, 0))
    return pl.pallas_call(
        body, name="ada_update", grid=(N_LAYERS,),
        in_specs=[_full((rows, N_DEV)), pl.BlockSpec((1, N_DEV, cols), lambda l: (l, 0, 0)), blk, blk, blk],
        out_specs=[blk] * 4, out_shape=[jax.ShapeDtypeStruct(w.shape, F32)] * 4,
        compiler_params=_params(("parallel",)),
    )(ca_t, dmod_cols, w, m, v)


def _exchange_sems(n):
    return [pltpu.SemaphoreType.DMA((n, N_DEV - 1)), pltpu.SemaphoreType.DMA((n, N_DEV - 1)),
            pltpu.SemaphoreType.DMA((n,))]


def _exchange_copies(plans, srcs, outs, sems, receiving, only=None):
    send_sems, recv_sems, local_sems = sems
    x, y, c = lax.axis_index("x"), lax.axis_index("y"), lax.axis_index("c")
    me = 4 * x + 2 * y + c

    def remote(i, k, incoming):
        _, o, send, land = plans[i]
        px = 1 - x if k & 4 else x
        py = 1 - y if k & 2 else y
        pc = 1 - c if k & 1 else c
        p = 4 * px + 2 * py + pc
        return pltpu.make_async_remote_copy(
            src_ref=send(srcs[i], p), dst_ref=land(outs[o], p if incoming else me),
            send_sem=send_sems.at[i, k - 1], recv_sem=recv_sems.at[i, k - 1],
            device_id=(px, py, pc), device_id_type=pl.DeviceIdType.MESH)

    which = range(len(plans)) if only is None else only
    pairs = [(i, k) for k in range(1, N_DEV) for i in which]
    local = [pltpu.make_async_copy(plans[i][2](srcs[i], me), plans[i][3](outs[plans[i][1]], me), local_sems.at[i])
             for i in which]
    return local, [remote(i, k, False) for i, k in pairs], [remote(i, k, True) for i, k in pairs] if receiving else []


def _exchange_start(plans, srcs, outs, sems, only=None):
    local, outgoing, _ = _exchange_copies(plans, srcs, outs, sems, False, only)
    for cp in local + outgoing:
        cp.start()


def _exchange_wait(plans, srcs, outs, sems, only=None):
    local, outgoing, incoming = _exchange_copies(plans, srcs, outs, sems, True, only)
    for cp in incoming:
        cp.wait_recv()
    for cp in outgoing:
        cp.wait_send()
    for cp in local:
        cp.wait()


def _exchange(name, ride):
    out_shapes, plans = ride
    n = len(plans)
    hbm = pl.BlockSpec(memory_space=pltpu.HBM)

    def body(*refs):
        srcs, outs, sems = refs[:n], refs[n:n + len(out_shapes)], refs[n + len(out_shapes):]
        _exchange_start(plans, srcs, outs, sems)
        _exchange_wait(plans, srcs, outs, sems)

    return pl.pallas_call(
        body, name=name, in_specs=[hbm] * n, out_specs=[hbm] * len(out_shapes), out_shape=list(out_shapes),
        scratch_shapes=_exchange_sems(n),
    )(*[p[0] for p in plans])


def _finish_exchange(big, big_rules, packed, dmod):
    n_rows = packed.shape[0]
    r = n_rows // N_DEV
    shapes, plans = _plans([*big, packed, dmod], [*big_rules, _scatter_rows, _gather])
    n_first = len(plans)
    i_small = n_first - 2
    _, send, land = _gather_rows(jax.ShapeDtypeStruct((r, 128), F32))
    plans = plans + [(None, len(shapes), send, land)]
    shapes = shapes + [jax.ShapeDtypeStruct((n_rows, 128), F32)]
    first = [i for i in range(n_first) if i != i_small]
    hbm = pl.BlockSpec(memory_space=pltpu.HBM)

    def body(*refs):
        srcs, outs = list(refs[:n_first]), refs[n_first:n_first + len(shapes)]
        parts_ref, sum_ref, local_sem = refs[n_first + len(shapes):n_first + len(shapes) + 3]
        sems = refs[n_first + len(shapes) + 3:]
        srcs.append(sum_ref)
        _exchange_start(plans, srcs, outs, sems, only=range(n_first))
        _exchange_wait(plans, srcs, outs, sems, only=[i_small])
        cp = pltpu.make_async_copy(outs[i_small], parts_ref, local_sem)
        cp.start()
        cp.wait()
        g = parts_ref[0]
        for q in range(1, N_DEV):
            g = g + parts_ref[q]
        sum_ref[...] = g
        _exchange_start(plans, srcs, outs, sems, only=[n_first])
        _exchange_wait(plans, srcs, outs, sems, only=[n_first])
        _exchange_wait(plans, srcs, outs, sems, only=first)

    res = pl.pallas_call(
        body, name="finish_exchange", in_specs=[hbm] * n_first, out_specs=[hbm] * len(shapes), out_shape=shapes,
        scratch_shapes=[pltpu.VMEM((N_DEV, r, 128), F32), pltpu.VMEM((r, 128), F32), pltpu.SemaphoreType.DMA(()),
                        *_exchange_sems(len(plans))],
    )(*big, packed, dmod)
    return (*res[:len(big)], res[-1], res[n_first - 1])


def _tiled_call(body, args, *, name, grid, in_specs, out_specs, out_shape, scratch_shapes=(), ride=None):
    params = _params(("arbitrary",) * len(grid))
    if ride is None:
        return pl.pallas_call(body, name=name, grid=grid, in_specs=in_specs, out_specs=out_specs, out_shape=out_shape,
                              scratch_shapes=list(scratch_shapes), compiler_params=params)(*args), []
    shapes, plans = ride
    n_in, n_src, n_out, n_dst, n_scr = len(in_specs), len(plans), len(out_specs), len(shapes), len(scratch_shapes)
    hbm = pl.BlockSpec(memory_space=pltpu.HBM)

    def carrying(*refs):
        ins, srcs, refs = refs[:n_in], refs[n_in:n_in + n_src], refs[n_in + n_src:]
        outs, dsts, refs = refs[:n_out], refs[n_out:n_out + n_dst], refs[n_out + n_dst:]
        scratch, sems = refs[:n_scr], refs[n_scr:]
        ids = [pl.program_id(a) for a in range(len(grid))]
        first = functools.reduce(jnp.logical_and, [i == 0 for i in ids])
        last = functools.reduce(jnp.logical_and, [i == g - 1 for i, g in zip(ids, grid)])

        @pl.when(first)
        def _():
            _exchange_start(plans, srcs, dsts, sems)

        body(*ins, *outs, *scratch)

        @pl.when(last)
        def _():
            _exchange_wait(plans, srcs, dsts, sems)

    res = pl.pallas_call(
        carrying, name=name, grid=grid, in_specs=[*in_specs, *[hbm] * n_src], out_specs=[*out_specs, *[hbm] * n_dst],
        out_shape=[*out_shape, *shapes], scratch_shapes=[*scratch_shapes, *_exchange_sems(n_src)],
        compiler_params=params)(*args, *[p[0] for p in plans])
    return res[:n_out], res[n_out:]


def _tail(nd, idx):
    return (slice(None),) * (nd - 2) + idx


def _gather(a):
    return jax.ShapeDtypeStruct((N_DEV,) + a.shape, a.dtype), lambda s, p: s, lambda o, q: o.at[q]


def _gather_rows(a):
    r = a.shape[-2]
    return (jax.ShapeDtypeStruct(a.shape[:-2] + (N_DEV * r, a.shape[-1]), a.dtype), lambda s, p: s,
            lambda o, q: o.at[_tail(a.ndim, (pl.ds(pl.multiple_of(q * r, r), r), slice(None)))])


def _gather_cols(a):
    c = a.shape[-1]
    return (jax.ShapeDtypeStruct(a.shape[:-1] + (N_DEV * c,), a.dtype), lambda s, p: s,
            lambda o, q: o.at[_tail(a.ndim, (slice(None), pl.ds(pl.multiple_of(q * c, c), c)))])


def _scatter_rows(a):
    r = a.shape[0] // N_DEV
    return (jax.ShapeDtypeStruct((N_DEV, r, a.shape[1]), a.dtype),
            lambda s, p: s.at[pl.ds(pl.multiple_of(p * r, r), r), :], lambda o, q: o.at[q])


def _scatter_cols(a):
    c = a.shape[1] // N_DEV
    return (jax.ShapeDtypeStruct((N_DEV, a.shape[0], c), a.dtype),
            lambda s, p: s.at[:, pl.ds(pl.multiple_of(p * c, c), c)], lambda o, q: o.at[q])


def _plans(arrays, rules):
    shapes, plans = [], []
    for o, (a, rule) in enumerate(zip(arrays, rules)):
        shape, send, land = rule(a)
        shapes.append(shape)
        plans.append((a, o, send, land))
    return shapes, plans


def _pack(pieces, rows_multiple=8):
    flat = []
    for a in pieces:
        f = a.reshape(-1)
        flat.append(jnp.pad(f, (0, (-f.shape[0]) % 128)))
    total = sum(f.shape[0] for f in flat)
    flat.append(jnp.zeros(((-total) % (128 * rows_multiple),), F32))
    return jnp.concatenate(flat).reshape(-1, 128)


def _unpack(buf, shapes, lead=()):
    flat = buf.reshape(lead + (-1,))
    out, off = [], 0
    for s in shapes:
        n = math.prod(s)
        out.append(flat[..., off:off + n].reshape(lead + tuple(s)))
        off += n + (-n) % 128
    return out


def _pad_rows(a, rows):
    return jnp.pad(a, ((0, rows - a.shape[0]), (0, 0)))


VEC_NAMES = ('pool_scale', 'b_dw_c', 'ln_g_c', 'ln_b_c', 'b_pw2_c', 'ln_g_d', 'ln_b_d')
GATHERED = ('w_in', 'w_out', 'w_pw2_c', 'w_conv_a', 'w_dw_c')
GATHER_RULES = (_gather_cols, _gather_rows, _gather_rows, _gather, _gather)
SCATTER_RULES = (_scatter_cols, _scatter_rows, _scatter_rows)


def _weight_shards(shard, l):
    return [shard[n][l].astype(BF16) if n in ('w_in', 'w_out') else shard[n][l] for n in GATHERED]


def _layer_weights(shard, l, gathered):
    w_in_b, w_out_b, w_pw2, wconv_parts, wdw_parts = gathered
    wconv = wconv_parts.transpose(1, 0, 2).reshape(CONV_A, GROUP)
    wdw = wdw_parts.transpose(1, 0, 2).reshape(CONV_C, GROUP)
    wp = jnp.einsum('gcd,gh->gchd', shard['w_pool'][l], jnp.eye(4, dtype=F32)).reshape(GROUP, GROUP)
    ws = shard['w_s_d'][l] * jnp.tril(jnp.ones((SUB, SUB), F32))
    vec = jnp.stack([shard[n][l] for n in VEC_NAMES])
    small = (_pad_rows(wconv, 8), _pad_rows(wdw, HALO), _pad_rows(vec, 16), wp.astype(BF16), w_pw2.astype(BF16),
             ws.reshape(4 * SUB, SUB).astype(BF16), jnp.repeat(shard['b_s_d'][l].T, 64, axis=1))
    small_t = (wp.T.astype(BF16), w_pw2.T.astype(BF16), ws.transpose(0, 2, 1).reshape(4 * SUB, SUB).astype(BF16))
    return w_in_b, w_out_b, small, small_t


def kernel(x, c, norm_g, w_ada, b_ada, w_in, w_conv_a, w_pool, pool_scale, w_dw_c, b_dw_c, ln_g_c, ln_b_c, w_pw2_c, b_pw2_c, ln_g_d, ln_b_d, w_s_d, b_s_d, w_out, final_g, loss_target, m_norm_g, m_w_ada, m_b_ada, m_w_in, m_w_conv_a, m_w_pool, m_pool_scale, m_w_dw_c, m_b_dw_c, m_ln_g_c, m_ln_b_c, m_w_pw2_c, m_b_pw2_c, m_ln_g_d, m_ln_b_d, m_w_s_d, m_b_s_d, m_w_out, m_final_g, v_norm_g, v_w_ada, v_b_ada, v_w_in, v_w_conv_a, v_w_pool, v_pool_scale, v_w_dw_c, v_b_dw_c, v_ln_g_c, v_ln_b_c, v_w_pw2_c, v_b_pw2_c, v_ln_g_d, v_ln_b_d, v_w_s_d, v_b_s_d, v_w_out, v_final_g):
    given = dict(locals())
    shard = {n: given[n] for n in WEIGHTS}
    mom_m = {n: given['m_' + n] for n in WEIGHTS}
    mom_v = {n: given['v_' + n] for n in WEIGHTS}
    me = 4 * lax.axis_index("x") + 2 * lax.axis_index("y") + lax.axis_index("c")
    n_tok = x.shape[1]
    tile = min(TOKEN_TILE, n_tok)
    x0 = x.reshape(n_tok, D_MODEL)
    target = loss_target.reshape(n_tok, D_MODEL)
    ada_cols = w_ada.shape[2]

    first_shards = _weight_shards(shard, 0)
    c_all, w_in_first = _exchange("gather_weights", _plans([c, first_shards[0]], (_gather, GATHER_RULES[0])))

    b_cols = lax.dynamic_slice_in_dim(b_ada, me * ada_cols, ada_cols, axis=1)
    c_act, mod_cols = _modulation_columns(c_all.reshape(N_DEV, D_MODEL), w_ada, b_cols)
    (mod_all,) = _exchange("gather_modulation", _plans([mod_cols], [_gather]))
    mod = lax.dynamic_index_in_dim(mod_all, me, axis=2, keepdims=False)
    mod = mod.transpose(1, 0, 2).reshape(N_LAYERS, 3 * D_MODEL)
    shift, scale, gate = (mod[:, k * D_MODEL:(k + 1) * D_MODEL].reshape(N_LAYERS, 1, D_MODEL) for k in range(3))
    gs = norm_g.reshape(N_LAYERS, 1, D_MODEL) * (1.0 + scale)

    xs, hs, zs, ocs, layers = [x0], [], [], [], []
    for l in range(N_LAYERS):
        if l == 0:
            (h, z), rest = _in_proj(xs[0], gs[0], shift[0], w_in_first, tile,
                                    ride=_plans(first_shards[1:], GATHER_RULES[1:]))
            layers.append(_layer_weights(shard, 0, [w_in_first, *rest]))
        else:
            (h, z), _ = _in_proj(xs[l], gs[l], shift[l], layers[l][0], tile)
        _, w_out_b, small, _ = layers[l]
        hs.append(h)
        zs.append(z)
        ride = _plans(_weight_shards(shard, l + 1), GATHER_RULES) if l + 1 < N_LAYERS else None
        x_next, o_c, gathered = _mix_out(z, xs[l], gate[l], small, w_out_b, tile, ride=ride)
        xs.append(x_next)
        ocs.append(o_c)
        if ride:
            layers.append(_layer_weights(shard, l + 1, gathered))
    dx, loss_part, dfinal_g = _loss_head(xs[N_LAYERS], final_g.reshape(1, D_MODEL), target, tile)
    loss = lax.psum(loss_part[0, 0], ("x", "y", "c"))

    part = {}
    layer_parts = [None] * N_LAYERS
    slots = [None] * N_LAYERS
    for l in reversed(range(N_LAYERS)):
        w_in_b, w_out_b, small, small_t = layers[l]
        ride = _plans(layer_parts[l + 1]['big'], SCATTER_RULES) if l + 1 < N_LAYERS else None
        (dz, ycat, sums, dwp, dw2, dws, dbs), rode = _mix_bwd(zs[l], ocs[l], dx, gate[l], small, small_t, w_out_b, tile,
                                                              ride=ride)
        if ride:
            slots[l + 1] = rode
        dw_in, _ = _tokens_matmul(hs[l], dz, "in_proj_tokens_matmul", out_dtype=BF16)
        ride = _plans([dw_in], SCATTER_RULES[:1]) if l == 0 else None
        m_out, rode = _tokens_matmul(ycat, dx, "out_proj_tokens_matmul", ride=ride)
        if ride:
            slots[l] = list(rode)
        dw_out, dgate = _out_proj_grads(m_out, w_out_b, gate[l])
        (dx, dshift, dgs), _ = _norm_bwd(xs[l], dz, dx, gs[l], w_in_b, tile)
        layer_parts[l] = dict(
            big=[dw_in, dw_out, dw2],
            b_ada=jnp.concatenate([dshift, dgs * norm_g[l][None], dgate], axis=1)[0],
            norm_g=(dgs * (1.0 + scale[l]))[0],
            w_conv_a=sums[S_WCONV:S_WCONV + CONV_A], w_dw_c=sums[S_WDW:S_WDW + CONV_C],
            pool_scale=sums[S_PSCALE], b_dw_c=sums[S_BDW], ln_g_c=sums[S_LNGC], ln_b_c=sums[S_LNBC],
            b_pw2_c=sums[S_BPW2], ln_g_d=sums[S_LNGD], ln_b_d=sums[S_LNBD],
            w_pool=jnp.einsum('gchd,gh->gcd', dwp.reshape(4, 64, 4, 64), jnp.eye(4, dtype=F32)),
            w_s_d=dws.reshape(4, SUB, SUB) * jnp.tril(jnp.ones((SUB, SUB), F32)),
            b_s_d=dbs.reshape(SUB, 4, 64).sum(axis=-1).T)
    grad_x = dx.reshape(x.shape)
    for n in REPLICATED + CHANNEL_SHARDED:
        part[n] = dfinal_g[0] if n == 'final_g' else jnp.stack([layer_parts[l][n] for l in range(N_LAYERS)])

    small_names = REPLICATED + CHANNEL_SHARDED
    small_shapes = [part[n].shape for n in small_names]
    slots_out, slots_pw2, small_sum, dmod_all = _finish_exchange(
        layer_parts[0]['big'][1:], SCATTER_RULES[1:], _pack([part[n] for n in small_names], rows_multiple=8 * N_DEV),
        part['b_ada'])
    slots[0] += [slots_out, slots_pw2]

    grads, deltas, new_m, new_v = {}, {}, {}, {}
    for j, n in enumerate(('w_in', 'w_out', 'w_pw2_c')):
        outs = [_adam_update(shard[n][l], slots[l][j], mom_m[n][l], mom_v[n][l], "update_" + n) for l in range(N_LAYERS)]
        grads[n], deltas[n], new_m[n], new_v[n] = (jnp.stack(o) for o in zip(*outs))

    gsum = dict(zip(small_names, _unpack(small_sum, small_shapes)))
    for n in CHANNEL_SHARDED:
        width = shard[n].shape[2]
        gsum[n] = lax.dynamic_slice_in_dim(gsum[n], me * width, width, axis=2)
    outs = _adam_update(_pack([shard[n] for n in small_names]), _pack([gsum[n] for n in small_names]),
                        _pack([mom_m[n] for n in small_names]), _pack([mom_v[n] for n in small_names]), "update_small")
    own_shapes = [shard[n].shape for n in small_names]
    _, d_small, m_small, v_small = (_unpack(o, own_shapes) for o in outs)
    for j, n in enumerate(small_names):
        grads[n], deltas[n], new_m[n], new_v[n] = gsum[n], d_small[j], m_small[j], v_small[j]

    dmod_cols = lax.dynamic_slice_in_dim(dmod_all, me * ada_cols, ada_cols, axis=2).transpose(1, 0, 2)
    grads['w_ada'], deltas['w_ada'], new_m['w_ada'], new_v['w_ada'] = _ada_update(
        c_act.T, dmod_cols, w_ada, m_w_ada, v_w_ada)

    return (loss, grad_x, *[grads[n] for n in WEIGHTS], *[deltas[n] for n in WEIGHTS],
            *[new_m[n] for n in WEIGHTS], *[new_v[n] for n in WEIGHTS])
```

```python
import functools
import math

import jax
import jax.numpy as jnp
from jax import lax
from jax.experimental import pallas as pl
from jax.experimental.pallas import tpu as pltpu

F32 = jnp.float32
BF16 = jnp.bfloat16

N_DEV = 8
D_MODEL = 1024
GROUP = 256
D_IN = 12 * GROUP
N_LAYERS = 2
HALO = 32
SUB = 128
WIN = SUB + HALO
TOKEN_TILE = 512
REDUCE_TILE = 1024
EPS = 1e-6
VMEM_BYTES_V7X = 64 * 1024 * 1024
VMEM_LIMIT = VMEM_BYTES_V7X - 8 * 1024 * 1024

ADAM_LR = 0.001
ADAM_B1 = 0.9
ADAM_B2 = 0.999
ADAM_EPS = 1e-08
ADAM_WD = 0.01
ADAM_STEP = 10

A_B, A_C, A_X, A_G, B_P, B_G, C_A, C_GL, C_G, D_U, D_V, D_G = range(12)
V_PSCALE, V_BDW, V_LNGC, V_LNBC, V_BPW2, V_LNGD, V_LNBD = range(7)
S_WCONV, S_PSCALE, S_BDW, S_LNGC, S_LNBC, S_BPW2, S_LNGD, S_LNBD, S_WDW = 0, 3, 4, 5, 6, 7, 8, 9, 16
N_SUMS = 64
CONV_A = 3
CONV_C = 31

WEIGHTS = ('norm_g', 'w_ada', 'b_ada', 'w_in', 'w_conv_a', 'w_pool', 'pool_scale', 'w_dw_c', 'b_dw_c', 'ln_g_c',
           'ln_b_c', 'w_pw2_c', 'b_pw2_c', 'ln_g_d', 'ln_b_d', 'w_s_d', 'b_s_d', 'w_out', 'final_g')
REPLICATED = ('norm_g', 'b_ada', 'w_pool', 'pool_scale', 'b_dw_c', 'ln_g_c', 'ln_b_c', 'b_pw2_c', 'ln_g_d', 'ln_b_d',
              'w_s_d', 'b_s_d', 'final_g')
CHANNEL_SHARDED = ('w_conv_a', 'w_dw_c')


def _params(semantics, vmem=VMEM_LIMIT):
    return pltpu.CompilerParams(dimension_semantics=semantics, vmem_limit_bytes=vmem)


def _cols(g):
    return slice(g * GROUP, (g + 1) * GROUP)


def _full(shape):
    return pl.BlockSpec(shape, lambda *_: (0,) * len(shape))


def _silu(x):
    s = jax.nn.sigmoid(x)
    return x * s, s


def _dsilu(x, s):
    return s * (1.0 + x * (1.0 - s))


_GELU_C0 = math.sqrt(2.0 / math.pi)
_GELU_C1 = 0.044715


def _gelu(x):
    th = jnp.tanh(_GELU_C0 * (x + _GELU_C1 * (x * x * x)))
    return 0.5 * x * (1.0 + th), th


def _dgelu(x, th):
    return 0.5 * (1.0 + th) + 0.5 * x * (1.0 - th * th) * (_GELU_C0 * (1.0 + 3.0 * _GELU_C1 * (x * x)))


def _layer_norm(x):
    mu = jnp.mean(x, axis=-1, keepdims=True)
    xc = x - mu
    rstd = lax.rsqrt(jnp.mean(xc * xc, axis=-1, keepdims=True) + EPS)
    return xc * rstd, rstd


def _layer_norm_bwd(dn, n, rstd):
    return rstd * (dn - jnp.mean(dn, axis=-1, keepdims=True) - n * jnp.mean(dn * n, axis=-1, keepdims=True))


def _shift_rows(a, k):
    k = k % a.shape[0]
    return a if k == 0 else pltpu.roll(a, k, 0)


def _row_sum8(a):
    s = a[0:8]
    for m in range(1, a.shape[0] // 8):
        s = s + a[8 * m:8 * m + 8]
    return s


def _lane():
    return lax.broadcasted_iota(jnp.int32, (SUB, GROUP), 1)


def _by_quarter(lane, parts):
    return jnp.where(lane < 64, parts[0], jnp.where(lane < 128, parts[1], jnp.where(lane < 192, parts[2], parts[3])))


def _conv_inputs(z_ref, rows):
    def f(g):
        return z_ref[rows, _cols(g)].astype(F32)
    return f(A_C) * f(A_X), f(B_P), f(C_A) * jax.nn.sigmoid(f(C_GL))


def _fill_past(past_ref, zh_ref, zm_ref, is_first, tile):
    parts = _conv_inputs(zh_ref, slice(None))
    for n, a in enumerate(parts):
        past_ref[0:HALO, _cols(n)] = jnp.where(is_first, 0.0, a)

    def body(j, carry):
        r0 = pl.multiple_of(j * SUB, SUB)
        for n, a in enumerate(_conv_inputs(zm_ref, pl.ds(r0, SUB))):
            past_ref[pl.ds(r0 + HALO, SUB), _cols(n)] = a
        return carry

    lax.fori_loop(0, tile // SUB, body, 0)


def _short_conv_taps(qw):
    return [_shift_rows(qw, CONV_A - 1 - k)[HALO:WIN] for k in range(CONV_A)]


def _window_sums(pw, lane):
    s2 = pw + _shift_rows(pw, 1)
    s4 = s2 + _shift_rows(s2, 2)
    s8 = s4 + _shift_rows(s4, 4)
    s16 = s8 + _shift_rows(s8, 8)
    return _by_quarter(lane, [s[HALO:WIN] for s in (s2, s4, s8, s16)])


def _inv_count(lane, t_first):
    width = _by_quarter(lane, [2.0, 4.0, 8.0, 16.0])
    t = lax.broadcasted_iota(jnp.int32, (SUB, GROUP), 0) + t_first
    return 1.0 / jnp.minimum((t + 1).astype(F32), width)


def _forward_window_sums(ew, lane):
    n = ew.shape[0]
    f2 = ew + _shift_rows(ew, n - 1)
    f4 = f2 + _shift_rows(f2, n - 2)
    f8 = f4 + _shift_rows(f4, n - 4)
    f16 = f8 + _shift_rows(f8, n - 8)
    return _by_quarter(lane, [f[0:SUB] for f in (f2, f4, f8, f16)])


def _mixers_forward(zc, win, t_first, wc_ref, wdw_ref, vec_ref, wp_ref, w2_ref, ws_ref, bs_ref, o_c=None):
    lane = _lane()

    def vec(n):
        return vec_ref[n:n + 1, :]

    taps = _short_conv_taps(win(0))
    o_a = wc_ref[0:1, :] * taps[0] + wc_ref[1:2, :] * taps[1] + wc_ref[2:3, :] * taps[2]
    a_b, a_g = zc(A_B), zc(A_G)
    sg_a, s_a = _silu(a_g)
    y_a = a_b * o_a * sg_a

    pw = win(1)
    ic = _inv_count(lane, t_first)
    pooled = _window_sums(pw, lane) * ic - pw[HALO:WIN]
    pooled_b = pooled.astype(BF16)
    y0_b = jnp.dot(pooled_b, wp_ref[...], preferred_element_type=F32)
    b_g = zc(B_G)
    sg_b, s_b = _silu(b_g)
    y_b = y0_b * vec(V_PSCALE) * sg_b

    hw = win(2)
    if o_c is None:
        o_c = wdw_ref[CONV_C - 1:CONV_C, :] * hw[HALO:WIN] + vec(V_BDW)
        for k in range(CONV_C - 1):
            o_c = o_c + wdw_ref[k:k + 1, :] * _shift_rows(hw, CONV_C - 1 - k)[HALO:WIN]
    n_c, rstd_c = _layer_norm(o_c)
    ln_c = n_c * vec(V_LNGC) + vec(V_LNBC)
    sl_c, ssl_c = _silu(ln_c)
    sl_b = sl_c.astype(BF16)
    yc = jnp.dot(sl_b, w2_ref[...], preferred_element_type=F32) + vec(V_BPW2)
    c_g = zc(C_G)
    sg_c, s_c = _silu(c_g)
    y_c = yc * sg_c

    d_u, d_v, d_g = zc(D_U), zc(D_V), zc(D_G)
    u, th_u = _gelu(d_u)
    gv, th_v = _gelu(d_v)
    n_d, rstd_d = _layer_norm(gv)
    v_b = (n_d * vec(V_LNGD) + vec(V_LNBD)).astype(BF16)
    r = jnp.dot(ws_ref[...], v_b, preferred_element_type=F32)
    mixed = _by_quarter(lane, [r[h * SUB:(h + 1) * SUB] for h in range(4)]) + bs_ref[...]
    sg_d, s_d = _silu(d_g)
    y_d = u * mixed * sg_d

    saved = dict(lane=lane, taps=taps, o_a=o_a, a_b=a_b, a_g=a_g, sg_a=sg_a, s_a=s_a,
                 ic=ic, pooled_b=pooled_b, y0_b=y0_b, b_g=b_g, sg_b=sg_b, s_b=s_b,
                 hw=hw, o_c=o_c, n_c=n_c, rstd_c=rstd_c, ln_c=ln_c, ssl_c=ssl_c, sl_b=sl_b, yc=yc, c_g=c_g, sg_c=sg_c, s_c=s_c,
                 d_u=d_u, d_v=d_v, d_g=d_g, u=u, th_u=th_u, th_v=th_v, n_d=n_d, rstd_d=rstd_d, v_b=v_b, mixed=mixed,
                 sg_d=sg_d, s_d=s_d)
    return (y_a, y_b, y_c, y_d), saved


def _in_proj(x, gs, shift, w_in_b, tile, ride=None):
    n_tok = x.shape[0]

    def body(x_ref, gs_ref, sh_ref, w_ref, h_ref, z_ref):
        xv = x_ref[...]
        r = lax.rsqrt(jnp.mean(xv * xv, axis=-1, keepdims=True) + EPS)
        h = ((xv * r) * gs_ref[...] + sh_ref[...]).astype(BF16)
        h_ref[...] = h
        for j in range(D_IN // D_MODEL):
            cs = slice(j * D_MODEL, (j + 1) * D_MODEL)
            z_ref[:, cs] = jnp.dot(h, w_ref[:, cs], preferred_element_type=F32).astype(BF16)

    return _tiled_call(
        body, (x, gs, shift, w_in_b), name="in_proj", grid=(n_tok // tile,),
        in_specs=[pl.BlockSpec((tile, D_MODEL), lambda i: (i, 0)), _full((1, D_MODEL)), _full((1, D_MODEL)),
                  _full((D_MODEL, D_IN))],
        out_specs=[pl.BlockSpec((tile, D_MODEL), lambda i: (i, 0)), pl.BlockSpec((tile, D_IN), lambda i: (i, 0))],
        out_shape=[jax.ShapeDtypeStruct((n_tok, D_MODEL), BF16), jax.ShapeDtypeStruct((n_tok, D_IN), BF16)],
        ride=ride)


def _small_specs(with_transposes):
    specs = [_full((8, GROUP)), _full((HALO, GROUP)), _full((16, GROUP)), _full((GROUP, GROUP)), _full((GROUP, GROUP)),
             _full((4 * SUB, SUB)), _full((SUB, GROUP))]
    if with_transposes:
        specs += [_full((GROUP, GROUP)), _full((GROUP, GROUP)), _full((4 * SUB, SUB))]
    return specs


def _mix_out(z, x, gate, small, w_out_b, tile, ride=None, head=None):
    n_tok = x.shape[0]
    n_tiles = n_tok // tile
    n_sub = tile // SUB
    cw = D_MODEL // n_sub
    per_halo = tile // HALO
    n_in = 12 + (2 if head else 0)
    n_out = 4 if head else 2

    def cur(i):
        return jnp.minimum(i, n_tiles - 1)

    def prev(i):
        return jnp.maximum(i - 1, 0)

    def body(*refs):
        (zm_ref, zh_ref, x_ref, gate_ref, wc_ref, wdw_ref, vec_ref, wp_ref, w2_ref, ws_ref, bs_ref, wout_ref) = refs[:12]
        xo_ref, oc_ref = refs[n_in:n_in + 2]
        past_ref, ycat_ref, ycat_prev_ref = refs[n_in + n_out:n_in + n_out + 3]
        i = pl.program_id(0)
        t = cur(i)
        if head:
            g_ref, tgt_ref = refs[12:14]
            loss_ref, dg_ref = refs[n_in + 2:n_in + 4]
            xn_ref, acc_ref = refs[n_in + n_out + 3:]
        else:
            xn_ref = xo_ref

        @pl.when(i == 0)
        def _():
            ycat_prev_ref[...] = jnp.zeros_like(ycat_prev_ref)
            if head:
                acc_ref[...] = jnp.zeros_like(acc_ref)

        _fill_past(past_ref, zh_ref, zm_ref, t == 0, tile)
        for j in range(n_sub):
            cs = slice(j * cw, (j + 1) * cw)
            y = jnp.dot(ycat_prev_ref[...], wout_ref[:, cs], preferred_element_type=F32)
            xn_ref[:, cs] = x_ref[:, cs] + gate_ref[:, cs] * y
            rows = slice(j * SUB, (j + 1) * SUB)
            ys, s = _mixers_forward(
                lambda g: zm_ref[rows, _cols(g)].astype(F32), lambda n: past_ref[j * SUB:j * SUB + WIN, _cols(n)],
                t * tile + j * SUB, wc_ref, wdw_ref, vec_ref, wp_ref, w2_ref, ws_ref, bs_ref)
            for n, y in enumerate(ys):
                ycat_ref[rows, _cols(n)] = y.astype(BF16)
            oc_ref[rows, :] = s["o_c"]
        ycat_prev_ref[...] = ycat_ref[...]
        if head:
            counted = jnp.where(i > 0, 1.0, 0.0)
            xo_ref[...] = _loss_head_block(xn_ref[...], g_ref[...], tgt_ref[...], acc_ref, counted)

            @pl.when(i == n_tiles)
            def _():
                loss_ref[...] = jnp.full((8, 128), 0.5 / D_MODEL, F32) * jnp.sum(acc_ref[0])
                dg_ref[...] = jnp.sum(acc_ref[1], axis=0, keepdims=True)

    in_specs = [pl.BlockSpec((tile, D_IN), lambda i: (cur(i), 0)),
                pl.BlockSpec((HALO, D_IN), lambda i: (jnp.maximum(cur(i) * per_halo - 1, 0), 0)),
                pl.BlockSpec((tile, D_MODEL), lambda i: (prev(i), 0)), _full((1, D_MODEL)),
                *_small_specs(False), _full((D_MODEL, D_MODEL))]
    out_specs = [pl.BlockSpec((tile, D_MODEL), lambda i: (prev(i), 0)), pl.BlockSpec((tile, GROUP), lambda i: (cur(i), 0))]
    out_shape = [jax.ShapeDtypeStruct((n_tok, D_MODEL), F32), jax.ShapeDtypeStruct((n_tok, GROUP), F32)]
    scratch = [pltpu.VMEM((tile + HALO, 3 * GROUP), F32), pltpu.VMEM((tile, D_MODEL), BF16), pltpu.VMEM((tile, D_MODEL), BF16)]
    args = (z, z, x, gate, *small, w_out_b)
    if head:
        in_specs += [_full((1, D_MODEL)), pl.BlockSpec((tile, D_MODEL), lambda i: (prev(i), 0))]
        out_specs += [_full((8, 128)), _full((1, D_MODEL))]
        out_shape += [jax.ShapeDtypeStruct((8, 128), F32), jax.ShapeDtypeStruct((1, D_MODEL), F32)]
        scratch += [pltpu.VMEM((tile, D_MODEL), F32), pltpu.VMEM((2, 8, D_MODEL), F32)]
        args += tuple(head)
    outs, rode = _tiled_call(body, args, name="mix_out", grid=(n_tiles + 1,), in_specs=in_specs, out_specs=out_specs,
                             out_shape=out_shape, scratch_shapes=scratch, ride=ride)
    return outs, rode


def _loss_head_block(xv, g, target, acc_ref, counted):
    r = lax.rsqrt(jnp.mean(xv * xv, axis=-1, keepdims=True) + EPS)
    xn = xv * r
    err = xn * g - target
    acc_ref[0] = acc_ref[0] + counted * _row_sum8(err * err)
    dy = err * (1.0 / D_MODEL)
    acc_ref[1] = acc_ref[1] + counted * _row_sum8(dy * xn)
    a = dy * g
    return r * (a - xn * jnp.mean(a * xn, axis=-1, keepdims=True))


def _mix_bwd(z, o_c, dx_next, gate, small, small_t, w_out_b, tile, ride=None):
    n_tok = z.shape[0]
    n_tiles = n_tok // tile
    n_sub = tile // SUB
    cw = D_MODEL // n_sub
    per_halo = tile // HALO
    nt_dims = (((1,), (1,)), ((), ()))

    def tile_of(i):
        return n_tiles - 1 - i

    def next_tile_of(i):
        return jnp.maximum(n_tiles - 2 - i, 0)

    def body(zm_ref, zh_ref, oc_ref, dxn_ref, dxn_next_ref, gate_ref, wc_ref, wdw_ref, vec_ref, wp_ref, w2_ref, ws_ref,
             bs_ref, wpt_ref, w2t_ref, wst_ref, wout_ref,
             dz_ref, ycat_ref, sums_ref, dwp_ref, dw2_ref, dws_ref, dbs_ref,
             past_ref, future_ref, dy_ref, dy_next_ref, acc_ref):
        i = pl.program_id(0)
        t = tile_of(i)

        @pl.when(i == 0)
        def _():
            acc_ref[...] = jnp.zeros_like(acc_ref)
            dwp_ref[...] = jnp.zeros_like(dwp_ref)
            dw2_ref[...] = jnp.zeros_like(dw2_ref)
            dws_ref[...] = jnp.zeros_like(dws_ref)
            dbs_ref[...] = jnp.zeros_like(dbs_ref)
            future_ref[tile:tile + HALO, :] = jnp.zeros((HALO, 3 * GROUP), F32)
            dy_ref[...] = lax.dot_general((dxn_ref[...] * gate_ref[...]).astype(BF16), wout_ref[...], nt_dims,
                                        preferred_element_type=F32)

        _fill_past(past_ref, zh_ref, zm_ref, t == 0, tile)
        dyb_next = (dxn_next_ref[...] * gate_ref[...]).astype(BF16)

        def vec(n):
            return vec_ref[n:n + 1, :]

        for jj in range(n_sub):
            j = n_sub - 1 - jj
            r0 = j * SUB
            rows = slice(r0, r0 + SUB)

            def zc(g):
                return zm_ref[rows, _cols(g)].astype(F32)

            def add(n, a):
                acc_ref[n] = acc_ref[n] + _row_sum8(a)

            def put(g, a):
                dz_ref[rows, _cols(g)] = a.astype(BF16)

            def future_window(n, a):
                future_ref[rows, _cols(n)] = a
                return future_ref[r0:r0 + WIN, _cols(n)]

            ys, s = _mixers_forward(zc, lambda n: past_ref[r0:r0 + WIN, _cols(n)], t * tile + r0,
                                    wc_ref, wdw_ref, vec_ref, wp_ref, w2_ref, ws_ref, bs_ref, o_c=oc_ref[rows, :])
            for n, y in enumerate(ys):
                ycat_ref[rows, _cols(n)] = y.astype(BF16)
            lane = s["lane"]
            ks = slice(jj * cw, (jj + 1) * cw)
            dy_next_ref[:, ks] = lax.dot_general(dyb_next, wout_ref[ks, :], nt_dims, preferred_element_type=F32)

            dy = dy_ref[rows,_cols(0)]
            put(A_B, dy * s["o_a"] * s["sg_a"])
            put(A_G, dy * s["a_b"] * s["o_a"] * _dsilu(s["a_g"], s["s_a"]))
            do = dy * s["a_b"] * s["sg_a"]
            for k in range(CONV_A):
                add(S_WCONV + k, do * s["taps"][k])
            dow = future_window(0, do)
            dq = wc_ref[CONV_A - 1:CONV_A, :] * dow[0:SUB]
            for k in range(CONV_A - 1):
                dq = dq + wc_ref[k:k + 1, :] * _shift_rows(dow, WIN - (CONV_A - 1 - k))[0:SUB]
            put(A_C, dq * zc(A_X))
            put(A_X, dq * zc(A_C))

            dy = dy_ref[rows,_cols(1)]
            put(B_G, dy * (s["y0_b"] * vec(V_PSCALE)) * _dsilu(s["b_g"], s["s_b"]))
            dyb = dy * s["sg_b"]
            add(S_PSCALE, dyb * s["y0_b"])
            dpw_b = (dyb * vec(V_PSCALE)).astype(BF16)
            dwp_ref[...] += lax.dot_general(s["pooled_b"], dpw_b, (((0,), (0,)), ((), ())), preferred_element_type=F32)
            dpooled = jnp.dot(dpw_b, wpt_ref[...], preferred_element_type=F32)
            ew = future_window(1, dpooled * s["ic"])
            put(B_P, _forward_window_sums(ew, lane) - dpooled)

            dy = dy_ref[rows,_cols(2)]
            put(C_G, dy * s["yc"] * _dsilu(s["c_g"], s["s_c"]))
            dyc = dy * s["sg_c"]
            add(S_BPW2, dyc)
            dyc_b = dyc.astype(BF16)
            dw2_ref[...] += lax.dot_general(s["sl_b"], dyc_b, (((0,), (0,)), ((), ())), preferred_element_type=F32)
            dln = jnp.dot(dyc_b, w2t_ref[...], preferred_element_type=F32) * _dsilu(s["ln_c"], s["ssl_c"])
            add(S_LNGC, dln * s["n_c"])
            add(S_LNBC, dln)
            do = _layer_norm_bwd(dln * vec(V_LNGC), s["n_c"], s["rstd_c"])
            add(S_BDW, do)
            hw = s["hw"]
            for k in range(CONV_C):
                add(S_WDW + k, do * _shift_rows(hw, CONV_C - 1 - k)[HALO:WIN])
            dow = future_window(2, do)
            dhc = wdw_ref[CONV_C - 1:CONV_C, :] * dow[0:SUB]
            for k in range(CONV_C - 1):
                dhc = dhc + wdw_ref[k:k + 1, :] * _shift_rows(dow, WIN - (CONV_C - 1 - k))[0:SUB]
            c_a = zc(C_A)
            sgl = jax.nn.sigmoid(zc(C_GL))
            put(C_A, dhc * sgl)
            put(C_GL, dhc * c_a * sgl * (1.0 - sgl))

            dy = dy_ref[rows,_cols(3)]
            put(D_G, dy * s["u"] * s["mixed"] * _dsilu(s["d_g"], s["s_d"]))
            put(D_U, dy * s["mixed"] * s["sg_d"] * _dgelu(s["d_u"], s["th_u"]))
            dmixed = dy * s["u"] * s["sg_d"]
            dbs_ref[...] += dmixed
            by_head = jnp.concatenate(
                [jnp.where((lane >= 64 * h) & (lane < 64 * h + 64), dmixed, 0.0) for h in range(4)], axis=0).astype(BF16)
            dws_ref[...] += lax.dot_general(by_head, s["v_b"], (((1,), (1,)), ((), ())), preferred_element_type=F32)
            rv = jnp.dot(wst_ref[...], dmixed.astype(BF16), preferred_element_type=F32)
            dv = _by_quarter(lane, [rv[h * SUB:(h + 1) * SUB] for h in range(4)])
            add(S_LNGD, dv * s["n_d"])
            add(S_LNBD, dv)
            dgv = _layer_norm_bwd(dv * vec(V_LNGD), s["n_d"], s["rstd_d"])
            put(D_V, dgv * _dgelu(s["d_v"], s["th_v"]))

        future_ref[tile:tile + HALO, :] = future_ref[0:HALO, :]
        dy_ref[...] = dy_next_ref[...]

        @pl.when(i == n_tiles - 1)
        def _():
            for n in range(N_SUMS):
                sums_ref[n:n + 1, :] = jnp.sum(acc_ref[n], axis=0, keepdims=True)

    return _tiled_call(
        body, (z, z, o_c, dx_next, dx_next, gate, *small, *small_t, w_out_b), name="mix_bwd", grid=(n_tiles,),
        in_specs=[pl.BlockSpec((tile, D_IN), lambda i: (tile_of(i), 0)),
                  pl.BlockSpec((HALO, D_IN), lambda i: (jnp.maximum(tile_of(i) * per_halo - 1, 0), 0)),
                  pl.BlockSpec((tile, GROUP), lambda i: (tile_of(i), 0)),
                  pl.BlockSpec((tile, D_MODEL), lambda i: (tile_of(i), 0)),
                  pl.BlockSpec((tile, D_MODEL), lambda i: (next_tile_of(i), 0)), _full((1, D_MODEL)),
                  *_small_specs(True), _full((D_MODEL, D_MODEL))],
        out_specs=[pl.BlockSpec((tile, D_IN), lambda i: (tile_of(i), 0)),
                   pl.BlockSpec((tile, D_MODEL), lambda i: (tile_of(i), 0)),
                   _full((N_SUMS, GROUP)), _full((GROUP, GROUP)), _full((GROUP, GROUP)), _full((4 * SUB, SUB)),
                   _full((SUB, GROUP))],
        out_shape=[jax.ShapeDtypeStruct((n_tok, D_IN), BF16), jax.ShapeDtypeStruct((n_tok, D_MODEL), BF16),
                   jax.ShapeDtypeStruct((N_SUMS, GROUP), F32), jax.ShapeDtypeStruct((GROUP, GROUP), F32),
                   jax.ShapeDtypeStruct((GROUP, GROUP), F32), jax.ShapeDtypeStruct((4 * SUB, SUB), F32),
                   jax.ShapeDtypeStruct((SUB, GROUP), F32)],
        scratch_shapes=[pltpu.VMEM((tile + HALO, 3 * GROUP), F32), pltpu.VMEM((tile + HALO, 3 * GROUP), F32),
                        pltpu.VMEM((tile, D_MODEL), F32), pltpu.VMEM((tile, D_MODEL), F32),
                        pltpu.VMEM((N_SUMS, 8, GROUP), F32)], ride=ride)


def _norm_bwd(x, dz, dx_next, gs, w_in_b, tile, ride=None):
    n_tok = x.shape[0]
    n_tiles = n_tok // tile

    def body(x_ref, dz_ref, dxn_ref, gs_ref, w_ref, dx_ref, dsh_ref, dgs_ref, acc_ref):
        i = pl.program_id(0)

        @pl.when(i == 0)
        def _():
            acc_ref[...] = jnp.zeros_like(acc_ref)

        dh = lax.dot_general(dz_ref[...], w_ref[...], (((1,), (1,)), ((), ())), preferred_element_type=F32)
        xv = x_ref[...]
        r = lax.rsqrt(jnp.mean(xv * xv, axis=-1, keepdims=True) + EPS)
        xn = xv * r
        acc_ref[0] = acc_ref[0] + _row_sum8(dh)
        acc_ref[1] = acc_ref[1] + _row_sum8(dh * xn)
        dxn = dh * gs_ref[...]
        dx_ref[...] = dxn_ref[...] + r * (dxn - xn * jnp.mean(dxn * xn, axis=-1, keepdims=True))

        @pl.when(i == n_tiles - 1)
        def _():
            dsh_ref[...] = jnp.sum(acc_ref[0], axis=0, keepdims=True)
            dgs_ref[...] = jnp.sum(acc_ref[1], axis=0, keepdims=True)

    return _tiled_call(
        body, (x, dz, dx_next, gs, w_in_b), name="norm_bwd", grid=(n_tiles,),
        in_specs=[pl.BlockSpec((tile, D_MODEL), lambda i: (i, 0)), pl.BlockSpec((tile, D_IN), lambda i: (i, 0)),
                  pl.BlockSpec((tile, D_MODEL), lambda i: (i, 0)), _full((1, D_MODEL)), _full((D_MODEL, D_IN))],
        out_specs=[pl.BlockSpec((tile, D_MODEL), lambda i: (i, 0)), _full((1, D_MODEL)), _full((1, D_MODEL))],
        out_shape=[jax.ShapeDtypeStruct((n_tok, D_MODEL), F32), jax.ShapeDtypeStruct((1, D_MODEL), F32),
                   jax.ShapeDtypeStruct((1, D_MODEL), F32)],
        scratch_shapes=[pltpu.VMEM((2, 8, D_MODEL), F32)], ride=ride)


def _tokens_matmul(a, b, name, out_dtype=F32, ride=None):
    n_tok, ka = a.shape
    nb = b.shape[1]
    tk = min(REDUCE_TILE, n_tok)
    cb = min(D_MODEL, nb)
    n_steps = n_tok // tk

    def body(a_ref, b_ref, o_ref, acc_ref):
        i = pl.program_id(1)

        @pl.when(i == 0)
        def _():
            acc_ref[...] = jnp.zeros_like(acc_ref)

        acc_ref[...] += lax.dot_general(a_ref[...], b_ref[...].astype(BF16), (((0,), (0,)), ((), ())),
                                        preferred_element_type=F32)

        @pl.when(i == n_steps - 1)
        def _():
            o_ref[...] = acc_ref[...].astype(out_dtype)

    (out,), rode = _tiled_call(
        body, (a, b), name=name, grid=(nb // cb, n_steps),
        in_specs=[pl.BlockSpec((tk, ka), lambda j, i: (i, 0)), pl.BlockSpec((tk, cb), lambda j, i: (i, j))],
        out_specs=[pl.BlockSpec((ka, cb), lambda j, i: (0, j))],
        out_shape=[jax.ShapeDtypeStruct((ka, nb), out_dtype)],
        scratch_shapes=[pltpu.VMEM((ka, cb), F32)], ride=ride)
    return out, rode


def _out_proj_grads(m, w_out_b, gate):
    rb = 256
    n_blocks = D_MODEL // rb

    def body(m_ref, w_ref, gate_ref, dw_ref, dgate_ref, acc_ref):
        i = pl.program_id(0)

        @pl.when(i == 0)
        def _():
            acc_ref[...] = jnp.zeros_like(acc_ref)

        mv = m_ref[...]
        dw_ref[...] = (mv * gate_ref[...]).astype(BF16)
        acc_ref[...] += _row_sum8(mv * w_ref[...].astype(F32))

        @pl.when(i == n_blocks - 1)
        def _():
            dgate_ref[...] = jnp.sum(acc_ref[...], axis=0, keepdims=True)

    return pl.pallas_call(
        body, name="out_proj_grads", grid=(n_blocks,),
        in_specs=[pl.BlockSpec((rb, D_MODEL), lambda i: (i, 0)), pl.BlockSpec((rb, D_MODEL), lambda i: (i, 0)),
                  _full((1, D_MODEL))],
        out_specs=[pl.BlockSpec((rb, D_MODEL), lambda i: (i, 0)), _full((1, D_MODEL))],
        out_shape=[jax.ShapeDtypeStruct((D_MODEL, D_MODEL), BF16), jax.ShapeDtypeStruct((1, D_MODEL), F32)],
        scratch_shapes=[pltpu.VMEM((8, D_MODEL), F32)],
        compiler_params=_params(("arbitrary",)),
    )(m, w_out_b, gate)


def _modulation_columns(c_all, w_ada, b_cols):
    cols = w_ada.shape[2]

    def body(c_ref, w_ref, b_ref, ca_ref, mod_ref):
        ca, _ = _silu(c_ref[...])
        ca_ref[...] = ca
        for l in range(N_LAYERS):
            mod_ref[l] = jnp.dot(ca, w_ref[l], precision=lax.Precision.HIGHEST, preferred_element_type=F32) + b_ref[l:l + 1, :]

    return pl.pallas_call(
        body, name="modulation_columns",
        out_shape=[jax.ShapeDtypeStruct((N_DEV, D_MODEL), F32), jax.ShapeDtypeStruct((N_LAYERS, N_DEV, cols), F32)],
        compiler_params=pltpu.CompilerParams(vmem_limit_bytes=VMEM_LIMIT),
    )(c_all, w_ada, b_cols)


def _adam(w, g, m, v):
    m2 = ADAM_B1 * m + (1.0 - ADAM_B1) * g
    v2 = ADAM_B2 * v + (1.0 - ADAM_B2) * (g * g)
    m_hat = m2 / (1.0 - ADAM_B1 ** ADAM_STEP)
    v_hat = v2 / (1.0 - ADAM_B2 ** ADAM_STEP)
    return -ADAM_LR * (m_hat / (jnp.sqrt(v_hat) + ADAM_EPS) + ADAM_WD * w), m2, v2


def _row_block(rows, cols, slots):
    target = max(8, (1 << 19) // (cols * max(slots, 1)))
    rb = rows
    while rb > target and rb % 2 == 0 and (rb // 2) % 8 == 0:
        rb //= 2
    return rb


def _adam_update(w, g, m, v, name):
    rows, cols = w.shape
    slotted = g.ndim == 3
    rb = _row_block(rows, cols, N_DEV if slotted else 1)

    def body(w_ref, g_ref, m_ref, v_ref, go_ref, d_ref, mo_ref, vo_ref):
        if slotted:
            gv = g_ref[0].astype(F32)
            for q in range(1, N_DEV):
                gv = gv + g_ref[q].astype(F32)
        else:
            gv = g_ref[...]
        go_ref[...] = gv
        d_ref[...], mo_ref[...], vo_ref[...] = _adam(w_ref[...], gv, m_ref[...], v_ref[...])

    blk = pl.BlockSpec((rb, cols), lambda i: (i, 0))
    g_blk = pl.BlockSpec((N_DEV, rb, cols), lambda i: (0, i, 0)) if slotted else blk
    return pl.pallas_call(
        body, name=name, grid=(rows // rb,),
        in_specs=[blk, g_blk, blk, blk], out_specs=[blk] * 4,
        out_shape=[jax.ShapeDtypeStruct((rows, cols), F32)] * 4,
        compiler_params=_params(("parallel",)),
    )(w, g, m, v)


def _ada_update(ca_t, dmod_cols, w, m, v):
    _, rows, cols = w.shape

    def body(ca_ref, dm_ref, w_ref, m_ref, v_ref, g_ref, d_ref, mo_ref, vo_ref):
        g = ca_ref[:, 0:1] * dm_ref[0, 0:1, :]
        for b in range(1, N_DEV):
            g = g + ca_ref[:, b:b + 1] * dm_ref[0, b:b + 1, :]
        g_ref[0] = g
        d_ref[0], mo_ref[0], vo_ref[0] = _adam(w_ref[0], g, m_ref[0], v_ref[0])

    blk = pl.BlockSpec((1, rows, cols), lambda l: (l, 0, 0))
    return pl.pallas_call(
        body, name="ada_update", grid=(N_LAYERS,),
        in_specs=[_full((rows, N_DEV)), pl.BlockSpec((1, N_DEV, cols), lambda l: (l, 0, 0)), blk, blk, blk],
        out_specs=[blk] * 4, out_shape=[jax.ShapeDtypeStruct(w.shape, F32)] * 4,
        compiler_params=_params(("parallel",)),
    )(ca_t, dmod_cols, w, m, v)


def _exchange_sems(n):
    return [pltpu.SemaphoreType.DMA((n, N_DEV - 1)), pltpu.SemaphoreType.DMA((n, N_DEV - 1)),
            pltpu.SemaphoreType.DMA((n,))]


def _exchange_copies(plans, srcs, outs, sems, receiving, only=None):
    send_sems, recv_sems, local_sems = sems
    x, y, c = lax.axis_index("x"), lax.axis_index("y"), lax.axis_index("c")
    me = 4 * x + 2 * y + c

    def remote(i, k, incoming):
        _, o, send, land = plans[i]
        px = 1 - x if k & 4 else x
        py = 1 - y if k & 2 else y
        pc = 1 - c if k & 1 else c
        p = 4 * px + 2 * py + pc
        return pltpu.make_async_remote_copy(
            src_ref=send(srcs[i], p), dst_ref=land(outs[o], p if incoming else me),
            send_sem=send_sems.at[i, k - 1], recv_sem=recv_sems.at[i, k - 1],
            device_id=(px, py, pc), device_id_type=pl.DeviceIdType.MESH)

    which = range(len(plans)) if only is None else only
    pairs = [(i, k) for k in range(1, N_DEV) for i in which]
    local = [pltpu.make_async_copy(plans[i][2](srcs[i], me), plans[i][3](outs[plans[i][1]], me), local_sems.at[i])
             for i in which]
    return local, [remote(i, k, False) for i, k in pairs], [remote(i, k, True) for i, k in pairs] if receiving else []


def _exchange_start(plans, srcs, outs, sems, only=None):
    local, outgoing, _ = _exchange_copies(plans, srcs, outs, sems, False, only)
    for cp in local + outgoing:
        cp.start()


def _exchange_wait(plans, srcs, outs, sems, only=None):
    local, outgoing, incoming = _exchange_copies(plans, srcs, outs, sems, True, only)
    for cp in incoming:
        cp.wait_recv()
    for cp in outgoing:
        cp.wait_send()
    for cp in local:
        cp.wait()


def _exchange(name, ride):
    out_shapes, plans = ride
    n = len(plans)
    hbm = pl.BlockSpec(memory_space=pltpu.HBM)

    def body(*refs):
        srcs, outs, sems = refs[:n], refs[n:n + len(out_shapes)], refs[n + len(out_shapes):]
        _exchange_start(plans, srcs, outs, sems)
        _exchange_wait(plans, srcs, outs, sems)

    return pl.pallas_call(
        body, name=name, in_specs=[hbm] * n, out_specs=[hbm] * len(out_shapes), out_shape=list(out_shapes),
        scratch_shapes=_exchange_sems(n),
    )(*[p[0] for p in plans])


def _finish_exchange(big, big_rules, packed, dmod):
    n_rows = packed.shape[0]
    r = n_rows // N_DEV
    shapes, plans = _plans([*big, packed, dmod], [*big_rules, _scatter_rows, _gather])
    n_first = len(plans)
    i_small = n_first - 2
    _, send, land = _gather_rows(jax.ShapeDtypeStruct((r, 128), F32))
    plans = plans + [(None, len(shapes), send, land)]
    shapes = shapes + [jax.ShapeDtypeStruct((n_rows, 128), F32)]
    first = [i for i in range(n_first) if i != i_small]
    hbm = pl.BlockSpec(memory_space=pltpu.HBM)

    def body(*refs):
        srcs, outs = list(refs[:n_first]), refs[n_first:n_first + len(shapes)]
        parts_ref, sum_ref, local_sem = refs[n_first + len(shapes):n_first + len(shapes) + 3]
        sems = refs[n_first + len(shapes) + 3:]
        srcs.append(sum_ref)
        _exchange_start(plans, srcs, outs, sems, only=range(n_first))
        _exchange_wait(plans, srcs, outs, sems, only=[i_small])
        cp = pltpu.make_async_copy(outs[i_small], parts_ref, local_sem)
        cp.start()
        cp.wait()
        g = parts_ref[0]
        for q in range(1, N_DEV):
            g = g + parts_ref[q]
        sum_ref[...] = g
        _exchange_start(plans, srcs, outs, sems, only=[n_first])
        _exchange_wait(plans, srcs, outs, sems, only=[n_first])
        _exchange_wait(plans, srcs, outs, sems, only=first)

    res = pl.pallas_call(
        body, name="finish_exchange", in_specs=[hbm] * n_first, out_specs=[hbm] * len(shapes), out_shape=shapes,
        scratch_shapes=[pltpu.VMEM((N_DEV, r, 128), F32), pltpu.VMEM((r, 128), F32), pltpu.SemaphoreType.DMA(()),
                        *_exchange_sems(len(plans))],
    )(*big, packed, dmod)
    return (*res[:len(big)], res[-1], res[n_first - 1])


def _tiled_call(body, args, *, name, grid, in_specs, out_specs, out_shape, scratch_shapes=(), ride=None):
    params = _params(("arbitrary",) * len(grid))
    if ride is None:
        return pl.pallas_call(body, name=name, grid=grid, in_specs=in_specs, out_specs=out_specs, out_shape=out_shape,
                              scratch_shapes=list(scratch_shapes), compiler_params=params)(*args), []
    shapes, plans = ride
    n_in, n_src, n_out, n_dst, n_scr = len(in_specs), len(plans), len(out_specs), len(shapes), len(scratch_shapes)
    hbm = pl.BlockSpec(memory_space=pltpu.HBM)

    def carrying(*refs):
        ins, srcs, refs = refs[:n_in], refs[n_in:n_in + n_src], refs[n_in + n_src:]
        outs, dsts, refs = refs[:n_out], refs[n_out:n_out + n_dst], refs[n_out + n_dst:]
        scratch, sems = refs[:n_scr], refs[n_scr:]
        ids = [pl.program_id(a) for a in range(len(grid))]
        first = functools.reduce(jnp.logical_and, [i == 0 for i in ids])
        last = functools.reduce(jnp.logical_and, [i == g - 1 for i, g in zip(ids, grid)])

        @pl.when(first)
        def _():
            _exchange_start(plans, srcs, dsts, sems)

        body(*ins, *outs, *scratch)

        @pl.when(last)
        def _():
            _exchange_wait(plans, srcs, dsts, sems)

    res = pl.pallas_call(
        carrying, name=name, grid=grid, in_specs=[*in_specs, *[hbm] * n_src], out_specs=[*out_specs, *[hbm] * n_dst],
        out_shape=[*out_shape, *shapes], scratch_shapes=[*scratch_shapes, *_exchange_sems(n_src)],
        compiler_params=params)(*args, *[p[0] for p in plans])
    return res[:n_out], res[n_out:]


def _tail(nd, idx):
    return (slice(None),) * (nd - 2) + idx


def _gather(a):
    return jax.ShapeDtypeStruct((N_DEV,) + a.shape, a.dtype), lambda s, p: s, lambda o, q: o.at[q]


def _gather_rows(a):
    r = a.shape[-2]
    return (jax.ShapeDtypeStruct(a.shape[:-2] + (N_DEV * r, a.shape[-1]), a.dtype), lambda s, p: s,
            lambda o, q: o.at[_tail(a.ndim, (pl.ds(pl.multiple_of(q * r, r), r), slice(None)))])


def _gather_cols(a):
    c = a.shape[-1]
    return (jax.ShapeDtypeStruct(a.shape[:-1] + (N_DEV * c,), a.dtype), lambda s, p: s,
            lambda o, q: o.at[_tail(a.ndim, (slice(None), pl.ds(pl.multiple_of(q * c, c), c)))])


def _scatter_rows(a):
    r = a.shape[0] // N_DEV
    return (jax.ShapeDtypeStruct((N_DEV, r, a.shape[1]), a.dtype),
            lambda s, p: s.at[pl.ds(pl.multiple_of(p * r, r), r), :], lambda o, q: o.at[q])


def _scatter_cols(a):
    c = a.shape[1] // N_DEV
    return (jax.ShapeDtypeStruct((N_DEV, a.shape[0], c), a.dtype),
            lambda s, p: s.at[:, pl.ds(pl.multiple_of(p * c, c), c)], lambda o, q: o.at[q])


def _plans(arrays, rules):
    shapes, plans = [], []
    for o, (a, rule) in enumerate(zip(arrays, rules)):
        shape, send, land = rule(a)
        shapes.append(shape)
        plans.append((a, o, send, land))
    return shapes, plans


def _pack(pieces, rows_multiple=8):
    flat = []
    for a in pieces:
        f = a.reshape(-1)
        flat.append(jnp.pad(f, (0, (-f.shape[0]) % 128)))
    total = sum(f.shape[0] for f in flat)
    flat.append(jnp.zeros(((-total) % (128 * rows_multiple),), F32))
    return jnp.concatenate(flat).reshape(-1, 128)


def _unpack(buf, shapes, lead=()):
    flat = buf.reshape(lead + (-1,))
    out, off = [], 0
    for s in shapes:
        n = math.prod(s)
        out.append(flat[..., off:off + n].reshape(lead + tuple(s)))
        off += n + (-n) % 128
    return out


def _pad_rows(a, rows):
    return jnp.pad(a, ((0, rows - a.shape[0]), (0, 0)))


VEC_NAMES = ('pool_scale', 'b_dw_c', 'ln_g_c', 'ln_b_c', 'b_pw2_c', 'ln_g_d', 'ln_b_d')
GATHERED = ('w_in', 'w_out', 'w_pw2_c', 'w_conv_a', 'w_dw_c')
GATHER_RULES = (_gather_cols, _gather_rows, _gather_rows, _gather, _gather)
SCATTER_RULES = (_scatter_cols, _scatter_rows, _scatter_rows)


def _weight_shards(shard, l):
    return [shard[n][l].astype(BF16) if n in ('w_in', 'w_out') else shard[n][l] for n in GATHERED]


def _layer_weights(shard, l, gathered):
    w_in_b, w_out_b, w_pw2, wconv_parts, wdw_parts = gathered
    wconv = wconv_parts.transpose(1, 0, 2).reshape(CONV_A, GROUP)
    wdw = wdw_parts.transpose(1, 0, 2).reshape(CONV_C, GROUP)
    wp = jnp.einsum('gcd,gh->gchd', shard['w_pool'][l], jnp.eye(4, dtype=F32)).reshape(GROUP, GROUP)
    ws = shard['w_s_d'][l] * jnp.tril(jnp.ones((SUB, SUB), F32))
    vec = jnp.stack([shard[n][l] for n in VEC_NAMES])
    small = (_pad_rows(wconv, 8), _pad_rows(wdw, HALO), _pad_rows(vec, 16), wp.astype(BF16), w_pw2.astype(BF16),
             ws.reshape(4 * SUB, SUB).astype(BF16), jnp.repeat(shard['b_s_d'][l].T, 64, axis=1))
    small_t = (wp.T.astype(BF16), w_pw2.T.astype(BF16), ws.transpose(0, 2, 1).reshape(4 * SUB, SUB).astype(BF16))
    return w_in_b, w_out_b, small, small_t


def kernel(x, c, norm_g, w_ada, b_ada, w_in, w_conv_a, w_pool, pool_scale, w_dw_c, b_dw_c, ln_g_c, ln_b_c, w_pw2_c, b_pw2_c, ln_g_d, ln_b_d, w_s_d, b_s_d, w_out, final_g, loss_target, m_norm_g, m_w_ada, m_b_ada, m_w_in, m_w_conv_a, m_w_pool, m_pool_scale, m_w_dw_c, m_b_dw_c, m_ln_g_c, m_ln_b_c, m_w_pw2_c, m_b_pw2_c, m_ln_g_d, m_ln_b_d, m_w_s_d, m_b_s_d, m_w_out, m_final_g, v_norm_g, v_w_ada, v_b_ada, v_w_in, v_w_conv_a, v_w_pool, v_pool_scale, v_w_dw_c, v_b_dw_c, v_ln_g_c, v_ln_b_c, v_w_pw2_c, v_b_pw2_c, v_ln_g_d, v_ln_b_d, v_w_s_d, v_b_s_d, v_w_out, v_final_g):
    given = dict(locals())
    shard = {n: given[n] for n in WEIGHTS}
    mom_m = {n: given['m_' + n] for n in WEIGHTS}
    mom_v = {n: given['v_' + n] for n in WEIGHTS}
    me = 4 * lax.axis_index("x") + 2 * lax.axis_index("y") + lax.axis_index("c")
    n_tok = x.shape[1]
    tile = min(TOKEN_TILE, n_tok)
    x0 = x.reshape(n_tok, D_MODEL)
    target = loss_target.reshape(n_tok, D_MODEL)
    ada_cols = w_ada.shape[2]

    first_shards = _weight_shards(shard, 0)
    c_all, w_in_first = _exchange("gather_weights", _plans([c, first_shards[0]], (_gather, GATHER_RULES[0])))

    b_cols = lax.dynamic_slice_in_dim(b_ada, me * ada_cols, ada_cols, axis=1)
    c_act, mod_cols = _modulation_columns(c_all.reshape(N_DEV, D_MODEL), w_ada, b_cols)
    (mod_all,) = _exchange("gather_modulation", _plans([mod_cols], [_gather]))
    mod = lax.dynamic_index_in_dim(mod_all, me, axis=2, keepdims=False)
    mod = mod.transpose(1, 0, 2).reshape(N_LAYERS, 3 * D_MODEL)
    shift, scale, gate = (mod[:, k * D_MODEL:(k + 1) * D_MODEL].reshape(N_LAYERS, 1, D_MODEL) for k in range(3))
    gs = norm_g.reshape(N_LAYERS, 1, D_MODEL) * (1.0 + scale)

    xs, hs, zs, ocs, layers = [x0], [], [], [], []
    for l in range(N_LAYERS):
        if l == 0:
            (h, z), rest = _in_proj(xs[0], gs[0], shift[0], w_in_first, tile,
                                    ride=_plans(first_shards[1:], GATHER_RULES[1:]))
            layers.append(_layer_weights(shard, 0, [w_in_first, *rest]))
        else:
            (h, z), _ = _in_proj(xs[l], gs[l], shift[l], layers[l][0], tile)
        _, w_out_b, small, _ = layers[l]
        hs.append(h)
        zs.append(z)
        if l + 1 < N_LAYERS:
            (x_next, o_c), gathered = _mix_out(z, xs[l], gate[l], small, w_out_b, tile,
                                               ride=_plans(_weight_shards(shard, l + 1), GATHER_RULES))
            xs.append(x_next)
            layers.append(_layer_weights(shard, l + 1, gathered))
        else:
            (dx, o_c, loss_part, dfinal_g), _ = _mix_out(z, xs[l], gate[l], small, w_out_b, tile,
                                                         head=(final_g.reshape(1, D_MODEL), target))
        ocs.append(o_c)
    loss = lax.psum(loss_part[0, 0], ("x", "y", "c"))

    part = {}
    layer_parts = [None] * N_LAYERS
    slots = [None] * N_LAYERS
    for l in reversed(range(N_LAYERS)):
        w_in_b, w_out_b, small, small_t = layers[l]
        ride = _plans(layer_parts[l + 1]['big'], SCATTER_RULES) if l + 1 < N_LAYERS else None
        (dz, ycat, sums, dwp, dw2, dws, dbs), rode = _mix_bwd(zs[l], ocs[l], dx, gate[l], small, small_t, w_out_b, tile,
                                                              ride=ride)
        if ride:
            slots[l + 1] = rode
        dw_in, _ = _tokens_matmul(hs[l], dz, "in_proj_tokens_matmul", out_dtype=BF16)
        ride = _plans([dw_in], SCATTER_RULES[:1]) if l == 0 else None
        m_out, rode = _tokens_matmul(ycat, dx, "out_proj_tokens_matmul", ride=ride)
        if ride:
            slots[l] = list(rode)
        dw_out, dgate = _out_proj_grads(m_out, w_out_b, gate[l])
        (dx, dshift, dgs), _ = _norm_bwd(xs[l], dz, dx, gs[l], w_in_b, tile)
        layer_parts[l] = dict(
            big=[dw_in, dw_out, dw2],
            b_ada=jnp.concatenate([dshift, dgs * norm_g[l][None], dgate], axis=1)[0],
            norm_g=(dgs * (1.0 + scale[l]))[0],
            w_conv_a=sums[S_WCONV:S_WCONV + CONV_A], w_dw_c=sums[S_WDW:S_WDW + CONV_C],
            pool_scale=sums[S_PSCALE], b_dw_c=sums[S_BDW], ln_g_c=sums[S_LNGC], ln_b_c=sums[S_LNBC],
            b_pw2_c=sums[S_BPW2], ln_g_d=sums[S_LNGD], ln_b_d=sums[S_LNBD],
            w_pool=jnp.einsum('gchd,gh->gcd', dwp.reshape(4, 64, 4, 64), jnp.eye(4, dtype=F32)),
            w_s_d=dws.reshape(4, SUB, SUB) * jnp.tril(jnp.ones((SUB, SUB), F32)),
            b_s_d=dbs.reshape(SUB, 4, 64).sum(axis=-1).T)
    grad_x = dx.reshape(x.shape)
    for n in REPLICATED + CHANNEL_SHARDED:
        part[n] = dfinal_g[0] if n == 'final_g' else jnp.stack([layer_parts[l][n] for l in range(N_LAYERS)])

    small_names = REPLICATED + CHANNEL_SHARDED
    small_shapes = [part[n].shape for n in small_names]
    slots_out, slots_pw2, small_sum, dmod_all = _finish_exchange(
        layer_parts[0]['big'][1:], SCATTER_RULES[1:], _pack([part[n] for n in small_names], rows_multiple=8 * N_DEV),
        part['b_ada'])
    slots[0] += [slots_out, slots_pw2]

    grads, deltas, new_m, new_v = {}, {}, {}, {}
    for j, n in enumerate(('w_in', 'w_out', 'w_pw2_c')):
        outs = [_adam_update(shard[n][l], slots[l][j], mom_m[n][l], mom_v[n][l], "update_" + n) for l in range(N_LAYERS)]
        grads[n], deltas[n], new_m[n], new_v[n] = (jnp.stack(o) for o in zip(*outs))

    gsum = dict(zip(small_names, _unpack(small_sum, small_shapes)))
    for n in CHANNEL_SHARDED:
        width = shard[n].shape[2]
        gsum[n] = lax.dynamic_slice_in_dim(gsum[n], me * width, width, axis=2)
    outs = _adam_update(_pack([shard[n] for n in small_names]), _pack([gsum[n] for n in small_names]),
                        _pack([mom_m[n] for n in small_names]), _pack([mom_v[n] for n in small_names]), "update_small")
    own_shapes = [shard[n].shape for n in small_names]
    _, d_small, m_small, v_small = (_unpack(o, own_shapes) for o in outs)
    for j, n in enumerate(small_names):
        grads[n], deltas[n], new_m[n], new_v[n] = gsum[n], d_small[j], m_small[j], v_small[j]

    dmod_cols = lax.dynamic_slice_in_dim(dmod_all, me * ada_cols, ada_cols, axis=2).transpose(1, 0, 2)
    grads['w_ada'], deltas['w_ada'], new_m['w_ada'], new_v['w_ada'] = _ada_update(
        c_act.T, dmod_cols, w_ada, m_w_ada, v_w_ada)

    return (loss, grad_x, *[grads[n] for n in WEIGHTS], *[deltas[n] for n in WEIGHTS],
            *[new_m[n] for n in WEIGHTS], *[new_v[n] for n in WEIGHTS])
```

```python
import functools
import math

import jax
import jax.numpy as jnp
from jax import lax
from jax.experimental import pallas as pl
from jax.experimental.pallas import tpu as pltpu

F32 = jnp.float32
BF16 = jnp.bfloat16

N_DEV = 8
D_MODEL = 1024
GROUP = 256
D_IN = 12 * GROUP
N_LAYERS = 2
HALO = 32
SUB = 128
WIN = SUB + HALO
TOKEN_TILE = 512
REDUCE_TILE = 2048
EPS = 1e-6
VMEM_BYTES_V7X = 64 * 1024 * 1024
VMEM_LIMIT = VMEM_BYTES_V7X - 8 * 1024 * 1024

ADAM_LR = 0.001
ADAM_B1 = 0.9
ADAM_B2 = 0.999
ADAM_EPS = 1e-08
ADAM_WD = 0.01
ADAM_STEP = 10

A_B, A_C, A_X, A_G, B_P, B_G, C_A, C_GL, C_G, D_U, D_V, D_G = range(12)
V_PSCALE, V_BDW, V_LNGC, V_LNBC, V_BPW2, V_LNGD, V_LNBD = range(7)
S_WCONV, S_PSCALE, S_BDW, S_LNGC, S_LNBC, S_BPW2, S_LNGD, S_LNBD, S_WDW = 0, 3, 4, 5, 6, 7, 8, 9, 16
N_SUMS = 64
CONV_A = 3
CONV_C = 31

WEIGHTS = ('norm_g', 'w_ada', 'b_ada', 'w_in', 'w_conv_a', 'w_pool', 'pool_scale', 'w_dw_c', 'b_dw_c', 'ln_g_c',
           'ln_b_c', 'w_pw2_c', 'b_pw2_c', 'ln_g_d', 'ln_b_d', 'w_s_d', 'b_s_d', 'w_out', 'final_g')
REPLICATED = ('norm_g', 'b_ada', 'w_pool', 'pool_scale', 'b_dw_c', 'ln_g_c', 'ln_b_c', 'b_pw2_c', 'ln_g_d', 'ln_b_d',
              'w_s_d', 'b_s_d', 'final_g')
CHANNEL_SHARDED = ('w_conv_a', 'w_dw_c')


def _params(semantics, vmem=VMEM_LIMIT):
    return pltpu.CompilerParams(dimension_semantics=semantics, vmem_limit_bytes=vmem)


def _cols(g):
    return slice(g * GROUP, (g + 1) * GROUP)


def _full(shape):
    return pl.BlockSpec(shape, lambda *_: (0,) * len(shape))


def _silu(x):
    s = jax.nn.sigmoid(x)
    return x * s, s


def _dsilu(x, s):
    return s * (1.0 + x * (1.0 - s))


_GELU_C0 = math.sqrt(2.0 / math.pi)
_GELU_C1 = 0.044715


def _gelu(x):
    th = jnp.tanh(_GELU_C0 * (x + _GELU_C1 * (x * x * x)))
    return 0.5 * x * (1.0 + th), th


def _dgelu(x, th):
    return 0.5 * (1.0 + th) + 0.5 * x * (1.0 - th * th) * (_GELU_C0 * (1.0 + 3.0 * _GELU_C1 * (x * x)))


def _layer_norm(x):
    mu = jnp.mean(x, axis=-1, keepdims=True)
    xc = x - mu
    rstd = lax.rsqrt(jnp.mean(xc * xc, axis=-1, keepdims=True) + EPS)
    return xc * rstd, rstd


def _layer_norm_bwd(dn, n, rstd):
    return rstd * (dn - jnp.mean(dn, axis=-1, keepdims=True) - n * jnp.mean(dn * n, axis=-1, keepdims=True))


def _shift_rows(a, k):
    k = k % a.shape[0]
    return a if k == 0 else pltpu.roll(a, k, 0)


def _row_sum8(a):
    s = a[0:8]
    for m in range(1, a.shape[0] // 8):
        s = s + a[8 * m:8 * m + 8]
    return s


def _lane():
    return lax.broadcasted_iota(jnp.int32, (SUB, GROUP), 1)


def _by_quarter(lane, parts):
    return jnp.where(lane < 64, parts[0], jnp.where(lane < 128, parts[1], jnp.where(lane < 192, parts[2], parts[3])))


def _conv_inputs(z_ref, rows):
    def f(g):
        return z_ref[rows, _cols(g)].astype(F32)
    return f(A_C) * f(A_X), f(B_P), f(C_A) * jax.nn.sigmoid(f(C_GL))


def _fill_past(past_ref, zh_ref, zm_ref, is_first, tile):
    parts = _conv_inputs(zh_ref, slice(None))
    for n, a in enumerate(parts):
        past_ref[0:HALO, _cols(n)] = jnp.where(is_first, 0.0, a)

    def body(j, carry):
        r0 = pl.multiple_of(j * SUB, SUB)
        for n, a in enumerate(_conv_inputs(zm_ref, pl.ds(r0, SUB))):
            past_ref[pl.ds(r0 + HALO, SUB), _cols(n)] = a
        return carry

    lax.fori_loop(0, tile // SUB, body, 0)


def _short_conv_taps(qw):
    return [_shift_rows(qw, CONV_A - 1 - k)[HALO:WIN] for k in range(CONV_A)]


def _window_sums(pw, lane):
    s2 = pw + _shift_rows(pw, 1)
    s4 = s2 + _shift_rows(s2, 2)
    s8 = s4 + _shift_rows(s4, 4)
    s16 = s8 + _shift_rows(s8, 8)
    return _by_quarter(lane, [s[HALO:WIN] for s in (s2, s4, s8, s16)])


def _inv_count(lane, t_first):
    width = _by_quarter(lane, [2.0, 4.0, 8.0, 16.0])
    t = lax.broadcasted_iota(jnp.int32, (SUB, GROUP), 0) + t_first
    return 1.0 / jnp.minimum((t + 1).astype(F32), width)


def _forward_window_sums(ew, lane):
    n = ew.shape[0]
    f2 = ew + _shift_rows(ew, n - 1)
    f4 = f2 + _shift_rows(f2, n - 2)
    f8 = f4 + _shift_rows(f4, n - 4)
    f16 = f8 + _shift_rows(f8, n - 8)
    return _by_quarter(lane, [f[0:SUB] for f in (f2, f4, f8, f16)])


def _mixers_forward(zc, win, t_first, wc_ref, wdw_ref, vec_ref, wp_ref, w2_ref, ws_ref, bs_ref, o_c=None):
    lane = _lane()

    def vec(n):
        return vec_ref[n:n + 1, :]

    taps = _short_conv_taps(win(0))
    o_a = wc_ref[0:1, :] * taps[0] + wc_ref[1:2, :] * taps[1] + wc_ref[2:3, :] * taps[2]
    a_b, a_g = zc(A_B), zc(A_G)
    sg_a, s_a = _silu(a_g)
    y_a = a_b * o_a * sg_a

    pw = win(1)
    ic = _inv_count(lane, t_first)
    pooled = _window_sums(pw, lane) * ic - pw[HALO:WIN]
    pooled_b = pooled.astype(BF16)
    y0_b = jnp.dot(pooled_b, wp_ref[...], preferred_element_type=F32)
    b_g = zc(B_G)
    sg_b, s_b = _silu(b_g)
    y_b = y0_b * vec(V_PSCALE) * sg_b

    hw = win(2)
    if o_c is None:
        o_c = wdw_ref[CONV_C - 1:CONV_C, :] * hw[HALO:WIN] + vec(V_BDW)
        for k in range(CONV_C - 1):
            o_c = o_c + wdw_ref[k:k + 1, :] * _shift_rows(hw, CONV_C - 1 - k)[HALO:WIN]
    n_c, rstd_c = _layer_norm(o_c)
    ln_c = n_c * vec(V_LNGC) + vec(V_LNBC)
    sl_c, ssl_c = _silu(ln_c)
    sl_b = sl_c.astype(BF16)
    yc = jnp.dot(sl_b, w2_ref[...], preferred_element_type=F32) + vec(V_BPW2)
    c_g = zc(C_G)
    sg_c, s_c = _silu(c_g)
    y_c = yc * sg_c

    d_u, d_v, d_g = zc(D_U), zc(D_V), zc(D_G)
    u, th_u = _gelu(d_u)
    gv, th_v = _gelu(d_v)
    n_d, rstd_d = _layer_norm(gv)
    v_b = (n_d * vec(V_LNGD) + vec(V_LNBD)).astype(BF16)
    r = jnp.dot(ws_ref[...], v_b, preferred_element_type=F32)
    mixed = _by_quarter(lane, [r[h * SUB:(h + 1) * SUB] for h in range(4)]) + bs_ref[...]
    sg_d, s_d = _silu(d_g)
    y_d = u * mixed * sg_d

    saved = dict(lane=lane, taps=taps, o_a=o_a, a_b=a_b, a_g=a_g, sg_a=sg_a, s_a=s_a,
                 ic=ic, pooled_b=pooled_b, y0_b=y0_b, b_g=b_g, sg_b=sg_b, s_b=s_b,
                 hw=hw, o_c=o_c, n_c=n_c, rstd_c=rstd_c, ln_c=ln_c, ssl_c=ssl_c, sl_b=sl_b, yc=yc, c_g=c_g, sg_c=sg_c, s_c=s_c,
                 d_u=d_u, d_v=d_v, d_g=d_g, u=u, th_u=th_u, th_v=th_v, n_d=n_d, rstd_d=rstd_d, v_b=v_b, mixed=mixed,
                 sg_d=sg_d, s_d=s_d)
    return (y_a, y_b, y_c, y_d), saved


def _in_proj(x, gs, shift, w_in_b, tile, ride=None):
    n_tok = x.shape[0]

    def body(x_ref, gs_ref, sh_ref, w_ref, h_ref, z_ref):
        xv = x_ref[...]
        r = lax.rsqrt(jnp.mean(xv * xv, axis=-1, keepdims=True) + EPS)
        h = ((xv * r) * gs_ref[...] + sh_ref[...]).astype(BF16)
        h_ref[...] = h
        for j in range(D_IN // D_MODEL):
            cs = slice(j * D_MODEL, (j + 1) * D_MODEL)
            z_ref[:, cs] = jnp.dot(h, w_ref[:, cs], preferred_element_type=F32).astype(BF16)

    return _tiled_call(
        body, (x, gs, shift, w_in_b), name="in_proj", grid=(n_tok // tile,),
        in_specs=[pl.BlockSpec((tile, D_MODEL), lambda i: (i, 0)), _full((1, D_MODEL)), _full((1, D_MODEL)),
                  _full((D_MODEL, D_IN))],
        out_specs=[pl.BlockSpec((tile, D_MODEL), lambda i: (i, 0)), pl.BlockSpec((tile, D_IN), lambda i: (i, 0))],
        out_shape=[jax.ShapeDtypeStruct((n_tok, D_MODEL), BF16), jax.ShapeDtypeStruct((n_tok, D_IN), BF16)],
        ride=ride)


def _small_specs(with_transposes):
    specs = [_full((8, GROUP)), _full((HALO, GROUP)), _full((16, GROUP)), _full((GROUP, GROUP)), _full((GROUP, GROUP)),
             _full((4 * SUB, SUB)), _full((SUB, GROUP))]
    if with_transposes:
        specs += [_full((GROUP, GROUP)), _full((GROUP, GROUP)), _full((4 * SUB, SUB))]
    return specs


def _mix_out(z, x, gate, small, w_out_b, tile, ride=None, head=None):
    n_tok = x.shape[0]
    n_tiles = n_tok // tile
    n_sub = tile // SUB
    cw = D_MODEL // n_sub
    per_halo = tile // HALO
    n_in = 12 + (2 if head else 0)
    n_out = 4 if head else 2

    def cur(i):
        return jnp.minimum(i, n_tiles - 1)

    def prev(i):
        return jnp.maximum(i - 1, 0)

    def body(*refs):
        (zm_ref, zh_ref, x_ref, gate_ref, wc_ref, wdw_ref, vec_ref, wp_ref, w2_ref, ws_ref, bs_ref, wout_ref) = refs[:12]
        xo_ref, oc_ref = refs[n_in:n_in + 2]
        past_ref, ycat_ref, ycat_prev_ref = refs[n_in + n_out:n_in + n_out + 3]
        i = pl.program_id(0)
        t = cur(i)
        if head:
            g_ref, tgt_ref = refs[12:14]
            loss_ref, dg_ref = refs[n_in + 2:n_in + 4]
            xn_ref, acc_ref = refs[n_in + n_out + 3:]
        else:
            xn_ref = xo_ref

        @pl.when(i == 0)
        def _():
            ycat_prev_ref[...] = jnp.zeros_like(ycat_prev_ref)
            if head:
                acc_ref[...] = jnp.zeros_like(acc_ref)

        _fill_past(past_ref, zh_ref, zm_ref, t == 0, tile)
        for j in range(n_sub):
            cs = slice(j * cw, (j + 1) * cw)
            y = jnp.dot(ycat_prev_ref[...], wout_ref[:, cs], preferred_element_type=F32)
            xn_ref[:, cs] = x_ref[:, cs] + gate_ref[:, cs] * y
            rows = slice(j * SUB, (j + 1) * SUB)
            ys, s = _mixers_forward(
                lambda g: zm_ref[rows, _cols(g)].astype(F32), lambda n: past_ref[j * SUB:j * SUB + WIN, _cols(n)],
                t * tile + j * SUB, wc_ref, wdw_ref, vec_ref, wp_ref, w2_ref, ws_ref, bs_ref)
            for n, y in enumerate(ys):
                ycat_ref[rows, _cols(n)] = y.astype(BF16)
            oc_ref[rows, :] = s["o_c"]
        ycat_prev_ref[...] = ycat_ref[...]
        if head:
            counted = jnp.where(i > 0, 1.0, 0.0)
            xo_ref[...] = _loss_head_block(xn_ref[...], g_ref[...], tgt_ref[...], acc_ref, counted)

            @pl.when(i == n_tiles)
            def _():
                loss_ref[...] = jnp.full((8, 128), 0.5 / D_MODEL, F32) * jnp.sum(acc_ref[0])
                dg_ref[...] = jnp.sum(acc_ref[1], axis=0, keepdims=True)

    in_specs = [pl.BlockSpec((tile, D_IN), lambda i: (cur(i), 0)),
                pl.BlockSpec((HALO, D_IN), lambda i: (jnp.maximum(cur(i) * per_halo - 1, 0), 0)),
                pl.BlockSpec((tile, D_MODEL), lambda i: (prev(i), 0)), _full((1, D_MODEL)),
                *_small_specs(False), _full((D_MODEL, D_MODEL))]
    out_specs = [pl.BlockSpec((tile, D_MODEL), lambda i: (prev(i), 0)), pl.BlockSpec((tile, GROUP), lambda i: (cur(i), 0))]
    out_shape = [jax.ShapeDtypeStruct((n_tok, D_MODEL), F32), jax.ShapeDtypeStruct((n_tok, GROUP), F32)]
    scratch = [pltpu.VMEM((tile + HALO, 3 * GROUP), F32), pltpu.VMEM((tile, D_MODEL), BF16), pltpu.VMEM((tile, D_MODEL), BF16)]
    args = (z, z, x, gate, *small, w_out_b)
    if head:
        in_specs += [_full((1, D_MODEL)), pl.BlockSpec((tile, D_MODEL), lambda i: (prev(i), 0))]
        out_specs += [_full((8, 128)), _full((1, D_MODEL))]
        out_shape += [jax.ShapeDtypeStruct((8, 128), F32), jax.ShapeDtypeStruct((1, D_MODEL), F32)]
        scratch += [pltpu.VMEM((tile, D_MODEL), F32), pltpu.VMEM((2, 8, D_MODEL), F32)]
        args += tuple(head)
    outs, rode = _tiled_call(body, args, name="mix_out", grid=(n_tiles + 1,), in_specs=in_specs, out_specs=out_specs,
                             out_shape=out_shape, scratch_shapes=scratch, ride=ride)
    return outs, rode


def _loss_head_block(xv, g, target, acc_ref, counted):
    r = lax.rsqrt(jnp.mean(xv * xv, axis=-1, keepdims=True) + EPS)
    xn = xv * r
    err = xn * g - target
    acc_ref[0] = acc_ref[0] + counted * _row_sum8(err * err)
    dy = err * (1.0 / D_MODEL)
    acc_ref[1] = acc_ref[1] + counted * _row_sum8(dy * xn)
    a = dy * g
    return r * (a - xn * jnp.mean(a * xn, axis=-1, keepdims=True))


def _mix_bwd(z, o_c, dx_next, gate, small, small_t, w_out_b, tile, ride=None):
    n_tok = z.shape[0]
    n_tiles = n_tok // tile
    n_sub = tile // SUB
    cw = D_MODEL // n_sub
    per_halo = tile // HALO
    nt_dims = (((1,), (1,)), ((), ()))

    def tile_of(i):
        return n_tiles - 1 - i

    def next_tile_of(i):
        return jnp.maximum(n_tiles - 2 - i, 0)

    def body(zm_ref, zh_ref, oc_ref, dxn_ref, dxn_next_ref, gate_ref, wc_ref, wdw_ref, vec_ref, wp_ref, w2_ref, ws_ref,
             bs_ref, wpt_ref, w2t_ref, wst_ref, wout_ref,
             dz_ref, ycat_ref, sums_ref, dwp_ref, dw2_ref, dws_ref, dbs_ref,
             past_ref, future_ref, dy_ref, dy_next_ref, acc_ref):
        i = pl.program_id(0)
        t = tile_of(i)

        @pl.when(i == 0)
        def _():
            acc_ref[...] = jnp.zeros_like(acc_ref)
            dwp_ref[...] = jnp.zeros_like(dwp_ref)
            dw2_ref[...] = jnp.zeros_like(dw2_ref)
            dws_ref[...] = jnp.zeros_like(dws_ref)
            dbs_ref[...] = jnp.zeros_like(dbs_ref)
            future_ref[tile:tile + HALO, :] = jnp.zeros((HALO, 3 * GROUP), F32)
            dy_ref[...] = lax.dot_general((dxn_ref[...] * gate_ref[...]).astype(BF16), wout_ref[...], nt_dims,
                                        preferred_element_type=F32)

        _fill_past(past_ref, zh_ref, zm_ref, t == 0, tile)
        dyb_next = (dxn_next_ref[...] * gate_ref[...]).astype(BF16)

        def vec(n):
            return vec_ref[n:n + 1, :]

        for jj in range(n_sub):
            j = n_sub - 1 - jj
            r0 = j * SUB
            rows = slice(r0, r0 + SUB)

            def zc(g):
                return zm_ref[rows, _cols(g)].astype(F32)

            def add(n, a):
                acc_ref[n] = acc_ref[n] + _row_sum8(a)

            def put(g, a):
                dz_ref[rows, _cols(g)] = a.astype(BF16)

            def future_window(n, a):
                future_ref[rows, _cols(n)] = a
                return future_ref[r0:r0 + WIN, _cols(n)]

            ys, s = _mixers_forward(zc, lambda n: past_ref[r0:r0 + WIN, _cols(n)], t * tile + r0,
                                    wc_ref, wdw_ref, vec_ref, wp_ref, w2_ref, ws_ref, bs_ref, o_c=oc_ref[rows, :])
            for n, y in enumerate(ys):
                ycat_ref[rows, _cols(n)] = y.astype(BF16)
            lane = s["lane"]
            ks = slice(jj * cw, (jj + 1) * cw)
            dy_next_ref[:, ks] = lax.dot_general(dyb_next, wout_ref[ks, :], nt_dims, preferred_element_type=F32)

            dy = dy_ref[rows,_cols(0)]
            put(A_B, dy * s["o_a"] * s["sg_a"])
            put(A_G, dy * s["a_b"] * s["o_a"] * _dsilu(s["a_g"], s["s_a"]))
            do = dy * s["a_b"] * s["sg_a"]
            for k in range(CONV_A):
                add(S_WCONV + k, do * s["taps"][k])
            dow = future_window(0, do)
            dq = wc_ref[CONV_A - 1:CONV_A, :] * dow[0:SUB]
            for k in range(CONV_A - 1):
                dq = dq + wc_ref[k:k + 1, :] * _shift_rows(dow, WIN - (CONV_A - 1 - k))[0:SUB]
            put(A_C, dq * zc(A_X))
            put(A_X, dq * zc(A_C))

            dy = dy_ref[rows,_cols(1)]
            put(B_G, dy * (s["y0_b"] * vec(V_PSCALE)) * _dsilu(s["b_g"], s["s_b"]))
            dyb = dy * s["sg_b"]
            add(S_PSCALE, dyb * s["y0_b"])
            dpw_b = (dyb * vec(V_PSCALE)).astype(BF16)
            dwp_ref[...] += lax.dot_general(s["pooled_b"], dpw_b, (((0,), (0,)), ((), ())), preferred_element_type=F32)
            dpooled = jnp.dot(dpw_b, wpt_ref[...], preferred_element_type=F32)
            ew = future_window(1, dpooled * s["ic"])
            put(B_P, _forward_window_sums(ew, lane) - dpooled)

            dy = dy_ref[rows,_cols(2)]
            put(C_G, dy * s["yc"] * _dsilu(s["c_g"], s["s_c"]))
            dyc = dy * s["sg_c"]
            add(S_BPW2, dyc)
            dyc_b = dyc.astype(BF16)
            dw2_ref[...] += lax.dot_general(s["sl_b"], dyc_b, (((0,), (0,)), ((), ())), preferred_element_type=F32)
            dln = jnp.dot(dyc_b, w2t_ref[...], preferred_element_type=F32) * _dsilu(s["ln_c"], s["ssl_c"])
            add(S_LNGC, dln * s["n_c"])
            add(S_LNBC, dln)
            do = _layer_norm_bwd(dln * vec(V_LNGC), s["n_c"], s["rstd_c"])
            add(S_BDW, do)
            hw = s["hw"]
            for k in range(CONV_C):
                add(S_WDW + k, do * _shift_rows(hw, CONV_C - 1 - k)[HALO:WIN])
            dow = future_window(2, do)
            dhc = wdw_ref[CONV_C - 1:CONV_C, :] * dow[0:SUB]
            for k in range(CONV_C - 1):
                dhc = dhc + wdw_ref[k:k + 1, :] * _shift_rows(dow, WIN - (CONV_C - 1 - k))[0:SUB]
            c_a = zc(C_A)
            sgl = jax.nn.sigmoid(zc(C_GL))
            put(C_A, dhc * sgl)
            put(C_GL, dhc * c_a * sgl * (1.0 - sgl))

            dy = dy_ref[rows,_cols(3)]
            put(D_G, dy * s["u"] * s["mixed"] * _dsilu(s["d_g"], s["s_d"]))
            put(D_U, dy * s["mixed"] * s["sg_d"] * _dgelu(s["d_u"], s["th_u"]))
            dmixed = dy * s["u"] * s["sg_d"]
            dbs_ref[...] += dmixed
            by_head = jnp.concatenate(
                [jnp.where((lane >= 64 * h) & (lane < 64 * h + 64), dmixed, 0.0) for h in range(4)], axis=0).astype(BF16)
            dws_ref[...] += lax.dot_general(by_head, s["v_b"], (((1,), (1,)), ((), ())), preferred_element_type=F32)
            rv = jnp.dot(wst_ref[...], dmixed.astype(BF16), preferred_element_type=F32)
            dv = _by_quarter(lane, [rv[h * SUB:(h + 1) * SUB] for h in range(4)])
            add(S_LNGD, dv * s["n_d"])
            add(S_LNBD, dv)
            dgv = _layer_norm_bwd(dv * vec(V_LNGD), s["n_d"], s["rstd_d"])
            put(D_V, dgv * _dgelu(s["d_v"], s["th_v"]))

        future_ref[tile:tile + HALO, :] = future_ref[0:HALO, :]
        dy_ref[...] = dy_next_ref[...]

        @pl.when(i == n_tiles - 1)
        def _():
            for n in range(N_SUMS):
                sums_ref[n:n + 1, :] = jnp.sum(acc_ref[n], axis=0, keepdims=True)

    return _tiled_call(
        body, (z, z, o_c, dx_next, dx_next, gate, *small, *small_t, w_out_b), name="mix_bwd", grid=(n_tiles,),
        in_specs=[pl.BlockSpec((tile, D_IN), lambda i: (tile_of(i), 0)),
                  pl.BlockSpec((HALO, D_IN), lambda i: (jnp.maximum(tile_of(i) * per_halo - 1, 0), 0)),
                  pl.BlockSpec((tile, GROUP), lambda i: (tile_of(i), 0)),
                  pl.BlockSpec((tile, D_MODEL), lambda i: (tile_of(i), 0)),
                  pl.BlockSpec((tile, D_MODEL), lambda i: (next_tile_of(i), 0)), _full((1, D_MODEL)),
                  *_small_specs(True), _full((D_MODEL, D_MODEL))],
        out_specs=[pl.BlockSpec((tile, D_IN), lambda i: (tile_of(i), 0)),
                   pl.BlockSpec((tile, D_MODEL), lambda i: (tile_of(i), 0)),
                   _full((N_SUMS, GROUP)), _full((GROUP, GROUP)), _full((GROUP, GROUP)), _full((4 * SUB, SUB)),
                   _full((SUB, GROUP))],
        out_shape=[jax.ShapeDtypeStruct((n_tok, D_IN), BF16), jax.ShapeDtypeStruct((n_tok, D_MODEL), BF16),
                   jax.ShapeDtypeStruct((N_SUMS, GROUP), F32), jax.ShapeDtypeStruct((GROUP, GROUP), F32),
                   jax.ShapeDtypeStruct((GROUP, GROUP), F32), jax.ShapeDtypeStruct((4 * SUB, SUB), F32),
                   jax.ShapeDtypeStruct((SUB, GROUP), F32)],
        scratch_shapes=[pltpu.VMEM((tile + HALO, 3 * GROUP), F32), pltpu.VMEM((tile + HALO, 3 * GROUP), F32),
                        pltpu.VMEM((tile, D_MODEL), F32), pltpu.VMEM((tile, D_MODEL), F32),
                        pltpu.VMEM((N_SUMS, 8, GROUP), F32)], ride=ride)


def _norm_bwd(x, dz, dx_next, gs, w_in_b, tile, ride=None):
    n_tok = x.shape[0]
    n_tiles = n_tok // tile

    def body(x_ref, dz_ref, dxn_ref, gs_ref, w_ref, dx_ref, dsh_ref, dgs_ref, acc_ref):
        i = pl.program_id(0)

        @pl.when(i == 0)
        def _():
            acc_ref[...] = jnp.zeros_like(acc_ref)

        dh = lax.dot_general(dz_ref[...], w_ref[...], (((1,), (1,)), ((), ())), preferred_element_type=F32)
        xv = x_ref[...]
        r = lax.rsqrt(jnp.mean(xv * xv, axis=-1, keepdims=True) + EPS)
        xn = xv * r
        acc_ref[0] = acc_ref[0] + _row_sum8(dh)
        acc_ref[1] = acc_ref[1] + _row_sum8(dh * xn)
        dxn = dh * gs_ref[...]
        dx_ref[...] = dxn_ref[...] + r * (dxn - xn * jnp.mean(dxn * xn, axis=-1, keepdims=True))

        @pl.when(i == n_tiles - 1)
        def _():
            dsh_ref[...] = jnp.sum(acc_ref[0], axis=0, keepdims=True)
            dgs_ref[...] = jnp.sum(acc_ref[1], axis=0, keepdims=True)

    return _tiled_call(
        body, (x, dz, dx_next, gs, w_in_b), name="norm_bwd", grid=(n_tiles,),
        in_specs=[pl.BlockSpec((tile, D_MODEL), lambda i: (i, 0)), pl.BlockSpec((tile, D_IN), lambda i: (i, 0)),
                  pl.BlockSpec((tile, D_MODEL), lambda i: (i, 0)), _full((1, D_MODEL)), _full((D_MODEL, D_IN))],
        out_specs=[pl.BlockSpec((tile, D_MODEL), lambda i: (i, 0)), _full((1, D_MODEL)), _full((1, D_MODEL))],
        out_shape=[jax.ShapeDtypeStruct((n_tok, D_MODEL), F32), jax.ShapeDtypeStruct((1, D_MODEL), F32),
                   jax.ShapeDtypeStruct((1, D_MODEL), F32)],
        scratch_shapes=[pltpu.VMEM((2, 8, D_MODEL), F32)], ride=ride)


def _tokens_matmul(a, b, name, out_dtype=F32, ride=None):
    n_tok, ka = a.shape
    nb = b.shape[1]
    tk = min(REDUCE_TILE, n_tok)
    cb = min(D_MODEL, nb)
    n_steps = n_tok // tk

    def body(a_ref, b_ref, o_ref, acc_ref):
        i = pl.program_id(1)

        @pl.when(i == 0)
        def _():
            acc_ref[...] = jnp.zeros_like(acc_ref)

        acc_ref[...] += lax.dot_general(a_ref[...], b_ref[...].astype(BF16), (((0,), (0,)), ((), ())),
                                        preferred_element_type=F32)

        @pl.when(i == n_steps - 1)
        def _():
            o_ref[...] = acc_ref[...].astype(out_dtype)

    (out,), rode = _tiled_call(
        body, (a, b), name=name, grid=(nb // cb, n_steps),
        in_specs=[pl.BlockSpec((tk, ka), lambda j, i: (i, 0)), pl.BlockSpec((tk, cb), lambda j, i: (i, j))],
        out_specs=[pl.BlockSpec((ka, cb), lambda j, i: (0, j))],
        out_shape=[jax.ShapeDtypeStruct((ka, nb), out_dtype)],
        scratch_shapes=[pltpu.VMEM((ka, cb), F32)], ride=ride)
    return out, rode


def _out_proj_grads(m, w_out_b, gate):
    rb = 256
    n_blocks = D_MODEL // rb

    def body(m_ref, w_ref, gate_ref, dw_ref, dgate_ref, acc_ref):
        i = pl.program_id(0)

        @pl.when(i == 0)
        def _():
            acc_ref[...] = jnp.zeros_like(acc_ref)

        mv = m_ref[...]
        dw_ref[...] = (mv * gate_ref[...]).astype(BF16)
        acc_ref[...] += _row_sum8(mv * w_ref[...].astype(F32))

        @pl.when(i == n_blocks - 1)
        def _():
            dgate_ref[...] = jnp.sum(acc_ref[...], axis=0, keepdims=True)

    return pl.pallas_call(
        body, name="out_proj_grads", grid=(n_blocks,),
        in_specs=[pl.BlockSpec((rb, D_MODEL), lambda i: (i, 0)), pl.BlockSpec((rb, D_MODEL), lambda i: (i, 0)),
                  _full((1, D_MODEL))],
        out_specs=[pl.BlockSpec((rb, D_MODEL), lambda i: (i, 0)), _full((1, D_MODEL))],
        out_shape=[jax.ShapeDtypeStruct((D_MODEL, D_MODEL), BF16), jax.ShapeDtypeStruct((1, D_MODEL), F32)],
        scratch_shapes=[pltpu.VMEM((8, D_MODEL), F32)],
        compiler_params=_params(("arbitrary",)),
    )(m, w_out_b, gate)


def _modulation_columns(c_all, w_ada, b_cols):
    cols = w_ada.shape[2]

    def body(c_ref, w_ref, b_ref, ca_ref, mod_ref):
        ca, _ = _silu(c_ref[...])
        ca_ref[...] = ca
        for l in range(N_LAYERS):
            mod_ref[l] = jnp.dot(ca, w_ref[l], precision=lax.Precision.HIGHEST, preferred_element_type=F32) + b_ref[l:l + 1, :]

    return pl.pallas_call(
        body, name="modulation_columns",
        out_shape=[jax.ShapeDtypeStruct((N_DEV, D_MODEL), F32), jax.ShapeDtypeStruct((N_LAYERS, N_DEV, cols), F32)],
        compiler_params=pltpu.CompilerParams(vmem_limit_bytes=VMEM_LIMIT),
    )(c_all, w_ada, b_cols)


def _adam(w, g, m, v):
    m2 = ADAM_B1 * m + (1.0 - ADAM_B1) * g
    v2 = ADAM_B2 * v + (1.0 - ADAM_B2) * (g * g)
    m_hat = m2 / (1.0 - ADAM_B1 ** ADAM_STEP)
    v_hat = v2 / (1.0 - ADAM_B2 ** ADAM_STEP)
    return -ADAM_LR * (m_hat / (jnp.sqrt(v_hat) + ADAM_EPS) + ADAM_WD * w), m2, v2


def _row_block(rows, cols, slots):
    target = max(8, (1 << 19) // (cols * max(slots, 1)))
    rb = rows
    while rb > target and rb % 2 == 0 and (rb // 2) % 8 == 0:
        rb //= 2
    return rb


def _adam_update(w, g, m, v, name):
    rows, cols = w.shape
    slotted = g.ndim == 3
    rb = _row_block(rows, cols, N_DEV if slotted else 1)

    def body(w_ref, g_ref, m_ref, v_ref, go_ref, d_ref, mo_ref, vo_ref):
        if slotted:
            gv = g_ref[0].astype(F32)
            for q in range(1, N_DEV):
                gv = gv + g_ref[q].astype(F32)
        else:
            gv = g_ref[...]
        go_ref[...] = gv
        d_ref[...], mo_ref[...], vo_ref[...] = _adam(w_ref[...], gv, m_ref[...], v_ref[...])

    blk = pl.BlockSpec((rb, cols), lambda i: (i, 0))
    g_blk = pl.BlockSpec((N_DEV, rb, cols), lambda i: (0, i, 0)) if slotted else blk
    return pl.pallas_call(
        body, name=name, grid=(rows // rb,),
        in_specs=[blk, g_blk, blk, blk], out_specs=[blk] * 4,
        out_shape=[jax.ShapeDtypeStruct((rows, cols), F32)] * 4,
        compiler_params=_params(("parallel",)),
    )(w, g, m, v)


def _adam_many(ws, gs, ms, vs, name):
    n = len(ws)

    def body(*refs):
        w_refs, g_refs, m_refs, v_refs, d_refs, mo_refs, vo_refs = (refs[k * n:(k + 1) * n] for k in range(7))
        for j in range(n):
            d_refs[j][...], mo_refs[j][...], vo_refs[j][...] = _adam(w_refs[j][...], g_refs[j][...], m_refs[j][...],
                                                                  v_refs[j][...])

    res = pl.pallas_call(
        body, name=name, out_shape=[jax.ShapeDtypeStruct(w.shape, F32) for w in ws] * 3,
        compiler_params=pltpu.CompilerParams(vmem_limit_bytes=VMEM_LIMIT),
    )(*ws, *gs, *ms, *vs)
    return res[:n], res[n:2 * n], res[2 * n:]


def _as_rows(a):
    return a.reshape(-1, a.shape[-1]) if a.ndim > 1 else a.reshape(1, -1)


def _ada_update(ca_t, dmod_cols, w, m, v):
    _, rows, cols = w.shape

    def body(ca_ref, dm_ref, w_ref, m_ref, v_ref, g_ref, d_ref, mo_ref, vo_ref):
        g = ca_ref[:, 0:1] * dm_ref[0, 0:1, :]
        for b in range(1, N_DEV):
            g = g + ca_ref[:, b:b + 1] * dm_ref[0, b:b + 1, :]
        g_ref[0] = g
        d_ref[0], mo_ref[0], vo_ref[0] = _adam(w_ref[0], g, m_ref[0], v_ref[0])

    blk = pl.BlockSpec((1, rows, cols), lambda l: (l, 0, 0))
    return pl.pallas_call(
        body, name="ada_update", grid=(N_LAYERS,),
        in_specs=[_full((rows, N_DEV)), pl.BlockSpec((1, N_DEV, cols), lambda l: (l, 0, 0)), blk, blk, blk],
        out_specs=[blk] * 4, out_shape=[jax.ShapeDtypeStruct(w.shape, F32)] * 4,
        compiler_params=_params(("parallel",)),
    )(ca_t, dmod_cols, w, m, v)


def _exchange_sems(n):
    return [pltpu.SemaphoreType.DMA((n, N_DEV - 1)), pltpu.SemaphoreType.DMA((n, N_DEV - 1)),
            pltpu.SemaphoreType.DMA((n,))]


def _exchange_copies(plans, srcs, outs, sems, receiving, only=None):
    send_sems, recv_sems, local_sems = sems
    x, y, c = lax.axis_index("x"), lax.axis_index("y"), lax.axis_index("c")
    me = 4 * x + 2 * y + c

    def remote(i, k, incoming):
        _, o, send, land = plans[i]
        px = 1 - x if k & 4 else x
        py = 1 - y if k & 2 else y
        pc = 1 - c if k & 1 else c
        p = 4 * px + 2 * py + pc
        return pltpu.make_async_remote_copy(
            src_ref=send(srcs[i], p), dst_ref=land(outs[o], p if incoming else me),
            send_sem=send_sems.at[i, k - 1], recv_sem=recv_sems.at[i, k - 1],
            device_id=(px, py, pc), device_id_type=pl.DeviceIdType.MESH)

    which = range(len(plans)) if only is None else only
    pairs = [(i, k) for k in range(1, N_DEV) for i in which]
    local = [pltpu.make_async_copy(plans[i][2](srcs[i], me), plans[i][3](outs[plans[i][1]], me), local_sems.at[i])
             for i in which]
    return local, [remote(i, k, False) for i, k in pairs], [remote(i, k, True) for i, k in pairs] if receiving else []


def _exchange_start(plans, srcs, outs, sems, only=None):
    local, outgoing, _ = _exchange_copies(plans, srcs, outs, sems, False, only)
    for cp in local + outgoing:
        cp.start()


def _exchange_wait(plans, srcs, outs, sems, only=None):
    local, outgoing, incoming = _exchange_copies(plans, srcs, outs, sems, True, only)
    for cp in incoming:
        cp.wait_recv()
    for cp in outgoing:
        cp.wait_send()
    for cp in local:
        cp.wait()


def _exchange(name, ride):
    out_shapes, plans = ride
    n = len(plans)
    hbm = pl.BlockSpec(memory_space=pltpu.HBM)

    def body(*refs):
        srcs, outs, sems = refs[:n], refs[n:n + len(out_shapes)], refs[n + len(out_shapes):]
        _exchange_start(plans, srcs, outs, sems)
        _exchange_wait(plans, srcs, outs, sems)

    return pl.pallas_call(
        body, name=name, in_specs=[hbm] * n, out_specs=[hbm] * len(out_shapes), out_shape=list(out_shapes),
        scratch_shapes=_exchange_sems(n),
    )(*[p[0] for p in plans])


def _first_gather(c, w):
    rows, n = w.shape
    c_shapes, c_plans = _plans([c], [_gather])
    hbm = pl.BlockSpec(memory_space=pltpu.HBM)

    def body(c_ref, w_ref, c_all_ref, w_all_ref, send_sems, recv_sems, local_sem, *c_sems):
        x, y, core = lax.axis_index("x"), lax.axis_index("y"), lax.axis_index("c")
        me, sibling = (x, y, core), (x, y, 1 - core)
        chips = [(1 - x, y), (x, 1 - y), (1 - x, 1 - y)]

        def block(px, py, pc):
            return w_all_ref.at[:, pl.ds(pl.multiple_of((4 * px + 2 * py + pc) * n, n), n)]

        def copy(k, origin, to, src=None):
            return pltpu.make_async_remote_copy(
                src_ref=block(*origin) if src is None else src, dst_ref=block(*origin),
                send_sem=send_sems.at[k], recv_sem=recv_sems.at[k], device_id=to, device_id_type=pl.DeviceIdType.MESH)

        _exchange_start(c_plans, [c_ref], [c_all_ref], c_sems)
        mine = pltpu.make_async_copy(w_ref, block(*me), local_sem)
        mine.start()
        first = [copy(0, me, sibling, src=w_ref)]
        first += [copy(1 + j, me, (*chip, core), src=w_ref) for j, chip in enumerate(chips)]
        for cp in first:
            cp.start()
        passed = [copy(4 + j, (*chip, core), sibling) for j, chip in enumerate(chips)]
        for j, chip in enumerate(chips):
            copy(1 + j, (*chip, core), me).wait_recv()
            passed[j].start()
        copy(0, sibling, me).wait_recv()
        for j, chip in enumerate(chips):
            copy(4 + j, (*chip, 1 - core), me).wait_recv()
        for cp in first + passed:
            cp.wait_send()
        mine.wait()
        _exchange_wait(c_plans, [c_ref], [c_all_ref], c_sems)

    return pl.pallas_call(
        body, name="first_gather", in_specs=[hbm, hbm], out_specs=[hbm, hbm],
        out_shape=[c_shapes[0], jax.ShapeDtypeStruct((rows, N_DEV * n), w.dtype)],
        scratch_shapes=[pltpu.SemaphoreType.DMA((N_DEV - 1,)), pltpu.SemaphoreType.DMA((N_DEV - 1,)),
                        pltpu.SemaphoreType.DMA(()), *_exchange_sems(1)],
    )(c, w)


def _finish_exchange(big, big_rules, packed, dmod):
    n_rows = packed.shape[0]
    r = n_rows // N_DEV
    shapes, plans = _plans([*big, packed, dmod], [*big_rules, _scatter_rows, _gather])
    n_first = len(plans)
    i_small = n_first - 2
    _, send, land = _gather_rows(jax.ShapeDtypeStruct((r, 128), F32))
    plans = plans + [(None, len(shapes), send, land)]
    shapes = shapes + [jax.ShapeDtypeStruct((n_rows, 128), F32)]
    first = [i for i in range(n_first) if i != i_small]
    hbm = pl.BlockSpec(memory_space=pltpu.HBM)

    def body(*refs):
        srcs, outs = list(refs[:n_first]), refs[n_first:n_first + len(shapes)]
        parts_ref, sum_ref, local_sem = refs[n_first + len(shapes):n_first + len(shapes) + 3]
        sems = refs[n_first + len(shapes) + 3:]
        srcs.append(sum_ref)
        _exchange_start(plans, srcs, outs, sems, only=range(n_first))
        _exchange_wait(plans, srcs, outs, sems, only=[i_small])
        cp = pltpu.make_async_copy(outs[i_small], parts_ref, local_sem)
        cp.start()
        cp.wait()
        g = parts_ref[0]
        for q in range(1, N_DEV):
            g = g + parts_ref[q]
        sum_ref[...] = g
        _exchange_start(plans, srcs, outs, sems, only=[n_first])
        _exchange_wait(plans, srcs, outs, sems, only=[n_first])
        _exchange_wait(plans, srcs, outs, sems, only=first)

    res = pl.pallas_call(
        body, name="finish_exchange", in_specs=[hbm] * n_first, out_specs=[hbm] * len(shapes), out_shape=shapes,
        scratch_shapes=[pltpu.VMEM((N_DEV, r, 128), F32), pltpu.VMEM((r, 128), F32), pltpu.SemaphoreType.DMA(()),
                        *_exchange_sems(len(plans))],
    )(*big, packed, dmod)
    return (*res[:len(big)], res[-1], res[n_first - 1])


def _tiled_call(body, args, *, name, grid, in_specs, out_specs, out_shape, scratch_shapes=(), ride=None):
    params = _params(("arbitrary",) * len(grid))
    if ride is None:
        return pl.pallas_call(body, name=name, grid=grid, in_specs=in_specs, out_specs=out_specs, out_shape=out_shape,
                              scratch_shapes=list(scratch_shapes), compiler_params=params)(*args), []
    shapes, plans = ride
    n_in, n_src, n_out, n_dst, n_scr = len(in_specs), len(plans), len(out_specs), len(shapes), len(scratch_shapes)
    hbm = pl.BlockSpec(memory_space=pltpu.HBM)

    def carrying(*refs):
        ins, srcs, refs = refs[:n_in], refs[n_in:n_in + n_src], refs[n_in + n_src:]
        outs, dsts, refs = refs[:n_out], refs[n_out:n_out + n_dst], refs[n_out + n_dst:]
        scratch, sems = refs[:n_scr], refs[n_scr:]
        ids = [pl.program_id(a) for a in range(len(grid))]
        first = functools.reduce(jnp.logical_and, [i == 0 for i in ids])
        last = functools.reduce(jnp.logical_and, [i == g - 1 for i, g in zip(ids, grid)])

        @pl.when(first)
        def _():
            _exchange_start(plans, srcs, dsts, sems)

        body(*ins, *outs, *scratch)

        @pl.when(last)
        def _():
            _exchange_wait(plans, srcs, dsts, sems)

    res = pl.pallas_call(
        carrying, name=name, grid=grid, in_specs=[*in_specs, *[hbm] * n_src], out_specs=[*out_specs, *[hbm] * n_dst],
        out_shape=[*out_shape, *shapes], scratch_shapes=[*scratch_shapes, *_exchange_sems(n_src)],
        compiler_params=params)(*args, *[p[0] for p in plans])
    return res[:n_out], res[n_out:]


def _tail(nd, idx):
    return (slice(None),) * (nd - 2) + idx


def _gather(a):
    return jax.ShapeDtypeStruct((N_DEV,) + a.shape, a.dtype), lambda s, p: s, lambda o, q: o.at[q]


def _gather_rows(a):
    r = a.shape[-2]
    return (jax.ShapeDtypeStruct(a.shape[:-2] + (N_DEV * r, a.shape[-1]), a.dtype), lambda s, p: s,
            lambda o, q: o.at[_tail(a.ndim, (pl.ds(pl.multiple_of(q * r, r), r), slice(None)))])


def _gather_cols(a):
    c = a.shape[-1]
    return (jax.ShapeDtypeStruct(a.shape[:-1] + (N_DEV * c,), a.dtype), lambda s, p: s,
            lambda o, q: o.at[_tail(a.ndim, (slice(None), pl.ds(pl.multiple_of(q * c, c), c)))])


def _scatter_rows(a):
    r = a.shape[0] // N_DEV
    return (jax.ShapeDtypeStruct((N_DEV, r, a.shape[1]), a.dtype),
            lambda s, p: s.at[pl.ds(pl.multiple_of(p * r, r), r), :], lambda o, q: o.at[q])


def _scatter_cols(a):
    c = a.shape[1] // N_DEV
    return (jax.ShapeDtypeStruct((N_DEV, a.shape[0], c), a.dtype),
            lambda s, p: s.at[:, pl.ds(pl.multiple_of(p * c, c), c)], lambda o, q: o.at[q])


def _plans(arrays, rules):
    shapes, plans = [], []
    for o, (a, rule) in enumerate(zip(arrays, rules)):
        shape, send, land = rule(a)
        shapes.append(shape)
        plans.append((a, o, send, land))
    return shapes, plans


def _pack(pieces, rows_multiple=8):
    flat = []
    for a in pieces:
        f = a.reshape(-1)
        flat.append(jnp.pad(f, (0, (-f.shape[0]) % 128)))
    total = sum(f.shape[0] for f in flat)
    flat.append(jnp.zeros(((-total) % (128 * rows_multiple),), F32))
    return jnp.concatenate(flat).reshape(-1, 128)


def _unpack(buf, shapes, lead=()):
    flat = buf.reshape(lead + (-1,))
    out, off = [], 0
    for s in shapes:
        n = math.prod(s)
        out.append(flat[..., off:off + n].reshape(lead + tuple(s)))
        off += n + (-n) % 128
    return out


def _pad_rows(a, rows):
    return jnp.pad(a, ((0, rows - a.shape[0]), (0, 0)))


VEC_NAMES = ('pool_scale', 'b_dw_c', 'ln_g_c', 'ln_b_c', 'b_pw2_c', 'ln_g_d', 'ln_b_d')
GATHERED = ('w_in', 'w_out', 'w_pw2_c', 'w_conv_a', 'w_dw_c')
GATHER_RULES = (_gather_cols, _gather_rows, _gather_rows, _gather, _gather)
SCATTER_RULES = (_scatter_cols, _scatter_rows, _scatter_rows)


def _weight_shards(shard, l):
    return [shard[n][l].astype(BF16) if n in ('w_in', 'w_out') else shard[n][l] for n in GATHERED]


def _layer_weights(shard, l, gathered):
    w_in_b, w_out_b, w_pw2, wconv_parts, wdw_parts = gathered
    wconv = wconv_parts.transpose(1, 0, 2).reshape(CONV_A, GROUP)
    wdw = wdw_parts.transpose(1, 0, 2).reshape(CONV_C, GROUP)
    wp = jnp.einsum('gcd,gh->gchd', shard['w_pool'][l], jnp.eye(4, dtype=F32)).reshape(GROUP, GROUP)
    ws = shard['w_s_d'][l] * jnp.tril(jnp.ones((SUB, SUB), F32))
    vec = jnp.stack([shard[n][l] for n in VEC_NAMES])
    small = (_pad_rows(wconv, 8), _pad_rows(wdw, HALO), _pad_rows(vec, 16), wp.astype(BF16), w_pw2.astype(BF16),
             ws.reshape(4 * SUB, SUB).astype(BF16), jnp.repeat(shard['b_s_d'][l].T, 64, axis=1))
    small_t = (wp.T.astype(BF16), w_pw2.T.astype(BF16), ws.transpose(0, 2, 1).reshape(4 * SUB, SUB).astype(BF16))
    return w_in_b, w_out_b, small, small_t


def kernel(x, c, norm_g, w_ada, b_ada, w_in, w_conv_a, w_pool, pool_scale, w_dw_c, b_dw_c, ln_g_c, ln_b_c, w_pw2_c, b_pw2_c, ln_g_d, ln_b_d, w_s_d, b_s_d, w_out, final_g, loss_target, m_norm_g, m_w_ada, m_b_ada, m_w_in, m_w_conv_a, m_w_pool, m_pool_scale, m_w_dw_c, m_b_dw_c, m_ln_g_c, m_ln_b_c, m_w_pw2_c, m_b_pw2_c, m_ln_g_d, m_ln_b_d, m_w_s_d, m_b_s_d, m_w_out, m_final_g, v_norm_g, v_w_ada, v_b_ada, v_w_in, v_w_conv_a, v_w_pool, v_pool_scale, v_w_dw_c, v_b_dw_c, v_ln_g_c, v_ln_b_c, v_w_pw2_c, v_b_pw2_c, v_ln_g_d, v_ln_b_d, v_w_s_d, v_b_s_d, v_w_out, v_final_g):
    given = dict(locals())
    shard = {n: given[n] for n in WEIGHTS}
    mom_m = {n: given['m_' + n] for n in WEIGHTS}
    mom_v = {n: given['v_' + n] for n in WEIGHTS}
    me = 4 * lax.axis_index("x") + 2 * lax.axis_index("y") + lax.axis_index("c")
    n_tok = x.shape[1]
    tile = min(TOKEN_TILE, n_tok)
    x0 = x.reshape(n_tok, D_MODEL)
    target = loss_target.reshape(n_tok, D_MODEL)
    ada_cols = w_ada.shape[2]

    first_shards = _weight_shards(shard, 0)
    c_all, w_in_first = _first_gather(c, first_shards[0])

    b_cols = lax.dynamic_slice_in_dim(b_ada, me * ada_cols, ada_cols, axis=1)
    c_act, mod_cols = _modulation_columns(c_all.reshape(N_DEV, D_MODEL), w_ada, b_cols)
    (mod_all,) = _exchange("gather_modulation", _plans([mod_cols], [_gather]))
    mod = lax.dynamic_index_in_dim(mod_all, me, axis=2, keepdims=False)
    mod = mod.transpose(1, 0, 2).reshape(N_LAYERS, 3 * D_MODEL)
    shift, scale, gate = (mod[:, k * D_MODEL:(k + 1) * D_MODEL].reshape(N_LAYERS, 1, D_MODEL) for k in range(3))
    gs = norm_g.reshape(N_LAYERS, 1, D_MODEL) * (1.0 + scale)

    xs, hs, zs, ocs, layers = [x0], [], [], [], []
    for l in range(N_LAYERS):
        if l == 0:
            (h, z), rest = _in_proj(xs[0], gs[0], shift[0], w_in_first, tile,
                                    ride=_plans(first_shards[1:], GATHER_RULES[1:]))
            layers.append(_layer_weights(shard, 0, [w_in_first, *rest]))
        else:
            (h, z), _ = _in_proj(xs[l], gs[l], shift[l], layers[l][0], tile)
        _, w_out_b, small, _ = layers[l]
        hs.append(h)
        zs.append(z)
        if l + 1 < N_LAYERS:
            (x_next, o_c), gathered = _mix_out(z, xs[l], gate[l], small, w_out_b, tile,
                                               ride=_plans(_weight_shards(shard, l + 1), GATHER_RULES))
            xs.append(x_next)
            layers.append(_layer_weights(shard, l + 1, gathered))
        else:
            (dx, o_c, loss_part, dfinal_g), _ = _mix_out(z, xs[l], gate[l], small, w_out_b, tile,
                                                         head=(final_g.reshape(1, D_MODEL), target))
        ocs.append(o_c)
    loss = lax.psum(loss_part[0, 0], ("x", "y", "c"))

    part = {}
    layer_parts = [None] * N_LAYERS
    slots = [None] * N_LAYERS
    for l in reversed(range(N_LAYERS)):
        w_in_b, w_out_b, small, small_t = layers[l]
        ride = _plans(layer_parts[l + 1]['big'], SCATTER_RULES) if l + 1 < N_LAYERS else None
        (dz, ycat, sums, dwp, dw2, dws, dbs), rode = _mix_bwd(zs[l], ocs[l], dx, gate[l], small, small_t, w_out_b, tile,
                                                              ride=ride)
        if ride:
            slots[l + 1] = rode
        dw_in, _ = _tokens_matmul(hs[l], dz, "in_proj_tokens_matmul", out_dtype=BF16)
        ride = _plans([dw_in], SCATTER_RULES[:1]) if l == 0 else None
        m_out, rode = _tokens_matmul(ycat, dx, "out_proj_tokens_matmul", ride=ride)
        if ride:
            slots[l] = list(rode)
        dw_out, dgate = _out_proj_grads(m_out, w_out_b, gate[l])
        (dx, dshift, dgs), _ = _norm_bwd(xs[l], dz, dx, gs[l], w_in_b, tile)
        layer_parts[l] = dict(
            big=[dw_in, dw_out, dw2],
            b_ada=jnp.concatenate([dshift, dgs * norm_g[l][None], dgate], axis=1)[0],
            norm_g=(dgs * (1.0 + scale[l]))[0],
            w_conv_a=sums[S_WCONV:S_WCONV + CONV_A], w_dw_c=sums[S_WDW:S_WDW + CONV_C],
            pool_scale=sums[S_PSCALE], b_dw_c=sums[S_BDW], ln_g_c=sums[S_LNGC], ln_b_c=sums[S_LNBC],
            b_pw2_c=sums[S_BPW2], ln_g_d=sums[S_LNGD], ln_b_d=sums[S_LNBD],
            w_pool=jnp.einsum('gchd,gh->gcd', dwp.reshape(4, 64, 4, 64), jnp.eye(4, dtype=F32)),
            w_s_d=dws.reshape(4, SUB, SUB) * jnp.tril(jnp.ones((SUB, SUB), F32)),
            b_s_d=dbs.reshape(SUB, 4, 64).sum(axis=-1).T)
    grad_x = dx.reshape(x.shape)
    for n in REPLICATED + CHANNEL_SHARDED:
        part[n] = dfinal_g[0] if n == 'final_g' else jnp.stack([layer_parts[l][n] for l in range(N_LAYERS)])

    small_names = REPLICATED + CHANNEL_SHARDED
    small_shapes = [part[n].shape for n in small_names]
    slots_out, slots_pw2, small_sum, dmod_all = _finish_exchange(
        layer_parts[0]['big'][1:], SCATTER_RULES[1:], _pack([part[n] for n in small_names], rows_multiple=8 * N_DEV),
        part['b_ada'])
    slots[0] += [slots_out, slots_pw2]

    grads, deltas, new_m, new_v = {}, {}, {}, {}
    for j, n in enumerate(('w_in', 'w_out', 'w_pw2_c')):
        outs = [_adam_update(shard[n][l], slots[l][j], mom_m[n][l], mom_v[n][l], "update_" + n) for l in range(N_LAYERS)]
        grads[n], deltas[n], new_m[n], new_v[n] = (jnp.stack(o) for o in zip(*outs))

    gsum = dict(zip(small_names, _unpack(small_sum, small_shapes)))
    for n in CHANNEL_SHARDED:
        width = shard[n].shape[2]
        gsum[n] = lax.dynamic_slice_in_dim(gsum[n], me * width, width, axis=2)
    d_small, m_small, v_small = _adam_many(*[[_as_rows(d[n]) for n in small_names] for d in (shard, gsum, mom_m, mom_v)],
                                           "update_small")
    for j, n in enumerate(small_names):
        grads[n] = gsum[n]
        deltas[n], new_m[n], new_v[n] = (o[j].reshape(shard[n].shape) for o in (d_small, m_small, v_small))

    dmod_cols = lax.dynamic_slice_in_dim(dmod_all, me * ada_cols, ada_cols, axis=2).transpose(1, 0, 2)
    grads['w_ada'], deltas['w_ada'], new_m['w_ada'], new_v['w_ada'] = _ada_update(
        c_act.T, dmod_cols, w_ada, m_w_ada, v_w_ada)

    return (loss, grad_x, *[grads[n] for n in WEIGHTS], *[deltas[n] for n in WEIGHTS],
            *[new_m[n] for n in WEIGHTS], *[new_v[n] for n in WEIGHTS])
```

```python
import functools
import math

import jax
import jax.numpy as jnp
from jax import lax
from jax.experimental import pallas as pl
from jax.experimental.pallas import tpu as pltpu

F32 = jnp.float32
BF16 = jnp.bfloat16

N_DEV = 8
D_MODEL = 1024
GROUP = 256
D_IN = 12 * GROUP
N_LAYERS = 2
HALO = 32
SUB = 128
WIN = SUB + HALO
TOKEN_TILE = 512
REDUCE_TILE = 2048
EPS = 1e-6
VMEM_BYTES_V7X = 64 * 1024 * 1024
VMEM_LIMIT = VMEM_BYTES_V7X - 8 * 1024 * 1024

ADAM_LR = 0.001
ADAM_B1 = 0.9
ADAM_B2 = 0.999
ADAM_EPS = 1e-08
ADAM_WD = 0.01
ADAM_STEP = 10

A_B, A_C, A_X, A_G, B_P, B_G, C_A, C_GL, C_G, D_U, D_V, D_G = range(12)
V_PSCALE, V_BDW, V_LNGC, V_LNBC, V_BPW2, V_LNGD, V_LNBD = range(7)
S_WCONV, S_PSCALE, S_BDW, S_LNGC, S_LNBC, S_BPW2, S_LNGD, S_LNBD, S_WDW = 0, 3, 4, 5, 6, 7, 8, 9, 16
N_SUMS = 64
CONV_A = 3
CONV_C = 31

WEIGHTS = ('norm_g', 'w_ada', 'b_ada', 'w_in', 'w_conv_a', 'w_pool', 'pool_scale', 'w_dw_c', 'b_dw_c', 'ln_g_c',
           'ln_b_c', 'w_pw2_c', 'b_pw2_c', 'ln_g_d', 'ln_b_d', 'w_s_d', 'b_s_d', 'w_out', 'final_g')
REPLICATED = ('norm_g', 'b_ada', 'w_pool', 'pool_scale', 'b_dw_c', 'ln_g_c', 'ln_b_c', 'b_pw2_c', 'ln_g_d', 'ln_b_d',
              'w_s_d', 'b_s_d', 'final_g')
CHANNEL_SHARDED = ('w_conv_a', 'w_dw_c')


def _params(semantics, vmem=VMEM_LIMIT):
    return pltpu.CompilerParams(dimension_semantics=semantics, vmem_limit_bytes=vmem)


def _cols(g):
    return slice(g * GROUP, (g + 1) * GROUP)


def _full(shape):
    return pl.BlockSpec(shape, lambda *_: (0,) * len(shape))


def _silu(x):
    s = jax.nn.sigmoid(x)
    return x * s, s


def _dsilu(x, s):
    return s * (1.0 + x * (1.0 - s))


_GELU_C0 = math.sqrt(2.0 / math.pi)
_GELU_C1 = 0.044715


def _gelu(x):
    th = jnp.tanh(_GELU_C0 * (x + _GELU_C1 * (x * x * x)))
    return 0.5 * x * (1.0 + th), th


def _dgelu(x, th):
    return 0.5 * (1.0 + th) + 0.5 * x * (1.0 - th * th) * (_GELU_C0 * (1.0 + 3.0 * _GELU_C1 * (x * x)))


def _layer_norm(x):
    mu = jnp.mean(x, axis=-1, keepdims=True)
    xc = x - mu
    rstd = lax.rsqrt(jnp.mean(xc * xc, axis=-1, keepdims=True) + EPS)
    return xc * rstd, rstd


def _layer_norm_bwd(dn, n, rstd):
    return rstd * (dn - jnp.mean(dn, axis=-1, keepdims=True) - n * jnp.mean(dn * n, axis=-1, keepdims=True))


def _shift_rows(a, k):
    k = k % a.shape[0]
    return a if k == 0 else pltpu.roll(a, k, 0)


def _row_sum8(a):
    s = a[0:8]
    for m in range(1, a.shape[0] // 8):
        s = s + a[8 * m:8 * m + 8]
    return s


def _lane():
    return lax.broadcasted_iota(jnp.int32, (SUB, GROUP), 1)


def _by_quarter(lane, parts):
    return jnp.where(lane < 64, parts[0], jnp.where(lane < 128, parts[1], jnp.where(lane < 192, parts[2], parts[3])))


def _conv_inputs(z_ref, rows):
    def f(g):
        return z_ref[rows, _cols(g)].astype(F32)
    return f(A_C) * f(A_X), f(B_P), f(C_A) * jax.nn.sigmoid(f(C_GL))


def _fill_past(past_ref, zh_ref, zm_ref, is_first, tile):
    parts = _conv_inputs(zh_ref, slice(None))
    for n, a in enumerate(parts):
        past_ref[0:HALO, _cols(n)] = jnp.where(is_first, 0.0, a)

    def body(j, carry):
        r0 = pl.multiple_of(j * SUB, SUB)
        for n, a in enumerate(_conv_inputs(zm_ref, pl.ds(r0, SUB))):
            past_ref[pl.ds(r0 + HALO, SUB), _cols(n)] = a
        return carry

    lax.fori_loop(0, tile // SUB, body, 0)


def _short_conv_taps(qw):
    return [_shift_rows(qw, CONV_A - 1 - k)[HALO:WIN] for k in range(CONV_A)]


def _window_sums(pw, lane):
    s2 = pw + _shift_rows(pw, 1)
    s4 = s2 + _shift_rows(s2, 2)
    s8 = s4 + _shift_rows(s4, 4)
    s16 = s8 + _shift_rows(s8, 8)
    return _by_quarter(lane, [s[HALO:WIN] for s in (s2, s4, s8, s16)])


def _inv_count(lane, t_first):
    width = _by_quarter(lane, [2.0, 4.0, 8.0, 16.0])
    t = lax.broadcasted_iota(jnp.int32, (SUB, GROUP), 0) + t_first
    return 1.0 / jnp.minimum((t + 1).astype(F32), width)


def _forward_window_sums(ew, lane):
    n = ew.shape[0]
    f2 = ew + _shift_rows(ew, n - 1)
    f4 = f2 + _shift_rows(f2, n - 2)
    f8 = f4 + _shift_rows(f4, n - 4)
    f16 = f8 + _shift_rows(f8, n - 8)
    return _by_quarter(lane, [f[0:SUB] for f in (f2, f4, f8, f16)])


def _mixers_forward(zc, win, t_first, wc_ref, wdw_ref, vec_ref, wp_ref, w2_ref, ws_ref, bs_ref, o_c=None):
    lane = _lane()

    def vec(n):
        return vec_ref[n:n + 1, :]

    taps = _short_conv_taps(win(0))
    o_a = wc_ref[0:1, :] * taps[0] + wc_ref[1:2, :] * taps[1] + wc_ref[2:3, :] * taps[2]
    a_b, a_g = zc(A_B), zc(A_G)
    sg_a, s_a = _silu(a_g)
    y_a = a_b * o_a * sg_a

    pw = win(1)
    ic = _inv_count(lane, t_first)
    pooled = _window_sums(pw, lane) * ic - pw[HALO:WIN]
    pooled_b = pooled.astype(BF16)
    y0_b = jnp.dot(pooled_b, wp_ref[...], preferred_element_type=F32)
    b_g = zc(B_G)
    sg_b, s_b = _silu(b_g)
    y_b = y0_b * vec(V_PSCALE) * sg_b

    hw = win(2)
    if o_c is None:
        o_c = wdw_ref[CONV_C - 1:CONV_C, :] * hw[HALO:WIN] + vec(V_BDW)
        for k in range(CONV_C - 1):
            o_c = o_c + wdw_ref[k:k + 1, :] * _shift_rows(hw, CONV_C - 1 - k)[HALO:WIN]
    n_c, rstd_c = _layer_norm(o_c)
    ln_c = n_c * vec(V_LNGC) + vec(V_LNBC)
    sl_c, ssl_c = _silu(ln_c)
    sl_b = sl_c.astype(BF16)
    yc = jnp.dot(sl_b, w2_ref[...], preferred_element_type=F32) + vec(V_BPW2)
    c_g = zc(C_G)
    sg_c, s_c = _silu(c_g)
    y_c = yc * sg_c

    d_u, d_v, d_g = zc(D_U), zc(D_V), zc(D_G)
    u, th_u = _gelu(d_u)
    gv, th_v = _gelu(d_v)
    n_d, rstd_d = _layer_norm(gv)
    v_b = (n_d * vec(V_LNGD) + vec(V_LNBD)).astype(BF16)
    r = jnp.dot(ws_ref[...], v_b, preferred_element_type=F32)
    mixed = _by_quarter(lane, [r[h * SUB:(h + 1) * SUB] for h in range(4)]) + bs_ref[...]
    sg_d, s_d = _silu(d_g)
    y_d = u * mixed * sg_d

    saved = dict(lane=lane, taps=taps, o_a=o_a, a_b=a_b, a_g=a_g, sg_a=sg_a, s_a=s_a,
                 ic=ic, pooled_b=pooled_b, y0_b=y0_b, b_g=b_g, sg_b=sg_b, s_b=s_b,
                 hw=hw, o_c=o_c, n_c=n_c, rstd_c=rstd_c, ln_c=ln_c, ssl_c=ssl_c, sl_b=sl_b, yc=yc, c_g=c_g, sg_c=sg_c, s_c=s_c,
                 d_u=d_u, d_v=d_v, d_g=d_g, u=u, th_u=th_u, th_v=th_v, n_d=n_d, rstd_d=rstd_d, v_b=v_b, mixed=mixed,
                 sg_d=sg_d, s_d=s_d)
    return (y_a, y_b, y_c, y_d), saved


def _in_proj(x, gs, shift, w_in_b, tile, ride=None):
    n_tok = x.shape[0]

    def body(x_ref, gs_ref, sh_ref, w_ref, h_ref, z_ref):
        xv = x_ref[...]
        r = lax.rsqrt(jnp.mean(xv * xv, axis=-1, keepdims=True) + EPS)
        h = ((xv * r) * gs_ref[...] + sh_ref[...]).astype(BF16)
        h_ref[...] = h
        for j in range(D_IN // D_MODEL):
            cs = slice(j * D_MODEL, (j + 1) * D_MODEL)
            z_ref[:, cs] = jnp.dot(h, w_ref[:, cs], preferred_element_type=F32).astype(BF16)

    return _tiled_call(
        body, (x, gs, shift, w_in_b), name="in_proj", grid=(n_tok // tile,),
        in_specs=[pl.BlockSpec((tile, D_MODEL), lambda i: (i, 0)), _full((1, D_MODEL)), _full((1, D_MODEL)),
                  _full((D_MODEL, D_IN))],
        out_specs=[pl.BlockSpec((tile, D_MODEL), lambda i: (i, 0)), pl.BlockSpec((tile, D_IN), lambda i: (i, 0))],
        out_shape=[jax.ShapeDtypeStruct((n_tok, D_MODEL), BF16), jax.ShapeDtypeStruct((n_tok, D_IN), BF16)],
        ride=ride)


def _small_specs(with_transposes):
    specs = [_full((8, GROUP)), _full((HALO, GROUP)), _full((16, GROUP)), _full((GROUP, GROUP)), _full((GROUP, GROUP)),
             _full((4 * SUB, SUB)), _full((SUB, GROUP))]
    if with_transposes:
        specs += [_full((GROUP, GROUP)), _full((GROUP, GROUP)), _full((4 * SUB, SUB))]
    return specs


def _mix_out(z, x, gate, small, w_out_b, tile, ride=None, head=None):
    n_tok = x.shape[0]
    n_tiles = n_tok // tile
    n_sub = tile // SUB
    cw = D_MODEL // n_sub
    per_halo = tile // HALO
    n_in = 12 + (2 if head else 0)
    n_out = 4 if head else 2

    def cur(i):
        return jnp.minimum(i, n_tiles - 1)

    def prev(i):
        return jnp.maximum(i - 1, 0)

    def body(*refs):
        (zm_ref, zh_ref, x_ref, gate_ref, wc_ref, wdw_ref, vec_ref, wp_ref, w2_ref, ws_ref, bs_ref, wout_ref) = refs[:12]
        xo_ref, oc_ref = refs[n_in:n_in + 2]
        past_ref, ycat_ref, ycat_prev_ref = refs[n_in + n_out:n_in + n_out + 3]
        i = pl.program_id(0)
        t = cur(i)
        if head:
            g_ref, tgt_ref = refs[12:14]
            loss_ref, dg_ref = refs[n_in + 2:n_in + 4]
            xn_ref, acc_ref = refs[n_in + n_out + 3:]
        else:
            xn_ref = xo_ref

        @pl.when(i == 0)
        def _():
            ycat_prev_ref[...] = jnp.zeros_like(ycat_prev_ref)
            if head:
                acc_ref[...] = jnp.zeros_like(acc_ref)

        _fill_past(past_ref, zh_ref, zm_ref, t == 0, tile)
        for j in range(n_sub):
            cs = slice(j * cw, (j + 1) * cw)
            y = jnp.dot(ycat_prev_ref[...], wout_ref[:, cs], preferred_element_type=F32)
            xn_ref[:, cs] = x_ref[:, cs] + gate_ref[:, cs] * y
            rows = slice(j * SUB, (j + 1) * SUB)
            ys, s = _mixers_forward(
                lambda g: zm_ref[rows, _cols(g)].astype(F32), lambda n: past_ref[j * SUB:j * SUB + WIN, _cols(n)],
                t * tile + j * SUB, wc_ref, wdw_ref, vec_ref, wp_ref, w2_ref, ws_ref, bs_ref)
            for n, y in enumerate(ys):
                ycat_ref[rows, _cols(n)] = y.astype(BF16)
            oc_ref[rows, :] = s["o_c"]
        ycat_prev_ref[...] = ycat_ref[...]
        if head:
            counted = jnp.where(i > 0, 1.0, 0.0)
            xo_ref[...] = _loss_head_block(xn_ref[...], g_ref[...], tgt_ref[...], acc_ref, counted)

            @pl.when(i == n_tiles)
            def _():
                loss_ref[...] = jnp.full((8, 128), 0.5 / D_MODEL, F32) * jnp.sum(acc_ref[0])
                dg_ref[...] = jnp.sum(acc_ref[1], axis=0, keepdims=True)

    in_specs = [pl.BlockSpec((tile, D_IN), lambda i: (cur(i), 0)),
                pl.BlockSpec((HALO, D_IN), lambda i: (jnp.maximum(cur(i) * per_halo - 1, 0), 0)),
                pl.BlockSpec((tile, D_MODEL), lambda i: (prev(i), 0)), _full((1, D_MODEL)),
                *_small_specs(False), _full((D_MODEL, D_MODEL))]
    out_specs = [pl.BlockSpec((tile, D_MODEL), lambda i: (prev(i), 0)), pl.BlockSpec((tile, GROUP), lambda i: (cur(i), 0))]
    out_shape = [jax.ShapeDtypeStruct((n_tok, D_MODEL), F32), jax.ShapeDtypeStruct((n_tok, GROUP), F32)]
    scratch = [pltpu.VMEM((tile + HALO, 3 * GROUP), F32), pltpu.VMEM((tile, D_MODEL), BF16), pltpu.VMEM((tile, D_MODEL), BF16)]
    args = (z, z, x, gate, *small, w_out_b)
    if head:
        in_specs += [_full((1, D_MODEL)), pl.BlockSpec((tile, D_MODEL), lambda i: (prev(i), 0))]
        out_specs += [_full((8, 128)), _full((1, D_MODEL))]
        out_shape += [jax.ShapeDtypeStruct((8, 128), F32), jax.ShapeDtypeStruct((1, D_MODEL), F32)]
        scratch += [pltpu.VMEM((tile, D_MODEL), F32), pltpu.VMEM((2, 8, D_MODEL), F32)]
        args += tuple(head)
    outs, rode = _tiled_call(body, args, name="mix_out", grid=(n_tiles + 1,), in_specs=in_specs, out_specs=out_specs,
                             out_shape=out_shape, scratch_shapes=scratch, ride=ride)
    return outs, rode


def _loss_head_block(xv, g, target, acc_ref, counted):
    r = lax.rsqrt(jnp.mean(xv * xv, axis=-1, keepdims=True) + EPS)
    xn = xv * r
    err = xn * g - target
    acc_ref[0] = acc_ref[0] + counted * _row_sum8(err * err)
    dy = err * (1.0 / D_MODEL)
    acc_ref[1] = acc_ref[1] + counted * _row_sum8(dy * xn)
    a = dy * g
    return r * (a - xn * jnp.mean(a * xn, axis=-1, keepdims=True))


def _mix_bwd(z, o_c, dx_next, gate, small, small_t, w_out_b, tile, ride=None):
    n_tok = z.shape[0]
    n_tiles = n_tok // tile
    n_sub = tile // SUB
    cw = D_MODEL // n_sub
    per_halo = tile // HALO
    nt_dims = (((1,), (1,)), ((), ()))

    def tile_of(i):
        return n_tiles - 1 - i

    def next_tile_of(i):
        return jnp.maximum(n_tiles - 2 - i, 0)

    def body(zm_ref, zh_ref, oc_ref, dxn_ref, dxn_next_ref, gate_ref, wc_ref, wdw_ref, vec_ref, wp_ref, w2_ref, ws_ref,
             bs_ref, wpt_ref, w2t_ref, wst_ref, wout_ref,
             dz_ref, ycat_ref, sums_ref, dwp_ref, dw2_ref, dws_ref, dbs_ref,
             past_ref, future_ref, dy_ref, dy_next_ref, acc_ref):
        i = pl.program_id(0)
        t = tile_of(i)

        @pl.when(i == 0)
        def _():
            acc_ref[...] = jnp.zeros_like(acc_ref)
            dwp_ref[...] = jnp.zeros_like(dwp_ref)
            dw2_ref[...] = jnp.zeros_like(dw2_ref)
            dws_ref[...] = jnp.zeros_like(dws_ref)
            dbs_ref[...] = jnp.zeros_like(dbs_ref)
            future_ref[tile:tile + HALO, :] = jnp.zeros((HALO, 3 * GROUP), F32)
            dy_ref[...] = lax.dot_general((dxn_ref[...] * gate_ref[...]).astype(BF16), wout_ref[...], nt_dims,
                                        preferred_element_type=F32)

        _fill_past(past_ref, zh_ref, zm_ref, t == 0, tile)
        dyb_next = (dxn_next_ref[...] * gate_ref[...]).astype(BF16)

        def vec(n):
            return vec_ref[n:n + 1, :]

        for jj in range(n_sub):
            j = n_sub - 1 - jj
            r0 = j * SUB
            rows = slice(r0, r0 + SUB)

            def zc(g):
                return zm_ref[rows, _cols(g)].astype(F32)

            def add(n, a):
                acc_ref[n] = acc_ref[n] + _row_sum8(a)

            def put(g, a):
                dz_ref[rows, _cols(g)] = a.astype(BF16)

            def future_window(n, a):
                future_ref[rows, _cols(n)] = a
                return future_ref[r0:r0 + WIN, _cols(n)]

            ys, s = _mixers_forward(zc, lambda n: past_ref[r0:r0 + WIN, _cols(n)], t * tile + r0,
                                    wc_ref, wdw_ref, vec_ref, wp_ref, w2_ref, ws_ref, bs_ref, o_c=oc_ref[rows, :])
            for n, y in enumerate(ys):
                ycat_ref[rows, _cols(n)] = y.astype(BF16)
            lane = s["lane"]
            ks = slice(jj * cw, (jj + 1) * cw)
            dy_next_ref[:, ks] = lax.dot_general(dyb_next, wout_ref[ks, :], nt_dims, preferred_element_type=F32)

            dy = dy_ref[rows,_cols(0)]
            put(A_B, dy * s["o_a"] * s["sg_a"])
            put(A_G, dy * s["a_b"] * s["o_a"] * _dsilu(s["a_g"], s["s_a"]))
            do = dy * s["a_b"] * s["sg_a"]
            for k in range(CONV_A):
                add(S_WCONV + k, do * s["taps"][k])
            dow = future_window(0, do)
            dq = wc_ref[CONV_A - 1:CONV_A, :] * dow[0:SUB]
            for k in range(CONV_A - 1):
                dq = dq + wc_ref[k:k + 1, :] * _shift_rows(dow, WIN - (CONV_A - 1 - k))[0:SUB]
            put(A_C, dq * zc(A_X))
            put(A_X, dq * zc(A_C))

            dy = dy_ref[rows,_cols(1)]
            put(B_G, dy * (s["y0_b"] * vec(V_PSCALE)) * _dsilu(s["b_g"], s["s_b"]))
            dyb = dy * s["sg_b"]
            add(S_PSCALE, dyb * s["y0_b"])
            dpw_b = (dyb * vec(V_PSCALE)).astype(BF16)
            dwp_ref[...] += lax.dot_general(s["pooled_b"], dpw_b, (((0,), (0,)), ((), ())), preferred_element_type=F32)
            dpooled = jnp.dot(dpw_b, wpt_ref[...], preferred_element_type=F32)
            ew = future_window(1, dpooled * s["ic"])
            put(B_P, _forward_window_sums(ew, lane) - dpooled)

            dy = dy_ref[rows,_cols(2)]
            put(C_G, dy * s["yc"] * _dsilu(s["c_g"], s["s_c"]))
            dyc = dy * s["sg_c"]
            add(S_BPW2, dyc)
            dyc_b = dyc.astype(BF16)
            dw2_ref[...] += lax.dot_general(s["sl_b"], dyc_b, (((0,), (0,)), ((), ())), preferred_element_type=F32)
            dln = jnp.dot(dyc_b, w2t_ref[...], preferred_element_type=F32) * _dsilu(s["ln_c"], s["ssl_c"])
            add(S_LNGC, dln * s["n_c"])
            add(S_LNBC, dln)
            do = _layer_norm_bwd(dln * vec(V_LNGC), s["n_c"], s["rstd_c"])
            add(S_BDW, do)
            hw = s["hw"]
            for k in range(CONV_C):
                add(S_WDW + k, do * _shift_rows(hw, CONV_C - 1 - k)[HALO:WIN])
            dow = future_window(2, do)
            dhc = wdw_ref[CONV_C - 1:CONV_C, :] * dow[0:SUB]
            for k in range(CONV_C - 1):
                dhc = dhc + wdw_ref[k:k + 1, :] * _shift_rows(dow, WIN - (CONV_C - 1 - k))[0:SUB]
            c_a = zc(C_A)
            sgl = jax.nn.sigmoid(zc(C_GL))
            put(C_A, dhc * sgl)
            put(C_GL, dhc * c_a * sgl * (1.0 - sgl))

            dy = dy_ref[rows,_cols(3)]
            put(D_G, dy * s["u"] * s["mixed"] * _dsilu(s["d_g"], s["s_d"]))
            put(D_U, dy * s["mixed"] * s["sg_d"] * _dgelu(s["d_u"], s["th_u"]))
            dmixed = dy * s["u"] * s["sg_d"]
            dbs_ref[...] += dmixed
            by_head = jnp.concatenate(
                [jnp.where((lane >= 64 * h) & (lane < 64 * h + 64), dmixed, 0.0) for h in range(4)], axis=0).astype(BF16)
            dws_ref[...] += lax.dot_general(by_head, s["v_b"], (((1,), (1,)), ((), ())), preferred_element_type=F32)
            rv = jnp.dot(wst_ref[...], dmixed.astype(BF16), preferred_element_type=F32)
            dv = _by_quarter(lane, [rv[h * SUB:(h + 1) * SUB] for h in range(4)])
            add(S_LNGD, dv * s["n_d"])
            add(S_LNBD, dv)
            dgv = _layer_norm_bwd(dv * vec(V_LNGD), s["n_d"], s["rstd_d"])
            put(D_V, dgv * _dgelu(s["d_v"], s["th_v"]))

        future_ref[tile:tile + HALO, :] = future_ref[0:HALO, :]
        dy_ref[...] = dy_next_ref[...]

        @pl.when(i == n_tiles - 1)
        def _():
            for n in range(N_SUMS):
                sums_ref[n:n + 1, :] = jnp.sum(acc_ref[n], axis=0, keepdims=True)

    return _tiled_call(
        body, (z, z, o_c, dx_next, dx_next, gate, *small, *small_t, w_out_b), name="mix_bwd", grid=(n_tiles,),
        in_specs=[pl.BlockSpec((tile, D_IN), lambda i: (tile_of(i), 0)),
                  pl.BlockSpec((HALO, D_IN), lambda i: (jnp.maximum(tile_of(i) * per_halo - 1, 0), 0)),
                  pl.BlockSpec((tile, GROUP), lambda i: (tile_of(i), 0)),
                  pl.BlockSpec((tile, D_MODEL), lambda i: (tile_of(i), 0)),
                  pl.BlockSpec((tile, D_MODEL), lambda i: (next_tile_of(i), 0)), _full((1, D_MODEL)),
                  *_small_specs(True), _full((D_MODEL, D_MODEL))],
        out_specs=[pl.BlockSpec((tile, D_IN), lambda i: (tile_of(i), 0)),
                   pl.BlockSpec((tile, D_MODEL), lambda i: (tile_of(i), 0)),
                   _full((N_SUMS, GROUP)), _full((GROUP, GROUP)), _full((GROUP, GROUP)), _full((4 * SUB, SUB)),
                   _full((SUB, GROUP))],
        out_shape=[jax.ShapeDtypeStruct((n_tok, D_IN), BF16), jax.ShapeDtypeStruct((n_tok, D_MODEL), BF16),
                   jax.ShapeDtypeStruct((N_SUMS, GROUP), F32), jax.ShapeDtypeStruct((GROUP, GROUP), F32),
                   jax.ShapeDtypeStruct((GROUP, GROUP), F32), jax.ShapeDtypeStruct((4 * SUB, SUB), F32),
                   jax.ShapeDtypeStruct((SUB, GROUP), F32)],
        scratch_shapes=[pltpu.VMEM((tile + HALO, 3 * GROUP), F32), pltpu.VMEM((tile + HALO, 3 * GROUP), F32),
                        pltpu.VMEM((tile, D_MODEL), F32), pltpu.VMEM((tile, D_MODEL), F32),
                        pltpu.VMEM((N_SUMS, 8, GROUP), F32)], ride=ride)


def _norm_bwd(x, dz, dx_next, gs, w_in_b, tile, ride=None, blocks=None, begun=None, finish=True):
    n_tok = x.shape[0]
    first, n_tiles = blocks or (0, n_tok // tile)
    n_in = 5 + (2 if begun else 0)

    def body(*refs):
        x_ref, dz_ref, dxn_ref, gs_ref, w_ref = refs[:5]
        dx_ref = refs[n_in]
        acc_ref = refs[-1]
        i = pl.program_id(0)

        @pl.when(i == 0)
        def _():
            acc_ref[...] = refs[6][...] if begun else jnp.zeros_like(acc_ref)

        dh = lax.dot_general(dz_ref[...], w_ref[...], (((1,), (1,)), ((), ())), preferred_element_type=F32)
        xv = x_ref[...]
        r = lax.rsqrt(jnp.mean(xv * xv, axis=-1, keepdims=True) + EPS)
        xn = xv * r
        acc_ref[0] = acc_ref[0] + _row_sum8(dh)
        acc_ref[1] = acc_ref[1] + _row_sum8(dh * xn)
        dxn = dh * gs_ref[...]
        dx_ref[...] = dxn_ref[...] + r * (dxn - xn * jnp.mean(dxn * xn, axis=-1, keepdims=True))

        @pl.when(i == n_tiles - 1)
        def _():
            if finish:
                refs[n_in + 1][...] = jnp.sum(acc_ref[0], axis=0, keepdims=True)
                refs[n_in + 2][...] = jnp.sum(acc_ref[1], axis=0, keepdims=True)
            else:
                refs[n_in + 1][...] = acc_ref[...]

    def rows(i):
        return (first + i, 0)

    in_specs = [pl.BlockSpec((tile, D_MODEL), rows), pl.BlockSpec((tile, D_IN), rows), pl.BlockSpec((tile, D_MODEL), rows),
                _full((1, D_MODEL)), _full((D_MODEL, D_IN))]
    args = (x, dz, dx_next, gs, w_in_b)
    if begun:
        in_specs += [pl.BlockSpec(memory_space=pl.ANY), _full((2, 8, D_MODEL))]
        args += tuple(begun)
    vec = jax.ShapeDtypeStruct((1, D_MODEL), F32)
    return _tiled_call(
        body, args, name="norm_bwd", grid=(n_tiles,), in_specs=in_specs,
        out_specs=[pl.BlockSpec((tile, D_MODEL), rows)] + ([_full((1, D_MODEL))] * 2 if finish else [_full((2, 8, D_MODEL))]),
        out_shape=[jax.ShapeDtypeStruct((n_tok, D_MODEL), F32)]
        + ([vec, vec] if finish else [jax.ShapeDtypeStruct((2, 8, D_MODEL), F32)]),
        scratch_shapes=[pltpu.VMEM((2, 8, D_MODEL), F32)], ride=ride, aliases={5: 0} if begun else None)


def _tokens_matmul(a, b, name, out_dtype=F32, ride=None):
    n_tok, ka = a.shape
    nb = b.shape[1]
    tk = min(REDUCE_TILE, n_tok)
    cb = min(D_MODEL, nb)
    n_steps = n_tok // tk

    def body(a_ref, b_ref, o_ref, acc_ref):
        i = pl.program_id(1)

        @pl.when(i == 0)
        def _():
            acc_ref[...] = jnp.zeros_like(acc_ref)

        acc_ref[...] += lax.dot_general(a_ref[...], b_ref[...].astype(BF16), (((0,), (0,)), ((), ())),
                                        preferred_element_type=F32)

        @pl.when(i == n_steps - 1)
        def _():
            o_ref[...] = acc_ref[...].astype(out_dtype)

    (out,), rode = _tiled_call(
        body, (a, b), name=name, grid=(nb // cb, n_steps),
        in_specs=[pl.BlockSpec((tk, ka), lambda j, i: (i, 0)), pl.BlockSpec((tk, cb), lambda j, i: (i, j))],
        out_specs=[pl.BlockSpec((ka, cb), lambda j, i: (0, j))],
        out_shape=[jax.ShapeDtypeStruct((ka, nb), out_dtype)],
        scratch_shapes=[pltpu.VMEM((ka, cb), F32)], ride=ride)
    return out, rode


def _out_proj_grads(m, w_out_b, gate):
    rb = 256
    n_blocks = D_MODEL // rb

    def body(m_ref, w_ref, gate_ref, dw_ref, dgate_ref, acc_ref):
        i = pl.program_id(0)

        @pl.when(i == 0)
        def _():
            acc_ref[...] = jnp.zeros_like(acc_ref)

        mv = m_ref[...]
        dw_ref[...] = (mv * gate_ref[...]).astype(BF16)
        acc_ref[...] += _row_sum8(mv * w_ref[...].astype(F32))

        @pl.when(i == n_blocks - 1)
        def _():
            dgate_ref[...] = jnp.sum(acc_ref[...], axis=0, keepdims=True)

    return pl.pallas_call(
        body, name="out_proj_grads", grid=(n_blocks,),
        in_specs=[pl.BlockSpec((rb, D_MODEL), lambda i: (i, 0)), pl.BlockSpec((rb, D_MODEL), lambda i: (i, 0)),
                  _full((1, D_MODEL))],
        out_specs=[pl.BlockSpec((rb, D_MODEL), lambda i: (i, 0)), _full((1, D_MODEL))],
        out_shape=[jax.ShapeDtypeStruct((D_MODEL, D_MODEL), BF16), jax.ShapeDtypeStruct((1, D_MODEL), F32)],
        scratch_shapes=[pltpu.VMEM((8, D_MODEL), F32)],
        compiler_params=_params(("arbitrary",)),
    )(m, w_out_b, gate)


def _modulation_columns(c_all, w_ada, b_cols):
    cols = w_ada.shape[2]

    def body(c_ref, w_ref, b_ref, ca_ref, mod_ref):
        ca, _ = _silu(c_ref[...])
        ca_ref[...] = ca
        for l in range(N_LAYERS):
            mod_ref[l] = jnp.dot(ca, w_ref[l], precision=lax.Precision.HIGHEST, preferred_element_type=F32) + b_ref[l:l + 1, :]

    return pl.pallas_call(
        body, name="modulation_columns",
        out_shape=[jax.ShapeDtypeStruct((N_DEV, D_MODEL), F32), jax.ShapeDtypeStruct((N_LAYERS, N_DEV, cols), F32)],
        compiler_params=pltpu.CompilerParams(vmem_limit_bytes=VMEM_LIMIT),
    )(c_all, w_ada, b_cols)


def _adam(w, g, m, v):
    m2 = ADAM_B1 * m + (1.0 - ADAM_B1) * g
    v2 = ADAM_B2 * v + (1.0 - ADAM_B2) * (g * g)
    m_hat = m2 / (1.0 - ADAM_B1 ** ADAM_STEP)
    v_hat = v2 / (1.0 - ADAM_B2 ** ADAM_STEP)
    return -ADAM_LR * (m_hat / (jnp.sqrt(v_hat) + ADAM_EPS) + ADAM_WD * w), m2, v2


def _row_block(rows, cols, slots):
    target = max(8, (1 << 19) // (cols * max(slots, 1)))
    rb = rows
    while rb > target and rb % 2 == 0 and (rb // 2) % 8 == 0:
        rb //= 2
    return rb


def _adam_update(w, g, m, v, name):
    rows, cols = w.shape
    slotted = g.ndim == 3
    rb = _row_block(rows, cols, N_DEV if slotted else 1)

    def body(w_ref, g_ref, m_ref, v_ref, go_ref, d_ref, mo_ref, vo_ref):
        if slotted:
            gv = g_ref[0].astype(F32)
            for q in range(1, N_DEV):
                gv = gv + g_ref[q].astype(F32)
        else:
            gv = g_ref[...]
        go_ref[...] = gv
        d_ref[...], mo_ref[...], vo_ref[...] = _adam(w_ref[...], gv, m_ref[...], v_ref[...])

    blk = pl.BlockSpec((rb, cols), lambda i: (i, 0))
    g_blk = pl.BlockSpec((N_DEV, rb, cols), lambda i: (0, i, 0)) if slotted else blk
    return pl.pallas_call(
        body, name=name, grid=(rows // rb,),
        in_specs=[blk, g_blk, blk, blk], out_specs=[blk] * 4,
        out_shape=[jax.ShapeDtypeStruct((rows, cols), F32)] * 4,
        compiler_params=_params(("parallel",)),
    )(w, g, m, v)


def _adam_many(ws, gs, ms, vs, name):
    n = len(ws)

    def body(*refs):
        w_refs, g_refs, m_refs, v_refs, d_refs, mo_refs, vo_refs = (refs[k * n:(k + 1) * n] for k in range(7))
        for j in range(n):
            d_refs[j][...], mo_refs[j][...], vo_refs[j][...] = _adam(w_refs[j][...], g_refs[j][...], m_refs[j][...],
                                                                  v_refs[j][...])

    res = pl.pallas_call(
        body, name=name, out_shape=[jax.ShapeDtypeStruct(w.shape, F32) for w in ws] * 3,
        compiler_params=pltpu.CompilerParams(vmem_limit_bytes=VMEM_LIMIT),
    )(*ws, *gs, *ms, *vs)
    return res[:n], res[n:2 * n], res[2 * n:]


def _as_rows(a):
    return a.reshape(-1, a.shape[-1]) if a.ndim > 1 else a.reshape(1, -1)


def _ada_update(ca_t, dmod_cols, w, m, v):
    _, rows, cols = w.shape

    def body(ca_ref, dm_ref, w_ref, m_ref, v_ref, g_ref, d_ref, mo_ref, vo_ref):
        g = ca_ref[:, 0:1] * dm_ref[0, 0:1, :]
        for b in range(1, N_DEV):
            g = g + ca_ref[:, b:b + 1] * dm_ref[0, b:b + 1, :]
        g_ref[0] = g
        d_ref[0], mo_ref[0], vo_ref[0] = _adam(w_ref[0], g, m_ref[0], v_ref[0])

    blk = pl.BlockSpec((1, rows, cols), lambda l: (l, 0, 0))
    return pl.pallas_call(
        body, name="ada_update", grid=(N_LAYERS,),
        in_specs=[_full((rows, N_DEV)), pl.BlockSpec((1, N_DEV, cols), lambda l: (l, 0, 0)), blk, blk, blk],
        out_specs=[blk] * 4, out_shape=[jax.ShapeDtypeStruct(w.shape, F32)] * 4,
        compiler_params=_params(("parallel",)),
    )(ca_t, dmod_cols, w, m, v)


def _exchange_sems(n):
    return [pltpu.SemaphoreType.DMA((n, N_DEV - 1)), pltpu.SemaphoreType.DMA((n, N_DEV - 1)),
            pltpu.SemaphoreType.DMA((n,))]


def _exchange_copies(plans, srcs, outs, sems, receiving, only=None):
    send_sems, recv_sems, local_sems = sems
    x, y, c = lax.axis_index("x"), lax.axis_index("y"), lax.axis_index("c")
    me = 4 * x + 2 * y + c

    def remote(i, k, incoming):
        _, o, send, land = plans[i]
        px = 1 - x if k & 4 else x
        py = 1 - y if k & 2 else y
        pc = 1 - c if k & 1 else c
        p = 4 * px + 2 * py + pc
        return pltpu.make_async_remote_copy(
            src_ref=send(srcs[i], p), dst_ref=land(outs[o], p if incoming else me),
            send_sem=send_sems.at[i, k - 1], recv_sem=recv_sems.at[i, k - 1],
            device_id=(px, py, pc), device_id_type=pl.DeviceIdType.MESH)

    which = range(len(plans)) if only is None else only
    pairs = [(i, k) for k in range(1, N_DEV) for i in which]
    local = [pltpu.make_async_copy(plans[i][2](srcs[i], me), plans[i][3](outs[plans[i][1]], me), local_sems.at[i])
             for i in which]
    return local, [remote(i, k, False) for i, k in pairs], [remote(i, k, True) for i, k in pairs] if receiving else []


def _exchange_start(plans, srcs, outs, sems, only=None):
    local, outgoing, _ = _exchange_copies(plans, srcs, outs, sems, False, only)
    for cp in local + outgoing:
        cp.start()


def _exchange_wait(plans, srcs, outs, sems, only=None):
    local, outgoing, incoming = _exchange_copies(plans, srcs, outs, sems, True, only)
    for cp in incoming:
        cp.wait_recv()
    for cp in outgoing:
        cp.wait_send()
    for cp in local:
        cp.wait()


def _exchange(name, ride):
    out_shapes, plans = ride
    n = len(plans)
    hbm = pl.BlockSpec(memory_space=pltpu.HBM)

    def body(*refs):
        srcs, outs, sems = refs[:n], refs[n:n + len(out_shapes)], refs[n + len(out_shapes):]
        _exchange_start(plans, srcs, outs, sems)
        _exchange_wait(plans, srcs, outs, sems)

    return pl.pallas_call(
        body, name=name, in_specs=[hbm] * n, out_specs=[hbm] * len(out_shapes), out_shape=list(out_shapes),
        scratch_shapes=_exchange_sems(n),
    )(*[p[0] for p in plans])


def _first_gather(c, w):
    rows, n = w.shape
    c_shapes, c_plans = _plans([c], [_gather])
    hbm = pl.BlockSpec(memory_space=pltpu.HBM)

    def body(c_ref, w_ref, c_all_ref, w_all_ref, send_sems, recv_sems, local_sem, *c_sems):
        x, y, core = lax.axis_index("x"), lax.axis_index("y"), lax.axis_index("c")
        me, sibling = (x, y, core), (x, y, 1 - core)
        chips = [(1 - x, y), (x, 1 - y), (1 - x, 1 - y)]

        def block(px, py, pc):
            return w_all_ref.at[:, pl.ds(pl.multiple_of((4 * px + 2 * py + pc) * n, n), n)]

        def copy(k, origin, to, src=None):
            return pltpu.make_async_remote_copy(
                src_ref=block(*origin) if src is None else src, dst_ref=block(*origin),
                send_sem=send_sems.at[k], recv_sem=recv_sems.at[k], device_id=to, device_id_type=pl.DeviceIdType.MESH)

        _exchange_start(c_plans, [c_ref], [c_all_ref], c_sems)
        mine = pltpu.make_async_copy(w_ref, block(*me), local_sem)
        mine.start()
        first = [copy(0, me, sibling, src=w_ref)]
        first += [copy(1 + j, me, (*chip, core), src=w_ref) for j, chip in enumerate(chips)]
        for cp in first:
            cp.start()
        passed = [copy(4 + j, (*chip, core), sibling) for j, chip in enumerate(chips)]
        for j, chip in enumerate(chips):
            copy(1 + j, (*chip, core), me).wait_recv()
            passed[j].start()
        copy(0, sibling, me).wait_recv()
        for j, chip in enumerate(chips):
            copy(4 + j, (*chip, 1 - core), me).wait_recv()
        for cp in first + passed:
            cp.wait_send()
        mine.wait()
        _exchange_wait(c_plans, [c_ref], [c_all_ref], c_sems)

    return pl.pallas_call(
        body, name="first_gather", in_specs=[hbm, hbm], out_specs=[hbm, hbm],
        out_shape=[c_shapes[0], jax.ShapeDtypeStruct((rows, N_DEV * n), w.dtype)],
        scratch_shapes=[pltpu.SemaphoreType.DMA((N_DEV - 1,)), pltpu.SemaphoreType.DMA((N_DEV - 1,)),
                        pltpu.SemaphoreType.DMA(()), *_exchange_sems(1)],
    )(c, w)


def _finish_exchange(big, big_rules, packed, dmod):
    n_rows = packed.shape[0]
    r = n_rows // N_DEV
    shapes, plans = _plans([*big, packed, dmod], [*big_rules, _scatter_rows, _gather])
    n_first = len(plans)
    i_small = n_first - 2
    _, send, land = _gather_rows(jax.ShapeDtypeStruct((r, 128), F32))
    plans = plans + [(None, len(shapes), send, land)]
    shapes = shapes + [jax.ShapeDtypeStruct((n_rows, 128), F32)]
    first = [i for i in range(n_first) if i != i_small]
    hbm = pl.BlockSpec(memory_space=pltpu.HBM)

    def body(*refs):
        srcs, outs = list(refs[:n_first]), refs[n_first:n_first + len(shapes)]
        parts_ref, sum_ref, local_sem = refs[n_first + len(shapes):n_first + len(shapes) + 3]
        sems = refs[n_first + len(shapes) + 3:]
        srcs.append(sum_ref)
        _exchange_start(plans, srcs, outs, sems, only=range(n_first))
        _exchange_wait(plans, srcs, outs, sems, only=[i_small])
        cp = pltpu.make_async_copy(outs[i_small], parts_ref, local_sem)
        cp.start()
        cp.wait()
        g = parts_ref[0]
        for q in range(1, N_DEV):
            g = g + parts_ref[q]
        sum_ref[...] = g
        _exchange_start(plans, srcs, outs, sems, only=[n_first])
        _exchange_wait(plans, srcs, outs, sems, only=[n_first])
        _exchange_wait(plans, srcs, outs, sems, only=first)

    res = pl.pallas_call(
        body, name="finish_exchange", in_specs=[hbm] * n_first, out_specs=[hbm] * len(shapes), out_shape=shapes,
        scratch_shapes=[pltpu.VMEM((N_DEV, r, 128), F32), pltpu.VMEM((r, 128), F32), pltpu.SemaphoreType.DMA(()),
                        *_exchange_sems(len(plans))],
    )(*big, packed, dmod)
    return (*res[:len(big)], res[-1], res[n_first - 1])


def _tiled_call(body, args, *, name, grid, in_specs, out_specs, out_shape, scratch_shapes=(), ride=None, aliases=None):
    params = _params(("arbitrary",) * len(grid))
    if ride is None:
        return pl.pallas_call(body, name=name, grid=grid, in_specs=in_specs, out_specs=out_specs, out_shape=out_shape,
                              scratch_shapes=list(scratch_shapes), input_output_aliases=aliases or {},
                              compiler_params=params)(*args), []
    shapes, plans = ride
    n_in, n_src, n_out, n_dst, n_scr = len(in_specs), len(plans), len(out_specs), len(shapes), len(scratch_shapes)
    hbm = pl.BlockSpec(memory_space=pltpu.HBM)

    def carrying(*refs):
        ins, srcs, refs = refs[:n_in], refs[n_in:n_in + n_src], refs[n_in + n_src:]
        outs, dsts, refs = refs[:n_out], refs[n_out:n_out + n_dst], refs[n_out + n_dst:]
        scratch, sems = refs[:n_scr], refs[n_scr:]
        ids = [pl.program_id(a) for a in range(len(grid))]
        first = functools.reduce(jnp.logical_and, [i == 0 for i in ids])
        last = functools.reduce(jnp.logical_and, [i == g - 1 for i, g in zip(ids, grid)])

        @pl.when(first)
        def _():
            _exchange_start(plans, srcs, dsts, sems)

        body(*ins, *outs, *scratch)

        @pl.when(last)
        def _():
            _exchange_wait(plans, srcs, dsts, sems)

    res = pl.pallas_call(
        carrying, name=name, grid=grid, in_specs=[*in_specs, *[hbm] * n_src], out_specs=[*out_specs, *[hbm] * n_dst],
        out_shape=[*out_shape, *shapes], scratch_shapes=[*scratch_shapes, *_exchange_sems(n_src)],
        input_output_aliases=aliases or {}, compiler_params=params)(*args, *[p[0] for p in plans])
    return res[:n_out], res[n_out:]


def _tail(nd, idx):
    return (slice(None),) * (nd - 2) + idx


def _gather(a):
    return jax.ShapeDtypeStruct((N_DEV,) + a.shape, a.dtype), lambda s, p: s, lambda o, q: o.at[q]


def _gather_rows(a):
    r = a.shape[-2]
    return (jax.ShapeDtypeStruct(a.shape[:-2] + (N_DEV * r, a.shape[-1]), a.dtype), lambda s, p: s,
            lambda o, q: o.at[_tail(a.ndim, (pl.ds(pl.multiple_of(q * r, r), r), slice(None)))])


def _gather_cols(a):
    c = a.shape[-1]
    return (jax.ShapeDtypeStruct(a.shape[:-1] + (N_DEV * c,), a.dtype), lambda s, p: s,
            lambda o, q: o.at[_tail(a.ndim, (slice(None), pl.ds(pl.multiple_of(q * c, c), c)))])


def _scatter_rows(a):
    r = a.shape[0] // N_DEV
    return (jax.ShapeDtypeStruct((N_DEV, r, a.shape[1]), a.dtype),
            lambda s, p: s.at[pl.ds(pl.multiple_of(p * r, r), r), :], lambda o, q: o.at[q])


def _scatter_cols(a):
    c = a.shape[1] // N_DEV
    return (jax.ShapeDtypeStruct((N_DEV, a.shape[0], c), a.dtype),
            lambda s, p: s.at[:, pl.ds(pl.multiple_of(p * c, c), c)], lambda o, q: o.at[q])


def _plans(arrays, rules):
    shapes, plans = [], []
    for o, (a, rule) in enumerate(zip(arrays, rules)):
        shape, send, land = rule(a)
        shapes.append(shape)
        plans.append((a, o, send, land))
    return shapes, plans


def _pack(pieces, rows_multiple=8):
    flat = []
    for a in pieces:
        f = a.reshape(-1)
        flat.append(jnp.pad(f, (0, (-f.shape[0]) % 128)))
    total = sum(f.shape[0] for f in flat)
    flat.append(jnp.zeros(((-total) % (128 * rows_multiple),), F32))
    return jnp.concatenate(flat).reshape(-1, 128)


def _unpack(buf, shapes, lead=()):
    flat = buf.reshape(lead + (-1,))
    out, off = [], 0
    for s in shapes:
        n = math.prod(s)
        out.append(flat[..., off:off + n].reshape(lead + tuple(s)))
        off += n + (-n) % 128
    return out


def _pad_rows(a, rows):
    return jnp.pad(a, ((0, rows - a.shape[0]), (0, 0)))


VEC_NAMES = ('pool_scale', 'b_dw_c', 'ln_g_c', 'ln_b_c', 'b_pw2_c', 'ln_g_d', 'ln_b_d')
GATHERED = ('w_in', 'w_out', 'w_pw2_c', 'w_conv_a', 'w_dw_c')
GATHER_RULES = (_gather_cols, _gather_rows, _gather_rows, _gather, _gather)
SCATTER_RULES = (_scatter_cols, _scatter_rows, _scatter_rows)


def _weight_shards(shard, l):
    return [shard[n][l].astype(BF16) if n in ('w_in', 'w_out') else shard[n][l] for n in GATHERED]


def _layer_weights(shard, l, gathered):
    w_in_b, w_out_b, w_pw2, wconv_parts, wdw_parts = gathered
    wconv = wconv_parts.transpose(1, 0, 2).reshape(CONV_A, GROUP)
    wdw = wdw_parts.transpose(1, 0, 2).reshape(CONV_C, GROUP)
    wp = jnp.einsum('gcd,gh->gchd', shard['w_pool'][l], jnp.eye(4, dtype=F32)).reshape(GROUP, GROUP)
    ws = shard['w_s_d'][l] * jnp.tril(jnp.ones((SUB, SUB), F32))
    vec = jnp.stack([shard[n][l] for n in VEC_NAMES])
    small = (_pad_rows(wconv, 8), _pad_rows(wdw, HALO), _pad_rows(vec, 16), wp.astype(BF16), w_pw2.astype(BF16),
             ws.reshape(4 * SUB, SUB).astype(BF16), jnp.repeat(shard['b_s_d'][l].T, 64, axis=1))
    small_t = (wp.T.astype(BF16), w_pw2.T.astype(BF16), ws.transpose(0, 2, 1).reshape(4 * SUB, SUB).astype(BF16))
    return w_in_b, w_out_b, small, small_t


def kernel(x, c, norm_g, w_ada, b_ada, w_in, w_conv_a, w_pool, pool_scale, w_dw_c, b_dw_c, ln_g_c, ln_b_c, w_pw2_c, b_pw2_c, ln_g_d, ln_b_d, w_s_d, b_s_d, w_out, final_g, loss_target, m_norm_g, m_w_ada, m_b_ada, m_w_in, m_w_conv_a, m_w_pool, m_pool_scale, m_w_dw_c, m_b_dw_c, m_ln_g_c, m_ln_b_c, m_w_pw2_c, m_b_pw2_c, m_ln_g_d, m_ln_b_d, m_w_s_d, m_b_s_d, m_w_out, m_final_g, v_norm_g, v_w_ada, v_b_ada, v_w_in, v_w_conv_a, v_w_pool, v_pool_scale, v_w_dw_c, v_b_dw_c, v_ln_g_c, v_ln_b_c, v_w_pw2_c, v_b_pw2_c, v_ln_g_d, v_ln_b_d, v_w_s_d, v_b_s_d, v_w_out, v_final_g):
    given = dict(locals())
    shard = {n: given[n] for n in WEIGHTS}
    mom_m = {n: given['m_' + n] for n in WEIGHTS}
    mom_v = {n: given['v_' + n] for n in WEIGHTS}
    me = 4 * lax.axis_index("x") + 2 * lax.axis_index("y") + lax.axis_index("c")
    n_tok = x.shape[1]
    tile = min(TOKEN_TILE, n_tok)
    wide_tile = min(2 * TOKEN_TILE, n_tok)
    x0 = x.reshape(n_tok, D_MODEL)
    target = loss_target.reshape(n_tok, D_MODEL)
    ada_cols = w_ada.shape[2]

    first_shards = _weight_shards(shard, 0)
    c_all, w_in_first = _first_gather(c, first_shards[0])

    b_cols = lax.dynamic_slice_in_dim(b_ada, me * ada_cols, ada_cols, axis=1)
    c_act, mod_cols = _modulation_columns(c_all.reshape(N_DEV, D_MODEL), w_ada, b_cols)
    (mod_all,) = _exchange("gather_modulation", _plans([mod_cols], [_gather]))
    mod = lax.dynamic_index_in_dim(mod_all, me, axis=2, keepdims=False)
    mod = mod.transpose(1, 0, 2).reshape(N_LAYERS, 3 * D_MODEL)
    shift, scale, gate = (mod[:, k * D_MODEL:(k + 1) * D_MODEL].reshape(N_LAYERS, 1, D_MODEL) for k in range(3))
    gs = norm_g.reshape(N_LAYERS, 1, D_MODEL) * (1.0 + scale)

    xs, hs, zs, ocs, layers = [x0], [], [], [], []
    for l in range(N_LAYERS):
        if l == 0:
            (h, z), rest = _in_proj(xs[0], gs[0], shift[0], w_in_first, wide_tile,
                                    ride=_plans(first_shards[1:], GATHER_RULES[1:]))
            layers.append(_layer_weights(shard, 0, [w_in_first, *rest]))
        else:
            (h, z), _ = _in_proj(xs[l], gs[l], shift[l], layers[l][0], wide_tile)
        _, w_out_b, small, _ = layers[l]
        hs.append(h)
        zs.append(z)
        if l + 1 < N_LAYERS:
            (x_next, o_c), gathered = _mix_out(z, xs[l], gate[l], small, w_out_b, tile,
                                               ride=_plans(_weight_shards(shard, l + 1), GATHER_RULES))
            xs.append(x_next)
            layers.append(_layer_weights(shard, l + 1, gathered))
        else:
            (dx, o_c, loss_part, dfinal_g), _ = _mix_out(z, xs[l], gate[l], small, w_out_b, tile,
                                                         head=(final_g.reshape(1, D_MODEL), target))
        ocs.append(o_c)
    loss = lax.psum(loss_part[0, 0], ("x", "y", "c"))

    part = {}
    layer_parts = [None] * N_LAYERS
    slots = [None] * N_LAYERS
    for l in reversed(range(N_LAYERS)):
        w_in_b, w_out_b, small, small_t = layers[l]
        ride = _plans(layer_parts[l + 1]['big'], SCATTER_RULES) if l + 1 < N_LAYERS else None
        (dz, ycat, sums, dwp, dw2, dws, dbs), rode = _mix_bwd(zs[l], ocs[l], dx, gate[l], small, small_t, w_out_b, tile,
                                                              ride=ride)
        if ride:
            slots[l + 1] = rode
        dw_in, _ = _tokens_matmul(hs[l], dz, "in_proj_tokens_matmul", out_dtype=BF16)
        m_out, _ = _tokens_matmul(ycat, dx, "out_proj_tokens_matmul")
        dw_out, dgate = _out_proj_grads(m_out, w_out_b, gate[l])
        if l > 0:
            (dx, dshift, dgs), _ = _norm_bwd(xs[l], dz, dx, gs[l], w_in_b, tile)
        else:
            n_blocks = n_tok // tile
            n_under = max(1, 3 * n_blocks // 4)
            begun, slots[l] = _norm_bwd(xs[l], dz, dx, gs[l], w_in_b, tile, blocks=(0, n_under), finish=False,
                                        ride=_plans([dw_in, dw_out, dw2], SCATTER_RULES))
            (dx, dshift, dgs), _ = _norm_bwd(xs[l], dz, dx, gs[l], w_in_b, tile, blocks=(n_under, n_blocks - n_under),
                                             begun=begun)
        layer_parts[l] = dict(
            big=[dw_in, dw_out, dw2],
            b_ada=jnp.concatenate([dshift, dgs * norm_g[l][None], dgate], axis=1)[0],
            norm_g=(dgs * (1.0 + scale[l]))[0],
            w_conv_a=sums[S_WCONV:S_WCONV + CONV_A], w_dw_c=sums[S_WDW:S_WDW + CONV_C],
            pool_scale=sums[S_PSCALE], b_dw_c=sums[S_BDW], ln_g_c=sums[S_LNGC], ln_b_c=sums[S_LNBC],
            b_pw2_c=sums[S_BPW2], ln_g_d=sums[S_LNGD], ln_b_d=sums[S_LNBD],
            w_pool=jnp.einsum('gchd,gh->gcd', dwp.reshape(4, 64, 4, 64), jnp.eye(4, dtype=F32)),
            w_s_d=dws.reshape(4, SUB, SUB) * jnp.tril(jnp.ones((SUB, SUB), F32)),
            b_s_d=dbs.reshape(SUB, 4, 64).sum(axis=-1).T)
    grad_x = dx.reshape(x.shape)
    for n in REPLICATED + CHANNEL_SHARDED:
        part[n] = dfinal_g[0] if n == 'final_g' else jnp.stack([layer_parts[l][n] for l in range(N_LAYERS)])

    small_names = REPLICATED + CHANNEL_SHARDED
    small_shapes = [part[n].shape for n in small_names]
    small_sum, dmod_all = _finish_exchange([], [], _pack([part[n] for n in small_names], rows_multiple=8 * N_DEV),
                                           part['b_ada'])

    grads, deltas, new_m, new_v = {}, {}, {}, {}
    for j, n in enumerate(('w_in', 'w_out', 'w_pw2_c')):
        outs = [_adam_update(shard[n][l], slots[l][j], mom_m[n][l], mom_v[n][l], "update_" + n) for l in range(N_LAYERS)]
        grads[n], deltas[n], new_m[n], new_v[n] = (jnp.stack(o) for o in zip(*outs))

    gsum = dict(zip(small_names, _unpack(small_sum, small_shapes)))
    for n in CHANNEL_SHARDED:
        width = shard[n].shape[2]
        gsum[n] = lax.dynamic_slice_in_dim(gsum[n], me * width, width, axis=2)
    d_small, m_small, v_small = _adam_many(*[[_as_rows(d[n]) for n in small_names] for d in (shard, gsum, mom_m, mom_v)],
                                           "update_small")
    for j, n in enumerate(small_names):
        grads[n] = gsum[n]
        deltas[n], new_m[n], new_v[n] = (o[j].reshape(shard[n].shape) for o in (d_small, m_small, v_small))

    dmod_cols = lax.dynamic_slice_in_dim(dmod_all, me * ada_cols, ada_cols, axis=2).transpose(1, 0, 2)
    grads['w_ada'], deltas['w_ada'], new_m['w_ada'], new_v['w_ada'] = _ada_update(
        c_act.T, dmod_cols, w_ada, m_w_ada, v_w_ada)

    return (loss, grad_x, *[grads[n] for n in WEIGHTS], *[deltas[n] for n in WEIGHTS],
            *[new_m[n] for n in WEIGHTS], *[new_v[n] for n in WEIGHTS])
```

```python
import functools
import math

import jax
import jax.numpy as jnp
from jax import lax
from jax.experimental import pallas as pl
from jax.experimental.pallas import tpu as pltpu

F32 = jnp.float32
BF16 = jnp.bfloat16

N_DEV = 8
D_MODEL = 1024
GROUP = 256
D_IN = 12 * GROUP
N_LAYERS = 2
HALO = 32
SUB = 128
WIN = SUB + HALO
TOKEN_TILE = 512
REDUCE_TILE = 2048
EPS = 1e-6
VMEM_BYTES_V7X = 64 * 1024 * 1024
VMEM_LIMIT = VMEM_BYTES_V7X - 8 * 1024 * 1024

ADAM_LR = 0.001
ADAM_B1 = 0.9
ADAM_B2 = 0.999
ADAM_EPS = 1e-08
ADAM_WD = 0.01
ADAM_STEP = 10

A_B, A_C, A_X, A_G, B_P, B_G, C_A, C_GL, C_G, D_U, D_V, D_G = range(12)
V_PSCALE, V_BDW, V_LNGC, V_LNBC, V_BPW2, V_LNGD, V_LNBD = range(7)
S_WCONV, S_PSCALE, S_BDW, S_LNGC, S_LNBC, S_BPW2, S_LNGD, S_LNBD, S_WDW = 0, 3, 4, 5, 6, 7, 8, 9, 16
N_SUMS = 64
CONV_A = 3
CONV_C = 31

WEIGHTS = ('norm_g', 'w_ada', 'b_ada', 'w_in', 'w_conv_a', 'w_pool', 'pool_scale', 'w_dw_c', 'b_dw_c', 'ln_g_c',
           'ln_b_c', 'w_pw2_c', 'b_pw2_c', 'ln_g_d', 'ln_b_d', 'w_s_d', 'b_s_d', 'w_out', 'final_g')
REPLICATED = ('norm_g', 'b_ada', 'w_pool', 'pool_scale', 'b_dw_c', 'ln_g_c', 'ln_b_c', 'b_pw2_c', 'ln_g_d', 'ln_b_d',
              'w_s_d', 'b_s_d', 'final_g')
CHANNEL_SHARDED = ('w_conv_a', 'w_dw_c')


def _params(semantics, vmem=VMEM_LIMIT):
    return pltpu.CompilerParams(dimension_semantics=semantics, vmem_limit_bytes=vmem)


def _cols(g):
    return slice(g * GROUP, (g + 1) * GROUP)


def _full(shape):
    return pl.BlockSpec(shape, lambda *_: (0,) * len(shape))


def _silu(x):
    s = jax.nn.sigmoid(x)
    return x * s, s


def _dsilu(x, s):
    return s * (1.0 + x * (1.0 - s))


_GELU_C0 = math.sqrt(2.0 / math.pi)
_GELU_C1 = 0.044715


def _gelu(x):
    th = jnp.tanh(_GELU_C0 * (x + _GELU_C1 * (x * x * x)))
    return 0.5 * x * (1.0 + th), th


def _dgelu(x, th):
    return 0.5 * (1.0 + th) + 0.5 * x * (1.0 - th * th) * (_GELU_C0 * (1.0 + 3.0 * _GELU_C1 * (x * x)))


def _layer_norm(x):
    mu = jnp.mean(x, axis=-1, keepdims=True)
    xc = x - mu
    rstd = lax.rsqrt(jnp.mean(xc * xc, axis=-1, keepdims=True) + EPS)
    return xc * rstd, rstd


def _layer_norm_bwd(dn, n, rstd):
    return rstd * (dn - jnp.mean(dn, axis=-1, keepdims=True) - n * jnp.mean(dn * n, axis=-1, keepdims=True))


def _shift_rows(a, k):
    k = k % a.shape[0]
    return a if k == 0 else pltpu.roll(a, k, 0)


def _row_sum8(a):
    s = a[0:8]
    for m in range(1, a.shape[0] // 8):
        s = s + a[8 * m:8 * m + 8]
    return s


def _lane():
    return lax.broadcasted_iota(jnp.int32, (SUB, GROUP), 1)


def _by_quarter(lane, parts):
    return jnp.where(lane < 64, parts[0], jnp.where(lane < 128, parts[1], jnp.where(lane < 192, parts[2], parts[3])))


def _conv_inputs(z_ref, rows):
    def f(g):
        return z_ref[rows, _cols(g)].astype(F32)
    return f(A_C) * f(A_X), f(B_P), f(C_A) * jax.nn.sigmoid(f(C_GL))


def _fill_past(past_ref, zh_ref, zm_ref, is_first, tile):
    parts = _conv_inputs(zh_ref, slice(None))
    for n, a in enumerate(parts):
        past_ref[0:HALO, _cols(n)] = jnp.where(is_first, 0.0, a)

    def body(j, carry):
        r0 = pl.multiple_of(j * SUB, SUB)
        for n, a in enumerate(_conv_inputs(zm_ref, pl.ds(r0, SUB))):
            past_ref[pl.ds(r0 + HALO, SUB), _cols(n)] = a
        return carry

    lax.fori_loop(0, tile // SUB, body, 0)


def _short_conv_taps(qw):
    return [_shift_rows(qw, CONV_A - 1 - k)[HALO:WIN] for k in range(CONV_A)]


def _window_sums(pw, lane):
    s2 = pw + _shift_rows(pw, 1)
    s4 = s2 + _shift_rows(s2, 2)
    s8 = s4 + _shift_rows(s4, 4)
    s16 = s8 + _shift_rows(s8, 8)
    return _by_quarter(lane, [s[HALO:WIN] for s in (s2, s4, s8, s16)])


def _inv_count(lane, t_first):
    width = _by_quarter(lane, [2.0, 4.0, 8.0, 16.0])
    t = lax.broadcasted_iota(jnp.int32, (SUB, GROUP), 0) + t_first
    return 1.0 / jnp.minimum((t + 1).astype(F32), width)


def _forward_window_sums(ew, lane):
    n = ew.shape[0]
    f2 = ew + _shift_rows(ew, n - 1)
    f4 = f2 + _shift_rows(f2, n - 2)
    f8 = f4 + _shift_rows(f4, n - 4)
    f16 = f8 + _shift_rows(f8, n - 8)
    return _by_quarter(lane, [f[0:SUB] for f in (f2, f4, f8, f16)])


def _mixer_forwards(zc, win, t_first, wc_ref, wdw_ref, vec_ref, wp_ref, w2_ref, ws_ref, bs_ref, o_c=None):
    def vec(n):
        return vec_ref[n:n + 1, :]

    def short_conv():
        taps = _short_conv_taps(win(0))
        o_a = wc_ref[0:1, :] * taps[0] + wc_ref[1:2, :] * taps[1] + wc_ref[2:3, :] * taps[2]
        a_b, a_g = zc(A_B), zc(A_G)
        sg_a, s_a = _silu(a_g)
        return a_b * o_a * sg_a, dict(taps=taps, o_a=o_a, a_b=a_b, a_g=a_g, sg_a=sg_a, s_a=s_a)

    def pooling():
        lane = _lane()
        pw = win(1)
        ic = _inv_count(lane, t_first)
        pooled_b = (_window_sums(pw, lane) * ic - pw[HALO:WIN]).astype(BF16)
        y0_b = jnp.dot(pooled_b, wp_ref[...], preferred_element_type=F32)
        b_g = zc(B_G)
        sg_b, s_b = _silu(b_g)
        return y0_b * vec(V_PSCALE) * sg_b, dict(ic=ic, pooled_b=pooled_b, y0_b=y0_b, b_g=b_g, sg_b=sg_b, s_b=s_b)

    def conformer():
        hw = win(2)
        o = o_c
        if o is None:
            o = wdw_ref[CONV_C - 1:CONV_C, :] * hw[HALO:WIN] + vec(V_BDW)
            for k in range(CONV_C - 1):
                o = o + wdw_ref[k:k + 1, :] * _shift_rows(hw, CONV_C - 1 - k)[HALO:WIN]
        n_c, rstd_c = _layer_norm(o)
        ln_c = n_c * vec(V_LNGC) + vec(V_LNBC)
        sl_c, ssl_c = _silu(ln_c)
        sl_b = sl_c.astype(BF16)
        yc = jnp.dot(sl_b, w2_ref[...], preferred_element_type=F32) + vec(V_BPW2)
        c_g = zc(C_G)
        sg_c, s_c = _silu(c_g)
        return yc * sg_c, dict(hw=hw, o_c=o, n_c=n_c, rstd_c=rstd_c, ln_c=ln_c, ssl_c=ssl_c, sl_b=sl_b, yc=yc, c_g=c_g,
                               sg_c=sg_c, s_c=s_c)

    def gating():
        lane = _lane()
        d_u, d_v, d_g = zc(D_U), zc(D_V), zc(D_G)
        u, th_u = _gelu(d_u)
        gv, th_v = _gelu(d_v)
        n_d, rstd_d = _layer_norm(gv)
        v_b = (n_d * vec(V_LNGD) + vec(V_LNBD)).astype(BF16)
        r = jnp.dot(ws_ref[...], v_b, preferred_element_type=F32)
        mixed = _by_quarter(lane, [r[h * SUB:(h + 1) * SUB] for h in range(4)]) + bs_ref[...]
        sg_d, s_d = _silu(d_g)
        return u * mixed * sg_d, dict(d_u=d_u, d_v=d_v, d_g=d_g, u=u, th_u=th_u, th_v=th_v, n_d=n_d, rstd_d=rstd_d,
                                      v_b=v_b, mixed=mixed, sg_d=sg_d, s_d=s_d)

    return short_conv, pooling, conformer, gating


def _in_proj(x, gs, shift, w_in_b, tile, ride=None):
    n_tok = x.shape[0]

    def body(x_ref, gs_ref, sh_ref, w_ref, h_ref, z_ref):
        xv = x_ref[...]
        r = lax.rsqrt(jnp.mean(xv * xv, axis=-1, keepdims=True) + EPS)
        h = ((xv * r) * gs_ref[...] + sh_ref[...]).astype(BF16)
        h_ref[...] = h
        for j in range(D_IN // D_MODEL):
            cs = slice(j * D_MODEL, (j + 1) * D_MODEL)
            z_ref[:, cs] = jnp.dot(h, w_ref[:, cs], preferred_element_type=F32).astype(BF16)

    return _tiled_call(
        body, (x, gs, shift, w_in_b), name="in_proj", grid=(n_tok // tile,),
        in_specs=[pl.BlockSpec((tile, D_MODEL), lambda i: (i, 0)), _full((1, D_MODEL)), _full((1, D_MODEL)),
                  _full((D_MODEL, D_IN))],
        out_specs=[pl.BlockSpec((tile, D_MODEL), lambda i: (i, 0)), pl.BlockSpec((tile, D_IN), lambda i: (i, 0))],
        out_shape=[jax.ShapeDtypeStruct((n_tok, D_MODEL), BF16), jax.ShapeDtypeStruct((n_tok, D_IN), BF16)],
        ride=ride)


def _small_specs(with_transposes):
    specs = [_full((8, GROUP)), _full((HALO, GROUP)), _full((16, GROUP)), _full((GROUP, GROUP)), _full((GROUP, GROUP)),
             _full((4 * SUB, SUB)), _full((SUB, GROUP))]
    if with_transposes:
        specs += [_full((GROUP, GROUP)), _full((GROUP, GROUP)), _full((4 * SUB, SUB))]
    return specs


def _mix_out(z, x, gate, small, w_out_b, tile, ride=None, head=None):
    n_tok = x.shape[0]
    n_tiles = n_tok // tile
    n_sub = tile // SUB
    cw = D_MODEL // n_sub
    per_halo = tile // HALO
    n_in = 12 + (2 if head else 0)
    n_out = 4 if head else 2

    def cur(i):
        return jnp.minimum(i, n_tiles - 1)

    def prev(i):
        return jnp.maximum(i - 1, 0)

    def body(*refs):
        (zm_ref, zh_ref, x_ref, gate_ref, wc_ref, wdw_ref, vec_ref, wp_ref, w2_ref, ws_ref, bs_ref, wout_ref) = refs[:12]
        xo_ref, oc_ref = refs[n_in:n_in + 2]
        past_ref, ycat_ref, ycat_prev_ref = refs[n_in + n_out:n_in + n_out + 3]
        i = pl.program_id(0)
        t = cur(i)
        if head:
            g_ref, tgt_ref = refs[12:14]
            loss_ref, dg_ref = refs[n_in + 2:n_in + 4]
            xn_ref, acc_ref = refs[n_in + n_out + 3:]
        else:
            xn_ref = xo_ref

        @pl.when(i == 0)
        def _():
            ycat_prev_ref[...] = jnp.zeros_like(ycat_prev_ref)
            if head:
                acc_ref[...] = jnp.zeros_like(acc_ref)

        _fill_past(past_ref, zh_ref, zm_ref, t == 0, tile)
        for j in range(n_sub):
            cs = slice(j * cw, (j + 1) * cw)
            y = jnp.dot(ycat_prev_ref[...], wout_ref[:, cs], preferred_element_type=F32)
            xn_ref[:, cs] = x_ref[:, cs] + gate_ref[:, cs] * y
            rows = slice(j * SUB, (j + 1) * SUB)
            mixers = _mixer_forwards(
                lambda g: zm_ref[rows, _cols(g)].astype(F32), lambda n: past_ref[j * SUB:j * SUB + WIN, _cols(n)],
                t * tile + j * SUB, wc_ref, wdw_ref, vec_ref, wp_ref, w2_ref, ws_ref, bs_ref)
            for n, mixer in enumerate(mixers):
                y, s = mixer()
                ycat_ref[rows, _cols(n)] = y.astype(BF16)
                if "o_c" in s:
                    oc_ref[rows, :] = s["o_c"]
        ycat_prev_ref[...] = ycat_ref[...]
        if head:
            counted = jnp.where(i > 0, 1.0, 0.0)
            xo_ref[...] = _loss_head_block(xn_ref[...], g_ref[...], tgt_ref[...], acc_ref, counted)

            @pl.when(i == n_tiles)
            def _():
                loss_ref[...] = jnp.full((8, 128), 0.5 / D_MODEL, F32) * jnp.sum(acc_ref[0])
                dg_ref[...] = jnp.sum(acc_ref[1], axis=0, keepdims=True)

    in_specs = [pl.BlockSpec((tile, D_IN), lambda i: (cur(i), 0)),
                pl.BlockSpec((HALO, D_IN), lambda i: (jnp.maximum(cur(i) * per_halo - 1, 0), 0)),
                pl.BlockSpec((tile, D_MODEL), lambda i: (prev(i), 0)), _full((1, D_MODEL)),
                *_small_specs(False), _full((D_MODEL, D_MODEL))]
    out_specs = [pl.BlockSpec((tile, D_MODEL), lambda i: (prev(i), 0)), pl.BlockSpec((tile, GROUP), lambda i: (cur(i), 0))]
    out_shape = [jax.ShapeDtypeStruct((n_tok, D_MODEL), F32), jax.ShapeDtypeStruct((n_tok, GROUP), F32)]
    scratch = [pltpu.VMEM((tile + HALO, 3 * GROUP), F32), pltpu.VMEM((tile, D_MODEL), BF16), pltpu.VMEM((tile, D_MODEL), BF16)]
    args = (z, z, x, gate, *small, w_out_b)
    if head:
        in_specs += [_full((1, D_MODEL)), pl.BlockSpec((tile, D_MODEL), lambda i: (prev(i), 0))]
        out_specs += [_full((8, 128)), _full((1, D_MODEL))]
        out_shape += [jax.ShapeDtypeStruct((8, 128), F32), jax.ShapeDtypeStruct((1, D_MODEL), F32)]
        scratch += [pltpu.VMEM((tile, D_MODEL), F32), pltpu.VMEM((2, 8, D_MODEL), F32)]
        args += tuple(head)
    outs, rode = _tiled_call(body, args, name="mix_out", grid=(n_tiles + 1,), in_specs=in_specs, out_specs=out_specs,
                             out_shape=out_shape, scratch_shapes=scratch, ride=ride)
    return outs, rode


def _loss_head_block(xv, g, target, acc_ref, counted):
    r = lax.rsqrt(jnp.mean(xv * xv, axis=-1, keepdims=True) + EPS)
    xn = xv * r
    err = xn * g - target
    acc_ref[0] = acc_ref[0] + counted * _row_sum8(err * err)
    dy = err * (1.0 / D_MODEL)
    acc_ref[1] = acc_ref[1] + counted * _row_sum8(dy * xn)
    a = dy * g
    return r * (a - xn * jnp.mean(a * xn, axis=-1, keepdims=True))


def _mix_bwd(z, o_c, dx_next, gate, small, small_t, w_out_b, tile, ride=None):
    n_tok = z.shape[0]
    n_tiles = n_tok // tile
    n_sub = tile // SUB
    cw = D_MODEL // n_sub
    per_halo = tile // HALO
    nt_dims = (((1,), (1,)), ((), ()))

    def tile_of(i):
        return n_tiles - 1 - i

    def next_tile_of(i):
        return jnp.maximum(n_tiles - 2 - i, 0)

    def body(zm_ref, zh_ref, oc_ref, dxn_ref, dxn_next_ref, gate_ref, wc_ref, wdw_ref, vec_ref, wp_ref, w2_ref, ws_ref,
             bs_ref, wpt_ref, w2t_ref, wst_ref, wout_ref,
             dz_ref, ycat_ref, sums_ref, dwp_ref, dw2_ref, dws_ref, dbs_ref,
             past_ref, future_ref, dy_ref, dy_next_ref, acc_ref):
        i = pl.program_id(0)
        t = tile_of(i)

        @pl.when(i == 0)
        def _():
            acc_ref[...] = jnp.zeros_like(acc_ref)
            dwp_ref[...] = jnp.zeros_like(dwp_ref)
            dw2_ref[...] = jnp.zeros_like(dw2_ref)
            dws_ref[...] = jnp.zeros_like(dws_ref)
            dbs_ref[...] = jnp.zeros_like(dbs_ref)
            future_ref[tile:tile + HALO, :] = jnp.zeros((HALO, 3 * GROUP), F32)
            dy_ref[...] = lax.dot_general((dxn_ref[...] * gate_ref[...]).astype(BF16), wout_ref[...], nt_dims,
                                        preferred_element_type=F32)

        _fill_past(past_ref, zh_ref, zm_ref, t == 0, tile)
        dyb_next = (dxn_next_ref[...] * gate_ref[...]).astype(BF16)

        def vec(n):
            return vec_ref[n:n + 1, :]

        for jj in range(n_sub):
            j = n_sub - 1 - jj
            r0 = j * SUB
            rows = slice(r0, r0 + SUB)

            def zc(g):
                return zm_ref[rows, _cols(g)].astype(F32)

            def add(n, a):
                acc_ref[n] = acc_ref[n] + _row_sum8(a)

            def put(g, a):
                dz_ref[rows, _cols(g)] = a.astype(BF16)

            def future_window(n, a):
                future_ref[rows, _cols(n)] = a
                return future_ref[r0:r0 + WIN, _cols(n)]

            short_conv, pooling, conformer, gating = _mixer_forwards(
                zc, lambda n: past_ref[r0:r0 + WIN, _cols(n)], t * tile + r0,
                wc_ref, wdw_ref, vec_ref, wp_ref, w2_ref, ws_ref, bs_ref, o_c=oc_ref[rows, :])
            lane = _lane()
            ks = slice(jj * cw, (jj + 1) * cw)
            dy_next_ref[:, ks] = lax.dot_general(dyb_next, wout_ref[ks, :], nt_dims, preferred_element_type=F32)

            y, s = short_conv()
            ycat_ref[rows, _cols(0)] = y.astype(BF16)
            dy = dy_ref[rows,_cols(0)]
            put(A_B, dy * s["o_a"] * s["sg_a"])
            put(A_G, dy * s["a_b"] * s["o_a"] * _dsilu(s["a_g"], s["s_a"]))
            do = dy * s["a_b"] * s["sg_a"]
            for k in range(CONV_A):
                add(S_WCONV + k, do * s["taps"][k])
            dow = future_window(0, do)
            dq = wc_ref[CONV_A - 1:CONV_A, :] * dow[0:SUB]
            for k in range(CONV_A - 1):
                dq = dq + wc_ref[k:k + 1, :] * _shift_rows(dow, WIN - (CONV_A - 1 - k))[0:SUB]
            put(A_C, dq * zc(A_X))
            put(A_X, dq * zc(A_C))

            y, s = pooling()
            ycat_ref[rows, _cols(1)] = y.astype(BF16)
            dy = dy_ref[rows,_cols(1)]
            put(B_G, dy * (s["y0_b"] * vec(V_PSCALE)) * _dsilu(s["b_g"], s["s_b"]))
            dyb = dy * s["sg_b"]
            add(S_PSCALE, dyb * s["y0_b"])
            dpw_b = (dyb * vec(V_PSCALE)).astype(BF16)
            dwp_ref[...] += lax.dot_general(s["pooled_b"], dpw_b, (((0,), (0,)), ((), ())), preferred_element_type=F32)
            dpooled = jnp.dot(dpw_b, wpt_ref[...], preferred_element_type=F32)
            ew = future_window(1, dpooled * s["ic"])
            put(B_P, _forward_window_sums(ew, lane) - dpooled)

            y, s = conformer()
            ycat_ref[rows, _cols(2)] = y.astype(BF16)
            dy = dy_ref[rows,_cols(2)]
            put(C_G, dy * s["yc"] * _dsilu(s["c_g"], s["s_c"]))
            dyc = dy * s["sg_c"]
            add(S_BPW2, dyc)
            dyc_b = dyc.astype(BF16)
            dw2_ref[...] += lax.dot_general(s["sl_b"], dyc_b, (((0,), (0,)), ((), ())), preferred_element_type=F32)
            dln = jnp.dot(dyc_b, w2t_ref[...], preferred_element_type=F32) * _dsilu(s["ln_c"], s["ssl_c"])
            add(S_LNGC, dln * s["n_c"])
            add(S_LNBC, dln)
            do = _layer_norm_bwd(dln * vec(V_LNGC), s["n_c"], s["rstd_c"])
            add(S_BDW, do)
            hw = s["hw"]
            for k in range(CONV_C):
                add(S_WDW + k, do * _shift_rows(hw, CONV_C - 1 - k)[HALO:WIN])
            dow = future_window(2, do)
            dhc = wdw_ref[CONV_C - 1:CONV_C, :] * dow[0:SUB]
            for k in range(CONV_C - 1):
                dhc = dhc + wdw_ref[k:k + 1, :] * _shift_rows(dow, WIN - (CONV_C - 1 - k))[0:SUB]
            c_a = zc(C_A)
            sgl = jax.nn.sigmoid(zc(C_GL))
            put(C_A, dhc * sgl)
            put(C_GL, dhc * c_a * sgl * (1.0 - sgl))

            y, s = gating()
            ycat_ref[rows, _cols(3)] = y.astype(BF16)
            dy = dy_ref[rows,_cols(3)]
            put(D_G, dy * s["u"] * s["mixed"] * _dsilu(s["d_g"], s["s_d"]))
            put(D_U, dy * s["mixed"] * s["sg_d"] * _dgelu(s["d_u"], s["th_u"]))
            dmixed = dy * s["u"] * s["sg_d"]
            dbs_ref[...] += dmixed
            by_head = jnp.concatenate(
                [jnp.where((lane >= 64 * h) & (lane < 64 * h + 64), dmixed, 0.0) for h in range(4)], axis=0).astype(BF16)
            dws_ref[...] += lax.dot_general(by_head, s["v_b"], (((1,), (1,)), ((), ())), preferred_element_type=F32)
            rv = jnp.dot(wst_ref[...], dmixed.astype(BF16), preferred_element_type=F32)
            dv = _by_quarter(lane, [rv[h * SUB:(h + 1) * SUB] for h in range(4)])
            add(S_LNGD, dv * s["n_d"])
            add(S_LNBD, dv)
            dgv = _layer_norm_bwd(dv * vec(V_LNGD), s["n_d"], s["rstd_d"])
            put(D_V, dgv * _dgelu(s["d_v"], s["th_v"]))

        future_ref[tile:tile + HALO, :] = future_ref[0:HALO, :]
        dy_ref[...] = dy_next_ref[...]

        @pl.when(i == n_tiles - 1)
        def _():
            for n in range(N_SUMS):
                sums_ref[n:n + 1, :] = jnp.sum(acc_ref[n], axis=0, keepdims=True)

    return _tiled_call(
        body, (z, z, o_c, dx_next, dx_next, gate, *small, *small_t, w_out_b), name="mix_bwd", grid=(n_tiles,),
        in_specs=[pl.BlockSpec((tile, D_IN), lambda i: (tile_of(i), 0)),
                  pl.BlockSpec((HALO, D_IN), lambda i: (jnp.maximum(tile_of(i) * per_halo - 1, 0), 0)),
                  pl.BlockSpec((tile, GROUP), lambda i: (tile_of(i), 0)),
                  pl.BlockSpec((tile, D_MODEL), lambda i: (tile_of(i), 0)),
                  pl.BlockSpec((tile, D_MODEL), lambda i: (next_tile_of(i), 0)), _full((1, D_MODEL)),
                  *_small_specs(True), _full((D_MODEL, D_MODEL))],
        out_specs=[pl.BlockSpec((tile, D_IN), lambda i: (tile_of(i), 0)),
                   pl.BlockSpec((tile, D_MODEL), lambda i: (tile_of(i), 0)),
                   _full((N_SUMS, GROUP)), _full((GROUP, GROUP)), _full((GROUP, GROUP)), _full((4 * SUB, SUB)),
                   _full((SUB, GROUP))],
        out_shape=[jax.ShapeDtypeStruct((n_tok, D_IN), BF16), jax.ShapeDtypeStruct((n_tok, D_MODEL), BF16),
                   jax.ShapeDtypeStruct((N_SUMS, GROUP), F32), jax.ShapeDtypeStruct((GROUP, GROUP), F32),
                   jax.ShapeDtypeStruct((GROUP, GROUP), F32), jax.ShapeDtypeStruct((4 * SUB, SUB), F32),
                   jax.ShapeDtypeStruct((SUB, GROUP), F32)],
        scratch_shapes=[pltpu.VMEM((tile + HALO, 3 * GROUP), F32), pltpu.VMEM((tile + HALO, 3 * GROUP), F32),
                        pltpu.VMEM((tile, D_MODEL), F32), pltpu.VMEM((tile, D_MODEL), F32),
                        pltpu.VMEM((N_SUMS, 8, GROUP), F32)], ride=ride)


def _norm_bwd(x, dz, dx_next, gs, w_in_b, tile, ride=None, blocks=None, begun=None, finish=True):
    n_tok = x.shape[0]
    first, n_tiles = blocks or (0, n_tok // tile)
    n_in = 5 + (2 if begun else 0)

    def body(*refs):
        x_ref, dz_ref, dxn_ref, gs_ref, w_ref = refs[:5]
        dx_ref = refs[n_in]
        acc_ref = refs[-1]
        i = pl.program_id(0)

        @pl.when(i == 0)
        def _():
            acc_ref[...] = refs[6][...] if begun else jnp.zeros_like(acc_ref)

        dh = lax.dot_general(dz_ref[...], w_ref[...], (((1,), (1,)), ((), ())), preferred_element_type=F32)
        xv = x_ref[...]
        r = lax.rsqrt(jnp.mean(xv * xv, axis=-1, keepdims=True) + EPS)
        xn = xv * r
        acc_ref[0] = acc_ref[0] + _row_sum8(dh)
        acc_ref[1] = acc_ref[1] + _row_sum8(dh * xn)
        dxn = dh * gs_ref[...]
        dx_ref[...] = dxn_ref[...] + r * (dxn - xn * jnp.mean(dxn * xn, axis=-1, keepdims=True))

        @pl.when(i == n_tiles - 1)
        def _():
            if finish:
                refs[n_in + 1][...] = jnp.sum(acc_ref[0], axis=0, keepdims=True)
                refs[n_in + 2][...] = jnp.sum(acc_ref[1], axis=0, keepdims=True)
            else:
                refs[n_in + 1][...] = acc_ref[...]

    def rows(i):
        return (first + i, 0)

    in_specs = [pl.BlockSpec((tile, D_MODEL), rows), pl.BlockSpec((tile, D_IN), rows), pl.BlockSpec((tile, D_MODEL), rows),
                _full((1, D_MODEL)), _full((D_MODEL, D_IN))]
    args = (x, dz, dx_next, gs, w_in_b)
    if begun:
        in_specs += [pl.BlockSpec(memory_space=pl.ANY), _full((2, 8, D_MODEL))]
        args += tuple(begun)
    vec = jax.ShapeDtypeStruct((1, D_MODEL), F32)
    return _tiled_call(
        body, args, name="norm_bwd", grid=(n_tiles,), in_specs=in_specs,
        out_specs=[pl.BlockSpec((tile, D_MODEL), rows)] + ([_full((1, D_MODEL))] * 2 if finish else [_full((2, 8, D_MODEL))]),
        out_shape=[jax.ShapeDtypeStruct((n_tok, D_MODEL), F32)]
        + ([vec, vec] if finish else [jax.ShapeDtypeStruct((2, 8, D_MODEL), F32)]),
        scratch_shapes=[pltpu.VMEM((2, 8, D_MODEL), F32)], ride=ride, aliases={5: 0} if begun else None)


def _tokens_matmul(a, b, name, out_dtype=F32, ride=None, a_cols=None):
    n_tok = a.shape[0]
    a_block, ka = a_cols or (0, a.shape[1])
    nb = b.shape[1]
    tk = min(REDUCE_TILE * (4 // b.dtype.itemsize), n_tok)
    cb = min(D_MODEL, nb)
    n_steps = n_tok // tk

    def body(a_ref, b_ref, o_ref, acc_ref):
        i = pl.program_id(1)

        @pl.when(i == 0)
        def _():
            acc_ref[...] = jnp.zeros_like(acc_ref)

        acc_ref[...] += lax.dot_general(a_ref[...], b_ref[...].astype(BF16), (((0,), (0,)), ((), ())),
                                        preferred_element_type=F32)

        @pl.when(i == n_steps - 1)
        def _():
            o_ref[...] = acc_ref[...].astype(out_dtype)

    (out,), rode = _tiled_call(
        body, (a, b), name=name, grid=(nb // cb, n_steps),
        in_specs=[pl.BlockSpec((tk, ka), lambda j, i: (i, a_block)), pl.BlockSpec((tk, cb), lambda j, i: (i, j))],
        out_specs=[pl.BlockSpec((ka, cb), lambda j, i: (0, j))],
        out_shape=[jax.ShapeDtypeStruct((ka, nb), out_dtype)],
        scratch_shapes=[pltpu.VMEM((ka, cb), F32)], ride=ride)
    return out, rode


def _out_proj_grads(m, w_out_b, gate):
    rb = 256
    n_blocks = D_MODEL // rb

    def body(m_ref, w_ref, gate_ref, dw_ref, dgate_ref, acc_ref):
        i = pl.program_id(0)

        @pl.when(i == 0)
        def _():
            acc_ref[...] = jnp.zeros_like(acc_ref)

        mv = m_ref[...]
        dw_ref[...] = (mv * gate_ref[...]).astype(BF16)
        acc_ref[...] += _row_sum8(mv * w_ref[...].astype(F32))

        @pl.when(i == n_blocks - 1)
        def _():
            dgate_ref[...] = jnp.sum(acc_ref[...], axis=0, keepdims=True)

    return pl.pallas_call(
        body, name="out_proj_grads", grid=(n_blocks,),
        in_specs=[pl.BlockSpec((rb, D_MODEL), lambda i: (i, 0)), pl.BlockSpec((rb, D_MODEL), lambda i: (i, 0)),
                  _full((1, D_MODEL))],
        out_specs=[pl.BlockSpec((rb, D_MODEL), lambda i: (i, 0)), _full((1, D_MODEL))],
        out_shape=[jax.ShapeDtypeStruct((D_MODEL, D_MODEL), BF16), jax.ShapeDtypeStruct((1, D_MODEL), F32)],
        scratch_shapes=[pltpu.VMEM((8, D_MODEL), F32)],
        compiler_params=_params(("arbitrary",)),
    )(m, w_out_b, gate)


def _modulation_columns(c_all, w_ada, b_cols):
    cols = w_ada.shape[2]

    def body(c_ref, w_ref, b_ref, ca_ref, mod_ref):
        ca, _ = _silu(c_ref[...])
        ca_ref[...] = ca
        for l in range(N_LAYERS):
            mod_ref[l] = jnp.dot(ca, w_ref[l], precision=lax.Precision.HIGHEST, preferred_element_type=F32) + b_ref[l:l + 1, :]

    return pl.pallas_call(
        body, name="modulation_columns",
        out_shape=[jax.ShapeDtypeStruct((N_DEV, D_MODEL), F32), jax.ShapeDtypeStruct((N_LAYERS, N_DEV, cols), F32)],
        compiler_params=pltpu.CompilerParams(vmem_limit_bytes=VMEM_LIMIT),
    )(c_all, w_ada, b_cols)


def _adam(w, g, m, v):
    m2 = ADAM_B1 * m + (1.0 - ADAM_B1) * g
    v2 = ADAM_B2 * v + (1.0 - ADAM_B2) * (g * g)
    m_hat = m2 / (1.0 - ADAM_B1 ** ADAM_STEP)
    v_hat = v2 / (1.0 - ADAM_B2 ** ADAM_STEP)
    return -ADAM_LR * (m_hat / (jnp.sqrt(v_hat) + ADAM_EPS) + ADAM_WD * w), m2, v2


def _row_block(rows, cols, slots):
    target = max(8, (1 << 19) // (cols * max(slots, 1)))
    rb = rows
    while rb > target and rb % 2 == 0 and (rb // 2) % 8 == 0:
        rb //= 2
    return rb


def _adam_update(w, g, m, v, name):
    rows, cols = w.shape
    slotted = g.ndim == 3
    rb = _row_block(rows, cols, N_DEV if slotted else 1)

    def body(w_ref, g_ref, m_ref, v_ref, go_ref, d_ref, mo_ref, vo_ref):
        if slotted:
            gv = g_ref[0].astype(F32)
            for q in range(1, N_DEV):
                gv = gv + g_ref[q].astype(F32)
        else:
            gv = g_ref[...]
        go_ref[...] = gv
        d_ref[...], mo_ref[...], vo_ref[...] = _adam(w_ref[...], gv, m_ref[...], v_ref[...])

    blk = pl.BlockSpec((rb, cols), lambda i: (i, 0))
    g_blk = pl.BlockSpec((N_DEV, rb, cols), lambda i: (0, i, 0)) if slotted else blk
    return pl.pallas_call(
        body, name=name, grid=(rows // rb,),
        in_specs=[blk, g_blk, blk, blk], out_specs=[blk] * 4,
        out_shape=[jax.ShapeDtypeStruct((rows, cols), F32)] * 4,
        compiler_params=_params(("parallel",)),
    )(w, g, m, v)


def _adam_many(ws, gs, ms, vs, name):
    n = len(ws)

    def body(*refs):
        w_refs, g_refs, m_refs, v_refs, d_refs, mo_refs, vo_refs = (refs[k * n:(k + 1) * n] for k in range(7))
        for j in range(n):
            d_refs[j][...], mo_refs[j][...], vo_refs[j][...] = _adam(w_refs[j][...], g_refs[j][...], m_refs[j][...],
                                                                  v_refs[j][...])

    res = pl.pallas_call(
        body, name=name, out_shape=[jax.ShapeDtypeStruct(w.shape, F32) for w in ws] * 3,
        compiler_params=pltpu.CompilerParams(vmem_limit_bytes=VMEM_LIMIT),
    )(*ws, *gs, *ms, *vs)
    return res[:n], res[n:2 * n], res[2 * n:]


def _as_rows(a):
    return a.reshape(-1, a.shape[-1]) if a.ndim > 1 else a.reshape(1, -1)


def _ada_update(ca_t, dmod_cols, w, m, v):
    _, rows, cols = w.shape

    def body(ca_ref, dm_ref, w_ref, m_ref, v_ref, g_ref, d_ref, mo_ref, vo_ref):
        g = ca_ref[:, 0:1] * dm_ref[0, 0:1, :]
        for b in range(1, N_DEV):
            g = g + ca_ref[:, b:b + 1] * dm_ref[0, b:b + 1, :]
        g_ref[0] = g
        d_ref[0], mo_ref[0], vo_ref[0] = _adam(w_ref[0], g, m_ref[0], v_ref[0])

    blk = pl.BlockSpec((1, rows, cols), lambda l: (l, 0, 0))
    return pl.pallas_call(
        body, name="ada_update", grid=(N_LAYERS,),
        in_specs=[_full((rows, N_DEV)), pl.BlockSpec((1, N_DEV, cols), lambda l: (l, 0, 0)), blk, blk, blk],
        out_specs=[blk] * 4, out_shape=[jax.ShapeDtypeStruct(w.shape, F32)] * 4,
        compiler_params=_params(("parallel",)),
    )(ca_t, dmod_cols, w, m, v)


def _exchange_sems(n):
    return [pltpu.SemaphoreType.DMA((n, N_DEV - 1)), pltpu.SemaphoreType.DMA((n, N_DEV - 1)),
            pltpu.SemaphoreType.DMA((n,))]


def _exchange_copies(plans, srcs, outs, sems, receiving, only=None):
    send_sems, recv_sems, local_sems = sems
    x, y, c = lax.axis_index("x"), lax.axis_index("y"), lax.axis_index("c")
    me = 4 * x + 2 * y + c

    def remote(i, k, incoming):
        _, o, send, land = plans[i]
        px = 1 - x if k & 4 else x
        py = 1 - y if k & 2 else y
        pc = 1 - c if k & 1 else c
        p = 4 * px + 2 * py + pc
        return pltpu.make_async_remote_copy(
            src_ref=send(srcs[i], p), dst_ref=land(outs[o], p if incoming else me),
            send_sem=send_sems.at[i, k - 1], recv_sem=recv_sems.at[i, k - 1],
            device_id=(px, py, pc), device_id_type=pl.DeviceIdType.MESH)

    which = range(len(plans)) if only is None else only
    pairs = [(i, k) for k in range(1, N_DEV) for i in which]
    local = [pltpu.make_async_copy(plans[i][2](srcs[i], me), plans[i][3](outs[plans[i][1]], me), local_sems.at[i])
             for i in which]
    return local, [remote(i, k, False) for i, k in pairs], [remote(i, k, True) for i, k in pairs] if receiving else []


def _exchange_start(plans, srcs, outs, sems, only=None):
    local, outgoing, _ = _exchange_copies(plans, srcs, outs, sems, False, only)
    for cp in local + outgoing:
        cp.start()


def _exchange_wait(plans, srcs, outs, sems, only=None):
    local, outgoing, incoming = _exchange_copies(plans, srcs, outs, sems, True, only)
    for cp in incoming:
        cp.wait_recv()
    for cp in outgoing:
        cp.wait_send()
    for cp in local:
        cp.wait()


def _exchange(name, ride):
    out_shapes, plans = ride
    n = len(plans)
    hbm = pl.BlockSpec(memory_space=pltpu.HBM)

    def body(*refs):
        srcs, outs, sems = refs[:n], refs[n:n + len(out_shapes)], refs[n + len(out_shapes):]
        _exchange_start(plans, srcs, outs, sems)
        _exchange_wait(plans, srcs, outs, sems)

    return pl.pallas_call(
        body, name=name, in_specs=[hbm] * n, out_specs=[hbm] * len(out_shapes), out_shape=list(out_shapes),
        scratch_shapes=_exchange_sems(n),
    )(*[p[0] for p in plans])


def _first_gather(c, w):
    rows, n = w.shape
    c_shapes, c_plans = _plans([c], [_gather])
    hbm = pl.BlockSpec(memory_space=pltpu.HBM)

    def body(c_ref, w_ref, c_all_ref, w_all_ref, send_sems, recv_sems, local_sem, *c_sems):
        x, y, core = lax.axis_index("x"), lax.axis_index("y"), lax.axis_index("c")
        me, sibling = (x, y, core), (x, y, 1 - core)
        chips = [(1 - x, y), (x, 1 - y), (1 - x, 1 - y)]

        def block(px, py, pc):
            return w_all_ref.at[:, pl.ds(pl.multiple_of((4 * px + 2 * py + pc) * n, n), n)]

        def copy(k, origin, to, src=None):
            return pltpu.make_async_remote_copy(
                src_ref=block(*origin) if src is None else src, dst_ref=block(*origin),
                send_sem=send_sems.at[k], recv_sem=recv_sems.at[k], device_id=to, device_id_type=pl.DeviceIdType.MESH)

        _exchange_start(c_plans, [c_ref], [c_all_ref], c_sems)
        mine = pltpu.make_async_copy(w_ref, block(*me), local_sem)
        mine.start()
        first = [copy(0, me, sibling, src=w_ref)]
        first += [copy(1 + j, me, (*chip, core), src=w_ref) for j, chip in enumerate(chips)]
        for cp in first:
            cp.start()
        passed = [copy(4 + j, (*chip, core), sibling) for j, chip in enumerate(chips)]
        for j, chip in enumerate(chips):
            copy(1 + j, (*chip, core), me).wait_recv()
            passed[j].start()
        copy(0, sibling, me).wait_recv()
        for j, chip in enumerate(chips):
            copy(4 + j, (*chip, 1 - core), me).wait_recv()
        for cp in first + passed:
            cp.wait_send()
        mine.wait()
        _exchange_wait(c_plans, [c_ref], [c_all_ref], c_sems)

    return pl.pallas_call(
        body, name="first_gather", in_specs=[hbm, hbm], out_specs=[hbm, hbm],
        out_shape=[c_shapes[0], jax.ShapeDtypeStruct((rows, N_DEV * n), w.dtype)],
        scratch_shapes=[pltpu.SemaphoreType.DMA((N_DEV - 1,)), pltpu.SemaphoreType.DMA((N_DEV - 1,)),
                        pltpu.SemaphoreType.DMA(()), *_exchange_sems(1)],
    )(c, w)


def _finish_exchange(big, big_rules, packed, dmod):
    n_rows = packed.shape[0]
    r = n_rows // N_DEV
    shapes, plans = _plans([*big, packed, dmod], [*big_rules, _scatter_rows, _gather])
    n_first = len(plans)
    i_small = n_first - 2
    _, send, land = _gather_rows(jax.ShapeDtypeStruct((r, 128), F32))
    plans = plans + [(None, len(shapes), send, land)]
    shapes = shapes + [jax.ShapeDtypeStruct((n_rows, 128), F32)]
    first = [i for i in range(n_first) if i != i_small]
    hbm = pl.BlockSpec(memory_space=pltpu.HBM)

    def body(*refs):
        srcs, outs = list(refs[:n_first]), refs[n_first:n_first + len(shapes)]
        parts_ref, sum_ref, local_sem = refs[n_first + len(shapes):n_first + len(shapes) + 3]
        sems = refs[n_first + len(shapes) + 3:]
        srcs.append(sum_ref)
        _exchange_start(plans, srcs, outs, sems, only=range(n_first))
        _exchange_wait(plans, srcs, outs, sems, only=[i_small])
        cp = pltpu.make_async_copy(outs[i_small], parts_ref, local_sem)
        cp.start()
        cp.wait()
        g = parts_ref[0]
        for q in range(1, N_DEV):
            g = g + parts_ref[q]
        sum_ref[...] = g
        _exchange_start(plans, srcs, outs, sems, only=[n_first])
        _exchange_wait(plans, srcs, outs, sems, only=[n_first])
        _exchange_wait(plans, srcs, outs, sems, only=first)

    res = pl.pallas_call(
        body, name="finish_exchange", in_specs=[hbm] * n_first, out_specs=[hbm] * len(shapes), out_shape=shapes,
        scratch_shapes=[pltpu.VMEM((N_DEV, r, 128), F32), pltpu.VMEM((r, 128), F32), pltpu.SemaphoreType.DMA(()),
                        *_exchange_sems(len(plans))],
    )(*big, packed, dmod)
    return (*res[:len(big)], res[-1], res[n_first - 1])


def _tiled_call(body, args, *, name, grid, in_specs, out_specs, out_shape, scratch_shapes=(), ride=None, aliases=None):
    params = _params(("arbitrary",) * len(grid))
    if ride is None:
        return pl.pallas_call(body, name=name, grid=grid, in_specs=in_specs, out_specs=out_specs, out_shape=out_shape,
                              scratch_shapes=list(scratch_shapes), input_output_aliases=aliases or {},
                              compiler_params=params)(*args), []
    shapes, plans = ride
    n_in, n_src, n_out, n_dst, n_scr = len(in_specs), len(plans), len(out_specs), len(shapes), len(scratch_shapes)
    hbm = pl.BlockSpec(memory_space=pltpu.HBM)

    def carrying(*refs):
        ins, srcs, refs = refs[:n_in], refs[n_in:n_in + n_src], refs[n_in + n_src:]
        outs, dsts, refs = refs[:n_out], refs[n_out:n_out + n_dst], refs[n_out + n_dst:]
        scratch, sems = refs[:n_scr], refs[n_scr:]
        ids = [pl.program_id(a) for a in range(len(grid))]
        first = functools.reduce(jnp.logical_and, [i == 0 for i in ids])
        last = functools.reduce(jnp.logical_and, [i == g - 1 for i, g in zip(ids, grid)])

        @pl.when(first)
        def _():
            _exchange_start(plans, srcs, dsts, sems)

        body(*ins, *outs, *scratch)

        @pl.when(last)
        def _():
            _exchange_wait(plans, srcs, dsts, sems)

    res = pl.pallas_call(
        carrying, name=name, grid=grid, in_specs=[*in_specs, *[hbm] * n_src], out_specs=[*out_specs, *[hbm] * n_dst],
        out_shape=[*out_shape, *shapes], scratch_shapes=[*scratch_shapes, *_exchange_sems(n_src)],
        input_output_aliases=aliases or {}, compiler_params=params)(*args, *[p[0] for p in plans])
    return res[:n_out], res[n_out:]


def _tail(nd, idx):
    return (slice(None),) * (nd - 2) + idx


def _gather(a):
    return jax.ShapeDtypeStruct((N_DEV,) + a.shape, a.dtype), lambda s, p: s, lambda o, q: o.at[q]


def _gather_rows(a):
    r = a.shape[-2]
    return (jax.ShapeDtypeStruct(a.shape[:-2] + (N_DEV * r, a.shape[-1]), a.dtype), lambda s, p: s,
            lambda o, q: o.at[_tail(a.ndim, (pl.ds(pl.multiple_of(q * r, r), r), slice(None)))])


def _gather_cols(a):
    c = a.shape[-1]
    return (jax.ShapeDtypeStruct(a.shape[:-1] + (N_DEV * c,), a.dtype), lambda s, p: s,
            lambda o, q: o.at[_tail(a.ndim, (slice(None), pl.ds(pl.multiple_of(q * c, c), c)))])


def _scatter_rows(a):
    r = a.shape[0] // N_DEV
    return (jax.ShapeDtypeStruct((N_DEV, r, a.shape[1]), a.dtype),
            lambda s, p: s.at[pl.ds(pl.multiple_of(p * r, r), r), :], lambda o, q: o.at[q])


def _scatter_cols(a):
    c = a.shape[1] // N_DEV
    return (jax.ShapeDtypeStruct((N_DEV, a.shape[0], c), a.dtype),
            lambda s, p: s.at[:, pl.ds(pl.multiple_of(p * c, c), c)], lambda o, q: o.at[q])


def _plans(arrays, rules):
    shapes, plans = [], []
    for o, (a, rule) in enumerate(zip(arrays, rules)):
        shape, send, land = rule(a)
        shapes.append(shape)
        plans.append((a, o, send, land))
    return shapes, plans


def _pack(pieces, rows_multiple=8):
    flat = []
    for a in pieces:
        f = a.reshape(-1)
        flat.append(jnp.pad(f, (0, (-f.shape[0]) % 128)))
    total = sum(f.shape[0] for f in flat)
    flat.append(jnp.zeros(((-total) % (128 * rows_multiple),), F32))
    return jnp.concatenate(flat).reshape(-1, 128)


def _unpack(buf, shapes, lead=()):
    flat = buf.reshape(lead + (-1,))
    out, off = [], 0
    for s in shapes:
        n = math.prod(s)
        out.append(flat[..., off:off + n].reshape(lead + tuple(s)))
        off += n + (-n) % 128
    return out


def _pad_rows(a, rows):
    return jnp.pad(a, ((0, rows - a.shape[0]), (0, 0)))


VEC_NAMES = ('pool_scale', 'b_dw_c', 'ln_g_c', 'ln_b_c', 'b_pw2_c', 'ln_g_d', 'ln_b_d')
GATHERED = ('w_in', 'w_out', 'w_pw2_c', 'w_conv_a', 'w_dw_c')
GATHER_RULES = (_gather_cols, _gather_rows, _gather_rows, _gather, _gather)
SCATTER_RULES = (_scatter_cols, _scatter_rows, _scatter_rows)


def _weight_shards(shard, l):
    return [shard[n][l].astype(BF16) if n in ('w_in', 'w_out') else shard[n][l] for n in GATHERED]


def _layer_weights(shard, l, gathered):
    w_in_b, w_out_b, w_pw2, wconv_parts, wdw_parts = gathered
    wconv = wconv_parts.transpose(1, 0, 2).reshape(CONV_A, GROUP)
    wdw = wdw_parts.transpose(1, 0, 2).reshape(CONV_C, GROUP)
    wp = jnp.einsum('gcd,gh->gchd', shard['w_pool'][l], jnp.eye(4, dtype=F32)).reshape(GROUP, GROUP)
    ws = shard['w_s_d'][l] * jnp.tril(jnp.ones((SUB, SUB), F32))
    vec = jnp.stack([shard[n][l] for n in VEC_NAMES])
    small = (_pad_rows(wconv, 8), _pad_rows(wdw, HALO), _pad_rows(vec, 16), wp.astype(BF16), w_pw2.astype(BF16),
             ws.reshape(4 * SUB, SUB).astype(BF16), jnp.repeat(shard['b_s_d'][l].T, 64, axis=1))
    small_t = (wp.T.astype(BF16), w_pw2.T.astype(BF16), ws.transpose(0, 2, 1).reshape(4 * SUB, SUB).astype(BF16))
    return w_in_b, w_out_b, small, small_t


def kernel(x, c, norm_g, w_ada, b_ada, w_in, w_conv_a, w_pool, pool_scale, w_dw_c, b_dw_c, ln_g_c, ln_b_c, w_pw2_c, b_pw2_c, ln_g_d, ln_b_d, w_s_d, b_s_d, w_out, final_g, loss_target, m_norm_g, m_w_ada, m_b_ada, m_w_in, m_w_conv_a, m_w_pool, m_pool_scale, m_w_dw_c, m_b_dw_c, m_ln_g_c, m_ln_b_c, m_w_pw2_c, m_b_pw2_c, m_ln_g_d, m_ln_b_d, m_w_s_d, m_b_s_d, m_w_out, m_final_g, v_norm_g, v_w_ada, v_b_ada, v_w_in, v_w_conv_a, v_w_pool, v_pool_scale, v_w_dw_c, v_b_dw_c, v_ln_g_c, v_ln_b_c, v_w_pw2_c, v_b_pw2_c, v_ln_g_d, v_ln_b_d, v_w_s_d, v_b_s_d, v_w_out, v_final_g):
    given = dict(locals())
    shard = {n: given[n] for n in WEIGHTS}
    mom_m = {n: given['m_' + n] for n in WEIGHTS}
    mom_v = {n: given['v_' + n] for n in WEIGHTS}
    me = 4 * lax.axis_index("x") + 2 * lax.axis_index("y") + lax.axis_index("c")
    n_tok = x.shape[1]
    tile = min(TOKEN_TILE, n_tok)
    wide_tile = min(2 * TOKEN_TILE, n_tok)
    x0 = x.reshape(n_tok, D_MODEL)
    target = loss_target.reshape(n_tok, D_MODEL)
    ada_cols = w_ada.shape[2]

    first_shards = _weight_shards(shard, 0)
    c_all, w_in_first = _first_gather(c, first_shards[0])

    b_cols = lax.dynamic_slice_in_dim(b_ada, me * ada_cols, ada_cols, axis=1)
    c_act, mod_cols = _modulation_columns(c_all.reshape(N_DEV, D_MODEL), w_ada, b_cols)
    (mod_all,) = _exchange("gather_modulation", _plans([mod_cols], [_gather]))
    mod = lax.dynamic_index_in_dim(mod_all, me, axis=2, keepdims=False)
    mod = mod.transpose(1, 0, 2).reshape(N_LAYERS, 3 * D_MODEL)
    shift, scale, gate = (mod[:, k * D_MODEL:(k + 1) * D_MODEL].reshape(N_LAYERS, 1, D_MODEL) for k in range(3))
    gs = norm_g.reshape(N_LAYERS, 1, D_MODEL) * (1.0 + scale)

    xs, hs, zs, ocs, layers = [x0], [], [], [], []
    for l in range(N_LAYERS):
        if l == 0:
            (h, z), rest = _in_proj(xs[0], gs[0], shift[0], w_in_first, wide_tile,
                                    ride=_plans(first_shards[1:], GATHER_RULES[1:]))
            layers.append(_layer_weights(shard, 0, [w_in_first, *rest]))
        else:
            (h, z), _ = _in_proj(xs[l], gs[l], shift[l], layers[l][0], wide_tile)
        _, w_out_b, small, _ = layers[l]
        hs.append(h)
        zs.append(z)
        if l + 1 < N_LAYERS:
            (x_next, o_c), gathered = _mix_out(z, xs[l], gate[l], small, w_out_b, tile,
                                               ride=_plans(_weight_shards(shard, l + 1), GATHER_RULES))
            xs.append(x_next)
            layers.append(_layer_weights(shard, l + 1, gathered))
        else:
            (dx, o_c, loss_part, dfinal_g), _ = _mix_out(z, xs[l], gate[l], small, w_out_b, tile,
                                                         head=(final_g.reshape(1, D_MODEL), target))
        ocs.append(o_c)

    part = {}
    layer_parts = [None] * N_LAYERS
    slots = [None] * N_LAYERS
    for l in reversed(range(N_LAYERS)):
        w_in_b, w_out_b, small, small_t = layers[l]
        ride = _plans(layer_parts[l + 1]['big'], SCATTER_RULES) if l + 1 < N_LAYERS else None
        (dz, ycat, sums, dwp, dw2, dws, dbs), rode = _mix_bwd(zs[l], ocs[l], dx, gate[l], small, small_t, w_out_b, tile,
                                                              ride=ride)
        if ride:
            slots[l + 1] = rode
        m_out, _ = _tokens_matmul(ycat, dx, "out_proj_tokens_matmul")
        dw_out, dgate = _out_proj_grads(m_out, w_out_b, gate[l])
        if l > 0:
            dw_in, _ = _tokens_matmul(hs[l], dz, "in_proj_tokens_matmul", out_dtype=BF16)
        else:
            dw_in, (slots_out, slots_pw2) = _tokens_matmul(
                hs[l], dz, "in_proj_tokens_matmul", out_dtype=BF16, a_cols=(0, D_MODEL // 2),
                ride=_plans([dw_out, dw2], SCATTER_RULES[1:]))
            dw_in_last, (slots_in,) = _tokens_matmul(
                hs[l], dz, "in_proj_tokens_matmul", out_dtype=BF16, a_cols=(1, D_MODEL // 2),
                ride=_plans([dw_in], SCATTER_RULES[:1]))
            slots[l] = [slots_in, slots_out, slots_pw2]
        (dx, dshift, dgs), _ = _norm_bwd(xs[l], dz, dx, gs[l], w_in_b, tile)
        layer_parts[l] = dict(
            big=[dw_in, dw_out, dw2],
            b_ada=jnp.concatenate([dshift, dgs * norm_g[l][None], dgate], axis=1)[0],
            norm_g=(dgs * (1.0 + scale[l]))[0],
            w_conv_a=sums[S_WCONV:S_WCONV + CONV_A], w_dw_c=sums[S_WDW:S_WDW + CONV_C],
            pool_scale=sums[S_PSCALE], b_dw_c=sums[S_BDW], ln_g_c=sums[S_LNGC], ln_b_c=sums[S_LNBC],
            b_pw2_c=sums[S_BPW2], ln_g_d=sums[S_LNGD], ln_b_d=sums[S_LNBD],
            w_pool=jnp.einsum('gchd,gh->gcd', dwp.reshape(4, 64, 4, 64), jnp.eye(4, dtype=F32)),
            w_s_d=dws.reshape(4, SUB, SUB) * jnp.tril(jnp.ones((SUB, SUB), F32)),
            b_s_d=dbs.reshape(SUB, 4, 64).sum(axis=-1).T)
    grad_x = dx.reshape(x.shape)
    for n in REPLICATED + CHANNEL_SHARDED:
        part[n] = dfinal_g[0] if n == 'final_g' else jnp.stack([layer_parts[l][n] for l in range(N_LAYERS)])

    small_names = REPLICATED + CHANNEL_SHARDED
    small_shapes = [part[n].shape for n in small_names] + [(1, 128)]
    slots_in_last, small_sum, dmod_all = _finish_exchange(
        [dw_in_last], SCATTER_RULES[:1],
        _pack([part[n] for n in small_names] + [loss_part[0:1]], rows_multiple=8 * N_DEV), part['b_ada'])

    grads, deltas, new_m, new_v = {}, {}, {}, {}
    half = D_MODEL // 2
    for j, n in enumerate(('w_in', 'w_out', 'w_pw2_c')):
        outs = [_adam_update(shard[n][l], slots[l][j], mom_m[n][l], mom_v[n][l], "update_" + n)
                for l in range(1, N_LAYERS)]
        if n == 'w_in':
            halves = [_adam_update(shard[n][0][rows], s, mom_m[n][0][rows], mom_v[n][0][rows], "update_" + n)
                      for rows, s in ((slice(0, half), slots[0][0]), (slice(half, None), slots_in_last))]
            outs.insert(0, [jnp.concatenate(o) for o in zip(*halves)])
        else:
            outs.insert(0, _adam_update(shard[n][0], slots[0][j], mom_m[n][0], mom_v[n][0], "update_" + n))
        grads[n], deltas[n], new_m[n], new_v[n] = (jnp.stack(o) for o in zip(*outs))

    *small_sums, loss_sum = _unpack(small_sum, small_shapes)
    loss = loss_sum[0, 0]
    gsum = dict(zip(small_names, small_sums))
    for n in CHANNEL_SHARDED:
        width = shard[n].shape[2]
        gsum[n] = lax.dynamic_slice_in_dim(gsum[n], me * width, width, axis=2)
    d_small, m_small, v_small = _adam_many(*[[_as_rows(d[n]) for n in small_names] for d in (shard, gsum, mom_m, mom_v)],
                                           "update_small")
    for j, n in enumerate(small_names):
        grads[n] = gsum[n]
        deltas[n], new_m[n], new_v[n] = (o[j].reshape(shard[n].shape) for o in (d_small, m_small, v_small))

    dmod_cols = lax.dynamic_slice_in_dim(dmod_all, me * ada_cols, ada_cols, axis=2).transpose(1, 0, 2)
    grads['w_ada'], deltas['w_ada'], new_m['w_ada'], new_v['w_ada'] = _ada_update(
        c_act.T, dmod_cols, w_ada, m_w_ada, v_w_ada)

    return (loss, grad_x, *[grads[n] for n in WEIGHTS], *[deltas[n] for n in WEIGHTS],
            *[new_m[n] for n in WEIGHTS], *[new_v[n] for n in WEIGHTS])
```

```python
import functools
import math

import jax
import jax.numpy as jnp
from jax import lax
from jax.experimental import pallas as pl
from jax.experimental.pallas import tpu as pltpu

F32 = jnp.float32
BF16 = jnp.bfloat16

N_DEV = 8
D_MODEL = 1024
GROUP = 256
D_IN = 12 * GROUP
N_LAYERS = 2
HALO = 32
SUB = 128
WIN = SUB + HALO
TOKEN_TILE = 512
REDUCE_TILE = 2048
EPS = 1e-6
VMEM_BYTES_V7X = 64 * 1024 * 1024
VMEM_LIMIT = VMEM_BYTES_V7X - 8 * 1024 * 1024

ADAM_LR = 0.001
ADAM_B1 = 0.9
ADAM_B2 = 0.999
ADAM_EPS = 1e-08
ADAM_WD = 0.01
ADAM_STEP = 10

A_B, A_C, A_X, A_G, B_P, B_G, C_A, C_GL, C_G, D_U, D_V, D_G = range(12)
V_PSCALE, V_BDW, V_LNGC, V_LNBC, V_BPW2, V_LNGD, V_LNBD = range(7)
S_WCONV, S_PSCALE, S_BDW, S_LNGC, S_LNBC, S_BPW2, S_LNGD, S_LNBD, S_WDW = 0, 3, 4, 5, 6, 7, 8, 9, 16
N_SUMS = 64
CONV_A = 3
CONV_C = 31

WEIGHTS = ('norm_g', 'w_ada', 'b_ada', 'w_in', 'w_conv_a', 'w_pool', 'pool_scale', 'w_dw_c', 'b_dw_c', 'ln_g_c',
           'ln_b_c', 'w_pw2_c', 'b_pw2_c', 'ln_g_d', 'ln_b_d', 'w_s_d', 'b_s_d', 'w_out', 'final_g')
REPLICATED = ('norm_g', 'b_ada', 'w_pool', 'pool_scale', 'b_dw_c', 'ln_g_c', 'ln_b_c', 'b_pw2_c', 'ln_g_d', 'ln_b_d',
              'w_s_d', 'b_s_d', 'final_g')
CHANNEL_SHARDED = ('w_conv_a', 'w_dw_c')


def _params(semantics, vmem=VMEM_LIMIT):
    return pltpu.CompilerParams(dimension_semantics=semantics, vmem_limit_bytes=vmem)


def _cols(g):
    return slice(g * GROUP, (g + 1) * GROUP)


def _full(shape):
    return pl.BlockSpec(shape, lambda *_: (0,) * len(shape))


def _silu(x):
    s = jax.nn.sigmoid(x)
    return x * s, s


def _dsilu(x, s):
    return s * (1.0 + x * (1.0 - s))


_GELU_C0 = math.sqrt(2.0 / math.pi)
_GELU_C1 = 0.044715


def _gelu(x):
    th = jnp.tanh(_GELU_C0 * (x + _GELU_C1 * (x * x * x)))
    return 0.5 * x * (1.0 + th), th


def _dgelu(x, th):
    return 0.5 * (1.0 + th) + 0.5 * x * (1.0 - th * th) * (_GELU_C0 * (1.0 + 3.0 * _GELU_C1 * (x * x)))


def _layer_norm(x):
    mu = jnp.mean(x, axis=-1, keepdims=True)
    xc = x - mu
    rstd = lax.rsqrt(jnp.mean(xc * xc, axis=-1, keepdims=True) + EPS)
    return xc * rstd, rstd


def _layer_norm_bwd(dn, n, rstd):
    return rstd * (dn - jnp.mean(dn, axis=-1, keepdims=True) - n * jnp.mean(dn * n, axis=-1, keepdims=True))


def _shift_rows(a, k):
    k = k % a.shape[0]
    return a if k == 0 else pltpu.roll(a, k, 0)


def _row_sum8(a):
    s = a[0:8]
    for m in range(1, a.shape[0] // 8):
        s = s + a[8 * m:8 * m + 8]
    return s


def _lane():
    return lax.broadcasted_iota(jnp.int32, (SUB, GROUP), 1)


def _by_quarter(lane, parts):
    return jnp.where(lane < 64, parts[0], jnp.where(lane < 128, parts[1], jnp.where(lane < 192, parts[2], parts[3])))


def _conv_inputs(z_ref, rows):
    def f(g):
        return z_ref[rows, _cols(g)].astype(F32)
    return f(A_C) * f(A_X), f(B_P), f(C_A) * jax.nn.sigmoid(f(C_GL))


def _fill_past(past_ref, zh_ref, zm_ref, is_first, tile):
    parts = _conv_inputs(zh_ref, slice(None))
    for n, a in enumerate(parts):
        past_ref[0:HALO, _cols(n)] = jnp.where(is_first, 0.0, a)

    def body(j, carry):
        r0 = pl.multiple_of(j * SUB, SUB)
        for n, a in enumerate(_conv_inputs(zm_ref, pl.ds(r0, SUB))):
            past_ref[pl.ds(r0 + HALO, SUB), _cols(n)] = a
        return carry

    lax.fori_loop(0, tile // SUB, body, 0)


def _short_conv_taps(qw):
    return [_shift_rows(qw, CONV_A - 1 - k)[HALO:WIN] for k in range(CONV_A)]


def _window_sums(pw, lane):
    s2 = pw + _shift_rows(pw, 1)
    s4 = s2 + _shift_rows(s2, 2)
    s8 = s4 + _shift_rows(s4, 4)
    s16 = s8 + _shift_rows(s8, 8)
    return _by_quarter(lane, [s[HALO:WIN] for s in (s2, s4, s8, s16)])


def _inv_count(lane, t_first):
    width = _by_quarter(lane, [2.0, 4.0, 8.0, 16.0])
    t = lax.broadcasted_iota(jnp.int32, (SUB, GROUP), 0) + t_first
    return 1.0 / jnp.minimum((t + 1).astype(F32), width)


def _forward_window_sums(ew, lane):
    n = ew.shape[0]
    f2 = ew + _shift_rows(ew, n - 1)
    f4 = f2 + _shift_rows(f2, n - 2)
    f8 = f4 + _shift_rows(f4, n - 4)
    f16 = f8 + _shift_rows(f8, n - 8)
    return _by_quarter(lane, [f[0:SUB] for f in (f2, f4, f8, f16)])


def _mixer_forwards(zc, win, t_first, wc_ref, wdw_ref, vec_ref, wp_ref, w2_ref, ws_ref, bs_ref, o_c=None):
    def vec(n):
        return vec_ref[n:n + 1, :]

    def short_conv():
        taps = _short_conv_taps(win(0))
        o_a = wc_ref[0:1, :] * taps[0] + wc_ref[1:2, :] * taps[1] + wc_ref[2:3, :] * taps[2]
        a_b, a_g = zc(A_B), zc(A_G)
        sg_a, s_a = _silu(a_g)
        return a_b * o_a * sg_a, dict(taps=taps, o_a=o_a, a_b=a_b, a_g=a_g, sg_a=sg_a, s_a=s_a)

    def pooling():
        lane = _lane()
        pw = win(1)
        ic = _inv_count(lane, t_first)
        pooled_b = (_window_sums(pw, lane) * ic - pw[HALO:WIN]).astype(BF16)
        y0_b = jnp.dot(pooled_b, wp_ref[...], preferred_element_type=F32)
        b_g = zc(B_G)
        sg_b, s_b = _silu(b_g)
        return y0_b * vec(V_PSCALE) * sg_b, dict(ic=ic, pooled_b=pooled_b, y0_b=y0_b, b_g=b_g, sg_b=sg_b, s_b=s_b)

    def conformer():
        hw = win(2)
        o = o_c
        if o is None:
            o = wdw_ref[CONV_C - 1:CONV_C, :] * hw[HALO:WIN] + vec(V_BDW)
            for k in range(CONV_C - 1):
                o = o + wdw_ref[k:k + 1, :] * _shift_rows(hw, CONV_C - 1 - k)[HALO:WIN]
        n_c, rstd_c = _layer_norm(o)
        ln_c = n_c * vec(V_LNGC) + vec(V_LNBC)
        sl_c, ssl_c = _silu(ln_c)
        sl_b = sl_c.astype(BF16)
        yc = jnp.dot(sl_b, w2_ref[...], preferred_element_type=F32) + vec(V_BPW2)
        c_g = zc(C_G)
        sg_c, s_c = _silu(c_g)
        return yc * sg_c, dict(hw=hw, o_c=o, n_c=n_c, rstd_c=rstd_c, ln_c=ln_c, ssl_c=ssl_c, sl_b=sl_b, yc=yc, c_g=c_g,
                               sg_c=sg_c, s_c=s_c)

    def gating():
        lane = _lane()
        d_u, d_v, d_g = zc(D_U), zc(D_V), zc(D_G)
        u, th_u = _gelu(d_u)
        gv, th_v = _gelu(d_v)
        n_d, rstd_d = _layer_norm(gv)
        v_b = (n_d * vec(V_LNGD) + vec(V_LNBD)).astype(BF16)
        r = jnp.dot(ws_ref[...], v_b, preferred_element_type=F32)
        mixed = _by_quarter(lane, [r[h * SUB:(h + 1) * SUB] for h in range(4)]) + bs_ref[...]
        sg_d, s_d = _silu(d_g)
        return u * mixed * sg_d, dict(d_u=d_u, d_v=d_v, d_g=d_g, u=u, th_u=th_u, th_v=th_v, n_d=n_d, rstd_d=rstd_d,
                                      v_b=v_b, mixed=mixed, sg_d=sg_d, s_d=s_d)

    return short_conv, pooling, conformer, gating


def _in_proj(x, gs, shift, w_in_b, tile, ride=None):
    n_tok = x.shape[0]

    def body(x_ref, gs_ref, sh_ref, w_ref, h_ref, z_ref):
        xv = x_ref[...]
        r = lax.rsqrt(jnp.mean(xv * xv, axis=-1, keepdims=True) + EPS)
        h = ((xv * r) * gs_ref[...] + sh_ref[...]).astype(BF16)
        h_ref[...] = h
        for j in range(D_IN // D_MODEL):
            cs = slice(j * D_MODEL, (j + 1) * D_MODEL)
            z_ref[:, cs] = jnp.dot(h, w_ref[:, cs], preferred_element_type=F32).astype(BF16)

    return _tiled_call(
        body, (x, gs, shift, w_in_b), name="in_proj", grid=(n_tok // tile,),
        in_specs=[pl.BlockSpec((tile, D_MODEL), lambda i: (i, 0)), _full((1, D_MODEL)), _full((1, D_MODEL)),
                  _full((D_MODEL, D_IN))],
        out_specs=[pl.BlockSpec((tile, D_MODEL), lambda i: (i, 0)), pl.BlockSpec((tile, D_IN), lambda i: (i, 0))],
        out_shape=[jax.ShapeDtypeStruct((n_tok, D_MODEL), BF16), jax.ShapeDtypeStruct((n_tok, D_IN), BF16)],
        ride=ride)


def _small_specs(with_transposes):
    specs = [_full((8, GROUP)), _full((HALO, GROUP)), _full((16, GROUP)), _full((GROUP, GROUP)), _full((GROUP, GROUP)),
             _full((4 * SUB, SUB)), _full((SUB, GROUP))]
    if with_transposes:
        specs += [_full((GROUP, GROUP)), _full((GROUP, GROUP)), _full((4 * SUB, SUB))]
    return specs


def _mix_out(z, x, gate, small, w_out_b, tile, ride=None, head=None):
    n_tok = x.shape[0]
    n_tiles = n_tok // tile
    n_sub = tile // SUB
    cw = D_MODEL // n_sub
    per_halo = tile // HALO
    n_in = 12 + (2 if head else 0)
    n_out = 4 if head else 2

    def cur(i):
        return jnp.minimum(i, n_tiles - 1)

    def prev(i):
        return jnp.maximum(i - 1, 0)

    def body(*refs):
        (zm_ref, zh_ref, x_ref, gate_ref, wc_ref, wdw_ref, vec_ref, wp_ref, w2_ref, ws_ref, bs_ref, wout_ref) = refs[:12]
        xo_ref, oc_ref = refs[n_in:n_in + 2]
        past_ref, ycat_ref, ycat_prev_ref = refs[n_in + n_out:n_in + n_out + 3]
        i = pl.program_id(0)
        t = cur(i)
        if head:
            g_ref, tgt_ref = refs[12:14]
            loss_ref, dg_ref = refs[n_in + 2:n_in + 4]
            xn_ref, acc_ref = refs[n_in + n_out + 3:]
        else:
            xn_ref = xo_ref

        @pl.when(i == 0)
        def _():
            ycat_prev_ref[...] = jnp.zeros_like(ycat_prev_ref)
            if head:
                acc_ref[...] = jnp.zeros_like(acc_ref)

        _fill_past(past_ref, zh_ref, zm_ref, t == 0, tile)
        for j in range(n_sub):
            cs = slice(j * cw, (j + 1) * cw)
            y = jnp.dot(ycat_prev_ref[...], wout_ref[:, cs], preferred_element_type=F32)
            xn_ref[:, cs] = x_ref[:, cs] + gate_ref[:, cs] * y
            rows = slice(j * SUB, (j + 1) * SUB)
            mixers = _mixer_forwards(
                lambda g: zm_ref[rows, _cols(g)].astype(F32), lambda n: past_ref[j * SUB:j * SUB + WIN, _cols(n)],
                t * tile + j * SUB, wc_ref, wdw_ref, vec_ref, wp_ref, w2_ref, ws_ref, bs_ref)
            for n, mixer in enumerate(mixers):
                y, s = mixer()
                ycat_ref[rows, _cols(n)] = y.astype(BF16)
                if "o_c" in s:
                    oc_ref[rows, :] = s["o_c"]
        ycat_prev_ref[...] = ycat_ref[...]
        if head:
            counted = jnp.where(i > 0, 1.0, 0.0)
            xo_ref[...] = _loss_head_block(xn_ref[...], g_ref[...], tgt_ref[...], acc_ref, counted)

            @pl.when(i == n_tiles)
            def _():
                loss_ref[...] = jnp.full((8, 128), 0.5 / D_MODEL, F32) * jnp.sum(acc_ref[0])
                dg_ref[...] = jnp.sum(acc_ref[1], axis=0, keepdims=True)

    in_specs = [pl.BlockSpec((tile, D_IN), lambda i: (cur(i), 0)),
                pl.BlockSpec((HALO, D_IN), lambda i: (jnp.maximum(cur(i) * per_halo - 1, 0), 0)),
                pl.BlockSpec((tile, D_MODEL), lambda i: (prev(i), 0)), _full((1, D_MODEL)),
                *_small_specs(False), _full((D_MODEL, D_MODEL))]
    out_specs = [pl.BlockSpec((tile, D_MODEL), lambda i: (prev(i), 0)), pl.BlockSpec((tile, GROUP), lambda i: (cur(i), 0))]
    out_shape = [jax.ShapeDtypeStruct((n_tok, D_MODEL), F32), jax.ShapeDtypeStruct((n_tok, GROUP), F32)]
    scratch = [pltpu.VMEM((tile + HALO, 3 * GROUP), F32), pltpu.VMEM((tile, D_MODEL), BF16), pltpu.VMEM((tile, D_MODEL), BF16)]
    args = (z, z, x, gate, *small, w_out_b)
    if head:
        in_specs += [_full((1, D_MODEL)), pl.BlockSpec((tile, D_MODEL), lambda i: (prev(i), 0))]
        out_specs += [_full((8, 128)), _full((1, D_MODEL))]
        out_shape += [jax.ShapeDtypeStruct((8, 128), F32), jax.ShapeDtypeStruct((1, D_MODEL), F32)]
        scratch += [pltpu.VMEM((tile, D_MODEL), F32), pltpu.VMEM((2, 8, D_MODEL), F32)]
        args += tuple(head)
    outs, rode = _tiled_call(body, args, name="mix_out", grid=(n_tiles + 1,), in_specs=in_specs, out_specs=out_specs,
                             out_shape=out_shape, scratch_shapes=scratch, ride=ride)
    return outs, rode


def _layer_fwd(x, gs, shift, w_in_b, gate, small, w_out_b, tile, ride=None, head=None):
    n_tok = x.shape[0]
    n_tiles = n_tok // tile
    n_sub = tile // SUB
    cw = D_MODEL // n_sub
    groups_per_step = D_IN // GROUP // n_sub
    n_in = 14 + (2 if head else 0)
    n_out = 6 if head else 4

    def of_in(i):
        return jnp.minimum(i, n_tiles - 1)

    def of_mix(i):
        return jnp.clip(i - 1, 0, n_tiles - 1)

    def of_out(i):
        return jnp.clip(i - 2, 0, n_tiles - 1)

    def body(*refs):
        (x_ref, xres_ref, gs_ref, sh_ref, win_ref, gate_ref, wc_ref, wdw_ref, vec_ref, wp_ref, w2_ref, ws_ref, bs_ref,
         wout_ref) = refs[:14]
        h_ref, z_ref, xo_ref, oc_ref = refs[n_in:n_in + 4]
        zcur_ref, past_ref, ycat_ref, ycat_prev_ref = refs[n_in + n_out:n_in + n_out + 4]
        i = pl.program_id(0)
        t = of_mix(i)
        if head:
            g_ref, tgt_ref = refs[14:16]
            loss_ref, dg_ref = refs[n_in + 4:n_in + 6]
            xn_ref, acc_ref = refs[n_in + n_out + 4:]
        else:
            xn_ref = xo_ref

        @pl.when(i == 0)
        def _():
            zcur_ref[...] = jnp.zeros_like(zcur_ref)
            ycat_prev_ref[...] = jnp.zeros_like(ycat_prev_ref)
            past_ref[...] = jnp.zeros_like(past_ref)
            if head:
                acc_ref[...] = jnp.zeros_like(acc_ref)

        past_ref[0:HALO, :] = jnp.where(i <= 1, 0.0, past_ref[0:HALO, :])

        def fill(j, carry):
            r0 = pl.multiple_of(j * SUB, SUB)
            for n, a in enumerate(_conv_inputs(zcur_ref, pl.ds(r0, SUB))):
                past_ref[pl.ds(r0 + HALO, SUB), _cols(n)] = a
            return carry

        lax.fori_loop(0, n_sub, fill, 0)

        xv = x_ref[...]
        r = lax.rsqrt(jnp.mean(xv * xv, axis=-1, keepdims=True) + EPS)
        h_ref[...] = ((xv * r) * gs_ref[...] + sh_ref[...]).astype(BF16)
        for j in range(n_sub):
            cs = slice(j * cw, (j + 1) * cw)
            y = jnp.dot(ycat_prev_ref[...], wout_ref[:, cs], preferred_element_type=F32)
            xn_ref[:, cs] = xres_ref[:, cs] + gate_ref[:, cs] * y
            for g in range(j * groups_per_step, (j + 1) * groups_per_step):
                z_ref[:, _cols(g)] = jnp.dot(h_ref[...], win_ref[:, _cols(g)], preferred_element_type=F32).astype(BF16)
            rows = slice(j * SUB, (j + 1) * SUB)
            mixers = _mixer_forwards(
                lambda g: zcur_ref[rows, _cols(g)].astype(F32), lambda n: past_ref[j * SUB:j * SUB + WIN, _cols(n)],
                t * tile + j * SUB, wc_ref, wdw_ref, vec_ref, wp_ref, w2_ref, ws_ref, bs_ref)
            for n, mixer in enumerate(mixers):
                y, s = mixer()
                ycat_ref[rows, _cols(n)] = y.astype(BF16)
                if "o_c" in s:
                    oc_ref[rows, :] = s["o_c"]
        ycat_prev_ref[...] = ycat_ref[...]

        @pl.when(i < n_tiles)
        def _():
            past_ref[0:HALO, :] = past_ref[tile:tile + HALO, :]
            zcur_ref[...] = z_ref[...]

        if head:
            counted = jnp.where(i >= 2, 1.0, 0.0)
            xo_ref[...] = _loss_head_block(xn_ref[...], g_ref[...], tgt_ref[...], acc_ref, counted)

            @pl.when(i == n_tiles + 1)
            def _():
                loss_ref[...] = jnp.full((8, 128), 0.5 / D_MODEL, F32) * jnp.sum(acc_ref[0])
                dg_ref[...] = jnp.sum(acc_ref[1], axis=0, keepdims=True)

    in_specs = [pl.BlockSpec((tile, D_MODEL), lambda i: (of_in(i), 0)), pl.BlockSpec((tile, D_MODEL), lambda i: (of_out(i), 0)),
                _full((1, D_MODEL)), _full((1, D_MODEL)), _full((D_MODEL, D_IN)), _full((1, D_MODEL)),
                *_small_specs(False), _full((D_MODEL, D_MODEL))]
    out_specs = [pl.BlockSpec((tile, D_MODEL), lambda i: (of_in(i), 0)), pl.BlockSpec((tile, D_IN), lambda i: (of_in(i), 0)),
                 pl.BlockSpec((tile, D_MODEL), lambda i: (of_out(i), 0)), pl.BlockSpec((tile, GROUP), lambda i: (of_mix(i), 0))]
    out_shape = [jax.ShapeDtypeStruct((n_tok, D_MODEL), BF16), jax.ShapeDtypeStruct((n_tok, D_IN), BF16),
                 jax.ShapeDtypeStruct((n_tok, D_MODEL), F32), jax.ShapeDtypeStruct((n_tok, GROUP), F32)]
    scratch = [pltpu.VMEM((tile, D_IN), BF16), pltpu.VMEM((tile + HALO, 3 * GROUP), F32),
               pltpu.VMEM((tile, D_MODEL), BF16), pltpu.VMEM((tile, D_MODEL), BF16)]
    args = (x, x, gs, shift, w_in_b, gate, *small, w_out_b)
    if head:
        in_specs += [_full((1, D_MODEL)), pl.BlockSpec((tile, D_MODEL), lambda i: (of_out(i), 0))]
        out_specs += [_full((8, 128)), _full((1, D_MODEL))]
        out_shape += [jax.ShapeDtypeStruct((8, 128), F32), jax.ShapeDtypeStruct((1, D_MODEL), F32)]
        scratch += [pltpu.VMEM((tile, D_MODEL), F32), pltpu.VMEM((2, 8, D_MODEL), F32)]
        args += tuple(head)
    return _tiled_call(body, args, name="layer_fwd", grid=(n_tiles + 2,), in_specs=in_specs, out_specs=out_specs,
                       out_shape=out_shape, scratch_shapes=scratch, ride=ride)


def _loss_head_block(xv, g, target, acc_ref, counted):
    r = lax.rsqrt(jnp.mean(xv * xv, axis=-1, keepdims=True) + EPS)
    xn = xv * r
    err = xn * g - target
    acc_ref[0] = acc_ref[0] + counted * _row_sum8(err * err)
    dy = err * (1.0 / D_MODEL)
    acc_ref[1] = acc_ref[1] + counted * _row_sum8(dy * xn)
    a = dy * g
    return r * (a - xn * jnp.mean(a * xn, axis=-1, keepdims=True))


def _mix_bwd(z, o_c, dx_next, gate, small, small_t, w_out_b, tile, ride=None):
    n_tok = z.shape[0]
    n_tiles = n_tok // tile
    n_sub = tile // SUB
    cw = D_MODEL // n_sub
    per_halo = tile // HALO
    nt_dims = (((1,), (1,)), ((), ()))

    def tile_of(i):
        return n_tiles - 1 - i

    def next_tile_of(i):
        return jnp.maximum(n_tiles - 2 - i, 0)

    def body(zm_ref, zh_ref, oc_ref, dxn_ref, dxn_next_ref, gate_ref, wc_ref, wdw_ref, vec_ref, wp_ref, w2_ref, ws_ref,
             bs_ref, wpt_ref, w2t_ref, wst_ref, wout_ref,
             dz_ref, ycat_ref, sums_ref, dwp_ref, dw2_ref, dws_ref, dbs_ref,
             past_ref, future_ref, dy_ref, dy_next_ref, acc_ref):
        i = pl.program_id(0)
        t = tile_of(i)

        @pl.when(i == 0)
        def _():
            acc_ref[...] = jnp.zeros_like(acc_ref)
            dwp_ref[...] = jnp.zeros_like(dwp_ref)
            dw2_ref[...] = jnp.zeros_like(dw2_ref)
            dws_ref[...] = jnp.zeros_like(dws_ref)
            dbs_ref[...] = jnp.zeros_like(dbs_ref)
            future_ref[tile:tile + HALO, :] = jnp.zeros((HALO, 3 * GROUP), F32)
            dy_ref[...] = lax.dot_general((dxn_ref[...] * gate_ref[...]).astype(BF16), wout_ref[...], nt_dims,
                                        preferred_element_type=F32)

        _fill_past(past_ref, zh_ref, zm_ref, t == 0, tile)
        dyb_next = (dxn_next_ref[...] * gate_ref[...]).astype(BF16)

        def vec(n):
            return vec_ref[n:n + 1, :]

        for jj in range(n_sub):
            j = n_sub - 1 - jj
            r0 = j * SUB
            rows = slice(r0, r0 + SUB)

            def zc(g):
                return zm_ref[rows, _cols(g)].astype(F32)

            def add(n, a):
                acc_ref[n] = acc_ref[n] + _row_sum8(a)

            def put(g, a):
                dz_ref[rows, _cols(g)] = a.astype(BF16)

            def future_window(n, a):
                future_ref[rows, _cols(n)] = a
                return future_ref[r0:r0 + WIN, _cols(n)]

            short_conv, pooling, conformer, gating = _mixer_forwards(
                zc, lambda n: past_ref[r0:r0 + WIN, _cols(n)], t * tile + r0,
                wc_ref, wdw_ref, vec_ref, wp_ref, w2_ref, ws_ref, bs_ref, o_c=oc_ref[rows, :])
            lane = _lane()
            ks = slice(jj * cw, (jj + 1) * cw)
            dy_next_ref[:, ks] = lax.dot_general(dyb_next, wout_ref[ks, :], nt_dims, preferred_element_type=F32)

            y, s = short_conv()
            ycat_ref[rows, _cols(0)] = y.astype(BF16)
            dy = dy_ref[rows,_cols(0)]
            put(A_B, dy * s["o_a"] * s["sg_a"])
            put(A_G, dy * s["a_b"] * s["o_a"] * _dsilu(s["a_g"], s["s_a"]))
            do = dy * s["a_b"] * s["sg_a"]
            for k in range(CONV_A):
                add(S_WCONV + k, do * s["taps"][k])
            dow = future_window(0, do)
            dq = wc_ref[CONV_A - 1:CONV_A, :] * dow[0:SUB]
            for k in range(CONV_A - 1):
                dq = dq + wc_ref[k:k + 1, :] * _shift_rows(dow, WIN - (CONV_A - 1 - k))[0:SUB]
            put(A_C, dq * zc(A_X))
            put(A_X, dq * zc(A_C))

            y, s = pooling()
            ycat_ref[rows, _cols(1)] = y.astype(BF16)
            dy = dy_ref[rows,_cols(1)]
            put(B_G, dy * (s["y0_b"] * vec(V_PSCALE)) * _dsilu(s["b_g"], s["s_b"]))
            dyb = dy * s["sg_b"]
            add(S_PSCALE, dyb * s["y0_b"])
            dpw_b = (dyb * vec(V_PSCALE)).astype(BF16)
            dwp_ref[...] += lax.dot_general(s["pooled_b"], dpw_b, (((0,), (0,)), ((), ())), preferred_element_type=F32)
            dpooled = jnp.dot(dpw_b, wpt_ref[...], preferred_element_type=F32)
            ew = future_window(1, dpooled * s["ic"])
            put(B_P, _forward_window_sums(ew, lane) - dpooled)

            y, s = conformer()
            ycat_ref[rows, _cols(2)] = y.astype(BF16)
            dy = dy_ref[rows,_cols(2)]
            put(C_G, dy * s["yc"] * _dsilu(s["c_g"], s["s_c"]))
            dyc = dy * s["sg_c"]
            add(S_BPW2, dyc)
            dyc_b = dyc.astype(BF16)
            dw2_ref[...] += lax.dot_general(s["sl_b"], dyc_b, (((0,), (0,)), ((), ())), preferred_element_type=F32)
            dln = jnp.dot(dyc_b, w2t_ref[...], preferred_element_type=F32) * _dsilu(s["ln_c"], s["ssl_c"])
            add(S_LNGC, dln * s["n_c"])
            add(S_LNBC, dln)
            do = _layer_norm_bwd(dln * vec(V_LNGC), s["n_c"], s["rstd_c"])
            add(S_BDW, do)
            hw = s["hw"]
            for k in range(CONV_C):
                add(S_WDW + k, do * _shift_rows(hw, CONV_C - 1 - k)[HALO:WIN])
            dow = future_window(2, do)
            dhc = wdw_ref[CONV_C - 1:CONV_C, :] * dow[0:SUB]
            for k in range(CONV_C - 1):
                dhc = dhc + wdw_ref[k:k + 1, :] * _shift_rows(dow, WIN - (CONV_C - 1 - k))[0:SUB]
            c_a = zc(C_A)
            sgl = jax.nn.sigmoid(zc(C_GL))
            put(C_A, dhc * sgl)
            put(C_GL, dhc * c_a * sgl * (1.0 - sgl))

            y, s = gating()
            ycat_ref[rows, _cols(3)] = y.astype(BF16)
            dy = dy_ref[rows,_cols(3)]
            put(D_G, dy * s["u"] * s["mixed"] * _dsilu(s["d_g"], s["s_d"]))
            put(D_U, dy * s["mixed"] * s["sg_d"] * _dgelu(s["d_u"], s["th_u"]))
            dmixed = dy * s["u"] * s["sg_d"]
            dbs_ref[...] += dmixed
            by_head = jnp.concatenate(
                [jnp.where((lane >= 64 * h) & (lane < 64 * h + 64), dmixed, 0.0) for h in range(4)], axis=0).astype(BF16)
            dws_ref[...] += lax.dot_general(by_head, s["v_b"], (((1,), (1,)), ((), ())), preferred_element_type=F32)
            rv = jnp.dot(wst_ref[...], dmixed.astype(BF16), preferred_element_type=F32)
            dv = _by_quarter(lane, [rv[h * SUB:(h + 1) * SUB] for h in range(4)])
            add(S_LNGD, dv * s["n_d"])
            add(S_LNBD, dv)
            dgv = _layer_norm_bwd(dv * vec(V_LNGD), s["n_d"], s["rstd_d"])
            put(D_V, dgv * _dgelu(s["d_v"], s["th_v"]))

        future_ref[tile:tile + HALO, :] = future_ref[0:HALO, :]
        dy_ref[...] = dy_next_ref[...]

        @pl.when(i == n_tiles - 1)
        def _():
            for n in range(N_SUMS):
                sums_ref[n:n + 1, :] = jnp.sum(acc_ref[n], axis=0, keepdims=True)

    return _tiled_call(
        body, (z, z, o_c, dx_next, dx_next, gate, *small, *small_t, w_out_b), name="mix_bwd", grid=(n_tiles,),
        in_specs=[pl.BlockSpec((tile, D_IN), lambda i: (tile_of(i), 0)),
                  pl.BlockSpec((HALO, D_IN), lambda i: (jnp.maximum(tile_of(i) * per_halo - 1, 0), 0)),
                  pl.BlockSpec((tile, GROUP), lambda i: (tile_of(i), 0)),
                  pl.BlockSpec((tile, D_MODEL), lambda i: (tile_of(i), 0)),
                  pl.BlockSpec((tile, D_MODEL), lambda i: (next_tile_of(i), 0)), _full((1, D_MODEL)),
                  *_small_specs(True), _full((D_MODEL, D_MODEL))],
        out_specs=[pl.BlockSpec((tile, D_IN), lambda i: (tile_of(i), 0)),
                   pl.BlockSpec((tile, D_MODEL), lambda i: (tile_of(i), 0)),
                   _full((N_SUMS, GROUP)), _full((GROUP, GROUP)), _full((GROUP, GROUP)), _full((4 * SUB, SUB)),
                   _full((SUB, GROUP))],
        out_shape=[jax.ShapeDtypeStruct((n_tok, D_IN), BF16), jax.ShapeDtypeStruct((n_tok, D_MODEL), BF16),
                   jax.ShapeDtypeStruct((N_SUMS, GROUP), F32), jax.ShapeDtypeStruct((GROUP, GROUP), F32),
                   jax.ShapeDtypeStruct((GROUP, GROUP), F32), jax.ShapeDtypeStruct((4 * SUB, SUB), F32),
                   jax.ShapeDtypeStruct((SUB, GROUP), F32)],
        scratch_shapes=[pltpu.VMEM((tile + HALO, 3 * GROUP), F32), pltpu.VMEM((tile + HALO, 3 * GROUP), F32),
                        pltpu.VMEM((tile, D_MODEL), F32), pltpu.VMEM((tile, D_MODEL), F32),
                        pltpu.VMEM((N_SUMS, 8, GROUP), F32)], ride=ride)


def _norm_bwd(x, dz, dx_next, gs, w_in_b, tile, ride=None, blocks=None, begun=None, finish=True):
    n_tok = x.shape[0]
    first, n_tiles = blocks or (0, n_tok // tile)
    n_in = 5 + (2 if begun else 0)

    def body(*refs):
        x_ref, dz_ref, dxn_ref, gs_ref, w_ref = refs[:5]
        dx_ref = refs[n_in]
        acc_ref = refs[-1]
        i = pl.program_id(0)

        @pl.when(i == 0)
        def _():
            acc_ref[...] = refs[6][...] if begun else jnp.zeros_like(acc_ref)

        dh = lax.dot_general(dz_ref[...], w_ref[...], (((1,), (1,)), ((), ())), preferred_element_type=F32)
        xv = x_ref[...]
        r = lax.rsqrt(jnp.mean(xv * xv, axis=-1, keepdims=True) + EPS)
        xn = xv * r
        acc_ref[0] = acc_ref[0] + _row_sum8(dh)
        acc_ref[1] = acc_ref[1] + _row_sum8(dh * xn)
        dxn = dh * gs_ref[...]
        dx_ref[...] = dxn_ref[...] + r * (dxn - xn * jnp.mean(dxn * xn, axis=-1, keepdims=True))

        @pl.when(i == n_tiles - 1)
        def _():
            if finish:
                refs[n_in + 1][...] = jnp.sum(acc_ref[0], axis=0, keepdims=True)
                refs[n_in + 2][...] = jnp.sum(acc_ref[1], axis=0, keepdims=True)
            else:
                refs[n_in + 1][...] = acc_ref[...]

    def rows(i):
        return (first + i, 0)

    in_specs = [pl.BlockSpec((tile, D_MODEL), rows), pl.BlockSpec((tile, D_IN), rows), pl.BlockSpec((tile, D_MODEL), rows),
                _full((1, D_MODEL)), _full((D_MODEL, D_IN))]
    args = (x, dz, dx_next, gs, w_in_b)
    if begun:
        in_specs += [pl.BlockSpec(memory_space=pl.ANY), _full((2, 8, D_MODEL))]
        args += tuple(begun)
    vec = jax.ShapeDtypeStruct((1, D_MODEL), F32)
    return _tiled_call(
        body, args, name="norm_bwd", grid=(n_tiles,), in_specs=in_specs,
        out_specs=[pl.BlockSpec((tile, D_MODEL), rows)] + ([_full((1, D_MODEL))] * 2 if finish else [_full((2, 8, D_MODEL))]),
        out_shape=[jax.ShapeDtypeStruct((n_tok, D_MODEL), F32)]
        + ([vec, vec] if finish else [jax.ShapeDtypeStruct((2, 8, D_MODEL), F32)]),
        scratch_shapes=[pltpu.VMEM((2, 8, D_MODEL), F32)], ride=ride, aliases={5: 0} if begun else None)


def _tokens_matmul(a, b, name, out_dtype=F32, ride=None, a_cols=None):
    n_tok = a.shape[0]
    a_block, ka = a_cols or (0, a.shape[1])
    nb = b.shape[1]
    tk = min(REDUCE_TILE * (4 // b.dtype.itemsize), n_tok)
    cb = min(D_MODEL, nb)
    n_steps = n_tok // tk

    def body(a_ref, b_ref, o_ref, acc_ref):
        i = pl.program_id(1)

        @pl.when(i == 0)
        def _():
            acc_ref[...] = jnp.zeros_like(acc_ref)

        acc_ref[...] += lax.dot_general(a_ref[...], b_ref[...].astype(BF16), (((0,), (0,)), ((), ())),
                                        preferred_element_type=F32)

        @pl.when(i == n_steps - 1)
        def _():
            o_ref[...] = acc_ref[...].astype(out_dtype)

    (out,), rode = _tiled_call(
        body, (a, b), name=name, grid=(nb // cb, n_steps),
        in_specs=[pl.BlockSpec((tk, ka), lambda j, i: (i, a_block)), pl.BlockSpec((tk, cb), lambda j, i: (i, j))],
        out_specs=[pl.BlockSpec((ka, cb), lambda j, i: (0, j))],
        out_shape=[jax.ShapeDtypeStruct((ka, nb), out_dtype)],
        scratch_shapes=[pltpu.VMEM((ka, cb), F32)], ride=ride)
    return out, rode


def _out_proj_grads(m, w_out_b, gate):
    rb = 256
    n_blocks = D_MODEL // rb

    def body(m_ref, w_ref, gate_ref, dw_ref, dgate_ref, acc_ref):
        i = pl.program_id(0)

        @pl.when(i == 0)
        def _():
            acc_ref[...] = jnp.zeros_like(acc_ref)

        mv = m_ref[...]
        dw_ref[...] = (mv * gate_ref[...]).astype(BF16)
        acc_ref[...] += _row_sum8(mv * w_ref[...].astype(F32))

        @pl.when(i == n_blocks - 1)
        def _():
            dgate_ref[...] = jnp.sum(acc_ref[...], axis=0, keepdims=True)

    return pl.pallas_call(
        body, name="out_proj_grads", grid=(n_blocks,),
        in_specs=[pl.BlockSpec((rb, D_MODEL), lambda i: (i, 0)), pl.BlockSpec((rb, D_MODEL), lambda i: (i, 0)),
                  _full((1, D_MODEL))],
        out_specs=[pl.BlockSpec((rb, D_MODEL), lambda i: (i, 0)), _full((1, D_MODEL))],
        out_shape=[jax.ShapeDtypeStruct((D_MODEL, D_MODEL), BF16), jax.ShapeDtypeStruct((1, D_MODEL), F32)],
        scratch_shapes=[pltpu.VMEM((8, D_MODEL), F32)],
        compiler_params=_params(("arbitrary",)),
    )(m, w_out_b, gate)


def _modulation_columns(c_all, w_ada, b_cols):
    cols = w_ada.shape[2]

    def body(c_ref, w_ref, b_ref, ca_ref, mod_ref):
        ca, _ = _silu(c_ref[...])
        ca_ref[...] = ca
        for l in range(N_LAYERS):
            mod_ref[l] = jnp.dot(ca, w_ref[l], precision=lax.Precision.HIGHEST, preferred_element_type=F32) + b_ref[l:l + 1, :]

    return pl.pallas_call(
        body, name="modulation_columns",
        out_shape=[jax.ShapeDtypeStruct((N_DEV, D_MODEL), F32), jax.ShapeDtypeStruct((N_LAYERS, N_DEV, cols), F32)],
        compiler_params=pltpu.CompilerParams(vmem_limit_bytes=VMEM_LIMIT),
    )(c_all, w_ada, b_cols)


def _adam(w, g, m, v):
    m2 = ADAM_B1 * m + (1.0 - ADAM_B1) * g
    v2 = ADAM_B2 * v + (1.0 - ADAM_B2) * (g * g)
    m_hat = m2 / (1.0 - ADAM_B1 ** ADAM_STEP)
    v_hat = v2 / (1.0 - ADAM_B2 ** ADAM_STEP)
    return -ADAM_LR * (m_hat / (jnp.sqrt(v_hat) + ADAM_EPS) + ADAM_WD * w), m2, v2


def _row_block(rows, cols, slots):
    target = max(8, (1 << 19) // (cols * max(slots, 1)))
    rb = rows
    while rb > target and rb % 2 == 0 and (rb // 2) % 8 == 0:
        rb //= 2
    return rb


def _adam_update(w, g, m, v, name):
    rows, cols = w.shape
    slotted = g.ndim == 3
    rb = _row_block(rows, cols, N_DEV if slotted else 1)

    def body(w_ref, g_ref, m_ref, v_ref, go_ref, d_ref, mo_ref, vo_ref):
        if slotted:
            gv = g_ref[0].astype(F32)
            for q in range(1, N_DEV):
                gv = gv + g_ref[q].astype(F32)
        else:
            gv = g_ref[...]
        go_ref[...] = gv
        d_ref[...], mo_ref[...], vo_ref[...] = _adam(w_ref[...], gv, m_ref[...], v_ref[...])

    blk = pl.BlockSpec((rb, cols), lambda i: (i, 0))
    g_blk = pl.BlockSpec((N_DEV, rb, cols), lambda i: (0, i, 0)) if slotted else blk
    return pl.pallas_call(
        body, name=name, grid=(rows // rb,),
        in_specs=[blk, g_blk, blk, blk], out_specs=[blk] * 4,
        out_shape=[jax.ShapeDtypeStruct((rows, cols), F32)] * 4,
        compiler_params=_params(("parallel",)),
    )(w, g, m, v)


def _adam_many(ws, gs, ms, vs, name):
    n = len(ws)

    def body(*refs):
        w_refs, g_refs, m_refs, v_refs, d_refs, mo_refs, vo_refs = (refs[k * n:(k + 1) * n] for k in range(7))
        for j in range(n):
            d_refs[j][...], mo_refs[j][...], vo_refs[j][...] = _adam(w_refs[j][...], g_refs[j][...], m_refs[j][...],
                                                                  v_refs[j][...])

    res = pl.pallas_call(
        body, name=name, out_shape=[jax.ShapeDtypeStruct(w.shape, F32) for w in ws] * 3,
        compiler_params=pltpu.CompilerParams(vmem_limit_bytes=VMEM_LIMIT),
    )(*ws, *gs, *ms, *vs)
    return res[:n], res[n:2 * n], res[2 * n:]


def _as_rows(a):
    return a.reshape(-1, a.shape[-1]) if a.ndim > 1 else a.reshape(1, -1)


def _ada_update(ca_t, dmod_cols, w, m, v):
    _, rows, cols = w.shape

    def body(ca_ref, dm_ref, w_ref, m_ref, v_ref, g_ref, d_ref, mo_ref, vo_ref):
        g = ca_ref[:, 0:1] * dm_ref[0, 0:1, :]
        for b in range(1, N_DEV):
            g = g + ca_ref[:, b:b + 1] * dm_ref[0, b:b + 1, :]
        g_ref[0] = g
        d_ref[0], mo_ref[0], vo_ref[0] = _adam(w_ref[0], g, m_ref[0], v_ref[0])

    blk = pl.BlockSpec((1, rows, cols), lambda l: (l, 0, 0))
    return pl.pallas_call(
        body, name="ada_update", grid=(N_LAYERS,),
        in_specs=[_full((rows, N_DEV)), pl.BlockSpec((1, N_DEV, cols), lambda l: (l, 0, 0)), blk, blk, blk],
        out_specs=[blk] * 4, out_shape=[jax.ShapeDtypeStruct(w.shape, F32)] * 4,
        compiler_params=_params(("parallel",)),
    )(ca_t, dmod_cols, w, m, v)


def _exchange_sems(n):
    return [pltpu.SemaphoreType.DMA((n, N_DEV - 1)), pltpu.SemaphoreType.DMA((n, N_DEV - 1)),
            pltpu.SemaphoreType.DMA((n,))]


def _exchange_copies(plans, srcs, outs, sems, receiving, only=None):
    send_sems, recv_sems, local_sems = sems
    x, y, c = lax.axis_index("x"), lax.axis_index("y"), lax.axis_index("c")
    me = 4 * x + 2 * y + c

    def remote(i, k, incoming):
        _, o, send, land = plans[i]
        px = 1 - x if k & 4 else x
        py = 1 - y if k & 2 else y
        pc = 1 - c if k & 1 else c
        p = 4 * px + 2 * py + pc
        return pltpu.make_async_remote_copy(
            src_ref=send(srcs[i], p), dst_ref=land(outs[o], p if incoming else me),
            send_sem=send_sems.at[i, k - 1], recv_sem=recv_sems.at[i, k - 1],
            device_id=(px, py, pc), device_id_type=pl.DeviceIdType.MESH)

    which = range(len(plans)) if only is None else only
    pairs = [(i, k) for k in range(1, N_DEV) for i in which]
    local = [pltpu.make_async_copy(plans[i][2](srcs[i], me), plans[i][3](outs[plans[i][1]], me), local_sems.at[i])
             for i in which]
    return local, [remote(i, k, False) for i, k in pairs], [remote(i, k, True) for i, k in pairs] if receiving else []


def _exchange_start(plans, srcs, outs, sems, only=None):
    local, outgoing, _ = _exchange_copies(plans, srcs, outs, sems, False, only)
    for cp in local + outgoing:
        cp.start()


def _exchange_wait(plans, srcs, outs, sems, only=None):
    local, outgoing, incoming = _exchange_copies(plans, srcs, outs, sems, True, only)
    for cp in incoming:
        cp.wait_recv()
    for cp in outgoing:
        cp.wait_send()
    for cp in local:
        cp.wait()


def _exchange(name, ride):
    out_shapes, plans = ride
    n = len(plans)
    hbm = pl.BlockSpec(memory_space=pltpu.HBM)

    def body(*refs):
        srcs, outs, sems = refs[:n], refs[n:n + len(out_shapes)], refs[n + len(out_shapes):]
        _exchange_start(plans, srcs, outs, sems)
        _exchange_wait(plans, srcs, outs, sems)

    return pl.pallas_call(
        body, name=name, in_specs=[hbm] * n, out_specs=[hbm] * len(out_shapes), out_shape=list(out_shapes),
        scratch_shapes=_exchange_sems(n),
    )(*[p[0] for p in plans])


def _first_gather(c, arrays, rules):
    n = len(arrays)
    c_shapes, c_plans = _plans([c], [_gather])
    shapes, lands = zip(*[(shape, land) for shape, _, land in (rule(a) for a, rule in zip(arrays, rules))])
    hbm = pl.BlockSpec(memory_space=pltpu.HBM)

    def body(*refs):
        c_ref, srcs, c_all_ref, outs = refs[0], refs[1:1 + n], refs[1 + n], refs[2 + n:2 + 2 * n]
        send_sems, recv_sems, local_sems = refs[2 + 2 * n:5 + 2 * n]
        c_sems = refs[5 + 2 * n:]
        x, y, core = lax.axis_index("x"), lax.axis_index("y"), lax.axis_index("c")
        me, sibling = (x, y, core), (x, y, 1 - core)
        chips = [(1 - x, y), (x, 1 - y), (1 - x, 1 - y)]

        def block(a, px, py, pc):
            return lands[a](outs[a], 4 * px + 2 * py + pc)

        def copy(a, k, origin, to, own=False):
            return pltpu.make_async_remote_copy(
                src_ref=srcs[a] if own else block(a, *origin), dst_ref=block(a, *origin),
                send_sem=send_sems.at[a, k], recv_sem=recv_sems.at[a, k], device_id=to,
                device_id_type=pl.DeviceIdType.MESH)

        _exchange_start(c_plans, [c_ref], [c_all_ref], c_sems)
        mine = [pltpu.make_async_copy(srcs[a], block(a, *me), local_sems.at[a]) for a in range(n)]
        first = [copy(a, 0, me, sibling, own=True) for a in range(n)]
        first += [copy(a, 1 + j, me, (*chip, core), own=True) for j, chip in enumerate(chips) for a in range(n)]
        for cp in mine + first:
            cp.start()
        passed = [[copy(a, 4 + j, (*chip, core), sibling) for a in range(n)] for j, chip in enumerate(chips)]
        for j, chip in enumerate(chips):
            for a in range(n):
                copy(a, 1 + j, (*chip, core), me).wait_recv()
                passed[j][a].start()
        for a in range(n):
            copy(a, 0, sibling, me).wait_recv()
        for j, chip in enumerate(chips):
            for a in range(n):
                copy(a, 4 + j, (*chip, 1 - core), me).wait_recv()
        for cp in first + [cp for row in passed for cp in row]:
            cp.wait_send()
        for cp in mine:
            cp.wait()
        _exchange_wait(c_plans, [c_ref], [c_all_ref], c_sems)

    return pl.pallas_call(
        body, name="first_gather", in_specs=[hbm] * (1 + n), out_specs=[hbm] * (1 + n),
        out_shape=[c_shapes[0], *shapes],
        scratch_shapes=[pltpu.SemaphoreType.DMA((n, N_DEV - 1)), pltpu.SemaphoreType.DMA((n, N_DEV - 1)),
                        pltpu.SemaphoreType.DMA((n,)), *_exchange_sems(1)],
    )(c, *arrays)


def _finish_exchange(big, big_rules, packed, dmod):
    n_rows = packed.shape[0]
    r = n_rows // N_DEV
    shapes, plans = _plans([*big, packed, dmod], [*big_rules, _scatter_rows, _gather])
    n_first = len(plans)
    i_small = n_first - 2
    _, send, land = _gather_rows(jax.ShapeDtypeStruct((r, 128), F32))
    plans = plans + [(None, len(shapes), send, land)]
    shapes = shapes + [jax.ShapeDtypeStruct((n_rows, 128), F32)]
    first = [i for i in range(n_first) if i != i_small]
    hbm = pl.BlockSpec(memory_space=pltpu.HBM)

    def body(*refs):
        srcs, outs = list(refs[:n_first]), refs[n_first:n_first + len(shapes)]
        parts_ref, sum_ref, local_sem = refs[n_first + len(shapes):n_first + len(shapes) + 3]
        sems = refs[n_first + len(shapes) + 3:]
        srcs.append(sum_ref)
        _exchange_start(plans, srcs, outs, sems, only=range(n_first))
        _exchange_wait(plans, srcs, outs, sems, only=[i_small])
        cp = pltpu.make_async_copy(outs[i_small], parts_ref, local_sem)
        cp.start()
        cp.wait()
        g = parts_ref[0]
        for q in range(1, N_DEV):
            g = g + parts_ref[q]
        sum_ref[...] = g
        _exchange_start(plans, srcs, outs, sems, only=[n_first])
        _exchange_wait(plans, srcs, outs, sems, only=[n_first])
        _exchange_wait(plans, srcs, outs, sems, only=first)

    res = pl.pallas_call(
        body, name="finish_exchange", in_specs=[hbm] * n_first, out_specs=[hbm] * len(shapes), out_shape=shapes,
        scratch_shapes=[pltpu.VMEM((N_DEV, r, 128), F32), pltpu.VMEM((r, 128), F32), pltpu.SemaphoreType.DMA(()),
                        *_exchange_sems(len(plans))],
    )(*big, packed, dmod)
    return (*res[:len(big)], res[-1], res[n_first - 1])


def _tiled_call(body, args, *, name, grid, in_specs, out_specs, out_shape, scratch_shapes=(), ride=None, aliases=None):
    params = _params(("arbitrary",) * len(grid))
    if ride is None:
        return pl.pallas_call(body, name=name, grid=grid, in_specs=in_specs, out_specs=out_specs, out_shape=out_shape,
                              scratch_shapes=list(scratch_shapes), input_output_aliases=aliases or {},
                              compiler_params=params)(*args), []
    shapes, plans = ride
    n_in, n_src, n_out, n_dst, n_scr = len(in_specs), len(plans), len(out_specs), len(shapes), len(scratch_shapes)
    hbm = pl.BlockSpec(memory_space=pltpu.HBM)

    def carrying(*refs):
        ins, srcs, refs = refs[:n_in], refs[n_in:n_in + n_src], refs[n_in + n_src:]
        outs, dsts, refs = refs[:n_out], refs[n_out:n_out + n_dst], refs[n_out + n_dst:]
        scratch, sems = refs[:n_scr], refs[n_scr:]
        ids = [pl.program_id(a) for a in range(len(grid))]
        first = functools.reduce(jnp.logical_and, [i == 0 for i in ids])
        last = functools.reduce(jnp.logical_and, [i == g - 1 for i, g in zip(ids, grid)])

        @pl.when(first)
        def _():
            _exchange_start(plans, srcs, dsts, sems)

        body(*ins, *outs, *scratch)

        @pl.when(last)
        def _():
            _exchange_wait(plans, srcs, dsts, sems)

    res = pl.pallas_call(
        carrying, name=name, grid=grid, in_specs=[*in_specs, *[hbm] * n_src], out_specs=[*out_specs, *[hbm] * n_dst],
        out_shape=[*out_shape, *shapes], scratch_shapes=[*scratch_shapes, *_exchange_sems(n_src)],
        input_output_aliases=aliases or {}, compiler_params=params)(*args, *[p[0] for p in plans])
    return res[:n_out], res[n_out:]


def _tail(nd, idx):
    return (slice(None),) * (nd - 2) + idx


def _gather(a):
    return jax.ShapeDtypeStruct((N_DEV,) + a.shape, a.dtype), lambda s, p: s, lambda o, q: o.at[q]


def _gather_rows(a):
    r = a.shape[-2]
    return (jax.ShapeDtypeStruct(a.shape[:-2] + (N_DEV * r, a.shape[-1]), a.dtype), lambda s, p: s,
            lambda o, q: o.at[_tail(a.ndim, (pl.ds(pl.multiple_of(q * r, r), r), slice(None)))])


def _gather_cols(a):
    c = a.shape[-1]
    return (jax.ShapeDtypeStruct(a.shape[:-1] + (N_DEV * c,), a.dtype), lambda s, p: s,
            lambda o, q: o.at[_tail(a.ndim, (slice(None), pl.ds(pl.multiple_of(q * c, c), c)))])


def _scatter_rows(a):
    r = a.shape[0] // N_DEV
    return (jax.ShapeDtypeStruct((N_DEV, r, a.shape[1]), a.dtype),
            lambda s, p: s.at[pl.ds(pl.multiple_of(p * r, r), r), :], lambda o, q: o.at[q])


def _scatter_cols(a):
    c = a.shape[1] // N_DEV
    return (jax.ShapeDtypeStruct((N_DEV, a.shape[0], c), a.dtype),
            lambda s, p: s.at[:, pl.ds(pl.multiple_of(p * c, c), c)], lambda o, q: o.at[q])


def _plans(arrays, rules):
    shapes, plans = [], []
    for o, (a, rule) in enumerate(zip(arrays, rules)):
        shape, send, land = rule(a)
        shapes.append(shape)
        plans.append((a, o, send, land))
    return shapes, plans


def _pack(pieces, rows_multiple=8):
    flat = []
    for a in pieces:
        f = a.reshape(-1)
        flat.append(jnp.pad(f, (0, (-f.shape[0]) % 128)))
    total = sum(f.shape[0] for f in flat)
    flat.append(jnp.zeros(((-total) % (128 * rows_multiple),), F32))
    return jnp.concatenate(flat).reshape(-1, 128)


def _unpack(buf, shapes, lead=()):
    flat = buf.reshape(lead + (-1,))
    out, off = [], 0
    for s in shapes:
        n = math.prod(s)
        out.append(flat[..., off:off + n].reshape(lead + tuple(s)))
        off += n + (-n) % 128
    return out


def _pad_rows(a, rows):
    return jnp.pad(a, ((0, rows - a.shape[0]), (0, 0)))


VEC_NAMES = ('pool_scale', 'b_dw_c', 'ln_g_c', 'ln_b_c', 'b_pw2_c', 'ln_g_d', 'ln_b_d')
GATHERED = ('w_in', 'w_out', 'w_pw2_c', 'w_conv_a', 'w_dw_c')
GATHER_RULES = (_gather_cols, _gather_rows, _gather_rows, _gather, _gather)
SCATTER_RULES = (_scatter_cols, _scatter_rows, _scatter_rows)


def _weight_shards(shard, l):
    return [shard[n][l].astype(BF16) if n in ('w_in', 'w_out') else shard[n][l] for n in GATHERED]


def _layer_weights(shard, l, gathered):
    w_in_b, w_out_b, w_pw2, wconv_parts, wdw_parts = gathered
    wconv = wconv_parts.transpose(1, 0, 2).reshape(CONV_A, GROUP)
    wdw = wdw_parts.transpose(1, 0, 2).reshape(CONV_C, GROUP)
    wp = jnp.einsum('gcd,gh->gchd', shard['w_pool'][l], jnp.eye(4, dtype=F32)).reshape(GROUP, GROUP)
    ws = shard['w_s_d'][l] * jnp.tril(jnp.ones((SUB, SUB), F32))
    vec = jnp.stack([shard[n][l] for n in VEC_NAMES])
    small = (_pad_rows(wconv, 8), _pad_rows(wdw, HALO), _pad_rows(vec, 16), wp.astype(BF16), w_pw2.astype(BF16),
             ws.reshape(4 * SUB, SUB).astype(BF16), jnp.repeat(shard['b_s_d'][l].T, 64, axis=1))
    small_t = (wp.T.astype(BF16), w_pw2.T.astype(BF16), ws.transpose(0, 2, 1).reshape(4 * SUB, SUB).astype(BF16))
    return w_in_b, w_out_b, small, small_t


def kernel(x, c, norm_g, w_ada, b_ada, w_in, w_conv_a, w_pool, pool_scale, w_dw_c, b_dw_c, ln_g_c, ln_b_c, w_pw2_c, b_pw2_c, ln_g_d, ln_b_d, w_s_d, b_s_d, w_out, final_g, loss_target, m_norm_g, m_w_ada, m_b_ada, m_w_in, m_w_conv_a, m_w_pool, m_pool_scale, m_w_dw_c, m_b_dw_c, m_ln_g_c, m_ln_b_c, m_w_pw2_c, m_b_pw2_c, m_ln_g_d, m_ln_b_d, m_w_s_d, m_b_s_d, m_w_out, m_final_g, v_norm_g, v_w_ada, v_b_ada, v_w_in, v_w_conv_a, v_w_pool, v_pool_scale, v_w_dw_c, v_b_dw_c, v_ln_g_c, v_ln_b_c, v_w_pw2_c, v_b_pw2_c, v_ln_g_d, v_ln_b_d, v_w_s_d, v_b_s_d, v_w_out, v_final_g):
    given = dict(locals())
    shard = {n: given[n] for n in WEIGHTS}
    mom_m = {n: given['m_' + n] for n in WEIGHTS}
    mom_v = {n: given['v_' + n] for n in WEIGHTS}
    me = 4 * lax.axis_index("x") + 2 * lax.axis_index("y") + lax.axis_index("c")
    n_tok = x.shape[1]
    tile = min(TOKEN_TILE, n_tok)
    wide_tile = min(2 * TOKEN_TILE, n_tok)
    x0 = x.reshape(n_tok, D_MODEL)
    target = loss_target.reshape(n_tok, D_MODEL)
    ada_cols = w_ada.shape[2]

    c_all, *gathered = _first_gather(c, _weight_shards(shard, 0), GATHER_RULES)
    layers = [_layer_weights(shard, 0, gathered)]

    b_cols = lax.dynamic_slice_in_dim(b_ada, me * ada_cols, ada_cols, axis=1)
    c_act, mod_cols = _modulation_columns(c_all.reshape(N_DEV, D_MODEL), w_ada, b_cols)
    (mod_all,) = _exchange("gather_modulation", _plans([mod_cols], [_gather]))
    mod = lax.dynamic_index_in_dim(mod_all, me, axis=2, keepdims=False)
    mod = mod.transpose(1, 0, 2).reshape(N_LAYERS, 3 * D_MODEL)
    shift, scale, gate = (mod[:, k * D_MODEL:(k + 1) * D_MODEL].reshape(N_LAYERS, 1, D_MODEL) for k in range(3))
    gs = norm_g.reshape(N_LAYERS, 1, D_MODEL) * (1.0 + scale)

    xs, hs, zs, ocs = [x0], [], [], []
    for l in range(N_LAYERS):
        w_in_b, w_out_b, small, _ = layers[l]
        if l + 1 < N_LAYERS:
            (h, z, x_next, o_c), gathered = _layer_fwd(xs[l], gs[l], shift[l], w_in_b, gate[l], small, w_out_b, tile,
                                                       ride=_plans(_weight_shards(shard, l + 1), GATHER_RULES))
            xs.append(x_next)
            layers.append(_layer_weights(shard, l + 1, gathered))
        else:
            (h, z, dx, o_c, loss_part, dfinal_g), _ = _layer_fwd(xs[l], gs[l], shift[l], w_in_b, gate[l], small, w_out_b,
                                                                 tile, head=(final_g.reshape(1, D_MODEL), target))
        hs.append(h)
        zs.append(z)
        ocs.append(o_c)

    part = {}
    layer_parts = [None] * N_LAYERS
    slots = [None] * N_LAYERS
    for l in reversed(range(N_LAYERS)):
        w_in_b, w_out_b, small, small_t = layers[l]
        ride = _plans(layer_parts[l + 1]['big'], SCATTER_RULES) if l + 1 < N_LAYERS else None
        (dz, ycat, sums, dwp, dw2, dws, dbs), rode = _mix_bwd(zs[l], ocs[l], dx, gate[l], small, small_t, w_out_b, tile,
                                                              ride=ride)
        if ride:
            slots[l + 1] = rode
        m_out, _ = _tokens_matmul(ycat, dx, "out_proj_tokens_matmul")
        dw_out, dgate = _out_proj_grads(m_out, w_out_b, gate[l])
        if l > 0:
            dw_in, _ = _tokens_matmul(hs[l], dz, "in_proj_tokens_matmul", out_dtype=BF16)
        else:
            dw_in, (slots_out, slots_pw2) = _tokens_matmul(
                hs[l], dz, "in_proj_tokens_matmul", out_dtype=BF16, a_cols=(0, D_MODEL // 2),
                ride=_plans([dw_out, dw2], SCATTER_RULES[1:]))
            dw_in_last, (slots_in,) = _tokens_matmul(
                hs[l], dz, "in_proj_tokens_matmul", out_dtype=BF16, a_cols=(1, D_MODEL // 2),
                ride=_plans([dw_in], SCATTER_RULES[:1]))
            slots[l] = [slots_in, slots_out, slots_pw2]
        (dx, dshift, dgs), _ = _norm_bwd(xs[l], dz, dx, gs[l], w_in_b, tile)
        layer_parts[l] = dict(
            big=[dw_in, dw_out, dw2],
            b_ada=jnp.concatenate([dshift, dgs * norm_g[l][None], dgate], axis=1)[0],
            norm_g=(dgs * (1.0 + scale[l]))[0],
            w_conv_a=sums[S_WCONV:S_WCONV + CONV_A], w_dw_c=sums[S_WDW:S_WDW + CONV_C],
            pool_scale=sums[S_PSCALE], b_dw_c=sums[S_BDW], ln_g_c=sums[S_LNGC], ln_b_c=sums[S_LNBC],
            b_pw2_c=sums[S_BPW2], ln_g_d=sums[S_LNGD], ln_b_d=sums[S_LNBD],
            w_pool=jnp.einsum('gchd,gh->gcd', dwp.reshape(4, 64, 4, 64), jnp.eye(4, dtype=F32)),
            w_s_d=dws.reshape(4, SUB, SUB) * jnp.tril(jnp.ones((SUB, SUB), F32)),
            b_s_d=dbs.reshape(SUB, 4, 64).sum(axis=-1).T)
    grad_x = dx.reshape(x.shape)
    for n in REPLICATED + CHANNEL_SHARDED:
        part[n] = dfinal_g[0] if n == 'final_g' else jnp.stack([layer_parts[l][n] for l in range(N_LAYERS)])

    small_names = REPLICATED + CHANNEL_SHARDED
    small_shapes = [part[n].shape for n in small_names] + [(1, 128)]
    slots_in_last, small_sum, dmod_all = _finish_exchange(
        [dw_in_last], SCATTER_RULES[:1],
        _pack([part[n] for n in small_names] + [loss_part[0:1]], rows_multiple=8 * N_DEV), part['b_ada'])

    grads, deltas, new_m, new_v = {}, {}, {}, {}
    half = D_MODEL // 2
    for j, n in enumerate(('w_in', 'w_out', 'w_pw2_c')):
        outs = [_adam_update(shard[n][l], slots[l][j], mom_m[n][l], mom_v[n][l], "update_" + n)
                for l in range(1, N_LAYERS)]
        if n == 'w_in':
            halves = [_adam_update(shard[n][0][rows], s, mom_m[n][0][rows], mom_v[n][0][rows], "update_" + n)
                      for rows, s in ((slice(0, half), slots[0][0]), (slice(half, None), slots_in_last))]
            outs.insert(0, [jnp.concatenate(o) for o in zip(*halves)])
        else:
            outs.insert(0, _adam_update(shard[n][0], slots[0][j], mom_m[n][0], mom_v[n][0], "update_" + n))
        grads[n], deltas[n], new_m[n], new_v[n] = (jnp.stack(o) for o in zip(*outs))

    *small_sums, loss_sum = _unpack(small_sum, small_shapes)
    loss = loss_sum[0, 0]
    gsum = dict(zip(small_names, small_sums))
    for n in CHANNEL_SHARDED:
        width = shard[n].shape[2]
        gsum[n] = lax.dynamic_slice_in_dim(gsum[n], me * width, width, axis=2)
    d_small, m_small, v_small = _adam_many(*[[_as_rows(d[n]) for n in small_names] for d in (shard, gsum, mom_m, mom_v)],
                                           "update_small")
    for j, n in enumerate(small_names):
        grads[n] = gsum[n]
        deltas[n], new_m[n], new_v[n] = (o[j].reshape(shard[n].shape) for o in (d_small, m_small, v_small))

    dmod_cols = lax.dynamic_slice_in_dim(dmod_all, me * ada_cols, ada_cols, axis=2).transpose(1, 0, 2)
    grads['w_ada'], deltas['w_ada'], new_m['w_ada'], new_v['w_ada'] = _ada_update(
        c_act.T, dmod_cols, w_ada, m_w_ada, v_w_ada)

    return (loss, grad_x, *[grads[n] for n in WEIGHTS], *[deltas[n] for n in WEIGHTS],
            *[new_m[n] for n in WEIGHTS], *[new_v[n] for n in WEIGHTS])
```

```python
import functools
import math

import jax
import jax.numpy as jnp
from jax import lax
from jax.experimental import pallas as pl
from jax.experimental.pallas import tpu as pltpu

F32 = jnp.float32
BF16 = jnp.bfloat16

N_DEV = 8
D_MODEL = 1024
GROUP = 256
D_IN = 12 * GROUP
N_LAYERS = 2
HALO = 32
SUB = 128
WIN = SUB + HALO
TOKEN_TILE = 512
REDUCE_TILE = 2048
EPS = 1e-6
VMEM_BYTES_V7X = 64 * 1024 * 1024
VMEM_LIMIT = VMEM_BYTES_V7X - 8 * 1024 * 1024

ADAM_LR = 0.001
ADAM_B1 = 0.9
ADAM_B2 = 0.999
ADAM_EPS = 1e-08
ADAM_WD = 0.01
ADAM_STEP = 10

A_B, A_C, A_X, A_G, B_P, B_G, C_A, C_GL, C_G, D_U, D_V, D_G = range(12)
V_PSCALE, V_BDW, V_LNGC, V_LNBC, V_BPW2, V_LNGD, V_LNBD = range(7)
S_WCONV, S_PSCALE, S_BDW, S_LNGC, S_LNBC, S_BPW2, S_LNGD, S_LNBD, S_WDW = 0, 3, 4, 5, 6, 7, 8, 9, 16
N_SUMS = 64
CONV_A = 3
CONV_C = 31
SUM_ROWS = dict(w_conv_a=slice(S_WCONV, S_WCONV + CONV_A), w_dw_c=slice(S_WDW, S_WDW + CONV_C), pool_scale=S_PSCALE,
                b_dw_c=S_BDW, ln_g_c=S_LNGC, ln_b_c=S_LNBC, b_pw2_c=S_BPW2, ln_g_d=S_LNGD, ln_b_d=S_LNBD)

WEIGHTS = ('norm_g', 'w_ada', 'b_ada', 'w_in', 'w_conv_a', 'w_pool', 'pool_scale', 'w_dw_c', 'b_dw_c', 'ln_g_c',
           'ln_b_c', 'w_pw2_c', 'b_pw2_c', 'ln_g_d', 'ln_b_d', 'w_s_d', 'b_s_d', 'w_out', 'final_g')
REPLICATED = ('norm_g', 'b_ada', 'w_pool', 'pool_scale', 'b_dw_c', 'ln_g_c', 'ln_b_c', 'b_pw2_c', 'ln_g_d', 'ln_b_d',
              'w_s_d', 'b_s_d', 'final_g')
CHANNEL_SHARDED = ('w_conv_a', 'w_dw_c')


def _params(semantics, vmem=VMEM_LIMIT):
    return pltpu.CompilerParams(dimension_semantics=semantics, vmem_limit_bytes=vmem)


def _cols(g):
    return slice(g * GROUP, (g + 1) * GROUP)


def _full(shape):
    return pl.BlockSpec(shape, lambda *_: (0,) * len(shape))


def _silu(x):
    s = jax.nn.sigmoid(x)
    return x * s, s


def _dsilu(sg, s):
    return s + sg * (1.0 - s)


_GELU_C0 = math.sqrt(2.0 / math.pi)
_GELU_C1 = 0.044715


def _gelu(x):
    x2 = x * x
    th = jnp.tanh(_GELU_C0 * (x + _GELU_C1 * (x * x2)))
    p = 0.5 + 0.5 * th
    return x * p, (th, p, x2)


def _dgelu(x, aux):
    th, p, x2 = aux
    return p + (0.5 * x) * (1.0 - th * th) * (_GELU_C0 + (3.0 * _GELU_C0 * _GELU_C1) * x2)


def _layer_norm(x):
    mu = jnp.mean(x, axis=-1, keepdims=True)
    xc = x - mu
    rstd = lax.rsqrt(jnp.mean(xc * xc, axis=-1, keepdims=True) + EPS)
    return xc * rstd, rstd


def _layer_norm_bwd(dn, n, rstd):
    return rstd * (dn - jnp.mean(dn, axis=-1, keepdims=True) - n * jnp.mean(dn * n, axis=-1, keepdims=True))


def _shift_rows(a, k):
    k = k % a.shape[0]
    return a if k == 0 else pltpu.roll(a, k, 0)


def _row_sum8(a):
    s = a[0:8]
    for m in range(1, a.shape[0] // 8):
        s = s + a[8 * m:8 * m + 8]
    return s


def _lane():
    return lax.broadcasted_iota(jnp.int32, (SUB, GROUP), 1)


def _by_quarter(lane, parts):
    return jnp.where(lane < 64, parts[0], jnp.where(lane < 128, parts[1], jnp.where(lane < 192, parts[2], parts[3])))


def _conv_inputs(z_ref, rows):
    def f(g):
        return z_ref[rows, _cols(g)].astype(F32)
    return f(A_C) * f(A_X), f(B_P), f(C_A) * jax.nn.sigmoid(f(C_GL))


def _fill_past(past_ref, zh_ref, zm_ref, is_first, tile):
    parts = _conv_inputs(zh_ref, slice(None))
    for n, a in enumerate(parts):
        past_ref[0:HALO, _cols(n)] = jnp.where(is_first, 0.0, a)

    def body(j, carry):
        r0 = pl.multiple_of(j * SUB, SUB)
        for n, a in enumerate(_conv_inputs(zm_ref, pl.ds(r0, SUB))):
            past_ref[pl.ds(r0 + HALO, SUB), _cols(n)] = a
        return carry

    lax.fori_loop(0, tile // SUB, body, 0)


def _short_conv_taps(qw):
    return [_shift_rows(qw, CONV_A - 1 - k)[HALO:WIN] for k in range(CONV_A)]


def _doubling_sums(w, back, keep):
    n = w.shape[0]
    half = GROUP // 2
    lane = lax.broadcasted_iota(jnp.int32, (SUB, half), 1)

    def grow(s, k):
        return s + _shift_rows(s, k if back else n - k)

    lo2 = grow(w[:, :half], 1)
    lo4 = grow(lo2, 2)
    hi8 = grow(grow(grow(w[:, half:], 1), 2), 4)
    hi16 = grow(hi8, 8)
    return jnp.concatenate([jnp.where(lane < 64, lo2[keep], lo4[keep]), jnp.where(lane < 64, hi8[keep], hi16[keep])],
                           axis=1)


def _window_sums(pw):
    return _doubling_sums(pw, True, slice(HALO, WIN))


def _forward_window_sums(ew):
    return _doubling_sums(ew, False, slice(0, SUB))


def _inv_count(bs_ref, t_first):
    return jnp.where(t_first == 0, bs_ref[SUB:2 * SUB, :], bs_ref[2 * SUB:2 * SUB + 1, :])


def _mixer_forwards(zc, win, t_first, wc_ref, wdw_ref, vec_ref, wp_ref, w2_ref, ws_ref, bs_ref, o_c=None):
    def vec(n):
        return vec_ref[n:n + 1, :]

    def short_conv():
        taps = _short_conv_taps(win(0))
        o_a = wc_ref[0:1, :] * taps[0] + wc_ref[1:2, :] * taps[1] + wc_ref[2:3, :] * taps[2]
        a_b, a_g = zc(A_B), zc(A_G)
        sg_a, s_a = _silu(a_g)
        return a_b * o_a * sg_a, dict(taps=taps, o_a=o_a, a_b=a_b, a_g=a_g, sg_a=sg_a, s_a=s_a)

    def pooling():
        pw = win(1)
        ic = _inv_count(bs_ref, t_first)
        pooled_b = (_window_sums(pw) * ic - pw[HALO:WIN]).astype(BF16)
        y0_b = jnp.dot(pooled_b, wp_ref[...], preferred_element_type=F32)
        b_g = zc(B_G)
        sg_b, s_b = _silu(b_g)
        return y0_b * vec(V_PSCALE) * sg_b, dict(ic=ic, pooled_b=pooled_b, y0_b=y0_b, b_g=b_g, sg_b=sg_b, s_b=s_b)

    def conformer():
        hw = win(2)
        o = o_c
        if o is None:
            o = wdw_ref[CONV_C - 1:CONV_C, :] * hw[HALO:WIN] + vec(V_BDW)
            for k in range(CONV_C - 1):
                o = o + wdw_ref[k:k + 1, :] * _shift_rows(hw, CONV_C - 1 - k)[HALO:WIN]
        n_c, rstd_c = _layer_norm(o)
        ln_c = n_c * vec(V_LNGC) + vec(V_LNBC)
        sl_c, ssl_c = _silu(ln_c)
        sl_b = sl_c.astype(BF16)
        yc = jnp.dot(sl_b, w2_ref[...], preferred_element_type=F32) + vec(V_BPW2)
        c_g = zc(C_G)
        sg_c, s_c = _silu(c_g)
        return yc * sg_c, dict(hw=hw, o_c=o, n_c=n_c, rstd_c=rstd_c, sl_c=sl_c, ssl_c=ssl_c, sl_b=sl_b, yc=yc, c_g=c_g,
                               sg_c=sg_c, s_c=s_c)

    def gating():
        lane = _lane()
        d_u, d_v, d_g = zc(D_U), zc(D_V), zc(D_G)
        u, aux_u = _gelu(d_u)
        gv, aux_v = _gelu(d_v)
        n_d, rstd_d = _layer_norm(gv)
        v_b = (n_d * vec(V_LNGD) + vec(V_LNBD)).astype(BF16)
        r = jnp.dot(ws_ref[...], v_b, preferred_element_type=F32)
        mixed = _by_quarter(lane, [r[h * SUB:(h + 1) * SUB] for h in range(4)]) + bs_ref[0:SUB, :]
        sg_d, s_d = _silu(d_g)
        return u * mixed * sg_d, dict(d_u=d_u, d_v=d_v, u=u, aux_u=aux_u, aux_v=aux_v, n_d=n_d, rstd_d=rstd_d,
                                      v_b=v_b, mixed=mixed, sg_d=sg_d, s_d=s_d)

    return short_conv, pooling, conformer, gating


def _in_proj(x, gs, shift, w_in_b, tile, ride=None):
    n_tok = x.shape[0]

    def body(x_ref, gs_ref, sh_ref, w_ref, h_ref, z_ref):
        xv = x_ref[...]
        r = lax.rsqrt(jnp.mean(xv * xv, axis=-1, keepdims=True) + EPS)
        h = ((xv * r) * gs_ref[...] + sh_ref[...]).astype(BF16)
        h_ref[...] = h
        for j in range(D_IN // D_MODEL):
            cs = slice(j * D_MODEL, (j + 1) * D_MODEL)
            z_ref[:, cs] = jnp.dot(h, w_ref[:, cs], preferred_element_type=F32).astype(BF16)

    return _tiled_call(
        body, (x, gs, shift, w_in_b), name="in_proj", grid=(n_tok // tile,),
        in_specs=[pl.BlockSpec((tile, D_MODEL), lambda i: (i, 0)), _full((1, D_MODEL)), _full((1, D_MODEL)),
                  _full((D_MODEL, D_IN))],
        out_specs=[pl.BlockSpec((tile, D_MODEL), lambda i: (i, 0)), pl.BlockSpec((tile, D_IN), lambda i: (i, 0))],
        out_shape=[jax.ShapeDtypeStruct((n_tok, D_MODEL), BF16), jax.ShapeDtypeStruct((n_tok, D_IN), BF16)],
        ride=ride)


def _small_specs(with_transposes):
    specs = [_full((8, GROUP)), _full((HALO, GROUP)), _full((16, GROUP)), _full((GROUP, GROUP)), _full((GROUP, GROUP)),
             _full((4 * SUB, SUB)), _full((2 * SUB + 8, GROUP))]
    if with_transposes:
        specs += [_full((GROUP, GROUP)), _full((GROUP, GROUP)), _full((4 * SUB, SUB))]
    return specs


def _mix_out(z, x, gate, small, w_out_b, tile, ride=None, head=None):
    n_tok = x.shape[0]
    n_tiles = n_tok // tile
    n_sub = tile // SUB
    cw = D_MODEL // n_sub
    per_halo = tile // HALO
    n_in = 12 + (2 if head else 0)
    n_out = 4 if head else 2

    def cur(i):
        return jnp.minimum(i, n_tiles - 1)

    def prev(i):
        return jnp.maximum(i - 1, 0)

    def body(*refs):
        (zm_ref, zh_ref, x_ref, gate_ref, wc_ref, wdw_ref, vec_ref, wp_ref, w2_ref, ws_ref, bs_ref, wout_ref) = refs[:12]
        xo_ref, oc_ref = refs[n_in:n_in + 2]
        past_ref, ycat_ref, ycat_prev_ref = refs[n_in + n_out:n_in + n_out + 3]
        i = pl.program_id(0)
        t = cur(i)
        if head:
            g_ref, tgt_ref = refs[12:14]
            loss_ref, dg_ref = refs[n_in + 2:n_in + 4]
            xn_ref, acc_ref = refs[n_in + n_out + 3:]
        else:
            xn_ref = xo_ref

        @pl.when(i == 0)
        def _():
            ycat_prev_ref[...] = jnp.zeros_like(ycat_prev_ref)
            if head:
                acc_ref[...] = jnp.zeros_like(acc_ref)

        _fill_past(past_ref, zh_ref, zm_ref, t == 0, tile)
        for j in range(n_sub):
            cs = slice(j * cw, (j + 1) * cw)
            y = jnp.dot(ycat_prev_ref[...], wout_ref[:, cs], preferred_element_type=F32)
            xn_ref[:, cs] = x_ref[:, cs] + gate_ref[:, cs] * y
            rows = slice(j * SUB, (j + 1) * SUB)
            mixers = _mixer_forwards(
                lambda g: zm_ref[rows, _cols(g)].astype(F32), lambda n: past_ref[j * SUB:j * SUB + WIN, _cols(n)],
                t * tile + j * SUB, wc_ref, wdw_ref, vec_ref, wp_ref, w2_ref, ws_ref, bs_ref)
            for n, mixer in enumerate(mixers):
                y, s = mixer()
                ycat_ref[rows, _cols(n)] = y.astype(BF16)
                if "o_c" in s:
                    oc_ref[rows, :] = s["o_c"]
        ycat_prev_ref[...] = ycat_ref[...]
        if head:
            counted = jnp.where(i > 0, 1.0, 0.0)
            xo_ref[...] = _loss_head_block(xn_ref[...], g_ref[...], tgt_ref[...], acc_ref, counted)

            @pl.when(i == n_tiles)
            def _():
                loss_ref[...] = jnp.full((8, 128), 0.5 / D_MODEL, F32) * jnp.sum(acc_ref[0])
                dg_ref[...] = jnp.sum(acc_ref[1], axis=0, keepdims=True)

    in_specs = [pl.BlockSpec((tile, D_IN), lambda i: (cur(i), 0)),
                pl.BlockSpec((HALO, D_IN), lambda i: (jnp.maximum(cur(i) * per_halo - 1, 0), 0)),
                pl.BlockSpec((tile, D_MODEL), lambda i: (prev(i), 0)), _full((1, D_MODEL)),
                *_small_specs(False), _full((D_MODEL, D_MODEL))]
    out_specs = [pl.BlockSpec((tile, D_MODEL), lambda i: (prev(i), 0)), pl.BlockSpec((tile, GROUP), lambda i: (cur(i), 0))]
    out_shape = [jax.ShapeDtypeStruct((n_tok, D_MODEL), F32), jax.ShapeDtypeStruct((n_tok, GROUP), F32)]
    scratch = [pltpu.VMEM((tile + HALO, 3 * GROUP), F32), pltpu.VMEM((tile, D_MODEL), BF16), pltpu.VMEM((tile, D_MODEL), BF16)]
    args = (z, z, x, gate, *small, w_out_b)
    if head:
        in_specs += [_full((1, D_MODEL)), pl.BlockSpec((tile, D_MODEL), lambda i: (prev(i), 0))]
        out_specs += [_full((8, 128)), _full((1, D_MODEL))]
        out_shape += [jax.ShapeDtypeStruct((8, 128), F32), jax.ShapeDtypeStruct((1, D_MODEL), F32)]
        scratch += [pltpu.VMEM((tile, D_MODEL), F32), pltpu.VMEM((2, 8, D_MODEL), F32)]
        args += tuple(head)
    outs, rode = _tiled_call(body, args, name="mix_out", grid=(n_tiles + 1,), in_specs=in_specs, out_specs=out_specs,
                             out_shape=out_shape, scratch_shapes=scratch, ride=ride)
    return outs, rode


def _loss_head_block(xv, g, target, acc_ref, counted):
    r = lax.rsqrt(jnp.mean(xv * xv, axis=-1, keepdims=True) + EPS)
    xn = xv * r
    err = xn * g - target
    acc_ref[0] = acc_ref[0] + counted * _row_sum8(err * err)
    dy = err * (1.0 / D_MODEL)
    acc_ref[1] = acc_ref[1] + counted * _row_sum8(dy * xn)
    a = dy * g
    return r * (a - xn * jnp.mean(a * xn, axis=-1, keepdims=True))


def _mix_bwd(z, o_c, dx_next, gate, small, small_t, w_out_b, tile, ride=None):
    n_tok = z.shape[0]
    n_tiles = n_tok // tile
    n_sub = tile // SUB
    cw = D_MODEL // n_sub
    per_halo = tile // HALO
    nt_dims = (((1,), (1,)), ((), ()))

    def tile_of(i):
        return n_tiles - 1 - i

    def next_tile_of(i):
        return jnp.maximum(n_tiles - 2 - i, 0)

    def body(zm_ref, zh_ref, oc_ref, dxn_ref, dxn_next_ref, gate_ref, wc_ref, wdw_ref, vec_ref, wp_ref, w2_ref, ws_ref,
             bs_ref, wpt_ref, w2t_ref, wst_ref, wout_ref,
             dz_ref, ycat_ref, sums_ref, dwp_ref, dw2_ref, dws_ref, dbs_ref,
             past_ref, future_ref, dy_ref, dy_next_ref, acc_ref):
        i = pl.program_id(0)
        t = tile_of(i)

        @pl.when(i == 0)
        def _():
            acc_ref[...] = jnp.zeros_like(acc_ref)
            dwp_ref[...] = jnp.zeros_like(dwp_ref)
            dw2_ref[...] = jnp.zeros_like(dw2_ref)
            dws_ref[...] = jnp.zeros_like(dws_ref)
            dbs_ref[...] = jnp.zeros_like(dbs_ref)
            future_ref[tile:tile + HALO, :] = jnp.zeros((HALO, 3 * GROUP), F32)
            dy_ref[...] = lax.dot_general((dxn_ref[...] * gate_ref[...]).astype(BF16), wout_ref[...], nt_dims,
                                        preferred_element_type=F32)

        _fill_past(past_ref, zh_ref, zm_ref, t == 0, tile)
        dyb_next = (dxn_next_ref[...] * gate_ref[...]).astype(BF16)

        def vec(n):
            return vec_ref[n:n + 1, :]

        for jj in range(n_sub):
            j = n_sub - 1 - jj
            r0 = j * SUB
            rows = slice(r0, r0 + SUB)

            def zc(g):
                return zm_ref[rows, _cols(g)].astype(F32)

            def add(n, a):
                acc_ref[n] = acc_ref[n] + _row_sum8(a)

            def put(g, a):
                dz_ref[rows, _cols(g)] = a.astype(BF16)

            def future_window(n, a):
                future_ref[rows, _cols(n)] = a
                return future_ref[r0:r0 + WIN, _cols(n)]

            short_conv, pooling, conformer, gating = _mixer_forwards(
                zc, lambda n: past_ref[r0:r0 + WIN, _cols(n)], t * tile + r0,
                wc_ref, wdw_ref, vec_ref, wp_ref, w2_ref, ws_ref, bs_ref, o_c=oc_ref[rows, :])
            lane = _lane()
            ks = slice(jj * cw, (jj + 1) * cw)
            dy_next_ref[:, ks] = lax.dot_general(dyb_next, wout_ref[ks, :], nt_dims, preferred_element_type=F32)

            y, s = short_conv()
            ycat_ref[rows, _cols(0)] = y.astype(BF16)
            dy = dy_ref[rows,_cols(0)]
            put(A_B, dy * s["o_a"] * s["sg_a"])
            put(A_G, dy * s["a_b"] * s["o_a"] * _dsilu(s["sg_a"], s["s_a"]))
            do = dy * s["a_b"] * s["sg_a"]
            for k in range(CONV_A):
                add(S_WCONV + k, do * s["taps"][k])
            dow = future_window(0, do)
            dq = wc_ref[CONV_A - 1:CONV_A, :] * dow[0:SUB]
            for k in range(CONV_A - 1):
                dq = dq + wc_ref[k:k + 1, :] * _shift_rows(dow, WIN - (CONV_A - 1 - k))[0:SUB]
            put(A_C, dq * zc(A_X))
            put(A_X, dq * zc(A_C))

            y, s = pooling()
            ycat_ref[rows, _cols(1)] = y.astype(BF16)
            dy = dy_ref[rows,_cols(1)]
            put(B_G, dy * (s["y0_b"] * vec(V_PSCALE)) * _dsilu(s["sg_b"], s["s_b"]))
            dyb = dy * s["sg_b"]
            add(S_PSCALE, dyb * s["y0_b"])
            dpw_b = (dyb * vec(V_PSCALE)).astype(BF16)
            dwp_ref[...] += lax.dot_general(s["pooled_b"], dpw_b, (((0,), (0,)), ((), ())), preferred_element_type=F32)
            dpooled = jnp.dot(dpw_b, wpt_ref[...], preferred_element_type=F32)
            ew = future_window(1, dpooled * s["ic"])
            put(B_P, _forward_window_sums(ew) - dpooled)

            y, s = conformer()
            ycat_ref[rows, _cols(2)] = y.astype(BF16)
            dy = dy_ref[rows,_cols(2)]
            put(C_G, dy * s["yc"] * _dsilu(s["sg_c"], s["s_c"]))
            dyc = dy * s["sg_c"]
            add(S_BPW2, dyc)
            dyc_b = dyc.astype(BF16)
            dw2_ref[...] += lax.dot_general(s["sl_b"], dyc_b, (((0,), (0,)), ((), ())), preferred_element_type=F32)
            dln = jnp.dot(dyc_b, w2t_ref[...], preferred_element_type=F32) * _dsilu(s["sl_c"], s["ssl_c"])
            add(S_LNGC, dln * s["n_c"])
            add(S_LNBC, dln)
            do = _layer_norm_bwd(dln * vec(V_LNGC), s["n_c"], s["rstd_c"])
            add(S_BDW, do)
            hw = s["hw"]
            for k in range(CONV_C):
                add(S_WDW + k, do * _shift_rows(hw, CONV_C - 1 - k)[HALO:WIN])
            dow = future_window(2, do)
            dhc = wdw_ref[CONV_C - 1:CONV_C, :] * dow[0:SUB]
            for k in range(CONV_C - 1):
                dhc = dhc + wdw_ref[k:k + 1, :] * _shift_rows(dow, WIN - (CONV_C - 1 - k))[0:SUB]
            c_a = zc(C_A)
            sgl = jax.nn.sigmoid(zc(C_GL))
            put(C_A, dhc * sgl)
            put(C_GL, dhc * c_a * sgl * (1.0 - sgl))

            y, s = gating()
            ycat_ref[rows, _cols(3)] = y.astype(BF16)
            dy = dy_ref[rows,_cols(3)]
            put(D_G, dy * s["u"] * s["mixed"] * _dsilu(s["sg_d"], s["s_d"]))
            put(D_U, dy * s["mixed"] * s["sg_d"] * _dgelu(s["d_u"], s["aux_u"]))
            dmixed = dy * s["u"] * s["sg_d"]
            dbs_ref[...] += dmixed
            by_head = jnp.concatenate(
                [jnp.where((lane >= 64 * h) & (lane < 64 * h + 64), dmixed, 0.0) for h in range(4)], axis=0).astype(BF16)
            dws_ref[...] += lax.dot_general(by_head, s["v_b"], (((1,), (1,)), ((), ())), preferred_element_type=F32)
            rv = jnp.dot(wst_ref[...], dmixed.astype(BF16), preferred_element_type=F32)
            dv = _by_quarter(lane, [rv[h * SUB:(h + 1) * SUB] for h in range(4)])
            add(S_LNGD, dv * s["n_d"])
            add(S_LNBD, dv)
            dgv = _layer_norm_bwd(dv * vec(V_LNGD), s["n_d"], s["rstd_d"])
            put(D_V, dgv * _dgelu(s["d_v"], s["aux_v"]))

        future_ref[tile:tile + HALO, :] = future_ref[0:HALO, :]
        dy_ref[...] = dy_next_ref[...]

        @pl.when(i == n_tiles - 1)
        def _():
            for n in range(N_SUMS):
                sums_ref[n:n + 1, :] = jnp.sum(acc_ref[n], axis=0, keepdims=True)

    return _tiled_call(
        body, (z, z, o_c, dx_next, dx_next, gate, *small, *small_t, w_out_b), name="mix_bwd", grid=(n_tiles,),
        in_specs=[pl.BlockSpec((tile, D_IN), lambda i: (tile_of(i), 0)),
                  pl.BlockSpec((HALO, D_IN), lambda i: (jnp.maximum(tile_of(i) * per_halo - 1, 0), 0)),
                  pl.BlockSpec((tile, GROUP), lambda i: (tile_of(i), 0)),
                  pl.BlockSpec((tile, D_MODEL), lambda i: (tile_of(i), 0)),
                  pl.BlockSpec((tile, D_MODEL), lambda i: (next_tile_of(i), 0)), _full((1, D_MODEL)),
                  *_small_specs(True), _full((D_MODEL, D_MODEL))],
        out_specs=[pl.BlockSpec((tile, D_IN), lambda i: (tile_of(i), 0)),
                   pl.BlockSpec((tile, D_MODEL), lambda i: (tile_of(i), 0)),
                   _full((N_SUMS, GROUP)), _full((GROUP, GROUP)), _full((GROUP, GROUP)), _full((4 * SUB, SUB)),
                   _full((SUB, GROUP))],
        out_shape=[jax.ShapeDtypeStruct((n_tok, D_IN), BF16), jax.ShapeDtypeStruct((n_tok, D_MODEL), BF16),
                   jax.ShapeDtypeStruct((N_SUMS, GROUP), F32), jax.ShapeDtypeStruct((GROUP, GROUP), F32),
                   jax.ShapeDtypeStruct((GROUP, GROUP), F32), jax.ShapeDtypeStruct((4 * SUB, SUB), F32),
                   jax.ShapeDtypeStruct((SUB, GROUP), F32)],
        scratch_shapes=[pltpu.VMEM((tile + HALO, 3 * GROUP), F32), pltpu.VMEM((tile + HALO, 3 * GROUP), F32),
                        pltpu.VMEM((tile, D_MODEL), F32), pltpu.VMEM((tile, D_MODEL), F32),
                        pltpu.VMEM((N_SUMS, 8, GROUP), F32)], ride=ride)


def _norm_bwd(x, dz, dx_next, gs, w_in_b, tile, ride=None, blocks=None, begun=None, finish=True):
    n_tok = x.shape[0]
    first, n_tiles = blocks or (0, n_tok // tile)
    n_in = 5 + (2 if begun else 0)

    def body(*refs):
        x_ref, dz_ref, dxn_ref, gs_ref, w_ref = refs[:5]
        dx_ref = refs[n_in]
        acc_ref = refs[-1]
        i = pl.program_id(0)

        @pl.when(i == 0)
        def _():
            acc_ref[...] = refs[6][...] if begun else jnp.zeros_like(acc_ref)

        dh = lax.dot_general(dz_ref[...], w_ref[...], (((1,), (1,)), ((), ())), preferred_element_type=F32)
        xv = x_ref[...]
        r = lax.rsqrt(jnp.mean(xv * xv, axis=-1, keepdims=True) + EPS)
        xn = xv * r
        acc_ref[0] = acc_ref[0] + _row_sum8(dh)
        acc_ref[1] = acc_ref[1] + _row_sum8(dh * xn)
        dxn = dh * gs_ref[...]
        dx_ref[...] = dxn_ref[...] + r * (dxn - xn * jnp.mean(dxn * xn, axis=-1, keepdims=True))

        @pl.when(i == n_tiles - 1)
        def _():
            if finish:
                refs[n_in + 1][...] = jnp.sum(acc_ref[0], axis=0, keepdims=True)
                refs[n_in + 2][...] = jnp.sum(acc_ref[1], axis=0, keepdims=True)
            else:
                refs[n_in + 1][...] = acc_ref[...]

    def rows(i):
        return (first + i, 0)

    in_specs = [pl.BlockSpec((tile, D_MODEL), rows), pl.BlockSpec((tile, D_IN), rows), pl.BlockSpec((tile, D_MODEL), rows),
                _full((1, D_MODEL)), _full((D_MODEL, D_IN))]
    args = (x, dz, dx_next, gs, w_in_b)
    if begun:
        in_specs += [pl.BlockSpec(memory_space=pl.ANY), _full((2, 8, D_MODEL))]
        args += tuple(begun)
    vec = jax.ShapeDtypeStruct((1, D_MODEL), F32)
    return _tiled_call(
        body, args, name="norm_bwd", grid=(n_tiles,), in_specs=in_specs,
        out_specs=[pl.BlockSpec((tile, D_MODEL), rows)] + ([_full((1, D_MODEL))] * 2 if finish else [_full((2, 8, D_MODEL))]),
        out_shape=[jax.ShapeDtypeStruct((n_tok, D_MODEL), F32)]
        + ([vec, vec] if finish else [jax.ShapeDtypeStruct((2, 8, D_MODEL), F32)]),
        scratch_shapes=[pltpu.VMEM((2, 8, D_MODEL), F32)], ride=ride, aliases={5: 0} if begun else None)


def _tokens_matmul(a, b, name, out_dtype=F32, ride=None, a_cols=None):
    n_tok = a.shape[0]
    a_block, ka = a_cols or (0, a.shape[1])
    nb = b.shape[1]
    tk = min(REDUCE_TILE * (4 // b.dtype.itemsize), n_tok)
    cb = min(D_MODEL, nb)
    n_steps = n_tok // tk

    def body(a_ref, b_ref, o_ref, acc_ref):
        i = pl.program_id(1)

        @pl.when(i == 0)
        def _():
            acc_ref[...] = jnp.zeros_like(acc_ref)

        acc_ref[...] += lax.dot_general(a_ref[...], b_ref[...].astype(BF16), (((0,), (0,)), ((), ())),
                                        preferred_element_type=F32)

        @pl.when(i == n_steps - 1)
        def _():
            o_ref[...] = acc_ref[...].astype(out_dtype)

    (out,), rode = _tiled_call(
        body, (a, b), name=name, grid=(nb // cb, n_steps),
        in_specs=[pl.BlockSpec((tk, ka), lambda j, i: (i, a_block)), pl.BlockSpec((tk, cb), lambda j, i: (i, j))],
        out_specs=[pl.BlockSpec((ka, cb), lambda j, i: (0, j))],
        out_shape=[jax.ShapeDtypeStruct((ka, nb), out_dtype)],
        scratch_shapes=[pltpu.VMEM((ka, cb), F32)], ride=ride)
    return out, rode


def _out_proj_grads(m, w_out_b, gate):
    rb = 256
    n_blocks = D_MODEL // rb

    def body(m_ref, w_ref, gate_ref, dw_ref, dgate_ref, acc_ref):
        i = pl.program_id(0)

        @pl.when(i == 0)
        def _():
            acc_ref[...] = jnp.zeros_like(acc_ref)

        mv = m_ref[...]
        dw_ref[...] = (mv * gate_ref[...]).astype(BF16)
        acc_ref[...] += _row_sum8(mv * w_ref[...].astype(F32))

        @pl.when(i == n_blocks - 1)
        def _():
            dgate_ref[...] = jnp.sum(acc_ref[...], axis=0, keepdims=True)

    return pl.pallas_call(
        body, name="out_proj_grads", grid=(n_blocks,),
        in_specs=[pl.BlockSpec((rb, D_MODEL), lambda i: (i, 0)), pl.BlockSpec((rb, D_MODEL), lambda i: (i, 0)),
                  _full((1, D_MODEL))],
        out_specs=[pl.BlockSpec((rb, D_MODEL), lambda i: (i, 0)), _full((1, D_MODEL))],
        out_shape=[jax.ShapeDtypeStruct((D_MODEL, D_MODEL), BF16), jax.ShapeDtypeStruct((1, D_MODEL), F32)],
        scratch_shapes=[pltpu.VMEM((8, D_MODEL), F32)],
        compiler_params=_params(("arbitrary",)),
    )(m, w_out_b, gate)


def _modulation_columns(c_all, w_ada, b_cols):
    cols = w_ada.shape[2]

    def body(c_ref, w_ref, b_ref, ca_ref, mod_ref):
        ca, _ = _silu(c_ref[...])
        ca_ref[...] = ca
        for l in range(N_LAYERS):
            mod_ref[l] = jnp.dot(ca, w_ref[l], precision=lax.Precision.HIGHEST, preferred_element_type=F32) + b_ref[l:l + 1, :]

    return pl.pallas_call(
        body, name="modulation_columns",
        out_shape=[jax.ShapeDtypeStruct((N_DEV, D_MODEL), F32), jax.ShapeDtypeStruct((N_LAYERS, N_DEV, cols), F32)],
        compiler_params=pltpu.CompilerParams(vmem_limit_bytes=VMEM_LIMIT),
    )(c_all, w_ada, b_cols)


def _adam(w, g, m, v):
    m2 = ADAM_B1 * m + (1.0 - ADAM_B1) * g
    v2 = ADAM_B2 * v + (1.0 - ADAM_B2) * (g * g)
    m_hat = m2 / (1.0 - ADAM_B1 ** ADAM_STEP)
    v_hat = v2 / (1.0 - ADAM_B2 ** ADAM_STEP)
    return -ADAM_LR * (m_hat / (jnp.sqrt(v_hat) + ADAM_EPS) + ADAM_WD * w), m2, v2


def _row_block(rows, cols, slots):
    target = max(8, (1 << 19) // (cols * max(slots, 1)))
    rb = rows
    while rb > target and rb % 2 == 0 and (rb // 2) % 8 == 0:
        rb //= 2
    return rb


def _adam_update(w, g, m, v, name):
    rows, cols = w.shape
    slotted = g.ndim == 3
    rb = _row_block(rows, cols, N_DEV if slotted else 1)

    def body(w_ref, g_ref, m_ref, v_ref, go_ref, d_ref, mo_ref, vo_ref):
        if slotted:
            gv = g_ref[0].astype(F32)
            for q in range(1, N_DEV):
                gv = gv + g_ref[q].astype(F32)
        else:
            gv = g_ref[...]
        go_ref[...] = gv
        d_ref[...], mo_ref[...], vo_ref[...] = _adam(w_ref[...], gv, m_ref[...], v_ref[...])

    blk = pl.BlockSpec((rb, cols), lambda i: (i, 0))
    g_blk = pl.BlockSpec((N_DEV, rb, cols), lambda i: (0, i, 0)) if slotted else blk
    return pl.pallas_call(
        body, name=name, grid=(rows // rb,),
        in_specs=[blk, g_blk, blk, blk], out_specs=[blk] * 4,
        out_shape=[jax.ShapeDtypeStruct((rows, cols), F32)] * 4,
        compiler_params=_params(("parallel",)),
    )(w, g, m, v)


def _adam_many(ws, gs, ms, vs, name):
    n = len(ws)

    def body(*refs):
        w_refs, g_refs, m_refs, v_refs, d_refs, mo_refs, vo_refs = (refs[k * n:(k + 1) * n] for k in range(7))
        for j in range(n):
            d_refs[j][...], mo_refs[j][...], vo_refs[j][...] = _adam(w_refs[j][...], g_refs[j][...], m_refs[j][...],
                                                                  v_refs[j][...])

    res = pl.pallas_call(
        body, name=name, out_shape=[jax.ShapeDtypeStruct(w.shape, F32) for w in ws] * 3,
        compiler_params=pltpu.CompilerParams(vmem_limit_bytes=VMEM_LIMIT),
    )(*ws, *gs, *ms, *vs)
    return res[:n], res[n:2 * n], res[2 * n:]


def _as_rows(a):
    return a.reshape(-1, a.shape[-1]) if a.ndim > 1 else a.reshape(1, -1)


def _ada_update(ca_t, dmod_cols, w, m, v):
    _, rows, cols = w.shape

    def body(ca_ref, dm_ref, w_ref, m_ref, v_ref, g_ref, d_ref, mo_ref, vo_ref):
        g = ca_ref[:, 0:1] * dm_ref[0, 0:1, :]
        for b in range(1, N_DEV):
            g = g + ca_ref[:, b:b + 1] * dm_ref[0, b:b + 1, :]
        g_ref[0] = g
        d_ref[0], mo_ref[0], vo_ref[0] = _adam(w_ref[0], g, m_ref[0], v_ref[0])

    blk = pl.BlockSpec((1, rows, cols), lambda l: (l, 0, 0))
    return pl.pallas_call(
        body, name="ada_update", grid=(N_LAYERS,),
        in_specs=[_full((rows, N_DEV)), pl.BlockSpec((1, N_DEV, cols), lambda l: (l, 0, 0)), blk, blk, blk],
        out_specs=[blk] * 4, out_shape=[jax.ShapeDtypeStruct(w.shape, F32)] * 4,
        compiler_params=_params(("parallel",)),
    )(ca_t, dmod_cols, w, m, v)


def _exchange_sems(n):
    return [pltpu.SemaphoreType.DMA((n, N_DEV - 1)), pltpu.SemaphoreType.DMA((n, N_DEV - 1)),
            pltpu.SemaphoreType.DMA((n,))]


def _exchange_copies(plans, srcs, outs, sems, receiving, only=None):
    send_sems, recv_sems, local_sems = sems
    x, y, c = lax.axis_index("x"), lax.axis_index("y"), lax.axis_index("c")
    me = 4 * x + 2 * y + c

    def remote(i, k, incoming):
        _, o, send, land = plans[i]
        px = 1 - x if k & 4 else x
        py = 1 - y if k & 2 else y
        pc = 1 - c if k & 1 else c
        p = 4 * px + 2 * py + pc
        return pltpu.make_async_remote_copy(
            src_ref=send(srcs[i], p), dst_ref=land(outs[o], p if incoming else me),
            send_sem=send_sems.at[i, k - 1], recv_sem=recv_sems.at[i, k - 1],
            device_id=(px, py, pc), device_id_type=pl.DeviceIdType.MESH)

    which = range(len(plans)) if only is None else only
    pairs = [(i, k) for k in range(1, N_DEV) for i in which]
    local = [pltpu.make_async_copy(plans[i][2](srcs[i], me), plans[i][3](outs[plans[i][1]], me), local_sems.at[i])
             for i in which]
    return local, [remote(i, k, False) for i, k in pairs], [remote(i, k, True) for i, k in pairs] if receiving else []


def _exchange_start(plans, srcs, outs, sems, only=None):
    local, outgoing, _ = _exchange_copies(plans, srcs, outs, sems, False, only)
    for cp in local + outgoing:
        cp.start()


def _exchange_wait(plans, srcs, outs, sems, only=None):
    local, outgoing, incoming = _exchange_copies(plans, srcs, outs, sems, True, only)
    for cp in incoming:
        cp.wait_recv()
    for cp in outgoing:
        cp.wait_send()
    for cp in local:
        cp.wait()


def _exchange(name, ride):
    out_shapes, plans = ride
    n = len(plans)
    hbm = pl.BlockSpec(memory_space=pltpu.HBM)

    def body(*refs):
        srcs, outs, sems = refs[:n], refs[n:n + len(out_shapes)], refs[n + len(out_shapes):]
        _exchange_start(plans, srcs, outs, sems)
        _exchange_wait(plans, srcs, outs, sems)

    return pl.pallas_call(
        body, name=name, in_specs=[hbm] * n, out_specs=[hbm] * len(out_shapes), out_shape=list(out_shapes),
        scratch_shapes=_exchange_sems(n),
    )(*[p[0] for p in plans])


def _first_gather(c, arrays, rules):
    n = len(arrays)
    c_shapes, c_plans = _plans([c], [_gather])
    shapes, lands = zip(*[(shape, land) for shape, _, land in (rule(a) for a, rule in zip(arrays, rules))])
    hbm = pl.BlockSpec(memory_space=pltpu.HBM)

    def body(*refs):
        c_ref, srcs, c_all_ref, outs = refs[0], refs[1:1 + n], refs[1 + n], refs[2 + n:2 + 2 * n]
        send_sems, recv_sems, local_sems = refs[2 + 2 * n:5 + 2 * n]
        c_sems = refs[5 + 2 * n:]
        x, y, core = lax.axis_index("x"), lax.axis_index("y"), lax.axis_index("c")
        me, sibling = (x, y, core), (x, y, 1 - core)
        chips = [(1 - x, y), (x, 1 - y), (1 - x, 1 - y)]

        def block(a, px, py, pc):
            return lands[a](outs[a], 4 * px + 2 * py + pc)

        def copy(a, k, origin, to, own=False):
            return pltpu.make_async_remote_copy(
                src_ref=srcs[a] if own else block(a, *origin), dst_ref=block(a, *origin),
                send_sem=send_sems.at[a, k], recv_sem=recv_sems.at[a, k], device_id=to,
                device_id_type=pl.DeviceIdType.MESH)

        _exchange_start(c_plans, [c_ref], [c_all_ref], c_sems)
        mine = [pltpu.make_async_copy(srcs[a], block(a, *me), local_sems.at[a]) for a in range(n)]
        first = [copy(a, 0, me, sibling, own=True) for a in range(n)]
        first += [copy(a, 1 + j, me, (*chip, core), own=True) for j, chip in enumerate(chips) for a in range(n)]
        for cp in mine + first:
            cp.start()
        passed = [[copy(a, 4 + j, (*chip, core), sibling) for a in range(n)] for j, chip in enumerate(chips)]
        for j, chip in enumerate(chips):
            for a in range(n):
                copy(a, 1 + j, (*chip, core), me).wait_recv()
                passed[j][a].start()
        for a in range(n):
            copy(a, 0, sibling, me).wait_recv()
        for j, chip in enumerate(chips):
            for a in range(n):
                copy(a, 4 + j, (*chip, 1 - core), me).wait_recv()
        for cp in first + [cp for row in passed for cp in row]:
            cp.wait_send()
        for cp in mine:
            cp.wait()
        _exchange_wait(c_plans, [c_ref], [c_all_ref], c_sems)

    return pl.pallas_call(
        body, name="first_gather", in_specs=[hbm] * (1 + n), out_specs=[hbm] * (1 + n),
        out_shape=[c_shapes[0], *shapes],
        scratch_shapes=[pltpu.SemaphoreType.DMA((n, N_DEV - 1)), pltpu.SemaphoreType.DMA((n, N_DEV - 1)),
                        pltpu.SemaphoreType.DMA((n,)), *_exchange_sems(1)],
    )(c, *arrays)


def _finish_exchange(big, big_rules, packed, dmod):
    n_rows = packed.shape[0]
    r = n_rows // N_DEV
    shapes, plans = _plans([*big, packed, dmod], [*big_rules, _scatter_rows, _gather])
    n_first = len(plans)
    i_small = n_first - 2
    _, send, land = _gather_rows(jax.ShapeDtypeStruct((r, 128), F32))
    plans = plans + [(None, len(shapes), send, land)]
    shapes = shapes + [jax.ShapeDtypeStruct((n_rows, 128), F32)]
    first = [i for i in range(n_first) if i != i_small]
    hbm = pl.BlockSpec(memory_space=pltpu.HBM)

    def body(*refs):
        srcs, outs = list(refs[:n_first]), refs[n_first:n_first + len(shapes)]
        parts_ref, sum_ref, local_sem = refs[n_first + len(shapes):n_first + len(shapes) + 3]
        sems = refs[n_first + len(shapes) + 3:]
        srcs.append(sum_ref)
        _exchange_start(plans, srcs, outs, sems, only=range(n_first))
        _exchange_wait(plans, srcs, outs, sems, only=[i_small])
        cp = pltpu.make_async_copy(outs[i_small], parts_ref, local_sem)
        cp.start()
        cp.wait()
        g = parts_ref[0]
        for q in range(1, N_DEV):
            g = g + parts_ref[q]
        sum_ref[...] = g
        _exchange_start(plans, srcs, outs, sems, only=[n_first])
        _exchange_wait(plans, srcs, outs, sems, only=[n_first])
        _exchange_wait(plans, srcs, outs, sems, only=first)

    res = pl.pallas_call(
        body, name="finish_exchange", in_specs=[hbm] * n_first, out_specs=[hbm] * len(shapes), out_shape=shapes,
        scratch_shapes=[pltpu.VMEM((N_DEV, r, 128), F32), pltpu.VMEM((r, 128), F32), pltpu.SemaphoreType.DMA(()),
                        *_exchange_sems(len(plans))],
    )(*big, packed, dmod)
    return (*res[:len(big)], res[-1], res[n_first - 1])


def _tiled_call(body, args, *, name, grid, in_specs, out_specs, out_shape, scratch_shapes=(), ride=None, aliases=None):
    params = _params(("arbitrary",) * len(grid))
    if ride is None:
        return pl.pallas_call(body, name=name, grid=grid, in_specs=in_specs, out_specs=out_specs, out_shape=out_shape,
                              scratch_shapes=list(scratch_shapes), input_output_aliases=aliases or {},
                              compiler_params=params)(*args), []
    shapes, plans = ride
    n_in, n_src, n_out, n_dst, n_scr = len(in_specs), len(plans), len(out_specs), len(shapes), len(scratch_shapes)
    hbm = pl.BlockSpec(memory_space=pltpu.HBM)

    def carrying(*refs):
        ins, srcs, refs = refs[:n_in], refs[n_in:n_in + n_src], refs[n_in + n_src:]
        outs, dsts, refs = refs[:n_out], refs[n_out:n_out + n_dst], refs[n_out + n_dst:]
        scratch, sems = refs[:n_scr], refs[n_scr:]
        ids = [pl.program_id(a) for a in range(len(grid))]
        first = functools.reduce(jnp.logical_and, [i == 0 for i in ids])
        last = functools.reduce(jnp.logical_and, [i == g - 1 for i, g in zip(ids, grid)])

        @pl.when(first)
        def _():
            _exchange_start(plans, srcs, dsts, sems)

        body(*ins, *outs, *scratch)

        @pl.when(last)
        def _():
            _exchange_wait(plans, srcs, dsts, sems)

    res = pl.pallas_call(
        carrying, name=name, grid=grid, in_specs=[*in_specs, *[hbm] * n_src], out_specs=[*out_specs, *[hbm] * n_dst],
        out_shape=[*out_shape, *shapes], scratch_shapes=[*scratch_shapes, *_exchange_sems(n_src)],
        input_output_aliases=aliases or {}, compiler_params=params)(*args, *[p[0] for p in plans])
    return res[:n_out], res[n_out:]


def _tail(nd, idx):
    return (slice(None),) * (nd - 2) + idx


def _gather(a):
    return jax.ShapeDtypeStruct((N_DEV,) + a.shape, a.dtype), lambda s, p: s, lambda o, q: o.at[q]


def _gather_rows(a):
    r = a.shape[-2]
    return (jax.ShapeDtypeStruct(a.shape[:-2] + (N_DEV * r, a.shape[-1]), a.dtype), lambda s, p: s,
            lambda o, q: o.at[_tail(a.ndim, (pl.ds(pl.multiple_of(q * r, r), r), slice(None)))])


def _gather_cols(a):
    c = a.shape[-1]
    return (jax.ShapeDtypeStruct(a.shape[:-1] + (N_DEV * c,), a.dtype), lambda s, p: s,
            lambda o, q: o.at[_tail(a.ndim, (slice(None), pl.ds(pl.multiple_of(q * c, c), c)))])


def _scatter_rows(a):
    r = a.shape[0] // N_DEV
    return (jax.ShapeDtypeStruct((N_DEV, r, a.shape[1]), a.dtype),
            lambda s, p: s.at[pl.ds(pl.multiple_of(p * r, r), r), :], lambda o, q: o.at[q])


def _scatter_cols(a):
    c = a.shape[1] // N_DEV
    return (jax.ShapeDtypeStruct((N_DEV, a.shape[0], c), a.dtype),
            lambda s, p: s.at[:, pl.ds(pl.multiple_of(p * c, c), c)], lambda o, q: o.at[q])


def _plans(arrays, rules):
    shapes, plans = [], []
    for o, (a, rule) in enumerate(zip(arrays, rules)):
        shape, send, land = rule(a)
        shapes.append(shape)
        plans.append((a, o, send, land))
    return shapes, plans


def _pack(pieces, rows_multiple=8):
    flat = []
    for a in pieces:
        f = a.reshape(-1)
        flat.append(jnp.pad(f, (0, (-f.shape[0]) % 128)))
    total = sum(f.shape[0] for f in flat)
    flat.append(jnp.zeros(((-total) % (128 * rows_multiple),), F32))
    return jnp.concatenate(flat).reshape(-1, 128)


def _unpack(buf, shapes, lead=()):
    flat = buf.reshape(lead + (-1,))
    out, off = [], 0
    for s in shapes:
        n = math.prod(s)
        out.append(flat[..., off:off + n].reshape(lead + tuple(s)))
        off += n + (-n) % 128
    return out


def _pad_rows(a, rows):
    return jnp.pad(a, ((0, rows - a.shape[0]), (0, 0)))


VEC_NAMES = ('pool_scale', 'b_dw_c', 'ln_g_c', 'ln_b_c', 'b_pw2_c', 'ln_g_d', 'ln_b_d')
GATHERED = ('w_in', 'w_out', 'w_pw2_c', 'w_conv_a', 'w_dw_c')
GATHER_RULES = (_gather_cols, _gather_rows, _gather_rows, _gather, _gather)
SCATTER_RULES = (_scatter_cols, _scatter_rows, _scatter_rows)


def _weight_shards(shard, l):
    return [shard[n][l].astype(BF16) if n in ('w_in', 'w_out') else shard[n][l] for n in GATHERED]


def _layer_weights(shard, l, gathered):
    w_in_b, w_out_b, w_pw2, wconv_parts, wdw_parts = gathered
    wconv = wconv_parts.transpose(1, 0, 2).reshape(CONV_A, GROUP)
    wdw = wdw_parts.transpose(1, 0, 2).reshape(CONV_C, GROUP)
    wp = jnp.einsum('gcd,gh->gchd', shard['w_pool'][l], jnp.eye(4, dtype=F32)).reshape(GROUP, GROUP)
    ws = shard['w_s_d'][l] * jnp.tril(jnp.ones((SUB, SUB), F32))
    vec = jnp.stack([shard[n][l] for n in VEC_NAMES])
    width = jnp.repeat(jnp.asarray([2.0, 4.0, 8.0, 16.0], F32), 64)[None]
    count = jnp.minimum(jnp.arange(1, SUB + 1, dtype=F32)[:, None], width)
    gating_and_counts = jnp.concatenate([jnp.repeat(shard['b_s_d'][l].T, 64, axis=1), 1.0 / count, 1.0 / width,
                                         jnp.zeros((7, GROUP), F32)])
    small = (_pad_rows(wconv, 8), _pad_rows(wdw, HALO), _pad_rows(vec, 16), wp.astype(BF16), w_pw2.astype(BF16),
             ws.reshape(4 * SUB, SUB).astype(BF16), gating_and_counts)
    small_t = (wp.T.astype(BF16), w_pw2.T.astype(BF16), ws.transpose(0, 2, 1).reshape(4 * SUB, SUB).astype(BF16))
    return w_in_b, w_out_b, small, small_t


def kernel(x, c, norm_g, w_ada, b_ada, w_in, w_conv_a, w_pool, pool_scale, w_dw_c, b_dw_c, ln_g_c, ln_b_c, w_pw2_c, b_pw2_c, ln_g_d, ln_b_d, w_s_d, b_s_d, w_out, final_g, loss_target, m_norm_g, m_w_ada, m_b_ada, m_w_in, m_w_conv_a, m_w_pool, m_pool_scale, m_w_dw_c, m_b_dw_c, m_ln_g_c, m_ln_b_c, m_w_pw2_c, m_b_pw2_c, m_ln_g_d, m_ln_b_d, m_w_s_d, m_b_s_d, m_w_out, m_final_g, v_norm_g, v_w_ada, v_b_ada, v_w_in, v_w_conv_a, v_w_pool, v_pool_scale, v_w_dw_c, v_b_dw_c, v_ln_g_c, v_ln_b_c, v_w_pw2_c, v_b_pw2_c, v_ln_g_d, v_ln_b_d, v_w_s_d, v_b_s_d, v_w_out, v_final_g):
    given = dict(locals())
    shard = {n: given[n] for n in WEIGHTS}
    mom_m = {n: given['m_' + n] for n in WEIGHTS}
    mom_v = {n: given['v_' + n] for n in WEIGHTS}
    me = 4 * lax.axis_index("x") + 2 * lax.axis_index("y") + lax.axis_index("c")
    n_tok = x.shape[1]
    tile = min(TOKEN_TILE, n_tok)
    wide_tile = min(2 * TOKEN_TILE, n_tok)
    x0 = x.reshape(n_tok, D_MODEL)
    target = loss_target.reshape(n_tok, D_MODEL)
    ada_cols = w_ada.shape[2]

    first_shards = _weight_shards(shard, 0)
    c_all, w_in_first = _first_gather(c, first_shards[:1], GATHER_RULES[:1])

    b_cols = lax.dynamic_slice_in_dim(b_ada, me * ada_cols, ada_cols, axis=1)
    c_act, mod_cols = _modulation_columns(c_all.reshape(N_DEV, D_MODEL), w_ada, b_cols)
    (mod_all,) = _exchange("gather_modulation", _plans([mod_cols], [_gather]))
    mod = lax.dynamic_index_in_dim(mod_all, me, axis=2, keepdims=False)
    mod = mod.transpose(1, 0, 2).reshape(N_LAYERS, 3 * D_MODEL)
    shift, scale, gate = (mod[:, k * D_MODEL:(k + 1) * D_MODEL].reshape(N_LAYERS, 1, D_MODEL) for k in range(3))
    gs = norm_g.reshape(N_LAYERS, 1, D_MODEL) * (1.0 + scale)

    xs, hs, zs, ocs, layers = [x0], [], [], [], []
    for l in range(N_LAYERS):
        if l == 0:
            (h, z), rest = _in_proj(xs[0], gs[0], shift[0], w_in_first, wide_tile,
                                    ride=_plans(first_shards[1:], GATHER_RULES[1:]))
            layers.append(_layer_weights(shard, 0, [w_in_first, *rest]))
        else:
            (h, z), _ = _in_proj(xs[l], gs[l], shift[l], layers[l][0], wide_tile)
        _, w_out_b, small, _ = layers[l]
        hs.append(h)
        zs.append(z)
        if l + 1 < N_LAYERS:
            (x_next, o_c), gathered = _mix_out(z, xs[l], gate[l], small, w_out_b, tile,
                                               ride=_plans(_weight_shards(shard, l + 1), GATHER_RULES))
            xs.append(x_next)
            layers.append(_layer_weights(shard, l + 1, gathered))
        else:
            (dx, o_c, loss_part, dfinal_g), _ = _mix_out(z, xs[l], gate[l], small, w_out_b, tile,
                                                         head=(final_g.reshape(1, D_MODEL), target))
        ocs.append(o_c)

    part = {}
    layer_parts = [None] * N_LAYERS
    slots = [None] * N_LAYERS
    for l in reversed(range(N_LAYERS)):
        w_in_b, w_out_b, small, small_t = layers[l]
        ride = _plans(layer_parts[l + 1]['big'], SCATTER_RULES) if l + 1 < N_LAYERS else None
        (dz, ycat, sums, dwp, dw2, dws, dbs), rode = _mix_bwd(zs[l], ocs[l], dx, gate[l], small, small_t, w_out_b, tile,
                                                              ride=ride)
        if ride:
            slots[l + 1] = rode
        m_out, _ = _tokens_matmul(ycat, dx, "out_proj_tokens_matmul")
        dw_out, dgate = _out_proj_grads(m_out, w_out_b, gate[l])
        if l > 0:
            dw_in, _ = _tokens_matmul(hs[l], dz, "in_proj_tokens_matmul", out_dtype=BF16)
        else:
            dw_in, (slots_out, slots_pw2) = _tokens_matmul(
                hs[l], dz, "in_proj_tokens_matmul", out_dtype=BF16, a_cols=(0, D_MODEL // 2),
                ride=_plans([dw_out, dw2], SCATTER_RULES[1:]))
            dw_in_last, (slots_in,) = _tokens_matmul(
                hs[l], dz, "in_proj_tokens_matmul", out_dtype=BF16, a_cols=(1, D_MODEL // 2),
                ride=_plans([dw_in], SCATTER_RULES[:1]))
            slots[l] = [slots_in, slots_out, slots_pw2]
        (dx, dshift, dgs), _ = _norm_bwd(xs[l], dz, dx, gs[l], w_in_b, tile)
        layer_parts[l] = dict(
            big=[dw_in, dw_out, dw2],
            b_ada=jnp.concatenate([dshift, dgs * norm_g[l][None], dgate], axis=1)[0],
            norm_g=(dgs * (1.0 + scale[l]))[0], sums=sums,
            w_pool=jnp.einsum('gchd,gh->gcd', dwp.reshape(4, 64, 4, 64), jnp.eye(4, dtype=F32)),
            w_s_d=dws.reshape(4, SUB, SUB) * jnp.tril(jnp.ones((SUB, SUB), F32)),
            b_s_d=dbs.reshape(SUB, 4, 64).sum(axis=-1).T)
    grad_x = dx.reshape(x.shape)
    small_names = REPLICATED + CHANNEL_SHARDED
    packed_names = [n for n in small_names if n not in SUM_ROWS] + ['sums']
    for n in packed_names:
        part[n] = dfinal_g[0] if n == 'final_g' else jnp.stack([layer_parts[l][n] for l in range(N_LAYERS)])

    small_shapes = [part[n].shape for n in packed_names] + [(1, 128)]
    slots_in_last, small_sum, dmod_all = _finish_exchange(
        [dw_in_last], SCATTER_RULES[:1],
        _pack([part[n] for n in packed_names] + [loss_part[0:1]], rows_multiple=8 * N_DEV), part['b_ada'])

    grads, deltas, new_m, new_v = {}, {}, {}, {}
    half = D_MODEL // 2
    for j, n in enumerate(('w_in', 'w_out', 'w_pw2_c')):
        outs = [_adam_update(shard[n][l], slots[l][j], mom_m[n][l], mom_v[n][l], "update_" + n)
                for l in range(1, N_LAYERS)]
        if n == 'w_in':
            halves = [_adam_update(shard[n][0][rows], s, mom_m[n][0][rows], mom_v[n][0][rows], "update_" + n)
                      for rows, s in ((slice(0, half), slots[0][0]), (slice(half, None), slots_in_last))]
            outs.insert(0, [jnp.concatenate(o) for o in zip(*halves)])
        else:
            outs.insert(0, _adam_update(shard[n][0], slots[0][j], mom_m[n][0], mom_v[n][0], "update_" + n))
        grads[n], deltas[n], new_m[n], new_v[n] = (jnp.stack(o) for o in zip(*outs))

    *small_sums, loss_sum = _unpack(small_sum, small_shapes)
    loss = loss_sum[0, 0]
    gsum = dict(zip(packed_names, small_sums))
    for n, rows in SUM_ROWS.items():
        gsum[n] = gsum['sums'][:, rows]
    for n in CHANNEL_SHARDED:
        width = shard[n].shape[2]
        gsum[n] = lax.dynamic_slice_in_dim(gsum[n], me * width, width, axis=2)
    d_small, m_small, v_small = _adam_many(*[[_as_rows(d[n]) for n in small_names] for d in (shard, gsum, mom_m, mom_v)],
                                           "update_small")
    for j, n in enumerate(small_names):
        grads[n] = gsum[n]
        deltas[n], new_m[n], new_v[n] = (o[j].reshape(shard[n].shape) for o in (d_small, m_small, v_small))

    dmod_cols = lax.dynamic_slice_in_dim(dmod_all, me * ada_cols, ada_cols, axis=2).transpose(1, 0, 2)
    grads['w_ada'], deltas['w_ada'], new_m['w_ada'], new_v['w_ada'] = _ada_update(
        c_act.T, dmod_cols, w_ada, m_w_ada, v_w_ada)

    return (loss, grad_x, *[grads[n] for n in WEIGHTS], *[deltas[n] for n in WEIGHTS],
            *[new_m[n] for n in WEIGHTS], *[new_v[n] for n in WEIGHTS])
```

```python
import functools
import math

import jax
import jax.numpy as jnp
from jax import lax
from jax.experimental import pallas as pl
from jax.experimental.pallas import tpu as pltpu

F32 = jnp.float32
BF16 = jnp.bfloat16

N_DEV = 8
D_MODEL = 1024
GROUP = 256
D_IN = 12 * GROUP
N_LAYERS = 2
HALO = 32
SUB = 128
WIN = SUB + HALO
TOKEN_TILE = 512
REDUCE_TILE = 2048
EPS = 1e-6
VMEM_BYTES_V7X = 64 * 1024 * 1024
VMEM_LIMIT = VMEM_BYTES_V7X - 8 * 1024 * 1024

ADAM_LR = 0.001
ADAM_B1 = 0.9
ADAM_B2 = 0.999
ADAM_EPS = 1e-08
ADAM_WD = 0.01
ADAM_STEP = 10

A_B, A_C, A_X, A_G, B_P, B_G, C_A, C_GL, C_G, D_U, D_V, D_G = range(12)
V_PSCALE, V_BDW, V_LNGC, V_LNBC, V_BPW2, V_LNGD, V_LNBD = range(7)
S_WCONV, S_PSCALE, S_BDW, S_LNGC, S_LNBC, S_BPW2, S_LNGD, S_LNBD, S_WDW = 0, 3, 4, 5, 6, 7, 8, 9, 16
N_SUMS = 64
CONV_A = 3
CONV_C = 31
SUM_ROWS = dict(w_conv_a=slice(S_WCONV, S_WCONV + CONV_A), w_dw_c=slice(S_WDW, S_WDW + CONV_C), pool_scale=S_PSCALE,
                b_dw_c=S_BDW, ln_g_c=S_LNGC, ln_b_c=S_LNBC, b_pw2_c=S_BPW2, ln_g_d=S_LNGD, ln_b_d=S_LNBD)

WEIGHTS = ('norm_g', 'w_ada', 'b_ada', 'w_in', 'w_conv_a', 'w_pool', 'pool_scale', 'w_dw_c', 'b_dw_c', 'ln_g_c',
           'ln_b_c', 'w_pw2_c', 'b_pw2_c', 'ln_g_d', 'ln_b_d', 'w_s_d', 'b_s_d', 'w_out', 'final_g')
REPLICATED = ('norm_g', 'b_ada', 'w_pool', 'pool_scale', 'b_dw_c', 'ln_g_c', 'ln_b_c', 'b_pw2_c', 'ln_g_d', 'ln_b_d',
              'w_s_d', 'b_s_d', 'final_g')
CHANNEL_SHARDED = ('w_conv_a', 'w_dw_c')


def _params(semantics, vmem=VMEM_LIMIT):
    return pltpu.CompilerParams(dimension_semantics=semantics, vmem_limit_bytes=vmem)


def _cols(g):
    return slice(g * GROUP, (g + 1) * GROUP)


def _full(shape):
    return pl.BlockSpec(shape, lambda *_: (0,) * len(shape))


def _silu(x):
    s = jax.nn.sigmoid(x)
    return x * s, s


def _dsilu(sg, s):
    return s + sg * (1.0 - s)


_GELU_C0 = math.sqrt(2.0 / math.pi)
_GELU_C1 = 0.044715


def _gelu(x):
    x2 = x * x
    th = jnp.tanh(_GELU_C0 * (x + _GELU_C1 * (x * x2)))
    p = 0.5 + 0.5 * th
    return x * p, (th, p, x2)


def _dgelu(x, aux):
    th, p, x2 = aux
    return p + (0.5 * x) * (1.0 - th * th) * (_GELU_C0 + (3.0 * _GELU_C0 * _GELU_C1) * x2)


def _layer_norm(x):
    mu = jnp.mean(x, axis=-1, keepdims=True)
    xc = x - mu
    rstd = lax.rsqrt(jnp.mean(xc * xc, axis=-1, keepdims=True) + EPS)
    return xc * rstd, rstd


def _layer_norm_bwd(dn, n, rstd):
    return rstd * (dn - jnp.mean(dn, axis=-1, keepdims=True) - n * jnp.mean(dn * n, axis=-1, keepdims=True))


def _shift_rows(a, k):
    k = k % a.shape[0]
    return a if k == 0 else pltpu.roll(a, k, 0)


def _row_sum8(a):
    s = a[0:8]
    for m in range(1, a.shape[0] // 8):
        s = s + a[8 * m:8 * m + 8]
    return s


def _lane():
    return lax.broadcasted_iota(jnp.int32, (SUB, GROUP), 1)


def _by_quarter(lane, parts):
    return jnp.where(lane < 64, parts[0], jnp.where(lane < 128, parts[1], jnp.where(lane < 192, parts[2], parts[3])))


def _conv_inputs(z_ref, rows):
    def f(g):
        return z_ref[rows, _cols(g)].astype(F32)
    return f(A_C) * f(A_X), f(B_P), f(C_A) * jax.nn.sigmoid(f(C_GL))


def _fill_past(past_ref, zh_ref, zm_ref, is_first, tile):
    parts = _conv_inputs(zh_ref, slice(None))
    for n, a in enumerate(parts):
        past_ref[0:HALO, _cols(n)] = jnp.where(is_first, 0.0, a)

    def body(j, carry):
        r0 = pl.multiple_of(j * SUB, SUB)
        for n, a in enumerate(_conv_inputs(zm_ref, pl.ds(r0, SUB))):
            past_ref[pl.ds(r0 + HALO, SUB), _cols(n)] = a
        return carry

    lax.fori_loop(0, tile // SUB, body, 0)


def _short_conv_taps(qw):
    return [_shift_rows(qw, CONV_A - 1 - k)[HALO:WIN] for k in range(CONV_A)]


def _doubling_sums(w, back, keep):
    n = w.shape[0]
    half = GROUP // 2
    lane = lax.broadcasted_iota(jnp.int32, (SUB, half), 1)

    def grow(s, k):
        return s + _shift_rows(s, k if back else n - k)

    lo2 = grow(w[:, :half], 1)
    lo4 = grow(lo2, 2)
    hi8 = grow(grow(grow(w[:, half:], 1), 2), 4)
    hi16 = grow(hi8, 8)
    return jnp.concatenate([jnp.where(lane < 64, lo2[keep], lo4[keep]), jnp.where(lane < 64, hi8[keep], hi16[keep])],
                           axis=1)


def _window_sums(pw):
    return _doubling_sums(pw, True, slice(HALO, WIN))


def _forward_window_sums(ew):
    return _doubling_sums(ew, False, slice(0, SUB))


def _inv_count(bs_ref, t_first):
    return jnp.where(t_first == 0, bs_ref[SUB:2 * SUB, :], bs_ref[2 * SUB:2 * SUB + 1, :])


def _mixer_forwards(zc, win, t_first, wc_ref, wdw_ref, vec_ref, wp_ref, w2_ref, ws_ref, bs_ref, o_c=None):
    def vec(n):
        return vec_ref[n:n + 1, :]

    def short_conv():
        taps = _short_conv_taps(win(0))
        o_a = wc_ref[0:1, :] * taps[0] + wc_ref[1:2, :] * taps[1] + wc_ref[2:3, :] * taps[2]
        a_b, a_g = zc(A_B), zc(A_G)
        sg_a, s_a = _silu(a_g)
        return a_b * o_a * sg_a, dict(taps=taps, o_a=o_a, a_b=a_b, a_g=a_g, sg_a=sg_a, s_a=s_a)

    def pooling():
        pw = win(1)
        ic = _inv_count(bs_ref, t_first)
        pooled_b = (_window_sums(pw) * ic - pw[HALO:WIN]).astype(BF16)
        y0_b = jnp.dot(pooled_b, wp_ref[...], preferred_element_type=F32)
        b_g = zc(B_G)
        sg_b, s_b = _silu(b_g)
        return y0_b * vec(V_PSCALE) * sg_b, dict(ic=ic, pooled_b=pooled_b, y0_b=y0_b, b_g=b_g, sg_b=sg_b, s_b=s_b)

    def conformer():
        hw = win(2)
        o = o_c
        if o is None:
            o = wdw_ref[CONV_C - 1:CONV_C, :] * hw[HALO:WIN] + vec(V_BDW)
            for k in range(CONV_C - 1):
                o = o + wdw_ref[k:k + 1, :] * _shift_rows(hw, CONV_C - 1 - k)[HALO:WIN]
        n_c, rstd_c = _layer_norm(o)
        ln_c = n_c * vec(V_LNGC) + vec(V_LNBC)
        sl_c, ssl_c = _silu(ln_c)
        sl_b = sl_c.astype(BF16)
        yc = jnp.dot(sl_b, w2_ref[...], preferred_element_type=F32) + vec(V_BPW2)
        c_g = zc(C_G)
        sg_c, s_c = _silu(c_g)
        return yc * sg_c, dict(hw=hw, o_c=o, n_c=n_c, rstd_c=rstd_c, sl_c=sl_c, ssl_c=ssl_c, sl_b=sl_b, yc=yc, c_g=c_g,
                               sg_c=sg_c, s_c=s_c)

    def gating():
        lane = _lane()
        d_u, d_v, d_g = zc(D_U), zc(D_V), zc(D_G)
        u, aux_u = _gelu(d_u)
        gv, aux_v = _gelu(d_v)
        n_d, rstd_d = _layer_norm(gv)
        v_b = (n_d * vec(V_LNGD) + vec(V_LNBD)).astype(BF16)
        r = jnp.dot(ws_ref[...], v_b, preferred_element_type=F32)
        mixed = _by_quarter(lane, [r[h * SUB:(h + 1) * SUB] for h in range(4)]) + bs_ref[0:SUB, :]
        sg_d, s_d = _silu(d_g)
        return u * mixed * sg_d, dict(d_u=d_u, d_v=d_v, u=u, aux_u=aux_u, aux_v=aux_v, n_d=n_d, rstd_d=rstd_d,
                                      v_b=v_b, mixed=mixed, sg_d=sg_d, s_d=s_d)

    return short_conv, pooling, conformer, gating


def _in_proj(x, gs, shift, w_in_b, tile, ride=None):
    n_tok = x.shape[0]

    def body(x_ref, gs_ref, sh_ref, w_ref, h_ref, z_ref):
        xv = x_ref[...]
        r = lax.rsqrt(jnp.mean(xv * xv, axis=-1, keepdims=True) + EPS)
        h = ((xv * r) * gs_ref[...] + sh_ref[...]).astype(BF16)
        h_ref[...] = h
        for j in range(D_IN // D_MODEL):
            cs = slice(j * D_MODEL, (j + 1) * D_MODEL)
            z_ref[:, cs] = jnp.dot(h, w_ref[:, cs], preferred_element_type=F32).astype(BF16)

    return _tiled_call(
        body, (x, gs, shift, w_in_b), name="in_proj", grid=(n_tok // tile,),
        in_specs=[pl.BlockSpec((tile, D_MODEL), lambda i: (i, 0)), _full((1, D_MODEL)), _full((1, D_MODEL)),
                  _full((D_MODEL, D_IN))],
        out_specs=[pl.BlockSpec((tile, D_MODEL), lambda i: (i, 0)), pl.BlockSpec((tile, D_IN), lambda i: (i, 0))],
        out_shape=[jax.ShapeDtypeStruct((n_tok, D_MODEL), BF16), jax.ShapeDtypeStruct((n_tok, D_IN), BF16)],
        ride=ride)


def _small_specs(with_transposes):
    specs = [_full((8, GROUP)), _full((HALO, GROUP)), _full((16, GROUP)), _full((GROUP, GROUP)), _full((GROUP, GROUP)),
             _full((4 * SUB, SUB)), _full((2 * SUB + 8, GROUP))]
    if with_transposes:
        specs += [_full((GROUP, GROUP)), _full((GROUP, GROUP)), _full((4 * SUB, SUB))]
    return specs


def _mix_out(z, x, gate, small, w_out_b, tile, ride=None, head=None):
    n_tok = x.shape[0]
    n_tiles = n_tok // tile
    n_sub = tile // SUB
    cw = D_MODEL // n_sub
    per_halo = tile // HALO
    n_in = 12 + (2 if head else 0)
    n_out = 4 if head else 2

    def cur(i):
        return jnp.minimum(i, n_tiles - 1)

    def prev(i):
        return jnp.maximum(i - 1, 0)

    def body(*refs):
        (zm_ref, zh_ref, x_ref, gate_ref, wc_ref, wdw_ref, vec_ref, wp_ref, w2_ref, ws_ref, bs_ref, wout_ref) = refs[:12]
        xo_ref, oc_ref = refs[n_in:n_in + 2]
        past_ref, ycat_ref, ycat_prev_ref = refs[n_in + n_out:n_in + n_out + 3]
        i = pl.program_id(0)
        t = cur(i)
        if head:
            g_ref, tgt_ref = refs[12:14]
            loss_ref, dg_ref = refs[n_in + 2:n_in + 4]
            xn_ref, acc_ref = refs[n_in + n_out + 3:]
        else:
            xn_ref = xo_ref

        @pl.when(i == 0)
        def _():
            ycat_prev_ref[...] = jnp.zeros_like(ycat_prev_ref)
            if head:
                acc_ref[...] = jnp.zeros_like(acc_ref)

        _fill_past(past_ref, zh_ref, zm_ref, t == 0, tile)
        for j in range(n_sub):
            cs = slice(j * cw, (j + 1) * cw)
            y = jnp.dot(ycat_prev_ref[...], wout_ref[:, cs], preferred_element_type=F32)
            xn_ref[:, cs] = x_ref[:, cs] + gate_ref[:, cs] * y
            rows = slice(j * SUB, (j + 1) * SUB)
            mixers = _mixer_forwards(
                lambda g: zm_ref[rows, _cols(g)].astype(F32), lambda n: past_ref[j * SUB:j * SUB + WIN, _cols(n)],
                t * tile + j * SUB, wc_ref, wdw_ref, vec_ref, wp_ref, w2_ref, ws_ref, bs_ref)
            for n, mixer in enumerate(mixers):
                y, s = mixer()
                ycat_ref[rows, _cols(n)] = y.astype(BF16)
                if "o_c" in s:
                    oc_ref[rows, :] = s["o_c"]
        ycat_prev_ref[...] = ycat_ref[...]
        if head:
            counted = jnp.where(i > 0, 1.0, 0.0)
            xo_ref[...] = _loss_head_block(xn_ref[...], g_ref[...], tgt_ref[...], acc_ref, counted)

            @pl.when(i == n_tiles)
            def _():
                loss_ref[...] = jnp.full((8, 128), 0.5 / D_MODEL, F32) * jnp.sum(acc_ref[0])
                dg_ref[...] = jnp.sum(acc_ref[1], axis=0, keepdims=True)

    in_specs = [pl.BlockSpec((tile, D_IN), lambda i: (cur(i), 0)),
                pl.BlockSpec((HALO, D_IN), lambda i: (jnp.maximum(cur(i) * per_halo - 1, 0), 0)),
                pl.BlockSpec((tile, D_MODEL), lambda i: (prev(i), 0)), _full((1, D_MODEL)),
                *_small_specs(False), _full((D_MODEL, D_MODEL))]
    out_specs = [pl.BlockSpec((tile, D_MODEL), lambda i: (prev(i), 0)), pl.BlockSpec((tile, GROUP), lambda i: (cur(i), 0))]
    out_shape = [jax.ShapeDtypeStruct((n_tok, D_MODEL), F32), jax.ShapeDtypeStruct((n_tok, GROUP), F32)]
    scratch = [pltpu.VMEM((tile + HALO, 3 * GROUP), F32), pltpu.VMEM((tile, D_MODEL), BF16), pltpu.VMEM((tile, D_MODEL), BF16)]
    args = (z, z, x, gate, *small, w_out_b)
    if head:
        in_specs += [_full((1, D_MODEL)), pl.BlockSpec((tile, D_MODEL), lambda i: (prev(i), 0))]
        out_specs += [_full((8, 128)), _full((1, D_MODEL))]
        out_shape += [jax.ShapeDtypeStruct((8, 128), F32), jax.ShapeDtypeStruct((1, D_MODEL), F32)]
        scratch += [pltpu.VMEM((tile, D_MODEL), F32), pltpu.VMEM((2, 8, D_MODEL), F32)]
        args += tuple(head)
    outs, rode = _tiled_call(body, args, name="mix_out", grid=(n_tiles + 1,), in_specs=in_specs, out_specs=out_specs,
                             out_shape=out_shape, scratch_shapes=scratch, ride=ride)
    return outs, rode


def _loss_head_block(xv, g, target, acc_ref, counted):
    r = lax.rsqrt(jnp.mean(xv * xv, axis=-1, keepdims=True) + EPS)
    xn = xv * r
    err = xn * g - target
    acc_ref[0] = acc_ref[0] + counted * _row_sum8(err * err)
    dy = err * (1.0 / D_MODEL)
    acc_ref[1] = acc_ref[1] + counted * _row_sum8(dy * xn)
    a = dy * g
    return r * (a - xn * jnp.mean(a * xn, axis=-1, keepdims=True))


def _mix_bwd(z, o_c, dx_next, gate, small, small_t, w_out_b, tile, ride=None):
    n_tok = z.shape[0]
    n_tiles = n_tok // tile
    n_sub = tile // SUB
    cw = D_MODEL // n_sub
    per_halo = tile // HALO
    nt_dims = (((1,), (1,)), ((), ()))

    def tile_of(i):
        return n_tiles - 1 - i

    def next_tile_of(i):
        return jnp.maximum(n_tiles - 2 - i, 0)

    def body(zm_ref, zh_ref, oc_ref, dxn_ref, dxn_next_ref, gate_ref, wc_ref, wdw_ref, vec_ref, wp_ref, w2_ref, ws_ref,
             bs_ref, wpt_ref, w2t_ref, wst_ref, wout_ref,
             dz_ref, ycat_ref, sums_ref, dwp_ref, dw2_ref, dws_ref, dbs_ref,
             past_ref, future_ref, dy_ref, dy_next_ref, acc_ref):
        i = pl.program_id(0)
        t = tile_of(i)

        @pl.when(i == 0)
        def _():
            acc_ref[...] = jnp.zeros_like(acc_ref)
            dwp_ref[...] = jnp.zeros_like(dwp_ref)
            dw2_ref[...] = jnp.zeros_like(dw2_ref)
            dws_ref[...] = jnp.zeros_like(dws_ref)
            dbs_ref[...] = jnp.zeros_like(dbs_ref)
            future_ref[tile:tile + HALO, :] = jnp.zeros((HALO, 3 * GROUP), F32)
            dy_ref[...] = lax.dot_general((dxn_ref[...] * gate_ref[...]).astype(BF16), wout_ref[...], nt_dims,
                                        preferred_element_type=F32)

        _fill_past(past_ref, zh_ref, zm_ref, t == 0, tile)
        dyb_next = (dxn_next_ref[...] * gate_ref[...]).astype(BF16)

        def vec(n):
            return vec_ref[n:n + 1, :]

        for jj in range(n_sub):
            j = n_sub - 1 - jj
            r0 = j * SUB
            rows = slice(r0, r0 + SUB)

            def zc(g):
                return zm_ref[rows, _cols(g)].astype(F32)

            def add(n, a):
                acc_ref[n] = acc_ref[n] + _row_sum8(a)

            def put(g, a):
                dz_ref[rows, _cols(g)] = a.astype(BF16)

            def future_window(n, a):
                future_ref[rows, _cols(n)] = a
                return future_ref[r0:r0 + WIN, _cols(n)]

            short_conv, pooling, conformer, gating = _mixer_forwards(
                zc, lambda n: past_ref[r0:r0 + WIN, _cols(n)], t * tile + r0,
                wc_ref, wdw_ref, vec_ref, wp_ref, w2_ref, ws_ref, bs_ref, o_c=oc_ref[rows, :])
            lane = _lane()
            ks = slice(jj * cw, (jj + 1) * cw)
            dy_next_ref[:, ks] = lax.dot_general(dyb_next, wout_ref[ks, :], nt_dims, preferred_element_type=F32)

            y, s = short_conv()
            ycat_ref[rows, _cols(0)] = y.astype(BF16)
            dy = dy_ref[rows,_cols(0)]
            put(A_B, dy * s["o_a"] * s["sg_a"])
            put(A_G, dy * s["a_b"] * s["o_a"] * _dsilu(s["sg_a"], s["s_a"]))
            do = dy * s["a_b"] * s["sg_a"]
            for k in range(CONV_A):
                add(S_WCONV + k, do * s["taps"][k])
            dow = future_window(0, do)
            dq = wc_ref[CONV_A - 1:CONV_A, :] * dow[0:SUB]
            for k in range(CONV_A - 1):
                dq = dq + wc_ref[k:k + 1, :] * _shift_rows(dow, WIN - (CONV_A - 1 - k))[0:SUB]
            put(A_C, dq * zc(A_X))
            put(A_X, dq * zc(A_C))

            y, s = pooling()
            ycat_ref[rows, _cols(1)] = y.astype(BF16)
            dy = dy_ref[rows,_cols(1)]
            put(B_G, dy * (s["y0_b"] * vec(V_PSCALE)) * _dsilu(s["sg_b"], s["s_b"]))
            dyb = dy * s["sg_b"]
            add(S_PSCALE, dyb * s["y0_b"])
            dpw_b = (dyb * vec(V_PSCALE)).astype(BF16)
            dwp_ref[...] += lax.dot_general(s["pooled_b"], dpw_b, (((0,), (0,)), ((), ())), preferred_element_type=F32)
            dpooled = jnp.dot(dpw_b, wpt_ref[...], preferred_element_type=F32)
            ew = future_window(1, dpooled * s["ic"])
            put(B_P, _forward_window_sums(ew) - dpooled)

            y, s = conformer()
            ycat_ref[rows, _cols(2)] = y.astype(BF16)
            dy = dy_ref[rows,_cols(2)]
            put(C_G, dy * s["yc"] * _dsilu(s["sg_c"], s["s_c"]))
            dyc = dy * s["sg_c"]
            add(S_BPW2, dyc)
            dyc_b = dyc.astype(BF16)
            dw2_ref[...] += lax.dot_general(s["sl_b"], dyc_b, (((0,), (0,)), ((), ())), preferred_element_type=F32)
            dln = jnp.dot(dyc_b, w2t_ref[...], preferred_element_type=F32) * _dsilu(s["sl_c"], s["ssl_c"])
            add(S_LNGC, dln * s["n_c"])
            add(S_LNBC, dln)
            do = _layer_norm_bwd(dln * vec(V_LNGC), s["n_c"], s["rstd_c"])
            add(S_BDW, do)
            hw = s["hw"]
            for k in range(CONV_C):
                add(S_WDW + k, do * _shift_rows(hw, CONV_C - 1 - k)[HALO:WIN])
            dow = future_window(2, do)
            dhc = wdw_ref[CONV_C - 1:CONV_C, :] * dow[0:SUB]
            for k in range(CONV_C - 1):
                dhc = dhc + wdw_ref[k:k + 1, :] * _shift_rows(dow, WIN - (CONV_C - 1 - k))[0:SUB]
            c_a = zc(C_A)
            sgl = jax.nn.sigmoid(zc(C_GL))
            put(C_A, dhc * sgl)
            put(C_GL, dhc * c_a * sgl * (1.0 - sgl))

            y, s = gating()
            ycat_ref[rows, _cols(3)] = y.astype(BF16)
            dy = dy_ref[rows,_cols(3)]
            put(D_G, dy * s["u"] * s["mixed"] * _dsilu(s["sg_d"], s["s_d"]))
            put(D_U, dy * s["mixed"] * s["sg_d"] * _dgelu(s["d_u"], s["aux_u"]))
            dmixed = dy * s["u"] * s["sg_d"]
            dbs_ref[...] += dmixed
            by_head = jnp.concatenate(
                [jnp.where((lane >= 64 * h) & (lane < 64 * h + 64), dmixed, 0.0) for h in range(4)], axis=0).astype(BF16)
            dws_ref[...] += lax.dot_general(by_head, s["v_b"], (((1,), (1,)), ((), ())), preferred_element_type=F32)
            rv = jnp.dot(wst_ref[...], dmixed.astype(BF16), preferred_element_type=F32)
            dv = _by_quarter(lane, [rv[h * SUB:(h + 1) * SUB] for h in range(4)])
            add(S_LNGD, dv * s["n_d"])
            add(S_LNBD, dv)
            dgv = _layer_norm_bwd(dv * vec(V_LNGD), s["n_d"], s["rstd_d"])
            put(D_V, dgv * _dgelu(s["d_v"], s["aux_v"]))

        future_ref[tile:tile + HALO, :] = future_ref[0:HALO, :]
        dy_ref[...] = dy_next_ref[...]

        @pl.when(i == n_tiles - 1)
        def _():
            for n in range(N_SUMS):
                sums_ref[n:n + 1, :] = jnp.sum(acc_ref[n], axis=0, keepdims=True)

    return _tiled_call(
        body, (z, z, o_c, dx_next, dx_next, gate, *small, *small_t, w_out_b), name="mix_bwd", grid=(n_tiles,),
        in_specs=[pl.BlockSpec((tile, D_IN), lambda i: (tile_of(i), 0)),
                  pl.BlockSpec((HALO, D_IN), lambda i: (jnp.maximum(tile_of(i) * per_halo - 1, 0), 0)),
                  pl.BlockSpec((tile, GROUP), lambda i: (tile_of(i), 0)),
                  pl.BlockSpec((tile, D_MODEL), lambda i: (tile_of(i), 0)),
                  pl.BlockSpec((tile, D_MODEL), lambda i: (next_tile_of(i), 0)), _full((1, D_MODEL)),
                  *_small_specs(True), _full((D_MODEL, D_MODEL))],
        out_specs=[pl.BlockSpec((tile, D_IN), lambda i: (tile_of(i), 0)),
                   pl.BlockSpec((tile, D_MODEL), lambda i: (tile_of(i), 0)),
                   _full((N_SUMS, GROUP)), _full((GROUP, GROUP)), _full((GROUP, GROUP)), _full((4 * SUB, SUB)),
                   _full((SUB, GROUP))],
        out_shape=[jax.ShapeDtypeStruct((n_tok, D_IN), BF16), jax.ShapeDtypeStruct((n_tok, D_MODEL), BF16),
                   jax.ShapeDtypeStruct((N_SUMS, GROUP), F32), jax.ShapeDtypeStruct((GROUP, GROUP), F32),
                   jax.ShapeDtypeStruct((GROUP, GROUP), F32), jax.ShapeDtypeStruct((4 * SUB, SUB), F32),
                   jax.ShapeDtypeStruct((SUB, GROUP), F32)],
        scratch_shapes=[pltpu.VMEM((tile + HALO, 3 * GROUP), F32), pltpu.VMEM((tile + HALO, 3 * GROUP), F32),
                        pltpu.VMEM((tile, D_MODEL), F32), pltpu.VMEM((tile, D_MODEL), F32),
                        pltpu.VMEM((N_SUMS, 8, GROUP), F32)], ride=ride)


def _norm_bwd(x, dz, dx_next, gs, w_in_b, tile, ride=None, blocks=None, begun=None, finish=True):
    n_tok = x.shape[0]
    first, n_tiles = blocks or (0, n_tok // tile)
    n_in = 5 + (2 if begun else 0)

    def body(*refs):
        x_ref, dz_ref, dxn_ref, gs_ref, w_ref = refs[:5]
        dx_ref = refs[n_in]
        acc_ref = refs[-1]
        i = pl.program_id(0)

        @pl.when(i == 0)
        def _():
            acc_ref[...] = refs[6][...] if begun else jnp.zeros_like(acc_ref)

        dh = lax.dot_general(dz_ref[...], w_ref[...], (((1,), (1,)), ((), ())), preferred_element_type=F32)
        xv = x_ref[...]
        r = lax.rsqrt(jnp.mean(xv * xv, axis=-1, keepdims=True) + EPS)
        xn = xv * r
        acc_ref[0] = acc_ref[0] + _row_sum8(dh)
        acc_ref[1] = acc_ref[1] + _row_sum8(dh * xn)
        dxn = dh * gs_ref[...]
        dx_ref[...] = dxn_ref[...] + r * (dxn - xn * jnp.mean(dxn * xn, axis=-1, keepdims=True))

        @pl.when(i == n_tiles - 1)
        def _():
            if finish:
                refs[n_in + 1][...] = jnp.sum(acc_ref[0], axis=0, keepdims=True)
                refs[n_in + 2][...] = jnp.sum(acc_ref[1], axis=0, keepdims=True)
            else:
                refs[n_in + 1][...] = acc_ref[...]

    def rows(i):
        return (first + i, 0)

    in_specs = [pl.BlockSpec((tile, D_MODEL), rows), pl.BlockSpec((tile, D_IN), rows), pl.BlockSpec((tile, D_MODEL), rows),
                _full((1, D_MODEL)), _full((D_MODEL, D_IN))]
    args = (x, dz, dx_next, gs, w_in_b)
    if begun:
        in_specs += [pl.BlockSpec(memory_space=pl.ANY), _full((2, 8, D_MODEL))]
        args += tuple(begun)
    vec = jax.ShapeDtypeStruct((1, D_MODEL), F32)
    return _tiled_call(
        body, args, name="norm_bwd", grid=(n_tiles,), in_specs=in_specs,
        out_specs=[pl.BlockSpec((tile, D_MODEL), rows)] + ([_full((1, D_MODEL))] * 2 if finish else [_full((2, 8, D_MODEL))]),
        out_shape=[jax.ShapeDtypeStruct((n_tok, D_MODEL), F32)]
        + ([vec, vec] if finish else [jax.ShapeDtypeStruct((2, 8, D_MODEL), F32)]),
        scratch_shapes=[pltpu.VMEM((2, 8, D_MODEL), F32)], ride=ride, aliases={5: 0} if begun else None)


def _tokens_matmul(a, b, name, out_dtype=F32, ride=None, a_cols=None, gated=None):
    n_tok = a.shape[0]
    a_block, ka = a_cols or (0, a.shape[1])
    nb = b.shape[1]
    tk = min(REDUCE_TILE * (4 // b.dtype.itemsize), n_tok)
    cb = min(D_MODEL, nb)
    n_steps = n_tok // tk
    n_in = 4 if gated else 2

    def body(*refs):
        a_ref, b_ref, o_ref, acc_ref = refs[0], refs[1], refs[n_in], refs[-1]
        i = pl.program_id(1)

        @pl.when(i == 0)
        def _():
            acc_ref[...] = jnp.zeros_like(acc_ref)

        acc_ref[...] += lax.dot_general(a_ref[...], b_ref[...].astype(BF16), (((0,), (0,)), ((), ())),
                                        preferred_element_type=F32)

        @pl.when(i == n_steps - 1)
        def _():
            m = acc_ref[...]
            if gated:
                w_ref, gate_ref, dgate_ref = refs[2], refs[3], refs[n_in + 1]
                o_ref[...] = (m * gate_ref[...]).astype(out_dtype)
                dgate_ref[...] = jnp.sum(m * w_ref[...].astype(F32), axis=0, keepdims=True)
            else:
                o_ref[...] = m.astype(out_dtype)

    in_specs = [pl.BlockSpec((tk, ka), lambda j, i: (i, a_block)), pl.BlockSpec((tk, cb), lambda j, i: (i, j))]
    out_specs = [pl.BlockSpec((ka, cb), lambda j, i: (0, j))]
    out_shape = [jax.ShapeDtypeStruct((ka, nb), out_dtype)]
    if gated:
        in_specs += [pl.BlockSpec((ka, cb), lambda j, i: (0, j)), pl.BlockSpec((1, cb), lambda j, i: (0, j))]
        out_specs += [pl.BlockSpec((1, cb), lambda j, i: (0, j))]
        out_shape += [jax.ShapeDtypeStruct((1, nb), F32)]
    outs, rode = _tiled_call(body, (a, b, *(gated or ())), name=name, grid=(nb // cb, n_steps), in_specs=in_specs,
                             out_specs=out_specs, out_shape=out_shape, scratch_shapes=[pltpu.VMEM((ka, cb), F32)],
                             ride=ride)
    return (outs if gated else outs[0]), rode


def _modulation_columns(c_all, w_ada, b_cols):
    cols = w_ada.shape[2]

    def body(c_ref, w_ref, b_ref, ca_ref, mod_ref):
        ca, _ = _silu(c_ref[...])
        ca_ref[...] = ca
        for l in range(N_LAYERS):
            mod_ref[l] = jnp.dot(ca, w_ref[l], precision=lax.Precision.HIGHEST, preferred_element_type=F32) + b_ref[l:l + 1, :]

    return pl.pallas_call(
        body, name="modulation_columns",
        out_shape=[jax.ShapeDtypeStruct((N_DEV, D_MODEL), F32), jax.ShapeDtypeStruct((N_LAYERS, N_DEV, cols), F32)],
        compiler_params=pltpu.CompilerParams(vmem_limit_bytes=VMEM_LIMIT),
    )(c_all, w_ada, b_cols)


def _adam(w, g, m, v):
    m2 = ADAM_B1 * m + (1.0 - ADAM_B1) * g
    v2 = ADAM_B2 * v + (1.0 - ADAM_B2) * (g * g)
    m_hat = m2 / (1.0 - ADAM_B1 ** ADAM_STEP)
    v_hat = v2 / (1.0 - ADAM_B2 ** ADAM_STEP)
    return -ADAM_LR * (m_hat / (jnp.sqrt(v_hat) + ADAM_EPS) + ADAM_WD * w), m2, v2


def _row_block(rows, cols, slots):
    target = max(8, (1 << 19) // (cols * max(slots, 1)))
    rb = rows
    while rb > target and rb % 2 == 0 and (rb // 2) % 8 == 0:
        rb //= 2
    return rb


def _adam_update(w, g, m, v, name):
    rows, cols = w.shape
    slotted = g.ndim == 3
    rb = _row_block(rows, cols, N_DEV if slotted else 1)

    def body(w_ref, g_ref, m_ref, v_ref, go_ref, d_ref, mo_ref, vo_ref):
        if slotted:
            gv = g_ref[0].astype(F32)
            for q in range(1, N_DEV):
                gv = gv + g_ref[q].astype(F32)
        else:
            gv = g_ref[...]
        go_ref[...] = gv
        d_ref[...], mo_ref[...], vo_ref[...] = _adam(w_ref[...], gv, m_ref[...], v_ref[...])

    blk = pl.BlockSpec((rb, cols), lambda i: (i, 0))
    g_blk = pl.BlockSpec((N_DEV, rb, cols), lambda i: (0, i, 0)) if slotted else blk
    return pl.pallas_call(
        body, name=name, grid=(rows // rb,),
        in_specs=[blk, g_blk, blk, blk], out_specs=[blk] * 4,
        out_shape=[jax.ShapeDtypeStruct((rows, cols), F32)] * 4,
        compiler_params=_params(("parallel",)),
    )(w, g, m, v)


def _adam_many(ws, gs, ms, vs, name):
    n = len(ws)

    def body(*refs):
        w_refs, g_refs, m_refs, v_refs, d_refs, mo_refs, vo_refs = (refs[k * n:(k + 1) * n] for k in range(7))
        for j in range(n):
            d_refs[j][...], mo_refs[j][...], vo_refs[j][...] = _adam(w_refs[j][...], g_refs[j][...], m_refs[j][...],
                                                                  v_refs[j][...])

    res = pl.pallas_call(
        body, name=name, out_shape=[jax.ShapeDtypeStruct(w.shape, F32) for w in ws] * 3,
        compiler_params=pltpu.CompilerParams(vmem_limit_bytes=VMEM_LIMIT),
    )(*ws, *gs, *ms, *vs)
    return res[:n], res[n:2 * n], res[2 * n:]


def _as_rows(a):
    return a.reshape(-1, a.shape[-1]) if a.ndim > 1 else a.reshape(1, -1)


def _ada_update(ca_t, dmod_cols, w, m, v):
    _, rows, cols = w.shape

    def body(ca_ref, dm_ref, w_ref, m_ref, v_ref, g_ref, d_ref, mo_ref, vo_ref):
        g = ca_ref[:, 0:1] * dm_ref[0, 0:1, :]
        for b in range(1, N_DEV):
            g = g + ca_ref[:, b:b + 1] * dm_ref[0, b:b + 1, :]
        g_ref[0] = g
        d_ref[0], mo_ref[0], vo_ref[0] = _adam(w_ref[0], g, m_ref[0], v_ref[0])

    blk = pl.BlockSpec((1, rows, cols), lambda l: (l, 0, 0))
    return pl.pallas_call(
        body, name="ada_update", grid=(N_LAYERS,),
        in_specs=[_full((rows, N_DEV)), pl.BlockSpec((1, N_DEV, cols), lambda l: (l, 0, 0)), blk, blk, blk],
        out_specs=[blk] * 4, out_shape=[jax.ShapeDtypeStruct(w.shape, F32)] * 4,
        compiler_params=_params(("parallel",)),
    )(ca_t, dmod_cols, w, m, v)


def _exchange_sems(n):
    return [pltpu.SemaphoreType.DMA((n, N_DEV - 1)), pltpu.SemaphoreType.DMA((n, N_DEV - 1)),
            pltpu.SemaphoreType.DMA((n,))]


def _exchange_copies(plans, srcs, outs, sems, receiving, only=None):
    send_sems, recv_sems, local_sems = sems
    x, y, c = lax.axis_index("x"), lax.axis_index("y"), lax.axis_index("c")
    me = 4 * x + 2 * y + c

    def remote(i, k, incoming):
        _, o, send, land = plans[i]
        px = 1 - x if k & 4 else x
        py = 1 - y if k & 2 else y
        pc = 1 - c if k & 1 else c
        p = 4 * px + 2 * py + pc
        return pltpu.make_async_remote_copy(
            src_ref=send(srcs[i], p), dst_ref=land(outs[o], p if incoming else me),
            send_sem=send_sems.at[i, k - 1], recv_sem=recv_sems.at[i, k - 1],
            device_id=(px, py, pc), device_id_type=pl.DeviceIdType.MESH)

    which = range(len(plans)) if only is None else only
    pairs = [(i, k) for k in range(1, N_DEV) for i in which]
    local = [pltpu.make_async_copy(plans[i][2](srcs[i], me), plans[i][3](outs[plans[i][1]], me), local_sems.at[i])
             for i in which]
    return local, [remote(i, k, False) for i, k in pairs], [remote(i, k, True) for i, k in pairs] if receiving else []


def _exchange_start(plans, srcs, outs, sems, only=None):
    local, outgoing, _ = _exchange_copies(plans, srcs, outs, sems, False, only)
    for cp in local + outgoing:
        cp.start()


def _exchange_wait(plans, srcs, outs, sems, only=None):
    local, outgoing, incoming = _exchange_copies(plans, srcs, outs, sems, True, only)
    for cp in incoming:
        cp.wait_recv()
    for cp in outgoing:
        cp.wait_send()
    for cp in local:
        cp.wait()


def _exchange(name, ride):
    out_shapes, plans = ride
    n = len(plans)
    hbm = pl.BlockSpec(memory_space=pltpu.HBM)

    def body(*refs):
        srcs, outs, sems = refs[:n], refs[n:n + len(out_shapes)], refs[n + len(out_shapes):]
        _exchange_start(plans, srcs, outs, sems)
        _exchange_wait(plans, srcs, outs, sems)

    return pl.pallas_call(
        body, name=name, in_specs=[hbm] * n, out_specs=[hbm] * len(out_shapes), out_shape=list(out_shapes),
        scratch_shapes=_exchange_sems(n),
    )(*[p[0] for p in plans])


def _first_gather(c, arrays, rules):
    n = len(arrays)
    c_shapes, c_plans = _plans([c], [_gather])
    shapes, lands = zip(*[(shape, land) for shape, _, land in (rule(a) for a, rule in zip(arrays, rules))])
    hbm = pl.BlockSpec(memory_space=pltpu.HBM)

    def body(*refs):
        c_ref, srcs, c_all_ref, outs = refs[0], refs[1:1 + n], refs[1 + n], refs[2 + n:2 + 2 * n]
        send_sems, recv_sems, local_sems = refs[2 + 2 * n:5 + 2 * n]
        c_sems = refs[5 + 2 * n:]
        x, y, core = lax.axis_index("x"), lax.axis_index("y"), lax.axis_index("c")
        me, sibling = (x, y, core), (x, y, 1 - core)
        chips = [(1 - x, y), (x, 1 - y), (1 - x, 1 - y)]

        def block(a, px, py, pc):
            return lands[a](outs[a], 4 * px + 2 * py + pc)

        def copy(a, k, origin, to, own=False):
            return pltpu.make_async_remote_copy(
                src_ref=srcs[a] if own else block(a, *origin), dst_ref=block(a, *origin),
                send_sem=send_sems.at[a, k], recv_sem=recv_sems.at[a, k], device_id=to,
                device_id_type=pl.DeviceIdType.MESH)

        _exchange_start(c_plans, [c_ref], [c_all_ref], c_sems)
        mine = [pltpu.make_async_copy(srcs[a], block(a, *me), local_sems.at[a]) for a in range(n)]
        first = [copy(a, 0, me, sibling, own=True) for a in range(n)]
        first += [copy(a, 1 + j, me, (*chip, core), own=True) for j, chip in enumerate(chips) for a in range(n)]
        for cp in mine + first:
            cp.start()
        passed = [[copy(a, 4 + j, (*chip, core), sibling) for a in range(n)] for j, chip in enumerate(chips)]
        for j, chip in enumerate(chips):
            for a in range(n):
                copy(a, 1 + j, (*chip, core), me).wait_recv()
                passed[j][a].start()
        for a in range(n):
            copy(a, 0, sibling, me).wait_recv()
        for j, chip in enumerate(chips):
            for a in range(n):
                copy(a, 4 + j, (*chip, 1 - core), me).wait_recv()
        for cp in first + [cp for row in passed for cp in row]:
            cp.wait_send()
        for cp in mine:
            cp.wait()
        _exchange_wait(c_plans, [c_ref], [c_all_ref], c_sems)

    return pl.pallas_call(
        body, name="first_gather", in_specs=[hbm] * (1 + n), out_specs=[hbm] * (1 + n),
        out_shape=[c_shapes[0], *shapes],
        scratch_shapes=[pltpu.SemaphoreType.DMA((n, N_DEV - 1)), pltpu.SemaphoreType.DMA((n, N_DEV - 1)),
                        pltpu.SemaphoreType.DMA((n,)), *_exchange_sems(1)],
    )(c, *arrays)


def _finish_exchange(big, big_rules, packed, dmod):
    n_rows = packed.shape[0]
    r = n_rows // N_DEV
    shapes, plans = _plans([*big, packed, dmod], [*big_rules, _scatter_rows, _gather])
    n_first = len(plans)
    i_small = n_first - 2
    _, send, land = _gather_rows(jax.ShapeDtypeStruct((r, 128), F32))
    plans = plans + [(None, len(shapes), send, land)]
    shapes = shapes + [jax.ShapeDtypeStruct((n_rows, 128), F32)]
    first = [i for i in range(n_first) if i != i_small]
    hbm = pl.BlockSpec(memory_space=pltpu.HBM)

    def body(*refs):
        srcs, outs = list(refs[:n_first]), refs[n_first:n_first + len(shapes)]
        parts_ref, sum_ref, local_sem = refs[n_first + len(shapes):n_first + len(shapes) + 3]
        sems = refs[n_first + len(shapes) + 3:]
        srcs.append(sum_ref)
        _exchange_start(plans, srcs, outs, sems, only=range(n_first))
        _exchange_wait(plans, srcs, outs, sems, only=[i_small])
        cp = pltpu.make_async_copy(outs[i_small], parts_ref, local_sem)
        cp.start()
        cp.wait()
        g = parts_ref[0]
        for q in range(1, N_DEV):
            g = g + parts_ref[q]
        sum_ref[...] = g
        _exchange_start(plans, srcs, outs, sems, only=[n_first])
        _exchange_wait(plans, srcs, outs, sems, only=[n_first])
        _exchange_wait(plans, srcs, outs, sems, only=first)

    res = pl.pallas_call(
        body, name="finish_exchange", in_specs=[hbm] * n_first, out_specs=[hbm] * len(shapes), out_shape=shapes,
        scratch_shapes=[pltpu.VMEM((N_DEV, r, 128), F32), pltpu.VMEM((r, 128), F32), pltpu.SemaphoreType.DMA(()),
                        *_exchange_sems(len(plans))],
    )(*big, packed, dmod)
    return (*res[:len(big)], res[-1], res[n_first - 1])


def _tiled_call(body, args, *, name, grid, in_specs, out_specs, out_shape, scratch_shapes=(), ride=None, aliases=None):
    params = _params(("arbitrary",) * len(grid))
    if ride is None:
        return pl.pallas_call(body, name=name, grid=grid, in_specs=in_specs, out_specs=out_specs, out_shape=out_shape,
                              scratch_shapes=list(scratch_shapes), input_output_aliases=aliases or {},
                              compiler_params=params)(*args), []
    shapes, plans = ride
    n_in, n_src, n_out, n_dst, n_scr = len(in_specs), len(plans), len(out_specs), len(shapes), len(scratch_shapes)
    hbm = pl.BlockSpec(memory_space=pltpu.HBM)

    def carrying(*refs):
        ins, srcs, refs = refs[:n_in], refs[n_in:n_in + n_src], refs[n_in + n_src:]
        outs, dsts, refs = refs[:n_out], refs[n_out:n_out + n_dst], refs[n_out + n_dst:]
        scratch, sems = refs[:n_scr], refs[n_scr:]
        ids = [pl.program_id(a) for a in range(len(grid))]
        first = functools.reduce(jnp.logical_and, [i == 0 for i in ids])
        last = functools.reduce(jnp.logical_and, [i == g - 1 for i, g in zip(ids, grid)])

        @pl.when(first)
        def _():
            _exchange_start(plans, srcs, dsts, sems)

        body(*ins, *outs, *scratch)

        @pl.when(last)
        def _():
            _exchange_wait(plans, srcs, dsts, sems)

    res = pl.pallas_call(
        carrying, name=name, grid=grid, in_specs=[*in_specs, *[hbm] * n_src], out_specs=[*out_specs, *[hbm] * n_dst],
        out_shape=[*out_shape, *shapes], scratch_shapes=[*scratch_shapes, *_exchange_sems(n_src)],
        input_output_aliases=aliases or {}, compiler_params=params)(*args, *[p[0] for p in plans])
    return res[:n_out], res[n_out:]


def _tail(nd, idx):
    return (slice(None),) * (nd - 2) + idx


def _gather(a):
    return jax.ShapeDtypeStruct((N_DEV,) + a.shape, a.dtype), lambda s, p: s, lambda o, q: o.at[q]


def _gather_rows(a):
    r = a.shape[-2]
    return (jax.ShapeDtypeStruct(a.shape[:-2] + (N_DEV * r, a.shape[-1]), a.dtype), lambda s, p: s,
            lambda o, q: o.at[_tail(a.ndim, (pl.ds(pl.multiple_of(q * r, r), r), slice(None)))])


def _gather_cols(a):
    c = a.shape[-1]
    return (jax.ShapeDtypeStruct(a.shape[:-1] + (N_DEV * c,), a.dtype), lambda s, p: s,
            lambda o, q: o.at[_tail(a.ndim, (slice(None), pl.ds(pl.multiple_of(q * c, c), c)))])


def _scatter_rows(a):
    r = a.shape[0] // N_DEV
    return (jax.ShapeDtypeStruct((N_DEV, r, a.shape[1]), a.dtype),
            lambda s, p: s.at[pl.ds(pl.multiple_of(p * r, r), r), :], lambda o, q: o.at[q])


def _scatter_cols(a):
    c = a.shape[1] // N_DEV
    return (jax.ShapeDtypeStruct((N_DEV, a.shape[0], c), a.dtype),
            lambda s, p: s.at[:, pl.ds(pl.multiple_of(p * c, c), c)], lambda o, q: o.at[q])


def _plans(arrays, rules):
    shapes, plans = [], []
    for o, (a, rule) in enumerate(zip(arrays, rules)):
        shape, send, land = rule(a)
        shapes.append(shape)
        plans.append((a, o, send, land))
    return shapes, plans


def _pack(pieces, rows_multiple=8):
    flat = []
    for a in pieces:
        f = a.reshape(-1)
        flat.append(jnp.pad(f, (0, (-f.shape[0]) % 128)))
    total = sum(f.shape[0] for f in flat)
    flat.append(jnp.zeros(((-total) % (128 * rows_multiple),), F32))
    return jnp.concatenate(flat).reshape(-1, 128)


def _unpack(buf, shapes, lead=()):
    flat = buf.reshape(lead + (-1,))
    out, off = [], 0
    for s in shapes:
        n = math.prod(s)
        out.append(flat[..., off:off + n].reshape(lead + tuple(s)))
        off += n + (-n) % 128
    return out


def _pad_rows(a, rows):
    return jnp.pad(a, ((0, rows - a.shape[0]), (0, 0)))


VEC_NAMES = ('pool_scale', 'b_dw_c', 'ln_g_c', 'ln_b_c', 'b_pw2_c', 'ln_g_d', 'ln_b_d')
GATHERED = ('w_in', 'w_out', 'w_pw2_c', 'w_conv_a', 'w_dw_c')
GATHER_RULES = (_gather_cols, _gather_rows, _gather_rows, _gather, _gather)
SCATTER_RULES = (_scatter_cols, _scatter_rows, _scatter_rows)


def _weight_shards(shard, l):
    return [shard[n][l].astype(BF16) if n in ('w_in', 'w_out') else shard[n][l] for n in GATHERED]


def _layer_weights(shard, l, gathered):
    w_in_b, w_out_b, w_pw2, wconv_parts, wdw_parts = gathered
    wconv = wconv_parts.transpose(1, 0, 2).reshape(CONV_A, GROUP)
    wdw = wdw_parts.transpose(1, 0, 2).reshape(CONV_C, GROUP)
    wp = jnp.einsum('gcd,gh->gchd', shard['w_pool'][l], jnp.eye(4, dtype=F32)).reshape(GROUP, GROUP)
    ws = shard['w_s_d'][l] * jnp.tril(jnp.ones((SUB, SUB), F32))
    vec = jnp.stack([shard[n][l] for n in VEC_NAMES])
    width = jnp.repeat(jnp.asarray([2.0, 4.0, 8.0, 16.0], F32), 64)[None]
    count = jnp.minimum(jnp.arange(1, SUB + 1, dtype=F32)[:, None], width)
    gating_and_counts = jnp.concatenate([jnp.repeat(shard['b_s_d'][l].T, 64, axis=1), 1.0 / count, 1.0 / width,
                                         jnp.zeros((7, GROUP), F32)])
    small = (_pad_rows(wconv, 8), _pad_rows(wdw, HALO), _pad_rows(vec, 16), wp.astype(BF16), w_pw2.astype(BF16),
             ws.reshape(4 * SUB, SUB).astype(BF16), gating_and_counts)
    small_t = (wp.T.astype(BF16), w_pw2.T.astype(BF16), ws.transpose(0, 2, 1).reshape(4 * SUB, SUB).astype(BF16))
    return w_in_b, w_out_b, small, small_t


def kernel(x, c, norm_g, w_ada, b_ada, w_in, w_conv_a, w_pool, pool_scale, w_dw_c, b_dw_c, ln_g_c, ln_b_c, w_pw2_c, b_pw2_c, ln_g_d, ln_b_d, w_s_d, b_s_d, w_out, final_g, loss_target, m_norm_g, m_w_ada, m_b_ada, m_w_in, m_w_conv_a, m_w_pool, m_pool_scale, m_w_dw_c, m_b_dw_c, m_ln_g_c, m_ln_b_c, m_w_pw2_c, m_b_pw2_c, m_ln_g_d, m_ln_b_d, m_w_s_d, m_b_s_d, m_w_out, m_final_g, v_norm_g, v_w_ada, v_b_ada, v_w_in, v_w_conv_a, v_w_pool, v_pool_scale, v_w_dw_c, v_b_dw_c, v_ln_g_c, v_ln_b_c, v_w_pw2_c, v_b_pw2_c, v_ln_g_d, v_ln_b_d, v_w_s_d, v_b_s_d, v_w_out, v_final_g):
    given = dict(locals())
    shard = {n: given[n] for n in WEIGHTS}
    mom_m = {n: given['m_' + n] for n in WEIGHTS}
    mom_v = {n: given['v_' + n] for n in WEIGHTS}
    me = 4 * lax.axis_index("x") + 2 * lax.axis_index("y") + lax.axis_index("c")
    n_tok = x.shape[1]
    tile = min(TOKEN_TILE, n_tok)
    wide_tile = min(2 * TOKEN_TILE, n_tok)
    x0 = x.reshape(n_tok, D_MODEL)
    target = loss_target.reshape(n_tok, D_MODEL)
    ada_cols = w_ada.shape[2]

    first_shards = _weight_shards(shard, 0)
    c_all, w_in_first = _first_gather(c, first_shards[:1], GATHER_RULES[:1])

    b_cols = lax.dynamic_slice_in_dim(b_ada, me * ada_cols, ada_cols, axis=1)
    c_act, mod_cols = _modulation_columns(c_all.reshape(N_DEV, D_MODEL), w_ada, b_cols)
    (mod_all,) = _exchange("gather_modulation", _plans([mod_cols], [_gather]))
    mod = lax.dynamic_index_in_dim(mod_all, me, axis=2, keepdims=False)
    mod = mod.transpose(1, 0, 2).reshape(N_LAYERS, 3 * D_MODEL)
    shift, scale, gate = (mod[:, k * D_MODEL:(k + 1) * D_MODEL].reshape(N_LAYERS, 1, D_MODEL) for k in range(3))
    gs = norm_g.reshape(N_LAYERS, 1, D_MODEL) * (1.0 + scale)

    xs, hs, zs, ocs, layers = [x0], [], [], [], []
    for l in range(N_LAYERS):
        if l == 0:
            (h, z), rest = _in_proj(xs[0], gs[0], shift[0], w_in_first, wide_tile,
                                    ride=_plans(first_shards[1:], GATHER_RULES[1:]))
            layers.append(_layer_weights(shard, 0, [w_in_first, *rest]))
        else:
            (h, z), _ = _in_proj(xs[l], gs[l], shift[l], layers[l][0], wide_tile)
        _, w_out_b, small, _ = layers[l]
        hs.append(h)
        zs.append(z)
        if l + 1 < N_LAYERS:
            (x_next, o_c), gathered = _mix_out(z, xs[l], gate[l], small, w_out_b, wide_tile,
                                               ride=_plans(_weight_shards(shard, l + 1), GATHER_RULES))
            xs.append(x_next)
            layers.append(_layer_weights(shard, l + 1, gathered))
        else:
            (dx, o_c, loss_part, dfinal_g), _ = _mix_out(z, xs[l], gate[l], small, w_out_b, tile,
                                                         head=(final_g.reshape(1, D_MODEL), target))
        ocs.append(o_c)

    part = {}
    layer_parts = [None] * N_LAYERS
    slots = [None] * N_LAYERS
    for l in reversed(range(N_LAYERS)):
        w_in_b, w_out_b, small, small_t = layers[l]
        ride = _plans(layer_parts[l + 1]['big'], SCATTER_RULES) if l + 1 < N_LAYERS else None
        (dz, ycat, sums, dwp, dw2, dws, dbs), rode = _mix_bwd(zs[l], ocs[l], dx, gate[l], small, small_t, w_out_b, tile,
                                                              ride=ride)
        if ride:
            slots[l + 1] = rode
        (dw_out, dgate), _ = _tokens_matmul(ycat, dx, "out_proj_tokens_matmul", out_dtype=BF16, gated=(w_out_b, gate[l]))
        if l > 0:
            dw_in, _ = _tokens_matmul(hs[l], dz, "in_proj_tokens_matmul", out_dtype=BF16)
        else:
            dw_in, (slots_out, slots_pw2) = _tokens_matmul(
                hs[l], dz, "in_proj_tokens_matmul", out_dtype=BF16, a_cols=(0, D_MODEL // 2),
                ride=_plans([dw_out, dw2], SCATTER_RULES[1:]))
            dw_in_last, (slots_in,) = _tokens_matmul(
                hs[l], dz, "in_proj_tokens_matmul", out_dtype=BF16, a_cols=(1, D_MODEL // 2),
                ride=_plans([dw_in], SCATTER_RULES[:1]))
            slots[l] = [slots_in, slots_out, slots_pw2]
        (dx, dshift, dgs), _ = _norm_bwd(xs[l], dz, dx, gs[l], w_in_b, tile)
        layer_parts[l] = dict(
            big=[dw_in, dw_out, dw2],
            b_ada=jnp.concatenate([dshift, dgs * norm_g[l][None], dgate], axis=1)[0],
            norm_g=(dgs * (1.0 + scale[l]))[0], sums=sums,
            w_pool=jnp.einsum('gchd,gh->gcd', dwp.reshape(4, 64, 4, 64), jnp.eye(4, dtype=F32)),
            w_s_d=dws.reshape(4, SUB, SUB) * jnp.tril(jnp.ones((SUB, SUB), F32)),
            b_s_d=dbs.reshape(SUB, 4, 64).sum(axis=-1).T)
    grad_x = dx.reshape(x.shape)
    small_names = REPLICATED + CHANNEL_SHARDED
    packed_names = [n for n in small_names if n not in SUM_ROWS] + ['sums']
    for n in packed_names:
        part[n] = dfinal_g[0] if n == 'final_g' else jnp.stack([layer_parts[l][n] for l in range(N_LAYERS)])

    small_shapes = [part[n].shape for n in packed_names] + [(1, 128)]
    slots_in_last, small_sum, dmod_all = _finish_exchange(
        [dw_in_last], SCATTER_RULES[:1],
        _pack([part[n] for n in packed_names] + [loss_part[0:1]], rows_multiple=8 * N_DEV), part['b_ada'])

    grads, deltas, new_m, new_v = {}, {}, {}, {}
    half = D_MODEL // 2
    for j, n in enumerate(('w_in', 'w_out', 'w_pw2_c')):
        outs = [_adam_update(shard[n][l], slots[l][j], mom_m[n][l], mom_v[n][l], "update_" + n)
                for l in range(1, N_LAYERS)]
        if n == 'w_in':
            halves = [_adam_update(shard[n][0][rows], s, mom_m[n][0][rows], mom_v[n][0][rows], "update_" + n)
                      for rows, s in ((slice(0, half), slots[0][0]), (slice(half, None), slots_in_last))]
            outs.insert(0, [jnp.concatenate(o) for o in zip(*halves)])
        else:
            outs.insert(0, _adam_update(shard[n][0], slots[0][j], mom_m[n][0], mom_v[n][0], "update_" + n))
        grads[n], deltas[n], new_m[n], new_v[n] = (jnp.stack(o) for o in zip(*outs))

    *small_sums, loss_sum = _unpack(small_sum, small_shapes)
    loss = loss_sum[0, 0]
    gsum = dict(zip(packed_names, small_sums))
    for n, rows in SUM_ROWS.items():
        gsum[n] = gsum['sums'][:, rows]
    for n in CHANNEL_SHARDED:
        width = shard[n].shape[2]
        gsum[n] = lax.dynamic_slice_in_dim(gsum[n], me * width, width, axis=2)
    d_small, m_small, v_small = _adam_many(*[[_as_rows(d[n]) for n in small_names] for d in (shard, gsum, mom_m, mom_v)],
                                           "update_small")
    for j, n in enumerate(small_names):
        grads[n] = gsum[n]
        deltas[n], new_m[n], new_v[n] = (o[j].reshape(shard[n].shape) for o in (d_small, m_small, v_small))

    dmod_cols = lax.dynamic_slice_in_dim(dmod_all, me * ada_cols, ada_cols, axis=2).transpose(1, 0, 2)
    grads['w_ada'], deltas['w_ada'], new_m['w_ada'], new_v['w_ada'] = _ada_update(
        c_act.T, dmod_cols, w_ada, m_w_ada, v_w_ada)

    return (loss, grad_x, *[grads[n] for n in WEIGHTS], *[deltas[n] for n in WEIGHTS],
            *[new_m[n] for n in WEIGHTS], *[new_v[n] for n in WEIGHTS])
```

```python
import functools
import math

import jax
import jax.numpy as jnp
from jax import lax
from jax.experimental import pallas as pl
from jax.experimental.pallas import tpu as pltpu

F32 = jnp.float32
BF16 = jnp.bfloat16

N_DEV = 8
D_MODEL = 1024
GROUP = 256
D_IN = 12 * GROUP
N_LAYERS = 2
HALO = 32
SUB = 128
WIN = SUB + HALO
TOKEN_TILE = 512
REDUCE_TILE = 2048
EPS = 1e-6
VMEM_BYTES_V7X = 64 * 1024 * 1024
VMEM_LIMIT = VMEM_BYTES_V7X - 8 * 1024 * 1024

ADAM_LR = 0.001
ADAM_B1 = 0.9
ADAM_B2 = 0.999
ADAM_EPS = 1e-08
ADAM_WD = 0.01
ADAM_STEP = 10

A_B, A_C, A_X, A_G, B_P, B_G, C_A, C_GL, C_G, D_U, D_V, D_G = range(12)
V_PSCALE, V_BDW, V_LNGC, V_LNBC, V_BPW2, V_LNGD, V_LNBD = range(7)
S_WCONV, S_PSCALE, S_BDW, S_LNGC, S_LNBC, S_BPW2, S_LNGD, S_LNBD, S_WDW = 0, 3, 4, 5, 6, 7, 8, 9, 16
N_SUMS = 64
CONV_A = 3
CONV_C = 31
SUM_ROWS = dict(w_conv_a=slice(S_WCONV, S_WCONV + CONV_A), w_dw_c=slice(S_WDW, S_WDW + CONV_C), pool_scale=S_PSCALE,
                b_dw_c=S_BDW, ln_g_c=S_LNGC, ln_b_c=S_LNBC, b_pw2_c=S_BPW2, ln_g_d=S_LNGD, ln_b_d=S_LNBD)

WEIGHTS = ('norm_g', 'w_ada', 'b_ada', 'w_in', 'w_conv_a', 'w_pool', 'pool_scale', 'w_dw_c', 'b_dw_c', 'ln_g_c',
           'ln_b_c', 'w_pw2_c', 'b_pw2_c', 'ln_g_d', 'ln_b_d', 'w_s_d', 'b_s_d', 'w_out', 'final_g')
REPLICATED = ('norm_g', 'b_ada', 'w_pool', 'pool_scale', 'b_dw_c', 'ln_g_c', 'ln_b_c', 'b_pw2_c', 'ln_g_d', 'ln_b_d',
              'w_s_d', 'b_s_d', 'final_g')
CHANNEL_SHARDED = ('w_conv_a', 'w_dw_c')


def _params(semantics, vmem=VMEM_LIMIT):
    return pltpu.CompilerParams(dimension_semantics=semantics, vmem_limit_bytes=vmem)


def _cols(g):
    return slice(g * GROUP, (g + 1) * GROUP)


def _full(shape):
    return pl.BlockSpec(shape, lambda *_: (0,) * len(shape))


def _silu(x):
    s = jax.nn.sigmoid(x)
    return x * s, s


def _dsilu(sg, s):
    return s + sg * (1.0 - s)


_GELU_C0 = math.sqrt(2.0 / math.pi)
_GELU_C1 = 0.044715


def _gelu(x):
    x2 = x * x
    th = jnp.tanh(_GELU_C0 * (x + _GELU_C1 * (x * x2)))
    p = 0.5 + 0.5 * th
    return x * p, (th, p, x2)


def _dgelu(x, aux):
    th, p, x2 = aux
    return p + (0.5 * x) * (1.0 - th * th) * (_GELU_C0 + (3.0 * _GELU_C0 * _GELU_C1) * x2)


def _layer_norm(x):
    mu = jnp.mean(x, axis=-1, keepdims=True)
    xc = x - mu
    rstd = lax.rsqrt(jnp.mean(xc * xc, axis=-1, keepdims=True) + EPS)
    return xc * rstd, rstd


def _layer_norm_bwd(dn, n, rstd):
    return rstd * (dn - jnp.mean(dn, axis=-1, keepdims=True) - n * jnp.mean(dn * n, axis=-1, keepdims=True))


def _shift_rows(a, k):
    k = k % a.shape[0]
    return a if k == 0 else pltpu.roll(a, k, 0)


def _row_sum8(a):
    s = a[0:8]
    for m in range(1, a.shape[0] // 8):
        s = s + a[8 * m:8 * m + 8]
    return s


def _lane():
    return lax.broadcasted_iota(jnp.int32, (SUB, GROUP), 1)


def _by_quarter(lane, parts):
    return jnp.where(lane < 64, parts[0], jnp.where(lane < 128, parts[1], jnp.where(lane < 192, parts[2], parts[3])))


def _conv_inputs(z_ref, rows):
    def f(g):
        return z_ref[rows, _cols(g)].astype(F32)
    return f(A_C) * f(A_X), f(B_P), f(C_A) * jax.nn.sigmoid(f(C_GL))


def _fill_past(past_ref, zh_ref, zm_ref, is_first, tile):
    parts = _conv_inputs(zh_ref, slice(None))
    for n, a in enumerate(parts):
        past_ref[0:HALO, _cols(n)] = jnp.where(is_first, 0.0, a)

    def body(j, carry):
        r0 = pl.multiple_of(j * SUB, SUB)
        for n, a in enumerate(_conv_inputs(zm_ref, pl.ds(r0, SUB))):
            past_ref[pl.ds(r0 + HALO, SUB), _cols(n)] = a
        return carry

    lax.fori_loop(0, tile // SUB, body, 0)


def _short_conv_taps(qw):
    return [_shift_rows(qw, CONV_A - 1 - k)[HALO:WIN] for k in range(CONV_A)]


def _doubling_sums(w, back, keep):
    n = w.shape[0]
    half = GROUP // 2
    lane = lax.broadcasted_iota(jnp.int32, (SUB, half), 1)

    def grow(s, k):
        return s + _shift_rows(s, k if back else n - k)

    lo2 = grow(w[:, :half], 1)
    lo4 = grow(lo2, 2)
    hi8 = grow(grow(grow(w[:, half:], 1), 2), 4)
    hi16 = grow(hi8, 8)
    return jnp.concatenate([jnp.where(lane < 64, lo2[keep], lo4[keep]), jnp.where(lane < 64, hi8[keep], hi16[keep])],
                           axis=1)


def _window_sums(pw):
    return _doubling_sums(pw, True, slice(HALO, WIN))


def _forward_window_sums(ew):
    return _doubling_sums(ew, False, slice(0, SUB))


def _inv_count(bs_ref, t_first):
    return jnp.where(t_first == 0, bs_ref[SUB:2 * SUB, :], bs_ref[2 * SUB:2 * SUB + 1, :])


def _mixer_forwards(zc, win, t_first, wc_ref, wdw_ref, vec_ref, wp_ref, w2_ref, ws_ref, bs_ref, o_c=None):
    def vec(n):
        return vec_ref[n:n + 1, :]

    def short_conv():
        taps = _short_conv_taps(win(0))
        o_a = wc_ref[0:1, :] * taps[0] + wc_ref[1:2, :] * taps[1] + wc_ref[2:3, :] * taps[2]
        a_b, a_g = zc(A_B), zc(A_G)
        sg_a, s_a = _silu(a_g)
        return a_b * o_a * sg_a, dict(taps=taps, o_a=o_a, a_b=a_b, a_g=a_g, sg_a=sg_a, s_a=s_a)

    def pooling():
        pw = win(1)
        ic = _inv_count(bs_ref, t_first)
        pooled_b = (_window_sums(pw) * ic - pw[HALO:WIN]).astype(BF16)
        y0_b = jnp.dot(pooled_b, wp_ref[...], preferred_element_type=F32)
        b_g = zc(B_G)
        sg_b, s_b = _silu(b_g)
        return y0_b * vec(V_PSCALE) * sg_b, dict(ic=ic, pooled_b=pooled_b, y0_b=y0_b, b_g=b_g, sg_b=sg_b, s_b=s_b)

    def conformer():
        hw = win(2)
        o = o_c
        if o is None:
            o = wdw_ref[CONV_C - 1:CONV_C, :] * hw[HALO:WIN] + vec(V_BDW)
            for k in range(CONV_C - 1):
                o = o + wdw_ref[k:k + 1, :] * _shift_rows(hw, CONV_C - 1 - k)[HALO:WIN]
        n_c, rstd_c = _layer_norm(o)
        ln_c = n_c * vec(V_LNGC) + vec(V_LNBC)
        sl_c, ssl_c = _silu(ln_c)
        sl_b = sl_c.astype(BF16)
        yc = jnp.dot(sl_b, w2_ref[...], preferred_element_type=F32) + vec(V_BPW2)
        c_g = zc(C_G)
        sg_c, s_c = _silu(c_g)
        return yc * sg_c, dict(hw=hw, o_c=o, n_c=n_c, rstd_c=rstd_c, sl_c=sl_c, ssl_c=ssl_c, sl_b=sl_b, yc=yc, c_g=c_g,
                               sg_c=sg_c, s_c=s_c)

    def gating():
        lane = _lane()
        d_u, d_v, d_g = zc(D_U), zc(D_V), zc(D_G)
        u, aux_u = _gelu(d_u)
        gv, aux_v = _gelu(d_v)
        n_d, rstd_d = _layer_norm(gv)
        v_b = (n_d * vec(V_LNGD) + vec(V_LNBD)).astype(BF16)
        r = jnp.dot(ws_ref[...], v_b, preferred_element_type=F32)
        mixed = _by_quarter(lane, [r[h * SUB:(h + 1) * SUB] for h in range(4)]) + bs_ref[0:SUB, :]
        sg_d, s_d = _silu(d_g)
        return u * mixed * sg_d, dict(d_u=d_u, d_v=d_v, u=u, aux_u=aux_u, aux_v=aux_v, n_d=n_d, rstd_d=rstd_d,
                                      v_b=v_b, mixed=mixed, sg_d=sg_d, s_d=s_d)

    return short_conv, pooling, conformer, gating


def _in_proj(x, gs, shift, w_in_b, tile, ride=None):
    n_tok = x.shape[0]

    def body(x_ref, gs_ref, sh_ref, w_ref, h_ref, z_ref):
        xv = x_ref[...]
        r = lax.rsqrt(jnp.mean(xv * xv, axis=-1, keepdims=True) + EPS)
        h = ((xv * r) * gs_ref[...] + sh_ref[...]).astype(BF16)
        h_ref[...] = h
        for j in range(D_IN // D_MODEL):
            cs = slice(j * D_MODEL, (j + 1) * D_MODEL)
            z_ref[:, cs] = jnp.dot(h, w_ref[:, cs], preferred_element_type=F32).astype(BF16)

    return _tiled_call(
        body, (x, gs, shift, w_in_b), name="in_proj", grid=(n_tok // tile,),
        in_specs=[pl.BlockSpec((tile, D_MODEL), lambda i: (i, 0)), _full((1, D_MODEL)), _full((1, D_MODEL)),
                  _full((D_MODEL, D_IN))],
        out_specs=[pl.BlockSpec((tile, D_MODEL), lambda i: (i, 0)), pl.BlockSpec((tile, D_IN), lambda i: (i, 0))],
        out_shape=[jax.ShapeDtypeStruct((n_tok, D_MODEL), BF16), jax.ShapeDtypeStruct((n_tok, D_IN), BF16)],
        ride=ride)


def _small_specs(with_transposes):
    specs = [_full((8, GROUP)), _full((HALO, GROUP)), _full((16, GROUP)), _full((GROUP, GROUP)), _full((GROUP, GROUP)),
             _full((4 * SUB, SUB)), _full((2 * SUB + 8, GROUP))]
    if with_transposes:
        specs += [_full((GROUP, GROUP)), _full((GROUP, GROUP)), _full((4 * SUB, SUB))]
    return specs


def _mix_out(z, x, gate, small, w_out_b, tile, ride=None, head=None):
    n_tok = x.shape[0]
    n_tiles = n_tok // tile
    n_sub = tile // SUB
    cw = D_MODEL // n_sub
    per_halo = tile // HALO
    n_in = 12 + (2 if head else 0)
    n_out = 4 if head else 2

    def cur(i):
        return jnp.minimum(i, n_tiles - 1)

    def prev(i):
        return jnp.maximum(i - 1, 0)

    def body(*refs):
        (zm_ref, zh_ref, x_ref, gate_ref, wc_ref, wdw_ref, vec_ref, wp_ref, w2_ref, ws_ref, bs_ref, wout_ref) = refs[:12]
        xo_ref, oc_ref = refs[n_in:n_in + 2]
        past_ref, ycat_ref, ycat_prev_ref = refs[n_in + n_out:n_in + n_out + 3]
        i = pl.program_id(0)
        t = cur(i)
        if head:
            g_ref, tgt_ref = refs[12:14]
            loss_ref, dg_ref = refs[n_in + 2:n_in + 4]
            xn_ref, acc_ref = refs[n_in + n_out + 3:]
        else:
            xn_ref = xo_ref

        @pl.when(i == 0)
        def _():
            ycat_prev_ref[...] = jnp.zeros_like(ycat_prev_ref)
            if head:
                acc_ref[...] = jnp.zeros_like(acc_ref)

        _fill_past(past_ref, zh_ref, zm_ref, t == 0, tile)
        for j in range(n_sub):
            cs = slice(j * cw, (j + 1) * cw)
            y = jnp.dot(ycat_prev_ref[...], wout_ref[:, cs], preferred_element_type=F32)
            xn_ref[:, cs] = x_ref[:, cs] + gate_ref[:, cs] * y
            rows = slice(j * SUB, (j + 1) * SUB)
            mixers = _mixer_forwards(
                lambda g: zm_ref[rows, _cols(g)].astype(F32), lambda n: past_ref[j * SUB:j * SUB + WIN, _cols(n)],
                t * tile + j * SUB, wc_ref, wdw_ref, vec_ref, wp_ref, w2_ref, ws_ref, bs_ref)
            for n, mixer in enumerate(mixers):
                y, s = mixer()
                ycat_ref[rows, _cols(n)] = y.astype(BF16)
                if "o_c" in s:
                    oc_ref[rows, :] = s["o_c"]
        ycat_prev_ref[...] = ycat_ref[...]
        if head:
            counted = jnp.where(i > 0, 1.0, 0.0)
            xo_ref[...] = _loss_head_block(xn_ref[...], g_ref[...], tgt_ref[...], acc_ref, counted)

            @pl.when(i == n_tiles)
            def _():
                loss_ref[...] = jnp.full((8, 128), 0.5 / D_MODEL, F32) * jnp.sum(acc_ref[0])
                dg_ref[...] = jnp.sum(acc_ref[1], axis=0, keepdims=True)

    in_specs = [pl.BlockSpec((tile, D_IN), lambda i: (cur(i), 0)),
                pl.BlockSpec((HALO, D_IN), lambda i: (jnp.maximum(cur(i) * per_halo - 1, 0), 0)),
                pl.BlockSpec((tile, D_MODEL), lambda i: (prev(i), 0)), _full((1, D_MODEL)),
                *_small_specs(False), _full((D_MODEL, D_MODEL))]
    out_specs = [pl.BlockSpec((tile, D_MODEL), lambda i: (prev(i), 0)), pl.BlockSpec((tile, GROUP), lambda i: (cur(i), 0))]
    out_shape = [jax.ShapeDtypeStruct((n_tok, D_MODEL), F32), jax.ShapeDtypeStruct((n_tok, GROUP), F32)]
    scratch = [pltpu.VMEM((tile + HALO, 3 * GROUP), F32), pltpu.VMEM((tile, D_MODEL), BF16), pltpu.VMEM((tile, D_MODEL), BF16)]
    args = (z, z, x, gate, *small, w_out_b)
    if head:
        in_specs += [_full((1, D_MODEL)), pl.BlockSpec((tile, D_MODEL), lambda i: (prev(i), 0))]
        out_specs += [_full((8, 128)), _full((1, D_MODEL))]
        out_shape += [jax.ShapeDtypeStruct((8, 128), F32), jax.ShapeDtypeStruct((1, D_MODEL), F32)]
        scratch += [pltpu.VMEM((tile, D_MODEL), F32), pltpu.VMEM((2, 8, D_MODEL), F32)]
        args += tuple(head)
    outs, rode = _tiled_call(body, args, name="mix_out", grid=(n_tiles + 1,), in_specs=in_specs, out_specs=out_specs,
                             out_shape=out_shape, scratch_shapes=scratch, ride=ride)
    return outs, rode


def _loss_head_block(xv, g, target, acc_ref, counted):
    r = lax.rsqrt(jnp.mean(xv * xv, axis=-1, keepdims=True) + EPS)
    xn = xv * r
    err = xn * g - target
    acc_ref[0] = acc_ref[0] + counted * _row_sum8(err * err)
    dy = err * (1.0 / D_MODEL)
    acc_ref[1] = acc_ref[1] + counted * _row_sum8(dy * xn)
    a = dy * g
    return r * (a - xn * jnp.mean(a * xn, axis=-1, keepdims=True))


def _mix_bwd(z, o_c, dx_next, gate, small, small_t, w_out_b, tile, ride=None):
    n_tok = z.shape[0]
    n_tiles = n_tok // tile
    n_sub = tile // SUB
    cw = D_MODEL // n_sub
    per_halo = tile // HALO
    nt_dims = (((1,), (1,)), ((), ()))

    def tile_of(i):
        return n_tiles - 1 - i

    def next_tile_of(i):
        return jnp.maximum(n_tiles - 2 - i, 0)

    def body(zm_ref, zh_ref, oc_ref, dxn_ref, dxn_next_ref, gate_ref, wc_ref, wdw_ref, vec_ref, wp_ref, w2_ref, ws_ref,
             bs_ref, wpt_ref, w2t_ref, wst_ref, wout_ref,
             dz_ref, ycat_ref, sums_ref, dwp_ref, dw2_ref, dws_ref, dbs_ref,
             past_ref, future_ref, dy_ref, dy_next_ref, acc_ref):
        i = pl.program_id(0)
        t = tile_of(i)

        @pl.when(i == 0)
        def _():
            acc_ref[...] = jnp.zeros_like(acc_ref)
            dwp_ref[...] = jnp.zeros_like(dwp_ref)
            dw2_ref[...] = jnp.zeros_like(dw2_ref)
            dws_ref[...] = jnp.zeros_like(dws_ref)
            dbs_ref[...] = jnp.zeros_like(dbs_ref)
            future_ref[tile:tile + HALO, :] = jnp.zeros((HALO, 3 * GROUP), F32)
            dy_ref[...] = lax.dot_general((dxn_ref[...] * gate_ref[...]).astype(BF16), wout_ref[...], nt_dims,
                                        preferred_element_type=F32)

        _fill_past(past_ref, zh_ref, zm_ref, t == 0, tile)
        dyb_next = (dxn_next_ref[...] * gate_ref[...]).astype(BF16)

        def vec(n):
            return vec_ref[n:n + 1, :]

        for jj in range(n_sub):
            j = n_sub - 1 - jj
            r0 = j * SUB
            rows = slice(r0, r0 + SUB)

            def zc(g):
                return zm_ref[rows, _cols(g)].astype(F32)

            def add(n, a):
                acc_ref[n] = acc_ref[n] + _row_sum8(a)

            def put(g, a):
                dz_ref[rows, _cols(g)] = a.astype(BF16)

            def future_window(n, a):
                future_ref[rows, _cols(n)] = a
                return future_ref[r0:r0 + WIN, _cols(n)]

            short_conv, pooling, conformer, gating = _mixer_forwards(
                zc, lambda n: past_ref[r0:r0 + WIN, _cols(n)], t * tile + r0,
                wc_ref, wdw_ref, vec_ref, wp_ref, w2_ref, ws_ref, bs_ref, o_c=oc_ref[rows, :])
            lane = _lane()
            ks = slice(jj * cw, (jj + 1) * cw)
            dy_next_ref[:, ks] = lax.dot_general(dyb_next, wout_ref[ks, :], nt_dims, preferred_element_type=F32)

            y, s = short_conv()
            ycat_ref[rows, _cols(0)] = y.astype(BF16)
            dy = dy_ref[rows,_cols(0)]
            put(A_B, dy * s["o_a"] * s["sg_a"])
            put(A_G, dy * s["a_b"] * s["o_a"] * _dsilu(s["sg_a"], s["s_a"]))
            do = dy * s["a_b"] * s["sg_a"]
            for k in range(CONV_A):
                add(S_WCONV + k, do * s["taps"][k])
            dow = future_window(0, do)
            dq = wc_ref[CONV_A - 1:CONV_A, :] * dow[0:SUB]
            for k in range(CONV_A - 1):
                dq = dq + wc_ref[k:k + 1, :] * _shift_rows(dow, WIN - (CONV_A - 1 - k))[0:SUB]
            put(A_C, dq * zc(A_X))
            put(A_X, dq * zc(A_C))

            y, s = pooling()
            ycat_ref[rows, _cols(1)] = y.astype(BF16)
            dy = dy_ref[rows,_cols(1)]
            put(B_G, dy * (s["y0_b"] * vec(V_PSCALE)) * _dsilu(s["sg_b"], s["s_b"]))
            dyb = dy * s["sg_b"]
            add(S_PSCALE, dyb * s["y0_b"])
            dpw_b = (dyb * vec(V_PSCALE)).astype(BF16)
            dwp_ref[...] += lax.dot_general(s["pooled_b"], dpw_b, (((0,), (0,)), ((), ())), preferred_element_type=F32)
            dpooled = jnp.dot(dpw_b, wpt_ref[...], preferred_element_type=F32)
            ew = future_window(1, dpooled * s["ic"])
            put(B_P, _forward_window_sums(ew) - dpooled)

            y, s = conformer()
            ycat_ref[rows, _cols(2)] = y.astype(BF16)
            dy = dy_ref[rows,_cols(2)]
            put(C_G, dy * s["yc"] * _dsilu(s["sg_c"], s["s_c"]))
            dyc = dy * s["sg_c"]
            add(S_BPW2, dyc)
            dyc_b = dyc.astype(BF16)
            dw2_ref[...] += lax.dot_general(s["sl_b"], dyc_b, (((0,), (0,)), ((), ())), preferred_element_type=F32)
            dln = jnp.dot(dyc_b, w2t_ref[...], preferred_element_type=F32) * _dsilu(s["sl_c"], s["ssl_c"])
            add(S_LNGC, dln * s["n_c"])
            add(S_LNBC, dln)
            do = _layer_norm_bwd(dln * vec(V_LNGC), s["n_c"], s["rstd_c"])
            add(S_BDW, do)
            hw = s["hw"]
            for k in range(CONV_C):
                add(S_WDW + k, do * _shift_rows(hw, CONV_C - 1 - k)[HALO:WIN])
            dow = future_window(2, do)
            dhc = wdw_ref[CONV_C - 1:CONV_C, :] * dow[0:SUB]
            for k in range(CONV_C - 1):
                dhc = dhc + wdw_ref[k:k + 1, :] * _shift_rows(dow, WIN - (CONV_C - 1 - k))[0:SUB]
            c_a = zc(C_A)
            sgl = jax.nn.sigmoid(zc(C_GL))
            put(C_A, dhc * sgl)
            put(C_GL, dhc * c_a * sgl * (1.0 - sgl))

            y, s = gating()
            ycat_ref[rows, _cols(3)] = y.astype(BF16)
            dy = dy_ref[rows,_cols(3)]
            put(D_G, dy * s["u"] * s["mixed"] * _dsilu(s["sg_d"], s["s_d"]))
            put(D_U, dy * s["mixed"] * s["sg_d"] * _dgelu(s["d_u"], s["aux_u"]))
            dmixed = dy * s["u"] * s["sg_d"]
            dbs_ref[...] += dmixed
            by_head = jnp.concatenate(
                [jnp.where((lane >= 64 * h) & (lane < 64 * h + 64), dmixed, 0.0) for h in range(4)], axis=0).astype(BF16)
            dws_ref[...] += lax.dot_general(by_head, s["v_b"], (((1,), (1,)), ((), ())), preferred_element_type=F32)
            rv = jnp.dot(wst_ref[...], dmixed.astype(BF16), preferred_element_type=F32)
            dv = _by_quarter(lane, [rv[h * SUB:(h + 1) * SUB] for h in range(4)])
            add(S_LNGD, dv * s["n_d"])
            add(S_LNBD, dv)
            dgv = _layer_norm_bwd(dv * vec(V_LNGD), s["n_d"], s["rstd_d"])
            put(D_V, dgv * _dgelu(s["d_v"], s["aux_v"]))

        future_ref[tile:tile + HALO, :] = future_ref[0:HALO, :]
        dy_ref[...] = dy_next_ref[...]

        @pl.when(i == n_tiles - 1)
        def _():
            for n in range(N_SUMS):
                sums_ref[n:n + 1, :] = jnp.sum(acc_ref[n], axis=0, keepdims=True)

    return _tiled_call(
        body, (z, z, o_c, dx_next, dx_next, gate, *small, *small_t, w_out_b), name="mix_bwd", grid=(n_tiles,),
        in_specs=[pl.BlockSpec((tile, D_IN), lambda i: (tile_of(i), 0)),
                  pl.BlockSpec((HALO, D_IN), lambda i: (jnp.maximum(tile_of(i) * per_halo - 1, 0), 0)),
                  pl.BlockSpec((tile, GROUP), lambda i: (tile_of(i), 0)),
                  pl.BlockSpec((tile, D_MODEL), lambda i: (tile_of(i), 0)),
                  pl.BlockSpec((tile, D_MODEL), lambda i: (next_tile_of(i), 0)), _full((1, D_MODEL)),
                  *_small_specs(True), _full((D_MODEL, D_MODEL))],
        out_specs=[pl.BlockSpec((tile, D_IN), lambda i: (tile_of(i), 0)),
                   pl.BlockSpec((tile, D_MODEL), lambda i: (tile_of(i), 0)),
                   _full((N_SUMS, GROUP)), _full((GROUP, GROUP)), _full((GROUP, GROUP)), _full((4 * SUB, SUB)),
                   _full((SUB, GROUP))],
        out_shape=[jax.ShapeDtypeStruct((n_tok, D_IN), BF16), jax.ShapeDtypeStruct((n_tok, D_MODEL), BF16),
                   jax.ShapeDtypeStruct((N_SUMS, GROUP), F32), jax.ShapeDtypeStruct((GROUP, GROUP), F32),
                   jax.ShapeDtypeStruct((GROUP, GROUP), F32), jax.ShapeDtypeStruct((4 * SUB, SUB), F32),
                   jax.ShapeDtypeStruct((SUB, GROUP), F32)],
        scratch_shapes=[pltpu.VMEM((tile + HALO, 3 * GROUP), F32), pltpu.VMEM((tile + HALO, 3 * GROUP), F32),
                        pltpu.VMEM((tile, D_MODEL), F32), pltpu.VMEM((tile, D_MODEL), F32),
                        pltpu.VMEM((N_SUMS, 8, GROUP), F32)], ride=ride)


def _norm_bwd(x, dz, dx_next, gs, w_in_b, tile, ride=None, blocks=None, begun=None, finish=True):
    n_tok = x.shape[0]
    first, n_tiles = blocks or (0, n_tok // tile)
    n_in = 5 + (2 if begun else 0)

    def body(*refs):
        x_ref, dz_ref, dxn_ref, gs_ref, w_ref = refs[:5]
        dx_ref = refs[n_in]
        acc_ref = refs[-1]
        i = pl.program_id(0)

        @pl.when(i == 0)
        def _():
            acc_ref[...] = refs[6][...] if begun else jnp.zeros_like(acc_ref)

        dh = lax.dot_general(dz_ref[...], w_ref[...], (((1,), (1,)), ((), ())), preferred_element_type=F32)
        xv = x_ref[...]
        r = lax.rsqrt(jnp.mean(xv * xv, axis=-1, keepdims=True) + EPS)
        xn = xv * r
        acc_ref[0] = acc_ref[0] + _row_sum8(dh)
        acc_ref[1] = acc_ref[1] + _row_sum8(dh * xn)
        dxn = dh * gs_ref[...]
        dx_ref[...] = dxn_ref[...] + r * (dxn - xn * jnp.mean(dxn * xn, axis=-1, keepdims=True))

        @pl.when(i == n_tiles - 1)
        def _():
            if finish:
                refs[n_in + 1][...] = jnp.sum(acc_ref[0], axis=0, keepdims=True)
                refs[n_in + 2][...] = jnp.sum(acc_ref[1], axis=0, keepdims=True)
            else:
                refs[n_in + 1][...] = acc_ref[...]

    def rows(i):
        return (first + i, 0)

    in_specs = [pl.BlockSpec((tile, D_MODEL), rows), pl.BlockSpec((tile, D_IN), rows), pl.BlockSpec((tile, D_MODEL), rows),
                _full((1, D_MODEL)), _full((D_MODEL, D_IN))]
    args = (x, dz, dx_next, gs, w_in_b)
    if begun:
        in_specs += [pl.BlockSpec(memory_space=pl.ANY), _full((2, 8, D_MODEL))]
        args += tuple(begun)
    vec = jax.ShapeDtypeStruct((1, D_MODEL), F32)
    return _tiled_call(
        body, args, name="norm_bwd", grid=(n_tiles,), in_specs=in_specs,
        out_specs=[pl.BlockSpec((tile, D_MODEL), rows)] + ([_full((1, D_MODEL))] * 2 if finish else [_full((2, 8, D_MODEL))]),
        out_shape=[jax.ShapeDtypeStruct((n_tok, D_MODEL), F32)]
        + ([vec, vec] if finish else [jax.ShapeDtypeStruct((2, 8, D_MODEL), F32)]),
        scratch_shapes=[pltpu.VMEM((2, 8, D_MODEL), F32)], ride=ride, aliases={5: 0} if begun else None)


def _tokens_matmul(a, b, name, out_dtype=F32, ride=None, a_cols=None, gated=None):
    n_tok = a.shape[0]
    a_block, ka = a_cols or (0, a.shape[1])
    nb = b.shape[1]
    tk = min(REDUCE_TILE * (4 // b.dtype.itemsize), n_tok)
    cb = min(D_MODEL, nb)
    n_steps = n_tok // tk
    n_in = 4 if gated else 2

    def body(*refs):
        a_ref, b_ref, o_ref, acc_ref = refs[0], refs[1], refs[n_in], refs[-1]
        i = pl.program_id(1)

        @pl.when(i == 0)
        def _():
            acc_ref[...] = jnp.zeros_like(acc_ref)

        acc_ref[...] += lax.dot_general(a_ref[...], b_ref[...].astype(BF16), (((0,), (0,)), ((), ())),
                                        preferred_element_type=F32)

        @pl.when(i == n_steps - 1)
        def _():
            m = acc_ref[...]
            if gated:
                w_ref, gate_ref, dgate_ref = refs[2], refs[3], refs[n_in + 1]
                o_ref[...] = (m * gate_ref[...]).astype(out_dtype)
                dgate_ref[...] = jnp.sum(m * w_ref[...].astype(F32), axis=0, keepdims=True)
            else:
                o_ref[...] = m.astype(out_dtype)

    in_specs = [pl.BlockSpec((tk, ka), lambda j, i: (i, a_block)), pl.BlockSpec((tk, cb), lambda j, i: (i, j))]
    out_specs = [pl.BlockSpec((ka, cb), lambda j, i: (0, j))]
    out_shape = [jax.ShapeDtypeStruct((ka, nb), out_dtype)]
    if gated:
        in_specs += [pl.BlockSpec((ka, cb), lambda j, i: (0, j)), pl.BlockSpec((1, cb), lambda j, i: (0, j))]
        out_specs += [pl.BlockSpec((1, cb), lambda j, i: (0, j))]
        out_shape += [jax.ShapeDtypeStruct((1, nb), F32)]
    outs, rode = _tiled_call(body, (a, b, *(gated or ())), name=name, grid=(nb // cb, n_steps), in_specs=in_specs,
                             out_specs=out_specs, out_shape=out_shape, scratch_shapes=[pltpu.VMEM((ka, cb), F32)],
                             ride=ride)
    return (outs if gated else outs[0]), rode


def _modulation_columns(c_all, w_ada, b_cols):
    cols = w_ada.shape[2]

    def body(c_ref, w_ref, b_ref, ca_ref, mod_ref):
        ca, _ = _silu(c_ref[...])
        ca_ref[...] = ca
        for l in range(N_LAYERS):
            mod_ref[l] = jnp.dot(ca, w_ref[l], precision=lax.Precision.HIGHEST, preferred_element_type=F32) + b_ref[l:l + 1, :]

    return pl.pallas_call(
        body, name="modulation_columns",
        out_shape=[jax.ShapeDtypeStruct((N_DEV, D_MODEL), F32), jax.ShapeDtypeStruct((N_LAYERS, N_DEV, cols), F32)],
        compiler_params=pltpu.CompilerParams(vmem_limit_bytes=VMEM_LIMIT),
    )(c_all, w_ada, b_cols)


def _adam(w, g, m, v):
    m2 = ADAM_B1 * m + (1.0 - ADAM_B1) * g
    v2 = ADAM_B2 * v + (1.0 - ADAM_B2) * (g * g)
    m_hat = m2 / (1.0 - ADAM_B1 ** ADAM_STEP)
    v_hat = v2 / (1.0 - ADAM_B2 ** ADAM_STEP)
    return -ADAM_LR * (m_hat / (jnp.sqrt(v_hat) + ADAM_EPS) + ADAM_WD * w), m2, v2


def _row_block(rows, cols, slots):
    target = max(8, (1 << 19) // (cols * max(slots, 1)))
    rb = rows
    while rb > target and rb % 2 == 0 and (rb // 2) % 8 == 0:
        rb //= 2
    return rb


def _adam_update(w, g, m, v, name):
    rows, cols = w.shape
    slotted = g.ndim == 3
    rb = _row_block(rows, cols, N_DEV if slotted else 1)

    def body(w_ref, g_ref, m_ref, v_ref, go_ref, d_ref, mo_ref, vo_ref):
        if slotted:
            gv = g_ref[0].astype(F32)
            for q in range(1, N_DEV):
                gv = gv + g_ref[q].astype(F32)
        else:
            gv = g_ref[...]
        go_ref[...] = gv
        d_ref[...], mo_ref[...], vo_ref[...] = _adam(w_ref[...], gv, m_ref[...], v_ref[...])

    blk = pl.BlockSpec((rb, cols), lambda i: (i, 0))
    g_blk = pl.BlockSpec((N_DEV, rb, cols), lambda i: (0, i, 0)) if slotted else blk
    return pl.pallas_call(
        body, name=name, grid=(rows // rb,),
        in_specs=[blk, g_blk, blk, blk], out_specs=[blk] * 4,
        out_shape=[jax.ShapeDtypeStruct((rows, cols), F32)] * 4,
        compiler_params=_params(("parallel",)),
    )(w, g, m, v)


def _adam_many(ws, gs, ms, vs, name):
    n = len(ws)

    def body(*refs):
        w_refs, g_refs, m_refs, v_refs, d_refs, mo_refs, vo_refs = (refs[k * n:(k + 1) * n] for k in range(7))
        for j in range(n):
            d_refs[j][...], mo_refs[j][...], vo_refs[j][...] = _adam(w_refs[j][...], g_refs[j][...], m_refs[j][...],
                                                                  v_refs[j][...])

    res = pl.pallas_call(
        body, name=name, out_shape=[jax.ShapeDtypeStruct(w.shape, F32) for w in ws] * 3,
        compiler_params=pltpu.CompilerParams(vmem_limit_bytes=VMEM_LIMIT),
    )(*ws, *gs, *ms, *vs)
    return res[:n], res[n:2 * n], res[2 * n:]


def _as_rows(a):
    return a.reshape(-1, a.shape[-1]) if a.ndim > 1 else a.reshape(1, -1)


def _ada_update(ca_t, dmod_cols, w, m, v):
    _, rows, cols = w.shape

    def body(ca_ref, dm_ref, w_ref, m_ref, v_ref, g_ref, d_ref, mo_ref, vo_ref):
        g = ca_ref[:, 0:1] * dm_ref[0, 0:1, :]
        for b in range(1, N_DEV):
            g = g + ca_ref[:, b:b + 1] * dm_ref[0, b:b + 1, :]
        g_ref[0] = g
        d_ref[0], mo_ref[0], vo_ref[0] = _adam(w_ref[0], g, m_ref[0], v_ref[0])

    blk = pl.BlockSpec((1, rows, cols), lambda l: (l, 0, 0))
    return pl.pallas_call(
        body, name="ada_update", grid=(N_LAYERS,),
        in_specs=[_full((rows, N_DEV)), pl.BlockSpec((1, N_DEV, cols), lambda l: (l, 0, 0)), blk, blk, blk],
        out_specs=[blk] * 4, out_shape=[jax.ShapeDtypeStruct(w.shape, F32)] * 4,
        compiler_params=_params(("parallel",)),
    )(ca_t, dmod_cols, w, m, v)


def _exchange_sems(n):
    return [pltpu.SemaphoreType.DMA((n, N_DEV - 1)), pltpu.SemaphoreType.DMA((n, N_DEV - 1)),
            pltpu.SemaphoreType.DMA((n,))]


def _exchange_copies(plans, srcs, outs, sems, receiving, only=None):
    send_sems, recv_sems, local_sems = sems
    x, y, c = lax.axis_index("x"), lax.axis_index("y"), lax.axis_index("c")
    me = 4 * x + 2 * y + c

    def remote(i, k, incoming):
        _, o, send, land = plans[i]
        px = 1 - x if k & 4 else x
        py = 1 - y if k & 2 else y
        pc = 1 - c if k & 1 else c
        p = 4 * px + 2 * py + pc
        return pltpu.make_async_remote_copy(
            src_ref=send(srcs[i], p), dst_ref=land(outs[o], p if incoming else me),
            send_sem=send_sems.at[i, k - 1], recv_sem=recv_sems.at[i, k - 1],
            device_id=(px, py, pc), device_id_type=pl.DeviceIdType.MESH)

    which = range(len(plans)) if only is None else only
    pairs = [(i, k) for k in range(1, N_DEV) for i in which]
    local = [pltpu.make_async_copy(plans[i][2](srcs[i], me), plans[i][3](outs[plans[i][1]], me), local_sems.at[i])
             for i in which]
    return local, [remote(i, k, False) for i, k in pairs], [remote(i, k, True) for i, k in pairs] if receiving else []


def _exchange_start(plans, srcs, outs, sems, only=None):
    local, outgoing, _ = _exchange_copies(plans, srcs, outs, sems, False, only)
    for cp in local + outgoing:
        cp.start()


def _exchange_wait(plans, srcs, outs, sems, only=None):
    local, outgoing, incoming = _exchange_copies(plans, srcs, outs, sems, True, only)
    for cp in incoming:
        cp.wait_recv()
    for cp in outgoing:
        cp.wait_send()
    for cp in local:
        cp.wait()


def _exchange(name, ride):
    out_shapes, plans = ride
    n = len(plans)
    hbm = pl.BlockSpec(memory_space=pltpu.HBM)

    def body(*refs):
        srcs, outs, sems = refs[:n], refs[n:n + len(out_shapes)], refs[n + len(out_shapes):]
        _exchange_start(plans, srcs, outs, sems)
        _exchange_wait(plans, srcs, outs, sems)

    return pl.pallas_call(
        body, name=name, in_specs=[hbm] * n, out_specs=[hbm] * len(out_shapes), out_shape=list(out_shapes),
        scratch_shapes=_exchange_sems(n),
    )(*[p[0] for p in plans])


def _first_gather(c, arrays, rules):
    n = len(arrays)
    c_shapes, c_plans = _plans([c], [_gather])
    shapes, lands = zip(*[(shape, land) for shape, _, land in (rule(a) for a, rule in zip(arrays, rules))])
    hbm = pl.BlockSpec(memory_space=pltpu.HBM)

    def body(*refs):
        c_ref, srcs, c_all_ref, outs = refs[0], refs[1:1 + n], refs[1 + n], refs[2 + n:2 + 2 * n]
        send_sems, recv_sems, local_sems = refs[2 + 2 * n:5 + 2 * n]
        c_sems = refs[5 + 2 * n:]
        x, y, core = lax.axis_index("x"), lax.axis_index("y"), lax.axis_index("c")
        me, sibling = (x, y, core), (x, y, 1 - core)
        chips = [(1 - x, y), (x, 1 - y), (1 - x, 1 - y)]

        def block(a, px, py, pc):
            return lands[a](outs[a], 4 * px + 2 * py + pc)

        def copy(a, k, origin, to, own=False):
            return pltpu.make_async_remote_copy(
                src_ref=srcs[a] if own else block(a, *origin), dst_ref=block(a, *origin),
                send_sem=send_sems.at[a, k], recv_sem=recv_sems.at[a, k], device_id=to,
                device_id_type=pl.DeviceIdType.MESH)

        _exchange_start(c_plans, [c_ref], [c_all_ref], c_sems)
        mine = [pltpu.make_async_copy(srcs[a], block(a, *me), local_sems.at[a]) for a in range(n)]
        first = [copy(a, 0, me, sibling, own=True) for a in range(n)]
        first += [copy(a, 1 + j, me, (*chip, core), own=True) for j, chip in enumerate(chips) for a in range(n)]
        for cp in mine + first:
            cp.start()
        passed = [[copy(a, 4 + j, (*chip, core), sibling) for a in range(n)] for j, chip in enumerate(chips)]
        for j, chip in enumerate(chips):
            for a in range(n):
                copy(a, 1 + j, (*chip, core), me).wait_recv()
                passed[j][a].start()
        for a in range(n):
            copy(a, 0, sibling, me).wait_recv()
        for j, chip in enumerate(chips):
            for a in range(n):
                copy(a, 4 + j, (*chip, 1 - core), me).wait_recv()
        for cp in first + [cp for row in passed for cp in row]:
            cp.wait_send()
        for cp in mine:
            cp.wait()
        _exchange_wait(c_plans, [c_ref], [c_all_ref], c_sems)

    return pl.pallas_call(
        body, name="first_gather", in_specs=[hbm] * (1 + n), out_specs=[hbm] * (1 + n),
        out_shape=[c_shapes[0], *shapes],
        scratch_shapes=[pltpu.SemaphoreType.DMA((n, N_DEV - 1)), pltpu.SemaphoreType.DMA((n, N_DEV - 1)),
                        pltpu.SemaphoreType.DMA((n,)), *_exchange_sems(1)],
    )(c, *arrays)


def _finish_exchange(big, big_rules, packed, dmod):
    n_rows = packed.shape[0]
    r = n_rows // N_DEV
    shapes, plans = _plans([*big, packed, dmod], [*big_rules, _scatter_rows, _gather])
    n_first = len(plans)
    i_small = n_first - 2
    _, send, land = _gather_rows(jax.ShapeDtypeStruct((r, 128), F32))
    plans = plans + [(None, len(shapes), send, land)]
    shapes = shapes + [jax.ShapeDtypeStruct((n_rows, 128), F32)]
    first = [i for i in range(n_first) if i != i_small]
    hbm = pl.BlockSpec(memory_space=pltpu.HBM)

    def body(*refs):
        srcs, outs = list(refs[:n_first]), refs[n_first:n_first + len(shapes)]
        parts_ref, sum_ref, local_sem = refs[n_first + len(shapes):n_first + len(shapes) + 3]
        sems = refs[n_first + len(shapes) + 3:]
        srcs.append(sum_ref)
        _exchange_start(plans, srcs, outs, sems, only=range(n_first))
        _exchange_wait(plans, srcs, outs, sems, only=[i_small])
        cp = pltpu.make_async_copy(outs[i_small], parts_ref, local_sem)
        cp.start()
        cp.wait()
        g = parts_ref[0]
        for q in range(1, N_DEV):
            g = g + parts_ref[q]
        sum_ref[...] = g
        _exchange_start(plans, srcs, outs, sems, only=[n_first])
        _exchange_wait(plans, srcs, outs, sems, only=[n_first])
        _exchange_wait(plans, srcs, outs, sems, only=first)

    res = pl.pallas_call(
        body, name="finish_exchange", in_specs=[hbm] * n_first, out_specs=[hbm] * len(shapes), out_shape=shapes,
        scratch_shapes=[pltpu.VMEM((N_DEV, r, 128), F32), pltpu.VMEM((r, 128), F32), pltpu.SemaphoreType.DMA(()),
                        *_exchange_sems(len(plans))],
    )(*big, packed, dmod)
    return (*res[:len(big)], res[-1], res[n_first - 1])


def _tiled_call(body, args, *, name, grid, in_specs, out_specs, out_shape, scratch_shapes=(), ride=None, aliases=None):
    params = _params(("arbitrary",) * len(grid))
    if ride is None:
        return pl.pallas_call(body, name=name, grid=grid, in_specs=in_specs, out_specs=out_specs, out_shape=out_shape,
                              scratch_shapes=list(scratch_shapes), input_output_aliases=aliases or {},
                              compiler_params=params)(*args), []
    shapes, plans = ride
    n_in, n_src, n_out, n_dst, n_scr = len(in_specs), len(plans), len(out_specs), len(shapes), len(scratch_shapes)
    hbm = pl.BlockSpec(memory_space=pltpu.HBM)

    def carrying(*refs):
        ins, srcs, refs = refs[:n_in], refs[n_in:n_in + n_src], refs[n_in + n_src:]
        outs, dsts, refs = refs[:n_out], refs[n_out:n_out + n_dst], refs[n_out + n_dst:]
        scratch, sems = refs[:n_scr], refs[n_scr:]
        ids = [pl.program_id(a) for a in range(len(grid))]
        first = functools.reduce(jnp.logical_and, [i == 0 for i in ids])
        last = functools.reduce(jnp.logical_and, [i == g - 1 for i, g in zip(ids, grid)])

        @pl.when(first)
        def _():
            _exchange_start(plans, srcs, dsts, sems)

        body(*ins, *outs, *scratch)

        @pl.when(last)
        def _():
            _exchange_wait(plans, srcs, dsts, sems)

    res = pl.pallas_call(
        carrying, name=name, grid=grid, in_specs=[*in_specs, *[hbm] * n_src], out_specs=[*out_specs, *[hbm] * n_dst],
        out_shape=[*out_shape, *shapes], scratch_shapes=[*scratch_shapes, *_exchange_sems(n_src)],
        input_output_aliases=aliases or {}, compiler_params=params)(*args, *[p[0] for p in plans])
    return res[:n_out], res[n_out:]


def _tail(nd, idx):
    return (slice(None),) * (nd - 2) + idx


def _gather(a):
    return jax.ShapeDtypeStruct((N_DEV,) + a.shape, a.dtype), lambda s, p: s, lambda o, q: o.at[q]


def _gather_rows(a):
    r = a.shape[-2]
    return (jax.ShapeDtypeStruct(a.shape[:-2] + (N_DEV * r, a.shape[-1]), a.dtype), lambda s, p: s,
            lambda o, q: o.at[_tail(a.ndim, (pl.ds(pl.multiple_of(q * r, r), r), slice(None)))])


def _gather_cols(a):
    c = a.shape[-1]
    return (jax.ShapeDtypeStruct(a.shape[:-1] + (N_DEV * c,), a.dtype), lambda s, p: s,
            lambda o, q: o.at[_tail(a.ndim, (slice(None), pl.ds(pl.multiple_of(q * c, c), c)))])


def _scatter_rows(a):
    r = a.shape[0] // N_DEV
    return (jax.ShapeDtypeStruct((N_DEV, r, a.shape[1]), a.dtype),
            lambda s, p: s.at[pl.ds(pl.multiple_of(p * r, r), r), :], lambda o, q: o.at[q])


def _scatter_cols(a):
    c = a.shape[1] // N_DEV
    return (jax.ShapeDtypeStruct((N_DEV, a.shape[0], c), a.dtype),
            lambda s, p: s.at[:, pl.ds(pl.multiple_of(p * c, c), c)], lambda o, q: o.at[q])


def _plans(arrays, rules):
    shapes, plans = [], []
    for o, (a, rule) in enumerate(zip(arrays, rules)):
        shape, send, land = rule(a)
        shapes.append(shape)
        plans.append((a, o, send, land))
    return shapes, plans


def _pack(pieces, rows_multiple=8):
    flat = []
    for a in pieces:
        f = a.reshape(-1)
        flat.append(jnp.pad(f, (0, (-f.shape[0]) % 128)))
    total = sum(f.shape[0] for f in flat)
    flat.append(jnp.zeros(((-total) % (128 * rows_multiple),), F32))
    return jnp.concatenate(flat).reshape(-1, 128)


def _unpack(buf, shapes, lead=()):
    flat = buf.reshape(lead + (-1,))
    out, off = [], 0
    for s in shapes:
        n = math.prod(s)
        out.append(flat[..., off:off + n].reshape(lead + tuple(s)))
        off += n + (-n) % 128
    return out


def _pad_rows(a, rows):
    return jnp.pad(a, ((0, rows - a.shape[0]), (0, 0)))


VEC_NAMES = ('pool_scale', 'b_dw_c', 'ln_g_c', 'ln_b_c', 'b_pw2_c', 'ln_g_d', 'ln_b_d')
GATHERED = ('w_in', 'w_out', 'w_pw2_c', 'w_conv_a', 'w_dw_c')
GATHER_RULES = (_gather_cols, _gather_rows, _gather_rows, _gather, _gather)
SCATTER_RULES = (_scatter_cols, _scatter_rows, _scatter_rows)


def _weight_shards(shard, l):
    return [shard[n][l].astype(BF16) if n in ('w_in', 'w_out') else shard[n][l] for n in GATHERED]


def _layer_weights(shard, l, gathered):
    w_in_b, w_out_b, w_pw2, wconv_parts, wdw_parts = gathered
    wconv = wconv_parts.transpose(1, 0, 2).reshape(CONV_A, GROUP)
    wdw = wdw_parts.transpose(1, 0, 2).reshape(CONV_C, GROUP)
    wp = jnp.einsum('gcd,gh->gchd', shard['w_pool'][l], jnp.eye(4, dtype=F32)).reshape(GROUP, GROUP)
    ws = shard['w_s_d'][l] * jnp.tril(jnp.ones((SUB, SUB), F32))
    vec = jnp.stack([shard[n][l] for n in VEC_NAMES])
    width = jnp.repeat(jnp.asarray([2.0, 4.0, 8.0, 16.0], F32), 64)[None]
    count = jnp.minimum(jnp.arange(1, SUB + 1, dtype=F32)[:, None], width)
    gating_and_counts = jnp.concatenate([jnp.repeat(shard['b_s_d'][l].T, 64, axis=1), 1.0 / count, 1.0 / width,
                                         jnp.zeros((7, GROUP), F32)])
    small = (_pad_rows(wconv, 8), _pad_rows(wdw, HALO), _pad_rows(vec, 16), wp.astype(BF16), w_pw2.astype(BF16),
             ws.reshape(4 * SUB, SUB).astype(BF16), gating_and_counts)
    small_t = (wp.T.astype(BF16), w_pw2.T.astype(BF16), ws.transpose(0, 2, 1).reshape(4 * SUB, SUB).astype(BF16))
    return w_in_b, w_out_b, small, small_t


def kernel(x, c, norm_g, w_ada, b_ada, w_in, w_conv_a, w_pool, pool_scale, w_dw_c, b_dw_c, ln_g_c, ln_b_c, w_pw2_c, b_pw2_c, ln_g_d, ln_b_d, w_s_d, b_s_d, w_out, final_g, loss_target, m_norm_g, m_w_ada, m_b_ada, m_w_in, m_w_conv_a, m_w_pool, m_pool_scale, m_w_dw_c, m_b_dw_c, m_ln_g_c, m_ln_b_c, m_w_pw2_c, m_b_pw2_c, m_ln_g_d, m_ln_b_d, m_w_s_d, m_b_s_d, m_w_out, m_final_g, v_norm_g, v_w_ada, v_b_ada, v_w_in, v_w_conv_a, v_w_pool, v_pool_scale, v_w_dw_c, v_b_dw_c, v_ln_g_c, v_ln_b_c, v_w_pw2_c, v_b_pw2_c, v_ln_g_d, v_ln_b_d, v_w_s_d, v_b_s_d, v_w_out, v_final_g):
    given = dict(locals())
    shard = {n: given[n] for n in WEIGHTS}
    mom_m = {n: given['m_' + n] for n in WEIGHTS}
    mom_v = {n: given['v_' + n] for n in WEIGHTS}
    me = 4 * lax.axis_index("x") + 2 * lax.axis_index("y") + lax.axis_index("c")
    n_tok = x.shape[1]
    tile = min(TOKEN_TILE, n_tok)
    wide_tile = min(2 * TOKEN_TILE, n_tok)
    x0 = x.reshape(n_tok, D_MODEL)
    target = loss_target.reshape(n_tok, D_MODEL)
    ada_cols = w_ada.shape[2]

    first_shards = _weight_shards(shard, 0)
    c_all, w_in_first = _first_gather(c, first_shards[:1], GATHER_RULES[:1])

    b_cols = lax.dynamic_slice_in_dim(b_ada, me * ada_cols, ada_cols, axis=1)
    c_act, mod_cols = _modulation_columns(c_all.reshape(N_DEV, D_MODEL), w_ada, b_cols)
    (mod_all,) = _exchange("gather_modulation", _plans([mod_cols], [_gather]))
    mod = lax.dynamic_index_in_dim(mod_all, me, axis=2, keepdims=False)
    mod = mod.transpose(1, 0, 2).reshape(N_LAYERS, 3 * D_MODEL)
    shift, scale, gate = (mod[:, k * D_MODEL:(k + 1) * D_MODEL].reshape(N_LAYERS, 1, D_MODEL) for k in range(3))
    gs = norm_g.reshape(N_LAYERS, 1, D_MODEL) * (1.0 + scale)

    xs, hs, zs, ocs, layers = [x0], [], [], [], []
    for l in range(N_LAYERS):
        if l == 0:
            (h, z), rest = _in_proj(xs[0], gs[0], shift[0], w_in_first, wide_tile,
                                    ride=_plans(first_shards[1:], GATHER_RULES[1:]))
            layers.append(_layer_weights(shard, 0, [w_in_first, *rest]))
        else:
            (h, z), _ = _in_proj(xs[l], gs[l], shift[l], layers[l][0], wide_tile)
        _, w_out_b, small, _ = layers[l]
        hs.append(h)
        zs.append(z)
        if l + 1 < N_LAYERS:
            (x_next, o_c), gathered = _mix_out(z, xs[l], gate[l], small, w_out_b, tile,
                                               ride=_plans(_weight_shards(shard, l + 1), GATHER_RULES))
            xs.append(x_next)
            layers.append(_layer_weights(shard, l + 1, gathered))
        else:
            (dx, o_c, loss_part, dfinal_g), _ = _mix_out(z, xs[l], gate[l], small, w_out_b, tile,
                                                         head=(final_g.reshape(1, D_MODEL), target))
        ocs.append(o_c)

    part = {}
    layer_parts = [None] * N_LAYERS
    slots = [None] * N_LAYERS
    for l in reversed(range(N_LAYERS)):
        w_in_b, w_out_b, small, small_t = layers[l]
        ride = _plans(layer_parts[l + 1]['big'], SCATTER_RULES) if l + 1 < N_LAYERS else None
        (dz, ycat, sums, dwp, dw2, dws, dbs), rode = _mix_bwd(zs[l], ocs[l], dx, gate[l], small, small_t, w_out_b, tile,
                                                              ride=ride)
        if ride:
            slots[l + 1] = rode
        (dw_out, dgate), _ = _tokens_matmul(ycat, dx, "out_proj_tokens_matmul", out_dtype=BF16, gated=(w_out_b, gate[l]))
        if l > 0:
            dw_in, _ = _tokens_matmul(hs[l], dz, "in_proj_tokens_matmul", out_dtype=BF16)
        else:
            dw_in, (slots_out, slots_pw2) = _tokens_matmul(
                hs[l], dz, "in_proj_tokens_matmul", out_dtype=BF16, a_cols=(0, D_MODEL // 2),
                ride=_plans([dw_out, dw2], SCATTER_RULES[1:]))
            dw_in_last, (slots_in,) = _tokens_matmul(
                hs[l], dz, "in_proj_tokens_matmul", out_dtype=BF16, a_cols=(1, D_MODEL // 2),
                ride=_plans([dw_in], SCATTER_RULES[:1]))
            slots[l] = [slots_in, slots_out, slots_pw2]
        (dx, dshift, dgs), _ = _norm_bwd(xs[l], dz, dx, gs[l], w_in_b, tile)
        layer_parts[l] = dict(
            big=[dw_in, dw_out, dw2],
            b_ada=jnp.concatenate([dshift, dgs * norm_g[l][None], dgate], axis=1)[0],
            norm_g=(dgs * (1.0 + scale[l]))[0], sums=sums,
            w_pool=jnp.einsum('gchd,gh->gcd', dwp.reshape(4, 64, 4, 64), jnp.eye(4, dtype=F32)),
            w_s_d=dws.reshape(4, SUB, SUB) * jnp.tril(jnp.ones((SUB, SUB), F32)),
            b_s_d=dbs.reshape(SUB, 4, 64).sum(axis=-1).T)
    grad_x = dx.reshape(x.shape)
    small_names = REPLICATED + CHANNEL_SHARDED
    packed_names = [n for n in small_names if n not in SUM_ROWS] + ['sums']
    for n in packed_names:
        part[n] = dfinal_g[0] if n == 'final_g' else jnp.stack([layer_parts[l][n] for l in range(N_LAYERS)])

    small_shapes = [part[n].shape for n in packed_names] + [(1, 128)]
    slots_in_last, small_sum, dmod_all = _finish_exchange(
        [dw_in_last], SCATTER_RULES[:1],
        _pack([part[n] for n in packed_names] + [loss_part[0:1]], rows_multiple=8 * N_DEV), part['b_ada'])

    grads, deltas, new_m, new_v = {}, {}, {}, {}
    half = D_MODEL // 2
    for j, n in enumerate(('w_in', 'w_out', 'w_pw2_c')):
        outs = [_adam_update(shard[n][l], slots[l][j], mom_m[n][l], mom_v[n][l], "update_" + n)
                for l in range(1, N_LAYERS)]
        if n == 'w_in':
            halves = [_adam_update(shard[n][0][rows], s, mom_m[n][0][rows], mom_v[n][0][rows], "update_" + n)
                      for rows, s in ((slice(0, half), slots[0][0]), (slice(half, None), slots_in_last))]
            outs.insert(0, [jnp.concatenate(o) for o in zip(*halves)])
        else:
            outs.insert(0, _adam_update(shard[n][0], slots[0][j], mom_m[n][0], mom_v[n][0], "update_" + n))
        grads[n], deltas[n], new_m[n], new_v[n] = (jnp.stack(o) for o in zip(*outs))

    *small_sums, loss_sum = _unpack(small_sum, small_shapes)
    loss = loss_sum[0, 0]
    gsum = dict(zip(packed_names, small_sums))
    for n, rows in SUM_ROWS.items():
        gsum[n] = gsum['sums'][:, rows]
    for n in CHANNEL_SHARDED:
        width = shard[n].shape[2]
        gsum[n] = lax.dynamic_slice_in_dim(gsum[n], me * width, width, axis=2)
    d_small, m_small, v_small = _adam_many(*[[_as_rows(d[n]) for n in small_names] for d in (shard, gsum, mom_m, mom_v)],
                                           "update_small")
    for j, n in enumerate(small_names):
        grads[n] = gsum[n]
        deltas[n], new_m[n], new_v[n] = (o[j].reshape(shard[n].shape) for o in (d_small, m_small, v_small))

    dmod_cols = lax.dynamic_slice_in_dim(dmod_all, me * ada_cols, ada_cols, axis=2).transpose(1, 0, 2)
    grads['w_ada'], deltas['w_ada'], new_m['w_ada'], new_v['w_ada'] = _ada_update(
        c_act.T, dmod_cols, w_ada, m_w_ada, v_w_ada)

    return (loss, grad_x, *[grads[n] for n in WEIGHTS], *[deltas[n] for n in WEIGHTS],
            *[new_m[n] for n in WEIGHTS], *[new_v[n] for n in WEIGHTS])
```

```python
import functools
import math

import jax
import jax.numpy as jnp
from jax import lax
from jax.experimental import pallas as pl
from jax.experimental.pallas import tpu as pltpu

F32 = jnp.float32
BF16 = jnp.bfloat16

N_DEV = 8
D_MODEL = 1024
GROUP = 256
D_IN = 12 * GROUP
N_LAYERS = 2
HALO = 32
SUB = 128
WIN = SUB + HALO
TOKEN_TILE = 512
REDUCE_TILE = 2048
EPS = 1e-6
VMEM_BYTES_V7X = 64 * 1024 * 1024
VMEM_LIMIT = VMEM_BYTES_V7X - 8 * 1024 * 1024

ADAM_LR = 0.001
ADAM_B1 = 0.9
ADAM_B2 = 0.999
ADAM_EPS = 1e-08
ADAM_WD = 0.01
ADAM_STEP = 10

A_B, A_C, A_X, A_G, B_P, B_G, C_A, C_GL, C_G, D_U, D_V, D_G = range(12)
V_PSCALE, V_BDW, V_LNGC, V_LNBC, V_BPW2, V_LNGD, V_LNBD = range(7)
S_WCONV, S_PSCALE, S_BDW, S_LNGC, S_LNBC, S_BPW2, S_LNGD, S_LNBD, S_WDW = 0, 3, 4, 5, 6, 7, 8, 9, 16
N_SUMS = 64
CONV_A = 3
CONV_C = 31
SUM_ROWS = dict(w_conv_a=slice(S_WCONV, S_WCONV + CONV_A), w_dw_c=slice(S_WDW, S_WDW + CONV_C), pool_scale=S_PSCALE,
                b_dw_c=S_BDW, ln_g_c=S_LNGC, ln_b_c=S_LNBC, b_pw2_c=S_BPW2, ln_g_d=S_LNGD, ln_b_d=S_LNBD)

WEIGHTS = ('norm_g', 'w_ada', 'b_ada', 'w_in', 'w_conv_a', 'w_pool', 'pool_scale', 'w_dw_c', 'b_dw_c', 'ln_g_c',
           'ln_b_c', 'w_pw2_c', 'b_pw2_c', 'ln_g_d', 'ln_b_d', 'w_s_d', 'b_s_d', 'w_out', 'final_g')
REPLICATED = ('norm_g', 'b_ada', 'w_pool', 'pool_scale', 'b_dw_c', 'ln_g_c', 'ln_b_c', 'b_pw2_c', 'ln_g_d', 'ln_b_d',
              'w_s_d', 'b_s_d', 'final_g')
CHANNEL_SHARDED = ('w_conv_a', 'w_dw_c')


def _params(semantics, vmem=VMEM_LIMIT):
    return pltpu.CompilerParams(dimension_semantics=semantics, vmem_limit_bytes=vmem)


def _cols(g):
    return slice(g * GROUP, (g + 1) * GROUP)


def _full(shape):
    return pl.BlockSpec(shape, lambda *_: (0,) * len(shape))


def _silu(x):
    s = jax.nn.sigmoid(x)
    return x * s, s


def _dsilu(sg, s):
    return s + sg * (1.0 - s)


_GELU_C0 = math.sqrt(2.0 / math.pi)
_GELU_C1 = 0.044715


def _gelu(x):
    x2 = x * x
    th = jnp.tanh(_GELU_C0 * (x + _GELU_C1 * (x * x2)))
    p = 0.5 + 0.5 * th
    return x * p, (th, p, x2)


def _dgelu(x, aux):
    th, p, x2 = aux
    return p + (0.5 * x) * (1.0 - th * th) * (_GELU_C0 + (3.0 * _GELU_C0 * _GELU_C1) * x2)


def _layer_norm(x):
    mu = jnp.mean(x, axis=-1, keepdims=True)
    xc = x - mu
    rstd = lax.rsqrt(jnp.mean(xc * xc, axis=-1, keepdims=True) + EPS)
    return xc * rstd, rstd


def _layer_norm_bwd(dn, n, rstd):
    return rstd * (dn - jnp.mean(dn, axis=-1, keepdims=True) - n * jnp.mean(dn * n, axis=-1, keepdims=True))


def _shift_rows(a, k):
    k = k % a.shape[0]
    return a if k == 0 else pltpu.roll(a, k, 0)


def _row_sum8(a):
    s = a[0:8]
    for m in range(1, a.shape[0] // 8):
        s = s + a[8 * m:8 * m + 8]
    return s


def _lane():
    return lax.broadcasted_iota(jnp.int32, (SUB, GROUP), 1)


def _by_quarter(lane, parts):
    return jnp.where(lane < 64, parts[0], jnp.where(lane < 128, parts[1], jnp.where(lane < 192, parts[2], parts[3])))


def _conv_inputs(z_ref, rows):
    def f(g):
        return z_ref[rows, _cols(g)].astype(F32)
    return f(A_C) * f(A_X), f(B_P), f(C_A) * jax.nn.sigmoid(f(C_GL))


def _fill_past(past_ref, zh_ref, zm_ref, is_first, tile):
    parts = _conv_inputs(zh_ref, slice(None))
    for n, a in enumerate(parts):
        past_ref[0:HALO, _cols(n)] = jnp.where(is_first, 0.0, a)

    def body(j, carry):
        r0 = pl.multiple_of(j * SUB, SUB)
        for n, a in enumerate(_conv_inputs(zm_ref, pl.ds(r0, SUB))):
            past_ref[pl.ds(r0 + HALO, SUB), _cols(n)] = a
        return carry

    lax.fori_loop(0, tile // SUB, body, 0)


def _short_conv_taps(qw):
    return [_shift_rows(qw, CONV_A - 1 - k)[HALO:WIN] for k in range(CONV_A)]


def _doubling_sums(w, back, keep):
    n = w.shape[0]
    half = GROUP // 2
    lane = lax.broadcasted_iota(jnp.int32, (SUB, half), 1)

    def grow(s, k):
        return s + _shift_rows(s, k if back else n - k)

    lo2 = grow(w[:, :half], 1)
    lo4 = grow(lo2, 2)
    hi8 = grow(grow(grow(w[:, half:], 1), 2), 4)
    hi16 = grow(hi8, 8)
    return jnp.concatenate([jnp.where(lane < 64, lo2[keep], lo4[keep]), jnp.where(lane < 64, hi8[keep], hi16[keep])],
                           axis=1)


def _window_sums(pw):
    return _doubling_sums(pw, True, slice(HALO, WIN))


def _forward_window_sums(ew):
    return _doubling_sums(ew, False, slice(0, SUB))


def _inv_count(bs_ref, t_first):
    return jnp.where(t_first == 0, bs_ref[SUB:2 * SUB, :], bs_ref[2 * SUB:2 * SUB + 1, :])


def _mixer_forwards(zc, win, t_first, wc_ref, wdw_ref, vec_ref, wp_ref, w2_ref, ws_ref, bs_ref, o_c=None):
    def vec(n):
        return vec_ref[n:n + 1, :]

    def short_conv():
        taps = _short_conv_taps(win(0))
        o_a = wc_ref[0:1, :] * taps[0] + wc_ref[1:2, :] * taps[1] + wc_ref[2:3, :] * taps[2]
        a_b, a_g = zc(A_B), zc(A_G)
        sg_a, s_a = _silu(a_g)
        return a_b * o_a * sg_a, dict(taps=taps, o_a=o_a, a_b=a_b, a_g=a_g, sg_a=sg_a, s_a=s_a)

    def pooling():
        pw = win(1)
        ic = _inv_count(bs_ref, t_first)
        pooled_b = (_window_sums(pw) * ic - pw[HALO:WIN]).astype(BF16)
        y0_b = jnp.dot(pooled_b, wp_ref[...], preferred_element_type=F32)
        b_g = zc(B_G)
        sg_b, s_b = _silu(b_g)
        return y0_b * vec(V_PSCALE) * sg_b, dict(ic=ic, pooled_b=pooled_b, y0_b=y0_b, b_g=b_g, sg_b=sg_b, s_b=s_b)

    def conformer():
        hw = win(2)
        o = o_c
        if o is None:
            o = wdw_ref[CONV_C - 1:CONV_C, :] * hw[HALO:WIN] + vec(V_BDW)
            for k in range(CONV_C - 1):
                o = o + wdw_ref[k:k + 1, :] * _shift_rows(hw, CONV_C - 1 - k)[HALO:WIN]
        n_c, rstd_c = _layer_norm(o)
        ln_c = n_c * vec(V_LNGC) + vec(V_LNBC)
        sl_c, ssl_c = _silu(ln_c)
        sl_b = sl_c.astype(BF16)
        yc = jnp.dot(sl_b, w2_ref[...], preferred_element_type=F32) + vec(V_BPW2)
        c_g = zc(C_G)
        sg_c, s_c = _silu(c_g)
        return yc * sg_c, dict(hw=hw, o_c=o, n_c=n_c, rstd_c=rstd_c, sl_c=sl_c, ssl_c=ssl_c, sl_b=sl_b, yc=yc, c_g=c_g,
                               sg_c=sg_c, s_c=s_c)

    def gating():
        lane = _lane()
        d_u, d_v, d_g = zc(D_U), zc(D_V), zc(D_G)
        u, aux_u = _gelu(d_u)
        gv, aux_v = _gelu(d_v)
        n_d, rstd_d = _layer_norm(gv)
        v_b = (n_d * vec(V_LNGD) + vec(V_LNBD)).astype(BF16)
        r = jnp.dot(ws_ref[...], v_b, preferred_element_type=F32)
        mixed = _by_quarter(lane, [r[h * SUB:(h + 1) * SUB] for h in range(4)]) + bs_ref[0:SUB, :]
        sg_d, s_d = _silu(d_g)
        return u * mixed * sg_d, dict(d_u=d_u, d_v=d_v, u=u, aux_u=aux_u, aux_v=aux_v, n_d=n_d, rstd_d=rstd_d,
                                      v_b=v_b, mixed=mixed, sg_d=sg_d, s_d=s_d)

    return short_conv, pooling, conformer, gating


def _in_proj(x, gs, shift, w_in_b, tile, ride=None):
    n_tok = x.shape[0]

    def body(x_ref, gs_ref, sh_ref, w_ref, h_ref, z_ref):
        xv = x_ref[...]
        r = lax.rsqrt(jnp.mean(xv * xv, axis=-1, keepdims=True) + EPS)
        h = ((xv * r) * gs_ref[...] + sh_ref[...]).astype(BF16)
        h_ref[...] = h
        for j in range(D_IN // D_MODEL):
            cs = slice(j * D_MODEL, (j + 1) * D_MODEL)
            z_ref[:, cs] = jnp.dot(h, w_ref[:, cs], preferred_element_type=F32).astype(BF16)

    return _tiled_call(
        body, (x, gs, shift, w_in_b), name="in_proj", grid=(n_tok // tile,),
        in_specs=[pl.BlockSpec((tile, D_MODEL), lambda i: (i, 0)), _full((1, D_MODEL)), _full((1, D_MODEL)),
                  _full((D_MODEL, D_IN))],
        out_specs=[pl.BlockSpec((tile, D_MODEL), lambda i: (i, 0)), pl.BlockSpec((tile, D_IN), lambda i: (i, 0))],
        out_shape=[jax.ShapeDtypeStruct((n_tok, D_MODEL), BF16), jax.ShapeDtypeStruct((n_tok, D_IN), BF16)],
        ride=ride)


def _small_specs(with_transposes):
    specs = [_full((8, GROUP)), _full((HALO, GROUP)), _full((16, GROUP)), _full((GROUP, GROUP)), _full((GROUP, GROUP)),
             _full((4 * SUB, SUB)), _full((2 * SUB + 8, GROUP))]
    if with_transposes:
        specs += [_full((GROUP, GROUP)), _full((GROUP, GROUP)), _full((4 * SUB, SUB))]
    return specs


def _mix_out(z, x, gate, small, w_out_b, tile, ride=None, head=None):
    n_tok = x.shape[0]
    n_tiles = n_tok // tile
    n_sub = tile // SUB
    cw = D_MODEL // n_sub
    per_halo = tile // HALO
    n_in = 12 + (2 if head else 0)
    n_out = 4 if head else 2

    def cur(i):
        return jnp.minimum(i, n_tiles - 1)

    def prev(i):
        return jnp.maximum(i - 1, 0)

    def body(*refs):
        (zm_ref, zh_ref, x_ref, gate_ref, wc_ref, wdw_ref, vec_ref, wp_ref, w2_ref, ws_ref, bs_ref, wout_ref) = refs[:12]
        xo_ref, oc_ref = refs[n_in:n_in + 2]
        past_ref, ycat_ref, ycat_prev_ref = refs[n_in + n_out:n_in + n_out + 3]
        i = pl.program_id(0)
        t = cur(i)
        if head:
            g_ref, tgt_ref = refs[12:14]
            loss_ref, dg_ref = refs[n_in + 2:n_in + 4]
            xn_ref, acc_ref = refs[n_in + n_out + 3:]
        else:
            xn_ref = xo_ref

        @pl.when(i == 0)
        def _():
            ycat_prev_ref[...] = jnp.zeros_like(ycat_prev_ref)
            if head:
                acc_ref[...] = jnp.zeros_like(acc_ref)

        _fill_past(past_ref, zh_ref, zm_ref, t == 0, tile)
        for j in range(n_sub):
            cs = slice(j * cw, (j + 1) * cw)
            y = jnp.dot(ycat_prev_ref[...], wout_ref[:, cs], preferred_element_type=F32)
            xn_ref[:, cs] = x_ref[:, cs] + gate_ref[:, cs] * y
            rows = slice(j * SUB, (j + 1) * SUB)
            mixers = _mixer_forwards(
                lambda g: zm_ref[rows, _cols(g)].astype(F32), lambda n: past_ref[j * SUB:j * SUB + WIN, _cols(n)],
                t * tile + j * SUB, wc_ref, wdw_ref, vec_ref, wp_ref, w2_ref, ws_ref, bs_ref)
            for n, mixer in enumerate(mixers):
                y, s = mixer()
                ycat_ref[rows, _cols(n)] = y.astype(BF16)
                if "o_c" in s:
                    oc_ref[rows, :] = s["o_c"]
        ycat_prev_ref[...] = ycat_ref[...]
        if head:
            counted = jnp.where(i > 0, 1.0, 0.0)
            xo_ref[...] = _loss_head_block(xn_ref[...], g_ref[...], tgt_ref[...], acc_ref, counted)

            @pl.when(i == n_tiles)
            def _():
                loss_ref[...] = jnp.full((8, 128), 0.5 / D_MODEL, F32) * jnp.sum(acc_ref[0])
                dg_ref[...] = jnp.sum(acc_ref[1], axis=0, keepdims=True)

    in_specs = [pl.BlockSpec((tile, D_IN), lambda i: (cur(i), 0)),
                pl.BlockSpec((HALO, D_IN), lambda i: (jnp.maximum(cur(i) * per_halo - 1, 0), 0)),
                pl.BlockSpec((tile, D_MODEL), lambda i: (prev(i), 0)), _full((1, D_MODEL)),
                *_small_specs(False), _full((D_MODEL, D_MODEL))]
    out_specs = [pl.BlockSpec((tile, D_MODEL), lambda i: (prev(i), 0)), pl.BlockSpec((tile, GROUP), lambda i: (cur(i), 0))]
    out_shape = [jax.ShapeDtypeStruct((n_tok, D_MODEL), F32), jax.ShapeDtypeStruct((n_tok, GROUP), F32)]
    scratch = [pltpu.VMEM((tile + HALO, 3 * GROUP), F32), pltpu.VMEM((tile, D_MODEL), BF16), pltpu.VMEM((tile, D_MODEL), BF16)]
    args = (z, z, x, gate, *small, w_out_b)
    if head:
        in_specs += [_full((1, D_MODEL)), pl.BlockSpec((tile, D_MODEL), lambda i: (prev(i), 0))]
        out_specs += [_full((8, 128)), _full((1, D_MODEL))]
        out_shape += [jax.ShapeDtypeStruct((8, 128), F32), jax.ShapeDtypeStruct((1, D_MODEL), F32)]
        scratch += [pltpu.VMEM((tile, D_MODEL), F32), pltpu.VMEM((2, 8, D_MODEL), F32)]
        args += tuple(head)
    outs, rode = _tiled_call(body, args, name="mix_out", grid=(n_tiles + 1,), in_specs=in_specs, out_specs=out_specs,
                             out_shape=out_shape, scratch_shapes=scratch, ride=ride)
    return outs, rode


def _loss_head_block(xv, g, target, acc_ref, counted):
    r = lax.rsqrt(jnp.mean(xv * xv, axis=-1, keepdims=True) + EPS)
    xn = xv * r
    err = xn * g - target
    acc_ref[0] = acc_ref[0] + counted * _row_sum8(err * err)
    dy = err * (1.0 / D_MODEL)
    acc_ref[1] = acc_ref[1] + counted * _row_sum8(dy * xn)
    a = dy * g
    return r * (a - xn * jnp.mean(a * xn, axis=-1, keepdims=True))


def _mix_bwd(z, o_c, dx_next, gate, small, small_t, w_out_b, tile, ride=None):
    n_tok = z.shape[0]
    n_tiles = n_tok // tile
    n_sub = tile // SUB
    cw = D_MODEL // n_sub
    per_halo = tile // HALO
    nt_dims = (((1,), (1,)), ((), ()))

    def tile_of(i):
        return n_tiles - 1 - i

    def next_tile_of(i):
        return jnp.maximum(n_tiles - 2 - i, 0)

    def body(zm_ref, zh_ref, oc_ref, dxn_ref, dxn_next_ref, gate_ref, wc_ref, wdw_ref, vec_ref, wp_ref, w2_ref, ws_ref,
             bs_ref, wpt_ref, w2t_ref, wst_ref, wout_ref,
             dz_ref, ycat_ref, sums_ref, dwp_ref, dw2_ref, dws_ref, dbs_ref,
             past_ref, future_ref, dy_ref, dy_next_ref, acc_ref):
        i = pl.program_id(0)
        t = tile_of(i)

        @pl.when(i == 0)
        def _():
            acc_ref[...] = jnp.zeros_like(acc_ref)
            dwp_ref[...] = jnp.zeros_like(dwp_ref)
            dw2_ref[...] = jnp.zeros_like(dw2_ref)
            dws_ref[...] = jnp.zeros_like(dws_ref)
            dbs_ref[...] = jnp.zeros_like(dbs_ref)
            future_ref[tile:tile + HALO, :] = jnp.zeros((HALO, 3 * GROUP), F32)
            dy_ref[...] = lax.dot_general((dxn_ref[...] * gate_ref[...]).astype(BF16), wout_ref[...], nt_dims,
                                        preferred_element_type=F32)

        _fill_past(past_ref, zh_ref, zm_ref, t == 0, tile)
        dyb_next = (dxn_next_ref[...] * gate_ref[...]).astype(BF16)

        def vec(n):
            return vec_ref[n:n + 1, :]

        for jj in range(n_sub):
            j = n_sub - 1 - jj
            r0 = j * SUB
            rows = slice(r0, r0 + SUB)

            def zc(g):
                return zm_ref[rows, _cols(g)].astype(F32)

            def add(n, a):
                acc_ref[n] = acc_ref[n] + _row_sum8(a)

            def put(g, a):
                dz_ref[rows, _cols(g)] = a.astype(BF16)

            def future_window(n, a):
                future_ref[rows, _cols(n)] = a
                return future_ref[r0:r0 + WIN, _cols(n)]

            short_conv, pooling, conformer, gating = _mixer_forwards(
                zc, lambda n: past_ref[r0:r0 + WIN, _cols(n)], t * tile + r0,
                wc_ref, wdw_ref, vec_ref, wp_ref, w2_ref, ws_ref, bs_ref, o_c=oc_ref[rows, :])
            lane = _lane()
            ks = slice(jj * cw, (jj + 1) * cw)
            dy_next_ref[:, ks] = lax.dot_general(dyb_next, wout_ref[ks, :], nt_dims, preferred_element_type=F32)

            y, s = short_conv()
            ycat_ref[rows, _cols(0)] = y.astype(BF16)
            dy = dy_ref[rows,_cols(0)]
            put(A_B, dy * s["o_a"] * s["sg_a"])
            put(A_G, dy * s["a_b"] * s["o_a"] * _dsilu(s["sg_a"], s["s_a"]))
            do = dy * s["a_b"] * s["sg_a"]
            for k in range(CONV_A):
                add(S_WCONV + k, do * s["taps"][k])
            dow = future_window(0, do)
            dq = wc_ref[CONV_A - 1:CONV_A, :] * dow[0:SUB]
            for k in range(CONV_A - 1):
                dq = dq + wc_ref[k:k + 1, :] * _shift_rows(dow, WIN - (CONV_A - 1 - k))[0:SUB]
            put(A_C, dq * zc(A_X))
            put(A_X, dq * zc(A_C))

            y, s = pooling()
            ycat_ref[rows, _cols(1)] = y.astype(BF16)
            dy = dy_ref[rows,_cols(1)]
            put(B_G, dy * (s["y0_b"] * vec(V_PSCALE)) * _dsilu(s["sg_b"], s["s_b"]))
            dyb = dy * s["sg_b"]
            add(S_PSCALE, dyb * s["y0_b"])
            dpw_b = (dyb * vec(V_PSCALE)).astype(BF16)
            dwp_ref[...] += lax.dot_general(s["pooled_b"], dpw_b, (((0,), (0,)), ((), ())), preferred_element_type=F32)
            dpooled = jnp.dot(dpw_b, wpt_ref[...], preferred_element_type=F32)
            ew = future_window(1, dpooled * s["ic"])
            put(B_P, _forward_window_sums(ew) - dpooled)

            y, s = conformer()
            ycat_ref[rows, _cols(2)] = y.astype(BF16)
            dy = dy_ref[rows,_cols(2)]
            put(C_G, dy * s["yc"] * _dsilu(s["sg_c"], s["s_c"]))
            dyc = dy * s["sg_c"]
            add(S_BPW2, dyc)
            dyc_b = dyc.astype(BF16)
            dw2_ref[...] += lax.dot_general(s["sl_b"], dyc_b, (((0,), (0,)), ((), ())), preferred_element_type=F32)
            dln = jnp.dot(dyc_b, w2t_ref[...], preferred_element_type=F32) * _dsilu(s["sl_c"], s["ssl_c"])
            add(S_LNGC, dln * s["n_c"])
            add(S_LNBC, dln)
            do = _layer_norm_bwd(dln * vec(V_LNGC), s["n_c"], s["rstd_c"])
            add(S_BDW, do)
            hw = s["hw"]
            for k in range(CONV_C):
                add(S_WDW + k, do * _shift_rows(hw, CONV_C - 1 - k)[HALO:WIN])
            dow = future_window(2, do)
            dhc = wdw_ref[CONV_C - 1:CONV_C, :] * dow[0:SUB]
            for k in range(CONV_C - 1):
                dhc = dhc + wdw_ref[k:k + 1, :] * _shift_rows(dow, WIN - (CONV_C - 1 - k))[0:SUB]
            c_a = zc(C_A)
            sgl = jax.nn.sigmoid(zc(C_GL))
            put(C_A, dhc * sgl)
            put(C_GL, dhc * c_a * sgl * (1.0 - sgl))

            y, s = gating()
            ycat_ref[rows, _cols(3)] = y.astype(BF16)
            dy = dy_ref[rows,_cols(3)]
            put(D_G, dy * s["u"] * s["mixed"] * _dsilu(s["sg_d"], s["s_d"]))
            put(D_U, dy * s["mixed"] * s["sg_d"] * _dgelu(s["d_u"], s["aux_u"]))
            dmixed = dy * s["u"] * s["sg_d"]
            dbs_ref[...] += dmixed
            by_head = jnp.concatenate(
                [jnp.where((lane >= 64 * h) & (lane < 64 * h + 64), dmixed, 0.0) for h in range(4)], axis=0).astype(BF16)
            dws_ref[...] += lax.dot_general(by_head, s["v_b"], (((1,), (1,)), ((), ())), preferred_element_type=F32)
            rv = jnp.dot(wst_ref[...], dmixed.astype(BF16), preferred_element_type=F32)
            dv = _by_quarter(lane, [rv[h * SUB:(h + 1) * SUB] for h in range(4)])
            add(S_LNGD, dv * s["n_d"])
            add(S_LNBD, dv)
            dgv = _layer_norm_bwd(dv * vec(V_LNGD), s["n_d"], s["rstd_d"])
            put(D_V, dgv * _dgelu(s["d_v"], s["aux_v"]))

        future_ref[tile:tile + HALO, :] = future_ref[0:HALO, :]
        dy_ref[...] = dy_next_ref[...]

        @pl.when(i == n_tiles - 1)
        def _():
            for n in range(N_SUMS):
                sums_ref[n:n + 1, :] = jnp.sum(acc_ref[n], axis=0, keepdims=True)

    return _tiled_call(
        body, (z, z, o_c, dx_next, dx_next, gate, *small, *small_t, w_out_b), name="mix_bwd", grid=(n_tiles,),
        in_specs=[pl.BlockSpec((tile, D_IN), lambda i: (tile_of(i), 0)),
                  pl.BlockSpec((HALO, D_IN), lambda i: (jnp.maximum(tile_of(i) * per_halo - 1, 0), 0)),
                  pl.BlockSpec((tile, GROUP), lambda i: (tile_of(i), 0)),
                  pl.BlockSpec((tile, D_MODEL), lambda i: (tile_of(i), 0)),
                  pl.BlockSpec((tile, D_MODEL), lambda i: (next_tile_of(i), 0)), _full((1, D_MODEL)),
                  *_small_specs(True), _full((D_MODEL, D_MODEL))],
        out_specs=[pl.BlockSpec((tile, D_IN), lambda i: (tile_of(i), 0)),
                   pl.BlockSpec((tile, D_MODEL), lambda i: (tile_of(i), 0)),
                   _full((N_SUMS, GROUP)), _full((GROUP, GROUP)), _full((GROUP, GROUP)), _full((4 * SUB, SUB)),
                   _full((SUB, GROUP))],
        out_shape=[jax.ShapeDtypeStruct((n_tok, D_IN), BF16), jax.ShapeDtypeStruct((n_tok, D_MODEL), BF16),
                   jax.ShapeDtypeStruct((N_SUMS, GROUP), F32), jax.ShapeDtypeStruct((GROUP, GROUP), F32),
                   jax.ShapeDtypeStruct((GROUP, GROUP), F32), jax.ShapeDtypeStruct((4 * SUB, SUB), F32),
                   jax.ShapeDtypeStruct((SUB, GROUP), F32)],
        scratch_shapes=[pltpu.VMEM((tile + HALO, 3 * GROUP), F32), pltpu.VMEM((tile + HALO, 3 * GROUP), F32),
                        pltpu.VMEM((tile, D_MODEL), F32), pltpu.VMEM((tile, D_MODEL), F32),
                        pltpu.VMEM((N_SUMS, 8, GROUP), F32)], ride=ride)


def _norm_bwd(x, dz, dx_next, gs, w_in_b, tile, ride=None, blocks=None, begun=None, finish=True):
    n_tok = x.shape[0]
    first, n_tiles = blocks or (0, n_tok // tile)
    n_in = 5 + (2 if begun else 0)

    def body(*refs):
        x_ref, dz_ref, dxn_ref, gs_ref, w_ref = refs[:5]
        dx_ref = refs[n_in]
        acc_ref = refs[-1]
        i = pl.program_id(0)

        @pl.when(i == 0)
        def _():
            acc_ref[...] = refs[6][...] if begun else jnp.zeros_like(acc_ref)

        dh = lax.dot_general(dz_ref[...], w_ref[...], (((1,), (1,)), ((), ())), preferred_element_type=F32)
        xv = x_ref[...]
        r = lax.rsqrt(jnp.mean(xv * xv, axis=-1, keepdims=True) + EPS)
        xn = xv * r
        acc_ref[0] = acc_ref[0] + _row_sum8(dh)
        acc_ref[1] = acc_ref[1] + _row_sum8(dh * xn)
        dxn = dh * gs_ref[...]
        dx_ref[...] = dxn_ref[...] + r * (dxn - xn * jnp.mean(dxn * xn, axis=-1, keepdims=True))

        @pl.when(i == n_tiles - 1)
        def _():
            if finish:
                refs[n_in + 1][...] = jnp.sum(acc_ref[0], axis=0, keepdims=True)
                refs[n_in + 2][...] = jnp.sum(acc_ref[1], axis=0, keepdims=True)
            else:
                refs[n_in + 1][...] = acc_ref[...]

    def rows(i):
        return (first + i, 0)

    in_specs = [pl.BlockSpec((tile, D_MODEL), rows), pl.BlockSpec((tile, D_IN), rows), pl.BlockSpec((tile, D_MODEL), rows),
                _full((1, D_MODEL)), _full((D_MODEL, D_IN))]
    args = (x, dz, dx_next, gs, w_in_b)
    if begun:
        in_specs += [pl.BlockSpec(memory_space=pl.ANY), _full((2, 8, D_MODEL))]
        args += tuple(begun)
    vec = jax.ShapeDtypeStruct((1, D_MODEL), F32)
    return _tiled_call(
        body, args, name="norm_bwd", grid=(n_tiles,), in_specs=in_specs,
        out_specs=[pl.BlockSpec((tile, D_MODEL), rows)] + ([_full((1, D_MODEL))] * 2 if finish else [_full((2, 8, D_MODEL))]),
        out_shape=[jax.ShapeDtypeStruct((n_tok, D_MODEL), F32)]
        + ([vec, vec] if finish else [jax.ShapeDtypeStruct((2, 8, D_MODEL), F32)]),
        scratch_shapes=[pltpu.VMEM((2, 8, D_MODEL), F32)], ride=ride, aliases={5: 0} if begun else None)


def _tokens_matmul(a, b, name, out_dtype=F32, ride=None, a_cols=None, gated=None):
    n_tok = a.shape[0]
    a_block, ka = a_cols or (0, a.shape[1])
    nb = b.shape[1]
    tk = min(REDUCE_TILE * (4 // b.dtype.itemsize), n_tok)
    cb = min(D_MODEL, nb)
    n_steps = n_tok // tk
    n_in = 4 if gated else 2

    def body(*refs):
        a_ref, b_ref, o_ref, acc_ref = refs[0], refs[1], refs[n_in], refs[-1]
        i = pl.program_id(1)

        @pl.when(i == 0)
        def _():
            acc_ref[...] = jnp.zeros_like(acc_ref)

        acc_ref[...] += lax.dot_general(a_ref[...], b_ref[...].astype(BF16), (((0,), (0,)), ((), ())),
                                        preferred_element_type=F32)

        @pl.when(i == n_steps - 1)
        def _():
            m = acc_ref[...]
            if gated:
                w_ref, gate_ref, dgate_ref = refs[2], refs[3], refs[n_in + 1]
                o_ref[...] = (m * gate_ref[...]).astype(out_dtype)
                dgate_ref[...] = jnp.sum(m * w_ref[...].astype(F32), axis=0, keepdims=True)
            else:
                o_ref[...] = m.astype(out_dtype)

    in_specs = [pl.BlockSpec((tk, ka), lambda j, i: (i, a_block)), pl.BlockSpec((tk, cb), lambda j, i: (i, j))]
    out_specs = [pl.BlockSpec((ka, cb), lambda j, i: (0, j))]
    out_shape = [jax.ShapeDtypeStruct((ka, nb), out_dtype)]
    if gated:
        in_specs += [pl.BlockSpec((ka, cb), lambda j, i: (0, j)), pl.BlockSpec((1, cb), lambda j, i: (0, j))]
        out_specs += [pl.BlockSpec((1, cb), lambda j, i: (0, j))]
        out_shape += [jax.ShapeDtypeStruct((1, nb), F32)]
    outs, rode = _tiled_call(body, (a, b, *(gated or ())), name=name, grid=(nb // cb, n_steps), in_specs=in_specs,
                             out_specs=out_specs, out_shape=out_shape, scratch_shapes=[pltpu.VMEM((ka, cb), F32)],
                             ride=ride)
    return (outs if gated else outs[0]), rode


def _modulation_columns(c_all, w_ada, b_cols):
    cols = w_ada.shape[2]

    def body(c_ref, w_ref, b_ref, ca_ref, mod_ref):
        ca, _ = _silu(c_ref[...])
        ca_ref[...] = ca
        for l in range(N_LAYERS):
            mod_ref[l] = jnp.dot(ca, w_ref[l], precision=lax.Precision.HIGHEST, preferred_element_type=F32) + b_ref[l:l + 1, :]

    return pl.pallas_call(
        body, name="modulation_columns",
        out_shape=[jax.ShapeDtypeStruct((N_DEV, D_MODEL), F32), jax.ShapeDtypeStruct((N_LAYERS, N_DEV, cols), F32)],
        compiler_params=pltpu.CompilerParams(vmem_limit_bytes=VMEM_LIMIT),
    )(c_all, w_ada, b_cols)


def _adam(w, g, m, v):
    m2 = ADAM_B1 * m + (1.0 - ADAM_B1) * g
    v2 = ADAM_B2 * v + (1.0 - ADAM_B2) * (g * g)
    m_hat = m2 / (1.0 - ADAM_B1 ** ADAM_STEP)
    v_hat = v2 / (1.0 - ADAM_B2 ** ADAM_STEP)
    return -ADAM_LR * (m_hat / (jnp.sqrt(v_hat) + ADAM_EPS) + ADAM_WD * w), m2, v2


def _row_block(rows, cols, slots):
    target = max(8, (1 << 19) // (cols * max(slots, 1)))
    rb = rows
    while rb > target and rb % 2 == 0 and (rb // 2) % 8 == 0:
        rb //= 2
    return rb


def _adam_update(w, g, m, v, name):
    rows, cols = w.shape
    slotted = g.ndim == 3
    rb = _row_block(rows, cols, N_DEV if slotted else 1)

    def body(w_ref, g_ref, m_ref, v_ref, go_ref, d_ref, mo_ref, vo_ref):
        if slotted:
            gv = g_ref[0].astype(F32)
            for q in range(1, N_DEV):
                gv = gv + g_ref[q].astype(F32)
        else:
            gv = g_ref[...]
        go_ref[...] = gv
        d_ref[...], mo_ref[...], vo_ref[...] = _adam(w_ref[...], gv, m_ref[...], v_ref[...])

    blk = pl.BlockSpec((rb, cols), lambda i: (i, 0))
    g_blk = pl.BlockSpec((N_DEV, rb, cols), lambda i: (0, i, 0)) if slotted else blk
    return pl.pallas_call(
        body, name=name, grid=(rows // rb,),
        in_specs=[blk, g_blk, blk, blk], out_specs=[blk] * 4,
        out_shape=[jax.ShapeDtypeStruct((rows, cols), F32)] * 4,
        compiler_params=_params(("parallel",)),
    )(w, g, m, v)


def _adam_many(ws, gs, ms, vs, name):
    n = len(ws)

    def body(*refs):
        w_refs, g_refs, m_refs, v_refs, d_refs, mo_refs, vo_refs = (refs[k * n:(k + 1) * n] for k in range(7))
        for j in range(n):
            d_refs[j][...], mo_refs[j][...], vo_refs[j][...] = _adam(w_refs[j][...], g_refs[j][...], m_refs[j][...],
                                                                  v_refs[j][...])

    res = pl.pallas_call(
        body, name=name, out_shape=[jax.ShapeDtypeStruct(w.shape, F32) for w in ws] * 3,
        compiler_params=pltpu.CompilerParams(vmem_limit_bytes=VMEM_LIMIT),
    )(*ws, *gs, *ms, *vs)
    return res[:n], res[n:2 * n], res[2 * n:]


def _as_rows(a):
    return a.reshape(-1, a.shape[-1]) if a.ndim > 1 else a.reshape(1, -1)


def _ada_update(ca_t, dmod_cols, w, m, v):
    _, rows, cols = w.shape

    def body(ca_ref, dm_ref, w_ref, m_ref, v_ref, g_ref, d_ref, mo_ref, vo_ref):
        g = ca_ref[:, 0:1] * dm_ref[0, 0:1, :]
        for b in range(1, N_DEV):
            g = g + ca_ref[:, b:b + 1] * dm_ref[0, b:b + 1, :]
        g_ref[0] = g
        d_ref[0], mo_ref[0], vo_ref[0] = _adam(w_ref[0], g, m_ref[0], v_ref[0])

    blk = pl.BlockSpec((1, rows, cols), lambda l: (l, 0, 0))
    return pl.pallas_call(
        body, name="ada_update", grid=(N_LAYERS,),
        in_specs=[_full((rows, N_DEV)), pl.BlockSpec((1, N_DEV, cols), lambda l: (l, 0, 0)), blk, blk, blk],
        out_specs=[blk] * 4, out_shape=[jax.ShapeDtypeStruct(w.shape, F32)] * 4,
        compiler_params=_params(("parallel",)),
    )(ca_t, dmod_cols, w, m, v)


def _exchange_sems(n):
    return [pltpu.SemaphoreType.DMA((n, N_DEV - 1)), pltpu.SemaphoreType.DMA((n, N_DEV - 1)),
            pltpu.SemaphoreType.DMA((n,))]


def _exchange_copies(plans, srcs, outs, sems, receiving, only=None):
    send_sems, recv_sems, local_sems = sems
    x, y, c = lax.axis_index("x"), lax.axis_index("y"), lax.axis_index("c")
    me = 4 * x + 2 * y + c

    def remote(i, k, incoming):
        _, o, send, land = plans[i]
        px = 1 - x if k & 4 else x
        py = 1 - y if k & 2 else y
        pc = 1 - c if k & 1 else c
        p = 4 * px + 2 * py + pc
        return pltpu.make_async_remote_copy(
            src_ref=send(srcs[i], p), dst_ref=land(outs[o], p if incoming else me),
            send_sem=send_sems.at[i, k - 1], recv_sem=recv_sems.at[i, k - 1],
            device_id=(px, py, pc), device_id_type=pl.DeviceIdType.MESH)

    which = range(len(plans)) if only is None else only
    pairs = [(i, k) for k in range(1, N_DEV) for i in which]
    local = [pltpu.make_async_copy(plans[i][2](srcs[i], me), plans[i][3](outs[plans[i][1]], me), local_sems.at[i])
             for i in which]
    return local, [remote(i, k, False) for i, k in pairs], [remote(i, k, True) for i, k in pairs] if receiving else []


def _exchange_start(plans, srcs, outs, sems, only=None):
    local, outgoing, _ = _exchange_copies(plans, srcs, outs, sems, False, only)
    for cp in local + outgoing:
        cp.start()


def _exchange_wait(plans, srcs, outs, sems, only=None):
    local, outgoing, incoming = _exchange_copies(plans, srcs, outs, sems, True, only)
    for cp in incoming:
        cp.wait_recv()
    for cp in outgoing:
        cp.wait_send()
    for cp in local:
        cp.wait()


def _exchange(name, ride):
    out_shapes, plans = ride
    n = len(plans)
    hbm = pl.BlockSpec(memory_space=pltpu.HBM)

    def body(*refs):
        srcs, outs, sems = refs[:n], refs[n:n + len(out_shapes)], refs[n + len(out_shapes):]
        _exchange_start(plans, srcs, outs, sems)
        _exchange_wait(plans, srcs, outs, sems)

    return pl.pallas_call(
        body, name=name, in_specs=[hbm] * n, out_specs=[hbm] * len(out_shapes), out_shape=list(out_shapes),
        scratch_shapes=_exchange_sems(n),
    )(*[p[0] for p in plans])


def _first_gather(c, arrays, rules):
    n = len(arrays)
    c_shapes, c_plans = _plans([c], [_gather])
    shapes, lands = zip(*[(shape, land) for shape, _, land in (rule(a) for a, rule in zip(arrays, rules))])
    hbm = pl.BlockSpec(memory_space=pltpu.HBM)

    def body(*refs):
        c_ref, srcs, c_all_ref, outs = refs[0], refs[1:1 + n], refs[1 + n], refs[2 + n:2 + 2 * n]
        send_sems, recv_sems, local_sems = refs[2 + 2 * n:5 + 2 * n]
        c_sems = refs[5 + 2 * n:]
        x, y, core = lax.axis_index("x"), lax.axis_index("y"), lax.axis_index("c")
        me, sibling = (x, y, core), (x, y, 1 - core)
        chips = [(1 - x, y), (x, 1 - y), (1 - x, 1 - y)]

        def block(a, px, py, pc):
            return lands[a](outs[a], 4 * px + 2 * py + pc)

        def copy(a, k, origin, to, own=False):
            return pltpu.make_async_remote_copy(
                src_ref=srcs[a] if own else block(a, *origin), dst_ref=block(a, *origin),
                send_sem=send_sems.at[a, k], recv_sem=recv_sems.at[a, k], device_id=to,
                device_id_type=pl.DeviceIdType.MESH)

        _exchange_start(c_plans, [c_ref], [c_all_ref], c_sems)
        mine = [pltpu.make_async_copy(srcs[a], block(a, *me), local_sems.at[a]) for a in range(n)]
        first = [copy(a, 0, me, sibling, own=True) for a in range(n)]
        first += [copy(a, 1 + j, me, (*chip, core), own=True) for j, chip in enumerate(chips) for a in range(n)]
        for cp in mine + first:
            cp.start()
        passed = [[copy(a, 4 + j, (*chip, core), sibling) for a in range(n)] for j, chip in enumerate(chips)]
        for j, chip in enumerate(chips):
            for a in range(n):
                copy(a, 1 + j, (*chip, core), me).wait_recv()
                passed[j][a].start()
        for a in range(n):
            copy(a, 0, sibling, me).wait_recv()
        for j, chip in enumerate(chips):
            for a in range(n):
                copy(a, 4 + j, (*chip, 1 - core), me).wait_recv()
        for cp in first + [cp for row in passed for cp in row]:
            cp.wait_send()
        for cp in mine:
            cp.wait()
        _exchange_wait(c_plans, [c_ref], [c_all_ref], c_sems)

    return pl.pallas_call(
        body, name="first_gather", in_specs=[hbm] * (1 + n), out_specs=[hbm] * (1 + n),
        out_shape=[c_shapes[0], *shapes],
        scratch_shapes=[pltpu.SemaphoreType.DMA((n, N_DEV - 1)), pltpu.SemaphoreType.DMA((n, N_DEV - 1)),
                        pltpu.SemaphoreType.DMA((n,)), *_exchange_sems(1)],
    )(c, *arrays)


def _finish_exchange(big, big_rules, packed, dmod):
    n_rows = packed.shape[0]
    r = n_rows // N_DEV
    shapes, plans = _plans([*big, packed, dmod], [*big_rules, _scatter_rows, _gather])
    n_first = len(plans)
    i_small = n_first - 2
    _, send, land = _gather_rows(jax.ShapeDtypeStruct((r, 128), F32))
    plans = plans + [(None, len(shapes), send, land)]
    shapes = shapes + [jax.ShapeDtypeStruct((n_rows, 128), F32)]
    first = [i for i in range(n_first) if i != i_small]
    hbm = pl.BlockSpec(memory_space=pltpu.HBM)

    def body(*refs):
        srcs, outs = list(refs[:n_first]), refs[n_first:n_first + len(shapes)]
        parts_ref, sum_ref, local_sem = refs[n_first + len(shapes):n_first + len(shapes) + 3]
        sems = refs[n_first + len(shapes) + 3:]
        srcs.append(sum_ref)
        _exchange_start(plans, srcs, outs, sems, only=range(n_first))
        _exchange_wait(plans, srcs, outs, sems, only=[i_small])
        cp = pltpu.make_async_copy(outs[i_small], parts_ref, local_sem)
        cp.start()
        cp.wait()
        g = parts_ref[0]
        for q in range(1, N_DEV):
            g = g + parts_ref[q]
        sum_ref[...] = g
        _exchange_start(plans, srcs, outs, sems, only=[n_first])
        _exchange_wait(plans, srcs, outs, sems, only=[n_first])
        _exchange_wait(plans, srcs, outs, sems, only=first)

    res = pl.pallas_call(
        body, name="finish_exchange", in_specs=[hbm] * n_first, out_specs=[hbm] * len(shapes), out_shape=shapes,
        scratch_shapes=[pltpu.VMEM((N_DEV, r, 128), F32), pltpu.VMEM((r, 128), F32), pltpu.SemaphoreType.DMA(()),
                        *_exchange_sems(len(plans))],
    )(*big, packed, dmod)
    return (*res[:len(big)], res[-1], res[n_first - 1])


def _tiled_call(body, args, *, name, grid, in_specs, out_specs, out_shape, scratch_shapes=(), ride=None, aliases=None):
    params = _params(("arbitrary",) * len(grid))
    if ride is None:
        return pl.pallas_call(body, name=name, grid=grid, in_specs=in_specs, out_specs=out_specs, out_shape=out_shape,
                              scratch_shapes=list(scratch_shapes), input_output_aliases=aliases or {},
                              compiler_params=params)(*args), []
    shapes, plans = ride
    n_in, n_src, n_out, n_dst, n_scr = len(in_specs), len(plans), len(out_specs), len(shapes), len(scratch_shapes)
    hbm = pl.BlockSpec(memory_space=pltpu.HBM)

    def carrying(*refs):
        ins, srcs, refs = refs[:n_in], refs[n_in:n_in + n_src], refs[n_in + n_src:]
        outs, dsts, refs = refs[:n_out], refs[n_out:n_out + n_dst], refs[n_out + n_dst:]
        scratch, sems = refs[:n_scr], refs[n_scr:]
        ids = [pl.program_id(a) for a in range(len(grid))]
        first = functools.reduce(jnp.logical_and, [i == 0 for i in ids])
        last = functools.reduce(jnp.logical_and, [i == g - 1 for i, g in zip(ids, grid)])

        @pl.when(first)
        def _():
            _exchange_start(plans, srcs, dsts, sems)

        body(*ins, *outs, *scratch)

        @pl.when(last)
        def _():
            _exchange_wait(plans, srcs, dsts, sems)

    res = pl.pallas_call(
        carrying, name=name, grid=grid, in_specs=[*in_specs, *[hbm] * n_src], out_specs=[*out_specs, *[hbm] * n_dst],
        out_shape=[*out_shape, *shapes], scratch_shapes=[*scratch_shapes, *_exchange_sems(n_src)],
        input_output_aliases=aliases or {}, compiler_params=params)(*args, *[p[0] for p in plans])
    return res[:n_out], res[n_out:]


def _tail(nd, idx):
    return (slice(None),) * (nd - 2) + idx


def _gather(a):
    return jax.ShapeDtypeStruct((N_DEV,) + a.shape, a.dtype), lambda s, p: s, lambda o, q: o.at[q]


def _gather_rows(a):
    r = a.shape[-2]
    return (jax.ShapeDtypeStruct(a.shape[:-2] + (N_DEV * r, a.shape[-1]), a.dtype), lambda s, p: s,
            lambda o, q: o.at[_tail(a.ndim, (pl.ds(pl.multiple_of(q * r, r), r), slice(None)))])


def _gather_cols(a):
    c = a.shape[-1]
    return (jax.ShapeDtypeStruct(a.shape[:-1] + (N_DEV * c,), a.dtype), lambda s, p: s,
            lambda o, q: o.at[_tail(a.ndim, (slice(None), pl.ds(pl.multiple_of(q * c, c), c)))])


def _scatter_rows(a):
    r = a.shape[0] // N_DEV
    return (jax.ShapeDtypeStruct((N_DEV, r, a.shape[1]), a.dtype),
            lambda s, p: s.at[pl.ds(pl.multiple_of(p * r, r), r), :], lambda o, q: o.at[q])


def _scatter_cols(a):
    c = a.shape[1] // N_DEV
    return (jax.ShapeDtypeStruct((N_DEV, a.shape[0], c), a.dtype),
            lambda s, p: s.at[:, pl.ds(pl.multiple_of(p * c, c), c)], lambda o, q: o.at[q])


def _plans(arrays, rules):
    shapes, plans = [], []
    for o, (a, rule) in enumerate(zip(arrays, rules)):
        shape, send, land = rule(a)
        shapes.append(shape)
        plans.append((a, o, send, land))
    return shapes, plans


def _pack(pieces, rows_multiple=8):
    flat = []
    for a in pieces:
        f = a.reshape(-1)
        flat.append(jnp.pad(f, (0, (-f.shape[0]) % 128)))
    total = sum(f.shape[0] for f in flat)
    flat.append(jnp.zeros(((-total) % (128 * rows_multiple),), F32))
    return jnp.concatenate(flat).reshape(-1, 128)


def _unpack(buf, shapes, lead=()):
    flat = buf.reshape(lead + (-1,))
    out, off = [], 0
    for s in shapes:
        n = math.prod(s)
        out.append(flat[..., off:off + n].reshape(lead + tuple(s)))
        off += n + (-n) % 128
    return out


def _pad_rows(a, rows):
    return jnp.pad(a, ((0, rows - a.shape[0]), (0, 0)))


VEC_NAMES = ('pool_scale', 'b_dw_c', 'ln_g_c', 'ln_b_c', 'b_pw2_c', 'ln_g_d', 'ln_b_d')
GATHERED = ('w_in', 'w_out', 'w_pw2_c', 'w_conv_a', 'w_dw_c')
GATHER_RULES = (_gather_cols, _gather_rows, _gather_rows, _gather, _gather)
SCATTER_RULES = (_scatter_cols, _scatter_rows, _scatter_rows)


def _weight_shards(shard, l):
    return [shard[n][l].astype(BF16) if n in ('w_in', 'w_out') else shard[n][l] for n in GATHERED]


def _layer_weights(shard, l, gathered):
    w_in_b, w_out_b, w_pw2, wconv_parts, wdw_parts = gathered
    wconv = wconv_parts.transpose(1, 0, 2).reshape(CONV_A, GROUP)
    wdw = wdw_parts.transpose(1, 0, 2).reshape(CONV_C, GROUP)
    wp = jnp.einsum('gcd,gh->gchd', shard['w_pool'][l], jnp.eye(4, dtype=F32)).reshape(GROUP, GROUP)
    ws = shard['w_s_d'][l] * jnp.tril(jnp.ones((SUB, SUB), F32))
    vec = jnp.stack([shard[n][l] for n in VEC_NAMES])
    width = jnp.repeat(jnp.asarray([2.0, 4.0, 8.0, 16.0], F32), 64)[None]
    count = jnp.minimum(jnp.arange(1, SUB + 1, dtype=F32)[:, None], width)
    gating_and_counts = jnp.concatenate([jnp.repeat(shard['b_s_d'][l].T, 64, axis=1), 1.0 / count, 1.0 / width,
                                         jnp.zeros((7, GROUP), F32)])
    small = (_pad_rows(wconv, 8), _pad_rows(wdw, HALO), _pad_rows(vec, 16), wp.astype(BF16), w_pw2.astype(BF16),
             ws.reshape(4 * SUB, SUB).astype(BF16), gating_and_counts)
    small_t = (wp.T.astype(BF16), w_pw2.T.astype(BF16), ws.transpose(0, 2, 1).reshape(4 * SUB, SUB).astype(BF16))
    return w_in_b, w_out_b, small, small_t


def kernel(x, c, norm_g, w_ada, b_ada, w_in, w_conv_a, w_pool, pool_scale, w_dw_c, b_dw_c, ln_g_c, ln_b_c, w_pw2_c, b_pw2_c, ln_g_d, ln_b_d, w_s_d, b_s_d, w_out, final_g, loss_target, m_norm_g, m_w_ada, m_b_ada, m_w_in, m_w_conv_a, m_w_pool, m_pool_scale, m_w_dw_c, m_b_dw_c, m_ln_g_c, m_ln_b_c, m_w_pw2_c, m_b_pw2_c, m_ln_g_d, m_ln_b_d, m_w_s_d, m_b_s_d, m_w_out, m_final_g, v_norm_g, v_w_ada, v_b_ada, v_w_in, v_w_conv_a, v_w_pool, v_pool_scale, v_w_dw_c, v_b_dw_c, v_ln_g_c, v_ln_b_c, v_w_pw2_c, v_b_pw2_c, v_ln_g_d, v_ln_b_d, v_w_s_d, v_b_s_d, v_w_out, v_final_g):
    given = dict(locals())
    shard = {n: given[n] for n in WEIGHTS}
    mom_m = {n: given['m_' + n] for n in WEIGHTS}
    mom_v = {n: given['v_' + n] for n in WEIGHTS}
    me = 4 * lax.axis_index("x") + 2 * lax.axis_index("y") + lax.axis_index("c")
    n_tok = x.shape[1]
    tile = min(TOKEN_TILE, n_tok)
    wide_tile = min(2 * TOKEN_TILE, n_tok)
    x0 = x.reshape(n_tok, D_MODEL)
    target = loss_target.reshape(n_tok, D_MODEL)
    ada_cols = w_ada.shape[2]

    first_shards = _weight_shards(shard, 0)
    c_all, w_in_first = _first_gather(c, first_shards[:1], GATHER_RULES[:1])

    b_cols = lax.dynamic_slice_in_dim(b_ada, me * ada_cols, ada_cols, axis=1)
    c_act, mod_cols = _modulation_columns(c_all.reshape(N_DEV, D_MODEL), w_ada, b_cols)
    (mod_all,) = _exchange("gather_modulation", _plans([mod_cols], [_gather]))
    mod = lax.dynamic_index_in_dim(mod_all, me, axis=2, keepdims=False)
    mod = mod.transpose(1, 0, 2).reshape(N_LAYERS, 3 * D_MODEL)
    shift, scale, gate = (mod[:, k * D_MODEL:(k + 1) * D_MODEL].reshape(N_LAYERS, 1, D_MODEL) for k in range(3))
    gs = norm_g.reshape(N_LAYERS, 1, D_MODEL) * (1.0 + scale)

    xs, hs, zs, ocs, layers = [x0], [], [], [], []
    for l in range(N_LAYERS):
        if l == 0:
            (h, z), rest = _in_proj(xs[0], gs[0], shift[0], w_in_first, wide_tile,
                                    ride=_plans(first_shards[1:], GATHER_RULES[1:]))
            layers.append(_layer_weights(shard, 0, [w_in_first, *rest]))
        else:
            (h, z), _ = _in_proj(xs[l], gs[l], shift[l], layers[l][0], wide_tile)
        _, w_out_b, small, _ = layers[l]
        hs.append(h)
        zs.append(z)
        if l + 1 < N_LAYERS:
            (x_next, o_c), gathered = _mix_out(z, xs[l], gate[l], small, w_out_b, tile,
                                               ride=_plans(_weight_shards(shard, l + 1), GATHER_RULES))
            xs.append(x_next)
            layers.append(_layer_weights(shard, l + 1, gathered))
        else:
            (dx, o_c, loss_part, dfinal_g), _ = _mix_out(z, xs[l], gate[l], small, w_out_b, tile,
                                                         head=(final_g.reshape(1, D_MODEL), target))
        ocs.append(o_c)

    part = {}
    layer_parts = [None] * N_LAYERS
    slots = [None] * N_LAYERS
    for l in reversed(range(N_LAYERS)):
        w_in_b, w_out_b, small, small_t = layers[l]
        ride = _plans(layer_parts[l + 1]['big'], SCATTER_RULES) if l + 1 < N_LAYERS else None
        (dz, ycat, sums, dwp, dw2, dws, dbs), rode = _mix_bwd(zs[l], ocs[l], dx, gate[l], small, small_t, w_out_b, tile,
                                                              ride=ride)
        if ride:
            slots[l + 1] = rode
        (dw_out, dgate), _ = _tokens_matmul(ycat, dx, "out_proj_tokens_matmul", out_dtype=BF16, gated=(w_out_b, gate[l]))
        if l > 0:
            dw_in, _ = _tokens_matmul(hs[l], dz, "in_proj_tokens_matmul", out_dtype=BF16)
        else:
            dw_in, (slots_out, slots_pw2) = _tokens_matmul(
                hs[l], dz, "in_proj_tokens_matmul", out_dtype=BF16, a_cols=(0, 3 * D_MODEL // 4),
                ride=_plans([dw_out, dw2], SCATTER_RULES[1:]))
            dw_in_last, (slots_in,) = _tokens_matmul(
                hs[l], dz, "in_proj_tokens_matmul", out_dtype=BF16, a_cols=(3, D_MODEL // 4),
                ride=_plans([dw_in], SCATTER_RULES[:1]))
            slots[l] = [slots_in, slots_out, slots_pw2]
        (dx, dshift, dgs), _ = _norm_bwd(xs[l], dz, dx, gs[l], w_in_b, tile)
        layer_parts[l] = dict(
            big=[dw_in, dw_out, dw2],
            b_ada=jnp.concatenate([dshift, dgs * norm_g[l][None], dgate], axis=1)[0],
            norm_g=(dgs * (1.0 + scale[l]))[0], sums=sums,
            w_pool=jnp.einsum('gchd,gh->gcd', dwp.reshape(4, 64, 4, 64), jnp.eye(4, dtype=F32)),
            w_s_d=dws.reshape(4, SUB, SUB) * jnp.tril(jnp.ones((SUB, SUB), F32)),
            b_s_d=dbs.reshape(SUB, 4, 64).sum(axis=-1).T)
    grad_x = dx.reshape(x.shape)
    small_names = REPLICATED + CHANNEL_SHARDED
    packed_names = [n for n in small_names if n not in SUM_ROWS] + ['sums']
    for n in packed_names:
        part[n] = dfinal_g[0] if n == 'final_g' else jnp.stack([layer_parts[l][n] for l in range(N_LAYERS)])

    small_shapes = [part[n].shape for n in packed_names] + [(1, 128)]
    slots_in_last, small_sum, dmod_all = _finish_exchange(
        [dw_in_last], SCATTER_RULES[:1],
        _pack([part[n] for n in packed_names] + [loss_part[0:1]], rows_multiple=8 * N_DEV), part['b_ada'])

    grads, deltas, new_m, new_v = {}, {}, {}, {}
    half = 3 * D_MODEL // 4
    for j, n in enumerate(('w_in', 'w_out', 'w_pw2_c')):
        outs = [_adam_update(shard[n][l], slots[l][j], mom_m[n][l], mom_v[n][l], "update_" + n)
                for l in range(1, N_LAYERS)]
        if n == 'w_in':
            halves = [_adam_update(shard[n][0][rows], s, mom_m[n][0][rows], mom_v[n][0][rows], "update_" + n)
                      for rows, s in ((slice(0, half), slots[0][0]), (slice(half, None), slots_in_last))]
            outs.insert(0, [jnp.concatenate(o) for o in zip(*halves)])
        else:
            outs.insert(0, _adam_update(shard[n][0], slots[0][j], mom_m[n][0], mom_v[n][0], "update_" + n))
        grads[n], deltas[n], new_m[n], new_v[n] = (jnp.stack(o) for o in zip(*outs))

    *small_sums, loss_sum = _unpack(small_sum, small_shapes)
    loss = loss_sum[0, 0]
    gsum = dict(zip(packed_names, small_sums))
    for n, rows in SUM_ROWS.items():
        gsum[n] = gsum['sums'][:, rows]
    for n in CHANNEL_SHARDED:
        width = shard[n].shape[2]
        gsum[n] = lax.dynamic_slice_in_dim(gsum[n], me * width, width, axis=2)
    d_small, m_small, v_small = _adam_many(*[[_as_rows(d[n]) for n in small_names] for d in (shard, gsum, mom_m, mom_v)],
                                           "update_small")
    for j, n in enumerate(small_names):
        grads[n] = gsum[n]
        deltas[n], new_m[n], new_v[n] = (o[j].reshape(shard[n].shape) for o in (d_small, m_small, v_small))

    dmod_cols = lax.dynamic_slice_in_dim(dmod_all, me * ada_cols, ada_cols, axis=2).transpose(1, 0, 2)
    grads['w_ada'], deltas['w_ada'], new_m['w_ada'], new_v['w_ada'] = _ada_update(
        c_act.T, dmod_cols, w_ada, m_w_ada, v_w_ada)

    return (loss, grad_x, *[grads[n] for n in WEIGHTS], *[deltas[n] for n in WEIGHTS],
            *[new_m[n] for n in WEIGHTS], *[new_v[n] for n in WEIGHTS])
```

```python
import functools
import math

import jax
import jax.numpy as jnp
from jax import lax
from jax.experimental import pallas as pl
from jax.experimental.pallas import tpu as pltpu

F32 = jnp.float32
BF16 = jnp.bfloat16

N_DEV = 8
D_MODEL = 1024
GROUP = 256
D_IN = 12 * GROUP
N_LAYERS = 2
HALO = 32
SUB = 128
WIN = SUB + HALO
TOKEN_TILE = 512
REDUCE_TILE = 2048
EPS = 1e-6
VMEM_BYTES_V7X = 64 * 1024 * 1024
VMEM_LIMIT = VMEM_BYTES_V7X - 8 * 1024 * 1024

ADAM_LR = 0.001
ADAM_B1 = 0.9
ADAM_B2 = 0.999
ADAM_EPS = 1e-08
ADAM_WD = 0.01
ADAM_STEP = 10

A_B, A_C, A_X, A_G, B_P, B_G, C_A, C_GL, C_G, D_U, D_V, D_G = range(12)
V_PSCALE, V_BDW, V_LNGC, V_LNBC, V_BPW2, V_LNGD, V_LNBD = range(7)
S_WCONV, S_PSCALE, S_BDW, S_LNGC, S_LNBC, S_BPW2, S_LNGD, S_LNBD, S_WDW = 0, 3, 4, 5, 6, 7, 8, 9, 16
N_SUMS = 64
CONV_A = 3
CONV_C = 31
SUM_ROWS = dict(w_conv_a=slice(S_WCONV, S_WCONV + CONV_A), w_dw_c=slice(S_WDW, S_WDW + CONV_C), pool_scale=S_PSCALE,
                b_dw_c=S_BDW, ln_g_c=S_LNGC, ln_b_c=S_LNBC, b_pw2_c=S_BPW2, ln_g_d=S_LNGD, ln_b_d=S_LNBD)

WEIGHTS = ('norm_g', 'w_ada', 'b_ada', 'w_in', 'w_conv_a', 'w_pool', 'pool_scale', 'w_dw_c', 'b_dw_c', 'ln_g_c',
           'ln_b_c', 'w_pw2_c', 'b_pw2_c', 'ln_g_d', 'ln_b_d', 'w_s_d', 'b_s_d', 'w_out', 'final_g')
REPLICATED = ('norm_g', 'b_ada', 'w_pool', 'pool_scale', 'b_dw_c', 'ln_g_c', 'ln_b_c', 'b_pw2_c', 'ln_g_d', 'ln_b_d',
              'w_s_d', 'b_s_d', 'final_g')
CHANNEL_SHARDED = ('w_conv_a', 'w_dw_c')


def _params(semantics, vmem=VMEM_LIMIT):
    return pltpu.CompilerParams(dimension_semantics=semantics, vmem_limit_bytes=vmem)


def _cols(g):
    return slice(g * GROUP, (g + 1) * GROUP)


def _full(shape):
    return pl.BlockSpec(shape, lambda *_: (0,) * len(shape))


def _silu(x):
    s = jax.nn.sigmoid(x)
    return x * s, s


def _dsilu(sg, s):
    return s + sg * (1.0 - s)


_GELU_C0 = math.sqrt(2.0 / math.pi)
_GELU_C1 = 0.044715


def _gelu(x):
    x2 = x * x
    th = jnp.tanh(_GELU_C0 * (x + _GELU_C1 * (x * x2)))
    p = 0.5 + 0.5 * th
    return x * p, (th, p, x2)


def _dgelu(x, aux):
    th, p, x2 = aux
    return p + (0.5 * x) * (1.0 - th * th) * (_GELU_C0 + (3.0 * _GELU_C0 * _GELU_C1) * x2)


def _layer_norm(x):
    mu = jnp.mean(x, axis=-1, keepdims=True)
    xc = x - mu
    rstd = lax.rsqrt(jnp.mean(xc * xc, axis=-1, keepdims=True) + EPS)
    return xc * rstd, rstd


def _layer_norm_bwd(dn, n, rstd):
    return rstd * (dn - jnp.mean(dn, axis=-1, keepdims=True) - n * jnp.mean(dn * n, axis=-1, keepdims=True))


def _shift_rows(a, k):
    k = k % a.shape[0]
    return a if k == 0 else pltpu.roll(a, k, 0)


def _row_sum8(a):
    s = a[0:8]
    for m in range(1, a.shape[0] // 8):
        s = s + a[8 * m:8 * m + 8]
    return s


def _lane():
    return lax.broadcasted_iota(jnp.int32, (SUB, GROUP), 1)


def _by_quarter(lane, parts):
    return jnp.where(lane < 64, parts[0], jnp.where(lane < 128, parts[1], jnp.where(lane < 192, parts[2], parts[3])))


def _conv_inputs(z_ref, rows):
    def f(g):
        return z_ref[rows, _cols(g)].astype(F32)
    return f(A_C) * f(A_X), f(B_P), f(C_A) * jax.nn.sigmoid(f(C_GL))


def _fill_past(past_ref, zh_ref, zm_ref, is_first, tile):
    parts = _conv_inputs(zh_ref, slice(None))
    for n, a in enumerate(parts):
        past_ref[0:HALO, _cols(n)] = jnp.where(is_first, 0.0, a)

    def body(j, carry):
        r0 = pl.multiple_of(j * SUB, SUB)
        for n, a in enumerate(_conv_inputs(zm_ref, pl.ds(r0, SUB))):
            past_ref[pl.ds(r0 + HALO, SUB), _cols(n)] = a
        return carry

    lax.fori_loop(0, tile // SUB, body, 0)


def _short_conv_taps(qw):
    return [_shift_rows(qw, CONV_A - 1 - k)[HALO:WIN] for k in range(CONV_A)]


def _doubling_sums(w, back, keep):
    n = w.shape[0]
    half = GROUP // 2
    lane = lax.broadcasted_iota(jnp.int32, (SUB, half), 1)

    def grow(s, k):
        return s + _shift_rows(s, k if back else n - k)

    lo2 = grow(w[:, :half], 1)
    lo4 = grow(lo2, 2)
    hi8 = grow(grow(grow(w[:, half:], 1), 2), 4)
    hi16 = grow(hi8, 8)
    return jnp.concatenate([jnp.where(lane < 64, lo2[keep], lo4[keep]), jnp.where(lane < 64, hi8[keep], hi16[keep])],
                           axis=1)


def _window_sums(pw):
    return _doubling_sums(pw, True, slice(HALO, WIN))


def _forward_window_sums(ew):
    return _doubling_sums(ew, False, slice(0, SUB))


def _inv_count(bs_ref, t_first):
    return jnp.where(t_first == 0, bs_ref[SUB:2 * SUB, :], bs_ref[2 * SUB:2 * SUB + 1, :])


def _mixer_forwards(zc, win, t_first, wc_ref, wdw_ref, vec_ref, wp_ref, w2_ref, ws_ref, bs_ref, o_c=None):
    def vec(n):
        return vec_ref[n:n + 1, :]

    def short_conv():
        taps = _short_conv_taps(win(0))
        o_a = wc_ref[0:1, :] * taps[0] + wc_ref[1:2, :] * taps[1] + wc_ref[2:3, :] * taps[2]
        a_b, a_g = zc(A_B), zc(A_G)
        sg_a, s_a = _silu(a_g)
        return a_b * o_a * sg_a, dict(taps=taps, o_a=o_a, a_b=a_b, a_g=a_g, sg_a=sg_a, s_a=s_a)

    def pooling():
        pw = win(1)
        ic = _inv_count(bs_ref, t_first)
        pooled_b = (_window_sums(pw) * ic - pw[HALO:WIN]).astype(BF16)
        y0_b = jnp.dot(pooled_b, wp_ref[...], preferred_element_type=F32)
        b_g = zc(B_G)
        sg_b, s_b = _silu(b_g)
        return y0_b * vec(V_PSCALE) * sg_b, dict(ic=ic, pooled_b=pooled_b, y0_b=y0_b, b_g=b_g, sg_b=sg_b, s_b=s_b)

    def conformer():
        hw = win(2)
        o = o_c
        if o is None:
            o = wdw_ref[CONV_C - 1:CONV_C, :] * hw[HALO:WIN] + vec(V_BDW)
            for k in range(CONV_C - 1):
                o = o + wdw_ref[k:k + 1, :] * _shift_rows(hw, CONV_C - 1 - k)[HALO:WIN]
        n_c, rstd_c = _layer_norm(o)
        ln_c = n_c * vec(V_LNGC) + vec(V_LNBC)
        sl_c, ssl_c = _silu(ln_c)
        sl_b = sl_c.astype(BF16)
        yc = jnp.dot(sl_b, w2_ref[...], preferred_element_type=F32) + vec(V_BPW2)
        c_g = zc(C_G)
        sg_c, s_c = _silu(c_g)
        return yc * sg_c, dict(hw=hw, o_c=o, n_c=n_c, rstd_c=rstd_c, sl_c=sl_c, ssl_c=ssl_c, sl_b=sl_b, yc=yc, c_g=c_g,
                               sg_c=sg_c, s_c=s_c)

    def gating():
        lane = _lane()
        d_u, d_v, d_g = zc(D_U), zc(D_V), zc(D_G)
        u, aux_u = _gelu(d_u)
        gv, aux_v = _gelu(d_v)
        n_d, rstd_d = _layer_norm(gv)
        v_b = (n_d * vec(V_LNGD) + vec(V_LNBD)).astype(BF16)
        r = jnp.dot(ws_ref[...], v_b, preferred_element_type=F32)
        mixed = _by_quarter(lane, [r[h * SUB:(h + 1) * SUB] for h in range(4)]) + bs_ref[0:SUB, :]
        sg_d, s_d = _silu(d_g)
        return u * mixed * sg_d, dict(d_u=d_u, d_v=d_v, u=u, aux_u=aux_u, aux_v=aux_v, n_d=n_d, rstd_d=rstd_d,
                                      v_b=v_b, mixed=mixed, sg_d=sg_d, s_d=s_d)

    return short_conv, pooling, conformer, gating


def _in_proj(x, gs, shift, w_in_b, tile, ride=None):
    n_tok = x.shape[0]

    def body(x_ref, gs_ref, sh_ref, w_ref, h_ref, z_ref):
        xv = x_ref[...]
        r = lax.rsqrt(jnp.mean(xv * xv, axis=-1, keepdims=True) + EPS)
        h = ((xv * r) * gs_ref[...] + sh_ref[...]).astype(BF16)
        h_ref[...] = h
        for j in range(D_IN // D_MODEL):
            cs = slice(j * D_MODEL, (j + 1) * D_MODEL)
            z_ref[:, cs] = jnp.dot(h, w_ref[:, cs], preferred_element_type=F32).astype(BF16)

    return _tiled_call(
        body, (x, gs, shift, w_in_b), name="in_proj", grid=(n_tok // tile,),
        in_specs=[pl.BlockSpec((tile, D_MODEL), lambda i: (i, 0)), _full((1, D_MODEL)), _full((1, D_MODEL)),
                  _full((D_MODEL, D_IN))],
        out_specs=[pl.BlockSpec((tile, D_MODEL), lambda i: (i, 0)), pl.BlockSpec((tile, D_IN), lambda i: (i, 0))],
        out_shape=[jax.ShapeDtypeStruct((n_tok, D_MODEL), BF16), jax.ShapeDtypeStruct((n_tok, D_IN), BF16)],
        ride=ride)


def _small_specs(with_transposes):
    specs = [_full((8, GROUP)), _full((HALO, GROUP)), _full((16, GROUP)), _full((GROUP, GROUP)), _full((GROUP, GROUP)),
             _full((4 * SUB, SUB)), _full((2 * SUB + 8, GROUP))]
    if with_transposes:
        specs += [_full((GROUP, GROUP)), _full((GROUP, GROUP)), _full((4 * SUB, SUB))]
    return specs


def _mix_out(z, x, gate, small, w_out_b, tile, ride=None, head=None):
    n_tok = x.shape[0]
    n_tiles = n_tok // tile
    n_sub = tile // SUB
    cw = D_MODEL // n_sub
    per_halo = tile // HALO
    n_in = 12 + (2 if head else 0)
    n_out = 4 if head else 2

    def cur(i):
        return jnp.minimum(i, n_tiles - 1)

    def prev(i):
        return jnp.maximum(i - 1, 0)

    def body(*refs):
        (zm_ref, zh_ref, x_ref, gate_ref, wc_ref, wdw_ref, vec_ref, wp_ref, w2_ref, ws_ref, bs_ref, wout_ref) = refs[:12]
        xo_ref, oc_ref = refs[n_in:n_in + 2]
        past_ref, ycat_ref, ycat_prev_ref = refs[n_in + n_out:n_in + n_out + 3]
        i = pl.program_id(0)
        t = cur(i)
        if head:
            g_ref, tgt_ref = refs[12:14]
            loss_ref, dg_ref = refs[n_in + 2:n_in + 4]
            xn_ref, acc_ref = refs[n_in + n_out + 3:]
        else:
            xn_ref = xo_ref

        @pl.when(i == 0)
        def _():
            ycat_prev_ref[...] = jnp.zeros_like(ycat_prev_ref)
            if head:
                acc_ref[...] = jnp.zeros_like(acc_ref)

        _fill_past(past_ref, zh_ref, zm_ref, t == 0, tile)
        for j in range(n_sub):
            cs = slice(j * cw, (j + 1) * cw)
            y = jnp.dot(ycat_prev_ref[...], wout_ref[:, cs], preferred_element_type=F32)
            xn_ref[:, cs] = x_ref[:, cs] + gate_ref[:, cs] * y
            rows = slice(j * SUB, (j + 1) * SUB)
            mixers = _mixer_forwards(
                lambda g: zm_ref[rows, _cols(g)].astype(F32), lambda n: past_ref[j * SUB:j * SUB + WIN, _cols(n)],
                t * tile + j * SUB, wc_ref, wdw_ref, vec_ref, wp_ref, w2_ref, ws_ref, bs_ref)
            for n, mixer in enumerate(mixers):
                y, s = mixer()
                ycat_ref[rows, _cols(n)] = y.astype(BF16)
                if "o_c" in s:
                    oc_ref[rows, :] = s["o_c"]
        ycat_prev_ref[...] = ycat_ref[...]
        if head:
            counted = jnp.where(i > 0, 1.0, 0.0)
            xo_ref[...] = _loss_head_block(xn_ref[...], g_ref[...], tgt_ref[...], acc_ref, counted)

            @pl.when(i == n_tiles)
            def _():
                loss_ref[...] = jnp.full((8, 128), 0.5 / D_MODEL, F32) * jnp.sum(acc_ref[0])
                dg_ref[...] = jnp.sum(acc_ref[1], axis=0, keepdims=True)

    in_specs = [pl.BlockSpec((tile, D_IN), lambda i: (cur(i), 0)),
                pl.BlockSpec((HALO, D_IN), lambda i: (jnp.maximum(cur(i) * per_halo - 1, 0), 0)),
                pl.BlockSpec((tile, D_MODEL), lambda i: (prev(i), 0)), _full((1, D_MODEL)),
                *_small_specs(False), _full((D_MODEL, D_MODEL))]
    out_specs = [pl.BlockSpec((tile, D_MODEL), lambda i: (prev(i), 0)), pl.BlockSpec((tile, GROUP), lambda i: (cur(i), 0))]
    out_shape = [jax.ShapeDtypeStruct((n_tok, D_MODEL), F32), jax.ShapeDtypeStruct((n_tok, GROUP), F32)]
    scratch = [pltpu.VMEM((tile + HALO, 3 * GROUP), F32), pltpu.VMEM((tile, D_MODEL), BF16), pltpu.VMEM((tile, D_MODEL), BF16)]
    args = (z, z, x, gate, *small, w_out_b)
    if head:
        in_specs += [_full((1, D_MODEL)), pl.BlockSpec((tile, D_MODEL), lambda i: (prev(i), 0))]
        out_specs += [_full((8, 128)), _full((1, D_MODEL))]
        out_shape += [jax.ShapeDtypeStruct((8, 128), F32), jax.ShapeDtypeStruct((1, D_MODEL), F32)]
        scratch += [pltpu.VMEM((tile, D_MODEL), F32), pltpu.VMEM((2, 8, D_MODEL), F32)]
        args += tuple(head)
    outs, rode = _tiled_call(body, args, name="mix_out", grid=(n_tiles + 1,), in_specs=in_specs, out_specs=out_specs,
                             out_shape=out_shape, scratch_shapes=scratch, ride=ride)
    return outs, rode


def _loss_head_block(xv, g, target, acc_ref, counted):
    r = lax.rsqrt(jnp.mean(xv * xv, axis=-1, keepdims=True) + EPS)
    xn = xv * r
    err = xn * g - target
    acc_ref[0] = acc_ref[0] + counted * _row_sum8(err * err)
    dy = err * (1.0 / D_MODEL)
    acc_ref[1] = acc_ref[1] + counted * _row_sum8(dy * xn)
    a = dy * g
    return r * (a - xn * jnp.mean(a * xn, axis=-1, keepdims=True))


def _mix_bwd(z, o_c, dx_next, gate, small, small_t, w_out_b, tile, ride=None):
    n_tok = z.shape[0]
    n_tiles = n_tok // tile
    n_sub = tile // SUB
    cw = D_MODEL // n_sub
    per_halo = tile // HALO
    nt_dims = (((1,), (1,)), ((), ()))

    def tile_of(i):
        return n_tiles - 1 - i

    def next_tile_of(i):
        return jnp.maximum(n_tiles - 2 - i, 0)

    def body(zm_ref, zh_ref, oc_ref, dxn_ref, dxn_next_ref, gate_ref, wc_ref, wdw_ref, vec_ref, wp_ref, w2_ref, ws_ref,
             bs_ref, wpt_ref, w2t_ref, wst_ref, wout_ref,
             dz_ref, ycat_ref, sums_ref, dwp_ref, dw2_ref, dws_ref, dbs_ref,
             past_ref, future_ref, dy_ref, dy_next_ref, acc_ref):
        i = pl.program_id(0)
        t = tile_of(i)

        @pl.when(i == 0)
        def _():
            acc_ref[...] = jnp.zeros_like(acc_ref)
            dwp_ref[...] = jnp.zeros_like(dwp_ref)
            dw2_ref[...] = jnp.zeros_like(dw2_ref)
            dws_ref[...] = jnp.zeros_like(dws_ref)
            dbs_ref[...] = jnp.zeros_like(dbs_ref)
            future_ref[tile:tile + HALO, :] = jnp.zeros((HALO, 3 * GROUP), F32)
            dy_ref[...] = lax.dot_general((dxn_ref[...] * gate_ref[...]).astype(BF16), wout_ref[...], nt_dims,
                                        preferred_element_type=F32)

        _fill_past(past_ref, zh_ref, zm_ref, t == 0, tile)
        dyb_next = (dxn_next_ref[...] * gate_ref[...]).astype(BF16)

        def vec(n):
            return vec_ref[n:n + 1, :]

        for jj in range(n_sub):
            j = n_sub - 1 - jj
            r0 = j * SUB
            rows = slice(r0, r0 + SUB)

            def zc(g):
                return zm_ref[rows, _cols(g)].astype(F32)

            def add(n, a):
                acc_ref[n] = acc_ref[n] + _row_sum8(a)

            def put(g, a):
                dz_ref[rows, _cols(g)] = a.astype(BF16)

            def future_window(n, a):
                future_ref[rows, _cols(n)] = a
                return future_ref[r0:r0 + WIN, _cols(n)]

            short_conv, pooling, conformer, gating = _mixer_forwards(
                zc, lambda n: past_ref[r0:r0 + WIN, _cols(n)], t * tile + r0,
                wc_ref, wdw_ref, vec_ref, wp_ref, w2_ref, ws_ref, bs_ref, o_c=oc_ref[rows, :])
            lane = _lane()
            ks = slice(jj * cw, (jj + 1) * cw)
            dy_next_ref[:, ks] = lax.dot_general(dyb_next, wout_ref[ks, :], nt_dims, preferred_element_type=F32)

            y, s = short_conv()
            ycat_ref[rows, _cols(0)] = y.astype(BF16)
            dy = dy_ref[rows,_cols(0)]
            put(A_B, dy * s["o_a"] * s["sg_a"])
            put(A_G, dy * s["a_b"] * s["o_a"] * _dsilu(s["sg_a"], s["s_a"]))
            do = dy * s["a_b"] * s["sg_a"]
            for k in range(CONV_A):
                add(S_WCONV + k, do * s["taps"][k])
            dow = future_window(0, do)
            dq = wc_ref[CONV_A - 1:CONV_A, :] * dow[0:SUB]
            for k in range(CONV_A - 1):
                dq = dq + wc_ref[k:k + 1, :] * _shift_rows(dow, WIN - (CONV_A - 1 - k))[0:SUB]
            put(A_C, dq * zc(A_X))
            put(A_X, dq * zc(A_C))

            y, s = pooling()
            ycat_ref[rows, _cols(1)] = y.astype(BF16)
            dy = dy_ref[rows,_cols(1)]
            put(B_G, dy * (s["y0_b"] * vec(V_PSCALE)) * _dsilu(s["sg_b"], s["s_b"]))
            dyb = dy * s["sg_b"]
            add(S_PSCALE, dyb * s["y0_b"])
            dpw_b = (dyb * vec(V_PSCALE)).astype(BF16)
            dwp_ref[...] += lax.dot_general(s["pooled_b"], dpw_b, (((0,), (0,)), ((), ())), preferred_element_type=F32)
            dpooled = jnp.dot(dpw_b, wpt_ref[...], preferred_element_type=F32)
            ew = future_window(1, dpooled * s["ic"])
            put(B_P, _forward_window_sums(ew) - dpooled)

            y, s = conformer()
            ycat_ref[rows, _cols(2)] = y.astype(BF16)
            dy = dy_ref[rows,_cols(2)]
            put(C_G, dy * s["yc"] * _dsilu(s["sg_c"], s["s_c"]))
            dyc = dy * s["sg_c"]
            add(S_BPW2, dyc)
            dyc_b = dyc.astype(BF16)
            dw2_ref[...] += lax.dot_general(s["sl_b"], dyc_b, (((0,), (0,)), ((), ())), preferred_element_type=F32)
            dln = jnp.dot(dyc_b, w2t_ref[...], preferred_element_type=F32) * _dsilu(s["sl_c"], s["ssl_c"])
            add(S_LNGC, dln * s["n_c"])
            add(S_LNBC, dln)
            do = _layer_norm_bwd(dln * vec(V_LNGC), s["n_c"], s["rstd_c"])
            add(S_BDW, do)
            hw = s["hw"]
            for k in range(CONV_C):
                add(S_WDW + k, do * _shift_rows(hw, CONV_C - 1 - k)[HALO:WIN])
            dow = future_window(2, do)
            dhc = wdw_ref[CONV_C - 1:CONV_C, :] * dow[0:SUB]
            for k in range(CONV_C - 1):
                dhc = dhc + wdw_ref[k:k + 1, :] * _shift_rows(dow, WIN - (CONV_C - 1 - k))[0:SUB]
            c_a = zc(C_A)
            sgl = jax.nn.sigmoid(zc(C_GL))
            put(C_A, dhc * sgl)
            put(C_GL, dhc * c_a * sgl * (1.0 - sgl))

            y, s = gating()
            ycat_ref[rows, _cols(3)] = y.astype(BF16)
            dy = dy_ref[rows,_cols(3)]
            put(D_G, dy * s["u"] * s["mixed"] * _dsilu(s["sg_d"], s["s_d"]))
            put(D_U, dy * s["mixed"] * s["sg_d"] * _dgelu(s["d_u"], s["aux_u"]))
            dmixed = dy * s["u"] * s["sg_d"]
            dbs_ref[...] += dmixed
            by_head = jnp.concatenate(
                [jnp.where((lane >= 64 * h) & (lane < 64 * h + 64), dmixed, 0.0) for h in range(4)], axis=0).astype(BF16)
            dws_ref[...] += lax.dot_general(by_head, s["v_b"], (((1,), (1,)), ((), ())), preferred_element_type=F32)
            rv = jnp.dot(wst_ref[...], dmixed.astype(BF16), preferred_element_type=F32)
            dv = _by_quarter(lane, [rv[h * SUB:(h + 1) * SUB] for h in range(4)])
            add(S_LNGD, dv * s["n_d"])
            add(S_LNBD, dv)
            dgv = _layer_norm_bwd(dv * vec(V_LNGD), s["n_d"], s["rstd_d"])
            put(D_V, dgv * _dgelu(s["d_v"], s["aux_v"]))

        future_ref[tile:tile + HALO, :] = future_ref[0:HALO, :]
        dy_ref[...] = dy_next_ref[...]

        @pl.when(i == n_tiles - 1)
        def _():
            for n in range(N_SUMS):
                sums_ref[n:n + 1, :] = jnp.sum(acc_ref[n], axis=0, keepdims=True)

    return _tiled_call(
        body, (z, z, o_c, dx_next, dx_next, gate, *small, *small_t, w_out_b), name="mix_bwd", grid=(n_tiles,),
        in_specs=[pl.BlockSpec((tile, D_IN), lambda i: (tile_of(i), 0)),
                  pl.BlockSpec((HALO, D_IN), lambda i: (jnp.maximum(tile_of(i) * per_halo - 1, 0), 0)),
                  pl.BlockSpec((tile, GROUP), lambda i: (tile_of(i), 0)),
                  pl.BlockSpec((tile, D_MODEL), lambda i: (tile_of(i), 0)),
                  pl.BlockSpec((tile, D_MODEL), lambda i: (next_tile_of(i), 0)), _full((1, D_MODEL)),
                  *_small_specs(True), _full((D_MODEL, D_MODEL))],
        out_specs=[pl.BlockSpec((tile, D_IN), lambda i: (tile_of(i), 0)),
                   pl.BlockSpec((tile, D_MODEL), lambda i: (tile_of(i), 0)),
                   _full((N_SUMS, GROUP)), _full((GROUP, GROUP)), _full((GROUP, GROUP)), _full((4 * SUB, SUB)),
                   _full((SUB, GROUP))],
        out_shape=[jax.ShapeDtypeStruct((n_tok, D_IN), BF16), jax.ShapeDtypeStruct((n_tok, D_MODEL), BF16),
                   jax.ShapeDtypeStruct((N_SUMS, GROUP), F32), jax.ShapeDtypeStruct((GROUP, GROUP), F32),
                   jax.ShapeDtypeStruct((GROUP, GROUP), F32), jax.ShapeDtypeStruct((4 * SUB, SUB), F32),
                   jax.ShapeDtypeStruct((SUB, GROUP), F32)],
        scratch_shapes=[pltpu.VMEM((tile + HALO, 3 * GROUP), F32), pltpu.VMEM((tile + HALO, 3 * GROUP), F32),
                        pltpu.VMEM((tile, D_MODEL), F32), pltpu.VMEM((tile, D_MODEL), F32),
                        pltpu.VMEM((N_SUMS, 8, GROUP), F32)], ride=ride)


def _norm_bwd(x, dz, dx_next, gs, w_in_b, tile, ride=None, blocks=None, begun=None, finish=True):
    n_tok = x.shape[0]
    first, n_tiles = blocks or (0, n_tok // tile)
    n_in = 5 + (2 if begun else 0)

    def body(*refs):
        x_ref, dz_ref, dxn_ref, gs_ref, w_ref = refs[:5]
        dx_ref = refs[n_in]
        acc_ref = refs[-1]
        i = pl.program_id(0)

        @pl.when(i == 0)
        def _():
            acc_ref[...] = refs[6][...] if begun else jnp.zeros_like(acc_ref)

        dh = lax.dot_general(dz_ref[...], w_ref[...], (((1,), (1,)), ((), ())), preferred_element_type=F32)
        xv = x_ref[...]
        r = lax.rsqrt(jnp.mean(xv * xv, axis=-1, keepdims=True) + EPS)
        xn = xv * r
        acc_ref[0] = acc_ref[0] + _row_sum8(dh)
        acc_ref[1] = acc_ref[1] + _row_sum8(dh * xn)
        dxn = dh * gs_ref[...]
        dx_ref[...] = dxn_ref[...] + r * (dxn - xn * jnp.mean(dxn * xn, axis=-1, keepdims=True))

        @pl.when(i == n_tiles - 1)
        def _():
            if finish:
                refs[n_in + 1][...] = jnp.sum(acc_ref[0], axis=0, keepdims=True)
                refs[n_in + 2][...] = jnp.sum(acc_ref[1], axis=0, keepdims=True)
            else:
                refs[n_in + 1][...] = acc_ref[...]

    def rows(i):
        return (first + i, 0)

    in_specs = [pl.BlockSpec((tile, D_MODEL), rows), pl.BlockSpec((tile, D_IN), rows), pl.BlockSpec((tile, D_MODEL), rows),
                _full((1, D_MODEL)), _full((D_MODEL, D_IN))]
    args = (x, dz, dx_next, gs, w_in_b)
    if begun:
        in_specs += [pl.BlockSpec(memory_space=pl.ANY), _full((2, 8, D_MODEL))]
        args += tuple(begun)
    vec = jax.ShapeDtypeStruct((1, D_MODEL), F32)
    return _tiled_call(
        body, args, name="norm_bwd", grid=(n_tiles,), in_specs=in_specs,
        out_specs=[pl.BlockSpec((tile, D_MODEL), rows)] + ([_full((1, D_MODEL))] * 2 if finish else [_full((2, 8, D_MODEL))]),
        out_shape=[jax.ShapeDtypeStruct((n_tok, D_MODEL), F32)]
        + ([vec, vec] if finish else [jax.ShapeDtypeStruct((2, 8, D_MODEL), F32)]),
        scratch_shapes=[pltpu.VMEM((2, 8, D_MODEL), F32)], ride=ride, aliases={5: 0} if begun else None)


def _tokens_matmul(a, b, name, out_dtype=F32, ride=None, a_cols=None, gated=None):
    n_tok = a.shape[0]
    a_block, ka = a_cols or (0, a.shape[1])
    nb = b.shape[1]
    tk = min(REDUCE_TILE * (4 // b.dtype.itemsize), n_tok)
    cb = min(D_MODEL, nb)
    n_steps = n_tok // tk
    n_in = 4 if gated else 2

    def body(*refs):
        a_ref, b_ref, o_ref, acc_ref = refs[0], refs[1], refs[n_in], refs[-1]
        i = pl.program_id(1)

        @pl.when(i == 0)
        def _():
            acc_ref[...] = jnp.zeros_like(acc_ref)

        acc_ref[...] += lax.dot_general(a_ref[...], b_ref[...].astype(BF16), (((0,), (0,)), ((), ())),
                                        preferred_element_type=F32)

        @pl.when(i == n_steps - 1)
        def _():
            m = acc_ref[...]
            if gated:
                w_ref, gate_ref, dgate_ref = refs[2], refs[3], refs[n_in + 1]
                o_ref[...] = (m * gate_ref[...]).astype(out_dtype)
                dgate_ref[...] = jnp.sum(m * w_ref[...].astype(F32), axis=0, keepdims=True)
            else:
                o_ref[...] = m.astype(out_dtype)

    in_specs = [pl.BlockSpec((tk, ka), lambda j, i: (i, a_block)), pl.BlockSpec((tk, cb), lambda j, i: (i, j))]
    out_specs = [pl.BlockSpec((ka, cb), lambda j, i: (0, j))]
    out_shape = [jax.ShapeDtypeStruct((ka, nb), out_dtype)]
    if gated:
        in_specs += [pl.BlockSpec((ka, cb), lambda j, i: (0, j)), pl.BlockSpec((1, cb), lambda j, i: (0, j))]
        out_specs += [pl.BlockSpec((1, cb), lambda j, i: (0, j))]
        out_shape += [jax.ShapeDtypeStruct((1, nb), F32)]
    outs, rode = _tiled_call(body, (a, b, *(gated or ())), name=name, grid=(nb // cb, n_steps), in_specs=in_specs,
                             out_specs=out_specs, out_shape=out_shape, scratch_shapes=[pltpu.VMEM((ka, cb), F32)],
                             ride=ride)
    return (outs if gated else outs[0]), rode


def _modulation_columns(c_all, w_ada, b_cols):
    cols = w_ada.shape[2]

    def body(c_ref, w_ref, b_ref, ca_ref, mod_ref):
        ca, _ = _silu(c_ref[...])
        ca_ref[...] = ca
        for l in range(N_LAYERS):
            mod_ref[l] = jnp.dot(ca, w_ref[l], precision=lax.Precision.HIGHEST, preferred_element_type=F32) + b_ref[l:l + 1, :]

    return pl.pallas_call(
        body, name="modulation_columns",
        out_shape=[jax.ShapeDtypeStruct((N_DEV, D_MODEL), F32), jax.ShapeDtypeStruct((N_LAYERS, N_DEV, cols), F32)],
        compiler_params=pltpu.CompilerParams(vmem_limit_bytes=VMEM_LIMIT),
    )(c_all, w_ada, b_cols)


def _adam(w, g, m, v):
    m2 = ADAM_B1 * m + (1.0 - ADAM_B1) * g
    v2 = ADAM_B2 * v + (1.0 - ADAM_B2) * (g * g)
    m_hat = m2 / (1.0 - ADAM_B1 ** ADAM_STEP)
    v_hat = v2 / (1.0 - ADAM_B2 ** ADAM_STEP)
    return -ADAM_LR * (m_hat / (jnp.sqrt(v_hat) + ADAM_EPS) + ADAM_WD * w), m2, v2


def _row_block(rows, cols, slots):
    target = max(8, (1 << 19) // (cols * max(slots, 1)))
    rb = rows
    while rb > target and rb % 2 == 0 and (rb // 2) % 8 == 0:
        rb //= 2
    return rb


def _adam_update(w, g, m, v, name):
    rows, cols = w.shape
    slotted = g.ndim == 3
    rb = _row_block(rows, cols, N_DEV if slotted else 1)

    def body(w_ref, g_ref, m_ref, v_ref, go_ref, d_ref, mo_ref, vo_ref):
        if slotted:
            gv = g_ref[0].astype(F32)
            for q in range(1, N_DEV):
                gv = gv + g_ref[q].astype(F32)
        else:
            gv = g_ref[...]
        go_ref[...] = gv
        d_ref[...], mo_ref[...], vo_ref[...] = _adam(w_ref[...], gv, m_ref[...], v_ref[...])

    blk = pl.BlockSpec((rb, cols), lambda i: (i, 0))
    g_blk = pl.BlockSpec((N_DEV, rb, cols), lambda i: (0, i, 0)) if slotted else blk
    return pl.pallas_call(
        body, name=name, grid=(rows // rb,),
        in_specs=[blk, g_blk, blk, blk], out_specs=[blk] * 4,
        out_shape=[jax.ShapeDtypeStruct((rows, cols), F32)] * 4,
        compiler_params=_params(("parallel",)),
    )(w, g, m, v)


def _adam_many(ws, gs, ms, vs, name):
    n = len(ws)

    def body(*refs):
        w_refs, g_refs, m_refs, v_refs, d_refs, mo_refs, vo_refs = (refs[k * n:(k + 1) * n] for k in range(7))
        for j in range(n):
            d_refs[j][...], mo_refs[j][...], vo_refs[j][...] = _adam(w_refs[j][...], g_refs[j][...], m_refs[j][...],
                                                                  v_refs[j][...])

    res = pl.pallas_call(
        body, name=name, out_shape=[jax.ShapeDtypeStruct(w.shape, F32) for w in ws] * 3,
        compiler_params=pltpu.CompilerParams(vmem_limit_bytes=VMEM_LIMIT),
    )(*ws, *gs, *ms, *vs)
    return res[:n], res[n:2 * n], res[2 * n:]


def _as_rows(a):
    return a.reshape(-1, a.shape[-1]) if a.ndim > 1 else a.reshape(1, -1)


def _ada_update(ca_t, dmod_cols, w, m, v):
    _, rows, cols = w.shape

    def body(ca_ref, dm_ref, w_ref, m_ref, v_ref, g_ref, d_ref, mo_ref, vo_ref):
        g = ca_ref[:, 0:1] * dm_ref[0, 0:1, :]
        for b in range(1, N_DEV):
            g = g + ca_ref[:, b:b + 1] * dm_ref[0, b:b + 1, :]
        g_ref[0] = g
        d_ref[0], mo_ref[0], vo_ref[0] = _adam(w_ref[0], g, m_ref[0], v_ref[0])

    blk = pl.BlockSpec((1, rows, cols), lambda l: (l, 0, 0))
    return pl.pallas_call(
        body, name="ada_update", grid=(N_LAYERS,),
        in_specs=[_full((rows, N_DEV)), pl.BlockSpec((1, N_DEV, cols), lambda l: (l, 0, 0)), blk, blk, blk],
        out_specs=[blk] * 4, out_shape=[jax.ShapeDtypeStruct(w.shape, F32)] * 4,
        compiler_params=_params(("parallel",)),
    )(ca_t, dmod_cols, w, m, v)


def _exchange_sems(n):
    return [pltpu.SemaphoreType.DMA((n, N_DEV - 1)), pltpu.SemaphoreType.DMA((n, N_DEV - 1)),
            pltpu.SemaphoreType.DMA((n,))]


def _exchange_copies(plans, srcs, outs, sems, receiving, only=None):
    send_sems, recv_sems, local_sems = sems
    x, y, c = lax.axis_index("x"), lax.axis_index("y"), lax.axis_index("c")
    me = 4 * x + 2 * y + c

    def remote(i, k, incoming):
        _, o, send, land = plans[i]
        px = 1 - x if k & 4 else x
        py = 1 - y if k & 2 else y
        pc = 1 - c if k & 1 else c
        p = 4 * px + 2 * py + pc
        return pltpu.make_async_remote_copy(
            src_ref=send(srcs[i], p), dst_ref=land(outs[o], p if incoming else me),
            send_sem=send_sems.at[i, k - 1], recv_sem=recv_sems.at[i, k - 1],
            device_id=(px, py, pc), device_id_type=pl.DeviceIdType.MESH)

    which = range(len(plans)) if only is None else only
    pairs = [(i, k) for k in range(1, N_DEV) for i in which]
    local = [pltpu.make_async_copy(plans[i][2](srcs[i], me), plans[i][3](outs[plans[i][1]], me), local_sems.at[i])
             for i in which]
    return local, [remote(i, k, False) for i, k in pairs], [remote(i, k, True) for i, k in pairs] if receiving else []


def _exchange_start(plans, srcs, outs, sems, only=None):
    local, outgoing, _ = _exchange_copies(plans, srcs, outs, sems, False, only)
    for cp in local + outgoing:
        cp.start()


def _exchange_wait(plans, srcs, outs, sems, only=None):
    local, outgoing, incoming = _exchange_copies(plans, srcs, outs, sems, True, only)
    for cp in incoming:
        cp.wait_recv()
    for cp in outgoing:
        cp.wait_send()
    for cp in local:
        cp.wait()


def _exchange(name, ride):
    out_shapes, plans = ride
    n = len(plans)
    hbm = pl.BlockSpec(memory_space=pltpu.HBM)

    def body(*refs):
        srcs, outs, sems = refs[:n], refs[n:n + len(out_shapes)], refs[n + len(out_shapes):]
        _exchange_start(plans, srcs, outs, sems)
        _exchange_wait(plans, srcs, outs, sems)

    return pl.pallas_call(
        body, name=name, in_specs=[hbm] * n, out_specs=[hbm] * len(out_shapes), out_shape=list(out_shapes),
        scratch_shapes=_exchange_sems(n),
    )(*[p[0] for p in plans])


def _first_gather(c, arrays, rules):
    n = len(arrays)
    c_shapes, c_plans = _plans([c], [_gather])
    shapes, lands = zip(*[(shape, land) for shape, _, land in (rule(a) for a, rule in zip(arrays, rules))])
    hbm = pl.BlockSpec(memory_space=pltpu.HBM)

    def body(*refs):
        c_ref, srcs, c_all_ref, outs = refs[0], refs[1:1 + n], refs[1 + n], refs[2 + n:2 + 2 * n]
        send_sems, recv_sems, local_sems = refs[2 + 2 * n:5 + 2 * n]
        c_sems = refs[5 + 2 * n:]
        x, y, core = lax.axis_index("x"), lax.axis_index("y"), lax.axis_index("c")
        me, sibling = (x, y, core), (x, y, 1 - core)
        chips = [(1 - x, y), (x, 1 - y), (1 - x, 1 - y)]

        def block(a, px, py, pc):
            return lands[a](outs[a], 4 * px + 2 * py + pc)

        def copy(a, k, origin, to, own=False):
            return pltpu.make_async_remote_copy(
                src_ref=srcs[a] if own else block(a, *origin), dst_ref=block(a, *origin),
                send_sem=send_sems.at[a, k], recv_sem=recv_sems.at[a, k], device_id=to,
                device_id_type=pl.DeviceIdType.MESH)

        _exchange_start(c_plans, [c_ref], [c_all_ref], c_sems)
        mine = [pltpu.make_async_copy(srcs[a], block(a, *me), local_sems.at[a]) for a in range(n)]
        first = [copy(a, 0, me, sibling, own=True) for a in range(n)]
        first += [copy(a, 1 + j, me, (*chip, core), own=True) for j, chip in enumerate(chips) for a in range(n)]
        for cp in mine + first:
            cp.start()
        passed = [[copy(a, 4 + j, (*chip, core), sibling) for a in range(n)] for j, chip in enumerate(chips)]
        for j, chip in enumerate(chips):
            for a in range(n):
                copy(a, 1 + j, (*chip, core), me).wait_recv()
                passed[j][a].start()
        for a in range(n):
            copy(a, 0, sibling, me).wait_recv()
        for j, chip in enumerate(chips):
            for a in range(n):
                copy(a, 4 + j, (*chip, 1 - core), me).wait_recv()
        for cp in first + [cp for row in passed for cp in row]:
            cp.wait_send()
        for cp in mine:
            cp.wait()
        _exchange_wait(c_plans, [c_ref], [c_all_ref], c_sems)

    return pl.pallas_call(
        body, name="first_gather", in_specs=[hbm] * (1 + n), out_specs=[hbm] * (1 + n),
        out_shape=[c_shapes[0], *shapes],
        scratch_shapes=[pltpu.SemaphoreType.DMA((n, N_DEV - 1)), pltpu.SemaphoreType.DMA((n, N_DEV - 1)),
                        pltpu.SemaphoreType.DMA((n,)), *_exchange_sems(1)],
    )(c, *arrays)


def _finish_exchange(big, big_rules, packed, dmod):
    n_rows = packed.shape[0]
    r = n_rows // N_DEV
    shapes, plans = _plans([*big, packed, dmod], [*big_rules, _scatter_rows, _gather])
    n_first = len(plans)
    i_small = n_first - 2
    _, send, land = _gather_rows(jax.ShapeDtypeStruct((r, 128), F32))
    plans = plans + [(None, len(shapes), send, land)]
    shapes = shapes + [jax.ShapeDtypeStruct((n_rows, 128), F32)]
    first = [i for i in range(n_first) if i != i_small]
    hbm = pl.BlockSpec(memory_space=pltpu.HBM)

    def body(*refs):
        srcs, outs = list(refs[:n_first]), refs[n_first:n_first + len(shapes)]
        parts_ref, sum_ref, local_sem = refs[n_first + len(shapes):n_first + len(shapes) + 3]
        sems = refs[n_first + len(shapes) + 3:]
        srcs.append(sum_ref)
        _exchange_start(plans, srcs, outs, sems, only=range(n_first))
        _exchange_wait(plans, srcs, outs, sems, only=[i_small])
        cp = pltpu.make_async_copy(outs[i_small], parts_ref, local_sem)
        cp.start()
        cp.wait()
        g = parts_ref[0]
        for q in range(1, N_DEV):
            g = g + parts_ref[q]
        sum_ref[...] = g
        _exchange_start(plans, srcs, outs, sems, only=[n_first])
        _exchange_wait(plans, srcs, outs, sems, only=[n_first])
        _exchange_wait(plans, srcs, outs, sems, only=first)

    res = pl.pallas_call(
        body, name="finish_exchange", in_specs=[hbm] * n_first, out_specs=[hbm] * len(shapes), out_shape=shapes,
        scratch_shapes=[pltpu.VMEM((N_DEV, r, 128), F32), pltpu.VMEM((r, 128), F32), pltpu.SemaphoreType.DMA(()),
                        *_exchange_sems(len(plans))],
    )(*big, packed, dmod)
    return (*res[:len(big)], res[-1], res[n_first - 1])


def _tiled_call(body, args, *, name, grid, in_specs, out_specs, out_shape, scratch_shapes=(), ride=None, aliases=None):
    params = _params(("arbitrary",) * len(grid))
    if ride is None:
        return pl.pallas_call(body, name=name, grid=grid, in_specs=in_specs, out_specs=out_specs, out_shape=out_shape,
                              scratch_shapes=list(scratch_shapes), input_output_aliases=aliases or {},
                              compiler_params=params)(*args), []
    shapes, plans = ride
    n_in, n_src, n_out, n_dst, n_scr = len(in_specs), len(plans), len(out_specs), len(shapes), len(scratch_shapes)
    hbm = pl.BlockSpec(memory_space=pltpu.HBM)

    def carrying(*refs):
        ins, srcs, refs = refs[:n_in], refs[n_in:n_in + n_src], refs[n_in + n_src:]
        outs, dsts, refs = refs[:n_out], refs[n_out:n_out + n_dst], refs[n_out + n_dst:]
        scratch, sems = refs[:n_scr], refs[n_scr:]
        ids = [pl.program_id(a) for a in range(len(grid))]
        first = functools.reduce(jnp.logical_and, [i == 0 for i in ids])
        last = functools.reduce(jnp.logical_and, [i == g - 1 for i, g in zip(ids, grid)])

        @pl.when(first)
        def _():
            _exchange_start(plans, srcs, dsts, sems)

        body(*ins, *outs, *scratch)

        @pl.when(last)
        def _():
            _exchange_wait(plans, srcs, dsts, sems)

    res = pl.pallas_call(
        carrying, name=name, grid=grid, in_specs=[*in_specs, *[hbm] * n_src], out_specs=[*out_specs, *[hbm] * n_dst],
        out_shape=[*out_shape, *shapes], scratch_shapes=[*scratch_shapes, *_exchange_sems(n_src)],
        input_output_aliases=aliases or {}, compiler_params=params)(*args, *[p[0] for p in plans])
    return res[:n_out], res[n_out:]


def _tail(nd, idx):
    return (slice(None),) * (nd - 2) + idx


def _gather(a):
    return jax.ShapeDtypeStruct((N_DEV,) + a.shape, a.dtype), lambda s, p: s, lambda o, q: o.at[q]


def _gather_rows(a):
    r = a.shape[-2]
    return (jax.ShapeDtypeStruct(a.shape[:-2] + (N_DEV * r, a.shape[-1]), a.dtype), lambda s, p: s,
            lambda o, q: o.at[_tail(a.ndim, (pl.ds(pl.multiple_of(q * r, r), r), slice(None)))])


def _gather_cols(a):
    c = a.shape[-1]
    return (jax.ShapeDtypeStruct(a.shape[:-1] + (N_DEV * c,), a.dtype), lambda s, p: s,
            lambda o, q: o.at[_tail(a.ndim, (slice(None), pl.ds(pl.multiple_of(q * c, c), c)))])


def _scatter_rows(a):
    r = a.shape[0] // N_DEV
    return (jax.ShapeDtypeStruct((N_DEV, r, a.shape[1]), a.dtype),
            lambda s, p: s.at[pl.ds(pl.multiple_of(p * r, r), r), :], lambda o, q: o.at[q])


def _scatter_cols(a):
    c = a.shape[1] // N_DEV
    return (jax.ShapeDtypeStruct((N_DEV, a.shape[0], c), a.dtype),
            lambda s, p: s.at[:, pl.ds(pl.multiple_of(p * c, c), c)], lambda o, q: o.at[q])


def _plans(arrays, rules):
    shapes, plans = [], []
    for o, (a, rule) in enumerate(zip(arrays, rules)):
        shape, send, land = rule(a)
        shapes.append(shape)
        plans.append((a, o, send, land))
    return shapes, plans


def _pack(pieces, rows_multiple=8):
    flat = []
    for a in pieces:
        f = a.reshape(-1)
        flat.append(jnp.pad(f, (0, (-f.shape[0]) % 128)))
    total = sum(f.shape[0] for f in flat)
    flat.append(jnp.zeros(((-total) % (128 * rows_multiple),), F32))
    return jnp.concatenate(flat).reshape(-1, 128)


def _unpack(buf, shapes, lead=()):
    flat = buf.reshape(lead + (-1,))
    out, off = [], 0
    for s in shapes:
        n = math.prod(s)
        out.append(flat[..., off:off + n].reshape(lead + tuple(s)))
        off += n + (-n) % 128
    return out


def _pad_rows(a, rows):
    return jnp.pad(a, ((0, rows - a.shape[0]), (0, 0)))


VEC_NAMES = ('pool_scale', 'b_dw_c', 'ln_g_c', 'ln_b_c', 'b_pw2_c', 'ln_g_d', 'ln_b_d')
GATHERED = ('w_in', 'w_out', 'w_pw2_c', 'w_conv_a', 'w_dw_c')
GATHER_RULES = (_gather_cols, _gather_rows, _gather_rows, _gather, _gather)
SCATTER_RULES = (_scatter_cols, _scatter_rows, _scatter_rows)


def _weight_shards(shard, l):
    return [shard[n][l].astype(BF16) if n in ('w_in', 'w_out') else shard[n][l] for n in GATHERED]


def _layer_weights(shard, l, gathered):
    w_in_b, w_out_b, w_pw2, wconv_parts, wdw_parts = gathered
    wconv = wconv_parts.transpose(1, 0, 2).reshape(CONV_A, GROUP)
    wdw = wdw_parts.transpose(1, 0, 2).reshape(CONV_C, GROUP)
    wp = jnp.einsum('gcd,gh->gchd', shard['w_pool'][l], jnp.eye(4, dtype=F32)).reshape(GROUP, GROUP)
    ws = shard['w_s_d'][l] * jnp.tril(jnp.ones((SUB, SUB), F32))
    vec = jnp.stack([shard[n][l] for n in VEC_NAMES])
    width = jnp.repeat(jnp.asarray([2.0, 4.0, 8.0, 16.0], F32), 64)[None]
    count = jnp.minimum(jnp.arange(1, SUB + 1, dtype=F32)[:, None], width)
    gating_and_counts = jnp.concatenate([jnp.repeat(shard['b_s_d'][l].T, 64, axis=1), 1.0 / count, 1.0 / width,
                                         jnp.zeros((7, GROUP), F32)])
    small = (_pad_rows(wconv, 8), _pad_rows(wdw, HALO), _pad_rows(vec, 16), wp.astype(BF16), w_pw2.astype(BF16),
             ws.reshape(4 * SUB, SUB).astype(BF16), gating_and_counts)
    small_t = (wp.T.astype(BF16), w_pw2.T.astype(BF16), ws.transpose(0, 2, 1).reshape(4 * SUB, SUB).astype(BF16))
    return w_in_b, w_out_b, small, small_t


def kernel(x, c, norm_g, w_ada, b_ada, w_in, w_conv_a, w_pool, pool_scale, w_dw_c, b_dw_c, ln_g_c, ln_b_c, w_pw2_c, b_pw2_c, ln_g_d, ln_b_d, w_s_d, b_s_d, w_out, final_g, loss_target, m_norm_g, m_w_ada, m_b_ada, m_w_in, m_w_conv_a, m_w_pool, m_pool_scale, m_w_dw_c, m_b_dw_c, m_ln_g_c, m_ln_b_c, m_w_pw2_c, m_b_pw2_c, m_ln_g_d, m_ln_b_d, m_w_s_d, m_b_s_d, m_w_out, m_final_g, v_norm_g, v_w_ada, v_b_ada, v_w_in, v_w_conv_a, v_w_pool, v_pool_scale, v_w_dw_c, v_b_dw_c, v_ln_g_c, v_ln_b_c, v_w_pw2_c, v_b_pw2_c, v_ln_g_d, v_ln_b_d, v_w_s_d, v_b_s_d, v_w_out, v_final_g):
    given = dict(locals())
    shard = {n: given[n] for n in WEIGHTS}
    mom_m = {n: given['m_' + n] for n in WEIGHTS}
    mom_v = {n: given['v_' + n] for n in WEIGHTS}
    me = 4 * lax.axis_index("x") + 2 * lax.axis_index("y") + lax.axis_index("c")
    n_tok = x.shape[1]
    tile = min(TOKEN_TILE, n_tok)
    wide_tile = min(2 * TOKEN_TILE, n_tok)
    x0 = x.reshape(n_tok, D_MODEL)
    target = loss_target.reshape(n_tok, D_MODEL)
    ada_cols = w_ada.shape[2]

    first_shards = _weight_shards(shard, 0)
    c_all, w_in_first = _first_gather(c, first_shards[:1], GATHER_RULES[:1])

    b_cols = lax.dynamic_slice_in_dim(b_ada, me * ada_cols, ada_cols, axis=1)
    c_act, mod_cols = _modulation_columns(c_all.reshape(N_DEV, D_MODEL), w_ada, b_cols)
    (mod_all,) = _exchange("gather_modulation", _plans([mod_cols], [_gather]))
    mod = lax.dynamic_index_in_dim(mod_all, me, axis=2, keepdims=False)
    mod = mod.transpose(1, 0, 2).reshape(N_LAYERS, 3 * D_MODEL)
    shift, scale, gate = (mod[:, k * D_MODEL:(k + 1) * D_MODEL].reshape(N_LAYERS, 1, D_MODEL) for k in range(3))
    gs = norm_g.reshape(N_LAYERS, 1, D_MODEL) * (1.0 + scale)

    xs, hs, zs, ocs, layers = [x0], [], [], [], []
    for l in range(N_LAYERS):
        if l == 0:
            (h, z), rest = _in_proj(xs[0], gs[0], shift[0], w_in_first, wide_tile,
                                    ride=_plans(first_shards[1:], GATHER_RULES[1:]))
            layers.append(_layer_weights(shard, 0, [w_in_first, *rest]))
        else:
            (h, z), _ = _in_proj(xs[l], gs[l], shift[l], layers[l][0], wide_tile)
        _, w_out_b, small, _ = layers[l]
        hs.append(h)
        zs.append(z)
        if l + 1 < N_LAYERS:
            (x_next, o_c), gathered = _mix_out(z, xs[l], gate[l], small, w_out_b, tile,
                                               ride=_plans(_weight_shards(shard, l + 1), GATHER_RULES))
            xs.append(x_next)
            layers.append(_layer_weights(shard, l + 1, gathered))
        else:
            (dx, o_c, loss_part, dfinal_g), _ = _mix_out(z, xs[l], gate[l], small, w_out_b, tile,
                                                         head=(final_g.reshape(1, D_MODEL), target))
        ocs.append(o_c)

    part = {}
    layer_parts = [None] * N_LAYERS
    slots = [None] * N_LAYERS
    for l in reversed(range(N_LAYERS)):
        w_in_b, w_out_b, small, small_t = layers[l]
        ride = _plans(layer_parts[l + 1]['big'], SCATTER_RULES) if l + 1 < N_LAYERS else None
        (dz, ycat, sums, dwp, dw2, dws, dbs), rode = _mix_bwd(zs[l], ocs[l], dx, gate[l], small, small_t, w_out_b, tile,
                                                              ride=ride)
        if ride:
            slots[l + 1] = rode
        if l > 0:
            (dw_out, dgate), _ = _tokens_matmul(ycat, dx, "out_proj_tokens_matmul", out_dtype=BF16,
                                                gated=(w_out_b, gate[l]))
            dw_in, _ = _tokens_matmul(hs[l], dz, "in_proj_tokens_matmul", out_dtype=BF16)
        else:
            dw_in, _ = _tokens_matmul(hs[l], dz, "in_proj_tokens_matmul", out_dtype=BF16, a_cols=(0, D_MODEL // 2))
            dw_in_last, (slots_in,) = _tokens_matmul(
                hs[l], dz, "in_proj_tokens_matmul", out_dtype=BF16, a_cols=(1, D_MODEL // 2),
                ride=_plans([dw_in], SCATTER_RULES[:1]))
            (dw_out, dgate), (slots_in_last,) = _tokens_matmul(
                ycat, dx, "out_proj_tokens_matmul", out_dtype=BF16, gated=(w_out_b, gate[l]),
                ride=_plans([dw_in_last], SCATTER_RULES[:1]))
        (dx, dshift, dgs), _ = _norm_bwd(xs[l], dz, dx, gs[l], w_in_b, tile)
        layer_parts[l] = dict(
            big=[dw_in, dw_out, dw2],
            b_ada=jnp.concatenate([dshift, dgs * norm_g[l][None], dgate], axis=1)[0],
            norm_g=(dgs * (1.0 + scale[l]))[0], sums=sums,
            w_pool=jnp.einsum('gchd,gh->gcd', dwp.reshape(4, 64, 4, 64), jnp.eye(4, dtype=F32)),
            w_s_d=dws.reshape(4, SUB, SUB) * jnp.tril(jnp.ones((SUB, SUB), F32)),
            b_s_d=dbs.reshape(SUB, 4, 64).sum(axis=-1).T)
    grad_x = dx.reshape(x.shape)
    small_names = REPLICATED + CHANNEL_SHARDED
    packed_names = [n for n in small_names if n not in SUM_ROWS] + ['sums']
    for n in packed_names:
        part[n] = dfinal_g[0] if n == 'final_g' else jnp.stack([layer_parts[l][n] for l in range(N_LAYERS)])

    small_shapes = [part[n].shape for n in packed_names] + [(1, 128)]
    slots_out, slots_pw2, small_sum, dmod_all = _finish_exchange(
        layer_parts[0]['big'][1:], SCATTER_RULES[1:],
        _pack([part[n] for n in packed_names] + [loss_part[0:1]], rows_multiple=8 * N_DEV), part['b_ada'])
    slots[0] = [slots_in, slots_out, slots_pw2]

    grads, deltas, new_m, new_v = {}, {}, {}, {}
    half = D_MODEL // 2
    for j, n in enumerate(('w_in', 'w_out', 'w_pw2_c')):
        outs = [_adam_update(shard[n][l], slots[l][j], mom_m[n][l], mom_v[n][l], "update_" + n)
                for l in range(1, N_LAYERS)]
        if n == 'w_in':
            halves = [_adam_update(shard[n][0][rows], s, mom_m[n][0][rows], mom_v[n][0][rows], "update_" + n)
                      for rows, s in ((slice(0, half), slots[0][0]), (slice(half, None), slots_in_last))]
            outs.insert(0, [jnp.concatenate(o) for o in zip(*halves)])
        else:
            outs.insert(0, _adam_update(shard[n][0], slots[0][j], mom_m[n][0], mom_v[n][0], "update_" + n))
        grads[n], deltas[n], new_m[n], new_v[n] = (jnp.stack(o) for o in zip(*outs))

    *small_sums, loss_sum = _unpack(small_sum, small_shapes)
    loss = loss_sum[0, 0]
    gsum = dict(zip(packed_names, small_sums))
    for n, rows in SUM_ROWS.items():
        gsum[n] = gsum['sums'][:, rows]
    for n in CHANNEL_SHARDED:
        width = shard[n].shape[2]
        gsum[n] = lax.dynamic_slice_in_dim(gsum[n], me * width, width, axis=2)
    d_small, m_small, v_small = _adam_many(*[[_as_rows(d[n]) for n in small_names] for d in (shard, gsum, mom_m, mom_v)],
                                           "update_small")
    for j, n in enumerate(small_names):
        grads[n] = gsum[n]
        deltas[n], new_m[n], new_v[n] = (o[j].reshape(shard[n].shape) for o in (d_small, m_small, v_small))

    dmod_cols = lax.dynamic_slice_in_dim(dmod_all, me * ada_cols, ada_cols, axis=2).transpose(1, 0, 2)
    grads['w_ada'], deltas['w_ada'], new_m['w_ada'], new_v['w_ada'] = _ada_update(
        c_act.T, dmod_cols, w_ada, m_w_ada, v_w_ada)

    return (loss, grad_x, *[grads[n] for n in WEIGHTS], *[deltas[n] for n in WEIGHTS],
            *[new_m[n] for n in WEIGHTS], *[new_v[n] for n in WEIGHTS])
```

```python
import functools
import math

import jax
import jax.numpy as jnp
from jax import lax
from jax.experimental import pallas as pl
from jax.experimental.pallas import tpu as pltpu

F32 = jnp.float32
BF16 = jnp.bfloat16

N_DEV = 8
D_MODEL = 1024
GROUP = 256
D_IN = 12 * GROUP
N_LAYERS = 2
HALO = 32
SUB = 128
WIN = SUB + HALO
TOKEN_TILE = 512
REDUCE_TILE = 2048
EPS = 1e-6
VMEM_BYTES_V7X = 64 * 1024 * 1024
VMEM_LIMIT = VMEM_BYTES_V7X - 8 * 1024 * 1024

ADAM_LR = 0.001
ADAM_B1 = 0.9
ADAM_B2 = 0.999
ADAM_EPS = 1e-08
ADAM_WD = 0.01
ADAM_STEP = 10

A_B, A_C, A_X, A_G, B_P, B_G, C_A, C_GL, C_G, D_U, D_V, D_G = range(12)
V_PSCALE, V_BDW, V_LNGC, V_LNBC, V_BPW2, V_LNGD, V_LNBD = range(7)
S_WCONV, S_PSCALE, S_BDW, S_LNGC, S_LNBC, S_BPW2, S_LNGD, S_LNBD, S_WDW = 0, 3, 4, 5, 6, 7, 8, 9, 16
N_SUMS = 64
CONV_A = 3
CONV_C = 31
SUM_ROWS = dict(w_conv_a=slice(S_WCONV, S_WCONV + CONV_A), w_dw_c=slice(S_WDW, S_WDW + CONV_C), pool_scale=S_PSCALE,
                b_dw_c=S_BDW, ln_g_c=S_LNGC, ln_b_c=S_LNBC, b_pw2_c=S_BPW2, ln_g_d=S_LNGD, ln_b_d=S_LNBD)

WEIGHTS = ('norm_g', 'w_ada', 'b_ada', 'w_in', 'w_conv_a', 'w_pool', 'pool_scale', 'w_dw_c', 'b_dw_c', 'ln_g_c',
           'ln_b_c', 'w_pw2_c', 'b_pw2_c', 'ln_g_d', 'ln_b_d', 'w_s_d', 'b_s_d', 'w_out', 'final_g')
REPLICATED = ('norm_g', 'b_ada', 'w_pool', 'pool_scale', 'b_dw_c', 'ln_g_c', 'ln_b_c', 'b_pw2_c', 'ln_g_d', 'ln_b_d',
              'w_s_d', 'b_s_d', 'final_g')
CHANNEL_SHARDED = ('w_conv_a', 'w_dw_c')


def _params(semantics, vmem=VMEM_LIMIT):
    return pltpu.CompilerParams(dimension_semantics=semantics, vmem_limit_bytes=vmem)


def _cols(g):
    return slice(g * GROUP, (g + 1) * GROUP)


def _full(shape):
    return pl.BlockSpec(shape, lambda *_: (0,) * len(shape))


def _silu(x):
    s = jax.nn.sigmoid(x)
    return x * s, s


def _dsilu(sg, s):
    return s + sg * (1.0 - s)


_GELU_C0 = math.sqrt(2.0 / math.pi)
_GELU_C1 = 0.044715


def _gelu(x):
    x2 = x * x
    th = jnp.tanh(_GELU_C0 * (x + _GELU_C1 * (x * x2)))
    p = 0.5 + 0.5 * th
    return x * p, (th, p, x2)


def _dgelu(x, aux):
    th, p, x2 = aux
    return p + (0.5 * x) * (1.0 - th * th) * (_GELU_C0 + (3.0 * _GELU_C0 * _GELU_C1) * x2)


def _layer_norm(x):
    mu = jnp.mean(x, axis=-1, keepdims=True)
    xc = x - mu
    rstd = lax.rsqrt(jnp.mean(xc * xc, axis=-1, keepdims=True) + EPS)
    return xc * rstd, rstd


def _layer_norm_bwd(dn, n, rstd):
    return rstd * (dn - jnp.mean(dn, axis=-1, keepdims=True) - n * jnp.mean(dn * n, axis=-1, keepdims=True))


def _shift_rows(a, k):
    k = k % a.shape[0]
    return a if k == 0 else pltpu.roll(a, k, 0)


def _row_sum8(a):
    s = a[0:8]
    for m in range(1, a.shape[0] // 8):
        s = s + a[8 * m:8 * m + 8]
    return s


def _lane():
    return lax.broadcasted_iota(jnp.int32, (SUB, GROUP), 1)


def _by_quarter(lane, parts):
    return jnp.where(lane < 64, parts[0], jnp.where(lane < 128, parts[1], jnp.where(lane < 192, parts[2], parts[3])))


def _conv_inputs(z_ref, rows):
    def f(g):
        return z_ref[rows, _cols(g)].astype(F32)
    return f(A_C) * f(A_X), f(B_P), f(C_A) * jax.nn.sigmoid(f(C_GL))


def _fill_past(past_ref, zh_ref, zm_ref, is_first, tile):
    parts = _conv_inputs(zh_ref, slice(None))
    for n, a in enumerate(parts):
        past_ref[0:HALO, _cols(n)] = jnp.where(is_first, 0.0, a)

    def body(j, carry):
        r0 = pl.multiple_of(j * SUB, SUB)
        for n, a in enumerate(_conv_inputs(zm_ref, pl.ds(r0, SUB))):
            past_ref[pl.ds(r0 + HALO, SUB), _cols(n)] = a
        return carry

    lax.fori_loop(0, tile // SUB, body, 0)


def _short_conv_taps(qw):
    return [_shift_rows(qw, CONV_A - 1 - k)[HALO:WIN] for k in range(CONV_A)]


def _doubling_sums(w, back, keep):
    n = w.shape[0]
    half = GROUP // 2
    lane = lax.broadcasted_iota(jnp.int32, (SUB, half), 1)

    def grow(s, k):
        return s + _shift_rows(s, k if back else n - k)

    lo2 = grow(w[:, :half], 1)
    lo4 = grow(lo2, 2)
    hi8 = grow(grow(grow(w[:, half:], 1), 2), 4)
    hi16 = grow(hi8, 8)
    return jnp.concatenate([jnp.where(lane < 64, lo2[keep], lo4[keep]), jnp.where(lane < 64, hi8[keep], hi16[keep])],
                           axis=1)


def _window_sums(pw):
    return _doubling_sums(pw, True, slice(HALO, WIN))


def _forward_window_sums(ew):
    return _doubling_sums(ew, False, slice(0, SUB))


def _inv_count(bs_ref, t_first):
    return jnp.where(t_first == 0, bs_ref[SUB:2 * SUB, :], bs_ref[2 * SUB:2 * SUB + 1, :])


def _mixer_forwards(zc, win, t_first, wc_ref, wdw_ref, vec_ref, wp_ref, w2_ref, ws_ref, bs_ref, o_c=None):
    def vec(n):
        return vec_ref[n:n + 1, :]

    def short_conv():
        taps = _short_conv_taps(win(0))
        o_a = wc_ref[0:1, :] * taps[0] + wc_ref[1:2, :] * taps[1] + wc_ref[2:3, :] * taps[2]
        a_b, a_g = zc(A_B), zc(A_G)
        sg_a, s_a = _silu(a_g)
        return a_b * o_a * sg_a, dict(taps=taps, o_a=o_a, a_b=a_b, a_g=a_g, sg_a=sg_a, s_a=s_a)

    def pooling():
        pw = win(1)
        ic = _inv_count(bs_ref, t_first)
        pooled_b = (_window_sums(pw) * ic - pw[HALO:WIN]).astype(BF16)
        y0_b = jnp.dot(pooled_b, wp_ref[...], preferred_element_type=F32)
        b_g = zc(B_G)
        sg_b, s_b = _silu(b_g)
        return y0_b * vec(V_PSCALE) * sg_b, dict(ic=ic, pooled_b=pooled_b, y0_b=y0_b, b_g=b_g, sg_b=sg_b, s_b=s_b)

    def conformer():
        hw = win(2)
        o = o_c
        if o is None:
            o = wdw_ref[CONV_C - 1:CONV_C, :] * hw[HALO:WIN] + vec(V_BDW)
            for k in range(CONV_C - 1):
                o = o + wdw_ref[k:k + 1, :] * _shift_rows(hw, CONV_C - 1 - k)[HALO:WIN]
        n_c, rstd_c = _layer_norm(o)
        ln_c = n_c * vec(V_LNGC) + vec(V_LNBC)
        sl_c, ssl_c = _silu(ln_c)
        sl_b = sl_c.astype(BF16)
        yc = jnp.dot(sl_b, w2_ref[...], preferred_element_type=F32) + vec(V_BPW2)
        c_g = zc(C_G)
        sg_c, s_c = _silu(c_g)
        return yc * sg_c, dict(hw=hw, o_c=o, n_c=n_c, rstd_c=rstd_c, sl_c=sl_c, ssl_c=ssl_c, sl_b=sl_b, yc=yc, c_g=c_g,
                               sg_c=sg_c, s_c=s_c)

    def gating():
        lane = _lane()
        d_u, d_v, d_g = zc(D_U), zc(D_V), zc(D_G)
        u, aux_u = _gelu(d_u)
        gv, aux_v = _gelu(d_v)
        n_d, rstd_d = _layer_norm(gv)
        v_b = (n_d * vec(V_LNGD) + vec(V_LNBD)).astype(BF16)
        r = jnp.dot(ws_ref[...], v_b, preferred_element_type=F32)
        mixed = _by_quarter(lane, [r[h * SUB:(h + 1) * SUB] for h in range(4)]) + bs_ref[0:SUB, :]
        sg_d, s_d = _silu(d_g)
        return u * mixed * sg_d, dict(d_u=d_u, d_v=d_v, u=u, aux_u=aux_u, aux_v=aux_v, n_d=n_d, rstd_d=rstd_d,
                                      v_b=v_b, mixed=mixed, sg_d=sg_d, s_d=s_d)

    return short_conv, pooling, conformer, gating


def _in_proj(x, gs, shift, w_in_b, tile, ride=None):
    n_tok = x.shape[0]

    def body(x_ref, gs_ref, sh_ref, w_ref, h_ref, z_ref):
        xv = x_ref[...]
        r = lax.rsqrt(jnp.mean(xv * xv, axis=-1, keepdims=True) + EPS)
        h = ((xv * r) * gs_ref[...] + sh_ref[...]).astype(BF16)
        h_ref[...] = h
        for j in range(D_IN // D_MODEL):
            cs = slice(j * D_MODEL, (j + 1) * D_MODEL)
            z_ref[:, cs] = jnp.dot(h, w_ref[:, cs], preferred_element_type=F32).astype(BF16)

    return _tiled_call(
        body, (x, gs, shift, w_in_b), name="in_proj", grid=(n_tok // tile,),
        in_specs=[pl.BlockSpec((tile, D_MODEL), lambda i: (i, 0)), _full((1, D_MODEL)), _full((1, D_MODEL)),
                  _full((D_MODEL, D_IN))],
        out_specs=[pl.BlockSpec((tile, D_MODEL), lambda i: (i, 0)), pl.BlockSpec((tile, D_IN), lambda i: (i, 0))],
        out_shape=[jax.ShapeDtypeStruct((n_tok, D_MODEL), BF16), jax.ShapeDtypeStruct((n_tok, D_IN), BF16)],
        ride=ride)


def _small_specs(with_transposes):
    specs = [_full((8, GROUP)), _full((HALO, GROUP)), _full((16, GROUP)), _full((GROUP, GROUP)), _full((GROUP, GROUP)),
             _full((4 * SUB, SUB)), _full((2 * SUB + 8, GROUP))]
    if with_transposes:
        specs += [_full((GROUP, GROUP)), _full((GROUP, GROUP)), _full((4 * SUB, SUB))]
    return specs


def _mix_out(z, x, gate, small, w_out_b, tile, ride=None, head=None):
    n_tok = x.shape[0]
    n_tiles = n_tok // tile
    n_sub = tile // SUB
    cw = D_MODEL // n_sub
    per_halo = tile // HALO
    n_in = 12 + (2 if head else 0)
    n_out = 4 if head else 2

    def cur(i):
        return jnp.minimum(i, n_tiles - 1)

    def prev(i):
        return jnp.maximum(i - 1, 0)

    def body(*refs):
        (zm_ref, zh_ref, x_ref, gate_ref, wc_ref, wdw_ref, vec_ref, wp_ref, w2_ref, ws_ref, bs_ref, wout_ref) = refs[:12]
        xo_ref, oc_ref = refs[n_in:n_in + 2]
        past_ref, ycat_ref, ycat_prev_ref = refs[n_in + n_out:n_in + n_out + 3]
        i = pl.program_id(0)
        t = cur(i)
        if head:
            g_ref, tgt_ref = refs[12:14]
            loss_ref, dg_ref = refs[n_in + 2:n_in + 4]
            xn_ref, acc_ref = refs[n_in + n_out + 3:]
        else:
            xn_ref = xo_ref

        @pl.when(i == 0)
        def _():
            ycat_prev_ref[...] = jnp.zeros_like(ycat_prev_ref)
            if head:
                acc_ref[...] = jnp.zeros_like(acc_ref)

        _fill_past(past_ref, zh_ref, zm_ref, t == 0, tile)
        for j in range(n_sub):
            cs = slice(j * cw, (j + 1) * cw)
            y = jnp.dot(ycat_prev_ref[...], wout_ref[:, cs], preferred_element_type=F32)
            xn_ref[:, cs] = x_ref[:, cs] + gate_ref[:, cs] * y
            rows = slice(j * SUB, (j + 1) * SUB)
            mixers = _mixer_forwards(
                lambda g: zm_ref[rows, _cols(g)].astype(F32), lambda n: past_ref[j * SUB:j * SUB + WIN, _cols(n)],
                t * tile + j * SUB, wc_ref, wdw_ref, vec_ref, wp_ref, w2_ref, ws_ref, bs_ref)
            for n, mixer in enumerate(mixers):
                y, s = mixer()
                ycat_ref[rows, _cols(n)] = y.astype(BF16)
                if "o_c" in s:
                    oc_ref[rows, :] = s["o_c"]
        ycat_prev_ref[...] = ycat_ref[...]
        if head:
            counted = jnp.where(i > 0, 1.0, 0.0)
            xo_ref[...] = _loss_head_block(xn_ref[...], g_ref[...], tgt_ref[...], acc_ref, counted)

            @pl.when(i == n_tiles)
            def _():
                loss_ref[...] = jnp.full((8, 128), 0.5 / D_MODEL, F32) * jnp.sum(acc_ref[0])
                dg_ref[...] = jnp.sum(acc_ref[1], axis=0, keepdims=True)

    in_specs = [pl.BlockSpec((tile, D_IN), lambda i: (cur(i), 0)),
                pl.BlockSpec((HALO, D_IN), lambda i: (jnp.maximum(cur(i) * per_halo - 1, 0), 0)),
                pl.BlockSpec((tile, D_MODEL), lambda i: (prev(i), 0)), _full((1, D_MODEL)),
                *_small_specs(False), _full((D_MODEL, D_MODEL))]
    out_specs = [pl.BlockSpec((tile, D_MODEL), lambda i: (prev(i), 0)), pl.BlockSpec((tile, GROUP), lambda i: (cur(i), 0))]
    out_shape = [jax.ShapeDtypeStruct((n_tok, D_MODEL), F32), jax.ShapeDtypeStruct((n_tok, GROUP), F32)]
    scratch = [pltpu.VMEM((tile + HALO, 3 * GROUP), F32), pltpu.VMEM((tile, D_MODEL), BF16), pltpu.VMEM((tile, D_MODEL), BF16)]
    args = (z, z, x, gate, *small, w_out_b)
    if head:
        in_specs += [_full((1, D_MODEL)), pl.BlockSpec((tile, D_MODEL), lambda i: (prev(i), 0))]
        out_specs += [_full((8, 128)), _full((1, D_MODEL))]
        out_shape += [jax.ShapeDtypeStruct((8, 128), F32), jax.ShapeDtypeStruct((1, D_MODEL), F32)]
        scratch += [pltpu.VMEM((tile, D_MODEL), F32), pltpu.VMEM((2, 8, D_MODEL), F32)]
        args += tuple(head)
    outs, rode = _tiled_call(body, args, name="mix_out", grid=(n_tiles + 1,), in_specs=in_specs, out_specs=out_specs,
                             out_shape=out_shape, scratch_shapes=scratch, ride=ride)
    return outs, rode


def _loss_head_block(xv, g, target, acc_ref, counted):
    r = lax.rsqrt(jnp.mean(xv * xv, axis=-1, keepdims=True) + EPS)
    xn = xv * r
    err = xn * g - target
    acc_ref[0] = acc_ref[0] + counted * _row_sum8(err * err)
    dy = err * (1.0 / D_MODEL)
    acc_ref[1] = acc_ref[1] + counted * _row_sum8(dy * xn)
    a = dy * g
    return r * (a - xn * jnp.mean(a * xn, axis=-1, keepdims=True))


def _mix_bwd(z, o_c, dx_next, gate, small, small_t, w_out_b, tile, ride=None):
    n_tok = z.shape[0]
    n_tiles = n_tok // tile
    n_sub = tile // SUB
    cw = D_MODEL // n_sub
    per_halo = tile // HALO
    nt_dims = (((1,), (1,)), ((), ()))

    def tile_of(i):
        return n_tiles - 1 - i

    def next_tile_of(i):
        return jnp.maximum(n_tiles - 2 - i, 0)

    def body(zm_ref, zh_ref, oc_ref, dxn_ref, dxn_next_ref, gate_ref, wc_ref, wdw_ref, vec_ref, wp_ref, w2_ref, ws_ref,
             bs_ref, wpt_ref, w2t_ref, wst_ref, wout_ref,
             dz_ref, ycat_ref, sums_ref, dwp_ref, dw2_ref, dws_ref, dbs_ref,
             past_ref, future_ref, dy_ref, dy_next_ref, acc_ref):
        i = pl.program_id(0)
        t = tile_of(i)

        @pl.when(i == 0)
        def _():
            acc_ref[...] = jnp.zeros_like(acc_ref)
            dwp_ref[...] = jnp.zeros_like(dwp_ref)
            dw2_ref[...] = jnp.zeros_like(dw2_ref)
            dws_ref[...] = jnp.zeros_like(dws_ref)
            dbs_ref[...] = jnp.zeros_like(dbs_ref)
            future_ref[tile:tile + HALO, :] = jnp.zeros((HALO, 3 * GROUP), F32)
            dy_ref[...] = lax.dot_general((dxn_ref[...] * gate_ref[...]).astype(BF16), wout_ref[...], nt_dims,
                                        preferred_element_type=F32)

        _fill_past(past_ref, zh_ref, zm_ref, t == 0, tile)
        dyb_next = (dxn_next_ref[...] * gate_ref[...]).astype(BF16)

        def vec(n):
            return vec_ref[n:n + 1, :]

        for jj in range(n_sub):
            j = n_sub - 1 - jj
            r0 = j * SUB
            rows = slice(r0, r0 + SUB)

            def zc(g):
                return zm_ref[rows, _cols(g)].astype(F32)

            def add(n, a):
                acc_ref[n] = acc_ref[n] + _row_sum8(a)

            def put(g, a):
                dz_ref[rows, _cols(g)] = a.astype(BF16)

            def future_window(n, a):
                future_ref[rows, _cols(n)] = a
                return future_ref[r0:r0 + WIN, _cols(n)]

            short_conv, pooling, conformer, gating = _mixer_forwards(
                zc, lambda n: past_ref[r0:r0 + WIN, _cols(n)], t * tile + r0,
                wc_ref, wdw_ref, vec_ref, wp_ref, w2_ref, ws_ref, bs_ref, o_c=oc_ref[rows, :])
            lane = _lane()
            ks = slice(jj * cw, (jj + 1) * cw)
            dy_next_ref[:, ks] = lax.dot_general(dyb_next, wout_ref[ks, :], nt_dims, preferred_element_type=F32)

            y, s = short_conv()
            ycat_ref[rows, _cols(0)] = y.astype(BF16)
            dy = dy_ref[rows,_cols(0)]
            put(A_B, dy * s["o_a"] * s["sg_a"])
            put(A_G, dy * s["a_b"] * s["o_a"] * _dsilu(s["sg_a"], s["s_a"]))
            do = dy * s["a_b"] * s["sg_a"]
            for k in range(CONV_A):
                add(S_WCONV + k, do * s["taps"][k])
            dow = future_window(0, do)
            dq = wc_ref[CONV_A - 1:CONV_A, :] * dow[0:SUB]
            for k in range(CONV_A - 1):
                dq = dq + wc_ref[k:k + 1, :] * _shift_rows(dow, WIN - (CONV_A - 1 - k))[0:SUB]
            put(A_C, dq * zc(A_X))
            put(A_X, dq * zc(A_C))

            y, s = pooling()
            ycat_ref[rows, _cols(1)] = y.astype(BF16)
            dy = dy_ref[rows,_cols(1)]
            put(B_G, dy * (s["y0_b"] * vec(V_PSCALE)) * _dsilu(s["sg_b"], s["s_b"]))
            dyb = dy * s["sg_b"]
            add(S_PSCALE, dyb * s["y0_b"])
            dpw_b = (dyb * vec(V_PSCALE)).astype(BF16)
            dwp_ref[...] += lax.dot_general(s["pooled_b"], dpw_b, (((0,), (0,)), ((), ())), preferred_element_type=F32)
            dpooled = jnp.dot(dpw_b, wpt_ref[...], preferred_element_type=F32)
            ew = future_window(1, dpooled * s["ic"])
            put(B_P, _forward_window_sums(ew) - dpooled)

            y, s = conformer()
            ycat_ref[rows, _cols(2)] = y.astype(BF16)
            dy = dy_ref[rows,_cols(2)]
            put(C_G, dy * s["yc"] * _dsilu(s["sg_c"], s["s_c"]))
            dyc = dy * s["sg_c"]
            add(S_BPW2, dyc)
            dyc_b = dyc.astype(BF16)
            dw2_ref[...] += lax.dot_general(s["sl_b"], dyc_b, (((0,), (0,)), ((), ())), preferred_element_type=F32)
            dln = jnp.dot(dyc_b, w2t_ref[...], preferred_element_type=F32) * _dsilu(s["sl_c"], s["ssl_c"])
            add(S_LNGC, dln * s["n_c"])
            add(S_LNBC, dln)
            do = _layer_norm_bwd(dln * vec(V_LNGC), s["n_c"], s["rstd_c"])
            add(S_BDW, do)
            hw = s["hw"]
            for k in range(CONV_C):
                add(S_WDW + k, do * _shift_rows(hw, CONV_C - 1 - k)[HALO:WIN])
            dow = future_window(2, do)
            dhc = wdw_ref[CONV_C - 1:CONV_C, :] * dow[0:SUB]
            for k in range(CONV_C - 1):
                dhc = dhc + wdw_ref[k:k + 1, :] * _shift_rows(dow, WIN - (CONV_C - 1 - k))[0:SUB]
            c_a = zc(C_A)
            sgl = jax.nn.sigmoid(zc(C_GL))
            put(C_A, dhc * sgl)
            put(C_GL, dhc * c_a * sgl * (1.0 - sgl))

            y, s = gating()
            ycat_ref[rows, _cols(3)] = y.astype(BF16)
            dy = dy_ref[rows,_cols(3)]
            put(D_G, dy * s["u"] * s["mixed"] * _dsilu(s["sg_d"], s["s_d"]))
            put(D_U, dy * s["mixed"] * s["sg_d"] * _dgelu(s["d_u"], s["aux_u"]))
            dmixed = dy * s["u"] * s["sg_d"]
            dbs_ref[...] += dmixed
            by_head = jnp.concatenate(
                [jnp.where((lane >= 64 * h) & (lane < 64 * h + 64), dmixed, 0.0) for h in range(4)], axis=0).astype(BF16)
            dws_ref[...] += lax.dot_general(by_head, s["v_b"], (((1,), (1,)), ((), ())), preferred_element_type=F32)
            rv = jnp.dot(wst_ref[...], dmixed.astype(BF16), preferred_element_type=F32)
            dv = _by_quarter(lane, [rv[h * SUB:(h + 1) * SUB] for h in range(4)])
            add(S_LNGD, dv * s["n_d"])
            add(S_LNBD, dv)
            dgv = _layer_norm_bwd(dv * vec(V_LNGD), s["n_d"], s["rstd_d"])
            put(D_V, dgv * _dgelu(s["d_v"], s["aux_v"]))

        future_ref[tile:tile + HALO, :] = future_ref[0:HALO, :]
        dy_ref[...] = dy_next_ref[...]

        @pl.when(i == n_tiles - 1)
        def _():
            for n in range(N_SUMS):
                sums_ref[n:n + 1, :] = jnp.sum(acc_ref[n], axis=0, keepdims=True)

    return _tiled_call(
        body, (z, z, o_c, dx_next, dx_next, gate, *small, *small_t, w_out_b), name="mix_bwd", grid=(n_tiles,),
        in_specs=[pl.BlockSpec((tile, D_IN), lambda i: (tile_of(i), 0)),
                  pl.BlockSpec((HALO, D_IN), lambda i: (jnp.maximum(tile_of(i) * per_halo - 1, 0), 0)),
                  pl.BlockSpec((tile, GROUP), lambda i: (tile_of(i), 0)),
                  pl.BlockSpec((tile, D_MODEL), lambda i: (tile_of(i), 0)),
                  pl.BlockSpec((tile, D_MODEL), lambda i: (next_tile_of(i), 0)), _full((1, D_MODEL)),
                  *_small_specs(True), _full((D_MODEL, D_MODEL))],
        out_specs=[pl.BlockSpec((tile, D_IN), lambda i: (tile_of(i), 0)),
                   pl.BlockSpec((tile, D_MODEL), lambda i: (tile_of(i), 0)),
                   _full((N_SUMS, GROUP)), _full((GROUP, GROUP)), _full((GROUP, GROUP)), _full((4 * SUB, SUB)),
                   _full((SUB, GROUP))],
        out_shape=[jax.ShapeDtypeStruct((n_tok, D_IN), BF16), jax.ShapeDtypeStruct((n_tok, D_MODEL), BF16),
                   jax.ShapeDtypeStruct((N_SUMS, GROUP), F32), jax.ShapeDtypeStruct((GROUP, GROUP), F32),
                   jax.ShapeDtypeStruct((GROUP, GROUP), F32), jax.ShapeDtypeStruct((4 * SUB, SUB), F32),
                   jax.ShapeDtypeStruct((SUB, GROUP), F32)],
        scratch_shapes=[pltpu.VMEM((tile + HALO, 3 * GROUP), F32), pltpu.VMEM((tile + HALO, 3 * GROUP), F32),
                        pltpu.VMEM((tile, D_MODEL), F32), pltpu.VMEM((tile, D_MODEL), F32),
                        pltpu.VMEM((N_SUMS, 8, GROUP), F32)], ride=ride)


def _norm_bwd(x, dz, dx_next, gs, w_in_b, tile, ride=None, blocks=None, begun=None, finish=True):
    n_tok = x.shape[0]
    first, n_tiles = blocks or (0, n_tok // tile)
    n_in = 5 + (2 if begun else 0)

    def body(*refs):
        x_ref, dz_ref, dxn_ref, gs_ref, w_ref = refs[:5]
        dx_ref = refs[n_in]
        acc_ref = refs[-1]
        i = pl.program_id(0)

        @pl.when(i == 0)
        def _():
            acc_ref[...] = refs[6][...] if begun else jnp.zeros_like(acc_ref)

        dh_ref = refs[-2]
        dh_ref[...] = lax.dot_general(dz_ref[...], w_ref[...], (((1,), (1,)), ((), ())), preferred_element_type=F32)

        def strip(k, carry):
            rows = pl.ds(pl.multiple_of(k * HALO, HALO), HALO)
            dh = dh_ref[rows, :]
            xv = x_ref[rows, :]
            r = lax.rsqrt(jnp.mean(xv * xv, axis=-1, keepdims=True) + EPS)
            xn = xv * r
            acc_ref[0] = acc_ref[0] + _row_sum8(dh)
            acc_ref[1] = acc_ref[1] + _row_sum8(dh * xn)
            dxn = dh * gs_ref[...]
            dx_ref[rows, :] = dxn_ref[rows, :] + r * (dxn - xn * jnp.mean(dxn * xn, axis=-1, keepdims=True))
            return carry

        lax.fori_loop(0, tile // HALO, strip, 0)

        @pl.when(i == n_tiles - 1)
        def _():
            if finish:
                refs[n_in + 1][...] = jnp.sum(acc_ref[0], axis=0, keepdims=True)
                refs[n_in + 2][...] = jnp.sum(acc_ref[1], axis=0, keepdims=True)
            else:
                refs[n_in + 1][...] = acc_ref[...]

    def rows(i):
        return (first + i, 0)

    in_specs = [pl.BlockSpec((tile, D_MODEL), rows), pl.BlockSpec((tile, D_IN), rows), pl.BlockSpec((tile, D_MODEL), rows),
                _full((1, D_MODEL)), _full((D_MODEL, D_IN))]
    args = (x, dz, dx_next, gs, w_in_b)
    if begun:
        in_specs += [pl.BlockSpec(memory_space=pl.ANY), _full((2, 8, D_MODEL))]
        args += tuple(begun)
    vec = jax.ShapeDtypeStruct((1, D_MODEL), F32)
    return _tiled_call(
        body, args, name="norm_bwd", grid=(n_tiles,), in_specs=in_specs,
        out_specs=[pl.BlockSpec((tile, D_MODEL), rows)] + ([_full((1, D_MODEL))] * 2 if finish else [_full((2, 8, D_MODEL))]),
        out_shape=[jax.ShapeDtypeStruct((n_tok, D_MODEL), F32)]
        + ([vec, vec] if finish else [jax.ShapeDtypeStruct((2, 8, D_MODEL), F32)]),
        scratch_shapes=[pltpu.VMEM((tile, D_MODEL), F32), pltpu.VMEM((2, 8, D_MODEL), F32)], ride=ride,
        aliases={5: 0} if begun else None)


def _tokens_matmul(a, b, name, out_dtype=F32, ride=None, a_cols=None, gated=None):
    n_tok = a.shape[0]
    a_block, ka = a_cols or (0, a.shape[1])
    nb = b.shape[1]
    tk = min(REDUCE_TILE * (4 // b.dtype.itemsize), n_tok)
    cb = min(D_MODEL, nb)
    n_steps = n_tok // tk
    n_in = 4 if gated else 2

    def body(*refs):
        a_ref, b_ref, o_ref, acc_ref = refs[0], refs[1], refs[n_in], refs[-1]
        i = pl.program_id(1)

        @pl.when(i == 0)
        def _():
            acc_ref[...] = jnp.zeros_like(acc_ref)

        acc_ref[...] += lax.dot_general(a_ref[...], b_ref[...].astype(BF16), (((0,), (0,)), ((), ())),
                                        preferred_element_type=F32)

        @pl.when(i == n_steps - 1)
        def _():
            m = acc_ref[...]
            if gated:
                w_ref, gate_ref, dgate_ref = refs[2], refs[3], refs[n_in + 1]
                o_ref[...] = (m * gate_ref[...]).astype(out_dtype)
                dgate_ref[...] = jnp.sum(m * w_ref[...].astype(F32), axis=0, keepdims=True)
            else:
                o_ref[...] = m.astype(out_dtype)

    in_specs = [pl.BlockSpec((tk, ka), lambda j, i: (i, a_block)), pl.BlockSpec((tk, cb), lambda j, i: (i, j))]
    out_specs = [pl.BlockSpec((ka, cb), lambda j, i: (0, j))]
    out_shape = [jax.ShapeDtypeStruct((ka, nb), out_dtype)]
    if gated:
        in_specs += [pl.BlockSpec((ka, cb), lambda j, i: (0, j)), pl.BlockSpec((1, cb), lambda j, i: (0, j))]
        out_specs += [pl.BlockSpec((1, cb), lambda j, i: (0, j))]
        out_shape += [jax.ShapeDtypeStruct((1, nb), F32)]
    outs, rode = _tiled_call(body, (a, b, *(gated or ())), name=name, grid=(nb // cb, n_steps), in_specs=in_specs,
                             out_specs=out_specs, out_shape=out_shape, scratch_shapes=[pltpu.VMEM((ka, cb), F32)],
                             ride=ride)
    return (outs if gated else outs[0]), rode


def _modulation_columns(c_all, w_ada, b_cols):
    cols = w_ada.shape[2]

    def body(c_ref, w_ref, b_ref, ca_ref, mod_ref):
        ca, _ = _silu(c_ref[...])
        ca_ref[...] = ca
        for l in range(N_LAYERS):
            mod_ref[l] = jnp.dot(ca, w_ref[l], precision=lax.Precision.HIGHEST, preferred_element_type=F32) + b_ref[l:l + 1, :]

    return pl.pallas_call(
        body, name="modulation_columns",
        out_shape=[jax.ShapeDtypeStruct((N_DEV, D_MODEL), F32), jax.ShapeDtypeStruct((N_LAYERS, N_DEV, cols), F32)],
        compiler_params=pltpu.CompilerParams(vmem_limit_bytes=VMEM_LIMIT),
    )(c_all, w_ada, b_cols)


def _adam(w, g, m, v):
    m2 = ADAM_B1 * m + (1.0 - ADAM_B1) * g
    v2 = ADAM_B2 * v + (1.0 - ADAM_B2) * (g * g)
    m_hat = m2 / (1.0 - ADAM_B1 ** ADAM_STEP)
    v_hat = v2 / (1.0 - ADAM_B2 ** ADAM_STEP)
    return -ADAM_LR * (m_hat / (jnp.sqrt(v_hat) + ADAM_EPS) + ADAM_WD * w), m2, v2


def _row_block(rows, cols, slots):
    target = max(8, (1 << 19) // (cols * max(slots, 1)))
    rb = rows
    while rb > target and rb % 2 == 0 and (rb // 2) % 8 == 0:
        rb //= 2
    return rb


def _adam_update(w, g, m, v, name):
    rows, cols = w.shape
    slotted = g.ndim == 3
    rb = _row_block(rows, cols, N_DEV if slotted else 1)

    def body(w_ref, g_ref, m_ref, v_ref, go_ref, d_ref, mo_ref, vo_ref):
        if slotted:
            gv = g_ref[0].astype(F32)
            for q in range(1, N_DEV):
                gv = gv + g_ref[q].astype(F32)
        else:
            gv = g_ref[...]
        go_ref[...] = gv
        d_ref[...], mo_ref[...], vo_ref[...] = _adam(w_ref[...], gv, m_ref[...], v_ref[...])

    blk = pl.BlockSpec((rb, cols), lambda i: (i, 0))
    g_blk = pl.BlockSpec((N_DEV, rb, cols), lambda i: (0, i, 0)) if slotted else blk
    return pl.pallas_call(
        body, name=name, grid=(rows // rb,),
        in_specs=[blk, g_blk, blk, blk], out_specs=[blk] * 4,
        out_shape=[jax.ShapeDtypeStruct((rows, cols), F32)] * 4,
        compiler_params=_params(("parallel",)),
    )(w, g, m, v)


def _adam_many(ws, gs, ms, vs, name):
    n = len(ws)

    def body(*refs):
        w_refs, g_refs, m_refs, v_refs, d_refs, mo_refs, vo_refs = (refs[k * n:(k + 1) * n] for k in range(7))
        for j in range(n):
            d_refs[j][...], mo_refs[j][...], vo_refs[j][...] = _adam(w_refs[j][...], g_refs[j][...], m_refs[j][...],
                                                                  v_refs[j][...])

    res = pl.pallas_call(
        body, name=name, out_shape=[jax.ShapeDtypeStruct(w.shape, F32) for w in ws] * 3,
        compiler_params=pltpu.CompilerParams(vmem_limit_bytes=VMEM_LIMIT),
    )(*ws, *gs, *ms, *vs)
    return res[:n], res[n:2 * n], res[2 * n:]


def _as_rows(a):
    return a.reshape(-1, a.shape[-1]) if a.ndim > 1 else a.reshape(1, -1)


def _ada_update(ca_t, dmod_cols, w, m, v):
    _, rows, cols = w.shape

    def body(ca_ref, dm_ref, w_ref, m_ref, v_ref, g_ref, d_ref, mo_ref, vo_ref):
        g = ca_ref[:, 0:1] * dm_ref[0, 0:1, :]
        for b in range(1, N_DEV):
            g = g + ca_ref[:, b:b + 1] * dm_ref[0, b:b + 1, :]
        g_ref[0] = g
        d_ref[0], mo_ref[0], vo_ref[0] = _adam(w_ref[0], g, m_ref[0], v_ref[0])

    blk = pl.BlockSpec((1, rows, cols), lambda l: (l, 0, 0))
    return pl.pallas_call(
        body, name="ada_update", grid=(N_LAYERS,),
        in_specs=[_full((rows, N_DEV)), pl.BlockSpec((1, N_DEV, cols), lambda l: (l, 0, 0)), blk, blk, blk],
        out_specs=[blk] * 4, out_shape=[jax.ShapeDtypeStruct(w.shape, F32)] * 4,
        compiler_params=_params(("parallel",)),
    )(ca_t, dmod_cols, w, m, v)


def _exchange_sems(n):
    return [pltpu.SemaphoreType.DMA((n, N_DEV - 1)), pltpu.SemaphoreType.DMA((n, N_DEV - 1)),
            pltpu.SemaphoreType.DMA((n,))]


def _exchange_copies(plans, srcs, outs, sems, receiving, only=None):
    send_sems, recv_sems, local_sems = sems
    x, y, c = lax.axis_index("x"), lax.axis_index("y"), lax.axis_index("c")
    me = 4 * x + 2 * y + c

    def remote(i, k, incoming):
        _, o, send, land = plans[i]
        px = 1 - x if k & 4 else x
        py = 1 - y if k & 2 else y
        pc = 1 - c if k & 1 else c
        p = 4 * px + 2 * py + pc
        return pltpu.make_async_remote_copy(
            src_ref=send(srcs[i], p), dst_ref=land(outs[o], p if incoming else me),
            send_sem=send_sems.at[i, k - 1], recv_sem=recv_sems.at[i, k - 1],
            device_id=(px, py, pc), device_id_type=pl.DeviceIdType.MESH)

    which = range(len(plans)) if only is None else only
    pairs = [(i, k) for k in range(1, N_DEV) for i in which]
    local = [pltpu.make_async_copy(plans[i][2](srcs[i], me), plans[i][3](outs[plans[i][1]], me), local_sems.at[i])
             for i in which]
    return local, [remote(i, k, False) for i, k in pairs], [remote(i, k, True) for i, k in pairs] if receiving else []


def _exchange_start(plans, srcs, outs, sems, only=None):
    local, outgoing, _ = _exchange_copies(plans, srcs, outs, sems, False, only)
    for cp in local + outgoing:
        cp.start()


def _exchange_wait(plans, srcs, outs, sems, only=None):
    local, outgoing, incoming = _exchange_copies(plans, srcs, outs, sems, True, only)
    for cp in incoming:
        cp.wait_recv()
    for cp in outgoing:
        cp.wait_send()
    for cp in local:
        cp.wait()


def _exchange(name, ride):
    out_shapes, plans = ride
    n = len(plans)
    hbm = pl.BlockSpec(memory_space=pltpu.HBM)

    def body(*refs):
        srcs, outs, sems = refs[:n], refs[n:n + len(out_shapes)], refs[n + len(out_shapes):]
        _exchange_start(plans, srcs, outs, sems)
        _exchange_wait(plans, srcs, outs, sems)

    return pl.pallas_call(
        body, name=name, in_specs=[hbm] * n, out_specs=[hbm] * len(out_shapes), out_shape=list(out_shapes),
        scratch_shapes=_exchange_sems(n),
    )(*[p[0] for p in plans])


def _first_gather(c, arrays, rules):
    n = len(arrays)
    c_shapes, c_plans = _plans([c], [_gather])
    shapes, lands = zip(*[(shape, land) for shape, _, land in (rule(a) for a, rule in zip(arrays, rules))])
    hbm = pl.BlockSpec(memory_space=pltpu.HBM)

    def body(*refs):
        c_ref, srcs, c_all_ref, outs = refs[0], refs[1:1 + n], refs[1 + n], refs[2 + n:2 + 2 * n]
        send_sems, recv_sems, local_sems = refs[2 + 2 * n:5 + 2 * n]
        c_sems = refs[5 + 2 * n:]
        x, y, core = lax.axis_index("x"), lax.axis_index("y"), lax.axis_index("c")
        me, sibling = (x, y, core), (x, y, 1 - core)
        chips = [(1 - x, y), (x, 1 - y), (1 - x, 1 - y)]

        def block(a, px, py, pc):
            return lands[a](outs[a], 4 * px + 2 * py + pc)

        def copy(a, k, origin, to, own=False):
            return pltpu.make_async_remote_copy(
                src_ref=srcs[a] if own else block(a, *origin), dst_ref=block(a, *origin),
                send_sem=send_sems.at[a, k], recv_sem=recv_sems.at[a, k], device_id=to,
                device_id_type=pl.DeviceIdType.MESH)

        _exchange_start(c_plans, [c_ref], [c_all_ref], c_sems)
        mine = [pltpu.make_async_copy(srcs[a], block(a, *me), local_sems.at[a]) for a in range(n)]
        first = [copy(a, 0, me, sibling, own=True) for a in range(n)]
        first += [copy(a, 1 + j, me, (*chip, core), own=True) for j, chip in enumerate(chips) for a in range(n)]
        for cp in mine + first:
            cp.start()
        passed = [[copy(a, 4 + j, (*chip, core), sibling) for a in range(n)] for j, chip in enumerate(chips)]
        for j, chip in enumerate(chips):
            for a in range(n):
                copy(a, 1 + j, (*chip, core), me).wait_recv()
                passed[j][a].start()
        for a in range(n):
            copy(a, 0, sibling, me).wait_recv()
        for j, chip in enumerate(chips):
            for a in range(n):
                copy(a, 4 + j, (*chip, 1 - core), me).wait_recv()
        for cp in first + [cp for row in passed for cp in row]:
            cp.wait_send()
        for cp in mine:
            cp.wait()
        _exchange_wait(c_plans, [c_ref], [c_all_ref], c_sems)

    return pl.pallas_call(
        body, name="first_gather", in_specs=[hbm] * (1 + n), out_specs=[hbm] * (1 + n),
        out_shape=[c_shapes[0], *shapes],
        scratch_shapes=[pltpu.SemaphoreType.DMA((n, N_DEV - 1)), pltpu.SemaphoreType.DMA((n, N_DEV - 1)),
                        pltpu.SemaphoreType.DMA((n,)), *_exchange_sems(1)],
    )(c, *arrays)


def _finish_exchange(big, big_rules, packed, dmod):
    n_rows = packed.shape[0]
    r = n_rows // N_DEV
    shapes, plans = _plans([*big, packed, dmod], [*big_rules, _scatter_rows, _gather])
    n_first = len(plans)
    i_small = n_first - 2
    _, send, land = _gather_rows(jax.ShapeDtypeStruct((r, 128), F32))
    plans = plans + [(None, len(shapes), send, land)]
    shapes = shapes + [jax.ShapeDtypeStruct((n_rows, 128), F32)]
    first = [i for i in range(n_first) if i != i_small]
    hbm = pl.BlockSpec(memory_space=pltpu.HBM)

    def body(*refs):
        srcs, outs = list(refs[:n_first]), refs[n_first:n_first + len(shapes)]
        parts_ref, sum_ref, local_sem = refs[n_first + len(shapes):n_first + len(shapes) + 3]
        sems = refs[n_first + len(shapes) + 3:]
        srcs.append(sum_ref)
        _exchange_start(plans, srcs, outs, sems, only=range(n_first))
        _exchange_wait(plans, srcs, outs, sems, only=[i_small])
        cp = pltpu.make_async_copy(outs[i_small], parts_ref, local_sem)
        cp.start()
        cp.wait()
        g = parts_ref[0]
        for q in range(1, N_DEV):
            g = g + parts_ref[q]
        sum_ref[...] = g
        _exchange_start(plans, srcs, outs, sems, only=[n_first])
        _exchange_wait(plans, srcs, outs, sems, only=[n_first])
        _exchange_wait(plans, srcs, outs, sems, only=first)

    res = pl.pallas_call(
        body, name="finish_exchange", in_specs=[hbm] * n_first, out_specs=[hbm] * len(shapes), out_shape=shapes,
        scratch_shapes=[pltpu.VMEM((N_DEV, r, 128), F32), pltpu.VMEM((r, 128), F32), pltpu.SemaphoreType.DMA(()),
                        *_exchange_sems(len(plans))],
    )(*big, packed, dmod)
    return (*res[:len(big)], res[-1], res[n_first - 1])


def _tiled_call(body, args, *, name, grid, in_specs, out_specs, out_shape, scratch_shapes=(), ride=None, aliases=None):
    params = _params(("arbitrary",) * len(grid))
    if ride is None:
        return pl.pallas_call(body, name=name, grid=grid, in_specs=in_specs, out_specs=out_specs, out_shape=out_shape,
                              scratch_shapes=list(scratch_shapes), input_output_aliases=aliases or {},
                              compiler_params=params)(*args), []
    shapes, plans = ride
    n_in, n_src, n_out, n_dst, n_scr = len(in_specs), len(plans), len(out_specs), len(shapes), len(scratch_shapes)
    hbm = pl.BlockSpec(memory_space=pltpu.HBM)

    def carrying(*refs):
        ins, srcs, refs = refs[:n_in], refs[n_in:n_in + n_src], refs[n_in + n_src:]
        outs, dsts, refs = refs[:n_out], refs[n_out:n_out + n_dst], refs[n_out + n_dst:]
        scratch, sems = refs[:n_scr], refs[n_scr:]
        ids = [pl.program_id(a) for a in range(len(grid))]
        first = functools.reduce(jnp.logical_and, [i == 0 for i in ids])
        last = functools.reduce(jnp.logical_and, [i == g - 1 for i, g in zip(ids, grid)])

        @pl.when(first)
        def _():
            _exchange_start(plans, srcs, dsts, sems)

        body(*ins, *outs, *scratch)

        @pl.when(last)
        def _():
            _exchange_wait(plans, srcs, dsts, sems)

    res = pl.pallas_call(
        carrying, name=name, grid=grid, in_specs=[*in_specs, *[hbm] * n_src], out_specs=[*out_specs, *[hbm] * n_dst],
        out_shape=[*out_shape, *shapes], scratch_shapes=[*scratch_shapes, *_exchange_sems(n_src)],
        input_output_aliases=aliases or {}, compiler_params=params)(*args, *[p[0] for p in plans])
    return res[:n_out], res[n_out:]


def _tail(nd, idx):
    return (slice(None),) * (nd - 2) + idx


def _gather(a):
    return jax.ShapeDtypeStruct((N_DEV,) + a.shape, a.dtype), lambda s, p: s, lambda o, q: o.at[q]


def _gather_rows(a):
    r = a.shape[-2]
    return (jax.ShapeDtypeStruct(a.shape[:-2] + (N_DEV * r, a.shape[-1]), a.dtype), lambda s, p: s,
            lambda o, q: o.at[_tail(a.ndim, (pl.ds(pl.multiple_of(q * r, r), r), slice(None)))])


def _gather_cols(a):
    c = a.shape[-1]
    return (jax.ShapeDtypeStruct(a.shape[:-1] + (N_DEV * c,), a.dtype), lambda s, p: s,
            lambda o, q: o.at[_tail(a.ndim, (slice(None), pl.ds(pl.multiple_of(q * c, c), c)))])


def _scatter_rows(a):
    r = a.shape[0] // N_DEV
    return (jax.ShapeDtypeStruct((N_DEV, r, a.shape[1]), a.dtype),
            lambda s, p: s.at[pl.ds(pl.multiple_of(p * r, r), r), :], lambda o, q: o.at[q])


def _scatter_cols(a):
    c = a.shape[1] // N_DEV
    return (jax.ShapeDtypeStruct((N_DEV, a.shape[0], c), a.dtype),
            lambda s, p: s.at[:, pl.ds(pl.multiple_of(p * c, c), c)], lambda o, q: o.at[q])


def _plans(arrays, rules):
    shapes, plans = [], []
    for o, (a, rule) in enumerate(zip(arrays, rules)):
        shape, send, land = rule(a)
        shapes.append(shape)
        plans.append((a, o, send, land))
    return shapes, plans


def _pack(pieces, rows_multiple=8):
    flat = []
    for a in pieces:
        f = a.reshape(-1)
        flat.append(jnp.pad(f, (0, (-f.shape[0]) % 128)))
    total = sum(f.shape[0] for f in flat)
    flat.append(jnp.zeros(((-total) % (128 * rows_multiple),), F32))
    return jnp.concatenate(flat).reshape(-1, 128)


def _unpack(buf, shapes, lead=()):
    flat = buf.reshape(lead + (-1,))
    out, off = [], 0
    for s in shapes:
        n = math.prod(s)
        out.append(flat[..., off:off + n].reshape(lead + tuple(s)))
        off += n + (-n) % 128
    return out


def _pad_rows(a, rows):
    return jnp.pad(a, ((0, rows - a.shape[0]), (0, 0)))


VEC_NAMES = ('pool_scale', 'b_dw_c', 'ln_g_c', 'ln_b_c', 'b_pw2_c', 'ln_g_d', 'ln_b_d')
GATHERED = ('w_in', 'w_out', 'w_pw2_c', 'w_conv_a', 'w_dw_c')
GATHER_RULES = (_gather_cols, _gather_rows, _gather_rows, _gather, _gather)
SCATTER_RULES = (_scatter_cols, _scatter_rows, _scatter_rows)


def _weight_shards(shard, l):
    return [shard[n][l].astype(BF16) if n in ('w_in', 'w_out') else shard[n][l] for n in GATHERED]


def _layer_weights(shard, l, gathered):
    w_in_b, w_out_b, w_pw2, wconv_parts, wdw_parts = gathered
    wconv = wconv_parts.transpose(1, 0, 2).reshape(CONV_A, GROUP)
    wdw = wdw_parts.transpose(1, 0, 2).reshape(CONV_C, GROUP)
    wp = jnp.einsum('gcd,gh->gchd', shard['w_pool'][l], jnp.eye(4, dtype=F32)).reshape(GROUP, GROUP)
    ws = shard['w_s_d'][l] * jnp.tril(jnp.ones((SUB, SUB), F32))
    vec = jnp.stack([shard[n][l] for n in VEC_NAMES])
    width = jnp.repeat(jnp.asarray([2.0, 4.0, 8.0, 16.0], F32), 64)[None]
    count = jnp.minimum(jnp.arange(1, SUB + 1, dtype=F32)[:, None], width)
    gating_and_counts = jnp.concatenate([jnp.repeat(shard['b_s_d'][l].T, 64, axis=1), 1.0 / count, 1.0 / width,
                                         jnp.zeros((7, GROUP), F32)])
    small = (_pad_rows(wconv, 8), _pad_rows(wdw, HALO), _pad_rows(vec, 16), wp.astype(BF16), w_pw2.astype(BF16),
             ws.reshape(4 * SUB, SUB).astype(BF16), gating_and_counts)
    small_t = (wp.T.astype(BF16), w_pw2.T.astype(BF16), ws.transpose(0, 2, 1).reshape(4 * SUB, SUB).astype(BF16))
    return w_in_b, w_out_b, small, small_t


def kernel(x, c, norm_g, w_ada, b_ada, w_in, w_conv_a, w_pool, pool_scale, w_dw_c, b_dw_c, ln_g_c, ln_b_c, w_pw2_c, b_pw2_c, ln_g_d, ln_b_d, w_s_d, b_s_d, w_out, final_g, loss_target, m_norm_g, m_w_ada, m_b_ada, m_w_in, m_w_conv_a, m_w_pool, m_pool_scale, m_w_dw_c, m_b_dw_c, m_ln_g_c, m_ln_b_c, m_w_pw2_c, m_b_pw2_c, m_ln_g_d, m_ln_b_d, m_w_s_d, m_b_s_d, m_w_out, m_final_g, v_norm_g, v_w_ada, v_b_ada, v_w_in, v_w_conv_a, v_w_pool, v_pool_scale, v_w_dw_c, v_b_dw_c, v_ln_g_c, v_ln_b_c, v_w_pw2_c, v_b_pw2_c, v_ln_g_d, v_ln_b_d, v_w_s_d, v_b_s_d, v_w_out, v_final_g):
    given = dict(locals())
    shard = {n: given[n] for n in WEIGHTS}
    mom_m = {n: given['m_' + n] for n in WEIGHTS}
    mom_v = {n: given['v_' + n] for n in WEIGHTS}
    me = 4 * lax.axis_index("x") + 2 * lax.axis_index("y") + lax.axis_index("c")
    n_tok = x.shape[1]
    tile = min(TOKEN_TILE, n_tok)
    wide_tile = min(2 * TOKEN_TILE, n_tok)
    x0 = x.reshape(n_tok, D_MODEL)
    target = loss_target.reshape(n_tok, D_MODEL)
    ada_cols = w_ada.shape[2]

    first_shards = _weight_shards(shard, 0)
    c_all, w_in_first = _first_gather(c, first_shards[:1], GATHER_RULES[:1])

    b_cols = lax.dynamic_slice_in_dim(b_ada, me * ada_cols, ada_cols, axis=1)
    c_act, mod_cols = _modulation_columns(c_all.reshape(N_DEV, D_MODEL), w_ada, b_cols)
    (mod_all,) = _exchange("gather_modulation", _plans([mod_cols], [_gather]))
    mod = lax.dynamic_index_in_dim(mod_all, me, axis=2, keepdims=False)
    mod = mod.transpose(1, 0, 2).reshape(N_LAYERS, 3 * D_MODEL)
    shift, scale, gate = (mod[:, k * D_MODEL:(k + 1) * D_MODEL].reshape(N_LAYERS, 1, D_MODEL) for k in range(3))
    gs = norm_g.reshape(N_LAYERS, 1, D_MODEL) * (1.0 + scale)

    xs, hs, zs, ocs, layers = [x0], [], [], [], []
    for l in range(N_LAYERS):
        if l == 0:
            (h, z), rest = _in_proj(xs[0], gs[0], shift[0], w_in_first, wide_tile,
                                    ride=_plans(first_shards[1:], GATHER_RULES[1:]))
            layers.append(_layer_weights(shard, 0, [w_in_first, *rest]))
        else:
            (h, z), _ = _in_proj(xs[l], gs[l], shift[l], layers[l][0], wide_tile)
        _, w_out_b, small, _ = layers[l]
        hs.append(h)
        zs.append(z)
        if l + 1 < N_LAYERS:
            (x_next, o_c), gathered = _mix_out(z, xs[l], gate[l], small, w_out_b, tile,
                                               ride=_plans(_weight_shards(shard, l + 1), GATHER_RULES))
            xs.append(x_next)
            layers.append(_layer_weights(shard, l + 1, gathered))
        else:
            (dx, o_c, loss_part, dfinal_g), _ = _mix_out(z, xs[l], gate[l], small, w_out_b, tile,
                                                         head=(final_g.reshape(1, D_MODEL), target))
        ocs.append(o_c)

    part = {}
    layer_parts = [None] * N_LAYERS
    slots = [None] * N_LAYERS
    for l in reversed(range(N_LAYERS)):
        w_in_b, w_out_b, small, small_t = layers[l]
        ride = _plans(layer_parts[l + 1]['big'], SCATTER_RULES) if l + 1 < N_LAYERS else None
        (dz, ycat, sums, dwp, dw2, dws, dbs), rode = _mix_bwd(zs[l], ocs[l], dx, gate[l], small, small_t, w_out_b, tile,
                                                              ride=ride)
        if ride:
            slots[l + 1] = rode
        if l > 0:
            (dw_out, dgate), _ = _tokens_matmul(ycat, dx, "out_proj_tokens_matmul", out_dtype=BF16,
                                                gated=(w_out_b, gate[l]))
            dw_in, _ = _tokens_matmul(hs[l], dz, "in_proj_tokens_matmul", out_dtype=BF16)
        else:
            dw_in, _ = _tokens_matmul(hs[l], dz, "in_proj_tokens_matmul", out_dtype=BF16, a_cols=(0, D_MODEL // 2))
            dw_in_last, (slots_in,) = _tokens_matmul(
                hs[l], dz, "in_proj_tokens_matmul", out_dtype=BF16, a_cols=(1, D_MODEL // 2),
                ride=_plans([dw_in], SCATTER_RULES[:1]))
            (dw_out, dgate), (slots_in_last,) = _tokens_matmul(
                ycat, dx, "out_proj_tokens_matmul", out_dtype=BF16, gated=(w_out_b, gate[l]),
                ride=_plans([dw_in_last], SCATTER_RULES[:1]))
        (dx, dshift, dgs), _ = _norm_bwd(xs[l], dz, dx, gs[l], w_in_b, tile)
        layer_parts[l] = dict(
            big=[dw_in, dw_out, dw2],
            b_ada=jnp.concatenate([dshift, dgs * norm_g[l][None], dgate], axis=1)[0],
            norm_g=(dgs * (1.0 + scale[l]))[0], sums=sums,
            w_pool=jnp.einsum('gchd,gh->gcd', dwp.reshape(4, 64, 4, 64), jnp.eye(4, dtype=F32)),
            w_s_d=dws.reshape(4, SUB, SUB) * jnp.tril(jnp.ones((SUB, SUB), F32)),
            b_s_d=dbs.reshape(SUB, 4, 64).sum(axis=-1).T)
    grad_x = dx.reshape(x.shape)
    small_names = REPLICATED + CHANNEL_SHARDED
    packed_names = [n for n in small_names if n not in SUM_ROWS] + ['sums']
    for n in packed_names:
        part[n] = dfinal_g[0] if n == 'final_g' else jnp.stack([layer_parts[l][n] for l in range(N_LAYERS)])

    small_shapes = [part[n].shape for n in packed_names] + [(1, 128)]
    slots_out, slots_pw2, small_sum, dmod_all = _finish_exchange(
        layer_parts[0]['big'][1:], SCATTER_RULES[1:],
        _pack([part[n] for n in packed_names] + [loss_part[0:1]], rows_multiple=8 * N_DEV), part['b_ada'])
    slots[0] = [slots_in, slots_out, slots_pw2]

    grads, deltas, new_m, new_v = {}, {}, {}, {}
    half = D_MODEL // 2
    for j, n in enumerate(('w_in', 'w_out', 'w_pw2_c')):
        outs = [_adam_update(shard[n][l], slots[l][j], mom_m[n][l], mom_v[n][l], "update_" + n)
                for l in range(1, N_LAYERS)]
        if n == 'w_in':
            halves = [_adam_update(shard[n][0][rows], s, mom_m[n][0][rows], mom_v[n][0][rows], "update_" + n)
                      for rows, s in ((slice(0, half), slots[0][0]), (slice(half, None), slots_in_last))]
            outs.insert(0, [jnp.concatenate(o) for o in zip(*halves)])
        else:
            outs.insert(0, _adam_update(shard[n][0], slots[0][j], mom_m[n][0], mom_v[n][0], "update_" + n))
        grads[n], deltas[n], new_m[n], new_v[n] = (jnp.stack(o) for o in zip(*outs))

    *small_sums, loss_sum = _unpack(small_sum, small_shapes)
    loss = loss_sum[0, 0]
    gsum = dict(zip(packed_names, small_sums))
    for n, rows in SUM_ROWS.items():
        gsum[n] = gsum['sums'][:, rows]
    for n in CHANNEL_SHARDED:
        width = shard[n].shape[2]
        gsum[n] = lax.dynamic_slice_in_dim(gsum[n], me * width, width, axis=2)
    d_small, m_small, v_small = _adam_many(*[[_as_rows(d[n]) for n in small_names] for d in (shard, gsum, mom_m, mom_v)],
                                           "update_small")
    for j, n in enumerate(small_names):
        grads[n] = gsum[n]
        deltas[n], new_m[n], new_v[n] = (o[j].reshape(shard[n].shape) for o in (d_small, m_small, v_small))

    dmod_cols = lax.dynamic_slice_in_dim(dmod_all, me * ada_cols, ada_cols, axis=2).transpose(1, 0, 2)
    grads['w_ada'], deltas['w_ada'], new_m['w_ada'], new_v['w_ada'] = _ada_update(
        c_act.T, dmod_cols, w_ada, m_w_ada, v_w_ada)

    return (loss, grad_x, *[grads[n] for n in WEIGHTS], *[deltas[n] for n in WEIGHTS],
            *[new_m[n] for n in WEIGHTS], *[new_v[n] for n in WEIGHTS])
```
